```python
import jax, jax.numpy as jnp
from jax import lax
import numpy as np

D_MODEL = 1024
BATCH = 8
SEQ = 2048
DEPTH = 2
DEC_BATCH = 32
DEC_SEQ = 16
PAST_LEN = 1024

CHUNK = 64
HEAD_DIM = 64
N_HEADS = D_MODEL // HEAD_DIM
H_B = (5 * N_HEADS) // 16
H_C = (5 * N_HEADS) // 16
H_A = N_HEADS - H_B - H_C
D_MIX = N_HEADS * HEAD_DIM
D_IN = 3 * D_MIX + H_A
BAND_CHUNKS = 8
BAND = BAND_CHUNKS * CHUNK
REL_CLIP = 128
N_REL = 2 * REL_CLIP + 1
Q_BLOCK = 128
N_GROUPS = 4
EXPERTS_PER_GROUP = 4
N_EXPERTS = N_GROUPS * EXPERTS_PER_GROUP
TOP_K = 2
D_EXPERT = D_MODEL // 4
D_PLE = 256
FORGET_BIAS = 2.0
EPS = 1e-6
ATTN_SCALE = HEAD_DIM ** -0.5
NEG_INF = -1e30

kernel_name = 'streaming_hybrid_fox_band_stickbreak_hmoe'


def rms_norm(x, g=None):
    xf = x.astype(jnp.float32)
    y = xf * lax.rsqrt(jnp.mean(xf * xf, axis=-1, keepdims=True) + EPS)
    if g is not None:
        y = y * g.astype(jnp.float32)
    return y.astype(x.dtype)


def project_heads(xn, w_in, b_f, g_qa, g_ka, g_qb, g_kb):
    b, t, _ = xn.shape
    z = xn @ w_in
    q, k, v = (z[..., j * D_MIX:(j + 1) * D_MIX].reshape(b, t, N_HEADS, HEAD_DIM) for j in range(3))
    logf = jax.nn.log_sigmoid(z[..., 3 * D_MIX:].astype(jnp.float32) + b_f.astype(jnp.float32))
    sa, sb, sc = slice(0, H_A), slice(H_A, H_A + H_B), slice(H_A + H_B, N_HEADS)
    return (rms_norm(q[:, :, sa], g_qa), rms_norm(k[:, :, sa], g_ka), v[:, :, sa], logf,
            rms_norm(q[:, :, sb], g_qb), rms_norm(k[:, :, sb], g_kb), v[:, :, sb],
            q[:, :, sc], k[:, :, sc], v[:, :, sc])


def sweep_query_blocks(fn, q_pos, *qs):
    s = q_pos.shape[0]
    nb = s // Q_BLOCK
    blocks = tuple(jnp.moveaxis(a.reshape((a.shape[0], nb, Q_BLOCK) + a.shape[2:]), 1, 0) for a in qs)
    out = lax.map(lambda xs: fn(xs[0], *xs[1]), (q_pos.reshape(nb, Q_BLOCK), blocks))
    out = jnp.moveaxis(out, 0, 1)
    return out.reshape((out.shape[0], s) + out.shape[3:])


def forgetting_attention(q, k, v, cq, ck, q_pos, k_pos):
    s = jnp.einsum('bqhd,bkhd->bhqk', q, k).astype(jnp.float32) * ATTN_SCALE
    s = s + jnp.swapaxes(cq, 1, 2)[..., :, None] - jnp.swapaxes(ck, 1, 2)[..., None, :]
    s = jnp.where(k_pos[None, :] <= q_pos[:, None], s, NEG_INF)
    p = jax.nn.softmax(s, axis=-1)
    return jnp.einsum('bhqk,bkhd->bqhd', p.astype(v.dtype), v)


def stick_breaking_attention(q, k, v, q_pos, k_pos):
    z = jnp.einsum('bqhd,bkhd->bhqk', q, k).astype(jnp.float32) * ATTN_SCALE
    seen = k_pos[None, :] < q_pos[:, None]
    log_1m_beta = jnp.where(seen, jax.nn.log_sigmoid(-z), 0.0)
    tail = lax.cumsum(log_1m_beta, axis=3, reverse=True) - log_1m_beta
    a = jnp.where(seen, jnp.exp(jax.nn.log_sigmoid(z) + tail), 0.0)
    return jnp.einsum('bhqk,bkhd->bqhd', a.astype(v.dtype), v)


def rel_bias_lookup(rel_bias, rel):
    return rel_bias[:, jnp.clip(rel, -REL_CLIP, REL_CLIP) + REL_CLIP].astype(jnp.float32)


def band_attention_prompt(q, k, v, rel_bias):
    b, s, h, d = q.shape
    nc = s // CHUNK
    width = (BAND_CHUNKS + 1) * CHUNK
    pad = ((0, 0), (BAND, 0), (0, 0), (0, 0))
    band_idx = jnp.arange(nc)[:, None] + jnp.arange(BAND_CHUNKS + 1)[None, :]
    kb = jnp.pad(k, pad).reshape(b, nc + BAND_CHUNKS, CHUNK, h, d)[:, band_idx].reshape(b, nc, width, h, d)
    vb = jnp.pad(v, pad).reshape(b, nc + BAND_CHUNKS, CHUNK, h, d)[:, band_idx].reshape(b, nc, width, h, d)
    qb = q.reshape(b, nc, CHUNK, h, d)
    k_off = jnp.arange(width) - BAND
    bias = rel_bias_lookup(rel_bias, jnp.arange(CHUNK)[:, None] - k_off[None, :])
    valid = (jnp.arange(nc)[:, None] * CHUNK + k_off[None, :]) >= 0
    sc = jnp.einsum('bcqhd,bckhd->bchqk', qb, kb).astype(jnp.float32) * ATTN_SCALE + bias
    sc = jnp.where(valid[None, :, None, None, :], sc, NEG_INF)
    p = jax.nn.softmax(sc, axis=-1)
    return jnp.einsum('bchqk,bckhd->bcqhd', p.astype(v.dtype), vb).reshape(b, s, h, d)


def band_attention_sample(q, k, v, q_pos, k_pos, rel_bias):
    s = jnp.einsum('bqhd,bkhd->bhqk', q, k).astype(jnp.float32) * ATTN_SCALE
    s = s + rel_bias_lookup(rel_bias, q_pos[:, None] - k_pos[None, :])
    p = jax.nn.softmax(s, axis=-1)
    return jnp.einsum('bhqk,bkhd->bqhd', p.astype(v.dtype), v)


def mix_prompt(heads, rel_bias):
    qa, ka, va, logf, qb, kb, vb, qc, kc, vc = heads
    s = qa.shape[1]
    pos = jnp.arange(s)
    c = jnp.cumsum(logf, axis=1)
    oa = sweep_query_blocks(lambda qp, qblk, cblk: forgetting_attention(qblk, ka, va, cblk, c, qp, pos), pos, qa, c)
    ob = band_attention_prompt(qb, kb, vb, rel_bias)
    oc = sweep_query_blocks(lambda qp, qblk: stick_breaking_attention(qblk, kc, vc, qp, pos), pos, qc)
    keep = max(s - BAND, 0)
    return (oa, ob, oc), (ka, va, logf, kb[:, keep:], vb[:, keep:], kc, vc)


def mix_sample(heads, cache_a_k, cache_a_v, cache_a_logf, cache_b_k, cache_b_v, cache_c_k, cache_c_v, rel_bias):
    qa, ka, va, logf, qb, kb, vb, qc, kc, vc = heads
    t = qa.shape[1]
    past = cache_a_k.shape[1]
    rows_b = cache_b_k.shape[1]
    q_pos = past + jnp.arange(t)
    k_pos = jnp.arange(past + t)
    c = jnp.cumsum(jnp.concatenate([cache_a_logf.astype(jnp.float32), logf], axis=1), axis=1)
    oa = forgetting_attention(qa, jnp.concatenate([cache_a_k, ka], axis=1), jnp.concatenate([cache_a_v, va], axis=1),
                              c[:, past:], c, q_pos, k_pos)
    kb_pos = jnp.concatenate([past - rows_b + jnp.arange(rows_b), q_pos])
    ob = band_attention_sample(qb, jnp.concatenate([cache_b_k, kb], axis=1), jnp.concatenate([cache_b_v, vb], axis=1),
                               q_pos, kb_pos, rel_bias)
    oc = stick_breaking_attention(qc, jnp.concatenate([cache_c_k, kc], axis=1), jnp.concatenate([cache_c_v, vc], axis=1),
                                  q_pos, k_pos)
    return (oa, ob, oc), (ka, va, logf, kb, vb, kc, vc)


def hier_moe(x, w_rg, b_rg, w_re, b_re, w_eg, w_eu, w_ed):
    xf = x.astype(jnp.float32)
    p_group = jax.nn.softmax(xf @ w_rg.astype(jnp.float32) + b_rg.astype(jnp.float32), axis=-1)
    g_sel = jnp.argmax(p_group, axis=-1)
    p_g = jnp.max(p_group, axis=-1, keepdims=True)
    e_logits = (xf @ w_re.astype(jnp.float32) + b_re.astype(jnp.float32)).reshape(-1, N_GROUPS, EXPERTS_PER_GROUP)
    e_logits = jnp.einsum('tge,tg->te', e_logits, jax.nn.one_hot(g_sel, N_GROUPS, dtype=jnp.float32))
    top_p, top_i = lax.top_k(jax.nn.softmax(e_logits, axis=-1), TOP_K)
    w = p_g * top_p / jnp.sum(top_p, axis=-1, keepdims=True)
    e_idx = g_sel[:, None] * EXPERTS_PER_GROUP + top_i
    gates = jnp.einsum('tk,tke->te', w, jax.nn.one_hot(e_idx, N_EXPERTS, dtype=jnp.float32))
    act = jax.nn.silu(jnp.einsum('td,edf->tef', x, w_eg)) * jnp.einsum('td,edf->tef', x, w_eu)
    act = act * gates[..., None].astype(act.dtype)
    return jnp.einsum('tef,efd->td', act, w_ed)


def finish_layer(h, outs, p_i, g_mix_out, w_out, norm_ffn, w_rg, b_rg, w_re, b_re,
                 w_eg, w_eu, w_ed, w_ple, norm_ple, w_ple_gate):
    b, t, _ = h.shape
    merged = jnp.concatenate([rms_norm(og.reshape(b, t, -1)) for og in outs], axis=-1)
    merged = (merged * g_mix_out).astype(h.dtype)
    h = h + merged @ w_out
    y = hier_moe(rms_norm(h, norm_ffn).reshape(b * t, D_MODEL), w_rg, b_rg, w_re, b_re, w_eg, w_eu, w_ed)
    h = h + y.reshape(b, t, D_MODEL).astype(h.dtype)
    gate = jax.nn.sigmoid((rms_norm(h, norm_ple) @ w_ple_gate).astype(jnp.float32))
    return h + ((p_i @ w_ple).astype(jnp.float32) * gate).astype(h.dtype)


def setup_inputs(seed: int = 0) -> dict:
    key = jax.random.key(seed)
    keys = iter(jax.random.split(key, 40))

    def normal(shape, scale=1.0, shift=0.0):
        return shift + scale * jax.random.normal(next(keys), shape, jnp.float32)

    rows_b = min(BAND, PAST_LEN)
    return {
        'x_prompt': normal((BATCH, SEQ, D_MODEL)),
        'x_sample': normal((DEC_BATCH, DEC_SEQ, D_MODEL)),
        'p_prompt': normal((DEPTH, BATCH, SEQ, D_PLE)),
        'p_sample': normal((DEPTH, DEC_BATCH, DEC_SEQ, D_PLE)),
        'cache_a_k': normal((DEPTH, DEC_BATCH, PAST_LEN, H_A, HEAD_DIM)),
        'cache_a_v': normal((DEPTH, DEC_BATCH, PAST_LEN, H_A, HEAD_DIM)),
        'cache_a_logf': jax.nn.log_sigmoid(normal((DEPTH, DEC_BATCH, PAST_LEN, H_A), 1.0, FORGET_BIAS)),
        'cache_b_k': normal((DEPTH, DEC_BATCH, rows_b, H_B, HEAD_DIM)),
        'cache_b_v': normal((DEPTH, DEC_BATCH, rows_b, H_B, HEAD_DIM)),
        'cache_c_k': normal((DEPTH, DEC_BATCH, PAST_LEN, H_C, HEAD_DIM)),
        'cache_c_v': normal((DEPTH, DEC_BATCH, PAST_LEN, H_C, HEAD_DIM)),
        'norm_mix': normal((DEPTH, D_MODEL), 0.1, 1.0),
        'w_in': normal((DEPTH, D_MODEL, D_IN), D_MODEL ** -0.5),
        'b_f': normal((DEPTH, H_A), 0.1, FORGET_BIAS),
        'g_qa': normal((DEPTH, HEAD_DIM), 0.1, 1.0),
        'g_ka': normal((DEPTH, HEAD_DIM), 0.1, 1.0),
        'g_qb': normal((DEPTH, HEAD_DIM), 0.1, 1.0),
        'g_kb': normal((DEPTH, HEAD_DIM), 0.1, 1.0),
        'rel_bias': normal((DEPTH, H_B, N_REL), 0.2),
        'g_mix_out': normal((DEPTH, D_MIX), 0.1, 1.0),
        'w_out': normal((DEPTH, D_MIX, D_MODEL), D_MIX ** -0.5),
        'norm_ffn': normal((DEPTH, D_MODEL), 0.1, 1.0),
        'w_router_group': normal((DEPTH, D_MODEL, N_GROUPS), D_MODEL ** -0.5),
        'b_router_group': normal((DEPTH, N_GROUPS), 0.01),
        'w_router_expert': normal((DEPTH, D_MODEL, N_EXPERTS), D_MODEL ** -0.5),
        'b_router_expert': normal((DEPTH, N_EXPERTS), 0.01),
        'w_exp_gate': normal((DEPTH, N_EXPERTS, D_MODEL, D_EXPERT), D_MODEL ** -0.5),
        'w_exp_up': normal((DEPTH, N_EXPERTS, D_MODEL, D_EXPERT), D_MODEL ** -0.5),
        'w_exp_down': normal((DEPTH, N_EXPERTS, D_EXPERT, D_MODEL), D_EXPERT ** -0.5),
        'w_ple': normal((DEPTH, D_PLE, D_MODEL), D_PLE ** -0.5),
        'norm_ple': normal((DEPTH, D_MODEL), 0.1, 1.0),
        'w_ple_gate': normal((DEPTH, D_MODEL, D_MODEL), D_MODEL ** -0.5),
    }


def reference(x_prompt, x_sample, p_prompt, p_sample,
              cache_a_k, cache_a_v, cache_a_logf, cache_b_k, cache_b_v, cache_c_k, cache_c_v,
              norm_mix, w_in, b_f, g_qa, g_ka, g_qb, g_kb, rel_bias, g_mix_out, w_out,
              norm_ffn, w_router_group, b_router_group, w_router_expert, b_router_expert,
              w_exp_gate, w_exp_up, w_exp_down, w_ple, norm_ple, w_ple_gate):
    hp, hs = x_prompt, x_sample
    new_p = [[] for _ in range(7)]
    new_s = [[] for _ in range(7)]
    for i in range(DEPTH):
        proj_w = (w_in[i], b_f[i], g_qa[i], g_ka[i], g_qb[i], g_kb[i])
        post_w = (g_mix_out[i], w_out[i], norm_ffn[i], w_router_group[i], b_router_group[i],
                  w_router_expert[i], b_router_expert[i], w_exp_gate[i], w_exp_up[i], w_exp_down[i],
                  w_ple[i], norm_ple[i], w_ple_gate[i])
        outs_p, rows_p = mix_prompt(project_heads(rms_norm(hp, norm_mix[i]), *proj_w), rel_bias[i])
        hp = finish_layer(hp, outs_p, p_prompt[i], *post_w)
        outs_s, rows_s = mix_sample(project_heads(rms_norm(hs, norm_mix[i]), *proj_w),
                                    cache_a_k[i], cache_a_v[i], cache_a_logf[i], cache_b_k[i], cache_b_v[i],
                                    cache_c_k[i], cache_c_v[i], rel_bias[i])
        hs = finish_layer(hs, outs_s, p_sample[i], *post_w)
        for j in range(7):
            new_p[j].append(rows_p[j])
            new_s[j].append(rows_s[j])
    a_k_p, a_v_p, a_f_p, b_k_p, b_v_p, c_k_p, c_v_p = [jnp.stack(r, axis=0) for r in new_p]
    a_k_s, a_v_s, a_f_s, b_k_s, b_v_s, c_k_s, c_v_s = [jnp.stack(r, axis=0) for r in new_s]
    return (hp, hs, a_k_p, a_v_p, a_f_p, b_k_p, b_v_p, c_k_p, c_v_p,
            a_k_s, a_v_s, a_f_s, b_k_s, b_v_s, c_k_s, c_v_s)
```

```python
import functools

import jax
import jax.numpy as jnp
from jax import lax
from jax.experimental import pallas as pl
from jax.experimental.pallas import tpu as pltpu

F32 = jnp.float32
BF16 = jnp.bfloat16

D_MODEL = 1024
HEAD_DIM = 64
H_A, H_B, H_C = 6, 5, 5
W_A, W_B, W_C = H_A * HEAD_DIM, H_B * HEAD_DIM, H_C * HEAD_DIM
GROUP_W = 384
PAIR_W = 2 * HEAD_DIM
N_PAIRS = GROUP_W // PAIR_W
CHUNK = 64
BAND_CHUNKS = 8
BAND = BAND_CHUNKS * CHUNK
REL_CLIP = 128
N_GROUPS = 4
EXPERTS_PER_GROUP = 4
N_EXPERTS = N_GROUPS * EXPERTS_PER_GROUP
D_EXPERT = D_MODEL // 4
EPS = 1e-6
ATTN_SCALE = HEAD_DIM ** -0.5
NEG_INF = -1e30
LANES = 128
VMEM_LIMIT = 48 * 1024 * 1024

ATT_TILE = 256
W_IN_COLS = 3 * 3 * GROUP_W + LANES

_NT = (((1,), (1,)), ((), ()))


def _cparams(*sem):
    return pltpu.CompilerParams(dimension_semantics=sem, vmem_limit_bytes=VMEM_LIMIT)


def _dot(a, b):
    return jnp.dot(a, b, preferred_element_type=F32)


def _dot_nt(a, b):
    return lax.dot_general(a, b, _NT, preferred_element_type=F32)


def _split3(x):
    hi = x.astype(BF16)
    r1 = x - hi.astype(F32)
    mid = r1.astype(BF16)
    lo = (r1 - mid.astype(F32)).astype(BF16)
    return hi, mid, lo


def _split2(x):
    hi = x.astype(BF16)
    lo = (x - hi.astype(F32)).astype(BF16)
    return hi, lo


def _lane_iota(n=LANES):
    return lax.broadcasted_iota(jnp.int32, (1, n), 1)


def _half_mask(half):
    lane = _lane_iota()
    return (lane < HEAD_DIM) if half == 0 else (lane >= HEAD_DIM)


def _softplus(z):
    return jnp.maximum(z, 0.0) + jnp.log(1.0 + jnp.exp(-jnp.abs(z)))


def _sigmoid(z):
    return 1.0 / (1.0 + jnp.exp(-z))


def _rms(x, g):
    return x * lax.rsqrt(jnp.mean(x * x, axis=-1, keepdims=True) + EPS) * g


def _pair_rms(z, gain):
    first = _half_mask(0)
    outs = []
    for j in range(N_PAIRS):
        blk = z[:, j * PAIR_W:(j + 1) * PAIR_W]
        sq = blk * blk
        lo = jnp.sum(jnp.where(first, sq, 0.0), axis=-1, keepdims=True)
        hi = jnp.sum(jnp.where(first, 0.0, sq), axis=-1, keepdims=True)
        ms = jnp.where(first, lo, hi) * (1.0 / HEAD_DIM)
        outs.append(blk * lax.rsqrt(ms + EPS) * gain[:, j * PAIR_W:(j + 1) * PAIR_W])
    return jnp.concatenate(outs, axis=-1)


def _inproj_kernel(x_ref, gn_ref, w_ref, gq_ref, gk_ref, bf_ref,
                   qa_ref, qb_ref, qc_ref, ka_ref, kb_ref, kc_ref, va_ref, vb_ref, vc_ref,
                   akf_ref, avf_ref, bkf_ref, bvf_ref, ckf_ref, cvf_ref, logf_ref):
    xb = _rms(x_ref[...], gn_ref[...]).astype(BF16)

    def proj(group):
        return _dot(xb, w_ref[:, group * GROUP_W:(group + 1) * GROUP_W])

    qa_ref[...] = (_pair_rms(proj(0), gq_ref[:, :GROUP_W]) * ATTN_SCALE).astype(BF16)
    qb_ref[...] = (_pair_rms(proj(1), gq_ref[:, GROUP_W:]) * ATTN_SCALE).astype(BF16)
    qc_ref[...] = (proj(2) * ATTN_SCALE).astype(BF16)
    ka = _pair_rms(proj(3), gk_ref[:, :GROUP_W])
    akf_ref[...] = ka
    ka_ref[...] = ka.astype(BF16)
    kb = _pair_rms(proj(4), gk_ref[:, GROUP_W:])
    bkf_ref[...] = kb[:, :W_B]
    kb_ref[...] = kb.astype(BF16)
    kc = proj(5)
    ckf_ref[...] = kc[:, :W_C]
    kc_ref[...] = kc.astype(BF16)
    va = proj(6)
    avf_ref[...] = va
    va_ref[...] = va.astype(BF16)
    vb = proj(7)
    bvf_ref[...] = vb[:, :W_B]
    vb_ref[...] = vb.astype(BF16)
    vc = proj(8)
    cvf_ref[...] = vc[:, :W_C]
    vc_ref[...] = vc.astype(BF16)
    zf = _dot(xb, w_ref[:, 9 * GROUP_W:]) + bf_ref[...]
    logf_ref[...] = jnp.minimum(zf, 0.0) - jnp.log(1.0 + jnp.exp(-jnp.abs(zf)))


def _inproj(x, lw, tm):
    t = x.shape[0]
    row = lambda w: pl.BlockSpec((tm, w), lambda i: (i, 0))
    full = lambda a: pl.BlockSpec(a.shape, lambda i: (0,) * a.ndim)
    bf_shape = jax.ShapeDtypeStruct((t, GROUP_W), BF16)
    out_shape = ([bf_shape] * 9 + [
        jax.ShapeDtypeStruct((t, W_A), F32), jax.ShapeDtypeStruct((t, W_A), F32),
        jax.ShapeDtypeStruct((t, W_B), F32), jax.ShapeDtypeStruct((t, W_B), F32),
        jax.ShapeDtypeStruct((t, W_C), F32), jax.ShapeDtypeStruct((t, W_C), F32),
        jax.ShapeDtypeStruct((t, LANES), F32)])
    out_specs = [row(GROUP_W)] * 9 + [row(W_A), row(W_A), row(W_B), row(W_B), row(W_C), row(W_C), row(LANES)]
    ins = (x, lw['norm_mix'], lw['w_in'], lw['gq'], lw['gk'], lw['b_f'])
    return pl.pallas_call(
        _inproj_kernel, grid=(t // tm,),
        in_specs=[row(D_MODEL)] + [full(a) for a in ins[1:]],
        out_specs=out_specs, out_shape=out_shape,
        compiler_params=_cparams("parallel"), name="inproj")(*ins)


def _cumsum_prompt_kernel(logf_ref, ccol_ref, crow_ref, *, nblk):
    tb = ATT_TILE
    r = lax.broadcasted_iota(jnp.int32, (tb, tb), 0)
    c = lax.broadcasted_iota(jnp.int32, (tb, tb), 1)
    tri = jnp.where(c <= r, 1.0, 0.0).astype(BF16)
    carry = jnp.zeros((1, LANES), F32)
    for blk in range(nblk):
        hi, mid, lo = _split3(logf_ref[blk * tb:(blk + 1) * tb, :])
        cs = (_dot(tri, hi) + _dot(tri, mid) + _dot(tri, lo)) + carry
        ccol_ref[blk * tb:(blk + 1) * tb, :] = cs
        carry = cs[tb - 1:tb, :]
        crow_ref[0, blk] = jnp.transpose(cs)[:8, :]


def _cumsum_prompt(logf, nb, s):
    nblk = s // ATT_TILE
    return pl.pallas_call(
        functools.partial(_cumsum_prompt_kernel, nblk=nblk), grid=(nb,),
        in_specs=[pl.BlockSpec((s, LANES), lambda b: (b, 0))],
        out_specs=[pl.BlockSpec((s, LANES), lambda b: (b, 0)),
                   pl.BlockSpec((1, nblk, 8, ATT_TILE), lambda b: (b, 0, 0, 0))],
        out_shape=[jax.ShapeDtypeStruct((nb * s, LANES), F32),
                   jax.ShapeDtypeStruct((nb, nblk, 8, ATT_TILE), F32)],
        compiler_params=_cparams("parallel"), name="cumsum_prompt")(logf)


def _suffix_sum_kernel(x_ref, o_ref):
    p = x_ref.shape[1]
    j = lax.broadcasted_iota(jnp.int32, (p, p), 0)
    s = lax.broadcasted_iota(jnp.int32, (p, p), 1)
    tri = jnp.where(j > s, 1.0, 0.0).astype(BF16)
    hi, mid, lo = _split3(x_ref[...])
    o_ref[...] = _dot(hi, tri) + _dot(mid, tri) + _dot(lo, tri)


def _suffix_sum(x):
    return pl.pallas_call(
        _suffix_sum_kernel, out_shape=jax.ShapeDtypeStruct(x.shape, F32),
        compiler_params=pltpu.CompilerParams(vmem_limit_bytes=VMEM_LIMIT), name="suffix_sum")(x)


def _attn_a_kernel(q_ref, k_ref, v_ref, ccol_ref, crow_ref, o_ref):
    tq = ATT_TILE
    qi = pl.program_id(1)
    first = _half_mask(0)
    for pair in range(N_PAIRS):
        cs = slice(pair * PAIR_W, (pair + 1) * PAIR_W)
        q2 = q_ref[:, cs]
        res = []
        for half in range(2):
            h = 2 * pair + half
            qm = jnp.where(_half_mask(half), q2, jnp.zeros_like(q2))
            cq = ccol_ref[:, h:h + 1]

            def step(kt, carry, masked, qm=qm, cq=cq, h=h, cs=cs):
                m, l, acc = carry
                ks = pl.multiple_of(kt * tq, tq)
                k2 = k_ref[pl.ds(ks, tq), cs]
                v2 = v_ref[pl.ds(ks, tq), cs]
                s = _dot_nt(qm, k2) + (cq - crow_ref[0, kt, h:h + 1, :])
                if masked:
                    r = lax.broadcasted_iota(jnp.int32, (tq, tq), 0)
                    c = lax.broadcasted_iota(jnp.int32, (tq, tq), 1)
                    s = jnp.where(c <= r, s, NEG_INF)
                m_new = jnp.maximum(m, jnp.max(s, axis=-1, keepdims=True))
                alpha = jnp.exp(m - m_new)
                p = jnp.exp(s - m_new)
                l = alpha * l + jnp.sum(p, axis=-1, keepdims=True)
                acc = alpha * acc + _dot(p.astype(BF16), v2)
                return m_new, l, acc

            init = (jnp.full((tq, 1), NEG_INF, F32), jnp.zeros((tq, 1), F32),
                    jnp.zeros((tq, PAIR_W), F32))
            carry = lax.fori_loop(0, qi, lambda kt, c: step(kt, c, False), init)
            m, l, acc = step(qi, carry, True)
            res.append(acc / l)
        o_ref[:, cs] = jnp.where(first, res[0], res[1])


def _attn_b_kernel(q_ref, k_ref, v_ref, tab_ref, o_ref):
    tq = ATT_TILE
    win = tq + BAND
    i = pl.program_id(1)
    var = jnp.minimum(i, BAND // tq)
    ws = pl.multiple_of(jnp.maximum(i * tq - BAND, 0), tq)
    first = _half_mask(0)
    for pair in range(N_PAIRS):
        cs = slice(pair * PAIR_W, (pair + 1) * PAIR_W)
        q2 = q_ref[:, cs]
        k2 = k_ref[pl.ds(ws, win), cs]
        v2 = v_ref[pl.ds(ws, win), cs]
        res = []
        for half in range(2):
            h = 2 * pair + half
            if h >= H_B:
                res.append(jnp.zeros((tq, PAIR_W), F32))
                continue
            qm = jnp.where(_half_mask(half), q2, jnp.zeros_like(q2))
            s = _dot_nt(qm, k2) + tab_ref[var, h]
            p = jnp.exp(s - jnp.max(s, axis=-1, keepdims=True))
            l = jnp.sum(p, axis=-1, keepdims=True)
            res.append(_dot(p.astype(BF16), v2) / l)
        o_ref[:, cs] = jnp.where(first, res[0], res[1])


def _suffix_tri(n):
    j = lax.broadcasted_iota(jnp.int32, (n, n), 0)
    s = lax.broadcasted_iota(jnp.int32, (n, n), 1)
    return jnp.where(j >= s, 1.0, 0.0).astype(BF16)


def _stick_tile(qm, k2, v2, tri, carry, acc, seen):
    z = _dot_nt(qm, k2)
    sp = _softplus(z)
    if seen is not None:
        sp = jnp.where(seen, sp, 0.0)
    hi, lo = _split2(sp)
    s_in = _dot(hi, tri) + _dot(lo, tri)
    a = jnp.exp(z - (s_in + carry))
    if seen is not None:
        a = jnp.where(seen, a, 0.0)
    acc = acc + _dot(a.astype(BF16), v2)
    return carry + s_in[:, 0:1], acc


def _attn_c_kernel(q_ref, k_ref, v_ref, o_ref):
    tq = ATT_TILE
    qi = pl.program_id(1)
    first = _half_mask(0)
    tri = _suffix_tri(tq)
    for pair in range(N_PAIRS):
        cs = slice(pair * PAIR_W, (pair + 1) * PAIR_W)
        q2 = q_ref[:, cs]
        res = []
        for half in range(2):
            h = 2 * pair + half
            if h >= H_C:
                res.append(jnp.zeros((tq, PAIR_W), F32))
                continue
            qm = jnp.where(_half_mask(half), q2, jnp.zeros_like(q2))

            def tile(kt, carry, acc, seen, qm=qm, cs=cs):
                ks = pl.multiple_of(kt * tq, tq)
                return _stick_tile(qm, k_ref[pl.ds(ks, tq), cs], v_ref[pl.ds(ks, tq), cs],
                                   tri, carry, acc, seen)

            r = lax.broadcasted_iota(jnp.int32, (tq, tq), 0)
            c = lax.broadcasted_iota(jnp.int32, (tq, tq), 1)
            carry, acc = tile(qi, jnp.zeros((tq, 1), F32), jnp.zeros((tq, PAIR_W), F32), c < r)
            carry, acc = lax.fori_loop(
                0, qi, lambda j, ca: tile(qi - 1 - j, ca[0], ca[1], None), (carry, acc))
            res.append(acc)
        o_ref[:, cs] = jnp.where(first, res[0], res[1])


def _prompt_attention(kernel, name, q, k, v, extra, extra_specs, nb, s):
    nq = s // ATT_TILE
    qspec = pl.BlockSpec((ATT_TILE, GROUP_W), lambda b, i: (b * nq + i, 0))
    kvspec = pl.BlockSpec((s, GROUP_W), lambda b, i: (b, 0))
    return pl.pallas_call(
        kernel, grid=(nb, nq),
        in_specs=[qspec, kvspec, kvspec] + extra_specs,
        out_specs=qspec, out_shape=jax.ShapeDtypeStruct((nb * s, GROUP_W), F32),
        compiler_params=_cparams("parallel", "arbitrary"), name=name)(q, k, v, *extra)


def _pair_cols(pair, width):
    lo = pair * PAIR_W
    return slice(lo, min(lo + PAIR_W, width))


def _sample_heads(n_heads, q_ref, width):
    for pair in range(N_PAIRS):
        cols = _pair_cols(pair, width)
        q2 = q_ref[:, cols]
        for half in range(2):
            h = 2 * pair + half
            if h >= n_heads:
                continue
            if cols.stop - cols.start == PAIR_W:
                qm = jnp.where(_half_mask(half), q2, jnp.zeros_like(q2))
            else:
                qm = q2
            yield pair, half, h, cols, qm


def _store_heads(o_ref, pair, res, width):
    lo = pair * PAIR_W
    if len(res) == 2:
        o_ref[:, lo:lo + PAIR_W] = jnp.where(_half_mask(0), res[0], res[1])
    else:
        o_ref[:, lo:lo + HEAD_DIM] = res[0]
        o_ref[:, lo + HEAD_DIM:lo + PAIR_W] = jnp.zeros_like(res[0])


def _sample_attn_kernel(qa_ref, qb_ref, qc_ref, kan_ref, kbn_ref, kcn_ref, van_ref, vbn_ref, vcn_ref,
                        cak_ref, cav_ref, cbk_ref, cbv_ref, cck_ref, ccv_ref,
                        rsum_ref, logf_ref, tabc_ref, tabn_ref,
                        oa_ref, ob_ref, oc_ref):
    t = qa_ref.shape[0]
    past = cak_ref.shape[1]
    r = lax.broadcasted_iota(jnp.int32, (t, t), 0)
    c = lax.broadcasted_iota(jnp.int32, (t, t), 1)

    ltri = jnp.where(c <= r, 1.0, 0.0).astype(BF16)
    hi, mid, lo = _split3(logf_ref[...])
    pcol = _dot(ltri, hi) + _dot(ltri, mid) + _dot(ltri, lo)
    prow = jnp.transpose(jnp.concatenate([pcol, jnp.zeros((LANES - t, LANES), F32)], axis=0))
    res = []
    for pair, half, h, cols, qm in _sample_heads(H_A, qa_ref, W_A):
        if half == 0:
            res = []
            kc = cak_ref[0, :, cols].astype(BF16)
            vc = cav_ref[0, :, cols].astype(BF16)
        pq = pcol[:, h:h + 1]
        s_c = _dot_nt(qm, kc) + (pq + rsum_ref[0, h:h + 1, :])
        s_n = _dot_nt(qm, kan_ref[:, cols]) + (pq - prow[h:h + 1, 0:t])
        s_n = jnp.where(c <= r, s_n, NEG_INF)
        m = jnp.maximum(jnp.max(s_c, axis=-1, keepdims=True), jnp.max(s_n, axis=-1, keepdims=True))
        p_c = jnp.exp(s_c - m)
        p_n = jnp.exp(s_n - m)
        l = jnp.sum(p_c, axis=-1, keepdims=True) + jnp.sum(p_n, axis=-1, keepdims=True)
        res.append((_dot(p_c.astype(BF16), vc) + _dot(p_n.astype(BF16), van_ref[:, cols])) / l)
        if half == 1:
            _store_heads(oa_ref, pair, res, W_A)

    for pair, half, h, cols, qm in _sample_heads(H_B, qb_ref, W_B):
        if half == 0:
            res = []
            kc = cbk_ref[0, :, cols].astype(BF16)
            vc = cbv_ref[0, :, cols].astype(BF16)
        s_c = _dot_nt(qm, kc) + tabc_ref[h]
        s_n = _dot_nt(qm, kbn_ref[:, cols]) + tabn_ref[h]
        m = jnp.maximum(jnp.max(s_c, axis=-1, keepdims=True), jnp.max(s_n, axis=-1, keepdims=True))
        p_c = jnp.exp(s_c - m)
        p_n = jnp.exp(s_n - m)
        l = jnp.sum(p_c, axis=-1, keepdims=True) + jnp.sum(p_n, axis=-1, keepdims=True)
        res.append((_dot(p_c.astype(BF16), vc) + _dot(p_n.astype(BF16), vbn_ref[:, cols])) / l)
        if half == 1 or h == H_B - 1:
            _store_heads(ob_ref, pair, res, W_B)

    tri_n = _suffix_tri(t)
    tri_c = _suffix_tri(ATT_TILE)
    for pair, half, h, cols, qm in _sample_heads(H_C, qc_ref, W_C):
        if half == 0:
            res = []
        carry, acc = _stick_tile(qm, kcn_ref[:, cols], vcn_ref[:, cols], tri_n,
                                 jnp.zeros((t, 1), F32), jnp.zeros((t, cols.stop - cols.start), F32), c < r)
        for blk in reversed(range(past // ATT_TILE)):
            rows = slice(blk * ATT_TILE, (blk + 1) * ATT_TILE)
            carry, acc = _stick_tile(qm, cck_ref[0, rows, cols].astype(BF16),
                                     ccv_ref[0, rows, cols].astype(BF16), tri_c, carry, acc, None)
        res.append(acc)
        if half == 1 or h == H_C - 1:
            _store_heads(oc_ref, pair, res, W_C)


def _sample_attention(proj, caches, rsum, tabc, tabn, nb, t):
    qa, qb, qc, ka, kb, kc, va, vb, vc = proj[:9]
    logf = proj[15]
    new = pl.BlockSpec((t, GROUP_W), lambda b: (b, 0))
    cache = lambda a: pl.BlockSpec((1,) + a.shape[1:], lambda b: (b, 0, 0))
    full = lambda a: pl.BlockSpec(a.shape, lambda b: (0,) * a.ndim)
    return pl.pallas_call(
        _sample_attn_kernel, grid=(nb,),
        in_specs=[new] * 9 + [cache(a) for a in caches] + [
            cache(rsum), pl.BlockSpec((t, LANES), lambda b: (b, 0)), full(tabc), full(tabn)],
        out_specs=[new] * 3,
        out_shape=[jax.ShapeDtypeStruct((nb * t, GROUP_W), F32)] * 3,
        compiler_params=_cparams("parallel"), name="sample_attn")(
            qa, qb, qc, ka, kb, kc, va, vb, vc, *caches, rsum, logf, tabc, tabn)


def _masked_max(x, mask):
    return jnp.max(jnp.where(mask, x, -jnp.inf), axis=-1, keepdims=True)


def _first_lane(mask, lane):
    return jnp.min(jnp.where(mask, lane, float(LANES)), axis=-1, keepdims=True)


def _route(logits):
    lane = _lane_iota().astype(F32)
    is_g = (lane >= N_EXPERTS) & (lane < N_EXPERTS + N_GROUPS)
    gmax = _masked_max(logits, is_g)
    p_g = 1.0 / jnp.sum(jnp.where(is_g, jnp.exp(logits - gmax), 0.0), axis=-1, keepdims=True)
    g_sel = _first_lane(is_g & (logits == gmax), lane) - N_EXPERTS
    lo = g_sel * EXPERTS_PER_GROUP
    in_g = (lane >= lo) & (lane < lo + EXPERTS_PER_GROUP)
    l1 = _masked_max(logits, in_g)
    i1 = _first_lane(in_g & (logits == l1), lane)
    rest = in_g & (lane != i1)
    l2 = _masked_max(logits, rest)
    i2 = _first_lane(rest & (logits == l2), lane)
    e2 = jnp.exp(l2 - l1)
    w1 = p_g / (1.0 + e2)
    w2 = p_g * e2 / (1.0 + e2)
    return jnp.where(lane == i1, w1, jnp.where(lane == i2, w2, 0.0))


def _post_kernel(oa_ref, ob_ref, oc_ref, h_ref, gmix_ref, wout_ref, nffn_ref, wrh_ref, wrl_ref, br_ref,
                 h1_ref, xn_ref, gates_ref):
    def gnorm(o, width):
        ms = jnp.sum(o * o, axis=-1, keepdims=True) * (1.0 / width)
        return o * lax.rsqrt(ms + EPS)

    h1 = h_ref[...]
    for g, (o_ref, width) in enumerate(((oa_ref, W_A), (ob_ref, W_B), (oc_ref, W_C))):
        merged = (gnorm(o_ref[...], width) * gmix_ref[g]).astype(BF16)
        h1 = h1 + _dot(merged, wout_ref[g])
    h1_ref[...] = h1
    xf = _rms(h1, nffn_ref[...])
    hi, lo = _split2(xf)
    xn_ref[...] = hi
    logits = _dot(hi, wrh_ref[...]) + _dot(lo, wrh_ref[...]) + _dot(hi, wrl_ref[...]) + br_ref[...]
    gates_ref[...] = _route(logits)


def _post(oa, ob, oc, h, lw, tm):
    t = h.shape[0]
    row = lambda w: pl.BlockSpec((tm, w), lambda i: (i, 0))
    full = lambda a: pl.BlockSpec(a.shape, lambda i: (0,) * a.ndim)
    ws = (lw['g_mix'], lw['w_out'], lw['norm_ffn'], lw['w_r_hi'], lw['w_r_lo'], lw['b_r'])
    return pl.pallas_call(
        _post_kernel, grid=(t // tm,),
        in_specs=[row(GROUP_W)] * 3 + [row(D_MODEL)] + [full(a) for a in ws],
        out_specs=[row(D_MODEL), row(D_MODEL), row(LANES)],
        out_shape=[jax.ShapeDtypeStruct((t, D_MODEL), F32), jax.ShapeDtypeStruct((t, D_MODEL), BF16),
                   jax.ShapeDtypeStruct((t, LANES), F32)],
        compiler_params=_cparams("parallel"), name="post")(oa, ob, oc, h, *ws)


def _moe_kernel(xn_ref, gates_ref, wg_ref, wu_ref, wd_ref, h1_ref, p_ref, wple_ref, nple_ref, wpg_ref,
                out_ref, acc_ref):
    e = pl.program_id(1)

    @pl.when(e == 0)
    def _():
        acc_ref[...] = jnp.zeros_like(acc_ref)

    x = xn_ref[...]
    g = _dot(x, wg_ref[0])
    u = _dot(x, wu_ref[0])
    gate = jnp.sum(jnp.where(_lane_iota() == e, gates_ref[...], 0.0), axis=-1, keepdims=True)
    act = (g * _sigmoid(g)) * u * gate
    acc_ref[...] += _dot(act.astype(BF16), wd_ref[0])

    @pl.when(e == N_EXPERTS - 1)
    def _():
        h2 = h1_ref[...] + acc_ref[...]
        gate_ple = _sigmoid(_dot(_rms(h2, nple_ref[...]).astype(BF16), wpg_ref[...]))
        out_ref[...] = h2 + _dot(p_ref[...].astype(BF16), wple_ref[...]) * gate_ple


def _moe(xn, gates, h1, p, lw, tm):
    t = xn.shape[0]
    d_ple = p.shape[1]
    row = lambda w: pl.BlockSpec((tm, w), lambda i, e: (i, 0))
    full = lambda a: pl.BlockSpec(a.shape, lambda i, e: (0,) * a.ndim)
    exp = lambda a: pl.BlockSpec((1,) + a.shape[1:], lambda i, e: (e, 0, 0))
    return pl.pallas_call(
        _moe_kernel, grid=(t // tm, N_EXPERTS),
        in_specs=[row(D_MODEL), row(LANES), exp(lw['w_eg']), exp(lw['w_eu']), exp(lw['w_ed']),
                  row(D_MODEL), row(d_ple), full(lw['w_ple']), full(lw['norm_ple']), full(lw['w_pg'])],
        out_specs=row(D_MODEL), out_shape=jax.ShapeDtypeStruct((t, D_MODEL), F32),
        scratch_shapes=[pltpu.VMEM((tm, D_MODEL), F32)],
        compiler_params=_cparams("parallel", "arbitrary"), name="moe")(
            xn, gates, lw['w_eg'], lw['w_eu'], lw['w_ed'], h1, p, lw['w_ple'], lw['norm_ple'], lw['w_pg'])


def _regroup_cols(w):
    z = jnp.zeros((w.shape[0], HEAD_DIM), w.dtype)
    return jnp.concatenate([w[:, :W_A], w[:, W_A:W_A + W_B], z, w[:, W_A + W_B:], z], axis=1)


def _regroup_rows(w):
    z = jnp.zeros((HEAD_DIM, w.shape[1]), w.dtype)
    return jnp.stack([w[:W_A], jnp.concatenate([w[W_A:W_A + W_B], z], axis=0),
                      jnp.concatenate([w[W_A + W_B:], z], axis=0)])


def _pad_lanes(a, n=LANES):
    return jnp.pad(a, [(0, 0)] * (a.ndim - 1) + [(0, n - a.shape[-1])])


def _rel_tables(rel_bias, t_new, rows_b):
    tq, win = ATT_TILE, ATT_TILE + BAND
    q = jnp.arange(tq)[:, None]
    tabs = []
    for var in range(BAND // tq + 1):
        rel_k = jnp.arange(win)[None, :] - var * tq
        kch = jnp.floor_divide(rel_k, CHUNK)
        qch = q // CHUNK
        valid = (kch <= qch) & (kch >= qch - BAND_CHUNKS)
        idx = jnp.clip(q - rel_k, -REL_CLIP, REL_CLIP) + REL_CLIP
        tabs.append(jnp.where(valid[None], rel_bias[:, idx], NEG_INF))
    tab_prompt = jnp.stack(tabs).astype(F32)
    tn = jnp.arange(t_new)
    rel_c = tn[:, None] + rows_b - jnp.arange(rows_b)[None, :]
    tab_c = rel_bias[:, jnp.clip(rel_c, -REL_CLIP, REL_CLIP) + REL_CLIP].astype(F32)
    rel_n = tn[:, None] - tn[None, :]
    tab_n = rel_bias[:, jnp.clip(rel_n, -REL_CLIP, REL_CLIP) + REL_CLIP].astype(F32)
    return tab_prompt, tab_c, tab_n


def _layer_weights(i, norm_mix, w_in, b_f, g_qa, g_ka, g_qb, g_kb, g_mix_out, w_out, norm_ffn,
                   w_rg, b_rg, w_re, b_re, w_eg, w_eu, w_ed, w_ple, norm_ple, w_pg):
    d_mix = W_A + W_B + W_C
    wi = w_in[i]
    w_big = jnp.concatenate(
        [_regroup_cols(wi[:, j * d_mix:(j + 1) * d_mix]) for j in range(3)]
        + [_pad_lanes(wi[:, 3 * d_mix:])], axis=1).astype(BF16)
    tile6 = lambda g: jnp.tile(g, GROUP_W // HEAD_DIM)
    w_r = _pad_lanes(jnp.concatenate([w_re[i], w_rg[i]], axis=1))
    w_r_hi = w_r.astype(BF16)
    gm = g_mix_out[i]
    zpad = jnp.zeros((HEAD_DIM,), F32)
    g_mix = jnp.stack([gm[:W_A], jnp.concatenate([gm[W_A:W_A + W_B], zpad]),
                       jnp.concatenate([gm[W_A + W_B:], zpad])])[:, None, :]
    return dict(
        norm_mix=norm_mix[i][None], w_in=w_big,
        gq=jnp.concatenate([tile6(g_qa[i]), tile6(g_qb[i])])[None],
        gk=jnp.concatenate([tile6(g_ka[i]), tile6(g_kb[i])])[None],
        b_f=_pad_lanes(b_f[i])[None],
        g_mix=g_mix, w_out=_regroup_rows(w_out[i]).astype(BF16), norm_ffn=norm_ffn[i][None],
        w_r_hi=w_r_hi, w_r_lo=(w_r - w_r_hi.astype(F32)).astype(BF16),
        b_r=_pad_lanes(jnp.concatenate([b_re[i], b_rg[i]]))[None],
        w_eg=w_eg[i].astype(BF16), w_eu=w_eu[i].astype(BF16), w_ed=w_ed[i].astype(BF16),
        w_ple=w_ple[i].astype(BF16), norm_ple=norm_ple[i][None], w_pg=w_pg[i].astype(BF16))


def _token_tile(t, pref):
    return pref if t % pref == 0 else t


def kernel(x_prompt, x_sample, p_prompt, p_sample, cache_a_k, cache_a_v, cache_a_logf, cache_b_k, cache_b_v, cache_c_k, cache_c_v, norm_mix, w_in, b_f, g_qa, g_ka, g_qb, g_kb, rel_bias, g_mix_out, w_out, norm_ffn, w_router_group, b_router_group, w_router_expert, b_router_expert, w_exp_gate, w_exp_up, w_exp_down, w_ple, norm_ple, w_ple_gate):
    nb, s, d = x_prompt.shape
    ns, t_new, _ = x_sample.shape
    depth = w_in.shape[0]
    past = cache_a_k.shape[2]
    rows_b = cache_b_k.shape[2]
    assert d == D_MODEL and s % ATT_TILE == 0 and s >= BAND + ATT_TILE and past % ATT_TILE == 0
    tp, ts = nb * s, ns * t_new
    keep = s - BAND

    hp = x_prompt.reshape(tp, d)
    hs = x_sample.reshape(ts, d)
    lf_rows = jnp.pad(jnp.swapaxes(cache_a_logf, 2, 3), ((0, 0), (0, 0), (0, 8 - H_A), (0, 0)))
    rsum = _suffix_sum(lf_rows.reshape(depth * ns * 8, past)).reshape(depth, ns, 8, past)

    new_p = [[] for _ in range(7)]
    new_s = [[] for _ in range(7)]
    for i in range(depth):
        lw = _layer_weights(i, norm_mix, w_in, b_f, g_qa, g_ka, g_qb, g_kb, g_mix_out, w_out, norm_ffn,
                            w_router_group, b_router_group, w_router_expert, b_router_expert,
                            w_exp_gate, w_exp_up, w_exp_down, w_ple, norm_ple, w_ple_gate)
        tab_p, tab_c, tab_n = _rel_tables(rel_bias[i], t_new, rows_b)

        pr = _inproj(hp, lw, _token_tile(tp, 512))
        qa, qb, qc, ka, kb, kc, va, vb, vc, akf, avf, bkf, bvf, ckf, cvf, logf = pr
        ccol, crow = _cumsum_prompt(logf, nb, s)
        nblk = s // ATT_TILE
        oa = _prompt_attention(
            _attn_a_kernel, "attn_a", qa, ka, va, (ccol, crow),
            [pl.BlockSpec((ATT_TILE, LANES), lambda b, j: (b * nblk + j, 0)),
             pl.BlockSpec((1, nblk, 8, ATT_TILE), lambda b, j: (b, 0, 0, 0))], nb, s)
        ob = _prompt_attention(
            _attn_b_kernel, "attn_b", qb, kb, vb, (tab_p,),
            [pl.BlockSpec(tab_p.shape, lambda b, j: (0, 0, 0, 0))], nb, s)
        oc = _prompt_attention(_attn_c_kernel, "attn_c", qc, kc, vc, (), [], nb, s)
        h1, xn, gates = _post(oa, ob, oc, hp, lw, _token_tile(tp, 512))
        hp = _moe(xn, gates, h1, p_prompt[i].reshape(tp, -1), lw, _token_tile(tp, 1024))
        rows = (akf.reshape(nb, s, H_A, HEAD_DIM), avf.reshape(nb, s, H_A, HEAD_DIM),
                logf[:, :H_A].reshape(nb, s, H_A),
                bkf.reshape(nb, s, H_B, HEAD_DIM)[:, keep:], bvf.reshape(nb, s, H_B, HEAD_DIM)[:, keep:],
                ckf.reshape(nb, s, H_C, HEAD_DIM), cvf.reshape(nb, s, H_C, HEAD_DIM))
        for j in range(7):
            new_p[j].append(rows[j])

        sr = _inproj(hs, lw, _token_tile(ts, 512))
        caches = (cache_a_k[i].reshape(ns, past, W_A), cache_a_v[i].reshape(ns, past, W_A),
                  cache_b_k[i].reshape(ns, rows_b, W_B), cache_b_v[i].reshape(ns, rows_b, W_B),
                  cache_c_k[i].reshape(ns, past, W_C), cache_c_v[i].reshape(ns, past, W_C))
        oa, ob, oc = _sample_attention(sr, caches, rsum[i], tab_c, tab_n, ns, t_new)
        h1, xn, gates = _post(oa, ob, oc, hs, lw, _token_tile(ts, 512))
        hs = _moe(xn, gates, h1, p_sample[i].reshape(ts, -1), lw, _token_tile(ts, 1024))
        akf, avf, bkf, bvf, ckf, cvf, logf = sr[9:]
        rows = (akf.reshape(ns, t_new, H_A, HEAD_DIM), avf.reshape(ns, t_new, H_A, HEAD_DIM),
                logf[:, :H_A].reshape(ns, t_new, H_A),
                bkf.reshape(ns, t_new, H_B, HEAD_DIM), bvf.reshape(ns, t_new, H_B, HEAD_DIM),
                ckf.reshape(ns, t_new, H_C, HEAD_DIM), cvf.reshape(ns, t_new, H_C, HEAD_DIM))
        for j in range(7):
            new_s[j].append(rows[j])

    outs_p = [jnp.stack(r, axis=0) for r in new_p]
    outs_s = [jnp.stack(r, axis=0) for r in new_s]
    return (hp.reshape(nb, s, d), hs.reshape(ns, t_new, d), *outs_p, *outs_s)
```

```python
import functools

import jax
import jax.numpy as jnp
from jax import lax
from jax.experimental import pallas as pl
from jax.experimental.pallas import tpu as pltpu

F32 = jnp.float32
BF16 = jnp.bfloat16

D_MODEL = 1024
HEAD_DIM = 64
H_A, H_B, H_C = 6, 5, 5
W_A, W_B, W_C = H_A * HEAD_DIM, H_B * HEAD_DIM, H_C * HEAD_DIM
GROUP_W = 384
PAIR_W = 2 * HEAD_DIM
N_PAIRS = GROUP_W // PAIR_W
CHUNK = 64
BAND_CHUNKS = 8
BAND = BAND_CHUNKS * CHUNK
REL_CLIP = 128
N_GROUPS = 4
EXPERTS_PER_GROUP = 4
N_EXPERTS = N_GROUPS * EXPERTS_PER_GROUP
D_EXPERT = D_MODEL // 4
EPS = 1e-6
ATTN_SCALE = HEAD_DIM ** -0.5
NEG_INF = -1e30
LANES = 128
VMEM_LIMIT = 48 * 1024 * 1024

ATT_TILE = 256
W_IN_COLS = 3 * 3 * GROUP_W + LANES

_NT = (((1,), (1,)), ((), ()))


def _cparams(*sem):
    return pltpu.CompilerParams(dimension_semantics=sem, vmem_limit_bytes=VMEM_LIMIT)


def _dot(a, b):
    return jnp.dot(a, b, preferred_element_type=F32)


def _dot_nt(a, b):
    return lax.dot_general(a, b, _NT, preferred_element_type=F32)


def _split3(x):
    hi = x.astype(BF16)
    r1 = x - hi.astype(F32)
    mid = r1.astype(BF16)
    lo = (r1 - mid.astype(F32)).astype(BF16)
    return hi, mid, lo


def _split2(x):
    hi = x.astype(BF16)
    lo = (x - hi.astype(F32)).astype(BF16)
    return hi, lo


def _lane_iota(n=LANES):
    return lax.broadcasted_iota(jnp.int32, (1, n), 1)


def _half_mask(half):
    lane = _lane_iota()
    return (lane < HEAD_DIM) if half == 0 else (lane >= HEAD_DIM)


def _softplus(z):
    return jnp.maximum(z, 0.0) + jnp.log(1.0 + jnp.exp(-jnp.abs(z)))


def _sigmoid(z):
    return 1.0 / (1.0 + jnp.exp(-z))


def _rms(x, g):
    return x * lax.rsqrt(jnp.mean(x * x, axis=-1, keepdims=True) + EPS) * g


def _pair_rms(z, gain):
    first = _half_mask(0)
    outs = []
    for j in range(N_PAIRS):
        blk = z[:, j * PAIR_W:(j + 1) * PAIR_W]
        sq = blk * blk
        lo = jnp.sum(jnp.where(first, sq, 0.0), axis=-1, keepdims=True)
        hi = jnp.sum(jnp.where(first, 0.0, sq), axis=-1, keepdims=True)
        ms = jnp.where(first, lo, hi) * (1.0 / HEAD_DIM)
        outs.append(blk * lax.rsqrt(ms + EPS) * gain[:, j * PAIR_W:(j + 1) * PAIR_W])
    return jnp.concatenate(outs, axis=-1)


def _inproj_kernel(x_ref, gn_ref, w_ref, gq_ref, gk_ref, bf_ref,
                   qa_ref, qb_ref, qc_ref, ka_ref, kb_ref, kc_ref, va_ref, vb_ref, vc_ref,
                   akf_ref, avf_ref, bkf_ref, bvf_ref, ckf_ref, cvf_ref, logf_ref):
    xb = _rms(x_ref[...], gn_ref[...]).astype(BF16)

    def proj(group):
        return _dot(xb, w_ref[:, group * GROUP_W:(group + 1) * GROUP_W])

    qa_ref[...] = (_pair_rms(proj(0), gq_ref[:, :GROUP_W]) * ATTN_SCALE).astype(BF16)
    qb_ref[...] = (_pair_rms(proj(1), gq_ref[:, GROUP_W:]) * ATTN_SCALE).astype(BF16)
    qc_ref[...] = (proj(2) * ATTN_SCALE).astype(BF16)
    ka = _pair_rms(proj(3), gk_ref[:, :GROUP_W])
    akf_ref[...] = ka
    ka_ref[...] = ka.astype(BF16)
    kb = _pair_rms(proj(4), gk_ref[:, GROUP_W:])
    bkf_ref[...] = kb[:, :W_B]
    kb_ref[...] = kb.astype(BF16)
    kc = proj(5)
    ckf_ref[...] = kc[:, :W_C]
    kc_ref[...] = kc.astype(BF16)
    va = proj(6)
    avf_ref[...] = va
    va_ref[...] = va.astype(BF16)
    vb = proj(7)
    bvf_ref[...] = vb[:, :W_B]
    vb_ref[...] = vb.astype(BF16)
    vc = proj(8)
    cvf_ref[...] = vc[:, :W_C]
    vc_ref[...] = vc.astype(BF16)
    zf = _dot(xb, w_ref[:, 9 * GROUP_W:]) + bf_ref[...]
    logf_ref[...] = jnp.minimum(zf, 0.0) - jnp.log(1.0 + jnp.exp(-jnp.abs(zf)))


def _inproj(x, lw, tm):
    t = x.shape[0]
    row = lambda w: pl.BlockSpec((tm, w), lambda i: (i, 0))
    full = lambda a: pl.BlockSpec(a.shape, lambda i: (0,) * a.ndim)
    bf_shape = jax.ShapeDtypeStruct((t, GROUP_W), BF16)
    out_shape = ([bf_shape] * 9 + [
        jax.ShapeDtypeStruct((t, W_A), F32), jax.ShapeDtypeStruct((t, W_A), F32),
        jax.ShapeDtypeStruct((t, W_B), F32), jax.ShapeDtypeStruct((t, W_B), F32),
        jax.ShapeDtypeStruct((t, W_C), F32), jax.ShapeDtypeStruct((t, W_C), F32),
        jax.ShapeDtypeStruct((t, LANES), F32)])
    out_specs = [row(GROUP_W)] * 9 + [row(W_A), row(W_A), row(W_B), row(W_B), row(W_C), row(W_C), row(LANES)]
    ins = (x, lw['norm_mix'], lw['w_in'], lw['gq'], lw['gk'], lw['b_f'])
    return pl.pallas_call(
        _inproj_kernel, grid=(t // tm,),
        in_specs=[row(D_MODEL)] + [full(a) for a in ins[1:]],
        out_specs=out_specs, out_shape=out_shape,
        compiler_params=_cparams("parallel"), name="inproj")(*ins)


def _cumsum_prompt_kernel(logf_ref, ccol_ref, crow_ref, *, nblk):
    tb = ATT_TILE
    r = lax.broadcasted_iota(jnp.int32, (tb, tb), 0)
    c = lax.broadcasted_iota(jnp.int32, (tb, tb), 1)
    tri = jnp.where(c <= r, 1.0, 0.0).astype(BF16)
    carry = jnp.zeros((1, LANES), F32)
    for blk in range(nblk):
        hi, mid, lo = _split3(logf_ref[blk * tb:(blk + 1) * tb, :])
        cs = (_dot(tri, hi) + _dot(tri, mid) + _dot(tri, lo)) + carry
        ccol_ref[blk * tb:(blk + 1) * tb, :] = cs
        carry = cs[tb - 1:tb, :]
        crow_ref[0, blk] = jnp.transpose(cs)[:8, :]


def _cumsum_prompt(logf, nb, s):
    nblk = s // ATT_TILE
    return pl.pallas_call(
        functools.partial(_cumsum_prompt_kernel, nblk=nblk), grid=(nb,),
        in_specs=[pl.BlockSpec((s, LANES), lambda b: (b, 0))],
        out_specs=[pl.BlockSpec((s, LANES), lambda b: (b, 0)),
                   pl.BlockSpec((1, nblk, 8, ATT_TILE), lambda b: (b, 0, 0, 0))],
        out_shape=[jax.ShapeDtypeStruct((nb * s, LANES), F32),
                   jax.ShapeDtypeStruct((nb, nblk, 8, ATT_TILE), F32)],
        compiler_params=_cparams("parallel"), name="cumsum_prompt")(logf)


def _suffix_sum_kernel(x_ref, o_ref):
    p = x_ref.shape[1]
    j = lax.broadcasted_iota(jnp.int32, (p, p), 0)
    s = lax.broadcasted_iota(jnp.int32, (p, p), 1)
    tri = jnp.where(j > s, 1.0, 0.0).astype(BF16)
    hi, mid, lo = _split3(x_ref[...])
    o_ref[...] = _dot(hi, tri) + _dot(mid, tri) + _dot(lo, tri)


def _suffix_sum(x):
    return pl.pallas_call(
        _suffix_sum_kernel, out_shape=jax.ShapeDtypeStruct(x.shape, F32),
        compiler_params=pltpu.CompilerParams(vmem_limit_bytes=VMEM_LIMIT), name="suffix_sum")(x)


def _masked_q(q_ref, qm_ref, n_heads):
    for h in range(n_heads):
        q2 = q_ref[:, (h // 2) * PAIR_W:(h // 2 + 1) * PAIR_W]
        qm_ref[h] = jnp.where(_half_mask(h % 2), q2, jnp.zeros_like(q2))


def _pair_slice(h):
    return slice((h // 2) * PAIR_W, (h // 2 + 1) * PAIR_W)


def _attn_a_kernel(q_ref, k_ref, v_ref, ccol_ref, crow_ref, o_ref, qm_ref, m_ref, l_ref, acc_ref):
    tq = ATT_TILE
    qi = pl.program_id(1)
    _masked_q(q_ref, qm_ref, H_A)
    m_ref[...] = jnp.full(m_ref.shape, NEG_INF, F32)
    l_ref[...] = jnp.zeros(l_ref.shape, F32)
    acc_ref[...] = jnp.zeros(acc_ref.shape, F32)

    def step(kt, masked):
        ks = pl.multiple_of(kt * tq, tq)
        for h in range(H_A):
            cs = _pair_slice(h)
            s = _dot_nt(qm_ref[h], k_ref[pl.ds(ks, tq), cs])
            s = s + (ccol_ref[:, h:h + 1] - crow_ref[0, kt, h:h + 1, :])
            if masked:
                r = lax.broadcasted_iota(jnp.int32, (tq, tq), 0)
                c = lax.broadcasted_iota(jnp.int32, (tq, tq), 1)
                s = jnp.where(c <= r, s, NEG_INF)
            m_prev = m_ref[h]
            m_new = jnp.maximum(m_prev, jnp.max(s, axis=-1, keepdims=True))
            alpha = jnp.exp(m_prev - m_new)
            p = jnp.exp(s - pltpu.repeat(m_new, tq // LANES, axis=1))
            l_ref[h] = alpha * l_ref[h] + jnp.sum(p, axis=-1, keepdims=True)
            acc_ref[h] = alpha * acc_ref[h] + _dot(p.astype(BF16), v_ref[pl.ds(ks, tq), cs])
            m_ref[h] = m_new

    def body(kt, carry):
        step(kt, False)
        return carry

    lax.fori_loop(0, qi, body, 0)
    step(qi, True)
    first = _half_mask(0)
    for pair in range(N_PAIRS):
        h = 2 * pair
        o_ref[:, pair * PAIR_W:(pair + 1) * PAIR_W] = jnp.where(
            first, acc_ref[h] / l_ref[h], acc_ref[h + 1] / l_ref[h + 1])


def _attn_b_kernel(q_ref, k_ref, v_ref, tab_ref, o_ref):
    tq = ATT_TILE
    win = tq + BAND
    i = pl.program_id(1)
    var = jnp.minimum(i, BAND // tq)
    ws = pl.multiple_of(jnp.maximum(i * tq - BAND, 0), tq)
    first = _half_mask(0)
    for pair in range(N_PAIRS):
        cs = slice(pair * PAIR_W, (pair + 1) * PAIR_W)
        q2 = q_ref[:, cs]
        k2 = k_ref[pl.ds(ws, win), cs]
        v2 = v_ref[pl.ds(ws, win), cs]
        res = []
        for half in range(2):
            h = 2 * pair + half
            if h >= H_B:
                res.append(jnp.zeros((tq, PAIR_W), F32))
                continue
            qm = jnp.where(_half_mask(half), q2, jnp.zeros_like(q2))
            s = _dot_nt(qm, k2) + tab_ref[var, h]
            p = jnp.exp(s - jnp.max(s, axis=-1, keepdims=True))
            l = jnp.sum(p, axis=-1, keepdims=True)
            res.append(_dot(p.astype(BF16), v2) / l)
        o_ref[:, cs] = jnp.where(first, res[0], res[1])


def _suffix_tri(n):
    j = lax.broadcasted_iota(jnp.int32, (n, n), 0)
    s = lax.broadcasted_iota(jnp.int32, (n, n), 1)
    return jnp.where(j >= s, 1.0, 0.0).astype(BF16)


def _stick_tile(qm, k2, v2, tri, carry, acc, seen):
    z = _dot_nt(qm, k2)
    sp = _softplus(z)
    if seen is not None:
        sp = jnp.where(seen, sp, 0.0)
    hi, lo = _split2(sp)
    s_in = _dot(hi, tri) + _dot(lo, tri)
    a = jnp.exp(z - (s_in + carry))
    if seen is not None:
        a = jnp.where(seen, a, 0.0)
    acc = acc + _dot(a.astype(BF16), v2)
    return carry + s_in[:, 0:1], acc


def _attn_c_kernel(q_ref, k_ref, v_ref, o_ref, qm_ref, carry_ref, acc_ref):
    tq = ATT_TILE
    qi = pl.program_id(1)
    _masked_q(q_ref, qm_ref, H_C)
    carry_ref[...] = jnp.zeros(carry_ref.shape, F32)
    acc_ref[...] = jnp.zeros(acc_ref.shape, F32)

    def step(kt, diag):
        ks = pl.multiple_of(kt * tq, tq)
        tri = _suffix_tri(tq)
        seen = None
        if diag:
            r = lax.broadcasted_iota(jnp.int32, (tq, tq), 0)
            c = lax.broadcasted_iota(jnp.int32, (tq, tq), 1)
            seen = c < r
        for h in range(H_C):
            cs = _pair_slice(h)
            carry, acc = _stick_tile(qm_ref[h], k_ref[pl.ds(ks, tq), cs], v_ref[pl.ds(ks, tq), cs],
                                     tri, carry_ref[h][:, 0:1], acc_ref[h], seen)
            carry_ref[h] = jnp.broadcast_to(carry, (tq, LANES))
            acc_ref[h] = acc

    def body(j, c):
        step(qi - 1 - j, False)
        return c

    step(qi, True)
    lax.fori_loop(0, qi, body, 0)
    first = _half_mask(0)
    for pair in range(N_PAIRS):
        h = 2 * pair
        second = acc_ref[h + 1] if h + 1 < H_C else jnp.zeros((tq, PAIR_W), F32)
        o_ref[:, pair * PAIR_W:(pair + 1) * PAIR_W] = jnp.where(first, acc_ref[h], second)


def _prompt_attention(kernel, name, q, k, v, extra, extra_specs, scratch, nb, s):
    nq = s // ATT_TILE
    qspec = pl.BlockSpec((ATT_TILE, GROUP_W), lambda b, i: (b * nq + i, 0))
    kvspec = pl.BlockSpec((s, GROUP_W), lambda b, i: (b, 0))
    return pl.pallas_call(
        kernel, grid=(nb, nq),
        in_specs=[qspec, kvspec, kvspec] + extra_specs,
        out_specs=qspec, out_shape=jax.ShapeDtypeStruct((nb * s, GROUP_W), F32),
        scratch_shapes=scratch,
        compiler_params=_cparams("parallel", "arbitrary"), name=name)(q, k, v, *extra)


def _head_scratch(n_heads, n_stats):
    return ([pltpu.VMEM((n_heads, ATT_TILE, PAIR_W), BF16)]
            + [pltpu.VMEM((n_heads, ATT_TILE, LANES), F32)] * n_stats
            + [pltpu.VMEM((n_heads, ATT_TILE, PAIR_W), F32)])


def _pair_cols(pair, width):
    lo = pair * PAIR_W
    return slice(lo, min(lo + PAIR_W, width))


def _sample_heads(n_heads, q_ref, width):
    for pair in range(N_PAIRS):
        cols = _pair_cols(pair, width)
        q2 = q_ref[:, cols]
        for half in range(2):
            h = 2 * pair + half
            if h >= n_heads:
                continue
            if cols.stop - cols.start == PAIR_W:
                qm = jnp.where(_half_mask(half), q2, jnp.zeros_like(q2))
            else:
                qm = q2
            yield pair, half, h, cols, qm


def _store_heads(o_ref, pair, res, width):
    lo = pair * PAIR_W
    if len(res) == 2:
        o_ref[:, lo:lo + PAIR_W] = jnp.where(_half_mask(0), res[0], res[1])
    else:
        o_ref[:, lo:lo + HEAD_DIM] = res[0]
        o_ref[:, lo + HEAD_DIM:lo + PAIR_W] = jnp.zeros_like(res[0])


def _sample_attn_kernel(qa_ref, qb_ref, qc_ref, kan_ref, kbn_ref, kcn_ref, van_ref, vbn_ref, vcn_ref,
                        cak_ref, cav_ref, cbk_ref, cbv_ref, cck_ref, ccv_ref,
                        rsum_ref, logf_ref, tabc_ref, tabn_ref,
                        oa_ref, ob_ref, oc_ref):
    t = qa_ref.shape[0]
    past = cak_ref.shape[1]
    r = lax.broadcasted_iota(jnp.int32, (t, t), 0)
    c = lax.broadcasted_iota(jnp.int32, (t, t), 1)

    ltri = jnp.where(c <= r, 1.0, 0.0).astype(BF16)
    hi, mid, lo = _split3(logf_ref[...])
    pcol = _dot(ltri, hi) + _dot(ltri, mid) + _dot(ltri, lo)
    prow = jnp.transpose(jnp.concatenate([pcol, jnp.zeros((LANES - t, LANES), F32)], axis=0))
    res = []
    for pair, half, h, cols, qm in _sample_heads(H_A, qa_ref, W_A):
        if half == 0:
            res = []
            kc = cak_ref[0, :, cols].astype(BF16)
            vc = cav_ref[0, :, cols].astype(BF16)
        pq = pcol[:, h:h + 1]
        s_c = _dot_nt(qm, kc) + (pq + rsum_ref[0, h:h + 1, :])
        s_n = _dot_nt(qm, kan_ref[:, cols]) + (pq - prow[h:h + 1, 0:t])
        s_n = jnp.where(c <= r, s_n, NEG_INF)
        m = jnp.maximum(jnp.max(s_c, axis=-1, keepdims=True), jnp.max(s_n, axis=-1, keepdims=True))
        p_c = jnp.exp(s_c - m)
        p_n = jnp.exp(s_n - m)
        l = jnp.sum(p_c, axis=-1, keepdims=True) + jnp.sum(p_n, axis=-1, keepdims=True)
        res.append((_dot(p_c.astype(BF16), vc) + _dot(p_n.astype(BF16), van_ref[:, cols])) / l)
        if half == 1:
            _store_heads(oa_ref, pair, res, W_A)

    for pair, half, h, cols, qm in _sample_heads(H_B, qb_ref, W_B):
        if half == 0:
            res = []
            kc = cbk_ref[0, :, cols].astype(BF16)
            vc = cbv_ref[0, :, cols].astype(BF16)
        s_c = _dot_nt(qm, kc) + tabc_ref[h]
        s_n = _dot_nt(qm, kbn_ref[:, cols]) + tabn_ref[h]
        m = jnp.maximum(jnp.max(s_c, axis=-1, keepdims=True), jnp.max(s_n, axis=-1, keepdims=True))
        p_c = jnp.exp(s_c - m)
        p_n = jnp.exp(s_n - m)
        l = jnp.sum(p_c, axis=-1, keepdims=True) + jnp.sum(p_n, axis=-1, keepdims=True)
        res.append((_dot(p_c.astype(BF16), vc) + _dot(p_n.astype(BF16), vbn_ref[:, cols])) / l)
        if half == 1 or h == H_B - 1:
            _store_heads(ob_ref, pair, res, W_B)

    tri_n = _suffix_tri(t)
    tri_c = _suffix_tri(ATT_TILE)
    for pair, half, h, cols, qm in _sample_heads(H_C, qc_ref, W_C):
        if half == 0:
            res = []
        carry, acc = _stick_tile(qm, kcn_ref[:, cols], vcn_ref[:, cols], tri_n,
                                 jnp.zeros((t, 1), F32), jnp.zeros((t, cols.stop - cols.start), F32), c < r)
        for blk in reversed(range(past // ATT_TILE)):
            rows = slice(blk * ATT_TILE, (blk + 1) * ATT_TILE)
            carry, acc = _stick_tile(qm, cck_ref[0, rows, cols].astype(BF16),
                                     ccv_ref[0, rows, cols].astype(BF16), tri_c, carry, acc, None)
        res.append(acc)
        if half == 1 or h == H_C - 1:
            _store_heads(oc_ref, pair, res, W_C)


def _sample_attention(proj, caches, rsum, tabc, tabn, nb, t):
    qa, qb, qc, ka, kb, kc, va, vb, vc = proj[:9]
    logf = proj[15]
    new = pl.BlockSpec((t, GROUP_W), lambda b: (b, 0))
    cache = lambda a: pl.BlockSpec((1,) + a.shape[1:], lambda b: (b, 0, 0))
    full = lambda a: pl.BlockSpec(a.shape, lambda b: (0,) * a.ndim)
    return pl.pallas_call(
        _sample_attn_kernel, grid=(nb,),
        in_specs=[new] * 9 + [cache(a) for a in caches] + [
            cache(rsum), pl.BlockSpec((t, LANES), lambda b: (b, 0)), full(tabc), full(tabn)],
        out_specs=[new] * 3,
        out_shape=[jax.ShapeDtypeStruct((nb * t, GROUP_W), F32)] * 3,
        compiler_params=_cparams("parallel"), name="sample_attn")(
            qa, qb, qc, ka, kb, kc, va, vb, vc, *caches, rsum, logf, tabc, tabn)


def _masked_max(x, mask):
    return jnp.max(jnp.where(mask, x, -jnp.inf), axis=-1, keepdims=True)


def _first_lane(mask, lane):
    return jnp.min(jnp.where(mask, lane, float(LANES)), axis=-1, keepdims=True)


def _route(logits):
    lane = _lane_iota().astype(F32)
    is_g = (lane >= N_EXPERTS) & (lane < N_EXPERTS + N_GROUPS)
    gmax = _masked_max(logits, is_g)
    p_g = 1.0 / jnp.sum(jnp.where(is_g, jnp.exp(logits - gmax), 0.0), axis=-1, keepdims=True)
    g_sel = _first_lane(is_g & (logits == gmax), lane) - N_EXPERTS
    lo = g_sel * EXPERTS_PER_GROUP
    in_g = (lane >= lo) & (lane < lo + EXPERTS_PER_GROUP)
    l1 = _masked_max(logits, in_g)
    i1 = _first_lane(in_g & (logits == l1), lane)
    rest = in_g & (lane != i1)
    l2 = _masked_max(logits, rest)
    i2 = _first_lane(rest & (logits == l2), lane)
    e2 = jnp.exp(l2 - l1)
    w1 = p_g / (1.0 + e2)
    w2 = p_g * e2 / (1.0 + e2)
    return jnp.where(lane == i1, w1, jnp.where(lane == i2, w2, 0.0))


def _post_kernel(oa_ref, ob_ref, oc_ref, h_ref, gmix_ref, wout_ref, nffn_ref, wrh_ref, wrl_ref, br_ref,
                 h1_ref, xn_ref, gates_ref):
    def gnorm(o, width):
        ms = jnp.sum(o * o, axis=-1, keepdims=True) * (1.0 / width)
        return o * lax.rsqrt(ms + EPS)

    h1 = h_ref[...]
    for g, (o_ref, width) in enumerate(((oa_ref, W_A), (ob_ref, W_B), (oc_ref, W_C))):
        merged = (gnorm(o_ref[...], width) * gmix_ref[g]).astype(BF16)
        h1 = h1 + _dot(merged, wout_ref[g])
    h1_ref[...] = h1
    xf = _rms(h1, nffn_ref[...])
    hi, lo = _split2(xf)
    xn_ref[...] = hi
    logits = _dot(hi, wrh_ref[...]) + _dot(lo, wrh_ref[...]) + _dot(hi, wrl_ref[...]) + br_ref[...]
    gates_ref[...] = _route(logits)


def _post(oa, ob, oc, h, lw, tm):
    t = h.shape[0]
    row = lambda w: pl.BlockSpec((tm, w), lambda i: (i, 0))
    full = lambda a: pl.BlockSpec(a.shape, lambda i: (0,) * a.ndim)
    ws = (lw['g_mix'], lw['w_out'], lw['norm_ffn'], lw['w_r_hi'], lw['w_r_lo'], lw['b_r'])
    return pl.pallas_call(
        _post_kernel, grid=(t // tm,),
        in_specs=[row(GROUP_W)] * 3 + [row(D_MODEL)] + [full(a) for a in ws],
        out_specs=[row(D_MODEL), row(D_MODEL), row(LANES)],
        out_shape=[jax.ShapeDtypeStruct((t, D_MODEL), F32), jax.ShapeDtypeStruct((t, D_MODEL), BF16),
                   jax.ShapeDtypeStruct((t, LANES), F32)],
        compiler_params=_cparams("parallel"), name="post")(oa, ob, oc, h, *ws)


def _moe_kernel(xn_ref, gates_ref, wg_ref, wu_ref, wd_ref, h1_ref, p_ref, wple_ref, nple_ref, wpg_ref,
                out_ref, acc_ref):
    e = pl.program_id(1)

    @pl.when(e == 0)
    def _():
        acc_ref[...] = jnp.zeros_like(acc_ref)

    x = xn_ref[...]
    g = _dot(x, wg_ref[0])
    u = _dot(x, wu_ref[0])
    gate = jnp.sum(jnp.where(_lane_iota() == e, gates_ref[...], 0.0), axis=-1, keepdims=True)
    act = (g * _sigmoid(g)) * u * gate
    acc_ref[...] += _dot(act.astype(BF16), wd_ref[0])

    @pl.when(e == N_EXPERTS - 1)
    def _():
        h2 = h1_ref[...] + acc_ref[...]
        gate_ple = _sigmoid(_dot(_rms(h2, nple_ref[...]).astype(BF16), wpg_ref[...]))
        out_ref[...] = h2 + _dot(p_ref[...].astype(BF16), wple_ref[...]) * gate_ple


def _moe(xn, gates, h1, p, lw, tm):
    t = xn.shape[0]
    d_ple = p.shape[1]
    row = lambda w: pl.BlockSpec((tm, w), lambda i, e: (i, 0))
    full = lambda a: pl.BlockSpec(a.shape, lambda i, e: (0,) * a.ndim)
    exp = lambda a: pl.BlockSpec((1,) + a.shape[1:], lambda i, e: (e, 0, 0))
    return pl.pallas_call(
        _moe_kernel, grid=(t // tm, N_EXPERTS),
        in_specs=[row(D_MODEL), row(LANES), exp(lw['w_eg']), exp(lw['w_eu']), exp(lw['w_ed']),
                  row(D_MODEL), row(d_ple), full(lw['w_ple']), full(lw['norm_ple']), full(lw['w_pg'])],
        out_specs=row(D_MODEL), out_shape=jax.ShapeDtypeStruct((t, D_MODEL), F32),
        scratch_shapes=[pltpu.VMEM((tm, D_MODEL), F32)],
        compiler_params=_cparams("parallel", "arbitrary"), name="moe")(
            xn, gates, lw['w_eg'], lw['w_eu'], lw['w_ed'], h1, p, lw['w_ple'], lw['norm_ple'], lw['w_pg'])


def _regroup_cols(w):
    z = jnp.zeros((w.shape[0], HEAD_DIM), w.dtype)
    return jnp.concatenate([w[:, :W_A], w[:, W_A:W_A + W_B], z, w[:, W_A + W_B:], z], axis=1)


def _regroup_rows(w):
    z = jnp.zeros((HEAD_DIM, w.shape[1]), w.dtype)
    return jnp.stack([w[:W_A], jnp.concatenate([w[W_A:W_A + W_B], z], axis=0),
                      jnp.concatenate([w[W_A + W_B:], z], axis=0)])


def _pad_lanes(a, n=LANES):
    return jnp.pad(a, [(0, 0)] * (a.ndim - 1) + [(0, n - a.shape[-1])])


def _toeplitz_bias(rel_bias, off, tq, win):
    length = tq + win
    pad = length + abs(off)
    ext = jnp.flip(jnp.pad(rel_bias, ((0, 0), (pad, pad)), mode='edge'), axis=1)
    s1 = ext.shape[1] - 1 - (off + REL_CLIP + pad)
    v = jnp.concatenate([ext[:, s1:s1 + win], ext[:, s1 - tq:s1]], axis=1)
    flat = jnp.tile(v, (1, tq))[:, :tq * (length - 1)]
    return flat.reshape(-1, tq, length - 1)[:, :, :win]


def _rel_tables(rel_bias, t_new, rows_b):
    tq, win = ATT_TILE, ATT_TILE + BAND
    q = jnp.arange(tq)[:, None]
    tabs = []
    for var in range(BAND // tq + 1):
        rel_k = jnp.arange(win)[None, :] - var * tq
        kch = jnp.floor_divide(rel_k, CHUNK)
        qch = q // CHUNK
        valid = (kch <= qch) & (kch >= qch - BAND_CHUNKS)
        tabs.append(jnp.where(valid[None], _toeplitz_bias(rel_bias, var * tq, tq, win), NEG_INF))
    tab_prompt = jnp.stack(tabs).astype(F32)
    tab_c = _toeplitz_bias(rel_bias, rows_b, t_new, rows_b).astype(F32)
    tab_n = _toeplitz_bias(rel_bias, 0, t_new, t_new).astype(F32)
    return tab_prompt, tab_c, tab_n


def _layer_weights(i, norm_mix, w_in, b_f, g_qa, g_ka, g_qb, g_kb, g_mix_out, w_out, norm_ffn,
                   w_rg, b_rg, w_re, b_re, w_eg, w_eu, w_ed, w_ple, norm_ple, w_pg):
    d_mix = W_A + W_B + W_C
    wi = w_in[i]
    w_big = jnp.concatenate(
        [_regroup_cols(wi[:, j * d_mix:(j + 1) * d_mix]) for j in range(3)]
        + [_pad_lanes(wi[:, 3 * d_mix:])], axis=1).astype(BF16)
    tile6 = lambda g: jnp.tile(g, GROUP_W // HEAD_DIM)
    w_r = _pad_lanes(jnp.concatenate([w_re[i], w_rg[i]], axis=1))
    w_r_hi = w_r.astype(BF16)
    gm = g_mix_out[i]
    zpad = jnp.zeros((HEAD_DIM,), F32)
    g_mix = jnp.stack([gm[:W_A], jnp.concatenate([gm[W_A:W_A + W_B], zpad]),
                       jnp.concatenate([gm[W_A + W_B:], zpad])])[:, None, :]
    return dict(
        norm_mix=norm_mix[i][None], w_in=w_big,
        gq=jnp.concatenate([tile6(g_qa[i]), tile6(g_qb[i])])[None],
        gk=jnp.concatenate([tile6(g_ka[i]), tile6(g_kb[i])])[None],
        b_f=_pad_lanes(b_f[i])[None],
        g_mix=g_mix, w_out=_regroup_rows(w_out[i]).astype(BF16), norm_ffn=norm_ffn[i][None],
        w_r_hi=w_r_hi, w_r_lo=(w_r - w_r_hi.astype(F32)).astype(BF16),
        b_r=_pad_lanes(jnp.concatenate([b_re[i], b_rg[i]]))[None],
        w_eg=w_eg[i].astype(BF16), w_eu=w_eu[i].astype(BF16), w_ed=w_ed[i].astype(BF16),
        w_ple=w_ple[i].astype(BF16), norm_ple=norm_ple[i][None], w_pg=w_pg[i].astype(BF16))


def _token_tile(t, pref):
    return pref if t % pref == 0 else t


def kernel(x_prompt, x_sample, p_prompt, p_sample, cache_a_k, cache_a_v, cache_a_logf, cache_b_k, cache_b_v, cache_c_k, cache_c_v, norm_mix, w_in, b_f, g_qa, g_ka, g_qb, g_kb, rel_bias, g_mix_out, w_out, norm_ffn, w_router_group, b_router_group, w_router_expert, b_router_expert, w_exp_gate, w_exp_up, w_exp_down, w_ple, norm_ple, w_ple_gate):
    nb, s, d = x_prompt.shape
    ns, t_new, _ = x_sample.shape
    depth = w_in.shape[0]
    past = cache_a_k.shape[2]
    rows_b = cache_b_k.shape[2]
    assert d == D_MODEL and s % ATT_TILE == 0 and s >= BAND + ATT_TILE and past % ATT_TILE == 0
    tp, ts = nb * s, ns * t_new
    keep = s - BAND

    hp = x_prompt.reshape(tp, d)
    hs = x_sample.reshape(ts, d)
    lf_rows = jnp.pad(jnp.swapaxes(cache_a_logf, 2, 3), ((0, 0), (0, 0), (0, 8 - H_A), (0, 0)))
    rsum = _suffix_sum(lf_rows.reshape(depth * ns * 8, past)).reshape(depth, ns, 8, past)

    new_p = [[] for _ in range(7)]
    new_s = [[] for _ in range(7)]
    for i in range(depth):
        lw = _layer_weights(i, norm_mix, w_in, b_f, g_qa, g_ka, g_qb, g_kb, g_mix_out, w_out, norm_ffn,
                            w_router_group, b_router_group, w_router_expert, b_router_expert,
                            w_exp_gate, w_exp_up, w_exp_down, w_ple, norm_ple, w_ple_gate)
        tab_p, tab_c, tab_n = _rel_tables(rel_bias[i], t_new, rows_b)

        pr = _inproj(hp, lw, _token_tile(tp, 512))
        qa, qb, qc, ka, kb, kc, va, vb, vc, akf, avf, bkf, bvf, ckf, cvf, logf = pr
        ccol, crow = _cumsum_prompt(logf, nb, s)
        nblk = s // ATT_TILE
        oa = _prompt_attention(
            _attn_a_kernel, "attn_a", qa, ka, va, (ccol, crow),
            [pl.BlockSpec((ATT_TILE, LANES), lambda b, j: (b * nblk + j, 0)),
             pl.BlockSpec((1, nblk, 8, ATT_TILE), lambda b, j: (b, 0, 0, 0))],
            _head_scratch(H_A, 2), nb, s)
        ob = _prompt_attention(
            _attn_b_kernel, "attn_b", qb, kb, vb, (tab_p,),
            [pl.BlockSpec(tab_p.shape, lambda b, j: (0, 0, 0, 0))], [], nb, s)
        oc = _prompt_attention(_attn_c_kernel, "attn_c", qc, kc, vc, (), [], _head_scratch(H_C, 1), nb, s)
        h1, xn, gates = _post(oa, ob, oc, hp, lw, _token_tile(tp, 512))
        hp = _moe(xn, gates, h1, p_prompt[i].reshape(tp, -1), lw, _token_tile(tp, 1024))
        rows = (akf.reshape(nb, s, H_A, HEAD_DIM), avf.reshape(nb, s, H_A, HEAD_DIM),
                logf[:, :H_A].reshape(nb, s, H_A),
                bkf.reshape(nb, s, H_B, HEAD_DIM)[:, keep:], bvf.reshape(nb, s, H_B, HEAD_DIM)[:, keep:],
                ckf.reshape(nb, s, H_C, HEAD_DIM), cvf.reshape(nb, s, H_C, HEAD_DIM))
        for j in range(7):
            new_p[j].append(rows[j])

        sr = _inproj(hs, lw, _token_tile(ts, 512))
        caches = (cache_a_k[i].reshape(ns, past, W_A), cache_a_v[i].reshape(ns, past, W_A),
                  cache_b_k[i].reshape(ns, rows_b, W_B), cache_b_v[i].reshape(ns, rows_b, W_B),
                  cache_c_k[i].reshape(ns, past, W_C), cache_c_v[i].reshape(ns, past, W_C))
        oa, ob, oc = _sample_attention(sr, caches, rsum[i], tab_c, tab_n, ns, t_new)
        h1, xn, gates = _post(oa, ob, oc, hs, lw, _token_tile(ts, 512))
        hs = _moe(xn, gates, h1, p_sample[i].reshape(ts, -1), lw, _token_tile(ts, 1024))
        akf, avf, bkf, bvf, ckf, cvf, logf = sr[9:]
        rows = (akf.reshape(ns, t_new, H_A, HEAD_DIM), avf.reshape(ns, t_new, H_A, HEAD_DIM),
                logf[:, :H_A].reshape(ns, t_new, H_A),
                bkf.reshape(ns, t_new, H_B, HEAD_DIM), bvf.reshape(ns, t_new, H_B, HEAD_DIM),
                ckf.reshape(ns, t_new, H_C, HEAD_DIM), cvf.reshape(ns, t_new, H_C, HEAD_DIM))
        for j in range(7):
            new_s[j].append(rows[j])

    outs_p = [jnp.stack(r, axis=0) for r in new_p]
    outs_s = [jnp.stack(r, axis=0) for r in new_s]
    return (hp.reshape(nb, s, d), hs.reshape(ns, t_new, d), *outs_p, *outs_s)
```

```python
import functools

import jax
import jax.numpy as jnp
from jax import lax
from jax.experimental import pallas as pl
from jax.experimental.pallas import tpu as pltpu

F32 = jnp.float32
BF16 = jnp.bfloat16

D_MODEL = 1024
HEAD_DIM = 64
H_A, H_B, H_C = 6, 5, 5
W_A, W_B, W_C = H_A * HEAD_DIM, H_B * HEAD_DIM, H_C * HEAD_DIM
GROUP_W = 384
PAIR_W = 2 * HEAD_DIM
N_PAIRS = GROUP_W // PAIR_W
CHUNK = 64
BAND_CHUNKS = 8
BAND = BAND_CHUNKS * CHUNK
REL_CLIP = 128
N_GROUPS = 4
EXPERTS_PER_GROUP = 4
N_EXPERTS = N_GROUPS * EXPERTS_PER_GROUP
EPS = 1e-6
ATTN_SCALE = HEAD_DIM ** -0.5
NEG_INF = -1e30
LANES = 128
SUBLANES = 8
VMEM_LIMIT = 48 * 1024 * 1024

ATT_TILE = 256
PROJ_TILE = BAND
AUG_SLOTS = 8

_NT = (((1,), (1,)), ((), ()))


def _cparams(*sem):
    return pltpu.CompilerParams(dimension_semantics=sem, vmem_limit_bytes=VMEM_LIMIT)


def _dot(a, b):
    return jnp.dot(a, b, preferred_element_type=F32)


def _dot_nt(a, b):
    return lax.dot_general(a, b, _NT, preferred_element_type=F32)


def _split3(x):
    hi = x.astype(BF16)
    r1 = x - hi.astype(F32)
    mid = r1.astype(BF16)
    lo = (r1 - mid.astype(F32)).astype(BF16)
    return hi, mid, lo


def _split2(x):
    hi = x.astype(BF16)
    lo = (x - hi.astype(F32)).astype(BF16)
    return hi, lo


def _lane_iota(n=LANES):
    return lax.broadcasted_iota(jnp.int32, (1, n), 1)


def _half_mask(half):
    lane = _lane_iota()
    return (lane < HEAD_DIM) if half == 0 else (lane >= HEAD_DIM)


def _pair_slice(h):
    return slice((h // 2) * PAIR_W, (h // 2 + 1) * PAIR_W)


def _softplus(z):
    return jnp.maximum(z, 0.0) + jnp.log(1.0 + jnp.exp(-jnp.abs(z)))


def _sigmoid(z):
    return 1.0 / (1.0 + jnp.exp(-z))


def _rms(x, g):
    return x * lax.rsqrt(jnp.mean(x * x, axis=-1, keepdims=True) + EPS) * g


def _pair_rms(z, gain):
    first = _half_mask(0)
    outs = []
    for j in range(N_PAIRS):
        blk = z[:, j * PAIR_W:(j + 1) * PAIR_W]
        sq = blk * blk
        lo = jnp.sum(jnp.where(first, sq, 0.0), axis=-1, keepdims=True)
        hi = jnp.sum(jnp.where(first, 0.0, sq), axis=-1, keepdims=True)
        ms = jnp.where(first, lo, hi) * (1.0 / HEAD_DIM)
        outs.append(blk * lax.rsqrt(ms + EPS) * gain[:, j * PAIR_W:(j + 1) * PAIR_W])
    return jnp.concatenate(outs, axis=-1)


def _projector(x_ref, gn_ref, w_ref):
    xb = _rms(x_ref[...], gn_ref[...]).astype(BF16)
    return lambda g, width=GROUP_W: _dot(xb, w_ref[:, g * GROUP_W:g * GROUP_W + width])


def _log_forget(zf, bf_ref):
    zf = zf + bf_ref[...]
    return jnp.minimum(zf, 0.0) - jnp.log(1.0 + jnp.exp(-jnp.abs(zf)))


def _inproj_sample_kernel(x_ref, gn_ref, w_ref, gq_ref, gk_ref, bf_ref,
                          qa_ref, qb_ref, qc_ref, ka_ref, kb_ref, kc_ref, va_ref, vb_ref, vc_ref,
                          akf_ref, avf_ref, bkf_ref, bvf_ref, ckf_ref, cvf_ref, logf_ref):
    proj = _projector(x_ref, gn_ref, w_ref)
    qa_ref[...] = (_pair_rms(proj(0), gq_ref[:, :GROUP_W]) * ATTN_SCALE).astype(BF16)
    qb_ref[...] = (_pair_rms(proj(1), gq_ref[:, GROUP_W:]) * ATTN_SCALE).astype(BF16)
    qc_ref[...] = (proj(2) * ATTN_SCALE).astype(BF16)
    for g, norm, bf_ref_, f_ref, width in (
            (3, gk_ref[:, :GROUP_W], ka_ref, akf_ref, W_A), (4, gk_ref[:, GROUP_W:], kb_ref, bkf_ref, W_B),
            (5, None, kc_ref, ckf_ref, W_C), (6, None, va_ref, avf_ref, W_A),
            (7, None, vb_ref, bvf_ref, W_B), (8, None, vc_ref, cvf_ref, W_C)):
        z = proj(g)
        if norm is not None:
            z = _pair_rms(z, norm)
        f_ref[...] = z[:, :width]
        bf_ref_[...] = z.astype(BF16)
    logf_ref[...] = _log_forget(proj(9, LANES), bf_ref)


def _inproj_sample(x, lw):
    t = x.shape[0]
    row = lambda w: pl.BlockSpec((t, w), lambda i: (0, 0))
    full = lambda a: pl.BlockSpec(a.shape, lambda i: (0,) * a.ndim)
    widths = [GROUP_W] * 9 + [W_A, W_A, W_B, W_B, W_C, W_C, LANES]
    out_shape = [jax.ShapeDtypeStruct((t, w), BF16 if j < 9 else F32) for j, w in enumerate(widths)]
    ins = (x, lw['norm_mix'], lw['w_in'], lw['gq'], lw['gk'], lw['b_f'])
    return pl.pallas_call(
        _inproj_sample_kernel, grid=(1,),
        in_specs=[row(D_MODEL)] + [full(a) for a in ins[1:]],
        out_specs=[row(w) for w in widths], out_shape=out_shape,
        compiler_params=_cparams("arbitrary"), name="inproj_sample")(*ins)


N_PROMPT_ROWS = 7


def _inproj_prompt_kernel(*refs, n_alias):
    x_ref, gn_ref, w_ref, gq_ref, gk_ref, bf_ref = refs[:6]
    (qat_ref, ka_ref, vat_ref, qb_ref, kb_ref, vb_ref, qc_ref, kc_ref, vc_ref, logf_ref,
     akt_ref, avt_ref, bkt_ref, bvt_ref, ckt_ref, cvt_ref, lft_ref) = refs[6 + n_alias:]
    proj = _projector(x_ref, gn_ref, w_ref)
    qat_ref[...] = jnp.transpose(_pair_rms(proj(0), gq_ref[:, :GROUP_W]) * ATTN_SCALE).astype(BF16)
    qb_ref[...] = (_pair_rms(proj(1), gq_ref[:, GROUP_W:]) * ATTN_SCALE).astype(BF16)
    qc_ref[...] = (proj(2) * ATTN_SCALE).astype(BF16)
    ka = _pair_rms(proj(3), gk_ref[:, :GROUP_W])
    ka_ref[...] = ka.astype(BF16)
    akt_ref[0, 0] = jnp.transpose(ka)
    kb = _pair_rms(proj(4), gk_ref[:, GROUP_W:])
    kb_ref[...] = kb.astype(BF16)
    kc = proj(5)
    kc_ref[...] = kc.astype(BF16)
    ckt_ref[0, 0] = jnp.transpose(kc)[:W_C]
    vat = jnp.transpose(proj(6))
    avt_ref[0, 0] = vat
    for j in range(vat_ref.shape[1]):
        vat_ref[0, j] = vat[:, j * ATT_TILE:(j + 1) * ATT_TILE].astype(BF16)
    vb = proj(7)
    vb_ref[...] = vb.astype(BF16)
    vc = proj(8)
    vc_ref[...] = vc.astype(BF16)
    cvt_ref[0, 0] = jnp.transpose(vc)[:W_C]

    @pl.when(pl.program_id(1) == pl.num_programs(1) - 1)
    def _():
        bkt_ref[0, 0] = jnp.transpose(kb)[:W_B]
        bvt_ref[0, 0] = jnp.transpose(vb)[:W_B]

    lf = _log_forget(proj(9, LANES), bf_ref)
    logf_ref[...] = lf
    lft_ref[0] = jnp.transpose(lf)[:H_A]


def _inproj_prompt(x, lw, layer, depth, nb, s, prev_rows):
    tm = PROJ_TILE
    nt = s // tm
    t = nb * s
    per = tm // ATT_TILE
    row = lambda w: pl.BlockSpec((tm, w), lambda b, i: (b * nt + i, 0))
    full = lambda a: pl.BlockSpec(a.shape, lambda b, i: (0,) * a.ndim)
    feat = lambda w: pl.BlockSpec((1, 1, w, tm), lambda b, i: (layer, b, 0, i))
    band = lambda w: pl.BlockSpec((1, 1, w, BAND), lambda b, i: (layer, b, 0, 0))
    tok = jax.ShapeDtypeStruct((t, GROUP_W), BF16)
    out_shape = [
        jax.ShapeDtypeStruct((GROUP_W, t), BF16), tok,
        jax.ShapeDtypeStruct((nb, s // ATT_TILE, GROUP_W, ATT_TILE), BF16),
        tok, tok, tok, tok, tok, tok, jax.ShapeDtypeStruct((t, LANES), F32),
        jax.ShapeDtypeStruct((depth, nb, W_A, s), F32), jax.ShapeDtypeStruct((depth, nb, W_A, s), F32),
        jax.ShapeDtypeStruct((depth, nb, W_B, BAND), F32), jax.ShapeDtypeStruct((depth, nb, W_B, BAND), F32),
        jax.ShapeDtypeStruct((depth, nb, W_C, s), F32), jax.ShapeDtypeStruct((depth, nb, W_C, s), F32),
        jax.ShapeDtypeStruct((depth, H_A, t), F32)]
    out_specs = [
        pl.BlockSpec((GROUP_W, tm), lambda b, i: (0, b * nt + i)), row(GROUP_W),
        pl.BlockSpec((1, per, GROUP_W, ATT_TILE), lambda b, i: (b, i, 0, 0)),
        row(GROUP_W), row(GROUP_W), row(GROUP_W), row(GROUP_W), row(GROUP_W), row(GROUP_W), row(LANES),
        feat(W_A), feat(W_A), band(W_B), band(W_B), feat(W_C), feat(W_C),
        pl.BlockSpec((1, H_A, tm), lambda b, i: (layer, 0, b * nt + i))]
    ins = [x, lw['norm_mix'], lw['w_in'], lw['gq'], lw['gk'], lw['b_f']]
    in_specs = [row(D_MODEL)] + [full(a) for a in ins[1:]]
    aliases = {}
    n_alias = 0
    if prev_rows is not None:
        n_alias = N_PROMPT_ROWS
        first_row_out = len(out_shape) - N_PROMPT_ROWS
        for j, a in enumerate(prev_rows):
            aliases[len(ins)] = first_row_out + j
            ins.append(a)
            in_specs.append(pl.BlockSpec(memory_space=pl.ANY))
    return pl.pallas_call(
        functools.partial(_inproj_prompt_kernel, n_alias=n_alias), grid=(nb, nt),
        in_specs=in_specs, out_specs=out_specs, out_shape=out_shape,
        input_output_aliases=aliases,
        compiler_params=_cparams("parallel", "arbitrary"), name="inproj_prompt")(*ins)


def _forget_aug_kernel(logf_ref, kaug_ref, qaug_ref, *, nblk):
    tb = ATT_TILE
    r = lax.broadcasted_iota(jnp.int32, (tb, tb), 0)
    c = lax.broadcasted_iota(jnp.int32, (tb, tb), 1)
    tri = jnp.where(c <= r, 1.0, 0.0).astype(BF16)
    src = lax.broadcasted_iota(jnp.int32, (LANES, LANES), 0)
    dst = lax.broadcasted_iota(jnp.int32, (LANES, LANES), 1)

    def place(slot):
        return jnp.where((dst == AUG_SLOTS * src + slot) & (src < H_A), 1.0, 0.0).astype(BF16)

    lane = _lane_iota()
    slot = jnp.bitwise_and(lane, AUG_SLOTS - 1)
    used = lane < AUG_SLOTS * H_A
    key_const = jnp.where(used & (slot >= 3) & (slot < 6), 1.0, 0.0)
    qry_const = jnp.where(used & (slot < 3), -1.0, 0.0)
    carry = jnp.zeros((1, LANES), F32)
    for blk in range(nblk):
        rows = slice(blk * tb, (blk + 1) * tb)
        hi, mid, lo = _split3(logf_ref[rows, :])
        cs = (_dot(tri, hi) + _dot(tri, mid) + _dot(tri, lo)) + carry
        carry = cs[tb - 1:tb, :]
        hi, mid, lo = _split3(cs)
        kaug = _dot(hi, place(0)) + _dot(mid, place(1)) + _dot(lo, place(2)) + key_const
        kaug_ref[rows, :] = kaug.astype(BF16)
        qaug = _dot(hi, place(3)) + _dot(mid, place(4)) + _dot(lo, place(5)) + qry_const
        qaug_ref[:, rows] = jnp.transpose(qaug).astype(BF16)


def _forget_aug(logf, nb, s):
    return pl.pallas_call(
        functools.partial(_forget_aug_kernel, nblk=s // ATT_TILE), grid=(nb,),
        in_specs=[pl.BlockSpec((s, LANES), lambda b: (b, 0))],
        out_specs=[pl.BlockSpec((s, LANES), lambda b: (b, 0)), pl.BlockSpec((LANES, s), lambda b: (0, b))],
        out_shape=[jax.ShapeDtypeStruct((nb * s, LANES), BF16), jax.ShapeDtypeStruct((LANES, nb * s), BF16)],
        compiler_params=_cparams("parallel"), name="forget_aug")(logf)


def _suffix_sum_kernel(x_ref, o_ref):
    p = x_ref.shape[1]
    j = lax.broadcasted_iota(jnp.int32, (p, p), 0)
    s = lax.broadcasted_iota(jnp.int32, (p, p), 1)
    tri = jnp.where(j > s, 1.0, 0.0).astype(BF16)
    hi, mid, lo = _split3(x_ref[...])
    o_ref[...] = _dot(hi, tri) + _dot(mid, tri) + _dot(lo, tri)


def _suffix_sum(x):
    return pl.pallas_call(
        _suffix_sum_kernel, out_shape=jax.ShapeDtypeStruct(x.shape, F32),
        compiler_params=pltpu.CompilerParams(vmem_limit_bytes=VMEM_LIMIT), name="suffix_sum")(x)


def _attn_a_kernel(qt_ref, k_ref, vt_ref, kaug_ref, qaug_ref, o_ref, qp_ref, m_ref, l_ref, acc_ref):
    tq = ATT_TILE
    qi = pl.program_id(1)
    row = lax.broadcasted_iota(jnp.int32, (PAIR_W, 1), 0)
    aug = qaug_ref[...]
    for h in range(H_A):
        q2 = qt_ref[_pair_slice(h), :]
        half = (row < HEAD_DIM) if h % 2 == 0 else (row >= HEAD_DIM)
        qp_ref[h, :PAIR_W, :] = jnp.where(half, q2, jnp.zeros_like(q2))
        mine = (row >= AUG_SLOTS * h) & (row < AUG_SLOTS * (h + 1))
        qp_ref[h, PAIR_W:, :] = jnp.where(mine, aug, jnp.zeros_like(aug))
    m_ref[...] = jnp.full(m_ref.shape, NEG_INF, F32)
    l_ref[...] = jnp.zeros(l_ref.shape, F32)
    acc_ref[...] = jnp.zeros(acc_ref.shape, F32)

    def step(kt, masked):
        ks = pl.multiple_of(kt * tq, tq)
        ka = kaug_ref[pl.ds(ks, tq), :]
        scores = [_dot(jnp.concatenate([k_ref[pl.ds(ks, tq), _pair_slice(h)], ka], axis=1), qp_ref[h])
                  for h in range(H_A)]
        probs, alphas = [], []
        for h in range(H_A):
            s = scores[h]
            if masked:
                key = lax.broadcasted_iota(jnp.int32, (tq, tq), 0)
                qry = lax.broadcasted_iota(jnp.int32, (tq, tq), 1)
                s = jnp.where(key <= qry, s, NEG_INF)
            m_prev = m_ref[h]
            m_new = jnp.maximum(m_prev, jnp.max(s, axis=0, keepdims=True))
            alpha = jnp.exp(m_prev - m_new)
            p = jnp.exp(s - m_new)
            l_ref[h] = alpha * l_ref[h] + jnp.sum(p, axis=0, keepdims=True)
            m_ref[h] = m_new
            alphas.append(alpha)
            probs.append(p.astype(BF16))
        for h in range(H_A):
            acc_ref[h] = alphas[h] * acc_ref[h] + _dot(vt_ref[0, kt, _pair_slice(h), :], probs[h])

    def body(kt, carry):
        step(kt, False)
        return carry

    lax.fori_loop(0, qi, body, 0)
    step(qi, True)
    for pair in range(N_PAIRS):
        h = 2 * pair
        out_t = jnp.where(row < HEAD_DIM, acc_ref[h] * (1.0 / l_ref[h]), acc_ref[h + 1] * (1.0 / l_ref[h + 1]))
        o_ref[:, pair * PAIR_W:(pair + 1) * PAIR_W] = jnp.transpose(out_t)


def _attn_a(qat, ka, vat, kaug, qaug, nb, s):
    nq = s // ATT_TILE
    tq = ATT_TILE
    return pl.pallas_call(
        _attn_a_kernel, grid=(nb, nq),
        in_specs=[pl.BlockSpec((GROUP_W, tq), lambda b, i: (0, b * nq + i)),
                  pl.BlockSpec((s, GROUP_W), lambda b, i: (b, 0)),
                  pl.BlockSpec((1, nq, GROUP_W, tq), lambda b, i: (b, 0, 0, 0)),
                  pl.BlockSpec((s, LANES), lambda b, i: (b, 0)),
                  pl.BlockSpec((LANES, tq), lambda b, i: (0, b * nq + i))],
        out_specs=pl.BlockSpec((tq, GROUP_W), lambda b, i: (b * nq + i, 0)),
        out_shape=jax.ShapeDtypeStruct((nb * s, GROUP_W), F32),
        scratch_shapes=[pltpu.VMEM((H_A, 2 * PAIR_W, tq), BF16), pltpu.VMEM((H_A, 1, tq), F32),
                        pltpu.VMEM((H_A, 1, tq), F32), pltpu.VMEM((H_A, PAIR_W, tq), F32)],
        compiler_params=_cparams("parallel", "arbitrary"), name="attn_a")(qat, ka, vat, kaug, qaug)


def _masked_q(q_ref, qm_ref, n_heads):
    for h in range(n_heads):
        q2 = q_ref[:, _pair_slice(h)]
        qm_ref[h] = jnp.where(_half_mask(h % 2), q2, jnp.zeros_like(q2))


def _attn_b_kernel(q_ref, k_ref, v_ref, tab_ref, o_ref):
    tq = ATT_TILE
    win = tq + BAND
    i = pl.program_id(1)
    var = jnp.minimum(i, BAND // tq)
    ws = pl.multiple_of(jnp.maximum(i * tq - BAND, 0), tq)
    first = _half_mask(0)
    for pair in range(N_PAIRS):
        cs = slice(pair * PAIR_W, (pair + 1) * PAIR_W)
        q2 = q_ref[:, cs]
        k2 = k_ref[pl.ds(ws, win), cs]
        v2 = v_ref[pl.ds(ws, win), cs]
        res = []
        for half in range(2):
            h = 2 * pair + half
            if h >= H_B:
                res.append(jnp.zeros((tq, PAIR_W), F32))
                continue
            qm = jnp.where(_half_mask(half), q2, jnp.zeros_like(q2))
            s = _dot_nt(qm, k2) + tab_ref[var, h]
            p = jnp.exp(s - jnp.max(s, axis=-1, keepdims=True))
            l = jnp.sum(p, axis=-1, keepdims=True)
            res.append(_dot(p.astype(BF16), v2) / l)
        o_ref[:, cs] = jnp.where(first, res[0], res[1])


def _suffix_tri(n):
    j = lax.broadcasted_iota(jnp.int32, (n, n), 0)
    s = lax.broadcasted_iota(jnp.int32, (n, n), 1)
    return jnp.where(j >= s, 1.0, 0.0).astype(BF16)


def _stick_scores(z, tri, seen):
    sp = _softplus(z)
    if seen is not None:
        sp = jnp.where(seen, sp, 0.0)
    hi, lo = _split2(sp)
    return _dot(hi, tri) + _dot(lo, tri)


def _stick_weights(z, s_in, carry, seen):
    a = jnp.exp(z - (s_in + carry))
    if seen is not None:
        a = jnp.where(seen, a, 0.0)
    return a.astype(BF16)


def _attn_c_kernel(q_ref, k_ref, v_ref, o_ref, qm_ref, carry_ref, acc_ref):
    tq = ATT_TILE
    qi = pl.program_id(1)
    _masked_q(q_ref, qm_ref, H_C)
    carry_ref[...] = jnp.zeros(carry_ref.shape, F32)
    acc_ref[...] = jnp.zeros(acc_ref.shape, F32)

    def step(kt, diag):
        ks = pl.multiple_of(kt * tq, tq)
        tri = _suffix_tri(tq)
        seen = None
        if diag:
            r = lax.broadcasted_iota(jnp.int32, (tq, tq), 0)
            c = lax.broadcasted_iota(jnp.int32, (tq, tq), 1)
            seen = c < r
        heads = range(H_C)
        zs = [_dot_nt(qm_ref[h], k_ref[pl.ds(ks, tq), _pair_slice(h)]) for h in heads]
        sums = [_stick_scores(zs[h], tri, seen) for h in heads]
        weights = []
        for h in heads:
            carry = carry_ref[h][:, 0:1]
            weights.append(_stick_weights(zs[h], sums[h], carry, seen))
            carry_ref[h] = jnp.broadcast_to(carry + sums[h][:, 0:1], (tq, LANES))
        for h in heads:
            acc_ref[h] += _dot(weights[h], v_ref[pl.ds(ks, tq), _pair_slice(h)])

    def body(j, c):
        step(qi - 1 - j, False)
        return c

    step(qi, True)
    lax.fori_loop(0, qi, body, 0)
    first = _half_mask(0)
    for pair in range(N_PAIRS):
        h = 2 * pair
        second = acc_ref[h + 1] if h + 1 < H_C else jnp.zeros((tq, PAIR_W), F32)
        o_ref[:, pair * PAIR_W:(pair + 1) * PAIR_W] = jnp.where(first, acc_ref[h], second)


def _prompt_attention(kernel, name, q, k, v, extra, extra_specs, scratch, nb, s):
    nq = s // ATT_TILE
    qspec = pl.BlockSpec((ATT_TILE, GROUP_W), lambda b, i: (b * nq + i, 0))
    kvspec = pl.BlockSpec((s, GROUP_W), lambda b, i: (b, 0))
    return pl.pallas_call(
        kernel, grid=(nb, nq),
        in_specs=[qspec, kvspec, kvspec] + extra_specs,
        out_specs=qspec, out_shape=jax.ShapeDtypeStruct((nb * s, GROUP_W), F32),
        scratch_shapes=scratch,
        compiler_params=_cparams("parallel", "arbitrary"), name=name)(q, k, v, *extra)


def _sample_heads(n_heads, q_ref, width):
    out = []
    for h in range(n_heads):
        lo = (h // 2) * PAIR_W
        span = slice(lo, min(lo + PAIR_W, width))
        q2 = q_ref[:, span]
        if span.stop - span.start == PAIR_W:
            q2 = jnp.where(_half_mask(h % 2), q2, jnp.zeros_like(q2))
        out.append((h, span, q2))
    return out


def _store_heads(o_ref, res, n_heads):
    for pair in range(N_PAIRS):
        lo = pair * PAIR_W
        h = 2 * pair
        if h + 1 < n_heads:
            o_ref[:, lo:lo + PAIR_W] = jnp.where(_half_mask(0), res[h], res[h + 1])
        else:
            o_ref[:, lo:lo + HEAD_DIM] = res[h]
            o_ref[:, lo + HEAD_DIM:lo + PAIR_W] = jnp.zeros_like(res[h])


def _joint_softmax_pv(s_c, s_n, vt_c, v_n):
    m = jnp.maximum(jnp.max(s_c, axis=-1, keepdims=True), jnp.max(s_n, axis=-1, keepdims=True))
    p_c = jnp.exp(s_c - m)
    p_n = jnp.exp(s_n - m)
    l = jnp.sum(p_c, axis=-1, keepdims=True) + jnp.sum(p_n, axis=-1, keepdims=True)
    return (_dot_nt(p_c.astype(BF16), vt_c) + _dot(p_n.astype(BF16), v_n)) / l


def _sample_attn_kernel(qa_ref, qb_ref, qc_ref, kan_ref, kbn_ref, kcn_ref, van_ref, vbn_ref, vcn_ref,
                        cak_ref, cav_ref, cbk_ref, cbv_ref, cck_ref, ccv_ref,
                        rsum_ref, logf_ref, tabc_ref, tabn_ref,
                        oa_ref, ob_ref, oc_ref):
    t = qa_ref.shape[0]
    past = cak_ref.shape[3]
    r = lax.broadcasted_iota(jnp.int32, (t, t), 0)
    c = lax.broadcasted_iota(jnp.int32, (t, t), 1)

    ltri = jnp.where(c <= r, 1.0, 0.0).astype(BF16)
    hi, mid, lo = _split3(logf_ref[...])
    pcol = _dot(ltri, hi) + _dot(ltri, mid) + _dot(ltri, lo)
    prow = jnp.transpose(jnp.concatenate([pcol, jnp.zeros((LANES - t, LANES), F32)], axis=0))
    heads = _sample_heads(H_A, qa_ref, W_A)
    kts = {sp.start: cak_ref[0, 0, sp, :].astype(BF16) for _, sp, _ in heads}
    vts = {sp.start: cav_ref[0, 0, sp, :].astype(BF16) for _, sp, _ in heads}
    s_c = [_dot(qm, kts[sp.start]) + (pcol[:, h:h + 1] + rsum_ref[0, 0, h:h + 1, :]) for h, sp, qm in heads]
    s_n = [jnp.where(c <= r, _dot_nt(qm, kan_ref[:, sp]) + (pcol[:, h:h + 1] - prow[h:h + 1, 0:t]), NEG_INF)
           for h, sp, qm in heads]
    _store_heads(oa_ref, [_joint_softmax_pv(s_c[h], s_n[h], vts[sp.start], van_ref[:, sp])
                          for h, sp, _ in heads], H_A)

    heads = _sample_heads(H_B, qb_ref, W_B)
    kts = {sp.start: cbk_ref[0, 0, sp, :].astype(BF16) for _, sp, _ in heads}
    vts = {sp.start: cbv_ref[0, 0, sp, :].astype(BF16) for _, sp, _ in heads}
    s_c = [_dot(qm, kts[sp.start]) + tabc_ref[h] for h, sp, qm in heads]
    s_n = [_dot_nt(qm, kbn_ref[:, sp]) + tabn_ref[h] for h, sp, qm in heads]
    _store_heads(ob_ref, [_joint_softmax_pv(s_c[h], s_n[h], vts[sp.start], vbn_ref[:, sp])
                          for h, sp, _ in heads], H_B)

    heads = _sample_heads(H_C, qc_ref, W_C)
    nblk = past // ATT_TILE
    tri_n = _suffix_tri(t)
    tri_c = _suffix_tri(ATT_TILE)
    seen = c < r
    kts = {sp.start: cck_ref[0, 0, sp, :].astype(BF16) for _, sp, _ in heads}
    vts = {sp.start: ccv_ref[0, 0, sp, :].astype(BF16) for _, sp, _ in heads}
    z_n = [_dot_nt(qm, kcn_ref[:, sp]) for _, sp, qm in heads]
    z_c = [_dot(qm, kts[sp.start]) for _, sp, qm in heads]
    sum_n = [_stick_scores(z, tri_n, seen) for z in z_n]
    sum_c = [[_stick_scores(z[:, b * ATT_TILE:(b + 1) * ATT_TILE], tri_c, None) for b in range(nblk)]
             for z in z_c]
    res = []
    for h, sp, _ in heads:
        acc = _dot(_stick_weights(z_n[h], sum_n[h], 0.0, seen), vcn_ref[:, sp])
        carry = sum_n[h][:, 0:1]
        for b in reversed(range(nblk)):
            cols = slice(b * ATT_TILE, (b + 1) * ATT_TILE)
            a = _stick_weights(z_c[h][:, cols], sum_c[h][b], carry, None)
            acc = acc + _dot_nt(a, vts[sp.start][:, cols])
            carry = carry + sum_c[h][b][:, 0:1]
        res.append(acc)
    _store_heads(oc_ref, res, H_C)


def _sample_attention(proj, caches, layer, rsum, tabc, tabn, nb, t):
    qa, qb, qc, ka, kb, kc, va, vb, vc = proj[:9]
    logf = proj[15]
    new = pl.BlockSpec((t, GROUP_W), lambda b: (b, 0))
    cache = lambda a: pl.BlockSpec((1, 1) + a.shape[2:], lambda b: (layer, b, 0, 0))
    full = lambda a: pl.BlockSpec(a.shape, lambda b: (0,) * a.ndim)
    return pl.pallas_call(
        _sample_attn_kernel, grid=(nb,),
        in_specs=[new] * 9 + [cache(a) for a in caches] + [
            cache(rsum), pl.BlockSpec((t, LANES), lambda b: (b, 0)), full(tabc), full(tabn)],
        out_specs=[new] * 3,
        out_shape=[jax.ShapeDtypeStruct((nb * t, GROUP_W), F32)] * 3,
        compiler_params=_cparams("parallel"), name="sample_attn")(
            qa, qb, qc, ka, kb, kc, va, vb, vc, *caches, rsum, logf, tabc, tabn)


def _masked_max(x, mask):
    return jnp.max(jnp.where(mask, x, -jnp.inf), axis=-1, keepdims=True)


def _first_lane(mask, lane):
    return jnp.min(jnp.where(mask, lane, float(LANES)), axis=-1, keepdims=True)


def _route(logits):
    lane = _lane_iota().astype(F32)
    is_g = (lane >= N_EXPERTS) & (lane < N_EXPERTS + N_GROUPS)
    gmax = _masked_max(logits, is_g)
    p_g = 1.0 / jnp.sum(jnp.where(is_g, jnp.exp(logits - gmax), 0.0), axis=-1, keepdims=True)
    g_sel = _first_lane(is_g & (logits == gmax), lane) - N_EXPERTS
    lo = g_sel * EXPERTS_PER_GROUP
    in_g = (lane >= lo) & (lane < lo + EXPERTS_PER_GROUP)
    l1 = _masked_max(logits, in_g)
    i1 = _first_lane(in_g & (logits == l1), lane)
    rest = in_g & (lane != i1)
    l2 = _masked_max(logits, rest)
    i2 = _first_lane(rest & (logits == l2), lane)
    e2 = jnp.exp(l2 - l1)
    w1 = p_g / (1.0 + e2)
    w2 = p_g * e2 / (1.0 + e2)
    return jnp.where(lane == i1, w1, jnp.where(lane == i2, w2, 0.0))


def _post_kernel(oa_ref, ob_ref, oc_ref, h_ref, gmix_ref, wout_ref, nffn_ref, wrh_ref, wrl_ref, br_ref,
                 h1_ref, xn_ref, gates_ref):
    def gnorm(o, width):
        ms = jnp.sum(o * o, axis=-1, keepdims=True) * (1.0 / width)
        return o * lax.rsqrt(ms + EPS)

    h1 = h_ref[...]
    for g, (o_ref, width) in enumerate(((oa_ref, W_A), (ob_ref, W_B), (oc_ref, W_C))):
        merged = (gnorm(o_ref[...], width) * gmix_ref[g]).astype(BF16)
        h1 = h1 + _dot(merged, wout_ref[g])
    h1_ref[...] = h1
    xf = _rms(h1, nffn_ref[...])
    hi, lo = _split2(xf)
    xn_ref[...] = hi
    logits = _dot(hi, wrh_ref[...]) + _dot(lo, wrh_ref[...]) + _dot(hi, wrl_ref[...]) + br_ref[...]
    gates_ref[...] = _route(logits)


def _post(oa, ob, oc, h, lw, tm):
    t = h.shape[0]
    row = lambda w: pl.BlockSpec((tm, w), lambda i: (i, 0))
    full = lambda a: pl.BlockSpec(a.shape, lambda i: (0,) * a.ndim)
    ws = (lw['g_mix'], lw['w_out'], lw['norm_ffn'], lw['w_r_hi'], lw['w_r_lo'], lw['b_r'])
    return pl.pallas_call(
        _post_kernel, grid=(t // tm,),
        in_specs=[row(GROUP_W)] * 3 + [row(D_MODEL)] + [full(a) for a in ws],
        out_specs=[row(D_MODEL), row(D_MODEL), row(LANES)],
        out_shape=[jax.ShapeDtypeStruct((t, D_MODEL), F32), jax.ShapeDtypeStruct((t, D_MODEL), BF16),
                   jax.ShapeDtypeStruct((t, LANES), F32)],
        compiler_params=_cparams("parallel"), name="post")(oa, ob, oc, h, *ws)


def _moe_kernel(xn_ref, gates_ref, wg_ref, wu_ref, wd_ref, h1_ref, p_ref, wple_ref, nple_ref, wpg_ref,
                out_ref, acc_ref):
    e = pl.program_id(1)

    @pl.when(e == 0)
    def _():
        acc_ref[...] = jnp.zeros_like(acc_ref)

    x = xn_ref[...]
    g = _dot(x, wg_ref[0])
    u = _dot(x, wu_ref[0])
    gate = jnp.sum(jnp.where(_lane_iota() == e, gates_ref[...], 0.0), axis=-1, keepdims=True)
    act = (g * _sigmoid(g)) * u * gate
    acc_ref[...] += _dot(act.astype(BF16), wd_ref[0])

    @pl.when(e == N_EXPERTS - 1)
    def _():
        h2 = h1_ref[...] + acc_ref[...]
        gate_ple = _sigmoid(_dot(_rms(h2, nple_ref[...]).astype(BF16), wpg_ref[...]))
        out_ref[...] = h2 + _dot(p_ref[0].astype(BF16), wple_ref[...]) * gate_ple


def _moe(xn, gates, h1, p_all, layer, lw, tm):
    t = xn.shape[0]
    row = lambda w: pl.BlockSpec((tm, w), lambda i, e: (i, 0))
    full = lambda a: pl.BlockSpec(a.shape, lambda i, e: (0,) * a.ndim)
    exp = lambda a: pl.BlockSpec((1,) + a.shape[1:], lambda i, e: (e, 0, 0))
    return pl.pallas_call(
        _moe_kernel, grid=(t // tm, N_EXPERTS),
        in_specs=[row(D_MODEL), row(LANES), exp(lw['w_eg']), exp(lw['w_eu']), exp(lw['w_ed']),
                  row(D_MODEL), pl.BlockSpec((1, tm, p_all.shape[2]), lambda i, e: (layer, i, 0)),
                  full(lw['w_ple']), full(lw['norm_ple']), full(lw['w_pg'])],
        out_specs=row(D_MODEL), out_shape=jax.ShapeDtypeStruct((t, D_MODEL), F32),
        scratch_shapes=[pltpu.VMEM((tm, D_MODEL), F32)],
        compiler_params=_cparams("parallel", "arbitrary"), name="moe")(
            xn, gates, lw['w_eg'], lw['w_eu'], lw['w_ed'], h1, p_all, lw['w_ple'], lw['norm_ple'], lw['w_pg'])


def _regroup_cols(w):
    z = jnp.zeros((w.shape[0], HEAD_DIM), w.dtype)
    return jnp.concatenate([w[:, :W_A], w[:, W_A:W_A + W_B], z, w[:, W_A + W_B:], z], axis=1)


def _regroup_rows(w):
    z = jnp.zeros((HEAD_DIM, w.shape[1]), w.dtype)
    return jnp.stack([w[:W_A], jnp.concatenate([w[W_A:W_A + W_B], z], axis=0),
                      jnp.concatenate([w[W_A + W_B:], z], axis=0)])


def _pad_lanes(a, n=LANES):
    return jnp.pad(a, [(0, 0)] * (a.ndim - 1) + [(0, n - a.shape[-1])])


def _toeplitz_bias(rel_bias, off, tq, win):
    length = tq + win
    pad = length + abs(off)
    ext = jnp.flip(jnp.pad(rel_bias, ((0, 0), (pad, pad)), mode='edge'), axis=1)
    s1 = ext.shape[1] - 1 - (off + REL_CLIP + pad)
    v = jnp.concatenate([ext[:, s1:s1 + win], ext[:, s1 - tq:s1]], axis=1)
    flat = jnp.tile(v, (1, tq))[:, :tq * (length - 1)]
    return flat.reshape(-1, tq, length - 1)[:, :, :win]


def _block_toeplitz_bias(rel_bias, off, tq, win):
    nq, nk = tq // LANES, win // LANES
    blocks = {d: _toeplitz_bias(rel_bias, off - d * LANES, LANES, LANES) for d in range(-(nq - 1), nk)}
    return jnp.concatenate(
        [jnp.concatenate([blocks[b - a] for b in range(nk)], axis=2) for a in range(nq)], axis=1)


def _rel_tables(rel_bias, t_new, rows_b):
    tq, win = ATT_TILE, ATT_TILE + BAND
    q = jnp.arange(tq)[:, None]
    tabs = []
    for var in range(BAND // tq + 1):
        rel_k = jnp.arange(win)[None, :] - var * tq
        kch = jnp.floor_divide(rel_k, CHUNK)
        qch = q // CHUNK
        valid = (kch <= qch) & (kch >= qch - BAND_CHUNKS)
        tabs.append(jnp.where(valid[None], _block_toeplitz_bias(rel_bias, var * tq, tq, win), NEG_INF))
    tab_prompt = jnp.stack(tabs).astype(F32)
    tab_c = _toeplitz_bias(rel_bias, rows_b, t_new, rows_b).astype(F32)
    tab_n = _toeplitz_bias(rel_bias, 0, t_new, t_new).astype(F32)
    return tab_prompt, tab_c, tab_n


def _layer_weights(i, norm_mix, w_in, b_f, g_qa, g_ka, g_qb, g_kb, g_mix_out, w_out, norm_ffn,
                   w_rg, b_rg, w_re, b_re, w_eg, w_eu, w_ed, w_ple, norm_ple, w_pg):
    d_mix = W_A + W_B + W_C
    wi = w_in[i]
    w_big = jnp.concatenate(
        [_regroup_cols(wi[:, j * d_mix:(j + 1) * d_mix]) for j in range(3)]
        + [_pad_lanes(wi[:, 3 * d_mix:])], axis=1).astype(BF16)
    tile6 = lambda g: jnp.tile(g, GROUP_W // HEAD_DIM)
    w_r = _pad_lanes(jnp.concatenate([w_re[i], w_rg[i]], axis=1))
    w_r_hi = w_r.astype(BF16)
    gm = g_mix_out[i]
    zpad = jnp.zeros((HEAD_DIM,), F32)
    g_mix = jnp.stack([gm[:W_A], jnp.concatenate([gm[W_A:W_A + W_B], zpad]),
                       jnp.concatenate([gm[W_A + W_B:], zpad])])[:, None, :]
    return dict(
        norm_mix=norm_mix[i][None], w_in=w_big,
        gq=jnp.concatenate([tile6(g_qa[i]), tile6(g_qb[i])])[None],
        gk=jnp.concatenate([tile6(g_ka[i]), tile6(g_kb[i])])[None],
        b_f=_pad_lanes(b_f[i])[None],
        g_mix=g_mix, w_out=_regroup_rows(w_out[i]).astype(BF16), norm_ffn=norm_ffn[i][None],
        w_r_hi=w_r_hi, w_r_lo=(w_r - w_r_hi.astype(F32)).astype(BF16),
        b_r=_pad_lanes(jnp.concatenate([b_re[i], b_rg[i]]))[None],
        w_eg=w_eg[i].astype(BF16), w_eu=w_eu[i].astype(BF16), w_ed=w_ed[i].astype(BF16),
        w_ple=w_ple[i].astype(BF16), norm_ple=norm_ple[i][None], w_pg=w_pg[i].astype(BF16))


def _token_tile(t, pref):
    return pref if t % pref == 0 else t


def _feature_major(cache):
    d, n, p, h, e = cache.shape
    return jnp.transpose(cache, (0, 1, 3, 4, 2)).reshape(d, n, h * e, p)


def _position_major(rows, heads):
    d, n, _, p = rows.shape
    return jnp.transpose(rows.reshape(d, n, heads, HEAD_DIM, p), (0, 1, 4, 2, 3))


def kernel(x_prompt, x_sample, p_prompt, p_sample, cache_a_k, cache_a_v, cache_a_logf, cache_b_k, cache_b_v, cache_c_k, cache_c_v, norm_mix, w_in, b_f, g_qa, g_ka, g_qb, g_kb, rel_bias, g_mix_out, w_out, norm_ffn, w_router_group, b_router_group, w_router_expert, b_router_expert, w_exp_gate, w_exp_up, w_exp_down, w_ple, norm_ple, w_ple_gate):
    nb, s, d = x_prompt.shape
    ns, t_new, _ = x_sample.shape
    depth = w_in.shape[0]
    past = cache_a_k.shape[2]
    rows_b = cache_b_k.shape[2]
    assert d == D_MODEL and s % PROJ_TILE == 0 and s >= BAND + ATT_TILE and past % ATT_TILE == 0
    tp, ts = nb * s, ns * t_new

    hp = x_prompt.reshape(tp, d)
    hs = x_sample.reshape(ts, d)
    pp = p_prompt.reshape(depth, tp, -1)
    ps = p_sample.reshape(depth, ts, -1)
    caches = tuple(_feature_major(c) for c in (cache_a_k, cache_a_v, cache_b_k, cache_b_v, cache_c_k, cache_c_v))
    lf_rows = jnp.transpose(cache_a_logf, (0, 3, 1, 2))
    rsum = _suffix_sum(lf_rows.reshape(depth * H_A * ns, past)).reshape(depth, H_A, ns, past)
    rsum = jnp.transpose(rsum, (0, 2, 1, 3))

    rows_p = None
    new_s = [[] for _ in range(7)]
    for i in range(depth):
        lw = _layer_weights(i, norm_mix, w_in, b_f, g_qa, g_ka, g_qb, g_kb, g_mix_out, w_out, norm_ffn,
                            w_router_group, b_router_group, w_router_expert, b_router_expert,
                            w_exp_gate, w_exp_up, w_exp_down, w_ple, norm_ple, w_ple_gate)
        tab_p, tab_c, tab_n = _rel_tables(rel_bias[i], t_new, rows_b)

        pr = _inproj_prompt(hp, lw, i, depth, nb, s, rows_p)
        qat, ka, vat, qb, kb, vb, qc, kc, vc, logf = pr[:10]
        rows_p = pr[10:]
        kaug, qaug = _forget_aug(logf, nb, s)
        oa = _attn_a(qat, ka, vat, kaug, qaug, nb, s)
        ob = _prompt_attention(
            _attn_b_kernel, "attn_b", qb, kb, vb, (tab_p,),
            [pl.BlockSpec(tab_p.shape, lambda b, j: (0, 0, 0, 0))], [], nb, s)
        oc = _prompt_attention(
            _attn_c_kernel, "attn_c", qc, kc, vc, (), [],
            [pltpu.VMEM((H_C, ATT_TILE, PAIR_W), BF16), pltpu.VMEM((H_C, ATT_TILE, LANES), F32),
             pltpu.VMEM((H_C, ATT_TILE, PAIR_W), F32)], nb, s)
        h1, xn, gates = _post(oa, ob, oc, hp, lw, _token_tile(tp, 512))
        hp = _moe(xn, gates, h1, pp, i, lw, _token_tile(tp, 1024))

        sr = _inproj_sample(hs, lw)
        oa, ob, oc = _sample_attention(sr, caches, i, rsum, tab_c, tab_n, ns, t_new)
        h1, xn, gates = _post(oa, ob, oc, hs, lw, _token_tile(ts, 512))
        hs = _moe(xn, gates, h1, ps, i, lw, _token_tile(ts, 1024))
        akf, avf, bkf, bvf, ckf, cvf, logf = sr[9:]
        rows = (akf.reshape(ns, t_new, H_A, HEAD_DIM), avf.reshape(ns, t_new, H_A, HEAD_DIM),
                logf[:, :H_A].reshape(ns, t_new, H_A),
                bkf.reshape(ns, t_new, H_B, HEAD_DIM), bvf.reshape(ns, t_new, H_B, HEAD_DIM),
                ckf.reshape(ns, t_new, H_C, HEAD_DIM), cvf.reshape(ns, t_new, H_C, HEAD_DIM))
        for j in range(7):
            new_s[j].append(rows[j])

    akt, avt, bkt, bvt, ckt, cvt, lft = rows_p
    outs_p = (_position_major(akt, H_A), _position_major(avt, H_A),
              jnp.transpose(lft.reshape(depth, H_A, nb, s), (0, 2, 3, 1)),
              _position_major(bkt, H_B), _position_major(bvt, H_B),
              _position_major(ckt, H_C), _position_major(cvt, H_C))
    outs_s = [jnp.stack(r, axis=0) for r in new_s]
    return (hp.reshape(nb, s, d), hs.reshape(ns, t_new, d), *outs_p, *outs_s)
```

```python
import functools
import math

import jax
import jax.numpy as jnp
from jax import lax
from jax.experimental import pallas as pl
from jax.experimental.pallas import tpu as pltpu

F32 = jnp.float32
BF16 = jnp.bfloat16

D_MODEL = 1024
HEAD_DIM = 64
H_A, H_B, H_C = 6, 5, 5
W_A, W_B, W_C = H_A * HEAD_DIM, H_B * HEAD_DIM, H_C * HEAD_DIM
GROUP_W = 384
PAIR_W = 2 * HEAD_DIM
N_PAIRS = GROUP_W // PAIR_W
CHUNK = 64
BAND_CHUNKS = 8
BAND = BAND_CHUNKS * CHUNK
REL_CLIP = 128
N_GROUPS = 4
EXPERTS_PER_GROUP = 4
N_EXPERTS = N_GROUPS * EXPERTS_PER_GROUP
EPS = 1e-6
ATTN_SCALE = HEAD_DIM ** -0.5
LOG2E = math.log2(math.e)
Q_SCALE = ATTN_SCALE * LOG2E
NEG_INF = -1e30
LANES = 128
SUBLANES = 8
VMEM_LIMIT = 48 * 1024 * 1024

ATT_TILE = 256
PROJ_TILE = BAND
AUG_SLOTS = 8

_NT = (((1,), (1,)), ((), ()))


def _cparams(*sem):
    return pltpu.CompilerParams(dimension_semantics=sem, vmem_limit_bytes=VMEM_LIMIT)


def _dot(a, b):
    return jnp.dot(a, b, preferred_element_type=F32)


def _dot_nt(a, b):
    return lax.dot_general(a, b, _NT, preferred_element_type=F32)


def _split3(x):
    hi = x.astype(BF16)
    r1 = x - hi.astype(F32)
    mid = r1.astype(BF16)
    lo = (r1 - mid.astype(F32)).astype(BF16)
    return hi, mid, lo


def _split2(x):
    hi = x.astype(BF16)
    lo = (x - hi.astype(F32)).astype(BF16)
    return hi, lo


def _lane_iota(n=LANES):
    return lax.broadcasted_iota(jnp.int32, (1, n), 1)


def _half_mask(half):
    lane = _lane_iota()
    return (lane < HEAD_DIM) if half == 0 else (lane >= HEAD_DIM)


def _pair_slice(h):
    return slice((h // 2) * PAIR_W, (h // 2 + 1) * PAIR_W)


def _softplus2(z2):
    return jnp.maximum(z2, 0.0) + jnp.log2(1.0 + jnp.exp2(-jnp.abs(z2)))


def _sigmoid(z):
    return 1.0 / (1.0 + jnp.exp(-z))


def _rms(x, g):
    return x * lax.rsqrt(jnp.mean(x * x, axis=-1, keepdims=True) + EPS) * g


def _pair_rms(z, gain):
    first = _half_mask(0)
    outs = []
    for j in range(N_PAIRS):
        blk = z[:, j * PAIR_W:(j + 1) * PAIR_W]
        sq = blk * blk
        lo = jnp.sum(jnp.where(first, sq, 0.0), axis=-1, keepdims=True)
        hi = jnp.sum(jnp.where(first, 0.0, sq), axis=-1, keepdims=True)
        ms = jnp.where(first, lo, hi) * (1.0 / HEAD_DIM)
        outs.append(blk * lax.rsqrt(ms + EPS) * gain[:, j * PAIR_W:(j + 1) * PAIR_W])
    return jnp.concatenate(outs, axis=-1)


def _projector(x_ref, gn_ref, w_ref):
    xb = _rms(x_ref[...], gn_ref[...]).astype(BF16)
    return lambda g, width=GROUP_W: _dot_nt(xb, w_ref[g * GROUP_W:g * GROUP_W + width, :])


def _log_forget(zf, bf_ref):
    zf = zf + bf_ref[...]
    return jnp.minimum(zf, 0.0) - jnp.log(1.0 + jnp.exp(-jnp.abs(zf)))


def _inproj_sample_kernel(x_ref, gn_ref, w_ref, gq_ref, gk_ref, bf_ref,
                          qa_ref, qb_ref, qc_ref, ka_ref, kb_ref, kc_ref, va_ref, vb_ref, vc_ref,
                          akf_ref, avf_ref, bkf_ref, bvf_ref, ckf_ref, cvf_ref, logf_ref):
    proj = _projector(x_ref, gn_ref, w_ref)
    qa_ref[...] = (_pair_rms(proj(0), gq_ref[:, :GROUP_W]) * Q_SCALE).astype(BF16)
    qb_ref[...] = (_pair_rms(proj(1), gq_ref[:, GROUP_W:]) * Q_SCALE).astype(BF16)
    qc_ref[...] = (proj(2) * Q_SCALE).astype(BF16)
    for g, norm, bf_ref_, f_ref, width in (
            (3, gk_ref[:, :GROUP_W], ka_ref, akf_ref, W_A), (4, gk_ref[:, GROUP_W:], kb_ref, bkf_ref, W_B),
            (5, None, kc_ref, ckf_ref, W_C), (6, None, va_ref, avf_ref, W_A),
            (7, None, vb_ref, bvf_ref, W_B), (8, None, vc_ref, cvf_ref, W_C)):
        z = proj(g)
        if norm is not None:
            z = _pair_rms(z, norm)
        f_ref[...] = z[:, :width]
        bf_ref_[...] = z.astype(BF16)
    logf_ref[...] = _log_forget(proj(9, LANES), bf_ref)


def _inproj_sample(x, lw):
    t = x.shape[0]
    row = lambda w: pl.BlockSpec((t, w), lambda i: (0, 0))
    full = lambda a: pl.BlockSpec(a.shape, lambda i: (0,) * a.ndim)
    widths = [GROUP_W] * 9 + [W_A, W_A, W_B, W_B, W_C, W_C, LANES]
    out_shape = [jax.ShapeDtypeStruct((t, w), BF16 if j < 9 else F32) for j, w in enumerate(widths)]
    ins = (x, lw['norm_mix'], lw['w_in'], lw['gq'], lw['gk'], lw['b_f'])
    return pl.pallas_call(
        _inproj_sample_kernel, grid=(1,),
        in_specs=[row(D_MODEL)] + [full(a) for a in ins[1:]],
        out_specs=[row(w) for w in widths], out_shape=out_shape,
        compiler_params=_cparams("arbitrary"), name="inproj_sample")(*ins)


N_PROMPT_ROWS = 7


def _inproj_prompt_kernel(*refs, n_alias):
    x_ref, gn_ref, w_ref, gq_ref, gk_ref, bf_ref = refs[:6]
    (qat_ref, ka_ref, vat_ref, qb_ref, kb_ref, vb_ref, qc_ref, kc_ref, vc_ref, logf_ref,
     akt_ref, avt_ref, bkt_ref, bvt_ref, ckt_ref, cvt_ref, lft_ref) = refs[6 + n_alias:]
    proj = _projector(x_ref, gn_ref, w_ref)
    qat_ref[...] = jnp.transpose(_pair_rms(proj(0), gq_ref[:, :GROUP_W]) * Q_SCALE).astype(BF16)
    qb_ref[...] = (_pair_rms(proj(1), gq_ref[:, GROUP_W:]) * Q_SCALE).astype(BF16)
    qc_ref[...] = (proj(2) * Q_SCALE).astype(BF16)
    ka = _pair_rms(proj(3), gk_ref[:, :GROUP_W])
    ka_ref[...] = ka.astype(BF16)
    akt_ref[0, 0] = jnp.transpose(ka)
    kb = _pair_rms(proj(4), gk_ref[:, GROUP_W:])
    kb_ref[...] = kb.astype(BF16)
    kc = proj(5)
    kc_ref[...] = kc.astype(BF16)
    ckt_ref[0, 0] = jnp.transpose(kc)[:W_C]
    vat = jnp.transpose(proj(6))
    avt_ref[0, 0] = vat
    for j in range(vat_ref.shape[1]):
        vat_ref[0, j] = vat[:, j * ATT_TILE:(j + 1) * ATT_TILE].astype(BF16)
    vb = proj(7)
    vb_ref[...] = vb.astype(BF16)
    vc = proj(8)
    vc_ref[...] = vc.astype(BF16)
    cvt_ref[0, 0] = jnp.transpose(vc)[:W_C]

    @pl.when(pl.program_id(1) == pl.num_programs(1) - 1)
    def _():
        bkt_ref[0, 0] = jnp.transpose(kb)[:W_B]
        bvt_ref[0, 0] = jnp.transpose(vb)[:W_B]

    lf = _log_forget(proj(9, LANES), bf_ref)
    logf_ref[...] = lf
    lft_ref[0] = jnp.transpose(lf)[:H_A]


def _inproj_prompt(x, lw, layer, depth, nb, s, prev_rows):
    tm = PROJ_TILE
    nt = s // tm
    t = nb * s
    per = tm // ATT_TILE
    row = lambda w: pl.BlockSpec((tm, w), lambda b, i: (b * nt + i, 0))
    full = lambda a: pl.BlockSpec(a.shape, lambda b, i: (0,) * a.ndim)
    feat = lambda w: pl.BlockSpec((1, 1, w, tm), lambda b, i: (layer, b, 0, i))
    band = lambda w: pl.BlockSpec((1, 1, w, BAND), lambda b, i: (layer, b, 0, 0))
    tok = jax.ShapeDtypeStruct((t, GROUP_W), BF16)
    out_shape = [
        jax.ShapeDtypeStruct((GROUP_W, t), BF16), tok,
        jax.ShapeDtypeStruct((nb, s // ATT_TILE, GROUP_W, ATT_TILE), BF16),
        tok, tok, tok, tok, tok, tok, jax.ShapeDtypeStruct((t, LANES), F32),
        jax.ShapeDtypeStruct((depth, nb, W_A, s), F32), jax.ShapeDtypeStruct((depth, nb, W_A, s), F32),
        jax.ShapeDtypeStruct((depth, nb, W_B, BAND), F32), jax.ShapeDtypeStruct((depth, nb, W_B, BAND), F32),
        jax.ShapeDtypeStruct((depth, nb, W_C, s), F32), jax.ShapeDtypeStruct((depth, nb, W_C, s), F32),
        jax.ShapeDtypeStruct((depth, H_A, t), F32)]
    out_specs = [
        pl.BlockSpec((GROUP_W, tm), lambda b, i: (0, b * nt + i)), row(GROUP_W),
        pl.BlockSpec((1, per, GROUP_W, ATT_TILE), lambda b, i: (b, i, 0, 0)),
        row(GROUP_W), row(GROUP_W), row(GROUP_W), row(GROUP_W), row(GROUP_W), row(GROUP_W), row(LANES),
        feat(W_A), feat(W_A), band(W_B), band(W_B), feat(W_C), feat(W_C),
        pl.BlockSpec((1, H_A, tm), lambda b, i: (layer, 0, b * nt + i))]
    ins = [x, lw['norm_mix'], lw['w_in'], lw['gq'], lw['gk'], lw['b_f']]
    in_specs = [row(D_MODEL)] + [full(a) for a in ins[1:]]
    aliases = {}
    n_alias = 0
    if prev_rows is not None:
        n_alias = N_PROMPT_ROWS
        first_row_out = len(out_shape) - N_PROMPT_ROWS
        for j, a in enumerate(prev_rows):
            aliases[len(ins)] = first_row_out + j
            ins.append(a)
            in_specs.append(pl.BlockSpec(memory_space=pl.ANY))
    return pl.pallas_call(
        functools.partial(_inproj_prompt_kernel, n_alias=n_alias), grid=(nb, nt),
        in_specs=in_specs, out_specs=out_specs, out_shape=out_shape,
        input_output_aliases=aliases,
        compiler_params=_cparams("parallel", "arbitrary"), name="inproj_prompt")(*ins)


def _forget_aug_kernel(logf_ref, kaug_ref, qaug_ref, *, nblk):
    tb = ATT_TILE
    r = lax.broadcasted_iota(jnp.int32, (tb, tb), 0)
    c = lax.broadcasted_iota(jnp.int32, (tb, tb), 1)
    tri = jnp.where(c <= r, 1.0, 0.0).astype(BF16)
    src = lax.broadcasted_iota(jnp.int32, (LANES, LANES), 0)
    dst = lax.broadcasted_iota(jnp.int32, (LANES, LANES), 1)

    def place(slot):
        return jnp.where((dst == AUG_SLOTS * src + slot) & (src < H_A), 1.0, 0.0).astype(BF16)

    lane = _lane_iota()
    slot = jnp.bitwise_and(lane, AUG_SLOTS - 1)
    used = lane < AUG_SLOTS * H_A
    key_const = jnp.where(used & (slot >= 3) & (slot < 6), 1.0, 0.0)
    qry_const = jnp.where(used & (slot < 3), -1.0, 0.0)
    carry = jnp.zeros((1, LANES), F32)
    for blk in range(nblk):
        rows = slice(blk * tb, (blk + 1) * tb)
        hi, mid, lo = _split3(logf_ref[rows, :])
        cs = (_dot(tri, hi) + _dot(tri, mid) + _dot(tri, lo)) + carry
        carry = cs[tb - 1:tb, :]
        hi, mid, lo = _split3(cs * LOG2E)
        kaug = _dot(hi, place(0)) + _dot(mid, place(1)) + _dot(lo, place(2)) + key_const
        kaug_ref[rows, :] = kaug.astype(BF16)
        qaug = _dot(hi, place(3)) + _dot(mid, place(4)) + _dot(lo, place(5)) + qry_const
        qaug_ref[:, rows] = jnp.transpose(qaug).astype(BF16)


def _forget_aug(logf, nb, s):
    return pl.pallas_call(
        functools.partial(_forget_aug_kernel, nblk=s // ATT_TILE), grid=(nb,),
        in_specs=[pl.BlockSpec((s, LANES), lambda b: (b, 0))],
        out_specs=[pl.BlockSpec((s, LANES), lambda b: (b, 0)), pl.BlockSpec((LANES, s), lambda b: (0, b))],
        out_shape=[jax.ShapeDtypeStruct((nb * s, LANES), BF16), jax.ShapeDtypeStruct((LANES, nb * s), BF16)],
        compiler_params=_cparams("parallel"), name="forget_aug")(logf)


def _suffix_sum_kernel(x_ref, o_ref):
    p = x_ref.shape[1]
    j = lax.broadcasted_iota(jnp.int32, (p, p), 0)
    s = lax.broadcasted_iota(jnp.int32, (p, p), 1)
    tri = jnp.where(j > s, 1.0, 0.0).astype(BF16)
    hi, mid, lo = _split3(x_ref[...])
    o_ref[...] = _dot(hi, tri) + _dot(mid, tri) + _dot(lo, tri)


def _suffix_sum(x):
    return pl.pallas_call(
        _suffix_sum_kernel, out_shape=jax.ShapeDtypeStruct(x.shape, F32),
        compiler_params=pltpu.CompilerParams(vmem_limit_bytes=VMEM_LIMIT), name="suffix_sum")(x)


def _attn_a_kernel(qt_ref, k_ref, vt_ref, kaug_ref, qaug_ref, o_ref, qp_ref, m_ref, l_ref, acc_ref):
    tq = ATT_TILE
    qi = pl.program_id(1)
    row = lax.broadcasted_iota(jnp.int32, (PAIR_W, 1), 0)
    aug = qaug_ref[...]
    for h in range(H_A):
        q2 = qt_ref[_pair_slice(h), :]
        half = (row < HEAD_DIM) if h % 2 == 0 else (row >= HEAD_DIM)
        qp_ref[h, :PAIR_W, :] = jnp.where(half, q2, jnp.zeros_like(q2))
        mine = (row >= AUG_SLOTS * h) & (row < AUG_SLOTS * (h + 1))
        qp_ref[h, PAIR_W:, :] = jnp.where(mine, aug, jnp.zeros_like(aug))
    m_ref[...] = jnp.full(m_ref.shape, NEG_INF, F32)
    l_ref[...] = jnp.zeros(l_ref.shape, F32)
    acc_ref[...] = jnp.zeros(acc_ref.shape, F32)

    def step(kt, masked):
        ks = pl.multiple_of(kt * tq, tq)
        ka = kaug_ref[pl.ds(ks, tq), :]
        tile_scores = [_dot(jnp.concatenate([k_ref[pl.ds(ks, tq), _pair_slice(h)], ka], axis=1), qp_ref[h])
                       for h in range(H_A)]
        probs, alphas = [], []
        for h in range(H_A):
            s = tile_scores[h]
            if masked:
                key = lax.broadcasted_iota(jnp.int32, (tq, tq), 0)
                qry = lax.broadcasted_iota(jnp.int32, (tq, tq), 1)
                s = jnp.where(key <= qry, s, NEG_INF)
            m_prev = m_ref[h]
            m_new = jnp.maximum(m_prev, jnp.max(s, axis=0, keepdims=True))
            alpha = jnp.exp2(m_prev - m_new)
            p = jnp.exp2(s - m_new)
            l_ref[h] = alpha * l_ref[h] + jnp.sum(p, axis=0, keepdims=True)
            m_ref[h] = m_new
            alphas.append(alpha)
            probs.append(p.astype(BF16))
        for h in range(H_A):
            acc_ref[h] = alphas[h] * acc_ref[h] + _dot(vt_ref[0, kt, _pair_slice(h), :], probs[h])

    def body(kt, carry):
        step(kt, False)
        return carry

    lax.fori_loop(0, qi, body, 0)
    step(qi, True)
    for pair in range(N_PAIRS):
        h = 2 * pair
        out_t = jnp.where(row < HEAD_DIM, acc_ref[h] * (1.0 / l_ref[h]), acc_ref[h + 1] * (1.0 / l_ref[h + 1]))
        o_ref[:, pair * PAIR_W:(pair + 1) * PAIR_W] = jnp.transpose(out_t)


def _attn_a(qat, ka, vat, kaug, qaug, nb, s):
    nq = s // ATT_TILE
    tq = ATT_TILE
    return pl.pallas_call(
        _attn_a_kernel, grid=(nb, nq),
        in_specs=[pl.BlockSpec((GROUP_W, tq), lambda b, i: (0, b * nq + i)),
                  pl.BlockSpec((s, GROUP_W), lambda b, i: (b, 0)),
                  pl.BlockSpec((1, nq, GROUP_W, tq), lambda b, i: (b, 0, 0, 0)),
                  pl.BlockSpec((s, LANES), lambda b, i: (b, 0)),
                  pl.BlockSpec((LANES, tq), lambda b, i: (0, b * nq + i))],
        out_specs=pl.BlockSpec((tq, GROUP_W), lambda b, i: (b * nq + i, 0)),
        out_shape=jax.ShapeDtypeStruct((nb * s, GROUP_W), F32),
        scratch_shapes=[pltpu.VMEM((H_A, 2 * PAIR_W, tq), BF16), pltpu.VMEM((H_A, 1, tq), F32),
                        pltpu.VMEM((H_A, 1, tq), F32), pltpu.VMEM((H_A, PAIR_W, tq), F32)],
        compiler_params=_cparams("parallel", "arbitrary"), name="attn_a")(qat, ka, vat, kaug, qaug)


def _masked_q(q_ref, qm_ref, n_heads):
    for h in range(n_heads):
        q2 = q_ref[:, _pair_slice(h)]
        qm_ref[h] = jnp.where(_half_mask(h % 2), q2, jnp.zeros_like(q2))


def _attn_b_kernel(q_ref, k_ref, v_ref, tab_ref, o_ref):
    tq = ATT_TILE
    win = tq + BAND
    i = pl.program_id(1)
    var = jnp.minimum(i, BAND // tq)
    ws = pl.multiple_of(jnp.maximum(i * tq - BAND, 0), tq)
    heads = range(H_B)
    scores = []
    for h in heads:
        q2 = q_ref[:, _pair_slice(h)]
        qm = jnp.where(_half_mask(h % 2), q2, jnp.zeros_like(q2))
        scores.append(_dot_nt(qm, k_ref[pl.ds(ws, win), _pair_slice(h)]) + tab_ref[var, h])
    probs, norms = [], []
    for h in heads:
        p = jnp.exp2(scores[h] - jnp.max(scores[h], axis=-1, keepdims=True))
        norms.append(1.0 / jnp.sum(p, axis=-1, keepdims=True))
        probs.append(p.astype(BF16))
    res = [_dot(probs[h], v_ref[pl.ds(ws, win), _pair_slice(h)]) * norms[h] for h in heads]
    res.append(jnp.zeros((tq, PAIR_W), F32))
    first = _half_mask(0)
    for pair in range(N_PAIRS):
        o_ref[:, pair * PAIR_W:(pair + 1) * PAIR_W] = jnp.where(first, res[2 * pair], res[2 * pair + 1])


def _suffix_tri(n):
    j = lax.broadcasted_iota(jnp.int32, (2 * n, n), 0)
    s = lax.broadcasted_iota(jnp.int32, (2 * n, n), 1)
    return jnp.where(jnp.where(j >= n, j - n, j) >= s, 1.0, 0.0).astype(BF16)


def _stick_scores(z, tri, seen):
    sp = _softplus2(z)
    if seen is not None:
        sp = jnp.where(seen, sp, 0.0)
    hi, lo = _split2(sp)
    n = sp.shape[1]
    if n % LANES == 0:
        return _dot(jnp.concatenate([hi, lo], axis=1), tri)
    return _dot(hi, tri[:n]) + _dot(lo, tri[:n])


def _stick_weights(z, s_in, carry, seen):
    a = jnp.exp2(z - (s_in + carry))
    if seen is not None:
        a = jnp.where(seen, a, 0.0)
    return a.astype(BF16)


def _attn_c_kernel(q_ref, k_ref, v_ref, o_ref, qm_ref, carry_ref, acc_ref):
    tq = ATT_TILE
    qi = pl.program_id(1)
    _masked_q(q_ref, qm_ref, H_C)
    carry_ref[...] = jnp.zeros(carry_ref.shape, F32)
    acc_ref[...] = jnp.zeros(acc_ref.shape, F32)
    heads = range(H_C)

    def step(kt, diag):
        ks = pl.multiple_of(kt * tq, tq)
        zs = [_dot_nt(qm_ref[h], k_ref[pl.ds(ks, tq), _pair_slice(h)]) for h in heads]
        tri = _suffix_tri(tq)
        seen = None
        if diag:
            r = lax.broadcasted_iota(jnp.int32, (tq, tq), 0)
            c = lax.broadcasted_iota(jnp.int32, (tq, tq), 1)
            seen = c < r
        sums = [_stick_scores(zs[h], tri, seen) for h in heads]
        weights = []
        for h in heads:
            carry = carry_ref[h][:, 0:1]
            weights.append(_stick_weights(zs[h], sums[h], carry, seen))
            carry_ref[h] = jnp.broadcast_to(carry + sums[h][:, 0:1], (tq, LANES))
        for h in heads:
            acc_ref[h] += _dot(weights[h], v_ref[pl.ds(ks, tq), _pair_slice(h)])

    def body(j, c):
        step(qi - 1 - j, False)
        return c

    step(qi, True)
    lax.fori_loop(0, qi, body, 0)
    first = _half_mask(0)
    for pair in range(N_PAIRS):
        h = 2 * pair
        second = acc_ref[h + 1] if h + 1 < H_C else jnp.zeros((tq, PAIR_W), F32)
        o_ref[:, pair * PAIR_W:(pair + 1) * PAIR_W] = jnp.where(first, acc_ref[h], second)


def _prompt_attention(kernel, name, q, k, v, extra, extra_specs, scratch, nb, s):
    nq = s // ATT_TILE
    qspec = pl.BlockSpec((ATT_TILE, GROUP_W), lambda b, i: (b * nq + i, 0))
    kvspec = pl.BlockSpec((s, GROUP_W), lambda b, i: (b, 0))
    return pl.pallas_call(
        kernel, grid=(nb, nq),
        in_specs=[qspec, kvspec, kvspec] + extra_specs,
        out_specs=qspec, out_shape=jax.ShapeDtypeStruct((nb * s, GROUP_W), F32),
        scratch_shapes=scratch,
        compiler_params=_cparams("parallel", "arbitrary"), name=name)(q, k, v, *extra)


def _sample_heads(n_heads, q_ref, width):
    out = []
    for h in range(n_heads):
        lo = (h // 2) * PAIR_W
        span = slice(lo, min(lo + PAIR_W, width))
        q2 = q_ref[:, span]
        if span.stop - span.start == PAIR_W:
            q2 = jnp.where(_half_mask(h % 2), q2, jnp.zeros_like(q2))
        out.append((h, span, q2))
    return out


def _store_heads(o_ref, res, n_heads):
    for pair in range(N_PAIRS):
        lo = pair * PAIR_W
        h = 2 * pair
        if h + 1 < n_heads:
            o_ref[:, lo:lo + PAIR_W] = jnp.where(_half_mask(0), res[h], res[h + 1])
        else:
            o_ref[:, lo:lo + HEAD_DIM] = res[h]
            o_ref[:, lo + HEAD_DIM:lo + PAIR_W] = jnp.zeros_like(res[h])


def _joint_softmax_pv(s_c, s_n, vt_c, v_n):
    m = jnp.maximum(jnp.max(s_c, axis=-1, keepdims=True), jnp.max(s_n, axis=-1, keepdims=True))
    p_c = jnp.exp2(s_c - m)
    p_n = jnp.exp2(s_n - m)
    l = jnp.sum(p_c, axis=-1, keepdims=True) + jnp.sum(p_n, axis=-1, keepdims=True)
    return (_dot_nt(p_c.astype(BF16), vt_c) + _dot(p_n.astype(BF16), v_n)) / l


def _sample_attn_kernel(qa_ref, qb_ref, qc_ref, kan_ref, kbn_ref, kcn_ref, van_ref, vbn_ref, vcn_ref,
                        cak_ref, cav_ref, cbk_ref, cbv_ref, cck_ref, ccv_ref,
                        rsum_ref, logf_ref, tabc_ref, tabn_ref,
                        oa_ref, ob_ref, oc_ref):
    t = qa_ref.shape[0]
    past = cak_ref.shape[3]
    r = lax.broadcasted_iota(jnp.int32, (t, t), 0)
    c = lax.broadcasted_iota(jnp.int32, (t, t), 1)

    ltri = jnp.where(c <= r, 1.0, 0.0).astype(BF16)
    hi, mid, lo = _split3(logf_ref[...])
    pcol = (_dot(ltri, hi) + _dot(ltri, mid) + _dot(ltri, lo)) * LOG2E
    prow = jnp.transpose(jnp.concatenate([pcol, jnp.zeros((LANES - t, LANES), F32)], axis=0))
    heads = _sample_heads(H_A, qa_ref, W_A)
    kts = {sp.start: cak_ref[0, 0, sp, :].astype(BF16) for _, sp, _ in heads}
    vts = {sp.start: cav_ref[0, 0, sp, :].astype(BF16) for _, sp, _ in heads}
    s_c = [_dot(qm, kts[sp.start]) + (pcol[:, h:h + 1] + rsum_ref[0, 0, h:h + 1, :] * LOG2E)
           for h, sp, qm in heads]
    s_n = [jnp.where(c <= r, _dot_nt(qm, kan_ref[:, sp]) + (pcol[:, h:h + 1] - prow[h:h + 1, 0:t]), NEG_INF)
           for h, sp, qm in heads]
    _store_heads(oa_ref, [_joint_softmax_pv(s_c[h], s_n[h], vts[sp.start], van_ref[:, sp])
                          for h, sp, _ in heads], H_A)

    heads = _sample_heads(H_B, qb_ref, W_B)
    kts = {sp.start: cbk_ref[0, 0, sp, :].astype(BF16) for _, sp, _ in heads}
    vts = {sp.start: cbv_ref[0, 0, sp, :].astype(BF16) for _, sp, _ in heads}
    s_c = [_dot(qm, kts[sp.start]) + tabc_ref[h] for h, sp, qm in heads]
    s_n = [_dot_nt(qm, kbn_ref[:, sp]) + tabn_ref[h] for h, sp, qm in heads]
    _store_heads(ob_ref, [_joint_softmax_pv(s_c[h], s_n[h], vts[sp.start], vbn_ref[:, sp])
                          for h, sp, _ in heads], H_B)

    heads = _sample_heads(H_C, qc_ref, W_C)
    nblk = past // ATT_TILE
    tri_n = _suffix_tri(t)
    tri_c = _suffix_tri(ATT_TILE)
    seen = c < r
    kts = {sp.start: cck_ref[0, 0, sp, :].astype(BF16) for _, sp, _ in heads}
    vts = {sp.start: ccv_ref[0, 0, sp, :].astype(BF16) for _, sp, _ in heads}
    z_n = [_dot_nt(qm, kcn_ref[:, sp]) for _, sp, qm in heads]
    z_c = [_dot(qm, kts[sp.start]) for _, sp, qm in heads]
    sum_n = [_stick_scores(z, tri_n, seen) for z in z_n]
    sum_c = [[_stick_scores(z[:, b * ATT_TILE:(b + 1) * ATT_TILE], tri_c, None) for b in range(nblk)]
             for z in z_c]
    res = []
    for h, sp, _ in heads:
        acc = _dot(_stick_weights(z_n[h], sum_n[h], 0.0, seen), vcn_ref[:, sp])
        carry = sum_n[h][:, 0:1]
        for b in reversed(range(nblk)):
            cols = slice(b * ATT_TILE, (b + 1) * ATT_TILE)
            a = _stick_weights(z_c[h][:, cols], sum_c[h][b], carry, None)
            acc = acc + _dot_nt(a, vts[sp.start][:, cols])
            carry = carry + sum_c[h][b][:, 0:1]
        res.append(acc)
    _store_heads(oc_ref, res, H_C)


def _sample_attention(proj, caches, layer, rsum, tabc, tabn, nb, t):
    qa, qb, qc, ka, kb, kc, va, vb, vc = proj[:9]
    logf = proj[15]
    new = pl.BlockSpec((t, GROUP_W), lambda b: (b, 0))
    cache = lambda a: pl.BlockSpec((1, 1) + a.shape[2:], lambda b: (layer, b, 0, 0))
    full = lambda a: pl.BlockSpec(a.shape, lambda b: (0,) * a.ndim)
    return pl.pallas_call(
        _sample_attn_kernel, grid=(nb,),
        in_specs=[new] * 9 + [cache(a) for a in caches] + [
            cache(rsum), pl.BlockSpec((t, LANES), lambda b: (b, 0)), full(tabc), full(tabn)],
        out_specs=[new] * 3,
        out_shape=[jax.ShapeDtypeStruct((nb * t, GROUP_W), F32)] * 3,
        compiler_params=_cparams("parallel"), name="sample_attn")(
            qa, qb, qc, ka, kb, kc, va, vb, vc, *caches, rsum, logf, tabc, tabn)


def _masked_max(x, mask):
    return jnp.max(jnp.where(mask, x, -jnp.inf), axis=-1, keepdims=True)


def _first_lane(mask, lane):
    return jnp.min(jnp.where(mask, lane, float(LANES)), axis=-1, keepdims=True)


def _route(logits):
    lane = _lane_iota().astype(F32)
    is_g = (lane >= N_EXPERTS) & (lane < N_EXPERTS + N_GROUPS)
    gmax = _masked_max(logits, is_g)
    p_g = 1.0 / jnp.sum(jnp.where(is_g, jnp.exp(logits - gmax), 0.0), axis=-1, keepdims=True)
    g_sel = _first_lane(is_g & (logits == gmax), lane) - N_EXPERTS
    lo = g_sel * EXPERTS_PER_GROUP
    in_g = (lane >= lo) & (lane < lo + EXPERTS_PER_GROUP)
    l1 = _masked_max(logits, in_g)
    i1 = _first_lane(in_g & (logits == l1), lane)
    rest = in_g & (lane != i1)
    l2 = _masked_max(logits, rest)
    i2 = _first_lane(rest & (logits == l2), lane)
    e2 = jnp.exp(l2 - l1)
    w1 = p_g / (1.0 + e2)
    w2 = p_g * e2 / (1.0 + e2)
    return jnp.where(lane == i1, w1, jnp.where(lane == i2, w2, 0.0))


def _post_kernel(oa_ref, ob_ref, oc_ref, h_ref, gmix_ref, wout_ref, nffn_ref, wrh_ref, wrl_ref, br_ref,
                 h1_ref, xn_ref, gates_ref):
    def gnorm(o, width):
        ms = jnp.sum(o * o, axis=-1, keepdims=True) * (1.0 / width)
        return o * lax.rsqrt(ms + EPS)

    h1 = h_ref[...]
    for g, (o_ref, width) in enumerate(((oa_ref, W_A), (ob_ref, W_B), (oc_ref, W_C))):
        merged = (gnorm(o_ref[...], width) * gmix_ref[g]).astype(BF16)
        h1 = h1 + _dot(merged, wout_ref[g])
    h1_ref[...] = h1
    xf = _rms(h1, nffn_ref[...])
    hi, lo = _split2(xf)
    xn_ref[...] = hi
    logits = _dot(hi, wrh_ref[...]) + _dot(lo, wrh_ref[...]) + _dot(hi, wrl_ref[...]) + br_ref[...]
    gates_ref[...] = _route(logits)


def _post(oa, ob, oc, h, lw, tm):
    t = h.shape[0]
    row = lambda w: pl.BlockSpec((tm, w), lambda i: (i, 0))
    full = lambda a: pl.BlockSpec(a.shape, lambda i: (0,) * a.ndim)
    ws = (lw['g_mix'], lw['w_out'], lw['norm_ffn'], lw['w_r_hi'], lw['w_r_lo'], lw['b_r'])
    return pl.pallas_call(
        _post_kernel, grid=(t // tm,),
        in_specs=[row(GROUP_W)] * 3 + [row(D_MODEL)] + [full(a) for a in ws],
        out_specs=[row(D_MODEL), row(D_MODEL), row(LANES)],
        out_shape=[jax.ShapeDtypeStruct((t, D_MODEL), F32), jax.ShapeDtypeStruct((t, D_MODEL), BF16),
                   jax.ShapeDtypeStruct((t, LANES), F32)],
        compiler_params=_cparams("parallel"), name="post")(oa, ob, oc, h, *ws)


def _moe_kernel(xn_ref, gates_ref, wg_ref, wu_ref, wd_ref, h1_ref, p_ref, wple_ref, nple_ref, wpg_ref,
                out_ref, acc_ref):
    e = pl.program_id(1)

    @pl.when(e == 0)
    def _():
        acc_ref[...] = jnp.zeros_like(acc_ref)

    x = xn_ref[...]
    g = _dot(x, wg_ref[0])
    u = _dot(x, wu_ref[0])
    gate = jnp.sum(jnp.where(_lane_iota() == e, gates_ref[...], 0.0), axis=-1, keepdims=True)
    act = (g * _sigmoid(g)) * u * gate
    acc_ref[...] += _dot(act.astype(BF16), wd_ref[0])

    @pl.when(e == N_EXPERTS - 1)
    def _():
        h2 = h1_ref[...] + acc_ref[...]
        gate_ple = _sigmoid(_dot(_rms(h2, nple_ref[...]).astype(BF16), wpg_ref[...]))
        out_ref[...] = h2 + _dot(p_ref[0].astype(BF16), wple_ref[...]) * gate_ple


def _moe(xn, gates, h1, p_all, layer, lw, tm):
    t = xn.shape[0]
    row = lambda w: pl.BlockSpec((tm, w), lambda i, e: (i, 0))
    full = lambda a: pl.BlockSpec(a.shape, lambda i, e: (0,) * a.ndim)
    exp = lambda a: pl.BlockSpec((1,) + a.shape[1:], lambda i, e: (e, 0, 0))
    return pl.pallas_call(
        _moe_kernel, grid=(t // tm, N_EXPERTS),
        in_specs=[row(D_MODEL), row(LANES), exp(lw['w_eg']), exp(lw['w_eu']), exp(lw['w_ed']),
                  row(D_MODEL), pl.BlockSpec((1, tm, p_all.shape[2]), lambda i, e: (layer, i, 0)),
                  full(lw['w_ple']), full(lw['norm_ple']), full(lw['w_pg'])],
        out_specs=row(D_MODEL), out_shape=jax.ShapeDtypeStruct((t, D_MODEL), F32),
        scratch_shapes=[pltpu.VMEM((tm, D_MODEL), F32)],
        compiler_params=_cparams("parallel", "arbitrary"), name="moe")(
            xn, gates, lw['w_eg'], lw['w_eu'], lw['w_ed'], h1, p_all, lw['w_ple'], lw['norm_ple'], lw['w_pg'])


def _regroup_out_rows(wt):
    z = jnp.zeros((HEAD_DIM, wt.shape[1]), wt.dtype)
    return jnp.concatenate([wt[:W_A], wt[W_A:W_A + W_B], z, wt[W_A + W_B:], z], axis=0)


def _regroup_rows(w):
    z = jnp.zeros((HEAD_DIM, w.shape[1]), w.dtype)
    return jnp.stack([w[:W_A], jnp.concatenate([w[W_A:W_A + W_B], z], axis=0),
                      jnp.concatenate([w[W_A + W_B:], z], axis=0)])


def _pad_lanes(a, n=LANES):
    return jnp.pad(a, [(0, 0)] * (a.ndim - 1) + [(0, n - a.shape[-1])])


def _toeplitz_bias(rel_bias, off, tq, win):
    length = tq + win
    pad = length + abs(off)
    ext = jnp.flip(jnp.pad(rel_bias, ((0, 0), (pad, pad)), mode='edge'), axis=1)
    s1 = ext.shape[1] - 1 - (off + REL_CLIP + pad)
    v = jnp.concatenate([ext[:, s1:s1 + win], ext[:, s1 - tq:s1]], axis=1)
    flat = jnp.tile(v, (1, tq))[:, :tq * (length - 1)]
    return flat.reshape(-1, tq, length - 1)[:, :, :win]


def _block_toeplitz_bias(rel_bias, off, tq, win):
    nq, nk = tq // LANES, win // LANES
    blocks = {d: _toeplitz_bias(rel_bias, off - d * LANES, LANES, LANES) for d in range(-(nq - 1), nk)}
    return jnp.concatenate(
        [jnp.concatenate([blocks[b - a] for b in range(nk)], axis=2) for a in range(nq)], axis=1)


def _rel_tables(rel_bias, t_new, rows_b):
    tq, win = ATT_TILE, ATT_TILE + BAND
    q = jnp.arange(tq)[:, None]
    tabs = []
    for var in range(BAND // tq + 1):
        rel_k = jnp.arange(win)[None, :] - var * tq
        kch = jnp.floor_divide(rel_k, CHUNK)
        qch = q // CHUNK
        valid = (kch <= qch) & (kch >= qch - BAND_CHUNKS)
        tabs.append(jnp.where(valid[None], _block_toeplitz_bias(rel_bias, var * tq, tq, win), NEG_INF))
    tab_prompt = jnp.stack(tabs).astype(F32)
    tab_c = _toeplitz_bias(rel_bias, rows_b, t_new, rows_b).astype(F32)
    tab_n = _toeplitz_bias(rel_bias, 0, t_new, t_new).astype(F32)
    return tab_prompt, tab_c, tab_n


def _layer_weights(i, norm_mix, w_in, b_f, g_qa, g_ka, g_qb, g_kb, g_mix_out, w_out, norm_ffn,
                   w_rg, b_rg, w_re, b_re, w_eg, w_eu, w_ed, w_ple, norm_ple, w_pg):
    d_mix = W_A + W_B + W_C
    wt = jnp.transpose(w_in, (2, 0, 1))[:, i, :]
    w_f = wt[3 * d_mix:]
    w_big = jnp.concatenate(
        [_regroup_out_rows(wt[j * d_mix:(j + 1) * d_mix]) for j in range(3)]
        + [jnp.pad(w_f, ((0, LANES - w_f.shape[0]), (0, 0)))], axis=0).astype(BF16)
    tile6 = lambda g: jnp.tile(g, GROUP_W // HEAD_DIM)
    w_r = _pad_lanes(jnp.concatenate([w_re[i], w_rg[i]], axis=1))
    w_r_hi = w_r.astype(BF16)
    gm = g_mix_out[i]
    zpad = jnp.zeros((HEAD_DIM,), F32)
    g_mix = jnp.stack([gm[:W_A], jnp.concatenate([gm[W_A:W_A + W_B], zpad]),
                       jnp.concatenate([gm[W_A + W_B:], zpad])])[:, None, :]
    return dict(
        norm_mix=norm_mix[i][None], w_in=w_big,
        gq=jnp.concatenate([tile6(g_qa[i]), tile6(g_qb[i])])[None],
        gk=jnp.concatenate([tile6(g_ka[i]), tile6(g_kb[i])])[None],
        b_f=_pad_lanes(b_f[i])[None],
        g_mix=g_mix, w_out=_regroup_rows(w_out[i]).astype(BF16), norm_ffn=norm_ffn[i][None],
        w_r_hi=w_r_hi, w_r_lo=(w_r - w_r_hi.astype(F32)).astype(BF16),
        b_r=_pad_lanes(jnp.concatenate([b_re[i], b_rg[i]]))[None],
        w_eg=w_eg[i].astype(BF16), w_eu=w_eu[i].astype(BF16), w_ed=w_ed[i].astype(BF16),
        w_ple=w_ple[i].astype(BF16), norm_ple=norm_ple[i][None], w_pg=w_pg[i].astype(BF16))


def _token_tile(t, pref):
    return pref if t % pref == 0 else t


def _feature_major(cache):
    d, n, p, h, e = cache.shape
    return jnp.transpose(cache, (0, 1, 3, 4, 2)).reshape(d, n, h * e, p)


def _position_major(rows, heads):
    d, n, _, p = rows.shape
    return jnp.transpose(rows.reshape(d, n, heads, HEAD_DIM, p), (0, 1, 4, 2, 3))


def kernel(x_prompt, x_sample, p_prompt, p_sample, cache_a_k, cache_a_v, cache_a_logf, cache_b_k, cache_b_v, cache_c_k, cache_c_v, norm_mix, w_in, b_f, g_qa, g_ka, g_qb, g_kb, rel_bias, g_mix_out, w_out, norm_ffn, w_router_group, b_router_group, w_router_expert, b_router_expert, w_exp_gate, w_exp_up, w_exp_down, w_ple, norm_ple, w_ple_gate):
    nb, s, d = x_prompt.shape
    ns, t_new, _ = x_sample.shape
    depth = w_in.shape[0]
    past = cache_a_k.shape[2]
    rows_b = cache_b_k.shape[2]
    assert d == D_MODEL and s % PROJ_TILE == 0 and s >= BAND + ATT_TILE and past % ATT_TILE == 0
    tp, ts = nb * s, ns * t_new

    hp = x_prompt.reshape(tp, d)
    hs = x_sample.reshape(ts, d)
    pp = p_prompt.reshape(depth, tp, -1)
    ps = p_sample.reshape(depth, ts, -1)
    caches = tuple(_feature_major(c) for c in (cache_a_k, cache_a_v, cache_b_k, cache_b_v, cache_c_k, cache_c_v))
    lf_rows = jnp.transpose(cache_a_logf, (0, 3, 1, 2))
    rsum = _suffix_sum(lf_rows.reshape(depth * H_A * ns, past)).reshape(depth, H_A, ns, past)
    rsum = jnp.transpose(rsum, (0, 2, 1, 3))

    rows_p = None
    new_s = [[] for _ in range(7)]
    for i in range(depth):
        lw = _layer_weights(i, norm_mix, w_in, b_f, g_qa, g_ka, g_qb, g_kb, g_mix_out, w_out, norm_ffn,
                            w_router_group, b_router_group, w_router_expert, b_router_expert,
                            w_exp_gate, w_exp_up, w_exp_down, w_ple, norm_ple, w_ple_gate)
        tab_p, tab_c, tab_n = _rel_tables(rel_bias[i] * LOG2E, t_new, rows_b)

        pr = _inproj_prompt(hp, lw, i, depth, nb, s, rows_p)
        qat, ka, vat, qb, kb, vb, qc, kc, vc, logf = pr[:10]
        rows_p = pr[10:]
        kaug, qaug = _forget_aug(logf, nb, s)
        oa = _attn_a(qat, ka, vat, kaug, qaug, nb, s)
        ob = _prompt_attention(
            _attn_b_kernel, "attn_b", qb, kb, vb, (tab_p,),
            [pl.BlockSpec(tab_p.shape, lambda b, j: (0, 0, 0, 0))], [], nb, s)
        oc = _prompt_attention(
            _attn_c_kernel, "attn_c", qc, kc, vc, (), [],
            [pltpu.VMEM((H_C, ATT_TILE, PAIR_W), BF16), pltpu.VMEM((H_C, ATT_TILE, LANES), F32),
             pltpu.VMEM((H_C, ATT_TILE, PAIR_W), F32)], nb, s)
        h1, xn, gates = _post(oa, ob, oc, hp, lw, _token_tile(tp, 512))
        hp = _moe(xn, gates, h1, pp, i, lw, _token_tile(tp, 1024))

        sr = _inproj_sample(hs, lw)
        oa, ob, oc = _sample_attention(sr, caches, i, rsum, tab_c, tab_n, ns, t_new)
        h1, xn, gates = _post(oa, ob, oc, hs, lw, _token_tile(ts, 512))
        hs = _moe(xn, gates, h1, ps, i, lw, _token_tile(ts, 1024))
        akf, avf, bkf, bvf, ckf, cvf, logf = sr[9:]
        rows = (akf.reshape(ns, t_new, H_A, HEAD_DIM), avf.reshape(ns, t_new, H_A, HEAD_DIM),
                logf[:, :H_A].reshape(ns, t_new, H_A),
                bkf.reshape(ns, t_new, H_B, HEAD_DIM), bvf.reshape(ns, t_new, H_B, HEAD_DIM),
                ckf.reshape(ns, t_new, H_C, HEAD_DIM), cvf.reshape(ns, t_new, H_C, HEAD_DIM))
        for j in range(7):
            new_s[j].append(rows[j])

    akt, avt, bkt, bvt, ckt, cvt, lft = rows_p
    outs_p = (_position_major(akt, H_A), _position_major(avt, H_A),
              jnp.transpose(lft.reshape(depth, H_A, nb, s), (0, 2, 3, 1)),
              _position_major(bkt, H_B), _position_major(bvt, H_B),
              _position_major(ckt, H_C), _position_major(cvt, H_C))
    outs_s = [jnp.stack(r, axis=0) for r in new_s]
    return (hp.reshape(nb, s, d), hs.reshape(ns, t_new, d), *outs_p, *outs_s)
```

```python
import functools
import math

import jax
import jax.numpy as jnp
from jax import lax
from jax.experimental import pallas as pl
from jax.experimental.pallas import tpu as pltpu

F32 = jnp.float32
BF16 = jnp.bfloat16

D_MODEL = 1024
HEAD_DIM = 64
H_A, H_B, H_C = 6, 5, 5
W_A, W_B, W_C = H_A * HEAD_DIM, H_B * HEAD_DIM, H_C * HEAD_DIM
GROUP_W = 384
PAIR_W = 2 * HEAD_DIM
N_PAIRS = GROUP_W // PAIR_W
CHUNK = 64
BAND_CHUNKS = 8
BAND = BAND_CHUNKS * CHUNK
REL_CLIP = 128
N_GROUPS = 4
EXPERTS_PER_GROUP = 4
N_EXPERTS = N_GROUPS * EXPERTS_PER_GROUP
EPS = 1e-6
ATTN_SCALE = HEAD_DIM ** -0.5
LOG2E = math.log2(math.e)
Q_SCALE = ATTN_SCALE * LOG2E
NEG_INF = -1e30
LANES = 128
SUBLANES = 8
VMEM_LIMIT = 48 * 1024 * 1024
MOE_VMEM_LIMIT = 58 * 1024 * 1024
MOE_BLOCK = 896
MOE_CAP = 256

ATT_TILE = 256
PROJ_TILE = BAND
AUG_SLOTS = 8

_NT = (((1,), (1,)), ((), ()))


def _cparams(*sem):
    return pltpu.CompilerParams(dimension_semantics=sem, vmem_limit_bytes=VMEM_LIMIT)


def _dot(a, b):
    return jnp.dot(a, b, preferred_element_type=F32)


def _dot_nt(a, b):
    return lax.dot_general(a, b, _NT, preferred_element_type=F32)


def _split3(x):
    hi = x.astype(BF16)
    r1 = x - hi.astype(F32)
    mid = r1.astype(BF16)
    lo = (r1 - mid.astype(F32)).astype(BF16)
    return hi, mid, lo


def _split2(x):
    hi = x.astype(BF16)
    lo = (x - hi.astype(F32)).astype(BF16)
    return hi, lo


def _lane_iota(n=LANES):
    return lax.broadcasted_iota(jnp.int32, (1, n), 1)


def _half_mask(half):
    lane = _lane_iota()
    return (lane < HEAD_DIM) if half == 0 else (lane >= HEAD_DIM)


def _pair_slice(h):
    return slice((h // 2) * PAIR_W, (h // 2 + 1) * PAIR_W)


def _softplus2(z2):
    return jnp.maximum(z2, 0.0) + jnp.log2(1.0 + jnp.exp2(-jnp.abs(z2)))


def _sigmoid(z):
    return 1.0 / (1.0 + jnp.exp(-z))


def _rms(x, g):
    return x * lax.rsqrt(jnp.mean(x * x, axis=-1, keepdims=True) + EPS) * g


def _pair_rms(z, gain):
    first = _half_mask(0)
    outs = []
    for j in range(N_PAIRS):
        blk = z[:, j * PAIR_W:(j + 1) * PAIR_W]
        sq = blk * blk
        lo = jnp.sum(jnp.where(first, sq, 0.0), axis=-1, keepdims=True)
        hi = jnp.sum(jnp.where(first, 0.0, sq), axis=-1, keepdims=True)
        ms = jnp.where(first, lo, hi) * (1.0 / HEAD_DIM)
        outs.append(blk * lax.rsqrt(ms + EPS) * gain[:, j * PAIR_W:(j + 1) * PAIR_W])
    return jnp.concatenate(outs, axis=-1)


def _projector(x_ref, gn_ref, w_ref):
    xb = _rms(x_ref[...], gn_ref[...]).astype(BF16)
    return lambda g, width=GROUP_W: _dot_nt(xb, w_ref[g * GROUP_W:g * GROUP_W + width, :])


def _log_forget(zf, bf_ref):
    zf = zf + bf_ref[...]
    return jnp.minimum(zf, 0.0) - jnp.log(1.0 + jnp.exp(-jnp.abs(zf)))


def _inproj_sample_kernel(x_ref, gn_ref, w_ref, gq_ref, gk_ref, bf_ref,
                          qa_ref, qb_ref, qc_ref, ka_ref, kb_ref, kc_ref, va_ref, vb_ref, vc_ref,
                          akf_ref, avf_ref, bkf_ref, bvf_ref, ckf_ref, cvf_ref, logf_ref):
    proj = _projector(x_ref, gn_ref, w_ref)
    qa_ref[...] = (_pair_rms(proj(0), gq_ref[:, :GROUP_W]) * Q_SCALE).astype(BF16)
    qb_ref[...] = (_pair_rms(proj(1), gq_ref[:, GROUP_W:]) * Q_SCALE).astype(BF16)
    qc_ref[...] = (proj(2) * Q_SCALE).astype(BF16)
    for g, norm, bf_ref_, f_ref, width in (
            (3, gk_ref[:, :GROUP_W], ka_ref, akf_ref, W_A), (4, gk_ref[:, GROUP_W:], kb_ref, bkf_ref, W_B),
            (5, None, kc_ref, ckf_ref, W_C), (6, None, va_ref, avf_ref, W_A),
            (7, None, vb_ref, bvf_ref, W_B), (8, None, vc_ref, cvf_ref, W_C)):
        z = proj(g)
        if norm is not None:
            z = _pair_rms(z, norm)
        f_ref[...] = z[:, :width]
        bf_ref_[...] = z.astype(BF16)
    logf_ref[...] = _log_forget(proj(9, LANES), bf_ref)


def _inproj_sample(x, lw):
    t = x.shape[0]
    row = lambda w: pl.BlockSpec((t, w), lambda i: (0, 0))
    full = lambda a: pl.BlockSpec(a.shape, lambda i: (0,) * a.ndim)
    widths = [GROUP_W] * 9 + [W_A, W_A, W_B, W_B, W_C, W_C, LANES]
    out_shape = [jax.ShapeDtypeStruct((t, w), BF16 if j < 9 else F32) for j, w in enumerate(widths)]
    ins = (x, lw['norm_mix'], lw['w_in'], lw['gq'], lw['gk'], lw['b_f'])
    return pl.pallas_call(
        _inproj_sample_kernel, grid=(1,),
        in_specs=[row(D_MODEL)] + [full(a) for a in ins[1:]],
        out_specs=[row(w) for w in widths], out_shape=out_shape,
        compiler_params=_cparams("arbitrary"), name="inproj_sample")(*ins)


N_PROMPT_ROWS = 7


def _inproj_prompt_kernel(*refs, n_alias):
    x_ref, gn_ref, w_ref, gq_ref, gk_ref, bf_ref = refs[:6]
    (qat_ref, ka_ref, vat_ref, qb_ref, kb_ref, vb_ref, qc_ref, kc_ref, vc_ref, logf_ref,
     akt_ref, avt_ref, bkt_ref, bvt_ref, ckt_ref, cvt_ref, lft_ref) = refs[6 + n_alias:]
    proj = _projector(x_ref, gn_ref, w_ref)
    qat_ref[...] = jnp.transpose(_pair_rms(proj(0), gq_ref[:, :GROUP_W]) * Q_SCALE).astype(BF16)
    qb_ref[...] = (_pair_rms(proj(1), gq_ref[:, GROUP_W:]) * Q_SCALE).astype(BF16)
    qc_ref[...] = (proj(2) * Q_SCALE).astype(BF16)
    ka = _pair_rms(proj(3), gk_ref[:, :GROUP_W])
    ka_ref[...] = ka.astype(BF16)
    akt_ref[0, 0] = jnp.transpose(ka)
    kb = _pair_rms(proj(4), gk_ref[:, GROUP_W:])
    kb_ref[...] = kb.astype(BF16)
    kc = proj(5)
    kc_ref[...] = kc.astype(BF16)
    ckt_ref[0, 0] = jnp.transpose(kc)[:W_C]
    vat = jnp.transpose(proj(6))
    avt_ref[0, 0] = vat
    for j in range(vat_ref.shape[1]):
        vat_ref[0, j] = vat[:, j * ATT_TILE:(j + 1) * ATT_TILE].astype(BF16)
    vb = proj(7)
    vb_ref[...] = vb.astype(BF16)
    vc = proj(8)
    vc_ref[...] = vc.astype(BF16)
    cvt_ref[0, 0] = jnp.transpose(vc)[:W_C]

    @pl.when(pl.program_id(1) == pl.num_programs(1) - 1)
    def _():
        bkt_ref[0, 0] = jnp.transpose(kb)[:W_B]
        bvt_ref[0, 0] = jnp.transpose(vb)[:W_B]

    lf = _log_forget(proj(9, LANES), bf_ref)
    logf_ref[...] = lf
    lft_ref[0] = jnp.transpose(lf)[:H_A]


def _inproj_prompt(x, lw, layer, depth, nb, s, prev_rows):
    tm = PROJ_TILE
    nt = s // tm
    t = nb * s
    per = tm // ATT_TILE
    row = lambda w: pl.BlockSpec((tm, w), lambda b, i: (b * nt + i, 0))
    full = lambda a: pl.BlockSpec(a.shape, lambda b, i: (0,) * a.ndim)
    feat = lambda w: pl.BlockSpec((1, 1, w, tm), lambda b, i: (layer, b, 0, i))
    band = lambda w: pl.BlockSpec((1, 1, w, BAND), lambda b, i: (layer, b, 0, 0))
    tok = jax.ShapeDtypeStruct((t, GROUP_W), BF16)
    out_shape = [
        jax.ShapeDtypeStruct((GROUP_W, t), BF16), tok,
        jax.ShapeDtypeStruct((nb, s // ATT_TILE, GROUP_W, ATT_TILE), BF16),
        tok, tok, tok, tok, tok, tok, jax.ShapeDtypeStruct((t, LANES), F32),
        jax.ShapeDtypeStruct((depth, nb, W_A, s), F32), jax.ShapeDtypeStruct((depth, nb, W_A, s), F32),
        jax.ShapeDtypeStruct((depth, nb, W_B, BAND), F32), jax.ShapeDtypeStruct((depth, nb, W_B, BAND), F32),
        jax.ShapeDtypeStruct((depth, nb, W_C, s), F32), jax.ShapeDtypeStruct((depth, nb, W_C, s), F32),
        jax.ShapeDtypeStruct((depth, H_A, t), F32)]
    out_specs = [
        pl.BlockSpec((GROUP_W, tm), lambda b, i: (0, b * nt + i)), row(GROUP_W),
        pl.BlockSpec((1, per, GROUP_W, ATT_TILE), lambda b, i: (b, i, 0, 0)),
        row(GROUP_W), row(GROUP_W), row(GROUP_W), row(GROUP_W), row(GROUP_W), row(GROUP_W), row(LANES),
        feat(W_A), feat(W_A), band(W_B), band(W_B), feat(W_C), feat(W_C),
        pl.BlockSpec((1, H_A, tm), lambda b, i: (layer, 0, b * nt + i))]
    ins = [x, lw['norm_mix'], lw['w_in'], lw['gq'], lw['gk'], lw['b_f']]
    in_specs = [row(D_MODEL)] + [full(a) for a in ins[1:]]
    aliases = {}
    n_alias = 0
    if prev_rows is not None:
        n_alias = N_PROMPT_ROWS
        first_row_out = len(out_shape) - N_PROMPT_ROWS
        for j, a in enumerate(prev_rows):
            aliases[len(ins)] = first_row_out + j
            ins.append(a)
            in_specs.append(pl.BlockSpec(memory_space=pl.ANY))
    return pl.pallas_call(
        functools.partial(_inproj_prompt_kernel, n_alias=n_alias), grid=(nb, nt),
        in_specs=in_specs, out_specs=out_specs, out_shape=out_shape,
        input_output_aliases=aliases,
        compiler_params=_cparams("parallel", "arbitrary"), name="inproj_prompt")(*ins)


def _forget_aug_kernel(logf_ref, kaug_ref, qaug_ref, *, nblk):
    tb = ATT_TILE
    r = lax.broadcasted_iota(jnp.int32, (tb, tb), 0)
    c = lax.broadcasted_iota(jnp.int32, (tb, tb), 1)
    tri = jnp.where(c <= r, 1.0, 0.0).astype(BF16)
    src = lax.broadcasted_iota(jnp.int32, (LANES, LANES), 0)
    dst = lax.broadcasted_iota(jnp.int32, (LANES, LANES), 1)

    def place(slot):
        return jnp.where((dst == AUG_SLOTS * src + slot) & (src < H_A), 1.0, 0.0).astype(BF16)

    lane = _lane_iota()
    slot = jnp.bitwise_and(lane, AUG_SLOTS - 1)
    used = lane < AUG_SLOTS * H_A
    key_const = jnp.where(used & (slot >= 3) & (slot < 6), 1.0, 0.0)
    qry_const = jnp.where(used & (slot < 3), -1.0, 0.0)
    carry = jnp.zeros((1, LANES), F32)
    for blk in range(nblk):
        rows = slice(blk * tb, (blk + 1) * tb)
        hi, mid, lo = _split3(logf_ref[rows, :])
        cs = (_dot(tri, hi) + _dot(tri, mid) + _dot(tri, lo)) + carry
        carry = cs[tb - 1:tb, :]
        hi, mid, lo = _split3(cs * LOG2E)
        kaug = _dot(hi, place(0)) + _dot(mid, place(1)) + _dot(lo, place(2)) + key_const
        kaug_ref[rows, :] = kaug.astype(BF16)
        qaug = _dot(hi, place(3)) + _dot(mid, place(4)) + _dot(lo, place(5)) + qry_const
        qaug_ref[:, rows] = jnp.transpose(qaug).astype(BF16)


def _forget_aug(logf, nb, s):
    return pl.pallas_call(
        functools.partial(_forget_aug_kernel, nblk=s // ATT_TILE), grid=(nb,),
        in_specs=[pl.BlockSpec((s, LANES), lambda b: (b, 0))],
        out_specs=[pl.BlockSpec((s, LANES), lambda b: (b, 0)), pl.BlockSpec((LANES, s), lambda b: (0, b))],
        out_shape=[jax.ShapeDtypeStruct((nb * s, LANES), BF16), jax.ShapeDtypeStruct((LANES, nb * s), BF16)],
        compiler_params=_cparams("parallel"), name="forget_aug")(logf)


def _suffix_sum_kernel(x_ref, o_ref):
    p = x_ref.shape[1]
    j = lax.broadcasted_iota(jnp.int32, (p, p), 0)
    s = lax.broadcasted_iota(jnp.int32, (p, p), 1)
    tri = jnp.where(j > s, 1.0, 0.0).astype(BF16)
    hi, mid, lo = _split3(x_ref[...])
    o_ref[...] = _dot(hi, tri) + _dot(mid, tri) + _dot(lo, tri)


def _suffix_sum(x):
    return pl.pallas_call(
        _suffix_sum_kernel, out_shape=jax.ShapeDtypeStruct(x.shape, F32),
        compiler_params=pltpu.CompilerParams(vmem_limit_bytes=VMEM_LIMIT), name="suffix_sum")(x)


def _attn_a_kernel(qt_ref, k_ref, vt_ref, kaug_ref, qaug_ref, o_ref, qp_ref, m_ref, l_ref, acc_ref):
    tq = ATT_TILE
    qi = pl.program_id(1)
    row = lax.broadcasted_iota(jnp.int32, (PAIR_W, 1), 0)
    aug = qaug_ref[...]
    for h in range(H_A):
        q2 = qt_ref[_pair_slice(h), :]
        half = (row < HEAD_DIM) if h % 2 == 0 else (row >= HEAD_DIM)
        qp_ref[h, :PAIR_W, :] = jnp.where(half, q2, jnp.zeros_like(q2))
        mine = (row >= AUG_SLOTS * h) & (row < AUG_SLOTS * (h + 1))
        qp_ref[h, PAIR_W:, :] = jnp.where(mine, aug, jnp.zeros_like(aug))
    m_ref[...] = jnp.full(m_ref.shape, NEG_INF, F32)
    l_ref[...] = jnp.zeros(l_ref.shape, F32)
    acc_ref[...] = jnp.zeros(acc_ref.shape, F32)

    def step(kt, masked):
        ks = pl.multiple_of(kt * tq, tq)
        ka = kaug_ref[pl.ds(ks, tq), :]
        tile_scores = [_dot(jnp.concatenate([k_ref[pl.ds(ks, tq), _pair_slice(h)], ka], axis=1), qp_ref[h])
                       for h in range(H_A)]
        probs, alphas = [], []
        for h in range(H_A):
            s = tile_scores[h]
            if masked:
                key = lax.broadcasted_iota(jnp.int32, (tq, tq), 0)
                qry = lax.broadcasted_iota(jnp.int32, (tq, tq), 1)
                s = jnp.where(key <= qry, s, NEG_INF)
            m_prev = m_ref[h]
            m_new = jnp.maximum(m_prev, jnp.max(s, axis=0, keepdims=True))
            alpha = jnp.exp2(m_prev - m_new)
            p = jnp.exp2(s - m_new)
            l_ref[h] = alpha * l_ref[h] + jnp.sum(p, axis=0, keepdims=True)
            m_ref[h] = m_new
            alphas.append(alpha)
            probs.append(p.astype(BF16))
        for h in range(H_A):
            acc_ref[h] = alphas[h] * acc_ref[h] + _dot(vt_ref[0, kt, _pair_slice(h), :], probs[h])

    def body(kt, carry):
        step(kt, False)
        return carry

    lax.fori_loop(0, qi, body, 0)
    step(qi, True)
    for pair in range(N_PAIRS):
        h = 2 * pair
        out_t = jnp.where(row < HEAD_DIM, acc_ref[h] * (1.0 / l_ref[h]), acc_ref[h + 1] * (1.0 / l_ref[h + 1]))
        o_ref[:, pair * PAIR_W:(pair + 1) * PAIR_W] = jnp.transpose(out_t)


def _attn_a(qat, ka, vat, kaug, qaug, nb, s):
    nq = s // ATT_TILE
    tq = ATT_TILE
    return pl.pallas_call(
        _attn_a_kernel, grid=(nb, nq),
        in_specs=[pl.BlockSpec((GROUP_W, tq), lambda b, i: (0, b * nq + i)),
                  pl.BlockSpec((s, GROUP_W), lambda b, i: (b, 0)),
                  pl.BlockSpec((1, nq, GROUP_W, tq), lambda b, i: (b, 0, 0, 0)),
                  pl.BlockSpec((s, LANES), lambda b, i: (b, 0)),
                  pl.BlockSpec((LANES, tq), lambda b, i: (0, b * nq + i))],
        out_specs=pl.BlockSpec((tq, GROUP_W), lambda b, i: (b * nq + i, 0)),
        out_shape=jax.ShapeDtypeStruct((nb * s, GROUP_W), F32),
        scratch_shapes=[pltpu.VMEM((H_A, 2 * PAIR_W, tq), BF16), pltpu.VMEM((H_A, 1, tq), F32),
                        pltpu.VMEM((H_A, 1, tq), F32), pltpu.VMEM((H_A, PAIR_W, tq), F32)],
        compiler_params=_cparams("parallel", "arbitrary"), name="attn_a")(qat, ka, vat, kaug, qaug)


def _masked_q(q_ref, qm_ref, n_heads):
    for h in range(n_heads):
        q2 = q_ref[:, _pair_slice(h)]
        qm_ref[h] = jnp.where(_half_mask(h % 2), q2, jnp.zeros_like(q2))


def _attn_b_kernel(q_ref, k_ref, v_ref, tab_ref, o_ref):
    tq = ATT_TILE
    win = tq + BAND
    i = pl.program_id(1)
    var = jnp.minimum(i, BAND // tq)
    ws = pl.multiple_of(jnp.maximum(i * tq - BAND, 0), tq)
    heads = range(H_B)
    scores = []
    for h in heads:
        q2 = q_ref[:, _pair_slice(h)]
        qm = jnp.where(_half_mask(h % 2), q2, jnp.zeros_like(q2))
        scores.append(_dot_nt(qm, k_ref[pl.ds(ws, win), _pair_slice(h)]) + tab_ref[var, h])
    probs, norms = [], []
    for h in heads:
        p = jnp.exp2(scores[h] - jnp.max(scores[h], axis=-1, keepdims=True))
        norms.append(1.0 / jnp.sum(p, axis=-1, keepdims=True))
        probs.append(p.astype(BF16))
    res = [_dot(probs[h], v_ref[pl.ds(ws, win), _pair_slice(h)]) * norms[h] for h in heads]
    res.append(jnp.zeros((tq, PAIR_W), F32))
    first = _half_mask(0)
    for pair in range(N_PAIRS):
        o_ref[:, pair * PAIR_W:(pair + 1) * PAIR_W] = jnp.where(first, res[2 * pair], res[2 * pair + 1])


def _suffix_tri(n):
    j = lax.broadcasted_iota(jnp.int32, (2 * n, n), 0)
    s = lax.broadcasted_iota(jnp.int32, (2 * n, n), 1)
    return jnp.where(jnp.where(j >= n, j - n, j) >= s, 1.0, 0.0).astype(BF16)


def _stick_scores(z, tri, seen):
    sp = _softplus2(z)
    if seen is not None:
        sp = jnp.where(seen, sp, 0.0)
    hi, lo = _split2(sp)
    n = sp.shape[1]
    if n % LANES == 0:
        return _dot(jnp.concatenate([hi, lo], axis=1), tri)
    return _dot(hi, tri[:n]) + _dot(lo, tri[:n])


def _stick_weights(z, s_in, carry, seen):
    a = jnp.exp2(z - (s_in + carry))
    if seen is not None:
        a = jnp.where(seen, a, 0.0)
    return a.astype(BF16)


def _attn_c_kernel(q_ref, k_ref, v_ref, o_ref, qm_ref, carry_ref, acc_ref):
    tq = ATT_TILE
    qi = pl.program_id(1)
    _masked_q(q_ref, qm_ref, H_C)
    carry_ref[...] = jnp.zeros(carry_ref.shape, F32)
    acc_ref[...] = jnp.zeros(acc_ref.shape, F32)
    heads = range(H_C)

    def step(kt, diag):
        ks = pl.multiple_of(kt * tq, tq)
        zs = [_dot_nt(qm_ref[h], k_ref[pl.ds(ks, tq), _pair_slice(h)]) for h in heads]
        tri = _suffix_tri(tq)
        seen = None
        if diag:
            r = lax.broadcasted_iota(jnp.int32, (tq, tq), 0)
            c = lax.broadcasted_iota(jnp.int32, (tq, tq), 1)
            seen = c < r
        sums = [_stick_scores(zs[h], tri, seen) for h in heads]
        weights = []
        for h in heads:
            carry = carry_ref[h][:, 0:1]
            weights.append(_stick_weights(zs[h], sums[h], carry, seen))
            carry_ref[h] = jnp.broadcast_to(carry + sums[h][:, 0:1], (tq, LANES))
        for h in heads:
            acc_ref[h] += _dot(weights[h], v_ref[pl.ds(ks, tq), _pair_slice(h)])

    def body(j, c):
        step(qi - 1 - j, False)
        return c

    step(qi, True)
    lax.fori_loop(0, qi, body, 0)
    first = _half_mask(0)
    for pair in range(N_PAIRS):
        h = 2 * pair
        second = acc_ref[h + 1] if h + 1 < H_C else jnp.zeros((tq, PAIR_W), F32)
        o_ref[:, pair * PAIR_W:(pair + 1) * PAIR_W] = jnp.where(first, acc_ref[h], second)


def _prompt_attention(kernel, name, q, k, v, extra, extra_specs, scratch, nb, s):
    nq = s // ATT_TILE
    qspec = pl.BlockSpec((ATT_TILE, GROUP_W), lambda b, i: (b * nq + i, 0))
    kvspec = pl.BlockSpec((s, GROUP_W), lambda b, i: (b, 0))
    return pl.pallas_call(
        kernel, grid=(nb, nq),
        in_specs=[qspec, kvspec, kvspec] + extra_specs,
        out_specs=qspec, out_shape=jax.ShapeDtypeStruct((nb * s, GROUP_W), F32),
        scratch_shapes=scratch,
        compiler_params=_cparams("parallel", "arbitrary"), name=name)(q, k, v, *extra)


def _sample_heads(n_heads, q_ref, width):
    out = []
    for h in range(n_heads):
        lo = (h // 2) * PAIR_W
        span = slice(lo, min(lo + PAIR_W, width))
        q2 = q_ref[:, span]
        if span.stop - span.start == PAIR_W:
            q2 = jnp.where(_half_mask(h % 2), q2, jnp.zeros_like(q2))
        out.append((h, span, q2))
    return out


def _store_heads(o_ref, res, n_heads):
    for pair in range(N_PAIRS):
        lo = pair * PAIR_W
        h = 2 * pair
        if h + 1 < n_heads:
            o_ref[:, lo:lo + PAIR_W] = jnp.where(_half_mask(0), res[h], res[h + 1])
        else:
            o_ref[:, lo:lo + HEAD_DIM] = res[h]
            o_ref[:, lo + HEAD_DIM:lo + PAIR_W] = jnp.zeros_like(res[h])


def _joint_softmax_pv(s_c, s_n, vt_c, v_n):
    m = jnp.maximum(jnp.max(s_c, axis=-1, keepdims=True), jnp.max(s_n, axis=-1, keepdims=True))
    p_c = jnp.exp2(s_c - m)
    p_n = jnp.exp2(s_n - m)
    l = jnp.sum(p_c, axis=-1, keepdims=True) + jnp.sum(p_n, axis=-1, keepdims=True)
    return (_dot_nt(p_c.astype(BF16), vt_c) + _dot(p_n.astype(BF16), v_n)) / l


def _sample_attn_kernel(qa_ref, qb_ref, qc_ref, kan_ref, kbn_ref, kcn_ref, van_ref, vbn_ref, vcn_ref,
                        cak_ref, cav_ref, cbk_ref, cbv_ref, cck_ref, ccv_ref,
                        rsum_ref, logf_ref, tabc_ref, tabn_ref,
                        oa_ref, ob_ref, oc_ref):
    t = qa_ref.shape[0]
    past = cak_ref.shape[3]
    r = lax.broadcasted_iota(jnp.int32, (t, t), 0)
    c = lax.broadcasted_iota(jnp.int32, (t, t), 1)

    ltri = jnp.where(c <= r, 1.0, 0.0).astype(BF16)
    hi, mid, lo = _split3(logf_ref[...])
    pcol = (_dot(ltri, hi) + _dot(ltri, mid) + _dot(ltri, lo)) * LOG2E
    prow = jnp.transpose(jnp.concatenate([pcol, jnp.zeros((LANES - t, LANES), F32)], axis=0))
    heads = _sample_heads(H_A, qa_ref, W_A)
    kts = {sp.start: cak_ref[0, 0, sp, :].astype(BF16) for _, sp, _ in heads}
    vts = {sp.start: cav_ref[0, 0, sp, :].astype(BF16) for _, sp, _ in heads}
    s_c = [_dot(qm, kts[sp.start]) + (pcol[:, h:h + 1] + rsum_ref[0, 0, h:h + 1, :] * LOG2E)
           for h, sp, qm in heads]
    s_n = [jnp.where(c <= r, _dot_nt(qm, kan_ref[:, sp]) + (pcol[:, h:h + 1] - prow[h:h + 1, 0:t]), NEG_INF)
           for h, sp, qm in heads]
    _store_heads(oa_ref, [_joint_softmax_pv(s_c[h], s_n[h], vts[sp.start], van_ref[:, sp])
                          for h, sp, _ in heads], H_A)

    heads = _sample_heads(H_B, qb_ref, W_B)
    kts = {sp.start: cbk_ref[0, 0, sp, :].astype(BF16) for _, sp, _ in heads}
    vts = {sp.start: cbv_ref[0, 0, sp, :].astype(BF16) for _, sp, _ in heads}
    s_c = [_dot(qm, kts[sp.start]) + tabc_ref[h] for h, sp, qm in heads]
    s_n = [_dot_nt(qm, kbn_ref[:, sp]) + tabn_ref[h] for h, sp, qm in heads]
    _store_heads(ob_ref, [_joint_softmax_pv(s_c[h], s_n[h], vts[sp.start], vbn_ref[:, sp])
                          for h, sp, _ in heads], H_B)

    heads = _sample_heads(H_C, qc_ref, W_C)
    nblk = past // ATT_TILE
    tri_n = _suffix_tri(t)
    tri_c = _suffix_tri(ATT_TILE)
    seen = c < r
    kts = {sp.start: cck_ref[0, 0, sp, :].astype(BF16) for _, sp, _ in heads}
    vts = {sp.start: ccv_ref[0, 0, sp, :].astype(BF16) for _, sp, _ in heads}
    z_n = [_dot_nt(qm, kcn_ref[:, sp]) for _, sp, qm in heads]
    z_c = [_dot(qm, kts[sp.start]) for _, sp, qm in heads]
    sum_n = [_stick_scores(z, tri_n, seen) for z in z_n]
    sum_c = [[_stick_scores(z[:, b * ATT_TILE:(b + 1) * ATT_TILE], tri_c, None) for b in range(nblk)]
             for z in z_c]
    res = []
    for h, sp, _ in heads:
        acc = _dot(_stick_weights(z_n[h], sum_n[h], 0.0, seen), vcn_ref[:, sp])
        carry = sum_n[h][:, 0:1]
        for b in reversed(range(nblk)):
            cols = slice(b * ATT_TILE, (b + 1) * ATT_TILE)
            a = _stick_weights(z_c[h][:, cols], sum_c[h][b], carry, None)
            acc = acc + _dot_nt(a, vts[sp.start][:, cols])
            carry = carry + sum_c[h][b][:, 0:1]
        res.append(acc)
    _store_heads(oc_ref, res, H_C)


def _sample_attention(proj, caches, layer, rsum, tabc, tabn, nb, t):
    qa, qb, qc, ka, kb, kc, va, vb, vc = proj[:9]
    logf = proj[15]
    new = pl.BlockSpec((t, GROUP_W), lambda b: (b, 0))
    cache = lambda a: pl.BlockSpec((1, 1) + a.shape[2:], lambda b: (layer, b, 0, 0))
    full = lambda a: pl.BlockSpec(a.shape, lambda b: (0,) * a.ndim)
    return pl.pallas_call(
        _sample_attn_kernel, grid=(nb,),
        in_specs=[new] * 9 + [cache(a) for a in caches] + [
            cache(rsum), pl.BlockSpec((t, LANES), lambda b: (b, 0)), full(tabc), full(tabn)],
        out_specs=[new] * 3,
        out_shape=[jax.ShapeDtypeStruct((nb * t, GROUP_W), F32)] * 3,
        compiler_params=_cparams("parallel"), name="sample_attn")(
            qa, qb, qc, ka, kb, kc, va, vb, vc, *caches, rsum, logf, tabc, tabn)


def _masked_max(x, mask):
    return jnp.max(jnp.where(mask, x, -jnp.inf), axis=-1, keepdims=True)


def _first_lane(mask, lane):
    return jnp.min(jnp.where(mask, lane, float(LANES)), axis=-1, keepdims=True)


def _route(logits):
    lane = _lane_iota().astype(F32)
    is_g = (lane >= N_EXPERTS) & (lane < N_EXPERTS + N_GROUPS)
    gmax = _masked_max(logits, is_g)
    p_g = 1.0 / jnp.sum(jnp.where(is_g, jnp.exp(logits - gmax), 0.0), axis=-1, keepdims=True)
    g_sel = _first_lane(is_g & (logits == gmax), lane) - N_EXPERTS
    lo = g_sel * EXPERTS_PER_GROUP
    in_g = (lane >= lo) & (lane < lo + EXPERTS_PER_GROUP)
    l1 = _masked_max(logits, in_g)
    i1 = _first_lane(in_g & (logits == l1), lane)
    rest = in_g & (lane != i1)
    l2 = _masked_max(logits, rest)
    i2 = _first_lane(rest & (logits == l2), lane)
    e2 = jnp.exp(l2 - l1)
    w1 = p_g / (1.0 + e2)
    w2 = p_g * e2 / (1.0 + e2)
    gates = jnp.where(lane == i1, w1, jnp.where(lane == i2, w2, 0.0))
    return jnp.where(lane == g_sel + N_EXPERTS, 1.0, gates)


def _post_kernel(oa_ref, ob_ref, oc_ref, h_ref, gmix_ref, wout_ref, nffn_ref, wrh_ref, wrl_ref, br_ref,
                 h1_ref, xn_ref, gates_ref, xt_ref, gt_ref):
    def gnorm(o, width):
        ms = jnp.sum(o * o, axis=-1, keepdims=True) * (1.0 / width)
        return o * lax.rsqrt(ms + EPS)

    h1 = h_ref[...]
    for g, (o_ref, width) in enumerate(((oa_ref, W_A), (ob_ref, W_B), (oc_ref, W_C))):
        merged = (gnorm(o_ref[...], width) * gmix_ref[g]).astype(BF16)
        h1 = h1 + _dot(merged, wout_ref[g])
    h1_ref[...] = h1
    xf = _rms(h1, nffn_ref[...])
    hi, lo = _split2(xf)
    xn_ref[...] = hi
    xt_ref[...] = jnp.transpose(xf).astype(BF16)
    logits = _dot(hi, wrh_ref[...]) + _dot(lo, wrh_ref[...]) + _dot(hi, wrl_ref[...]) + br_ref[...]
    gates = _route(logits)
    gates_ref[...] = gates
    gt_ref[...] = jnp.transpose(gates)


def _post(oa, ob, oc, h, lw, tm):
    t = h.shape[0]
    row = lambda w: pl.BlockSpec((tm, w), lambda i: (i, 0))
    full = lambda a: pl.BlockSpec(a.shape, lambda i: (0,) * a.ndim)
    ws = (lw['g_mix'], lw['w_out'], lw['norm_ffn'], lw['w_r_hi'], lw['w_r_lo'], lw['b_r'])
    return pl.pallas_call(
        _post_kernel, grid=(t // tm,),
        in_specs=[row(GROUP_W)] * 3 + [row(D_MODEL)] + [full(a) for a in ws],
        out_specs=[row(D_MODEL), row(D_MODEL), row(LANES),
                   pl.BlockSpec((D_MODEL, tm), lambda i: (0, i)), pl.BlockSpec((LANES, tm), lambda i: (0, i))],
        out_shape=[jax.ShapeDtypeStruct((t, D_MODEL), F32), jax.ShapeDtypeStruct((t, D_MODEL), BF16),
                   jax.ShapeDtypeStruct((t, LANES), F32),
                   jax.ShapeDtypeStruct((D_MODEL, t), BF16), jax.ShapeDtypeStruct((LANES, t), F32)],
        compiler_params=_cparams("parallel"), name="post")(oa, ob, oc, h, *ws)


def _moe_kernel(xn_ref, gates_ref, wg_ref, wu_ref, wd_ref, h1_ref, p_ref, wple_ref, nple_ref, wpg_ref,
                out_ref, acc_ref):
    e = pl.program_id(1)

    @pl.when(e == 0)
    def _():
        acc_ref[...] = jnp.zeros_like(acc_ref)

    x = xn_ref[...]
    g = _dot(x, wg_ref[0])
    u = _dot(x, wu_ref[0])
    gate = jnp.sum(jnp.where(_lane_iota() == e, gates_ref[...], 0.0), axis=-1, keepdims=True)
    act = (g * _sigmoid(g)) * u * gate
    acc_ref[...] += _dot(act.astype(BF16), wd_ref[0])

    @pl.when(e == N_EXPERTS - 1)
    def _():
        h2 = h1_ref[...] + acc_ref[...]
        gate_ple = _sigmoid(_dot(_rms(h2, nple_ref[...]).astype(BF16), wpg_ref[...]))
        out_ref[...] = h2 + _dot(p_ref[0].astype(BF16), wple_ref[...]) * gate_ple


def _ple_gate(h2, p, wple_ref, nple_ref, wpg_ref):
    gate_ple = _sigmoid(_dot(_rms(h2, nple_ref[...]).astype(BF16), wpg_ref[...]))
    return h2 + _dot(p.astype(BF16), wple_ref[...]) * gate_ple


def _moe_grouped_kernel(xt_ref, gt_ref, ltri_ref, wg_ref, wu_ref, wd_ref, h1_ref, p_ref, wple_ref, nple_ref,
                        wpg_ref, out_ref, acct_ref, slot_col_ref, slot_row_ref, gsplit_ref, *, cap):
    g = pl.program_id(1)
    nb = xt_ref.shape[1]
    row = lax.broadcasted_iota(jnp.int32, (LANES, 1), 0)

    @pl.when(g == 0)
    def _():
        acct_ref[...] = jnp.zeros_like(acct_ref)
        gt = gt_ref[...]
        member_t = jnp.where((row >= N_EXPERTS) & (row < N_EXPERTS + N_GROUPS), gt, 0.0)
        member = jnp.transpose(member_t)
        slot_row_ref[...] = jnp.where(member_t > 0.0, _dot_nt(member_t.astype(BF16), ltri_ref[...]), -1.0)
        slot_col_ref[...] = jnp.where(member > 0.0, _dot(ltri_ref[...], member.astype(BF16)), -1.0)
        hi, mid, lo = _split3(gt)
        gsplit_ref[0] = hi
        gsplit_ref[1] = mid
        gsplit_ref[2] = lo

    slot_row = slot_row_ref[pl.ds(N_EXPERTS + g, 1), :]
    slot_col = jnp.sum(jnp.where(_lane_iota() == N_EXPERTS + g, slot_col_ref[...], 0.0),
                       axis=-1, keepdims=True)
    count = jnp.sum(jnp.where(slot_row >= 0.0, 1.0, 0.0))
    d_expert = D_MODEL // EXPERTS_PER_GROUP

    def experts(x_t, gate_rows):
        hg = _dot(wg_ref[0], x_t)
        act = (hg * _sigmoid(hg)) * _dot(wu_ref[0], x_t)
        parts = []
        for e in range(EXPERTS_PER_GROUP):
            gate_e = jnp.sum(jnp.where(row == g * EXPERTS_PER_GROUP + e, gate_rows, 0.0), axis=0, keepdims=True)
            parts.append(act[e * d_expert:(e + 1) * d_expert] * gate_e)
        return _dot(wd_ref[0], jnp.concatenate(parts, axis=0).astype(BF16))

    unpack = jnp.where(_lane_iota(cap).astype(F32) == slot_col, 1.0, 0.0).astype(BF16)
    pack = jnp.where(lax.broadcasted_iota(jnp.int32, (cap, 1), 0).astype(F32) == slot_row, 1.0, 0.0).astype(BF16)
    xs_t = _dot(xt_ref[...], unpack).astype(BF16)
    gs_t = _dot(gsplit_ref[0], unpack) + _dot(gsplit_ref[1], unpack) + _dot(gsplit_ref[2], unpack)
    acct_ref[...] += _dot(experts(xs_t, gs_t).astype(BF16), pack)

    @pl.when(count > cap)
    def _():
        acct_ref[...] += jnp.where(slot_row >= cap, experts(xt_ref[...], gt_ref[...]), 0.0)

    @pl.when(g == N_GROUPS - 1)
    def _():
        h2 = h1_ref[...] + jnp.transpose(acct_ref[...])
        out_ref[...] = _ple_gate(h2, p_ref[0], wple_ref, nple_ref, wpg_ref)


def _moe_grouped_tail_kernel(*refs, cap):
    _moe_grouped_kernel(*refs[:11], *refs[12:], cap=cap)


def _moe_grouped_call(xt, gt, h1, p_all, layer, lw, nb, first, n_blocks, prev_out):
    t = h1.shape[0]
    once = pl.Buffered(1)
    full = lambda a: pl.BlockSpec(a.shape, lambda i, g: (0,) * a.ndim, pipeline_mode=once)
    grp = lambda a: pl.BlockSpec((1,) + a.shape[1:], lambda i, g: (g, 0, 0))
    r = lax.broadcasted_iota(jnp.int32, (nb, nb), 0)
    c = lax.broadcasted_iota(jnp.int32, (nb, nb), 1)
    ltri = jnp.where(c < r, 1.0, 0.0).astype(BF16)
    ins = [xt, gt, ltri, lw['w_tg'], lw['w_tu'], lw['w_td'], h1, p_all, lw['w_ple'], lw['norm_ple'], lw['w_pg']]
    in_specs = [pl.BlockSpec((D_MODEL, nb), lambda i, g: (0, first + i)),
                pl.BlockSpec((LANES, nb), lambda i, g: (0, first + i)),
                full(ltri), grp(lw['w_tg']), grp(lw['w_tu']), grp(lw['w_td']),
                pl.BlockSpec((nb, D_MODEL), lambda i, g: (first + i, 0), pipeline_mode=once),
                pl.BlockSpec((1, nb, p_all.shape[2]), lambda i, g: (layer, first + i, 0), pipeline_mode=once),
                full(lw['w_ple']), full(lw['norm_ple']), full(lw['w_pg'])]
    body, aliases = _moe_grouped_kernel, {}
    if prev_out is not None:
        body, aliases = _moe_grouped_tail_kernel, {len(ins): 0}
        ins.append(prev_out)
        in_specs.append(pl.BlockSpec(memory_space=pl.ANY))
    return pl.pallas_call(
        functools.partial(body, cap=min(MOE_CAP, nb)), grid=(n_blocks, N_GROUPS),
        in_specs=in_specs, out_specs=pl.BlockSpec((nb, D_MODEL), lambda i, g: (first + i, 0)),
        out_shape=jax.ShapeDtypeStruct((t, D_MODEL), F32), input_output_aliases=aliases,
        scratch_shapes=[pltpu.VMEM((D_MODEL, nb), F32), pltpu.VMEM((nb, LANES), F32),
                        pltpu.VMEM((LANES, nb), F32), pltpu.VMEM((3, LANES, nb), BF16)],
        compiler_params=pltpu.CompilerParams(dimension_semantics=("parallel", "arbitrary"),
                                             vmem_limit_bytes=MOE_VMEM_LIMIT),
        name="moe_grouped")(*ins)


def _moe_grouped(xt, gt, h1, p_all, layer, lw):
    t = h1.shape[0]
    n_main, tail = divmod(t, MOE_BLOCK)
    out = _moe_grouped_call(xt, gt, h1, p_all, layer, lw, MOE_BLOCK, 0, n_main, None)
    if tail:
        assert tail % LANES == 0 and (n_main * MOE_BLOCK) % tail == 0
        out = _moe_grouped_call(xt, gt, h1, p_all, layer, lw, tail, n_main * MOE_BLOCK // tail, 1, out)
    return out


def _moe(xn, gates, h1, p_all, layer, lw, tm):
    t = xn.shape[0]
    row = lambda w: pl.BlockSpec((tm, w), lambda i, e: (i, 0))
    full = lambda a: pl.BlockSpec(a.shape, lambda i, e: (0,) * a.ndim)
    exp = lambda a: pl.BlockSpec((1,) + a.shape[1:], lambda i, e: (e, 0, 0))
    return pl.pallas_call(
        _moe_kernel, grid=(t // tm, N_EXPERTS),
        in_specs=[row(D_MODEL), row(LANES), exp(lw['w_eg']), exp(lw['w_eu']), exp(lw['w_ed']),
                  row(D_MODEL), pl.BlockSpec((1, tm, p_all.shape[2]), lambda i, e: (layer, i, 0)),
                  full(lw['w_ple']), full(lw['norm_ple']), full(lw['w_pg'])],
        out_specs=row(D_MODEL), out_shape=jax.ShapeDtypeStruct((t, D_MODEL), F32),
        scratch_shapes=[pltpu.VMEM((tm, D_MODEL), F32)],
        compiler_params=_cparams("parallel", "arbitrary"), name="moe")(
            xn, gates, lw['w_eg'], lw['w_eu'], lw['w_ed'], h1, p_all, lw['w_ple'], lw['norm_ple'], lw['w_pg'])


def _regroup_out_rows(wt):
    z = jnp.zeros((HEAD_DIM, wt.shape[1]), wt.dtype)
    return jnp.concatenate([wt[:W_A], wt[W_A:W_A + W_B], z, wt[W_A + W_B:], z], axis=0)


def _regroup_rows(w):
    z = jnp.zeros((HEAD_DIM, w.shape[1]), w.dtype)
    return jnp.stack([w[:W_A], jnp.concatenate([w[W_A:W_A + W_B], z], axis=0),
                      jnp.concatenate([w[W_A + W_B:], z], axis=0)])


def _pad_lanes(a, n=LANES):
    return jnp.pad(a, [(0, 0)] * (a.ndim - 1) + [(0, n - a.shape[-1])])


def _toeplitz_bias(rel_bias, off, tq, win):
    length = tq + win
    pad = length + abs(off)
    ext = jnp.flip(jnp.pad(rel_bias, ((0, 0), (pad, pad)), mode='edge'), axis=1)
    s1 = ext.shape[1] - 1 - (off + REL_CLIP + pad)
    v = jnp.concatenate([ext[:, s1:s1 + win], ext[:, s1 - tq:s1]], axis=1)
    flat = jnp.tile(v, (1, tq))[:, :tq * (length - 1)]
    return flat.reshape(-1, tq, length - 1)[:, :, :win]


def _block_toeplitz_bias(rel_bias, off, tq, win):
    nq, nk = tq // LANES, win // LANES
    blocks = {d: _toeplitz_bias(rel_bias, off - d * LANES, LANES, LANES) for d in range(-(nq - 1), nk)}
    return jnp.concatenate(
        [jnp.concatenate([blocks[b - a] for b in range(nk)], axis=2) for a in range(nq)], axis=1)


def _rel_tables(rel_bias, t_new, rows_b):
    tq, win = ATT_TILE, ATT_TILE + BAND
    q = jnp.arange(tq)[:, None]
    tabs = []
    for var in range(BAND // tq + 1):
        rel_k = jnp.arange(win)[None, :] - var * tq
        kch = jnp.floor_divide(rel_k, CHUNK)
        qch = q // CHUNK
        valid = (kch <= qch) & (kch >= qch - BAND_CHUNKS)
        tabs.append(jnp.where(valid[None], _block_toeplitz_bias(rel_bias, var * tq, tq, win), NEG_INF))
    tab_prompt = jnp.stack(tabs).astype(F32)
    tab_c = _toeplitz_bias(rel_bias, rows_b, t_new, rows_b).astype(F32)
    tab_n = _toeplitz_bias(rel_bias, 0, t_new, t_new).astype(F32)
    return tab_prompt, tab_c, tab_n


def _layer_weights(i, norm_mix, w_in, b_f, g_qa, g_ka, g_qb, g_kb, g_mix_out, w_out, norm_ffn,
                   w_rg, b_rg, w_re, b_re, w_eg, w_eu, w_ed, w_ple, norm_ple, w_pg):
    d_mix = W_A + W_B + W_C
    wt = jnp.transpose(w_in, (2, 0, 1))[:, i, :]
    w_f = wt[3 * d_mix:]
    w_big = jnp.concatenate(
        [_regroup_out_rows(wt[j * d_mix:(j + 1) * d_mix]) for j in range(3)]
        + [jnp.pad(w_f, ((0, LANES - w_f.shape[0]), (0, 0)))], axis=0).astype(BF16)
    tile6 = lambda g: jnp.tile(g, GROUP_W // HEAD_DIM)
    w_r = _pad_lanes(jnp.concatenate([w_re[i], w_rg[i]], axis=1))
    w_r_hi = w_r.astype(BF16)
    gm = g_mix_out[i]
    zpad = jnp.zeros((HEAD_DIM,), F32)
    g_mix = jnp.stack([gm[:W_A], jnp.concatenate([gm[W_A:W_A + W_B], zpad]),
                       jnp.concatenate([gm[W_A + W_B:], zpad])])[:, None, :]
    return dict(
        norm_mix=norm_mix[i][None], w_in=w_big,
        gq=jnp.concatenate([tile6(g_qa[i]), tile6(g_qb[i])])[None],
        gk=jnp.concatenate([tile6(g_ka[i]), tile6(g_kb[i])])[None],
        b_f=_pad_lanes(b_f[i])[None],
        g_mix=g_mix, w_out=_regroup_rows(w_out[i]).astype(BF16), norm_ffn=norm_ffn[i][None],
        w_r_hi=w_r_hi, w_r_lo=(w_r - w_r_hi.astype(F32)).astype(BF16),
        b_r=_pad_lanes(jnp.concatenate([b_re[i], b_rg[i]]))[None],
        w_eg=w_eg[i].astype(BF16), w_eu=w_eu[i].astype(BF16), w_ed=w_ed[i].astype(BF16),
        w_tg=_group_hidden_rows(w_eg[i]), w_tu=_group_hidden_rows(w_eu[i]), w_td=_group_hidden_cols(w_ed[i]),
        w_ple=w_ple[i].astype(BF16), norm_ple=norm_ple[i][None], w_pg=w_pg[i].astype(BF16))


def _group_hidden_rows(w):
    e, d, f = w.shape
    return jnp.transpose(w, (0, 2, 1)).reshape(N_GROUPS, EXPERTS_PER_GROUP * f, d).astype(BF16)


def _group_hidden_cols(w):
    e, f, d = w.shape
    return jnp.transpose(w.reshape(N_GROUPS, EXPERTS_PER_GROUP * f, d), (0, 2, 1)).astype(BF16)


def _token_tile(t, pref):
    return pref if t % pref == 0 else t


def _feature_major(cache):
    d, n, p, h, e = cache.shape
    return jnp.transpose(cache, (0, 1, 3, 4, 2)).reshape(d, n, h * e, p)


def _position_major(rows, heads):
    d, n, _, p = rows.shape
    return jnp.transpose(rows.reshape(d, n, heads, HEAD_DIM, p), (0, 1, 4, 2, 3))


def kernel(x_prompt, x_sample, p_prompt, p_sample, cache_a_k, cache_a_v, cache_a_logf, cache_b_k, cache_b_v, cache_c_k, cache_c_v, norm_mix, w_in, b_f, g_qa, g_ka, g_qb, g_kb, rel_bias, g_mix_out, w_out, norm_ffn, w_router_group, b_router_group, w_router_expert, b_router_expert, w_exp_gate, w_exp_up, w_exp_down, w_ple, norm_ple, w_ple_gate):
    nb, s, d = x_prompt.shape
    ns, t_new, _ = x_sample.shape
    depth = w_in.shape[0]
    past = cache_a_k.shape[2]
    rows_b = cache_b_k.shape[2]
    assert d == D_MODEL and s % PROJ_TILE == 0 and s >= BAND + ATT_TILE and past % ATT_TILE == 0
    tp, ts = nb * s, ns * t_new

    hp = x_prompt.reshape(tp, d)
    hs = x_sample.reshape(ts, d)
    pp = p_prompt.reshape(depth, tp, -1)
    ps = p_sample.reshape(depth, ts, -1)
    caches = tuple(_feature_major(c) for c in (cache_a_k, cache_a_v, cache_b_k, cache_b_v, cache_c_k, cache_c_v))
    lf_rows = jnp.transpose(cache_a_logf, (0, 3, 1, 2))
    rsum = _suffix_sum(lf_rows.reshape(depth * H_A * ns, past)).reshape(depth, H_A, ns, past)
    rsum = jnp.transpose(rsum, (0, 2, 1, 3))

    rows_p = None
    new_s = [[] for _ in range(7)]
    for i in range(depth):
        lw = _layer_weights(i, norm_mix, w_in, b_f, g_qa, g_ka, g_qb, g_kb, g_mix_out, w_out, norm_ffn,
                            w_router_group, b_router_group, w_router_expert, b_router_expert,
                            w_exp_gate, w_exp_up, w_exp_down, w_ple, norm_ple, w_ple_gate)
        tab_p, tab_c, tab_n = _rel_tables(rel_bias[i] * LOG2E, t_new, rows_b)

        pr = _inproj_prompt(hp, lw, i, depth, nb, s, rows_p)
        qat, ka, vat, qb, kb, vb, qc, kc, vc, logf = pr[:10]
        rows_p = pr[10:]
        kaug, qaug = _forget_aug(logf, nb, s)
        oa = _attn_a(qat, ka, vat, kaug, qaug, nb, s)
        ob = _prompt_attention(
            _attn_b_kernel, "attn_b", qb, kb, vb, (tab_p,),
            [pl.BlockSpec(tab_p.shape, lambda b, j: (0, 0, 0, 0))], [], nb, s)
        oc = _prompt_attention(
            _attn_c_kernel, "attn_c", qc, kc, vc, (), [],
            [pltpu.VMEM((H_C, ATT_TILE, PAIR_W), BF16), pltpu.VMEM((H_C, ATT_TILE, LANES), F32),
             pltpu.VMEM((H_C, ATT_TILE, PAIR_W), F32)], nb, s)
        h1, _, _, xt, gt = _post(oa, ob, oc, hp, lw, _token_tile(tp, 512))
        hp = _moe_grouped(xt, gt, h1, pp, i, lw)

        sr = _inproj_sample(hs, lw)
        oa, ob, oc = _sample_attention(sr, caches, i, rsum, tab_c, tab_n, ns, t_new)
        h1, xn, gates, _, _ = _post(oa, ob, oc, hs, lw, _token_tile(ts, 512))
        hs = _moe(xn, gates, h1, ps, i, lw, _token_tile(ts, 1024))
        akf, avf, bkf, bvf, ckf, cvf, logf = sr[9:]
        rows = (akf.reshape(ns, t_new, H_A, HEAD_DIM), avf.reshape(ns, t_new, H_A, HEAD_DIM),
                logf[:, :H_A].reshape(ns, t_new, H_A),
                bkf.reshape(ns, t_new, H_B, HEAD_DIM), bvf.reshape(ns, t_new, H_B, HEAD_DIM),
                ckf.reshape(ns, t_new, H_C, HEAD_DIM), cvf.reshape(ns, t_new, H_C, HEAD_DIM))
        for j in range(7):
            new_s[j].append(rows[j])

    akt, avt, bkt, bvt, ckt, cvt, lft = rows_p
    outs_p = (_position_major(akt, H_A), _position_major(avt, H_A),
              jnp.transpose(lft.reshape(depth, H_A, nb, s), (0, 2, 3, 1)),
              _position_major(bkt, H_B), _position_major(bvt, H_B),
              _position_major(ckt, H_C), _position_major(cvt, H_C))
    outs_s = [jnp.stack(r, axis=0) for r in new_s]
    return (hp.reshape(nb, s, d), hs.reshape(ns, t_new, d), *outs_p, *outs_s)
```

```python
import functools
import math

import jax
import jax.numpy as jnp
from jax import lax
from jax.experimental import pallas as pl
from jax.experimental.pallas import tpu as pltpu

F32 = jnp.float32
BF16 = jnp.bfloat16

D_MODEL = 1024
HEAD_DIM = 64
H_A, H_B, H_C = 6, 5, 5
W_A, W_B, W_C = H_A * HEAD_DIM, H_B * HEAD_DIM, H_C * HEAD_DIM
GROUP_W = 384
PAIR_W = 2 * HEAD_DIM
N_PAIRS = GROUP_W // PAIR_W
CHUNK = 64
BAND_CHUNKS = 8
BAND = BAND_CHUNKS * CHUNK
REL_CLIP = 128
N_GROUPS = 4
EXPERTS_PER_GROUP = 4
N_EXPERTS = N_GROUPS * EXPERTS_PER_GROUP
EPS = 1e-6
ATTN_SCALE = HEAD_DIM ** -0.5
LOG2E = math.log2(math.e)
Q_SCALE = ATTN_SCALE * LOG2E
NEG_INF = -1e30
LANES = 128
SUBLANES = 8
VMEM_LIMIT = 48 * 1024 * 1024
MOE_VMEM_LIMIT = 60 * 1024 * 1024
MOE_BLOCK = 768
MOE_CAP = 256

ATT_TILE = 256
PROJ_TILE = BAND
AUG_SLOTS = 8

_NT = (((1,), (1,)), ((), ()))


def _cparams(*sem):
    return pltpu.CompilerParams(dimension_semantics=sem, vmem_limit_bytes=VMEM_LIMIT)


def _dot(a, b):
    return jnp.dot(a, b, preferred_element_type=F32)


def _dot_nt(a, b):
    return lax.dot_general(a, b, _NT, preferred_element_type=F32)


def _split3(x):
    hi = x.astype(BF16)
    r1 = x - hi.astype(F32)
    mid = r1.astype(BF16)
    lo = (r1 - mid.astype(F32)).astype(BF16)
    return hi, mid, lo


def _split2(x):
    hi = x.astype(BF16)
    lo = (x - hi.astype(F32)).astype(BF16)
    return hi, lo


def _lane_iota(n=LANES):
    return lax.broadcasted_iota(jnp.int32, (1, n), 1)


def _half_mask(half):
    lane = _lane_iota()
    return (lane < HEAD_DIM) if half == 0 else (lane >= HEAD_DIM)


def _pair_slice(h):
    return slice((h // 2) * PAIR_W, (h // 2 + 1) * PAIR_W)


def _softplus2(z2):
    return jnp.maximum(z2, 0.0) + jnp.log2(1.0 + jnp.exp2(-jnp.abs(z2)))


def _sigmoid(z):
    return 1.0 / (1.0 + jnp.exp(-z))


def _rms(x, g):
    return x * lax.rsqrt(jnp.mean(x * x, axis=-1, keepdims=True) + EPS) * g


def _pair_rms(z, gain):
    first = _half_mask(0)
    outs = []
    for j in range(N_PAIRS):
        blk = z[:, j * PAIR_W:(j + 1) * PAIR_W]
        sq = blk * blk
        lo = jnp.sum(jnp.where(first, sq, 0.0), axis=-1, keepdims=True)
        hi = jnp.sum(jnp.where(first, 0.0, sq), axis=-1, keepdims=True)
        ms = jnp.where(first, lo, hi) * (1.0 / HEAD_DIM)
        outs.append(blk * lax.rsqrt(ms + EPS) * gain[:, j * PAIR_W:(j + 1) * PAIR_W])
    return jnp.concatenate(outs, axis=-1)


def _projector(x_ref, gn_ref, w_ref):
    xb = _rms(x_ref[...], gn_ref[...]).astype(BF16)
    return lambda g, width=GROUP_W: _dot_nt(xb, w_ref[g * GROUP_W:g * GROUP_W + width, :])


def _log_forget(zf, bf_ref):
    zf = zf + bf_ref[...]
    return jnp.minimum(zf, 0.0) - jnp.log(1.0 + jnp.exp(-jnp.abs(zf)))


def _inproj_sample_kernel(x_ref, gn_ref, w_ref, gq_ref, gk_ref, bf_ref,
                          qa_ref, qb_ref, qc_ref, ka_ref, kb_ref, kc_ref, va_ref, vb_ref, vc_ref,
                          akf_ref, avf_ref, bkf_ref, bvf_ref, ckf_ref, cvf_ref, logf_ref):
    proj = _projector(x_ref, gn_ref, w_ref)
    qa_ref[...] = (_pair_rms(proj(0), gq_ref[:, :GROUP_W]) * Q_SCALE).astype(BF16)
    qb_ref[...] = (_pair_rms(proj(1), gq_ref[:, GROUP_W:]) * Q_SCALE).astype(BF16)
    qc_ref[...] = (proj(2) * Q_SCALE).astype(BF16)
    for g, norm, bf_ref_, f_ref, width in (
            (3, gk_ref[:, :GROUP_W], ka_ref, akf_ref, W_A), (4, gk_ref[:, GROUP_W:], kb_ref, bkf_ref, W_B),
            (5, None, kc_ref, ckf_ref, W_C), (6, None, va_ref, avf_ref, W_A),
            (7, None, vb_ref, bvf_ref, W_B), (8, None, vc_ref, cvf_ref, W_C)):
        z = proj(g)
        if norm is not None:
            z = _pair_rms(z, norm)
        f_ref[...] = z[:, :width]
        bf_ref_[...] = z.astype(BF16)
    logf_ref[...] = _log_forget(proj(9, LANES), bf_ref)


def _inproj_sample(x, lw):
    t = x.shape[0]
    row = lambda w: pl.BlockSpec((t, w), lambda i: (0, 0))
    full = lambda a: pl.BlockSpec(a.shape, lambda i: (0,) * a.ndim)
    widths = [GROUP_W] * 9 + [W_A, W_A, W_B, W_B, W_C, W_C, LANES]
    out_shape = [jax.ShapeDtypeStruct((t, w), BF16 if j < 9 else F32) for j, w in enumerate(widths)]
    ins = (x, lw['norm_mix'], lw['w_in'], lw['gq'], lw['gk'], lw['b_f'])
    return pl.pallas_call(
        _inproj_sample_kernel, grid=(1,),
        in_specs=[row(D_MODEL)] + [full(a) for a in ins[1:]],
        out_specs=[row(w) for w in widths], out_shape=out_shape,
        compiler_params=_cparams("arbitrary"), name="inproj_sample")(*ins)


N_PROMPT_ROWS = 7


def _inproj_prompt_kernel(*refs, n_alias):
    x_ref, gn_ref, w_ref, gq_ref, gk_ref, bf_ref = refs[:6]
    (qat_ref, ka_ref, vat_ref, qb_ref, kb_ref, vb_ref, qc_ref, kc_ref, vc_ref, logf_ref,
     akt_ref, avt_ref, bkt_ref, bvt_ref, ckt_ref, cvt_ref, lft_ref) = refs[6 + n_alias:]
    proj = _projector(x_ref, gn_ref, w_ref)
    qat_ref[...] = jnp.transpose(_pair_rms(proj(0), gq_ref[:, :GROUP_W]) * Q_SCALE).astype(BF16)
    qb_ref[...] = (_pair_rms(proj(1), gq_ref[:, GROUP_W:]) * Q_SCALE).astype(BF16)
    qc_ref[...] = (proj(2) * Q_SCALE).astype(BF16)
    ka = _pair_rms(proj(3), gk_ref[:, :GROUP_W])
    ka_ref[...] = ka.astype(BF16)
    akt_ref[0, 0] = jnp.transpose(ka)
    kb = _pair_rms(proj(4), gk_ref[:, GROUP_W:])
    kb_ref[...] = kb.astype(BF16)
    kc = proj(5)
    kc_ref[...] = kc.astype(BF16)
    ckt_ref[0, 0] = jnp.transpose(kc)[:W_C]
    vat = jnp.transpose(proj(6))
    avt_ref[0, 0] = vat
    for j in range(vat_ref.shape[1]):
        vat_ref[0, j] = vat[:, j * ATT_TILE:(j + 1) * ATT_TILE].astype(BF16)
    vb = proj(7)
    vb_ref[...] = vb.astype(BF16)
    vc = proj(8)
    vc_ref[...] = vc.astype(BF16)
    cvt_ref[0, 0] = jnp.transpose(vc)[:W_C]

    @pl.when(pl.program_id(1) == pl.num_programs(1) - 1)
    def _():
        bkt_ref[0, 0] = jnp.transpose(kb)[:W_B]
        bvt_ref[0, 0] = jnp.transpose(vb)[:W_B]

    lf = _log_forget(proj(9, LANES), bf_ref)
    logf_ref[...] = lf
    lft_ref[0] = jnp.transpose(lf)[:H_A]


def _inproj_prompt(x, lw, layer, depth, nb, s, prev_rows):
    tm = PROJ_TILE
    nt = s // tm
    t = nb * s
    per = tm // ATT_TILE
    row = lambda w: pl.BlockSpec((tm, w), lambda b, i: (b * nt + i, 0))
    full = lambda a: pl.BlockSpec(a.shape, lambda b, i: (0,) * a.ndim)
    feat = lambda w: pl.BlockSpec((1, 1, w, tm), lambda b, i: (layer, b, 0, i))
    band = lambda w: pl.BlockSpec((1, 1, w, BAND), lambda b, i: (layer, b, 0, 0))
    tok = jax.ShapeDtypeStruct((t, GROUP_W), BF16)
    out_shape = [
        jax.ShapeDtypeStruct((GROUP_W, t), BF16), tok,
        jax.ShapeDtypeStruct((nb, s // ATT_TILE, GROUP_W, ATT_TILE), BF16),
        tok, tok, tok, tok, tok, tok, jax.ShapeDtypeStruct((t, LANES), F32),
        jax.ShapeDtypeStruct((depth, nb, W_A, s), F32), jax.ShapeDtypeStruct((depth, nb, W_A, s), F32),
        jax.ShapeDtypeStruct((depth, nb, W_B, BAND), F32), jax.ShapeDtypeStruct((depth, nb, W_B, BAND), F32),
        jax.ShapeDtypeStruct((depth, nb, W_C, s), F32), jax.ShapeDtypeStruct((depth, nb, W_C, s), F32),
        jax.ShapeDtypeStruct((depth, H_A, t), F32)]
    out_specs = [
        pl.BlockSpec((GROUP_W, tm), lambda b, i: (0, b * nt + i)), row(GROUP_W),
        pl.BlockSpec((1, per, GROUP_W, ATT_TILE), lambda b, i: (b, i, 0, 0)),
        row(GROUP_W), row(GROUP_W), row(GROUP_W), row(GROUP_W), row(GROUP_W), row(GROUP_W), row(LANES),
        feat(W_A), feat(W_A), band(W_B), band(W_B), feat(W_C), feat(W_C),
        pl.BlockSpec((1, H_A, tm), lambda b, i: (layer, 0, b * nt + i))]
    ins = [x, lw['norm_mix'], lw['w_in'], lw['gq'], lw['gk'], lw['b_f']]
    in_specs = [row(D_MODEL)] + [full(a) for a in ins[1:]]
    aliases = {}
    n_alias = 0
    if prev_rows is not None:
        n_alias = N_PROMPT_ROWS
        first_row_out = len(out_shape) - N_PROMPT_ROWS
        for j, a in enumerate(prev_rows):
            aliases[len(ins)] = first_row_out + j
            ins.append(a)
            in_specs.append(pl.BlockSpec(memory_space=pl.ANY))
    return pl.pallas_call(
        functools.partial(_inproj_prompt_kernel, n_alias=n_alias), grid=(nb, nt),
        in_specs=in_specs, out_specs=out_specs, out_shape=out_shape,
        input_output_aliases=aliases,
        compiler_params=_cparams("parallel", "arbitrary"), name="inproj_prompt")(*ins)


def _forget_aug_kernel(logf_ref, kaug_ref, qaug_ref, *, nblk):
    tb = ATT_TILE
    r = lax.broadcasted_iota(jnp.int32, (tb, tb), 0)
    c = lax.broadcasted_iota(jnp.int32, (tb, tb), 1)
    tri = jnp.where(c <= r, 1.0, 0.0).astype(BF16)
    src = lax.broadcasted_iota(jnp.int32, (LANES, LANES), 0)
    dst = lax.broadcasted_iota(jnp.int32, (LANES, LANES), 1)

    def place(slot):
        return jnp.where((dst == AUG_SLOTS * src + slot) & (src < H_A), 1.0, 0.0).astype(BF16)

    lane = _lane_iota()
    slot = jnp.bitwise_and(lane, AUG_SLOTS - 1)
    used = lane < AUG_SLOTS * H_A
    key_const = jnp.where(used & (slot >= 3) & (slot < 6), 1.0, 0.0)
    qry_const = jnp.where(used & (slot < 3), -1.0, 0.0)
    carry = jnp.zeros((1, LANES), F32)
    for blk in range(nblk):
        rows = slice(blk * tb, (blk + 1) * tb)
        hi, mid, lo = _split3(logf_ref[rows, :])
        cs = (_dot(tri, hi) + _dot(tri, mid) + _dot(tri, lo)) + carry
        carry = cs[tb - 1:tb, :]
        hi, mid, lo = _split3(cs * LOG2E)
        kaug = _dot(hi, place(0)) + _dot(mid, place(1)) + _dot(lo, place(2)) + key_const
        kaug_ref[rows, :] = kaug.astype(BF16)
        qaug = _dot(hi, place(3)) + _dot(mid, place(4)) + _dot(lo, place(5)) + qry_const
        qaug_ref[:, rows] = jnp.transpose(qaug).astype(BF16)


def _forget_aug(logf, nb, s):
    return pl.pallas_call(
        functools.partial(_forget_aug_kernel, nblk=s // ATT_TILE), grid=(nb,),
        in_specs=[pl.BlockSpec((s, LANES), lambda b: (b, 0))],
        out_specs=[pl.BlockSpec((s, LANES), lambda b: (b, 0)), pl.BlockSpec((LANES, s), lambda b: (0, b))],
        out_shape=[jax.ShapeDtypeStruct((nb * s, LANES), BF16), jax.ShapeDtypeStruct((LANES, nb * s), BF16)],
        compiler_params=_cparams("parallel"), name="forget_aug")(logf)


def _suffix_sum_kernel(x_ref, o_ref):
    p = x_ref.shape[1]
    j = lax.broadcasted_iota(jnp.int32, (p, p), 0)
    s = lax.broadcasted_iota(jnp.int32, (p, p), 1)
    tri = jnp.where(j > s, 1.0, 0.0).astype(BF16)
    hi, mid, lo = _split3(x_ref[...])
    o_ref[...] = _dot(hi, tri) + _dot(mid, tri) + _dot(lo, tri)


def _suffix_sum(x):
    return pl.pallas_call(
        _suffix_sum_kernel, out_shape=jax.ShapeDtypeStruct(x.shape, F32),
        compiler_params=pltpu.CompilerParams(vmem_limit_bytes=VMEM_LIMIT), name="suffix_sum")(x)


def _attn_a_kernel(qt_ref, k_ref, vt_ref, kaug_ref, qaug_ref, o_ref, qp_ref, m_ref, l_ref, acc_ref):
    tq = ATT_TILE
    qi = pl.program_id(1)
    row = lax.broadcasted_iota(jnp.int32, (PAIR_W, 1), 0)
    aug = qaug_ref[...]
    for h in range(H_A):
        q2 = qt_ref[_pair_slice(h), :]
        half = (row < HEAD_DIM) if h % 2 == 0 else (row >= HEAD_DIM)
        qp_ref[h, :PAIR_W, :] = jnp.where(half, q2, jnp.zeros_like(q2))
        mine = (row >= AUG_SLOTS * h) & (row < AUG_SLOTS * (h + 1))
        qp_ref[h, PAIR_W:, :] = jnp.where(mine, aug, jnp.zeros_like(aug))
    m_ref[...] = jnp.full(m_ref.shape, NEG_INF, F32)
    l_ref[...] = jnp.zeros(l_ref.shape, F32)
    acc_ref[...] = jnp.zeros(acc_ref.shape, F32)

    def step(kt, masked):
        ks = pl.multiple_of(kt * tq, tq)
        ka = kaug_ref[pl.ds(ks, tq), :]
        tile_scores = [_dot(jnp.concatenate([k_ref[pl.ds(ks, tq), _pair_slice(h)], ka], axis=1), qp_ref[h])
                       for h in range(H_A)]
        probs, alphas = [], []
        for h in range(H_A):
            s = tile_scores[h]
            if masked:
                key = lax.broadcasted_iota(jnp.int32, (tq, tq), 0)
                qry = lax.broadcasted_iota(jnp.int32, (tq, tq), 1)
                s = jnp.where(key <= qry, s, NEG_INF)
            m_prev = m_ref[h]
            m_new = jnp.maximum(m_prev, jnp.max(s, axis=0, keepdims=True))
            alpha = jnp.exp2(m_prev - m_new)
            p = jnp.exp2(s - m_new)
            l_ref[h] = alpha * l_ref[h] + jnp.sum(p, axis=0, keepdims=True)
            m_ref[h] = m_new
            alphas.append(alpha)
            probs.append(p.astype(BF16))
        for h in range(H_A):
            acc_ref[h] = alphas[h] * acc_ref[h] + _dot(vt_ref[0, kt, _pair_slice(h), :], probs[h])

    def body(kt, carry):
        step(kt, False)
        return carry

    lax.fori_loop(0, qi, body, 0)
    step(qi, True)
    for pair in range(N_PAIRS):
        h = 2 * pair
        out_t = jnp.where(row < HEAD_DIM, acc_ref[h] * (1.0 / l_ref[h]), acc_ref[h + 1] * (1.0 / l_ref[h + 1]))
        o_ref[:, pair * PAIR_W:(pair + 1) * PAIR_W] = jnp.transpose(out_t)


def _attn_a(qat, ka, vat, kaug, qaug, nb, s):
    nq = s // ATT_TILE
    tq = ATT_TILE
    return pl.pallas_call(
        _attn_a_kernel, grid=(nb, nq),
        in_specs=[pl.BlockSpec((GROUP_W, tq), lambda b, i: (0, b * nq + i)),
                  pl.BlockSpec((s, GROUP_W), lambda b, i: (b, 0)),
                  pl.BlockSpec((1, nq, GROUP_W, tq), lambda b, i: (b, 0, 0, 0)),
                  pl.BlockSpec((s, LANES), lambda b, i: (b, 0)),
                  pl.BlockSpec((LANES, tq), lambda b, i: (0, b * nq + i))],
        out_specs=pl.BlockSpec((tq, GROUP_W), lambda b, i: (b * nq + i, 0)),
        out_shape=jax.ShapeDtypeStruct((nb * s, GROUP_W), F32),
        scratch_shapes=[pltpu.VMEM((H_A, 2 * PAIR_W, tq), BF16), pltpu.VMEM((H_A, 1, tq), F32),
                        pltpu.VMEM((H_A, 1, tq), F32), pltpu.VMEM((H_A, PAIR_W, tq), F32)],
        compiler_params=_cparams("parallel", "arbitrary"), name="attn_a")(qat, ka, vat, kaug, qaug)


def _masked_q(q_ref, qm_ref, n_heads):
    for h in range(n_heads):
        q2 = q_ref[:, _pair_slice(h)]
        qm_ref[h] = jnp.where(_half_mask(h % 2), q2, jnp.zeros_like(q2))


def _attn_b_kernel(q_ref, k_ref, v_ref, tab_ref, o_ref):
    tq = ATT_TILE
    win = tq + BAND
    i = pl.program_id(1)
    var = jnp.minimum(i, BAND // tq)
    ws = pl.multiple_of(jnp.maximum(i * tq - BAND, 0), tq)
    heads = range(H_B)
    scores = []
    for h in heads:
        q2 = q_ref[:, _pair_slice(h)]
        qm = jnp.where(_half_mask(h % 2), q2, jnp.zeros_like(q2))
        scores.append(_dot_nt(qm, k_ref[pl.ds(ws, win), _pair_slice(h)]) + tab_ref[var, h])
    probs, norms = [], []
    for h in heads:
        p = jnp.exp2(scores[h] - jnp.max(scores[h], axis=-1, keepdims=True))
        norms.append(1.0 / jnp.sum(p, axis=-1, keepdims=True))
        probs.append(p.astype(BF16))
    res = [_dot(probs[h], v_ref[pl.ds(ws, win), _pair_slice(h)]) * norms[h] for h in heads]
    res.append(jnp.zeros((tq, PAIR_W), F32))
    first = _half_mask(0)
    for pair in range(N_PAIRS):
        o_ref[:, pair * PAIR_W:(pair + 1) * PAIR_W] = jnp.where(first, res[2 * pair], res[2 * pair + 1])


def _suffix_tri(n):
    j = lax.broadcasted_iota(jnp.int32, (2 * n, n), 0)
    s = lax.broadcasted_iota(jnp.int32, (2 * n, n), 1)
    return jnp.where(jnp.where(j >= n, j - n, j) >= s, 1.0, 0.0).astype(BF16)


def _stick_scores(z, tri, seen):
    sp = _softplus2(z)
    if seen is not None:
        sp = jnp.where(seen, sp, 0.0)
    hi, lo = _split2(sp)
    n = sp.shape[1]
    if n % LANES == 0:
        return _dot(jnp.concatenate([hi, lo], axis=1), tri)
    return _dot(hi, tri[:n]) + _dot(lo, tri[:n])


def _stick_weights(z, s_in, carry, seen):
    a = jnp.exp2(z - (s_in + carry))
    if seen is not None:
        a = jnp.where(seen, a, 0.0)
    return a.astype(BF16)


def _attn_c_kernel(q_ref, k_ref, v_ref, o_ref, qm_ref, carry_ref, acc_ref):
    tq = ATT_TILE
    qi = pl.program_id(1)
    _masked_q(q_ref, qm_ref, H_C)
    carry_ref[...] = jnp.zeros(carry_ref.shape, F32)
    acc_ref[...] = jnp.zeros(acc_ref.shape, F32)
    heads = range(H_C)

    def step(kt, diag):
        ks = pl.multiple_of(kt * tq, tq)
        zs = [_dot_nt(qm_ref[h], k_ref[pl.ds(ks, tq), _pair_slice(h)]) for h in heads]
        tri = _suffix_tri(tq)
        seen = None
        if diag:
            r = lax.broadcasted_iota(jnp.int32, (tq, tq), 0)
            c = lax.broadcasted_iota(jnp.int32, (tq, tq), 1)
            seen = c < r
        sums = [_stick_scores(zs[h], tri, seen) for h in heads]
        weights = []
        for h in heads:
            carry = carry_ref[h][:, 0:1]
            weights.append(_stick_weights(zs[h], sums[h], carry, seen))
            carry_ref[h] = jnp.broadcast_to(carry + sums[h][:, 0:1], (tq, LANES))
        for h in heads:
            acc_ref[h] += _dot(weights[h], v_ref[pl.ds(ks, tq), _pair_slice(h)])

    def body(j, c):
        step(qi - 1 - j, False)
        return c

    step(qi, True)
    lax.fori_loop(0, qi, body, 0)
    first = _half_mask(0)
    for pair in range(N_PAIRS):
        h = 2 * pair
        second = acc_ref[h + 1] if h + 1 < H_C else jnp.zeros((tq, PAIR_W), F32)
        o_ref[:, pair * PAIR_W:(pair + 1) * PAIR_W] = jnp.where(first, acc_ref[h], second)


def _prompt_attention(kernel, name, q, k, v, extra, extra_specs, scratch, nb, s):
    nq = s // ATT_TILE
    qspec = pl.BlockSpec((ATT_TILE, GROUP_W), lambda b, i: (b * nq + i, 0))
    kvspec = pl.BlockSpec((s, GROUP_W), lambda b, i: (b, 0))
    return pl.pallas_call(
        kernel, grid=(nb, nq),
        in_specs=[qspec, kvspec, kvspec] + extra_specs,
        out_specs=qspec, out_shape=jax.ShapeDtypeStruct((nb * s, GROUP_W), F32),
        scratch_shapes=scratch,
        compiler_params=_cparams("parallel", "arbitrary"), name=name)(q, k, v, *extra)


def _sample_heads(n_heads, q_ref, width):
    out = []
    for h in range(n_heads):
        lo = (h // 2) * PAIR_W
        span = slice(lo, min(lo + PAIR_W, width))
        q2 = q_ref[:, span]
        if span.stop - span.start == PAIR_W:
            q2 = jnp.where(_half_mask(h % 2), q2, jnp.zeros_like(q2))
        out.append((h, span, q2))
    return out


def _store_heads(o_ref, res, n_heads):
    for pair in range(N_PAIRS):
        lo = pair * PAIR_W
        h = 2 * pair
        if h + 1 < n_heads:
            o_ref[:, lo:lo + PAIR_W] = jnp.where(_half_mask(0), res[h], res[h + 1])
        else:
            o_ref[:, lo:lo + HEAD_DIM] = res[h]
            o_ref[:, lo + HEAD_DIM:lo + PAIR_W] = jnp.zeros_like(res[h])


def _joint_softmax_pv(s_c, s_n, vt_c, v_n):
    m = jnp.maximum(jnp.max(s_c, axis=-1, keepdims=True), jnp.max(s_n, axis=-1, keepdims=True))
    p_c = jnp.exp2(s_c - m)
    p_n = jnp.exp2(s_n - m)
    l = jnp.sum(p_c, axis=-1, keepdims=True) + jnp.sum(p_n, axis=-1, keepdims=True)
    return (_dot_nt(p_c.astype(BF16), vt_c) + _dot(p_n.astype(BF16), v_n)) / l


def _sample_attn_kernel(qa_ref, qb_ref, qc_ref, kan_ref, kbn_ref, kcn_ref, van_ref, vbn_ref, vcn_ref,
                        cak_ref, cav_ref, cbk_ref, cbv_ref, cck_ref, ccv_ref,
                        rsum_ref, logf_ref, tabc_ref, tabn_ref,
                        oa_ref, ob_ref, oc_ref):
    t = qa_ref.shape[0]
    past = cak_ref.shape[3]
    r = lax.broadcasted_iota(jnp.int32, (t, t), 0)
    c = lax.broadcasted_iota(jnp.int32, (t, t), 1)

    ltri = jnp.where(c <= r, 1.0, 0.0).astype(BF16)
    hi, mid, lo = _split3(logf_ref[...])
    pcol = (_dot(ltri, hi) + _dot(ltri, mid) + _dot(ltri, lo)) * LOG2E
    prow = jnp.transpose(jnp.concatenate([pcol, jnp.zeros((LANES - t, LANES), F32)], axis=0))
    heads = _sample_heads(H_A, qa_ref, W_A)
    kts = {sp.start: cak_ref[0, 0, sp, :].astype(BF16) for _, sp, _ in heads}
    vts = {sp.start: cav_ref[0, 0, sp, :].astype(BF16) for _, sp, _ in heads}
    s_c = [_dot(qm, kts[sp.start]) + (pcol[:, h:h + 1] + rsum_ref[0, 0, h:h + 1, :] * LOG2E)
           for h, sp, qm in heads]
    s_n = [jnp.where(c <= r, _dot_nt(qm, kan_ref[:, sp]) + (pcol[:, h:h + 1] - prow[h:h + 1, 0:t]), NEG_INF)
           for h, sp, qm in heads]
    _store_heads(oa_ref, [_joint_softmax_pv(s_c[h], s_n[h], vts[sp.start], van_ref[:, sp])
                          for h, sp, _ in heads], H_A)

    heads = _sample_heads(H_B, qb_ref, W_B)
    kts = {sp.start: cbk_ref[0, 0, sp, :].astype(BF16) for _, sp, _ in heads}
    vts = {sp.start: cbv_ref[0, 0, sp, :].astype(BF16) for _, sp, _ in heads}
    s_c = [_dot(qm, kts[sp.start]) + tabc_ref[h] for h, sp, qm in heads]
    s_n = [_dot_nt(qm, kbn_ref[:, sp]) + tabn_ref[h] for h, sp, qm in heads]
    _store_heads(ob_ref, [_joint_softmax_pv(s_c[h], s_n[h], vts[sp.start], vbn_ref[:, sp])
                          for h, sp, _ in heads], H_B)

    heads = _sample_heads(H_C, qc_ref, W_C)
    nblk = past // ATT_TILE
    tri_n = _suffix_tri(t)
    tri_c = _suffix_tri(ATT_TILE)
    seen = c < r
    kts = {sp.start: cck_ref[0, 0, sp, :].astype(BF16) for _, sp, _ in heads}
    vts = {sp.start: ccv_ref[0, 0, sp, :].astype(BF16) for _, sp, _ in heads}
    z_n = [_dot_nt(qm, kcn_ref[:, sp]) for _, sp, qm in heads]
    z_c = [_dot(qm, kts[sp.start]) for _, sp, qm in heads]
    sum_n = [_stick_scores(z, tri_n, seen) for z in z_n]
    sum_c = [[_stick_scores(z[:, b * ATT_TILE:(b + 1) * ATT_TILE], tri_c, None) for b in range(nblk)]
             for z in z_c]
    res = []
    for h, sp, _ in heads:
        acc = _dot(_stick_weights(z_n[h], sum_n[h], 0.0, seen), vcn_ref[:, sp])
        carry = sum_n[h][:, 0:1]
        for b in reversed(range(nblk)):
            cols = slice(b * ATT_TILE, (b + 1) * ATT_TILE)
            a = _stick_weights(z_c[h][:, cols], sum_c[h][b], carry, None)
            acc = acc + _dot_nt(a, vts[sp.start][:, cols])
            carry = carry + sum_c[h][b][:, 0:1]
        res.append(acc)
    _store_heads(oc_ref, res, H_C)


def _sample_attention(proj, caches, layer, rsum, tabc, tabn, nb, t):
    qa, qb, qc, ka, kb, kc, va, vb, vc = proj[:9]
    logf = proj[15]
    new = pl.BlockSpec((t, GROUP_W), lambda b: (b, 0))
    cache = lambda a: pl.BlockSpec((1, 1) + a.shape[2:], lambda b: (layer, b, 0, 0))
    full = lambda a: pl.BlockSpec(a.shape, lambda b: (0,) * a.ndim)
    return pl.pallas_call(
        _sample_attn_kernel, grid=(nb,),
        in_specs=[new] * 9 + [cache(a) for a in caches] + [
            cache(rsum), pl.BlockSpec((t, LANES), lambda b: (b, 0)), full(tabc), full(tabn)],
        out_specs=[new] * 3,
        out_shape=[jax.ShapeDtypeStruct((nb * t, GROUP_W), F32)] * 3,
        compiler_params=_cparams("parallel"), name="sample_attn")(
            qa, qb, qc, ka, kb, kc, va, vb, vc, *caches, rsum, logf, tabc, tabn)


def _masked_max(x, mask):
    return jnp.max(jnp.where(mask, x, -jnp.inf), axis=-1, keepdims=True)


def _first_lane(mask, lane):
    return jnp.min(jnp.where(mask, lane, float(LANES)), axis=-1, keepdims=True)


def _route(logits):
    lane = _lane_iota().astype(F32)
    is_g = (lane >= N_EXPERTS) & (lane < N_EXPERTS + N_GROUPS)
    gmax = _masked_max(logits, is_g)
    p_g = 1.0 / jnp.sum(jnp.where(is_g, jnp.exp(logits - gmax), 0.0), axis=-1, keepdims=True)
    g_sel = _first_lane(is_g & (logits == gmax), lane) - N_EXPERTS
    lo = g_sel * EXPERTS_PER_GROUP
    in_g = (lane >= lo) & (lane < lo + EXPERTS_PER_GROUP)
    l1 = _masked_max(logits, in_g)
    i1 = _first_lane(in_g & (logits == l1), lane)
    rest = in_g & (lane != i1)
    l2 = _masked_max(logits, rest)
    i2 = _first_lane(rest & (logits == l2), lane)
    e2 = jnp.exp(l2 - l1)
    w1 = p_g / (1.0 + e2)
    w2 = p_g * e2 / (1.0 + e2)
    gates = jnp.where(lane == i1, w1, jnp.where(lane == i2, w2, 0.0))
    return jnp.where(lane == g_sel + N_EXPERTS, 1.0, gates)


def _post_kernel(oa_ref, ob_ref, oc_ref, h_ref, gmix_ref, wout_ref, nffn_ref, wrh_ref, wrl_ref, br_ref,
                 h1_ref, xn_ref, gates_ref, xt_ref, gt_ref):
    def gnorm(o, width):
        ms = jnp.sum(o * o, axis=-1, keepdims=True) * (1.0 / width)
        return o * lax.rsqrt(ms + EPS)

    h1 = h_ref[...]
    for g, (o_ref, width) in enumerate(((oa_ref, W_A), (ob_ref, W_B), (oc_ref, W_C))):
        merged = (gnorm(o_ref[...], width) * gmix_ref[g]).astype(BF16)
        h1 = h1 + _dot(merged, wout_ref[g])
    h1_ref[...] = h1
    xf = _rms(h1, nffn_ref[...])
    hi, lo = _split2(xf)
    xn_ref[...] = hi
    xt_ref[...] = jnp.transpose(xf).astype(BF16)
    logits = _dot(hi, wrh_ref[...]) + _dot(lo, wrh_ref[...]) + _dot(hi, wrl_ref[...]) + br_ref[...]
    gates = _route(logits)
    gates_ref[...] = gates
    gt_ref[...] = jnp.transpose(gates)


def _post(oa, ob, oc, h, lw, tm):
    t = h.shape[0]
    row = lambda w: pl.BlockSpec((tm, w), lambda i: (i, 0))
    full = lambda a: pl.BlockSpec(a.shape, lambda i: (0,) * a.ndim)
    ws = (lw['g_mix'], lw['w_out'], lw['norm_ffn'], lw['w_r_hi'], lw['w_r_lo'], lw['b_r'])
    return pl.pallas_call(
        _post_kernel, grid=(t // tm,),
        in_specs=[row(GROUP_W)] * 3 + [row(D_MODEL)] + [full(a) for a in ws],
        out_specs=[row(D_MODEL), row(D_MODEL), row(LANES),
                   pl.BlockSpec((D_MODEL, tm), lambda i: (0, i)), pl.BlockSpec((LANES, tm), lambda i: (0, i))],
        out_shape=[jax.ShapeDtypeStruct((t, D_MODEL), F32), jax.ShapeDtypeStruct((t, D_MODEL), BF16),
                   jax.ShapeDtypeStruct((t, LANES), F32),
                   jax.ShapeDtypeStruct((D_MODEL, t), BF16), jax.ShapeDtypeStruct((LANES, t), F32)],
        compiler_params=_cparams("parallel"), name="post")(oa, ob, oc, h, *ws)


def _moe_kernel(xn_ref, gates_ref, wg_ref, wu_ref, wd_ref, h1_ref, p_ref, wple_ref, nple_ref, wpg_ref,
                out_ref, acc_ref):
    e = pl.program_id(1)

    @pl.when(e == 0)
    def _():
        acc_ref[...] = jnp.zeros_like(acc_ref)

    x = xn_ref[...]
    g = _dot(x, wg_ref[0])
    u = _dot(x, wu_ref[0])
    gate = jnp.sum(jnp.where(_lane_iota() == e, gates_ref[...], 0.0), axis=-1, keepdims=True)
    act = (g * _sigmoid(g)) * u * gate
    acc_ref[...] += _dot(act.astype(BF16), wd_ref[0])

    @pl.when(e == N_EXPERTS - 1)
    def _():
        h2 = h1_ref[...] + acc_ref[...]
        gate_ple = _sigmoid(_dot(_rms(h2, nple_ref[...]).astype(BF16), wpg_ref[...]))
        out_ref[...] = h2 + _dot(p_ref[0].astype(BF16), wple_ref[...]) * gate_ple


def _ple_gate(h2, p, wple_ref, nple_ref, wpg_ref):
    gate_ple = _sigmoid(_dot(_rms(h2, nple_ref[...]).astype(BF16), wpg_ref[...]))
    return h2 + _dot(p.astype(BF16), wple_ref[...]) * gate_ple


def _moe_grouped_kernel(xt_ref, gt_ref, ltri_ref, wg_ref, wu_ref, wd_ref, h1_ref, p_ref, wple_ref, nple_ref,
                        wpg_ref, out_ref, acct_ref, slot_col_ref, slot_row_ref, gsplit_ref, *, cap):
    g = pl.program_id(1)
    nb = xt_ref.shape[1]
    row = lax.broadcasted_iota(jnp.int32, (LANES, 1), 0)

    @pl.when(g == 0)
    def _():
        acct_ref[...] = jnp.zeros_like(acct_ref)
        gt = gt_ref[...]
        member_t = jnp.where((row >= N_EXPERTS) & (row < N_EXPERTS + N_GROUPS), gt, 0.0)
        member = jnp.transpose(member_t)
        slot_row_ref[...] = jnp.where(member_t > 0.0, _dot_nt(member_t.astype(BF16), ltri_ref[...]), -1.0)
        slot_col_ref[...] = jnp.where(member > 0.0, _dot(ltri_ref[...], member.astype(BF16)), -1.0)
        hi, mid, lo = _split3(gt)
        gsplit_ref[0] = hi
        gsplit_ref[1] = mid
        gsplit_ref[2] = lo

    slot_row = slot_row_ref[pl.ds(N_EXPERTS + g, 1), :]
    slot_col = jnp.sum(jnp.where(_lane_iota() == N_EXPERTS + g, slot_col_ref[...], 0.0),
                       axis=-1, keepdims=True)
    count = jnp.sum(jnp.where(slot_row >= 0.0, 1.0, 0.0))
    d_expert = D_MODEL // EXPERTS_PER_GROUP

    def experts(x_t, gate_rows):
        hg = _dot(wg_ref[g], x_t)
        act = (hg * _sigmoid(hg)) * _dot(wu_ref[g], x_t)
        parts = []
        for e in range(EXPERTS_PER_GROUP):
            gate_e = jnp.sum(jnp.where(row == g * EXPERTS_PER_GROUP + e, gate_rows, 0.0), axis=0, keepdims=True)
            parts.append(act[e * d_expert:(e + 1) * d_expert] * gate_e)
        return _dot(wd_ref[g], jnp.concatenate(parts, axis=0).astype(BF16))

    unpack = jnp.where(_lane_iota(cap).astype(F32) == slot_col, 1.0, 0.0).astype(BF16)
    pack = jnp.where(lax.broadcasted_iota(jnp.int32, (cap, 1), 0).astype(F32) == slot_row, 1.0, 0.0).astype(BF16)
    xs_t = _dot(xt_ref[...], unpack).astype(BF16)
    gs_t = _dot(gsplit_ref[0], unpack) + _dot(gsplit_ref[1], unpack) + _dot(gsplit_ref[2], unpack)
    acct_ref[...] += _dot(experts(xs_t, gs_t).astype(BF16), pack)

    @pl.when(count > cap)
    def _():
        acct_ref[...] += jnp.where(slot_row >= cap, experts(xt_ref[...], gt_ref[...]), 0.0)

    @pl.when(g == N_GROUPS - 1)
    def _():
        h2 = h1_ref[...] + jnp.transpose(acct_ref[...])
        out_ref[...] = _ple_gate(h2, p_ref[0], wple_ref, nple_ref, wpg_ref)


def _moe_grouped_tail_kernel(*refs, cap):
    _moe_grouped_kernel(*refs[:11], *refs[12:], cap=cap)


def _moe_grouped_call(xt, gt, h1, p_all, layer, lw, nb, first, n_blocks, prev_out):
    t = h1.shape[0]
    once = pl.Buffered(1)
    full = lambda a: pl.BlockSpec(a.shape, lambda i, g: (0,) * a.ndim, pipeline_mode=once)
    r = lax.broadcasted_iota(jnp.int32, (nb, nb), 0)
    c = lax.broadcasted_iota(jnp.int32, (nb, nb), 1)
    ltri = jnp.where(c < r, 1.0, 0.0).astype(BF16)
    ins = [xt, gt, ltri, lw['w_tg'], lw['w_tu'], lw['w_td'], h1, p_all, lw['w_ple'], lw['norm_ple'], lw['w_pg']]
    in_specs = [pl.BlockSpec((D_MODEL, nb), lambda i, g: (0, first + i)),
                pl.BlockSpec((LANES, nb), lambda i, g: (0, first + i)),
                full(ltri), full(lw['w_tg']), full(lw['w_tu']), full(lw['w_td']),
                pl.BlockSpec((nb, D_MODEL), lambda i, g: (first + i, 0), pipeline_mode=once),
                pl.BlockSpec((1, nb, p_all.shape[2]), lambda i, g: (layer, first + i, 0), pipeline_mode=once),
                full(lw['w_ple']), full(lw['norm_ple']), full(lw['w_pg'])]
    body, aliases = _moe_grouped_kernel, {}
    if prev_out is not None:
        body, aliases = _moe_grouped_tail_kernel, {len(ins): 0}
        ins.append(prev_out)
        in_specs.append(pl.BlockSpec(memory_space=pl.ANY))
    return pl.pallas_call(
        functools.partial(body, cap=min(MOE_CAP, nb)), grid=(n_blocks, N_GROUPS),
        in_specs=in_specs, out_specs=pl.BlockSpec((nb, D_MODEL), lambda i, g: (first + i, 0)),
        out_shape=jax.ShapeDtypeStruct((t, D_MODEL), F32), input_output_aliases=aliases,
        scratch_shapes=[pltpu.VMEM((D_MODEL, nb), F32), pltpu.VMEM((nb, LANES), F32),
                        pltpu.VMEM((LANES, nb), F32), pltpu.VMEM((3, LANES, nb), BF16)],
        compiler_params=pltpu.CompilerParams(dimension_semantics=("parallel", "arbitrary"),
                                             vmem_limit_bytes=MOE_VMEM_LIMIT),
        name="moe_grouped")(*ins)


def _moe_grouped(xt, gt, h1, p_all, layer, lw):
    t = h1.shape[0]
    n_main, tail = divmod(t, MOE_BLOCK)
    out = _moe_grouped_call(xt, gt, h1, p_all, layer, lw, MOE_BLOCK, 0, n_main, None)
    if tail:
        assert tail % LANES == 0 and (n_main * MOE_BLOCK) % tail == 0
        out = _moe_grouped_call(xt, gt, h1, p_all, layer, lw, tail, n_main * MOE_BLOCK // tail, 1, out)
    return out


def _moe(xn, gates, h1, p_all, layer, lw, tm):
    t = xn.shape[0]
    row = lambda w: pl.BlockSpec((tm, w), lambda i, e: (i, 0))
    full = lambda a: pl.BlockSpec(a.shape, lambda i, e: (0,) * a.ndim)
    exp = lambda a: pl.BlockSpec((1,) + a.shape[1:], lambda i, e: (e, 0, 0))
    return pl.pallas_call(
        _moe_kernel, grid=(t // tm, N_EXPERTS),
        in_specs=[row(D_MODEL), row(LANES), exp(lw['w_eg']), exp(lw['w_eu']), exp(lw['w_ed']),
                  row(D_MODEL), pl.BlockSpec((1, tm, p_all.shape[2]), lambda i, e: (layer, i, 0)),
                  full(lw['w_ple']), full(lw['norm_ple']), full(lw['w_pg'])],
        out_specs=row(D_MODEL), out_shape=jax.ShapeDtypeStruct((t, D_MODEL), F32),
        scratch_shapes=[pltpu.VMEM((tm, D_MODEL), F32)],
        compiler_params=_cparams("parallel", "arbitrary"), name="moe")(
            xn, gates, lw['w_eg'], lw['w_eu'], lw['w_ed'], h1, p_all, lw['w_ple'], lw['norm_ple'], lw['w_pg'])


def _regroup_out_rows(wt):
    z = jnp.zeros((HEAD_DIM, wt.shape[1]), wt.dtype)
    return jnp.concatenate([wt[:W_A], wt[W_A:W_A + W_B], z, wt[W_A + W_B:], z], axis=0)


def _regroup_rows(w):
    z = jnp.zeros((HEAD_DIM, w.shape[1]), w.dtype)
    return jnp.stack([w[:W_A], jnp.concatenate([w[W_A:W_A + W_B], z], axis=0),
                      jnp.concatenate([w[W_A + W_B:], z], axis=0)])


def _pad_lanes(a, n=LANES):
    return jnp.pad(a, [(0, 0)] * (a.ndim - 1) + [(0, n - a.shape[-1])])


def _toeplitz_bias(rel_bias, off, tq, win):
    length = tq + win
    pad = length + abs(off)
    ext = jnp.flip(jnp.pad(rel_bias, ((0, 0), (pad, pad)), mode='edge'), axis=1)
    s1 = ext.shape[1] - 1 - (off + REL_CLIP + pad)
    v = jnp.concatenate([ext[:, s1:s1 + win], ext[:, s1 - tq:s1]], axis=1)
    flat = jnp.tile(v, (1, tq))[:, :tq * (length - 1)]
    return flat.reshape(-1, tq, length - 1)[:, :, :win]


def _block_toeplitz_bias(rel_bias, off, tq, win):
    nq, nk = tq // LANES, win // LANES
    blocks = {d: _toeplitz_bias(rel_bias, off - d * LANES, LANES, LANES) for d in range(-(nq - 1), nk)}
    return jnp.concatenate(
        [jnp.concatenate([blocks[b - a] for b in range(nk)], axis=2) for a in range(nq)], axis=1)


def _rel_tables(rel_bias, t_new, rows_b):
    tq, win = ATT_TILE, ATT_TILE + BAND
    q = jnp.arange(tq)[:, None]
    tabs = []
    for var in range(BAND // tq + 1):
        rel_k = jnp.arange(win)[None, :] - var * tq
        kch = jnp.floor_divide(rel_k, CHUNK)
        qch = q // CHUNK
        valid = (kch <= qch) & (kch >= qch - BAND_CHUNKS)
        tabs.append(jnp.where(valid[None], _block_toeplitz_bias(rel_bias, var * tq, tq, win), NEG_INF))
    tab_prompt = jnp.stack(tabs).astype(F32)
    tab_c = _toeplitz_bias(rel_bias, rows_b, t_new, rows_b).astype(F32)
    tab_n = _toeplitz_bias(rel_bias, 0, t_new, t_new).astype(F32)
    return tab_prompt, tab_c, tab_n


def _layer_weights(i, norm_mix, w_in, b_f, g_qa, g_ka, g_qb, g_kb, g_mix_out, w_out, norm_ffn,
                   w_rg, b_rg, w_re, b_re, w_eg, w_eu, w_ed, w_ple, norm_ple, w_pg):
    d_mix = W_A + W_B + W_C
    wt = jnp.transpose(w_in, (2, 0, 1))[:, i, :]
    w_f = wt[3 * d_mix:]
    w_big = jnp.concatenate(
        [_regroup_out_rows(wt[j * d_mix:(j + 1) * d_mix]) for j in range(3)]
        + [jnp.pad(w_f, ((0, LANES - w_f.shape[0]), (0, 0)))], axis=0).astype(BF16)
    tile6 = lambda g: jnp.tile(g, GROUP_W // HEAD_DIM)
    w_r = _pad_lanes(jnp.concatenate([w_re[i], w_rg[i]], axis=1))
    w_r_hi = w_r.astype(BF16)
    gm = g_mix_out[i]
    zpad = jnp.zeros((HEAD_DIM,), F32)
    g_mix = jnp.stack([gm[:W_A], jnp.concatenate([gm[W_A:W_A + W_B], zpad]),
                       jnp.concatenate([gm[W_A + W_B:], zpad])])[:, None, :]
    return dict(
        norm_mix=norm_mix[i][None], w_in=w_big,
        gq=jnp.concatenate([tile6(g_qa[i]), tile6(g_qb[i])])[None],
        gk=jnp.concatenate([tile6(g_ka[i]), tile6(g_kb[i])])[None],
        b_f=_pad_lanes(b_f[i])[None],
        g_mix=g_mix, w_out=_regroup_rows(w_out[i]).astype(BF16), norm_ffn=norm_ffn[i][None],
        w_r_hi=w_r_hi, w_r_lo=(w_r - w_r_hi.astype(F32)).astype(BF16),
        b_r=_pad_lanes(jnp.concatenate([b_re[i], b_rg[i]]))[None],
        w_eg=w_eg[i].astype(BF16), w_eu=w_eu[i].astype(BF16), w_ed=w_ed[i].astype(BF16),
        w_tg=_group_hidden_rows(w_eg[i]), w_tu=_group_hidden_rows(w_eu[i]), w_td=_group_hidden_cols(w_ed[i]),
        w_ple=w_ple[i].astype(BF16), norm_ple=norm_ple[i][None], w_pg=w_pg[i].astype(BF16))


def _group_hidden_rows(w):
    e, d, f = w.shape
    return jnp.transpose(w, (0, 2, 1)).reshape(N_GROUPS, EXPERTS_PER_GROUP * f, d).astype(BF16)


def _group_hidden_cols(w):
    e, f, d = w.shape
    return jnp.transpose(w.reshape(N_GROUPS, EXPERTS_PER_GROUP * f, d), (0, 2, 1)).astype(BF16)


def _token_tile(t, pref):
    return pref if t % pref == 0 else t


def _feature_major(cache):
    d, n, p, h, e = cache.shape
    return jnp.transpose(cache, (0, 1, 3, 4, 2)).reshape(d, n, h * e, p)


def _position_major(rows, heads):
    d, n, _, p = rows.shape
    return jnp.transpose(rows.reshape(d, n, heads, HEAD_DIM, p), (0, 1, 4, 2, 3))


def kernel(x_prompt, x_sample, p_prompt, p_sample, cache_a_k, cache_a_v, cache_a_logf, cache_b_k, cache_b_v, cache_c_k, cache_c_v, norm_mix, w_in, b_f, g_qa, g_ka, g_qb, g_kb, rel_bias, g_mix_out, w_out, norm_ffn, w_router_group, b_router_group, w_router_expert, b_router_expert, w_exp_gate, w_exp_up, w_exp_down, w_ple, norm_ple, w_ple_gate):
    nb, s, d = x_prompt.shape
    ns, t_new, _ = x_sample.shape
    depth = w_in.shape[0]
    past = cache_a_k.shape[2]
    rows_b = cache_b_k.shape[2]
    assert d == D_MODEL and s % PROJ_TILE == 0 and s >= BAND + ATT_TILE and past % ATT_TILE == 0
    tp, ts = nb * s, ns * t_new

    hp = x_prompt.reshape(tp, d)
    hs = x_sample.reshape(ts, d)
    pp = p_prompt.reshape(depth, tp, -1)
    ps = p_sample.reshape(depth, ts, -1)
    caches = tuple(_feature_major(c) for c in (cache_a_k, cache_a_v, cache_b_k, cache_b_v, cache_c_k, cache_c_v))
    lf_rows = jnp.transpose(cache_a_logf, (0, 3, 1, 2))
    rsum = _suffix_sum(lf_rows.reshape(depth * H_A * ns, past)).reshape(depth, H_A, ns, past)
    rsum = jnp.transpose(rsum, (0, 2, 1, 3))

    rows_p = None
    new_s = [[] for _ in range(7)]
    for i in range(depth):
        lw = _layer_weights(i, norm_mix, w_in, b_f, g_qa, g_ka, g_qb, g_kb, g_mix_out, w_out, norm_ffn,
                            w_router_group, b_router_group, w_router_expert, b_router_expert,
                            w_exp_gate, w_exp_up, w_exp_down, w_ple, norm_ple, w_ple_gate)
        tab_p, tab_c, tab_n = _rel_tables(rel_bias[i] * LOG2E, t_new, rows_b)

        pr = _inproj_prompt(hp, lw, i, depth, nb, s, rows_p)
        qat, ka, vat, qb, kb, vb, qc, kc, vc, logf = pr[:10]
        rows_p = pr[10:]
        kaug, qaug = _forget_aug(logf, nb, s)
        oa = _attn_a(qat, ka, vat, kaug, qaug, nb, s)
        ob = _prompt_attention(
            _attn_b_kernel, "attn_b", qb, kb, vb, (tab_p,),
            [pl.BlockSpec(tab_p.shape, lambda b, j: (0, 0, 0, 0))], [], nb, s)
        oc = _prompt_attention(
            _attn_c_kernel, "attn_c", qc, kc, vc, (), [],
            [pltpu.VMEM((H_C, ATT_TILE, PAIR_W), BF16), pltpu.VMEM((H_C, ATT_TILE, LANES), F32),
             pltpu.VMEM((H_C, ATT_TILE, PAIR_W), F32)], nb, s)
        h1, _, _, xt, gt = _post(oa, ob, oc, hp, lw, _token_tile(tp, 512))
        hp = _moe_grouped(xt, gt, h1, pp, i, lw)

        sr = _inproj_sample(hs, lw)
        oa, ob, oc = _sample_attention(sr, caches, i, rsum, tab_c, tab_n, ns, t_new)
        h1, xn, gates, _, _ = _post(oa, ob, oc, hs, lw, _token_tile(ts, 512))
        hs = _moe(xn, gates, h1, ps, i, lw, _token_tile(ts, 1024))
        akf, avf, bkf, bvf, ckf, cvf, logf = sr[9:]
        rows = (akf.reshape(ns, t_new, H_A, HEAD_DIM), avf.reshape(ns, t_new, H_A, HEAD_DIM),
                logf[:, :H_A].reshape(ns, t_new, H_A),
                bkf.reshape(ns, t_new, H_B, HEAD_DIM), bvf.reshape(ns, t_new, H_B, HEAD_DIM),
                ckf.reshape(ns, t_new, H_C, HEAD_DIM), cvf.reshape(ns, t_new, H_C, HEAD_DIM))
        for j in range(7):
            new_s[j].append(rows[j])

    akt, avt, bkt, bvt, ckt, cvt, lft = rows_p
    outs_p = (_position_major(akt, H_A), _position_major(avt, H_A),
              jnp.transpose(lft.reshape(depth, H_A, nb, s), (0, 2, 3, 1)),
              _position_major(bkt, H_B), _position_major(bvt, H_B),
              _position_major(ckt, H_C), _position_major(cvt, H_C))
    outs_s = [jnp.stack(r, axis=0) for r in new_s]
    return (hp.reshape(nb, s, d), hs.reshape(ns, t_new, d), *outs_p, *outs_s)
```

```python
import functools
import math

import jax
import jax.numpy as jnp
from jax import lax
from jax.experimental import pallas as pl
from jax.experimental.pallas import tpu as pltpu
from jax.experimental.pallas import tpu_sc as plsc

F32 = jnp.float32
BF16 = jnp.bfloat16

D_MODEL = 1024
HEAD_DIM = 64
H_A, H_B, H_C = 6, 5, 5
W_A, W_B, W_C = H_A * HEAD_DIM, H_B * HEAD_DIM, H_C * HEAD_DIM
GROUP_W = 384
PAIR_W = 2 * HEAD_DIM
N_PAIRS = GROUP_W // PAIR_W
CHUNK = 64
BAND_CHUNKS = 8
BAND = BAND_CHUNKS * CHUNK
REL_CLIP = 128
N_GROUPS = 4
EXPERTS_PER_GROUP = 4
N_EXPERTS = N_GROUPS * EXPERTS_PER_GROUP
EPS = 1e-6
ATTN_SCALE = HEAD_DIM ** -0.5
LOG2E = math.log2(math.e)
Q_SCALE = ATTN_SCALE * LOG2E
NEG_INF = -1e30
LANES = 128
SUBLANES = 8
VMEM_LIMIT = 48 * 1024 * 1024
EXPERTS_PER_STEP = 4
MOE_VMEM_LIMIT = 58 * 1024 * 1024

ATT_TILE = 256
A_QTILE = 512
C_QTILE = 256
PROJ_TILE = 1024
PROJ_VMEM_LIMIT = 56 * 1024 * 1024
AUG_SLOTS = 8

_NT = (((1,), (1,)), ((), ()))


def _cparams(*sem):
    return pltpu.CompilerParams(dimension_semantics=sem, vmem_limit_bytes=VMEM_LIMIT)


def _dot(a, b):
    return jnp.dot(a, b, preferred_element_type=F32)


def _dot_nt(a, b):
    return lax.dot_general(a, b, _NT, preferred_element_type=F32)


def _split3(x):
    hi = x.astype(BF16)
    r1 = x - hi.astype(F32)
    mid = r1.astype(BF16)
    lo = (r1 - mid.astype(F32)).astype(BF16)
    return hi, mid, lo


def _split2(x):
    hi = x.astype(BF16)
    lo = (x - hi.astype(F32)).astype(BF16)
    return hi, lo


def _lane_iota(n=LANES):
    return lax.broadcasted_iota(jnp.int32, (1, n), 1)


def _half_mask(half):
    lane = _lane_iota()
    return (lane < HEAD_DIM) if half == 0 else (lane >= HEAD_DIM)


def _pair_slice(h):
    return slice((h // 2) * PAIR_W, (h // 2 + 1) * PAIR_W)


def _softplus2(z2):
    return jnp.maximum(z2, 0.0) + jnp.log2(1.0 + jnp.exp2(-jnp.abs(z2)))


def _sigmoid(z):
    return 1.0 / (1.0 + jnp.exp(-z))


def _rms(x, g):
    return x * lax.rsqrt(jnp.mean(x * x, axis=-1, keepdims=True) + EPS) * g


def _pair_rms(z, gain):
    first = _half_mask(0)
    outs = []
    for j in range(N_PAIRS):
        blk = z[:, j * PAIR_W:(j + 1) * PAIR_W]
        sq = blk * blk
        lo = jnp.sum(jnp.where(first, sq, 0.0), axis=-1, keepdims=True)
        hi = jnp.sum(jnp.where(first, 0.0, sq), axis=-1, keepdims=True)
        ms = jnp.where(first, lo, hi) * (1.0 / HEAD_DIM)
        outs.append(blk * lax.rsqrt(ms + EPS) * gain[:, j * PAIR_W:(j + 1) * PAIR_W])
    return jnp.concatenate(outs, axis=-1)


def _projector(x_ref, gn_ref, w_ref):
    xb = _rms(x_ref[...], gn_ref[...]).astype(BF16)
    return lambda g, width=GROUP_W: _dot_nt(xb, w_ref[g * GROUP_W:g * GROUP_W + width, :])


def _log_forget(zf, bf_ref):
    zf = zf + bf_ref[...]
    return jnp.minimum(zf, 0.0) - jnp.log(1.0 + jnp.exp(-jnp.abs(zf)))


def _inproj_sample_kernel(x_ref, gn_ref, w_ref, gq_ref, gk_ref, bf_ref,
                          qa_ref, qb_ref, qc_ref, ka_ref, kb_ref, kc_ref, va_ref, vb_ref, vc_ref,
                          akf_ref, avf_ref, bkf_ref, bvf_ref, ckf_ref, cvf_ref, logf_ref):
    proj = _projector(x_ref, gn_ref, w_ref)
    qa_ref[...] = (_pair_rms(proj(0), gq_ref[:, :GROUP_W]) * Q_SCALE).astype(BF16)
    qb_ref[...] = (_pair_rms(proj(1), gq_ref[:, GROUP_W:]) * Q_SCALE).astype(BF16)
    qc_ref[...] = (proj(2) * Q_SCALE).astype(BF16)
    for g, norm, bf_ref_, f_ref, width in (
            (3, gk_ref[:, :GROUP_W], ka_ref, akf_ref, W_A), (4, gk_ref[:, GROUP_W:], kb_ref, bkf_ref, W_B),
            (5, None, kc_ref, ckf_ref, W_C), (6, None, va_ref, avf_ref, W_A),
            (7, None, vb_ref, bvf_ref, W_B), (8, None, vc_ref, cvf_ref, W_C)):
        z = proj(g)
        if norm is not None:
            z = _pair_rms(z, norm)
        f_ref[...] = z[:, :width]
        bf_ref_[...] = z.astype(BF16)
    logf_ref[...] = _log_forget(proj(9, LANES), bf_ref)


def _inproj_sample(x, lw):
    t = x.shape[0]
    row = lambda w: pl.BlockSpec((t, w), lambda i: (0, 0))
    full = lambda a: pl.BlockSpec(a.shape, lambda i: (0,) * a.ndim)
    widths = [GROUP_W] * 9 + [W_A, W_A, W_B, W_B, W_C, W_C, LANES]
    out_shape = [jax.ShapeDtypeStruct((t, w), BF16 if j < 9 else F32) for j, w in enumerate(widths)]
    ins = (x, lw['norm_mix'], lw['w_in'], lw['gq'], lw['gk'], lw['b_f'])
    return pl.pallas_call(
        _inproj_sample_kernel, grid=(1,),
        in_specs=[row(D_MODEL)] + [full(a) for a in ins[1:]],
        out_specs=[row(w) for w in widths], out_shape=out_shape,
        compiler_params=_cparams("arbitrary"), name="inproj_sample")(*ins)


N_PROMPT_ROWS = 7


def _inproj_prompt_kernel(*refs, n_alias):
    x_ref, gn_ref, w_ref, gq_ref, gk_ref, bf_ref = refs[:6]
    (qat_ref, ka_ref, vat_ref, qb_ref, kb_ref, vb_ref, qc_ref, kc_ref, vc_ref, logf_ref,
     akt_ref, avt_ref, bkt_ref, bvt_ref, ckt_ref, cvt_ref, lft_ref) = refs[6 + n_alias:]
    proj = _projector(x_ref, gn_ref, w_ref)
    qat_ref[...] = jnp.transpose(_pair_rms(proj(0), gq_ref[:, :GROUP_W]) * Q_SCALE).astype(BF16)
    qb_ref[...] = (_pair_rms(proj(1), gq_ref[:, GROUP_W:]) * Q_SCALE).astype(BF16)
    qc_ref[...] = (proj(2) * Q_SCALE).astype(BF16)
    ka = _pair_rms(proj(3), gk_ref[:, :GROUP_W])
    ka_ref[...] = ka.astype(BF16)
    akt_ref[0, 0] = jnp.transpose(ka)
    kb = _pair_rms(proj(4), gk_ref[:, GROUP_W:])
    kb_ref[...] = kb.astype(BF16)
    kc = proj(5)
    kc_ref[...] = kc.astype(BF16)
    ckt_ref[0, 0] = jnp.transpose(kc)[:W_C]
    vat = jnp.transpose(proj(6))
    avt_ref[0, 0] = vat
    for j in range(vat_ref.shape[1]):
        vat_ref[0, j] = vat[:, j * ATT_TILE:(j + 1) * ATT_TILE].astype(BF16)
    vb = proj(7)
    vb_ref[...] = vb.astype(BF16)
    vc = proj(8)
    vc_ref[...] = vc.astype(BF16)
    cvt_ref[0, 0] = jnp.transpose(vc)[:W_C]

    @pl.when(pl.program_id(1) == pl.num_programs(1) - 1)
    def _():
        keep = kb.shape[0] - BAND
        bkt_ref[0, 0] = jnp.transpose(kb)[:W_B, keep:]
        bvt_ref[0, 0] = jnp.transpose(vb)[:W_B, keep:]

    lf = _log_forget(proj(9, LANES), bf_ref)
    logf_ref[...] = lf
    lft_ref[0] = jnp.transpose(lf)[:H_A]


def _inproj_prompt(x, lw, layer, depth, nb, s, prev_rows):
    tm = PROJ_TILE
    nt = s // tm
    t = nb * s
    per = tm // ATT_TILE
    row = lambda w: pl.BlockSpec((tm, w), lambda b, i: (b * nt + i, 0))
    full = lambda a: pl.BlockSpec(a.shape, lambda b, i: (0,) * a.ndim, pipeline_mode=pl.Buffered(1))
    feat = lambda w: pl.BlockSpec((1, 1, w, tm), lambda b, i: (layer, b, 0, i))
    band = lambda w: pl.BlockSpec((1, 1, w, BAND), lambda b, i: (layer, b, 0, 0))
    tok = jax.ShapeDtypeStruct((t, GROUP_W), BF16)
    out_shape = [
        jax.ShapeDtypeStruct((GROUP_W, t), BF16), tok,
        jax.ShapeDtypeStruct((nb, s // ATT_TILE, GROUP_W, ATT_TILE), BF16),
        tok, tok, tok, tok, tok, tok, jax.ShapeDtypeStruct((t, LANES), F32),
        jax.ShapeDtypeStruct((depth, nb, W_A, s), F32), jax.ShapeDtypeStruct((depth, nb, W_A, s), F32),
        jax.ShapeDtypeStruct((depth, nb, W_B, BAND), F32), jax.ShapeDtypeStruct((depth, nb, W_B, BAND), F32),
        jax.ShapeDtypeStruct((depth, nb, W_C, s), F32), jax.ShapeDtypeStruct((depth, nb, W_C, s), F32),
        jax.ShapeDtypeStruct((depth, H_A, t), F32)]
    out_specs = [
        pl.BlockSpec((GROUP_W, tm), lambda b, i: (0, b * nt + i)), row(GROUP_W),
        pl.BlockSpec((1, per, GROUP_W, ATT_TILE), lambda b, i: (b, i, 0, 0)),
        row(GROUP_W), row(GROUP_W), row(GROUP_W), row(GROUP_W), row(GROUP_W), row(GROUP_W), row(LANES),
        feat(W_A), feat(W_A), band(W_B), band(W_B), feat(W_C), feat(W_C),
        pl.BlockSpec((1, H_A, tm), lambda b, i: (layer, 0, b * nt + i))]
    ins = [x, lw['norm_mix'], lw['w_in'], lw['gq'], lw['gk'], lw['b_f']]
    in_specs = [row(D_MODEL)] + [full(a) for a in ins[1:]]
    aliases = {}
    n_alias = 0
    if prev_rows is not None:
        n_alias = N_PROMPT_ROWS
        first_row_out = len(out_shape) - N_PROMPT_ROWS
        for j, a in enumerate(prev_rows):
            aliases[len(ins)] = first_row_out + j
            ins.append(a)
            in_specs.append(pl.BlockSpec(memory_space=pl.ANY))
    return pl.pallas_call(
        functools.partial(_inproj_prompt_kernel, n_alias=n_alias), grid=(nb, nt),
        in_specs=in_specs, out_specs=out_specs, out_shape=out_shape,
        input_output_aliases=aliases,
        compiler_params=pltpu.CompilerParams(dimension_semantics=("parallel", "arbitrary"),
                                             vmem_limit_bytes=PROJ_VMEM_LIMIT),
        name="inproj_prompt")(*ins)


def _forget_aug_kernel(logf_ref, kaug_ref, qaug_ref, *, nblk):
    tb = ATT_TILE
    r = lax.broadcasted_iota(jnp.int32, (tb, tb), 0)
    c = lax.broadcasted_iota(jnp.int32, (tb, tb), 1)
    tri = jnp.where(c <= r, 1.0, 0.0).astype(BF16)
    src = lax.broadcasted_iota(jnp.int32, (LANES, LANES), 0)
    dst = lax.broadcasted_iota(jnp.int32, (LANES, LANES), 1)

    def place(slot):
        return jnp.where((dst == AUG_SLOTS * src + slot) & (src < H_A), 1.0, 0.0).astype(BF16)

    lane = _lane_iota()
    slot = jnp.bitwise_and(lane, AUG_SLOTS - 1)
    used = lane < AUG_SLOTS * H_A
    key_const = jnp.where(used & (slot >= 3) & (slot < 6), 1.0, 0.0)
    qry_const = jnp.where(used & (slot < 3), -1.0, 0.0)
    carry = jnp.zeros((1, LANES), F32)
    for blk in range(nblk):
        rows = slice(blk * tb, (blk + 1) * tb)
        hi, mid, lo = _split3(logf_ref[rows, :])
        cs = (_dot(tri, hi) + _dot(tri, mid) + _dot(tri, lo)) + carry
        carry = cs[tb - 1:tb, :]
        hi, mid, lo = _split3(cs * LOG2E)
        kaug = _dot(hi, place(0)) + _dot(mid, place(1)) + _dot(lo, place(2)) + key_const
        kaug_ref[rows, :] = kaug.astype(BF16)
        qaug = _dot(hi, place(3)) + _dot(mid, place(4)) + _dot(lo, place(5)) + qry_const
        qaug_ref[:, rows] = jnp.transpose(qaug).astype(BF16)


def _forget_aug(logf, nb, s):
    return pl.pallas_call(
        functools.partial(_forget_aug_kernel, nblk=s // ATT_TILE), grid=(nb,),
        in_specs=[pl.BlockSpec((s, LANES), lambda b: (b, 0))],
        out_specs=[pl.BlockSpec((s, LANES), lambda b: (b, 0)), pl.BlockSpec((LANES, s), lambda b: (0, b))],
        out_shape=[jax.ShapeDtypeStruct((nb * s, LANES), BF16), jax.ShapeDtypeStruct((LANES, nb * s), BF16)],
        compiler_params=_cparams("parallel"), name="forget_aug")(logf)


def _suffix_sum_kernel(x_ref, o_ref):
    p = x_ref.shape[1]
    j = lax.broadcasted_iota(jnp.int32, (p, p), 0)
    s = lax.broadcasted_iota(jnp.int32, (p, p), 1)
    tri = jnp.where(j > s, 1.0, 0.0).astype(BF16)
    hi, mid, lo = _split3(x_ref[...])
    o_ref[...] = _dot(hi, tri) + _dot(mid, tri) + _dot(lo, tri)


def _suffix_sum(x):
    return pl.pallas_call(
        _suffix_sum_kernel, out_shape=jax.ShapeDtypeStruct(x.shape, F32),
        compiler_params=pltpu.CompilerParams(vmem_limit_bytes=VMEM_LIMIT), name="suffix_sum")(x)


def _attn_a_kernel(qt_ref, k_ref, vt_ref, kaug_ref, qaug_ref, o_ref, qp_ref, m_ref, l_ref, acc_ref):
    tq = qt_ref.shape[1]
    tk = ATT_TILE
    qi = pl.program_id(1)
    row = lax.broadcasted_iota(jnp.int32, (PAIR_W, 1), 0)
    aug = qaug_ref[...]
    for h in range(H_A):
        q2 = qt_ref[_pair_slice(h), :]
        half = (row < HEAD_DIM) if h % 2 == 0 else (row >= HEAD_DIM)
        qp_ref[h, :PAIR_W, :] = jnp.where(half, q2, jnp.zeros_like(q2))
        mine = (row >= AUG_SLOTS * h) & (row < AUG_SLOTS * (h + 1))
        qp_ref[h, PAIR_W:, :] = jnp.where(mine, aug, jnp.zeros_like(aug))
    m_ref[...] = jnp.full(m_ref.shape, NEG_INF, F32)
    l_ref[...] = jnp.zeros(l_ref.shape, F32)
    acc_ref[...] = jnp.zeros(acc_ref.shape, F32)

    def step(kt, key_offset):
        ks = pl.multiple_of(kt * tk, tk)
        ka = kaug_ref[pl.ds(ks, tk), :]
        tile_scores = [_dot(jnp.concatenate([k_ref[pl.ds(ks, tk), _pair_slice(h)], ka], axis=1), qp_ref[h])
                       for h in range(H_A)]
        probs, alphas = [], []
        for h in range(H_A):
            s = tile_scores[h]
            if key_offset is not None:
                key = lax.broadcasted_iota(jnp.int32, (tk, tq), 0)
                qry = lax.broadcasted_iota(jnp.int32, (tk, tq), 1)
                s = jnp.where(key + key_offset <= qry, s, NEG_INF)
            m_prev = m_ref[h]
            m_new = jnp.maximum(m_prev, jnp.max(s, axis=0, keepdims=True))
            alpha = jnp.exp2(m_prev - m_new)
            p = jnp.exp2(s - m_new)
            l_ref[h] = alpha * l_ref[h] + jnp.sum(p, axis=0, keepdims=True)
            m_ref[h] = m_new
            alphas.append(alpha)
            probs.append(p.astype(BF16))
        for h in range(H_A):
            acc_ref[h] = alphas[h] * acc_ref[h] + _dot(vt_ref[0, kt, _pair_slice(h), :], probs[h])

    def body(kt, carry):
        step(kt, None)
        return carry

    per = tq // tk
    lax.fori_loop(0, qi * per, body, 0)
    for j in range(per):
        step(qi * per + j, j * tk)
    for pair in range(N_PAIRS):
        h = 2 * pair
        out_t = jnp.where(row < HEAD_DIM, acc_ref[h] * (1.0 / l_ref[h]), acc_ref[h + 1] * (1.0 / l_ref[h + 1]))
        o_ref[:, pair * PAIR_W:(pair + 1) * PAIR_W] = jnp.transpose(out_t)


def _attn_a(qat, ka, vat, kaug, qaug, nb, s):
    tq = A_QTILE
    nq = s // tq
    return pl.pallas_call(
        _attn_a_kernel, grid=(nb, nq),
        in_specs=[pl.BlockSpec((GROUP_W, tq), lambda b, i: (0, b * nq + i)),
                  pl.BlockSpec((s, GROUP_W), lambda b, i: (b, 0)),
                  pl.BlockSpec((1, s // ATT_TILE, GROUP_W, ATT_TILE), lambda b, i: (b, 0, 0, 0)),
                  pl.BlockSpec((s, LANES), lambda b, i: (b, 0)),
                  pl.BlockSpec((LANES, tq), lambda b, i: (0, b * nq + i))],
        out_specs=pl.BlockSpec((tq, GROUP_W), lambda b, i: (b * nq + i, 0)),
        out_shape=jax.ShapeDtypeStruct((nb * s, GROUP_W), F32),
        scratch_shapes=[pltpu.VMEM((H_A, 2 * PAIR_W, tq), BF16), pltpu.VMEM((H_A, 1, tq), F32),
                        pltpu.VMEM((H_A, 1, tq), F32), pltpu.VMEM((H_A, PAIR_W, tq), F32)],
        compiler_params=_cparams("parallel", "arbitrary"), name="attn_a")(qat, ka, vat, kaug, qaug)


def _masked_q(q_ref, qm_ref, n_heads):
    for h in range(n_heads):
        q2 = q_ref[:, _pair_slice(h)]
        qm_ref[h] = jnp.where(_half_mask(h % 2), q2, jnp.zeros_like(q2))


def _attn_b_kernel(q_ref, k_ref, v_ref, tab_ref, o_ref):
    tq = ATT_TILE
    win = tq + BAND
    i = pl.program_id(1)
    var = jnp.minimum(i, BAND // tq)
    ws = pl.multiple_of(jnp.maximum(i * tq - BAND, 0), tq)
    heads = range(H_B)
    scores = []
    for h in heads:
        q2 = q_ref[:, _pair_slice(h)]
        qm = jnp.where(_half_mask(h % 2), q2, jnp.zeros_like(q2))
        scores.append(_dot_nt(qm, k_ref[pl.ds(ws, win), _pair_slice(h)]) + tab_ref[var, h])
    probs, norms = [], []
    for h in heads:
        p = jnp.exp2(scores[h] - jnp.max(scores[h], axis=-1, keepdims=True))
        norms.append(1.0 / jnp.sum(p, axis=-1, keepdims=True))
        probs.append(p.astype(BF16))
    res = [_dot(probs[h], v_ref[pl.ds(ws, win), _pair_slice(h)]) * norms[h] for h in heads]
    res.append(jnp.zeros((tq, PAIR_W), F32))
    first = _half_mask(0)
    for pair in range(N_PAIRS):
        o_ref[:, pair * PAIR_W:(pair + 1) * PAIR_W] = jnp.where(first, res[2 * pair], res[2 * pair + 1])


def _suffix_tri(n):
    j = lax.broadcasted_iota(jnp.int32, (2 * n, n), 0)
    s = lax.broadcasted_iota(jnp.int32, (2 * n, n), 1)
    return jnp.where(jnp.where(j >= n, j - n, j) >= s, 1.0, 0.0).astype(BF16)


def _stick_scores(z, tri, seen):
    sp = _softplus2(z)
    if seen is not None:
        sp = jnp.where(seen, sp, 0.0)
    hi, lo = _split2(sp)
    n = sp.shape[1]
    if n % LANES == 0:
        return _dot(jnp.concatenate([hi, lo], axis=1), tri)
    return _dot(hi, tri[:n]) + _dot(lo, tri[:n])


def _stick_weights(z, s_in, carry, seen):
    a = jnp.exp2(z - (s_in + carry))
    if seen is not None:
        a = jnp.where(seen, a, 0.0)
    return a.astype(BF16)


def _attn_c_kernel(q_ref, k_ref, v_ref, o_ref, qm_ref, carry_ref, acc_ref):
    tq = q_ref.shape[0]
    tk = ATT_TILE
    qi = pl.program_id(1)
    _masked_q(q_ref, qm_ref, H_C)
    carry_ref[...] = jnp.zeros(carry_ref.shape, F32)
    acc_ref[...] = jnp.zeros(acc_ref.shape, F32)
    heads = range(H_C)

    def step(kt, key_offset):
        ks = pl.multiple_of(kt * tk, tk)
        zs = [_dot_nt(qm_ref[h], k_ref[pl.ds(ks, tk), _pair_slice(h)]) for h in heads]
        tri = _suffix_tri(tk)
        seen = None
        if key_offset is not None:
            r = lax.broadcasted_iota(jnp.int32, (tq, tk), 0)
            c = lax.broadcasted_iota(jnp.int32, (tq, tk), 1)
            seen = c + key_offset < r
        sums = [_stick_scores(zs[h], tri, seen) for h in heads]
        weights = []
        for h in heads:
            carry = carry_ref[h][:, 0:1]
            weights.append(_stick_weights(zs[h], sums[h], carry, seen))
            carry_ref[h] = jnp.broadcast_to(carry + sums[h][:, 0:1], (tq, LANES))
        for h in heads:
            acc_ref[h] += _dot(weights[h], v_ref[pl.ds(ks, tk), _pair_slice(h)])

    per = tq // tk

    def body(j, c):
        step(qi * per - 1 - j, None)
        return c

    for j in reversed(range(per)):
        step(qi * per + j, j * tk)
    lax.fori_loop(0, qi * per, body, 0)
    first = _half_mask(0)
    for pair in range(N_PAIRS):
        h = 2 * pair
        second = acc_ref[h + 1] if h + 1 < H_C else jnp.zeros((tq, PAIR_W), F32)
        o_ref[:, pair * PAIR_W:(pair + 1) * PAIR_W] = jnp.where(first, acc_ref[h], second)


def _prompt_attention(kernel, name, q, k, v, extra, extra_specs, scratch, nb, s, tq):
    nq = s // tq
    qspec = pl.BlockSpec((tq, GROUP_W), lambda b, i: (b * nq + i, 0))
    kvspec = pl.BlockSpec((s, GROUP_W), lambda b, i: (b, 0))
    return pl.pallas_call(
        kernel, grid=(nb, nq),
        in_specs=[qspec, kvspec, kvspec] + extra_specs,
        out_specs=qspec, out_shape=jax.ShapeDtypeStruct((nb * s, GROUP_W), F32),
        scratch_shapes=scratch,
        compiler_params=_cparams("parallel", "arbitrary"), name=name)(q, k, v, *extra)


def _sample_heads(n_heads, q_ref, width):
    out = []
    for h in range(n_heads):
        lo = (h // 2) * PAIR_W
        span = slice(lo, min(lo + PAIR_W, width))
        q2 = q_ref[:, span]
        if span.stop - span.start == PAIR_W:
            q2 = jnp.where(_half_mask(h % 2), q2, jnp.zeros_like(q2))
        out.append((h, span, q2))
    return out


def _store_heads(o_ref, res, n_heads):
    for pair in range(N_PAIRS):
        lo = pair * PAIR_W
        h = 2 * pair
        if h + 1 < n_heads:
            o_ref[:, lo:lo + PAIR_W] = jnp.where(_half_mask(0), res[h], res[h + 1])
        else:
            o_ref[:, lo:lo + HEAD_DIM] = res[h]
            o_ref[:, lo + HEAD_DIM:lo + PAIR_W] = jnp.zeros_like(res[h])


def _joint_softmax_pv(s_c, s_n, vt_c, v_n):
    m = jnp.maximum(jnp.max(s_c, axis=-1, keepdims=True), jnp.max(s_n, axis=-1, keepdims=True))
    p_c = jnp.exp2(s_c - m)
    p_n = jnp.exp2(s_n - m)
    l = jnp.sum(p_c, axis=-1, keepdims=True) + jnp.sum(p_n, axis=-1, keepdims=True)
    return (_dot_nt(p_c.astype(BF16), vt_c) + _dot(p_n.astype(BF16), v_n)) / l


def _sample_attn_kernel(qa_ref, qb_ref, qc_ref, kan_ref, kbn_ref, kcn_ref, van_ref, vbn_ref, vcn_ref,
                        cak_ref, cav_ref, cbk_ref, cbv_ref, cck_ref, ccv_ref,
                        rsum_ref, logf_ref, tabc_ref, tabn_ref,
                        oa_ref, ob_ref, oc_ref):
    t = qa_ref.shape[0]
    past = cak_ref.shape[3]
    r = lax.broadcasted_iota(jnp.int32, (t, t), 0)
    c = lax.broadcasted_iota(jnp.int32, (t, t), 1)

    ltri = jnp.where(c <= r, 1.0, 0.0).astype(BF16)
    hi, mid, lo = _split3(logf_ref[...])
    pcol = (_dot(ltri, hi) + _dot(ltri, mid) + _dot(ltri, lo)) * LOG2E
    prow = jnp.transpose(jnp.concatenate([pcol, jnp.zeros((LANES - t, LANES), F32)], axis=0))
    heads = _sample_heads(H_A, qa_ref, W_A)
    kts = {sp.start: cak_ref[0, 0, sp, :].astype(BF16) for _, sp, _ in heads}
    vts = {sp.start: cav_ref[0, 0, sp, :].astype(BF16) for _, sp, _ in heads}
    s_c = [_dot(qm, kts[sp.start]) + (pcol[:, h:h + 1] + rsum_ref[0, 0, h:h + 1, :] * LOG2E)
           for h, sp, qm in heads]
    s_n = [jnp.where(c <= r, _dot_nt(qm, kan_ref[:, sp]) + (pcol[:, h:h + 1] - prow[h:h + 1, 0:t]), NEG_INF)
           for h, sp, qm in heads]
    _store_heads(oa_ref, [_joint_softmax_pv(s_c[h], s_n[h], vts[sp.start], van_ref[:, sp])
                          for h, sp, _ in heads], H_A)

    heads = _sample_heads(H_B, qb_ref, W_B)
    kts = {sp.start: cbk_ref[0, 0, sp, :].astype(BF16) for _, sp, _ in heads}
    vts = {sp.start: cbv_ref[0, 0, sp, :].astype(BF16) for _, sp, _ in heads}
    s_c = [_dot(qm, kts[sp.start]) + tabc_ref[h] for h, sp, qm in heads]
    s_n = [_dot_nt(qm, kbn_ref[:, sp]) + tabn_ref[h] for h, sp, qm in heads]
    _store_heads(ob_ref, [_joint_softmax_pv(s_c[h], s_n[h], vts[sp.start], vbn_ref[:, sp])
                          for h, sp, _ in heads], H_B)

    heads = _sample_heads(H_C, qc_ref, W_C)
    nblk = past // ATT_TILE
    tri_n = _suffix_tri(t)
    tri_c = _suffix_tri(ATT_TILE)
    seen = c < r
    kts = {sp.start: cck_ref[0, 0, sp, :].astype(BF16) for _, sp, _ in heads}
    vts = {sp.start: ccv_ref[0, 0, sp, :].astype(BF16) for _, sp, _ in heads}
    z_n = [_dot_nt(qm, kcn_ref[:, sp]) for _, sp, qm in heads]
    z_c = [_dot(qm, kts[sp.start]) for _, sp, qm in heads]
    sum_n = [_stick_scores(z, tri_n, seen) for z in z_n]
    sum_c = [[_stick_scores(z[:, b * ATT_TILE:(b + 1) * ATT_TILE], tri_c, None) for b in range(nblk)]
             for z in z_c]
    res = []
    for h, sp, _ in heads:
        acc = _dot(_stick_weights(z_n[h], sum_n[h], 0.0, seen), vcn_ref[:, sp])
        carry = sum_n[h][:, 0:1]
        for b in reversed(range(nblk)):
            cols = slice(b * ATT_TILE, (b + 1) * ATT_TILE)
            a = _stick_weights(z_c[h][:, cols], sum_c[h][b], carry, None)
            acc = acc + _dot_nt(a, vts[sp.start][:, cols])
            carry = carry + sum_c[h][b][:, 0:1]
        res.append(acc)
    _store_heads(oc_ref, res, H_C)


def _sample_attention(proj, caches, layer, rsum, tabc, tabn, nb, t):
    qa, qb, qc, ka, kb, kc, va, vb, vc = proj[:9]
    logf = proj[15]
    new = pl.BlockSpec((t, GROUP_W), lambda b: (b, 0))
    cache = lambda a: pl.BlockSpec((1, 1) + a.shape[2:], lambda b: (layer, b, 0, 0))
    full = lambda a: pl.BlockSpec(a.shape, lambda b: (0,) * a.ndim)
    return pl.pallas_call(
        _sample_attn_kernel, grid=(nb,),
        in_specs=[new] * 9 + [cache(a) for a in caches] + [
            cache(rsum), pl.BlockSpec((t, LANES), lambda b: (b, 0)), full(tabc), full(tabn)],
        out_specs=[new] * 3,
        out_shape=[jax.ShapeDtypeStruct((nb * t, GROUP_W), F32)] * 3,
        compiler_params=_cparams("parallel"), name="sample_attn")(
            qa, qb, qc, ka, kb, kc, va, vb, vc, *caches, rsum, logf, tabc, tabn)


def _masked_max(x, mask):
    return jnp.max(jnp.where(mask, x, -jnp.inf), axis=-1, keepdims=True)


def _first_lane(mask, lane):
    return jnp.min(jnp.where(mask, lane, float(LANES)), axis=-1, keepdims=True)


def _route(logits):
    lane = _lane_iota().astype(F32)
    is_g = (lane >= N_EXPERTS) & (lane < N_EXPERTS + N_GROUPS)
    gmax = _masked_max(logits, is_g)
    p_g = 1.0 / jnp.sum(jnp.where(is_g, jnp.exp(logits - gmax), 0.0), axis=-1, keepdims=True)
    g_sel = _first_lane(is_g & (logits == gmax), lane) - N_EXPERTS
    lo = g_sel * EXPERTS_PER_GROUP
    in_g = (lane >= lo) & (lane < lo + EXPERTS_PER_GROUP)
    l1 = _masked_max(logits, in_g)
    i1 = _first_lane(in_g & (logits == l1), lane)
    rest = in_g & (lane != i1)
    l2 = _masked_max(logits, rest)
    i2 = _first_lane(rest & (logits == l2), lane)
    e2 = jnp.exp(l2 - l1)
    w1 = p_g / (1.0 + e2)
    w2 = p_g * e2 / (1.0 + e2)
    gates = jnp.where(lane == i1, w1, jnp.where(lane == i2, w2, 0.0))
    return jnp.where(lane == g_sel + N_EXPERTS, 1.0, gates)


def _post_kernel(oa_ref, ob_ref, oc_ref, h_ref, gmix_ref, wout_ref, nffn_ref, wrh_ref, wrl_ref, br_ref,
                 h1_ref, xn_ref, gates_ref):
    def gnorm(o, width):
        ms = jnp.sum(o * o, axis=-1, keepdims=True) * (1.0 / width)
        return o * lax.rsqrt(ms + EPS)

    h1 = h_ref[...]
    for g, (o_ref, width) in enumerate(((oa_ref, W_A), (ob_ref, W_B), (oc_ref, W_C))):
        merged = (gnorm(o_ref[...], width) * gmix_ref[g]).astype(BF16)
        h1 = h1 + _dot(merged, wout_ref[g])
    h1_ref[...] = h1
    xf = _rms(h1, nffn_ref[...])
    hi, lo = _split2(xf)
    xn_ref[...] = hi
    logits = _dot(hi, wrh_ref[...]) + _dot(lo, wrh_ref[...]) + _dot(hi, wrl_ref[...]) + br_ref[...]
    gates_ref[...] = _route(logits)


def _post(oa, ob, oc, h, lw, tm):
    t = h.shape[0]
    row = lambda w: pl.BlockSpec((tm, w), lambda i: (i, 0))
    full = lambda a: pl.BlockSpec(a.shape, lambda i: (0,) * a.ndim)
    ws = (lw['g_mix'], lw['w_out'], lw['norm_ffn'], lw['w_r_hi'], lw['w_r_lo'], lw['b_r'])
    return pl.pallas_call(
        _post_kernel, grid=(t // tm,),
        in_specs=[row(GROUP_W)] * 3 + [row(D_MODEL)] + [full(a) for a in ws],
        out_specs=[row(D_MODEL), row(D_MODEL), row(LANES)],
        out_shape=[jax.ShapeDtypeStruct((t, D_MODEL), F32), jax.ShapeDtypeStruct((t, D_MODEL), BF16),
                   jax.ShapeDtypeStruct((t, LANES), F32)],
        compiler_params=_cparams("parallel"), name="post")(oa, ob, oc, h, *ws)


def _moe_kernel(xn_ref, gates_ref, wg_ref, wu_ref, wd_ref, h1_ref, p_ref, wple_ref, nple_ref, wpg_ref,
                out_ref, acc_ref):
    step = pl.program_id(1)

    @pl.when(step == 0)
    def _():
        acc_ref[...] = jnp.zeros_like(acc_ref)

    x = xn_ref[...]
    gates = gates_ref[...]
    acts = []
    for k in range(EXPERTS_PER_STEP):
        e = step * EXPERTS_PER_STEP + k
        g = _dot(x, wg_ref[k])
        u = _dot(x, wu_ref[k])
        gate = jnp.sum(jnp.where(_lane_iota() == e, gates, 0.0), axis=-1, keepdims=True)
        acts.append(((g * _sigmoid(g)) * u * gate).astype(BF16))
    acc_ref[...] += _dot(jnp.concatenate(acts, axis=1), wd_ref[...])

    @pl.when(step == pl.num_programs(1) - 1)
    def _():
        h2 = h1_ref[...] + acc_ref[...]
        gate_ple = _sigmoid(_dot(_rms(h2, nple_ref[...]).astype(BF16), wpg_ref[...]))
        out_ref[...] = h2 + _dot(p_ref[0].astype(BF16), wple_ref[...]) * gate_ple


def _moe(xn, gates, h1, p_all, layer, lw, tm):
    t = xn.shape[0]
    row = lambda w: pl.BlockSpec((tm, w), lambda i, e: (i, 0))
    once = pl.Buffered(1)
    full = lambda a: pl.BlockSpec(a.shape, lambda i, e: (0,) * a.ndim, pipeline_mode=once)
    exp = lambda a: pl.BlockSpec((EXPERTS_PER_STEP,) + a.shape[1:], lambda i, e: (e, 0, 0))
    d_hidden = lw['w_ed'].shape[0] // N_EXPERTS
    return pl.pallas_call(
        _moe_kernel, grid=(t // tm, N_EXPERTS // EXPERTS_PER_STEP),
        in_specs=[row(D_MODEL), row(LANES), exp(lw['w_eg']), exp(lw['w_eu']),
                  pl.BlockSpec((EXPERTS_PER_STEP * d_hidden, D_MODEL), lambda i, e: (e, 0)),
                  pl.BlockSpec((tm, D_MODEL), lambda i, e: (i, 0), pipeline_mode=once),
                  pl.BlockSpec((1, tm, p_all.shape[2]), lambda i, e: (layer, i, 0), pipeline_mode=once),
                  full(lw['w_ple']), full(lw['norm_ple']), full(lw['w_pg'])],
        out_specs=row(D_MODEL), out_shape=jax.ShapeDtypeStruct((t, D_MODEL), F32),
        scratch_shapes=[pltpu.VMEM((tm, D_MODEL), F32)],
        compiler_params=pltpu.CompilerParams(dimension_semantics=("parallel", "arbitrary"),
                                             vmem_limit_bytes=MOE_VMEM_LIMIT), name="moe")(
            xn, gates, lw['w_eg'], lw['w_eu'], lw['w_ed'], h1, p_all, lw['w_ple'], lw['norm_ple'], lw['w_pg'])


SORT_TILE = 512
SC_WINDOW = 128
SC_ROW_WORDS = 256


def _plan_kernel(gates_ref, dest_ref, total_ref, *, tb):
    lane = _lane_iota()
    is_g = (lane >= N_EXPERTS) & (lane < N_EXPERTS + N_GROUPS)
    total = jnp.sum(jnp.where(is_g, gates_ref[...], 0.0), axis=0, keepdims=True)
    padded = jnp.floor((total + (SORT_TILE - 1)) * (1.0 / SORT_TILE)) * SORT_TILE
    base = jnp.zeros((1, LANES), F32)
    for g in range(1, N_GROUPS):
        before = jnp.sum(jnp.where(lane < N_EXPERTS + g, padded, 0.0), axis=-1, keepdims=True)
        base = jnp.where(lane == N_EXPERTS + g, before, base)
    r = lax.broadcasted_iota(jnp.int32, (tb, tb), 0)
    c = lax.broadcasted_iota(jnp.int32, (tb, tb), 1)
    ltri = jnp.where(c < r, 1.0, 0.0).astype(BF16)
    carry = jnp.zeros((1, LANES), F32)
    for blk in range(gates_ref.shape[0] // tb):
        rows = slice(blk * tb, (blk + 1) * tb)
        member = jnp.where(is_g, gates_ref[rows, :], 0.0)
        rank = _dot(ltri, member.astype(BF16)) + carry
        carry = carry + jnp.sum(member, axis=0, keepdims=True)
        dest = jnp.sum(member * (base + rank), axis=-1, keepdims=True)
        dest_ref[rows, :] = jnp.broadcast_to(dest, (tb, LANES))
    total_ref[...] = jnp.broadcast_to(total, total_ref.shape)


def _plan(gates):
    t = gates.shape[0]
    return pl.pallas_call(
        functools.partial(_plan_kernel, tb=SORT_TILE),
        out_shape=[jax.ShapeDtypeStruct((t, LANES), F32), jax.ShapeDtypeStruct((SUBLANES, LANES), F32)],
        compiler_params=pltpu.CompilerParams(vmem_limit_bytes=VMEM_LIMIT), name="moe_plan")(gates)


def _sc_scatter_rows(x, idx, n_out):
    n, d = x.shape
    mesh = plsc.VectorSubcoreMesh(core_axis_name="core", subcore_axis_name="subcore")

    @functools.partial(pl.kernel, out_type=jax.ShapeDtypeStruct((n_out, d), x.dtype), mesh=mesh, scratch_types=[])
    def scatter(x_hbm, i_hbm, o_hbm):
        def body(x_vmem, i_vmem):
            pltpu.sync_copy(x_vmem, o_hbm.at[i_vmem.at[0]])

        pltpu.emit_pipeline(
            body, grid=(n // SC_WINDOW,),
            in_specs=[pl.BlockSpec((SC_WINDOW, d), index_map=lambda i: (i, 0)),
                      pl.BlockSpec((1, SC_WINDOW), index_map=lambda i: (0, i))],
            out_specs=[], core_axis_name='subcore', dimension_semantics=(pltpu.PARALLEL,))(x_hbm, i_hbm)

    return scatter(x, idx.reshape(1, n))


def _sc_gather_rows(x, idx):
    n = idx.shape[0]
    d = x.shape[1]
    mesh = plsc.VectorSubcoreMesh(core_axis_name="core", subcore_axis_name="subcore")

    @functools.partial(pl.kernel, out_type=jax.ShapeDtypeStruct((n, d), x.dtype), mesh=mesh)
    def gather(x_hbm, i_hbm, o_hbm):
        def body(i_vmem, o_vmem):
            pltpu.sync_copy(x_hbm.at[i_vmem.at[0]], o_vmem)

        pltpu.emit_pipeline(
            body, grid=(n // SC_WINDOW,),
            in_specs=[pl.BlockSpec((1, SC_WINDOW), index_map=lambda i: (0, i))],
            out_specs=[pl.BlockSpec((SC_WINDOW, d), index_map=lambda i: (i, 0))],
            core_axis_name='subcore', dimension_semantics=(pltpu.PARALLEL,))(i_hbm, o_hbm)

    return gather(x, idx.reshape(1, n))


def _split_rows(idx, k):
    return (idx[:, None] * k + jnp.arange(k, dtype=idx.dtype)[None, :]).reshape(-1)


def _experts_sorted_kernel(tg_ref, tv_ref, xs_ref, gs_ref, wg_ref, wu_ref, wd_ref, ys_ref):
    j = pl.program_id(0)

    @pl.when(tv_ref[j] > 0)
    def _():
        g = tg_ref[j]
        x = xs_ref[...]
        src = lax.broadcasted_iota(jnp.int32, (LANES, D_MODEL), 0)
        dst = lax.broadcasted_iota(jnp.int32, (LANES, D_MODEL), 1)
        d_shift = (D_MODEL // EXPERTS_PER_GROUP).bit_length() - 1
        expand = jnp.where(src == g * EXPERTS_PER_GROUP + jnp.right_shift(dst, d_shift), 1.0, 0.0).astype(BF16)
        hi, mid, lo = _split3(gs_ref[...])
        ge = _dot(hi, expand) + _dot(mid, expand) + _dot(lo, expand)
        hg = _dot(x, wg_ref[0])
        act = (hg * _sigmoid(hg)) * _dot(x, wu_ref[0]) * ge
        ys_ref[...] = _dot(act.astype(BF16), wd_ref[0])


def _experts_sorted(xs, gs, tile_group, tile_valid, lw):
    n_rows = xs.shape[0]
    row = lambda w: pl.BlockSpec((SORT_TILE, w), lambda j, tg, tv: (j, 0))
    grp = lambda a: pl.BlockSpec((1,) + a.shape[1:], lambda j, tg, tv: (tg[j], 0, 0))
    return pl.pallas_call(
        _experts_sorted_kernel,
        grid_spec=pltpu.PrefetchScalarGridSpec(
            num_scalar_prefetch=2, grid=(n_rows // SORT_TILE,),
            in_specs=[row(D_MODEL), row(LANES), grp(lw['w_gg']), grp(lw['w_gu']), grp(lw['w_gd'])],
            out_specs=row(D_MODEL)),
        out_shape=jax.ShapeDtypeStruct((n_rows, D_MODEL), F32),
        compiler_params=_cparams("arbitrary"), name="moe_sorted")(
            tile_group, tile_valid, xs, gs, lw['w_gg'], lw['w_gu'], lw['w_gd'])


def _ple_kernel(h1_ref, y_ref, p_ref, wple_ref, nple_ref, wpg_ref, out_ref):
    h2 = h1_ref[...] + y_ref[...]
    gate_ple = _sigmoid(_dot(_rms(h2, nple_ref[...]).astype(BF16), wpg_ref[...]))
    out_ref[...] = h2 + _dot(p_ref[0].astype(BF16), wple_ref[...]) * gate_ple


def _ple(h1, y, p_all, layer, lw, tm):
    t = h1.shape[0]
    row = lambda w: pl.BlockSpec((tm, w), lambda i: (i, 0))
    full = lambda a: pl.BlockSpec(a.shape, lambda i: (0,) * a.ndim)
    return pl.pallas_call(
        _ple_kernel, grid=(t // tm,),
        in_specs=[row(D_MODEL), row(D_MODEL), pl.BlockSpec((1, tm, p_all.shape[2]), lambda i: (layer, i, 0)),
                  full(lw['w_ple']), full(lw['norm_ple']), full(lw['w_pg'])],
        out_specs=row(D_MODEL), out_shape=jax.ShapeDtypeStruct((t, D_MODEL), F32),
        compiler_params=_cparams("parallel"), name="ple")(h1, y, p_all, lw['w_ple'], lw['norm_ple'], lw['w_pg'])


def _moe_sorted(xn, gates, h1, p_all, layer, lw):
    t = xn.shape[0]
    n_rows = t + N_GROUPS * SORT_TILE
    dest_rep, total = _plan(gates)
    dest = dest_rep[:, 0].astype(jnp.int32)
    padded = (total[0, N_EXPERTS:N_EXPERTS + N_GROUPS].astype(jnp.int32) + (SORT_TILE - 1)) // SORT_TILE * SORT_TILE
    ends = jnp.cumsum(padded)
    tile_start = jnp.arange(n_rows // SORT_TILE, dtype=jnp.int32) * SORT_TILE
    tile_group = jnp.minimum(jnp.sum(tile_start[:, None] >= ends[None, :], axis=1), N_GROUPS - 1).astype(jnp.int32)
    tile_valid = (tile_start < ends[-1]).astype(jnp.int32)
    k_x = D_MODEL // 2 // SC_ROW_WORDS
    x_words = lax.bitcast_convert_type(xn.reshape(t, D_MODEL // 2, 2), jnp.uint32).reshape(t * k_x, SC_ROW_WORDS)
    xs = _sc_scatter_rows(x_words, _split_rows(dest, k_x), n_rows * k_x)
    xs = lax.bitcast_convert_type(xs.reshape(n_rows, D_MODEL // 2), BF16).reshape(n_rows, D_MODEL)
    gs = _sc_scatter_rows(gates, dest, n_rows)
    ys = _experts_sorted(xs, gs, tile_group, tile_valid, lw)
    k_y = D_MODEL // SC_ROW_WORDS
    y = _sc_gather_rows(ys.reshape(n_rows * k_y, SC_ROW_WORDS), _split_rows(dest, k_y)).reshape(t, D_MODEL)
    return _ple(h1, y, p_all, layer, lw, _token_tile(t, 512))


def _regroup_out_rows(wt):
    z = jnp.zeros((HEAD_DIM, wt.shape[1]), wt.dtype)
    return jnp.concatenate([wt[:W_A], wt[W_A:W_A + W_B], z, wt[W_A + W_B:], z], axis=0)


def _regroup_rows(w):
    z = jnp.zeros((HEAD_DIM, w.shape[1]), w.dtype)
    return jnp.stack([w[:W_A], jnp.concatenate([w[W_A:W_A + W_B], z], axis=0),
                      jnp.concatenate([w[W_A + W_B:], z], axis=0)])


def _pad_lanes(a, n=LANES):
    return jnp.pad(a, [(0, 0)] * (a.ndim - 1) + [(0, n - a.shape[-1])])


def _toeplitz_bias(rel_bias, off, tq, win):
    length = tq + win
    pad = length + abs(off)
    ext = jnp.flip(jnp.pad(rel_bias, ((0, 0), (pad, pad)), mode='edge'), axis=1)
    s1 = ext.shape[1] - 1 - (off + REL_CLIP + pad)
    v = jnp.concatenate([ext[:, s1:s1 + win], ext[:, s1 - tq:s1]], axis=1)
    flat = jnp.tile(v, (1, tq))[:, :tq * (length - 1)]
    return flat.reshape(-1, tq, length - 1)[:, :, :win]


def _block_toeplitz_bias(rel_bias, off, tq, win):
    nq, nk = tq // LANES, win // LANES
    blocks = {d: _toeplitz_bias(rel_bias, off - d * LANES, LANES, LANES) for d in range(-(nq - 1), nk)}
    return jnp.concatenate(
        [jnp.concatenate([blocks[b - a] for b in range(nk)], axis=2) for a in range(nq)], axis=1)


def _rel_tables(rel_bias, t_new, rows_b):
    tq, win = ATT_TILE, ATT_TILE + BAND
    q = jnp.arange(tq)[:, None]
    tabs = []
    for var in range(BAND // tq + 1):
        rel_k = jnp.arange(win)[None, :] - var * tq
        kch = jnp.floor_divide(rel_k, CHUNK)
        qch = q // CHUNK
        valid = (kch <= qch) & (kch >= qch - BAND_CHUNKS)
        tabs.append(jnp.where(valid[None], _block_toeplitz_bias(rel_bias, var * tq, tq, win), NEG_INF))
    tab_prompt = jnp.stack(tabs).astype(F32)
    tab_c = _toeplitz_bias(rel_bias, rows_b, t_new, rows_b).astype(F32)
    tab_n = _toeplitz_bias(rel_bias, 0, t_new, t_new).astype(F32)
    return tab_prompt, tab_c, tab_n


def _layer_weights(i, norm_mix, w_in, b_f, g_qa, g_ka, g_qb, g_kb, g_mix_out, w_out, norm_ffn,
                   w_rg, b_rg, w_re, b_re, w_eg, w_eu, w_ed, w_ple, norm_ple, w_pg):
    d_mix = W_A + W_B + W_C
    wt = jnp.transpose(w_in, (2, 0, 1))[:, i, :]
    w_f = wt[3 * d_mix:]
    w_big = jnp.concatenate(
        [_regroup_out_rows(wt[j * d_mix:(j + 1) * d_mix]) for j in range(3)]
        + [jnp.pad(w_f, ((0, LANES - w_f.shape[0]), (0, 0)))], axis=0).astype(BF16)
    tile6 = lambda g: jnp.tile(g, GROUP_W // HEAD_DIM)
    w_r = _pad_lanes(jnp.concatenate([w_re[i], w_rg[i]], axis=1))
    w_r_hi = w_r.astype(BF16)
    gm = g_mix_out[i]
    zpad = jnp.zeros((HEAD_DIM,), F32)
    g_mix = jnp.stack([gm[:W_A], jnp.concatenate([gm[W_A:W_A + W_B], zpad]),
                       jnp.concatenate([gm[W_A + W_B:], zpad])])[:, None, :]
    return dict(
        norm_mix=norm_mix[i][None], w_in=w_big,
        gq=jnp.concatenate([tile6(g_qa[i]), tile6(g_qb[i])])[None],
        gk=jnp.concatenate([tile6(g_ka[i]), tile6(g_kb[i])])[None],
        b_f=_pad_lanes(b_f[i])[None],
        g_mix=g_mix, w_out=_regroup_rows(w_out[i]).astype(BF16), norm_ffn=norm_ffn[i][None],
        w_r_hi=w_r_hi, w_r_lo=(w_r - w_r_hi.astype(F32)).astype(BF16),
        b_r=_pad_lanes(jnp.concatenate([b_re[i], b_rg[i]]))[None],
        w_eg=w_eg[i].astype(BF16), w_eu=w_eu[i].astype(BF16),
        w_ed=w_ed[i].reshape(-1, w_ed.shape[-1]).astype(BF16),
        w_gg=_group_cols(w_eg[i]), w_gu=_group_cols(w_eu[i]),
        w_gd=w_ed[i].reshape(N_GROUPS, -1, w_ed.shape[-1]).astype(BF16),
        w_ple=w_ple[i].astype(BF16), norm_ple=norm_ple[i][None], w_pg=w_pg[i].astype(BF16))


def _group_cols(w):
    e, d, f = w.shape
    w = w.reshape(N_GROUPS, EXPERTS_PER_GROUP, d, f)
    return jnp.transpose(w, (0, 2, 1, 3)).reshape(N_GROUPS, d, EXPERTS_PER_GROUP * f).astype(BF16)


def _token_tile(t, pref):
    return pref if t % pref == 0 else t


def _feature_major(cache):
    d, n, p, h, e = cache.shape
    return jnp.transpose(cache, (0, 1, 3, 4, 2)).reshape(d, n, h * e, p)


def _position_major(rows, heads):
    d, n, _, p = rows.shape
    return jnp.transpose(rows.reshape(d, n, heads, HEAD_DIM, p), (0, 1, 4, 2, 3))


def kernel(x_prompt, x_sample, p_prompt, p_sample, cache_a_k, cache_a_v, cache_a_logf, cache_b_k, cache_b_v, cache_c_k, cache_c_v, norm_mix, w_in, b_f, g_qa, g_ka, g_qb, g_kb, rel_bias, g_mix_out, w_out, norm_ffn, w_router_group, b_router_group, w_router_expert, b_router_expert, w_exp_gate, w_exp_up, w_exp_down, w_ple, norm_ple, w_ple_gate):
    nb, s, d = x_prompt.shape
    ns, t_new, _ = x_sample.shape
    depth = w_in.shape[0]
    past = cache_a_k.shape[2]
    rows_b = cache_b_k.shape[2]
    assert d == D_MODEL and s % PROJ_TILE == 0 and s >= BAND + ATT_TILE and past % ATT_TILE == 0
    tp, ts = nb * s, ns * t_new

    hp = x_prompt.reshape(tp, d)
    hs = x_sample.reshape(ts, d)
    pp = p_prompt.reshape(depth, tp, -1)
    ps = p_sample.reshape(depth, ts, -1)
    caches = tuple(_feature_major(c) for c in (cache_a_k, cache_a_v, cache_b_k, cache_b_v, cache_c_k, cache_c_v))
    lf_rows = jnp.transpose(cache_a_logf, (0, 3, 1, 2))
    rsum = _suffix_sum(lf_rows.reshape(depth * H_A * ns, past)).reshape(depth, H_A, ns, past)
    rsum = jnp.transpose(rsum, (0, 2, 1, 3))

    rows_p = None
    new_s = [[] for _ in range(7)]
    for i in range(depth):
        lw = _layer_weights(i, norm_mix, w_in, b_f, g_qa, g_ka, g_qb, g_kb, g_mix_out, w_out, norm_ffn,
                            w_router_group, b_router_group, w_router_expert, b_router_expert,
                            w_exp_gate, w_exp_up, w_exp_down, w_ple, norm_ple, w_ple_gate)
        tab_p, tab_c, tab_n = _rel_tables(rel_bias[i] * LOG2E, t_new, rows_b)

        pr = _inproj_prompt(hp, lw, i, depth, nb, s, rows_p)
        qat, ka, vat, qb, kb, vb, qc, kc, vc, logf = pr[:10]
        rows_p = pr[10:]
        kaug, qaug = _forget_aug(logf, nb, s)
        oa = _attn_a(qat, ka, vat, kaug, qaug, nb, s)
        ob = _prompt_attention(
            _attn_b_kernel, "attn_b", qb, kb, vb, (tab_p,),
            [pl.BlockSpec(tab_p.shape, lambda b, j: (0, 0, 0, 0))], [], nb, s, ATT_TILE)
        oc = _prompt_attention(
            _attn_c_kernel, "attn_c", qc, kc, vc, (), [],
            [pltpu.VMEM((H_C, C_QTILE, PAIR_W), BF16), pltpu.VMEM((H_C, C_QTILE, LANES), F32),
             pltpu.VMEM((H_C, C_QTILE, PAIR_W), F32)], nb, s, C_QTILE)
        h1, xn, gates = _post(oa, ob, oc, hp, lw, _token_tile(tp, 512))
        hp = _moe_sorted(xn, gates, h1, pp, i, lw)

        sr = _inproj_sample(hs, lw)
        oa, ob, oc = _sample_attention(sr, caches, i, rsum, tab_c, tab_n, ns, t_new)
        h1, xn, gates = _post(oa, ob, oc, hs, lw, _token_tile(ts, 512))
        hs = _moe(xn, gates, h1, ps, i, lw, _token_tile(ts, 1024))
        akf, avf, bkf, bvf, ckf, cvf, logf = sr[9:]
        rows = (akf.reshape(ns, t_new, H_A, HEAD_DIM), avf.reshape(ns, t_new, H_A, HEAD_DIM),
                logf[:, :H_A].reshape(ns, t_new, H_A),
                bkf.reshape(ns, t_new, H_B, HEAD_DIM), bvf.reshape(ns, t_new, H_B, HEAD_DIM),
                ckf.reshape(ns, t_new, H_C, HEAD_DIM), cvf.reshape(ns, t_new, H_C, HEAD_DIM))
        for j in range(7):
            new_s[j].append(rows[j])

    akt, avt, bkt, bvt, ckt, cvt, lft = rows_p
    outs_p = (_position_major(akt, H_A), _position_major(avt, H_A),
              jnp.transpose(lft.reshape(depth, H_A, nb, s), (0, 2, 3, 1)),
              _position_major(bkt, H_B), _position_major(bvt, H_B),
              _position_major(ckt, H_C), _position_major(cvt, H_C))
    outs_s = [jnp.stack(r, axis=0) for r in new_s]
    return (hp.reshape(nb, s, d), hs.reshape(ns, t_new, d), *outs_p, *outs_s)
```

```python
import functools
import math

import jax
import jax.numpy as jnp
from jax import lax
from jax.experimental import pallas as pl
from jax.experimental.pallas import tpu as pltpu
from jax.experimental.pallas import tpu_sc as plsc

F32 = jnp.float32
BF16 = jnp.bfloat16

D_MODEL = 1024
HEAD_DIM = 64
H_A, H_B, H_C = 6, 5, 5
W_A, W_B, W_C = H_A * HEAD_DIM, H_B * HEAD_DIM, H_C * HEAD_DIM
GROUP_W = 384
PAIR_W = 2 * HEAD_DIM
N_PAIRS = GROUP_W // PAIR_W
CHUNK = 64
BAND_CHUNKS = 8
BAND = BAND_CHUNKS * CHUNK
REL_CLIP = 128
N_GROUPS = 4
EXPERTS_PER_GROUP = 4
N_EXPERTS = N_GROUPS * EXPERTS_PER_GROUP
EPS = 1e-6
ATTN_SCALE = HEAD_DIM ** -0.5
LOG2E = math.log2(math.e)
Q_SCALE = ATTN_SCALE * LOG2E
NEG_INF = -1e30
LANES = 128
SUBLANES = 8
VMEM_LIMIT = 48 * 1024 * 1024
EXPERTS_PER_STEP = 4
MOE_VMEM_LIMIT = 58 * 1024 * 1024

ATT_TILE = 256
A_QTILE = 512
C_QTILE = 256
PROJ_TILE = 1024
PROJ_VMEM_LIMIT = 56 * 1024 * 1024
AUG_SLOTS = 8

_NT = (((1,), (1,)), ((), ()))


def _cparams(*sem):
    return pltpu.CompilerParams(dimension_semantics=sem, vmem_limit_bytes=VMEM_LIMIT)


def _dot(a, b):
    return jnp.dot(a, b, preferred_element_type=F32)


def _dot_nt(a, b):
    return lax.dot_general(a, b, _NT, preferred_element_type=F32)


def _split3(x):
    hi = x.astype(BF16)
    r1 = x - hi.astype(F32)
    mid = r1.astype(BF16)
    lo = (r1 - mid.astype(F32)).astype(BF16)
    return hi, mid, lo


def _split2(x):
    hi = x.astype(BF16)
    lo = (x - hi.astype(F32)).astype(BF16)
    return hi, lo


def _lane_iota(n=LANES):
    return lax.broadcasted_iota(jnp.int32, (1, n), 1)


def _half_mask(half):
    lane = _lane_iota()
    return (lane < HEAD_DIM) if half == 0 else (lane >= HEAD_DIM)


def _pair_slice(h):
    return slice((h // 2) * PAIR_W, (h // 2 + 1) * PAIR_W)


def _softplus2(z2):
    return jnp.maximum(z2, 0.0) + jnp.log2(1.0 + jnp.exp2(-jnp.abs(z2)))


def _sigmoid(z):
    return 1.0 / (1.0 + jnp.exp(-z))


def _rms(x, g):
    return x * lax.rsqrt(jnp.mean(x * x, axis=-1, keepdims=True) + EPS) * g


def _pair_rms(z, gain):
    first = _half_mask(0)
    outs = []
    for j in range(N_PAIRS):
        blk = z[:, j * PAIR_W:(j + 1) * PAIR_W]
        sq = blk * blk
        lo = jnp.sum(jnp.where(first, sq, 0.0), axis=-1, keepdims=True)
        hi = jnp.sum(jnp.where(first, 0.0, sq), axis=-1, keepdims=True)
        ms = jnp.where(first, lo, hi) * (1.0 / HEAD_DIM)
        outs.append(blk * lax.rsqrt(ms + EPS) * gain[:, j * PAIR_W:(j + 1) * PAIR_W])
    return jnp.concatenate(outs, axis=-1)


def _projector(x_ref, gn_ref, w_ref):
    xb = _rms(x_ref[...], gn_ref[...]).astype(BF16)
    return lambda g, width=GROUP_W: _dot_nt(xb, w_ref[g * GROUP_W:g * GROUP_W + width, :])


def _log_forget(zf, bf_ref):
    zf = zf + bf_ref[...]
    return jnp.minimum(zf, 0.0) - jnp.log(1.0 + jnp.exp(-jnp.abs(zf)))


def _inproj_sample_kernel(x_ref, gn_ref, w_ref, gq_ref, gk_ref, bf_ref,
                          qa_ref, qb_ref, qc_ref, ka_ref, kb_ref, kc_ref, va_ref, vb_ref, vc_ref,
                          akf_ref, avf_ref, bkf_ref, bvf_ref, ckf_ref, cvf_ref, logf_ref):
    proj = _projector(x_ref, gn_ref, w_ref)
    qa_ref[...] = (_pair_rms(proj(0), gq_ref[:, :GROUP_W]) * Q_SCALE).astype(BF16)
    qb_ref[...] = (_pair_rms(proj(1), gq_ref[:, GROUP_W:]) * Q_SCALE).astype(BF16)
    qc_ref[...] = (proj(2) * Q_SCALE).astype(BF16)
    for g, norm, bf_ref_, f_ref, width in (
            (3, gk_ref[:, :GROUP_W], ka_ref, akf_ref, W_A), (4, gk_ref[:, GROUP_W:], kb_ref, bkf_ref, W_B),
            (5, None, kc_ref, ckf_ref, W_C), (6, None, va_ref, avf_ref, W_A),
            (7, None, vb_ref, bvf_ref, W_B), (8, None, vc_ref, cvf_ref, W_C)):
        z = proj(g)
        if norm is not None:
            z = _pair_rms(z, norm)
        f_ref[...] = z[:, :width]
        bf_ref_[...] = z.astype(BF16)
    logf_ref[...] = _log_forget(proj(9, LANES), bf_ref)


def _inproj_sample(x, lw):
    t = x.shape[0]
    row = lambda w: pl.BlockSpec((t, w), lambda i: (0, 0))
    full = lambda a: pl.BlockSpec(a.shape, lambda i: (0,) * a.ndim)
    widths = [GROUP_W] * 9 + [W_A, W_A, W_B, W_B, W_C, W_C, LANES]
    out_shape = [jax.ShapeDtypeStruct((t, w), BF16 if j < 9 else F32) for j, w in enumerate(widths)]
    ins = (x, lw['norm_mix'], lw['w_in'], lw['gq'], lw['gk'], lw['b_f'])
    return pl.pallas_call(
        _inproj_sample_kernel, grid=(1,),
        in_specs=[row(D_MODEL)] + [full(a) for a in ins[1:]],
        out_specs=[row(w) for w in widths], out_shape=out_shape,
        compiler_params=_cparams("arbitrary"), name="inproj_sample")(*ins)


N_PROMPT_ROWS = 7


def _inproj_prompt_kernel(*refs, n_alias):
    x_ref, gn_ref, w_ref, gq_ref, gk_ref, bf_ref = refs[:6]
    (qat_ref, ka_ref, vat_ref, qb_ref, kb_ref, vb_ref, qc_ref, kc_ref, vc_ref, logf_ref,
     akt_ref, avt_ref, bkt_ref, bvt_ref, ckt_ref, cvt_ref, lft_ref) = refs[6 + n_alias:]
    proj = _projector(x_ref, gn_ref, w_ref)
    qat_ref[...] = jnp.transpose(_pair_rms(proj(0), gq_ref[:, :GROUP_W]) * Q_SCALE).astype(BF16)
    qb_ref[...] = (_pair_rms(proj(1), gq_ref[:, GROUP_W:]) * Q_SCALE).astype(BF16)
    qc_ref[...] = (proj(2) * Q_SCALE).astype(BF16)
    ka = _pair_rms(proj(3), gk_ref[:, :GROUP_W])
    ka_ref[...] = ka.astype(BF16)
    akt_ref[0, 0] = jnp.transpose(ka)
    kb = _pair_rms(proj(4), gk_ref[:, GROUP_W:])
    kb_ref[...] = kb.astype(BF16)
    kc = proj(5)
    kc_ref[...] = kc.astype(BF16)
    ckt_ref[0, 0] = jnp.transpose(kc)[:W_C]
    vat = jnp.transpose(proj(6))
    avt_ref[0, 0] = vat
    for j in range(vat_ref.shape[1]):
        vat_ref[0, j] = vat[:, j * ATT_TILE:(j + 1) * ATT_TILE].astype(BF16)
    vb = proj(7)
    vb_ref[...] = vb.astype(BF16)
    vc = proj(8)
    vc_ref[...] = vc.astype(BF16)
    cvt_ref[0, 0] = jnp.transpose(vc)[:W_C]

    @pl.when(pl.program_id(1) == pl.num_programs(1) - 1)
    def _():
        keep = kb.shape[0] - BAND
        bkt_ref[0, 0] = jnp.transpose(kb)[:W_B, keep:]
        bvt_ref[0, 0] = jnp.transpose(vb)[:W_B, keep:]

    lf = _log_forget(proj(9, LANES), bf_ref)
    logf_ref[...] = lf
    lft_ref[0] = jnp.transpose(lf)[:H_A]


def _inproj_prompt(x, lw, layer, depth, nb, s, prev_rows):
    tm = PROJ_TILE
    nt = s // tm
    t = nb * s
    per = tm // ATT_TILE
    row = lambda w: pl.BlockSpec((tm, w), lambda b, i: (b * nt + i, 0))
    full = lambda a: pl.BlockSpec(a.shape, lambda b, i: (0,) * a.ndim, pipeline_mode=pl.Buffered(1))
    feat = lambda w: pl.BlockSpec((1, 1, w, tm), lambda b, i: (layer, b, 0, i))
    band = lambda w: pl.BlockSpec((1, 1, w, BAND), lambda b, i: (layer, b, 0, 0))
    tok = jax.ShapeDtypeStruct((t, GROUP_W), BF16)
    out_shape = [
        jax.ShapeDtypeStruct((GROUP_W, t), BF16), tok,
        jax.ShapeDtypeStruct((nb, s // ATT_TILE, GROUP_W, ATT_TILE), BF16),
        tok, tok, tok, tok, tok, tok, jax.ShapeDtypeStruct((t, LANES), F32),
        jax.ShapeDtypeStruct((depth, nb, W_A, s), F32), jax.ShapeDtypeStruct((depth, nb, W_A, s), F32),
        jax.ShapeDtypeStruct((depth, nb, W_B, BAND), F32), jax.ShapeDtypeStruct((depth, nb, W_B, BAND), F32),
        jax.ShapeDtypeStruct((depth, nb, W_C, s), F32), jax.ShapeDtypeStruct((depth, nb, W_C, s), F32),
        jax.ShapeDtypeStruct((depth, H_A, t), F32)]
    out_specs = [
        pl.BlockSpec((GROUP_W, tm), lambda b, i: (0, b * nt + i)), row(GROUP_W),
        pl.BlockSpec((1, per, GROUP_W, ATT_TILE), lambda b, i: (b, i, 0, 0)),
        row(GROUP_W), row(GROUP_W), row(GROUP_W), row(GROUP_W), row(GROUP_W), row(GROUP_W), row(LANES),
        feat(W_A), feat(W_A), band(W_B), band(W_B), feat(W_C), feat(W_C),
        pl.BlockSpec((1, H_A, tm), lambda b, i: (layer, 0, b * nt + i))]
    ins = [x, lw['norm_mix'], lw['w_in'], lw['gq'], lw['gk'], lw['b_f']]
    in_specs = [row(D_MODEL)] + [full(a) for a in ins[1:]]
    aliases = {}
    n_alias = 0
    if prev_rows is not None:
        n_alias = N_PROMPT_ROWS
        first_row_out = len(out_shape) - N_PROMPT_ROWS
        for j, a in enumerate(prev_rows):
            aliases[len(ins)] = first_row_out + j
            ins.append(a)
            in_specs.append(pl.BlockSpec(memory_space=pl.ANY))
    return pl.pallas_call(
        functools.partial(_inproj_prompt_kernel, n_alias=n_alias), grid=(nb, nt),
        in_specs=in_specs, out_specs=out_specs, out_shape=out_shape,
        input_output_aliases=aliases,
        compiler_params=pltpu.CompilerParams(dimension_semantics=("parallel", "arbitrary"),
                                             vmem_limit_bytes=PROJ_VMEM_LIMIT),
        name="inproj_prompt")(*ins)


def _forget_aug_kernel(logf_ref, kaug_ref, qaug_ref, *, nblk):
    tb = ATT_TILE
    r = lax.broadcasted_iota(jnp.int32, (tb, tb), 0)
    c = lax.broadcasted_iota(jnp.int32, (tb, tb), 1)
    tri = jnp.where(c <= r, 1.0, 0.0).astype(BF16)
    src = lax.broadcasted_iota(jnp.int32, (LANES, LANES), 0)
    dst = lax.broadcasted_iota(jnp.int32, (LANES, LANES), 1)

    def place(slot):
        return jnp.where((dst == AUG_SLOTS * src + slot) & (src < H_A), 1.0, 0.0).astype(BF16)

    lane = _lane_iota()
    slot = jnp.bitwise_and(lane, AUG_SLOTS - 1)
    used = lane < AUG_SLOTS * H_A
    key_const = jnp.where(used & (slot >= 3) & (slot < 6), 1.0, 0.0)
    qry_const = jnp.where(used & (slot < 3), -1.0, 0.0)
    carry = jnp.zeros((1, LANES), F32)
    for blk in range(nblk):
        rows = slice(blk * tb, (blk + 1) * tb)
        hi, mid, lo = _split3(logf_ref[rows, :])
        cs = (_dot(tri, hi) + _dot(tri, mid) + _dot(tri, lo)) + carry
        carry = cs[tb - 1:tb, :]
        hi, mid, lo = _split3(cs * LOG2E)
        kaug = _dot(hi, place(0)) + _dot(mid, place(1)) + _dot(lo, place(2)) + key_const
        kaug_ref[rows, :] = kaug.astype(BF16)
        qaug = _dot(hi, place(3)) + _dot(mid, place(4)) + _dot(lo, place(5)) + qry_const
        qaug_ref[:, rows] = jnp.transpose(qaug).astype(BF16)


def _forget_aug(logf, nb, s):
    return pl.pallas_call(
        functools.partial(_forget_aug_kernel, nblk=s // ATT_TILE), grid=(nb,),
        in_specs=[pl.BlockSpec((s, LANES), lambda b: (b, 0))],
        out_specs=[pl.BlockSpec((s, LANES), lambda b: (b, 0)), pl.BlockSpec((LANES, s), lambda b: (0, b))],
        out_shape=[jax.ShapeDtypeStruct((nb * s, LANES), BF16), jax.ShapeDtypeStruct((LANES, nb * s), BF16)],
        compiler_params=_cparams("parallel"), name="forget_aug")(logf)


def _suffix_sum_kernel(x_ref, o_ref):
    p = x_ref.shape[1]
    j = lax.broadcasted_iota(jnp.int32, (p, p), 0)
    s = lax.broadcasted_iota(jnp.int32, (p, p), 1)
    tri = jnp.where(j > s, 1.0, 0.0).astype(BF16)
    hi, mid, lo = _split3(x_ref[...])
    o_ref[...] = _dot(hi, tri) + _dot(mid, tri) + _dot(lo, tri)


def _suffix_sum(x):
    return pl.pallas_call(
        _suffix_sum_kernel, out_shape=jax.ShapeDtypeStruct(x.shape, F32),
        compiler_params=pltpu.CompilerParams(vmem_limit_bytes=VMEM_LIMIT), name="suffix_sum")(x)


def _attn_a_kernel(qt_ref, k_ref, vt_ref, kaug_ref, qaug_ref, o_ref, qp_ref, m_ref, l_ref, acc_ref):
    tq = qt_ref.shape[1]
    tk = ATT_TILE
    qi = pl.program_id(1)
    row = lax.broadcasted_iota(jnp.int32, (PAIR_W, 1), 0)
    aug = qaug_ref[...]
    for h in range(H_A):
        q2 = qt_ref[_pair_slice(h), :]
        half = (row < HEAD_DIM) if h % 2 == 0 else (row >= HEAD_DIM)
        qp_ref[h, :PAIR_W, :] = jnp.where(half, q2, jnp.zeros_like(q2))
        mine = (row >= AUG_SLOTS * h) & (row < AUG_SLOTS * (h + 1))
        qp_ref[h, PAIR_W:, :] = jnp.where(mine, aug, jnp.zeros_like(aug))
    m_ref[...] = jnp.full(m_ref.shape, NEG_INF, F32)
    l_ref[...] = jnp.zeros(l_ref.shape, F32)
    acc_ref[...] = jnp.zeros(acc_ref.shape, F32)

    def step(kt, key_offset):
        ks = pl.multiple_of(kt * tk, tk)
        ka = kaug_ref[pl.ds(ks, tk), :]
        tile_scores = [_dot(jnp.concatenate([k_ref[pl.ds(ks, tk), _pair_slice(h)], ka], axis=1), qp_ref[h])
                       for h in range(H_A)]
        probs, alphas = [], []
        for h in range(H_A):
            s = tile_scores[h]
            if key_offset is not None:
                key = lax.broadcasted_iota(jnp.int32, (tk, tq), 0)
                qry = lax.broadcasted_iota(jnp.int32, (tk, tq), 1)
                s = jnp.where(key + key_offset <= qry, s, NEG_INF)
            m_prev = m_ref[h]
            m_new = jnp.maximum(m_prev, jnp.max(s, axis=0, keepdims=True))
            alpha = jnp.exp2(m_prev - m_new)
            p = jnp.exp2(s - m_new)
            l_ref[h] = alpha * l_ref[h] + jnp.sum(p, axis=0, keepdims=True)
            m_ref[h] = m_new
            alphas.append(alpha)
            probs.append(p.astype(BF16))
        for h in range(H_A):
            acc_ref[h] = alphas[h] * acc_ref[h] + _dot(vt_ref[0, kt, _pair_slice(h), :], probs[h])

    def body(kt, carry):
        step(kt, None)
        return carry

    per = tq // tk
    lax.fori_loop(0, qi * per, body, 0)
    for j in range(per):
        step(qi * per + j, j * tk)
    for pair in range(N_PAIRS):
        h = 2 * pair
        out_t = jnp.where(row < HEAD_DIM, acc_ref[h] * (1.0 / l_ref[h]), acc_ref[h + 1] * (1.0 / l_ref[h + 1]))
        o_ref[:, pair * PAIR_W:(pair + 1) * PAIR_W] = jnp.transpose(out_t)


def _attn_a(qat, ka, vat, kaug, qaug, nb, s):
    tq = A_QTILE
    nq = s // tq
    return pl.pallas_call(
        _attn_a_kernel, grid=(nb, nq),
        in_specs=[pl.BlockSpec((GROUP_W, tq), lambda b, i: (0, b * nq + i)),
                  pl.BlockSpec((s, GROUP_W), lambda b, i: (b, 0)),
                  pl.BlockSpec((1, s // ATT_TILE, GROUP_W, ATT_TILE), lambda b, i: (b, 0, 0, 0)),
                  pl.BlockSpec((s, LANES), lambda b, i: (b, 0)),
                  pl.BlockSpec((LANES, tq), lambda b, i: (0, b * nq + i))],
        out_specs=pl.BlockSpec((tq, GROUP_W), lambda b, i: (b * nq + i, 0)),
        out_shape=jax.ShapeDtypeStruct((nb * s, GROUP_W), F32),
        scratch_shapes=[pltpu.VMEM((H_A, 2 * PAIR_W, tq), BF16), pltpu.VMEM((H_A, 1, tq), F32),
                        pltpu.VMEM((H_A, 1, tq), F32), pltpu.VMEM((H_A, PAIR_W, tq), F32)],
        compiler_params=_cparams("parallel", "arbitrary"), name="attn_a")(qat, ka, vat, kaug, qaug)


def _masked_q(q_ref, qm_ref, n_heads):
    for h in range(n_heads):
        q2 = q_ref[:, _pair_slice(h)]
        qm_ref[h] = jnp.where(_half_mask(h % 2), q2, jnp.zeros_like(q2))


def _attn_b_kernel(q_ref, k_ref, v_ref, tab_ref, o_ref):
    tq = ATT_TILE
    win = tq + BAND
    i = pl.program_id(1)
    var = jnp.minimum(i, BAND // tq)
    ws = pl.multiple_of(jnp.maximum(i * tq - BAND, 0), tq)
    heads = range(H_B)
    scores = []
    for h in heads:
        q2 = q_ref[:, _pair_slice(h)]
        qm = jnp.where(_half_mask(h % 2), q2, jnp.zeros_like(q2))
        scores.append(_dot_nt(qm, k_ref[pl.ds(ws, win), _pair_slice(h)]) + tab_ref[var, h])
    probs, norms = [], []
    for h in heads:
        p = jnp.exp2(scores[h] - jnp.max(scores[h], axis=-1, keepdims=True))
        norms.append(1.0 / jnp.sum(p, axis=-1, keepdims=True))
        probs.append(p.astype(BF16))
    res = [_dot(probs[h], v_ref[pl.ds(ws, win), _pair_slice(h)]) * norms[h] for h in heads]
    res.append(jnp.zeros((tq, PAIR_W), F32))
    first = _half_mask(0)
    for pair in range(N_PAIRS):
        o_ref[:, pair * PAIR_W:(pair + 1) * PAIR_W] = jnp.where(first, res[2 * pair], res[2 * pair + 1])


def _suffix_tri(n):
    j = lax.broadcasted_iota(jnp.int32, (2 * n, n), 0)
    s = lax.broadcasted_iota(jnp.int32, (2 * n, n), 1)
    return jnp.where(jnp.where(j >= n, j - n, j) >= s, 1.0, 0.0).astype(BF16)


def _stick_scores(z, tri, seen):
    sp = _softplus2(z)
    if seen is not None:
        sp = jnp.where(seen, sp, 0.0)
    hi, lo = _split2(sp)
    n = sp.shape[1]
    if n % LANES == 0:
        return _dot(jnp.concatenate([hi, lo], axis=1), tri)
    return _dot(hi, tri[:n]) + _dot(lo, tri[:n])


def _stick_weights(z, s_in, carry, seen):
    a = jnp.exp2(z - (s_in + carry))
    if seen is not None:
        a = jnp.where(seen, a, 0.0)
    return a.astype(BF16)


def _attn_c_kernel(q_ref, k_ref, v_ref, o_ref, qm_ref, carry_ref, acc_ref):
    tq = q_ref.shape[0]
    tk = ATT_TILE
    qi = pl.program_id(1)
    _masked_q(q_ref, qm_ref, H_C)
    carry_ref[...] = jnp.zeros(carry_ref.shape, F32)
    acc_ref[...] = jnp.zeros(acc_ref.shape, F32)
    heads = range(H_C)

    def step(kt, key_offset):
        ks = pl.multiple_of(kt * tk, tk)
        zs = [_dot_nt(qm_ref[h], k_ref[pl.ds(ks, tk), _pair_slice(h)]) for h in heads]
        tri = _suffix_tri(tk)
        seen = None
        if key_offset is not None:
            r = lax.broadcasted_iota(jnp.int32, (tq, tk), 0)
            c = lax.broadcasted_iota(jnp.int32, (tq, tk), 1)
            seen = c + key_offset < r
        sums = [_stick_scores(zs[h], tri, seen) for h in heads]
        weights = []
        for h in heads:
            carry = carry_ref[h][:, 0:1]
            weights.append(_stick_weights(zs[h], sums[h], carry, seen))
            carry_ref[h] = jnp.broadcast_to(carry + sums[h][:, 0:1], (tq, LANES))
        for h in heads:
            acc_ref[h] += _dot(weights[h], v_ref[pl.ds(ks, tk), _pair_slice(h)])

    per = tq // tk

    def body(j, c):
        step(qi * per - 1 - j, None)
        return c

    for j in reversed(range(per)):
        step(qi * per + j, j * tk)
    lax.fori_loop(0, qi * per, body, 0)
    first = _half_mask(0)
    for pair in range(N_PAIRS):
        h = 2 * pair
        second = acc_ref[h + 1] if h + 1 < H_C else jnp.zeros((tq, PAIR_W), F32)
        o_ref[:, pair * PAIR_W:(pair + 1) * PAIR_W] = jnp.where(first, acc_ref[h], second)


def _prompt_attention(kernel, name, q, k, v, extra, extra_specs, scratch, nb, s, tq):
    nq = s // tq
    qspec = pl.BlockSpec((tq, GROUP_W), lambda b, i: (b * nq + i, 0))
    kvspec = pl.BlockSpec((s, GROUP_W), lambda b, i: (b, 0))
    return pl.pallas_call(
        kernel, grid=(nb, nq),
        in_specs=[qspec, kvspec, kvspec] + extra_specs,
        out_specs=qspec, out_shape=jax.ShapeDtypeStruct((nb * s, GROUP_W), F32),
        scratch_shapes=scratch,
        compiler_params=_cparams("parallel", "arbitrary"), name=name)(q, k, v, *extra)


def _sample_heads(n_heads, q_ref, width):
    out = []
    for h in range(n_heads):
        lo = (h // 2) * PAIR_W
        span = slice(lo, min(lo + PAIR_W, width))
        q2 = q_ref[:, span]
        if span.stop - span.start == PAIR_W:
            q2 = jnp.where(_half_mask(h % 2), q2, jnp.zeros_like(q2))
        out.append((h, span, q2))
    return out


def _store_heads(o_ref, res, n_heads):
    for pair in range(N_PAIRS):
        lo = pair * PAIR_W
        h = 2 * pair
        if h + 1 < n_heads:
            o_ref[:, lo:lo + PAIR_W] = jnp.where(_half_mask(0), res[h], res[h + 1])
        else:
            o_ref[:, lo:lo + HEAD_DIM] = res[h]
            o_ref[:, lo + HEAD_DIM:lo + PAIR_W] = jnp.zeros_like(res[h])


def _joint_softmax_pv(s_c, s_n, vt_c, v_n):
    m = jnp.maximum(jnp.max(s_c, axis=-1, keepdims=True), jnp.max(s_n, axis=-1, keepdims=True))
    p_c = jnp.exp2(s_c - m)
    p_n = jnp.exp2(s_n - m)
    l = jnp.sum(p_c, axis=-1, keepdims=True) + jnp.sum(p_n, axis=-1, keepdims=True)
    return (_dot_nt(p_c.astype(BF16), vt_c) + _dot(p_n.astype(BF16), v_n)) / l


def _sample_attn_kernel(qa_ref, qb_ref, qc_ref, kan_ref, kbn_ref, kcn_ref, van_ref, vbn_ref, vcn_ref,
                        cak_ref, cav_ref, cbk_ref, cbv_ref, cck_ref, ccv_ref,
                        rsum_ref, logf_ref, tabc_ref, tabn_ref,
                        oa_ref, ob_ref, oc_ref):
    t = qa_ref.shape[0]
    past = cak_ref.shape[3]
    r = lax.broadcasted_iota(jnp.int32, (t, t), 0)
    c = lax.broadcasted_iota(jnp.int32, (t, t), 1)

    ltri = jnp.where(c <= r, 1.0, 0.0).astype(BF16)
    hi, mid, lo = _split3(logf_ref[...])
    pcol = (_dot(ltri, hi) + _dot(ltri, mid) + _dot(ltri, lo)) * LOG2E
    prow = jnp.transpose(jnp.concatenate([pcol, jnp.zeros((LANES - t, LANES), F32)], axis=0))
    heads = _sample_heads(H_A, qa_ref, W_A)
    kts = {sp.start: cak_ref[0, 0, sp, :].astype(BF16) for _, sp, _ in heads}
    vts = {sp.start: cav_ref[0, 0, sp, :].astype(BF16) for _, sp, _ in heads}
    s_c = [_dot(qm, kts[sp.start]) + (pcol[:, h:h + 1] + rsum_ref[0, 0, h:h + 1, :] * LOG2E)
           for h, sp, qm in heads]
    s_n = [jnp.where(c <= r, _dot_nt(qm, kan_ref[:, sp]) + (pcol[:, h:h + 1] - prow[h:h + 1, 0:t]), NEG_INF)
           for h, sp, qm in heads]
    _store_heads(oa_ref, [_joint_softmax_pv(s_c[h], s_n[h], vts[sp.start], van_ref[:, sp])
                          for h, sp, _ in heads], H_A)

    heads = _sample_heads(H_B, qb_ref, W_B)
    kts = {sp.start: cbk_ref[0, 0, sp, :].astype(BF16) for _, sp, _ in heads}
    vts = {sp.start: cbv_ref[0, 0, sp, :].astype(BF16) for _, sp, _ in heads}
    s_c = [_dot(qm, kts[sp.start]) + tabc_ref[h] for h, sp, qm in heads]
    s_n = [_dot_nt(qm, kbn_ref[:, sp]) + tabn_ref[h] for h, sp, qm in heads]
    _store_heads(ob_ref, [_joint_softmax_pv(s_c[h], s_n[h], vts[sp.start], vbn_ref[:, sp])
                          for h, sp, _ in heads], H_B)

    heads = _sample_heads(H_C, qc_ref, W_C)
    nblk = past // ATT_TILE
    tri_n = _suffix_tri(t)
    tri_c = _suffix_tri(ATT_TILE)
    seen = c < r
    kts = {sp.start: cck_ref[0, 0, sp, :].astype(BF16) for _, sp, _ in heads}
    vts = {sp.start: ccv_ref[0, 0, sp, :].astype(BF16) for _, sp, _ in heads}
    z_n = [_dot_nt(qm, kcn_ref[:, sp]) for _, sp, qm in heads]
    z_c = [_dot(qm, kts[sp.start]) for _, sp, qm in heads]
    sum_n = [_stick_scores(z, tri_n, seen) for z in z_n]
    sum_c = [[_stick_scores(z[:, b * ATT_TILE:(b + 1) * ATT_TILE], tri_c, None) for b in range(nblk)]
             for z in z_c]
    res = []
    for h, sp, _ in heads:
        acc = _dot(_stick_weights(z_n[h], sum_n[h], 0.0, seen), vcn_ref[:, sp])
        carry = sum_n[h][:, 0:1]
        for b in reversed(range(nblk)):
            cols = slice(b * ATT_TILE, (b + 1) * ATT_TILE)
            a = _stick_weights(z_c[h][:, cols], sum_c[h][b], carry, None)
            acc = acc + _dot_nt(a, vts[sp.start][:, cols])
            carry = carry + sum_c[h][b][:, 0:1]
        res.append(acc)
    _store_heads(oc_ref, res, H_C)


def _sample_attention(proj, caches, layer, rsum, tabc, tabn, nb, t):
    qa, qb, qc, ka, kb, kc, va, vb, vc = proj[:9]
    logf = proj[15]
    new = pl.BlockSpec((t, GROUP_W), lambda b: (b, 0))
    cache = lambda a: pl.BlockSpec((1, 1) + a.shape[2:], lambda b: (layer, b, 0, 0))
    full = lambda a: pl.BlockSpec(a.shape, lambda b: (0,) * a.ndim)
    return pl.pallas_call(
        _sample_attn_kernel, grid=(nb,),
        in_specs=[new] * 9 + [cache(a) for a in caches] + [
            cache(rsum), pl.BlockSpec((t, LANES), lambda b: (b, 0)), full(tabc), full(tabn)],
        out_specs=[new] * 3,
        out_shape=[jax.ShapeDtypeStruct((nb * t, GROUP_W), F32)] * 3,
        compiler_params=_cparams("parallel"), name="sample_attn")(
            qa, qb, qc, ka, kb, kc, va, vb, vc, *caches, rsum, logf, tabc, tabn)


def _masked_max(x, mask):
    return jnp.max(jnp.where(mask, x, -jnp.inf), axis=-1, keepdims=True)


def _first_lane(mask, lane):
    return jnp.min(jnp.where(mask, lane, float(LANES)), axis=-1, keepdims=True)


def _route(logits):
    lane = _lane_iota().astype(F32)
    is_g = (lane >= N_EXPERTS) & (lane < N_EXPERTS + N_GROUPS)
    gmax = _masked_max(logits, is_g)
    p_g = 1.0 / jnp.sum(jnp.where(is_g, jnp.exp(logits - gmax), 0.0), axis=-1, keepdims=True)
    g_sel = _first_lane(is_g & (logits == gmax), lane) - N_EXPERTS
    lo = g_sel * EXPERTS_PER_GROUP
    in_g = (lane >= lo) & (lane < lo + EXPERTS_PER_GROUP)
    l1 = _masked_max(logits, in_g)
    i1 = _first_lane(in_g & (logits == l1), lane)
    rest = in_g & (lane != i1)
    l2 = _masked_max(logits, rest)
    i2 = _first_lane(rest & (logits == l2), lane)
    e2 = jnp.exp(l2 - l1)
    w1 = p_g / (1.0 + e2)
    w2 = p_g * e2 / (1.0 + e2)
    gates = jnp.where(lane == i1, w1, jnp.where(lane == i2, w2, 0.0))
    return jnp.where(lane == g_sel + N_EXPERTS, 1.0, gates)


def _post_kernel(oa_ref, ob_ref, oc_ref, h_ref, gmix_ref, wout_ref, nffn_ref, wrh_ref, wrl_ref, br_ref,
                 h1_ref, xn_ref, gates_ref, *xchunk_refs):
    def gnorm(o, width):
        ms = jnp.sum(o * o, axis=-1, keepdims=True) * (1.0 / width)
        return o * lax.rsqrt(ms + EPS)

    h1 = h_ref[...]
    for g, (o_ref, width) in enumerate(((oa_ref, W_A), (ob_ref, W_B), (oc_ref, W_C))):
        merged = (gnorm(o_ref[...], width) * gmix_ref[g]).astype(BF16)
        h1 = h1 + _dot(merged, wout_ref[g])
    h1_ref[...] = h1
    xf = _rms(h1, nffn_ref[...])
    hi, lo = _split2(xf)
    xn_ref[...] = hi
    for j, ref in enumerate(xchunk_refs):
        ref[...] = xf[:, j * SC_ROW_WORDS:(j + 1) * SC_ROW_WORDS]
    logits = _dot(hi, wrh_ref[...]) + _dot(lo, wrh_ref[...]) + _dot(hi, wrl_ref[...]) + br_ref[...]
    gates_ref[...] = _route(logits)


def _post(oa, ob, oc, h, lw, tm, n_chunks=0):
    t = h.shape[0]
    row = lambda w: pl.BlockSpec((tm, w), lambda i: (i, 0))
    full = lambda a: pl.BlockSpec(a.shape, lambda i: (0,) * a.ndim)
    ws = (lw['g_mix'], lw['w_out'], lw['norm_ffn'], lw['w_r_hi'], lw['w_r_lo'], lw['b_r'])
    return pl.pallas_call(
        _post_kernel, grid=(t // tm,),
        in_specs=[row(GROUP_W)] * 3 + [row(D_MODEL)] + [full(a) for a in ws],
        out_specs=[row(D_MODEL), row(D_MODEL), row(LANES)] + [row(SC_ROW_WORDS)] * n_chunks,
        out_shape=[jax.ShapeDtypeStruct((t, D_MODEL), F32), jax.ShapeDtypeStruct((t, D_MODEL), BF16),
                   jax.ShapeDtypeStruct((t, LANES), F32)]
        + [jax.ShapeDtypeStruct((t, SC_ROW_WORDS), F32)] * n_chunks,
        compiler_params=_cparams("parallel"), name="post")(oa, ob, oc, h, *ws)


def _moe_kernel(xn_ref, gates_ref, wg_ref, wu_ref, wd_ref, h1_ref, p_ref, wple_ref, nple_ref, wpg_ref,
                out_ref, acc_ref):
    step = pl.program_id(1)

    @pl.when(step == 0)
    def _():
        acc_ref[...] = jnp.zeros_like(acc_ref)

    x = xn_ref[...]
    gates = gates_ref[...]
    acts = []
    for k in range(EXPERTS_PER_STEP):
        e = step * EXPERTS_PER_STEP + k
        g = _dot(x, wg_ref[k])
        u = _dot(x, wu_ref[k])
        gate = jnp.sum(jnp.where(_lane_iota() == e, gates, 0.0), axis=-1, keepdims=True)
        acts.append(((g * _sigmoid(g)) * u * gate).astype(BF16))
    acc_ref[...] += _dot(jnp.concatenate(acts, axis=1), wd_ref[...])

    @pl.when(step == pl.num_programs(1) - 1)
    def _():
        h2 = h1_ref[...] + acc_ref[...]
        gate_ple = _sigmoid(_dot(_rms(h2, nple_ref[...]).astype(BF16), wpg_ref[...]))
        out_ref[...] = h2 + _dot(p_ref[0].astype(BF16), wple_ref[...]) * gate_ple


def _moe(xn, gates, h1, p_all, layer, lw, tm):
    t = xn.shape[0]
    row = lambda w: pl.BlockSpec((tm, w), lambda i, e: (i, 0))
    once = pl.Buffered(1)
    full = lambda a: pl.BlockSpec(a.shape, lambda i, e: (0,) * a.ndim, pipeline_mode=once)
    exp = lambda a: pl.BlockSpec((EXPERTS_PER_STEP,) + a.shape[1:], lambda i, e: (e, 0, 0))
    d_hidden = lw['w_ed'].shape[0] // N_EXPERTS
    return pl.pallas_call(
        _moe_kernel, grid=(t // tm, N_EXPERTS // EXPERTS_PER_STEP),
        in_specs=[row(D_MODEL), row(LANES), exp(lw['w_eg']), exp(lw['w_eu']),
                  pl.BlockSpec((EXPERTS_PER_STEP * d_hidden, D_MODEL), lambda i, e: (e, 0)),
                  pl.BlockSpec((tm, D_MODEL), lambda i, e: (i, 0), pipeline_mode=once),
                  pl.BlockSpec((1, tm, p_all.shape[2]), lambda i, e: (layer, i, 0), pipeline_mode=once),
                  full(lw['w_ple']), full(lw['norm_ple']), full(lw['w_pg'])],
        out_specs=row(D_MODEL), out_shape=jax.ShapeDtypeStruct((t, D_MODEL), F32),
        scratch_shapes=[pltpu.VMEM((tm, D_MODEL), F32)],
        compiler_params=pltpu.CompilerParams(dimension_semantics=("parallel", "arbitrary"),
                                             vmem_limit_bytes=MOE_VMEM_LIMIT), name="moe")(
            xn, gates, lw['w_eg'], lw['w_eu'], lw['w_ed'], h1, p_all, lw['w_ple'], lw['norm_ple'], lw['w_pg'])


SORT_TILE = 512
SC_WINDOW = 128
SC_ROW_WORDS = 256
N_CHUNKS = D_MODEL // SC_ROW_WORDS


def _plan_kernel(gates_ref, dest_ref, total_ref, *, tb):
    lane = _lane_iota()
    is_g = (lane >= N_EXPERTS) & (lane < N_EXPERTS + N_GROUPS)
    total = jnp.sum(jnp.where(is_g, gates_ref[...], 0.0), axis=0, keepdims=True)
    padded = jnp.floor((total + (SORT_TILE - 1)) * (1.0 / SORT_TILE)) * SORT_TILE
    base = jnp.zeros((1, LANES), F32)
    for g in range(1, N_GROUPS):
        before = jnp.sum(jnp.where(lane < N_EXPERTS + g, padded, 0.0), axis=-1, keepdims=True)
        base = jnp.where(lane == N_EXPERTS + g, before, base)
    r = lax.broadcasted_iota(jnp.int32, (tb, tb), 0)
    c = lax.broadcasted_iota(jnp.int32, (tb, tb), 1)
    ltri = jnp.where(c < r, 1.0, 0.0).astype(BF16)
    carry = jnp.zeros((1, LANES), F32)
    for blk in range(gates_ref.shape[0] // tb):
        rows = slice(blk * tb, (blk + 1) * tb)
        member = jnp.where(is_g, gates_ref[rows, :], 0.0)
        rank = _dot(ltri, member.astype(BF16)) + carry
        carry = carry + jnp.sum(member, axis=0, keepdims=True)
        dest = jnp.sum(member * (base + rank), axis=-1, keepdims=True)
        dest_ref[rows, :] = jnp.broadcast_to(dest, (tb, LANES))
    total_ref[...] = jnp.broadcast_to(total, total_ref.shape)


def _plan(gates):
    t = gates.shape[0]
    return pl.pallas_call(
        functools.partial(_plan_kernel, tb=SORT_TILE),
        out_shape=[jax.ShapeDtypeStruct((t, LANES), F32), jax.ShapeDtypeStruct((SUBLANES, LANES), F32)],
        compiler_params=pltpu.CompilerParams(vmem_limit_bytes=VMEM_LIMIT), name="moe_plan")(gates)


def _sc_scatter_rows(x, idx, n_out):
    n, d = x.shape
    mesh = plsc.VectorSubcoreMesh(core_axis_name="core", subcore_axis_name="subcore")

    @functools.partial(pl.kernel, out_type=jax.ShapeDtypeStruct((n_out, d), x.dtype), mesh=mesh, scratch_types=[])
    def scatter(x_hbm, i_hbm, o_hbm):
        def body(x_vmem, i_vmem):
            pltpu.sync_copy(x_vmem, o_hbm.at[i_vmem.at[0]])

        pltpu.emit_pipeline(
            body, grid=(n // SC_WINDOW,),
            in_specs=[pl.BlockSpec((SC_WINDOW, d), index_map=lambda i: (i, 0)),
                      pl.BlockSpec((1, SC_WINDOW), index_map=lambda i: (0, i))],
            out_specs=[], core_axis_name=('core', 'subcore'), dimension_semantics=(pltpu.PARALLEL,))(x_hbm, i_hbm)

    return scatter(x, idx.reshape(1, n))


def _sc_gather_rows(x, idx):
    n = idx.shape[0]
    d = x.shape[1]
    mesh = plsc.VectorSubcoreMesh(core_axis_name="core", subcore_axis_name="subcore")

    @functools.partial(pl.kernel, out_type=jax.ShapeDtypeStruct((n, d), x.dtype), mesh=mesh)
    def gather(x_hbm, i_hbm, o_hbm):
        def body(i_vmem, o_vmem):
            pltpu.sync_copy(x_hbm.at[i_vmem.at[0]], o_vmem)

        pltpu.emit_pipeline(
            body, grid=(n // SC_WINDOW,),
            in_specs=[pl.BlockSpec((1, SC_WINDOW), index_map=lambda i: (0, i))],
            out_specs=[pl.BlockSpec((SC_WINDOW, d), index_map=lambda i: (i, 0))],
            core_axis_name=('core', 'subcore'), dimension_semantics=(pltpu.PARALLEL,))(i_hbm, o_hbm)

    return gather(x, idx.reshape(1, n))


def _experts_sorted_kernel(tg_ref, tv_ref, *refs):
    xs_refs, (gs_ref, wg_ref, wu_ref, wd_ref), ys_refs = refs[:N_CHUNKS], refs[N_CHUNKS:N_CHUNKS + 4], refs[-N_CHUNKS:]
    j = pl.program_id(0)

    @pl.when(tv_ref[j] > 0)
    def _():
        g = tg_ref[j]
        x = jnp.concatenate([r[...] for r in xs_refs], axis=1).astype(BF16)
        src = lax.broadcasted_iota(jnp.int32, (LANES, D_MODEL), 0)
        dst = lax.broadcasted_iota(jnp.int32, (LANES, D_MODEL), 1)
        d_shift = (D_MODEL // EXPERTS_PER_GROUP).bit_length() - 1
        expand = jnp.where(src == g * EXPERTS_PER_GROUP + jnp.right_shift(dst, d_shift), 1.0, 0.0).astype(BF16)
        hi, mid, lo = _split3(gs_ref[...])
        ge = _dot(hi, expand) + _dot(mid, expand) + _dot(lo, expand)
        hg = _dot(x, wg_ref[0])
        act = (hg * _sigmoid(hg)) * _dot(x, wu_ref[0]) * ge
        y = _dot(act.astype(BF16), wd_ref[0])
        for c, ref in enumerate(ys_refs):
            ref[...] = y[:, c * SC_ROW_WORDS:(c + 1) * SC_ROW_WORDS]


def _experts_sorted(xs_chunks, gs, tile_group, tile_valid, lw):
    n_rows = gs.shape[0]
    row = lambda w: pl.BlockSpec((SORT_TILE, w), lambda j, tg, tv: (j, 0))
    grp = lambda a: pl.BlockSpec((1,) + a.shape[1:], lambda j, tg, tv: (tg[j], 0, 0))
    return pl.pallas_call(
        _experts_sorted_kernel,
        grid_spec=pltpu.PrefetchScalarGridSpec(
            num_scalar_prefetch=2, grid=(n_rows // SORT_TILE,),
            in_specs=[row(SC_ROW_WORDS)] * N_CHUNKS + [row(LANES), grp(lw['w_gg']), grp(lw['w_gu']), grp(lw['w_gd'])],
            out_specs=[row(SC_ROW_WORDS)] * N_CHUNKS),
        out_shape=[jax.ShapeDtypeStruct((n_rows, SC_ROW_WORDS), F32)] * N_CHUNKS,
        compiler_params=_cparams("arbitrary"), name="moe_sorted")(
            tile_group, tile_valid, *xs_chunks, gs, lw['w_gg'], lw['w_gu'], lw['w_gd'])


def _ple_kernel(h1_ref, *refs):
    y_refs, (p_ref, wple_ref, nple_ref, wpg_ref, out_ref) = refs[:N_CHUNKS], refs[N_CHUNKS:]
    h2 = h1_ref[...] + jnp.concatenate([r[...] for r in y_refs], axis=1)
    gate_ple = _sigmoid(_dot(_rms(h2, nple_ref[...]).astype(BF16), wpg_ref[...]))
    out_ref[...] = h2 + _dot(p_ref[0].astype(BF16), wple_ref[...]) * gate_ple


def _ple(h1, y_chunks, p_all, layer, lw, tm):
    t = h1.shape[0]
    row = lambda w: pl.BlockSpec((tm, w), lambda i: (i, 0))
    full = lambda a: pl.BlockSpec(a.shape, lambda i: (0,) * a.ndim)
    return pl.pallas_call(
        _ple_kernel, grid=(t // tm,),
        in_specs=[row(D_MODEL)] + [row(SC_ROW_WORDS)] * N_CHUNKS
        + [pl.BlockSpec((1, tm, p_all.shape[2]), lambda i: (layer, i, 0)),
           full(lw['w_ple']), full(lw['norm_ple']), full(lw['w_pg'])],
        out_specs=row(D_MODEL), out_shape=jax.ShapeDtypeStruct((t, D_MODEL), F32),
        compiler_params=_cparams("parallel"), name="ple")(
            h1, *y_chunks, p_all, lw['w_ple'], lw['norm_ple'], lw['w_pg'])


def _moe_dispatch(x_chunks, gates):
    t = gates.shape[0]
    n_rows = t + N_GROUPS * SORT_TILE
    dest_rep, total = _plan(gates)
    dest = dest_rep[:, 0].astype(jnp.int32)
    padded = (total[0, N_EXPERTS:N_EXPERTS + N_GROUPS].astype(jnp.int32) + (SORT_TILE - 1)) // SORT_TILE * SORT_TILE
    ends = jnp.cumsum(padded)
    tile_start = jnp.arange(n_rows // SORT_TILE, dtype=jnp.int32) * SORT_TILE
    tile_group = jnp.minimum(jnp.sum(tile_start[:, None] >= ends[None, :], axis=1), N_GROUPS - 1).astype(jnp.int32)
    tile_valid = (tile_start < ends[-1]).astype(jnp.int32)
    xs_chunks = [_sc_scatter_rows(xc, dest, n_rows) for xc in x_chunks]
    gs = _sc_scatter_rows(gates, dest, n_rows)
    return xs_chunks, gs, tile_group, tile_valid, dest


def _moe_combine(dispatched, h1, p_all, layer, lw):
    xs_chunks, gs, tile_group, tile_valid, dest = dispatched
    ys_chunks = _experts_sorted(xs_chunks, gs, tile_group, tile_valid, lw)
    y_chunks = [_sc_gather_rows(yc, dest) for yc in ys_chunks]
    return _ple(h1, y_chunks, p_all, layer, lw, _token_tile(h1.shape[0], 512))


def _regroup_out_rows(wt):
    z = jnp.zeros((HEAD_DIM, wt.shape[1]), wt.dtype)
    return jnp.concatenate([wt[:W_A], wt[W_A:W_A + W_B], z, wt[W_A + W_B:], z], axis=0)


def _regroup_rows(w):
    z = jnp.zeros((HEAD_DIM, w.shape[1]), w.dtype)
    return jnp.stack([w[:W_A], jnp.concatenate([w[W_A:W_A + W_B], z], axis=0),
                      jnp.concatenate([w[W_A + W_B:], z], axis=0)])


def _pad_lanes(a, n=LANES):
    return jnp.pad(a, [(0, 0)] * (a.ndim - 1) + [(0, n - a.shape[-1])])


def _toeplitz_bias(rel_bias, off, tq, win):
    length = tq + win
    pad = length + abs(off)
    ext = jnp.flip(jnp.pad(rel_bias, ((0, 0), (pad, pad)), mode='edge'), axis=1)
    s1 = ext.shape[1] - 1 - (off + REL_CLIP + pad)
    v = jnp.concatenate([ext[:, s1:s1 + win], ext[:, s1 - tq:s1]], axis=1)
    flat = jnp.tile(v, (1, tq))[:, :tq * (length - 1)]
    return flat.reshape(-1, tq, length - 1)[:, :, :win]


def _block_toeplitz_bias(rel_bias, off, tq, win):
    nq, nk = tq // LANES, win // LANES
    blocks = {d: _toeplitz_bias(rel_bias, off - d * LANES, LANES, LANES) for d in range(-(nq - 1), nk)}
    return jnp.concatenate(
        [jnp.concatenate([blocks[b - a] for b in range(nk)], axis=2) for a in range(nq)], axis=1)


def _rel_tables(rel_bias, t_new, rows_b):
    tq, win = ATT_TILE, ATT_TILE + BAND
    q = jnp.arange(tq)[:, None]
    tabs = []
    for var in range(BAND // tq + 1):
        rel_k = jnp.arange(win)[None, :] - var * tq
        kch = jnp.floor_divide(rel_k, CHUNK)
        qch = q // CHUNK
        valid = (kch <= qch) & (kch >= qch - BAND_CHUNKS)
        tabs.append(jnp.where(valid[None], _block_toeplitz_bias(rel_bias, var * tq, tq, win), NEG_INF))
    tab_prompt = jnp.stack(tabs).astype(F32)
    tab_c = _toeplitz_bias(rel_bias, rows_b, t_new, rows_b).astype(F32)
    tab_n = _toeplitz_bias(rel_bias, 0, t_new, t_new).astype(F32)
    return tab_prompt, tab_c, tab_n


def _layer_weights(i, norm_mix, w_in, b_f, g_qa, g_ka, g_qb, g_kb, g_mix_out, w_out, norm_ffn,
                   w_rg, b_rg, w_re, b_re, w_eg, w_eu, w_ed, w_ple, norm_ple, w_pg):
    d_mix = W_A + W_B + W_C
    wt = jnp.transpose(w_in, (2, 0, 1))[:, i, :]
    w_f = wt[3 * d_mix:]
    w_big = jnp.concatenate(
        [_regroup_out_rows(wt[j * d_mix:(j + 1) * d_mix]) for j in range(3)]
        + [jnp.pad(w_f, ((0, LANES - w_f.shape[0]), (0, 0)))], axis=0).astype(BF16)
    tile6 = lambda g: jnp.tile(g, GROUP_W // HEAD_DIM)
    w_r = _pad_lanes(jnp.concatenate([w_re[i], w_rg[i]], axis=1))
    w_r_hi = w_r.astype(BF16)
    gm = g_mix_out[i]
    zpad = jnp.zeros((HEAD_DIM,), F32)
    g_mix = jnp.stack([gm[:W_A], jnp.concatenate([gm[W_A:W_A + W_B], zpad]),
                       jnp.concatenate([gm[W_A + W_B:], zpad])])[:, None, :]
    return dict(
        norm_mix=norm_mix[i][None], w_in=w_big,
        gq=jnp.concatenate([tile6(g_qa[i]), tile6(g_qb[i])])[None],
        gk=jnp.concatenate([tile6(g_ka[i]), tile6(g_kb[i])])[None],
        b_f=_pad_lanes(b_f[i])[None],
        g_mix=g_mix, w_out=_regroup_rows(w_out[i]).astype(BF16), norm_ffn=norm_ffn[i][None],
        w_r_hi=w_r_hi, w_r_lo=(w_r - w_r_hi.astype(F32)).astype(BF16),
        b_r=_pad_lanes(jnp.concatenate([b_re[i], b_rg[i]]))[None],
        w_eg=w_eg[i].astype(BF16), w_eu=w_eu[i].astype(BF16),
        w_ed=w_ed[i].reshape(-1, w_ed.shape[-1]).astype(BF16),
        w_gg=_group_cols(w_eg[i]), w_gu=_group_cols(w_eu[i]),
        w_gd=w_ed[i].reshape(N_GROUPS, -1, w_ed.shape[-1]).astype(BF16),
        w_ple=w_ple[i].astype(BF16), norm_ple=norm_ple[i][None], w_pg=w_pg[i].astype(BF16))


def _group_cols(w):
    e, d, f = w.shape
    w = w.reshape(N_GROUPS, EXPERTS_PER_GROUP, d, f)
    return jnp.transpose(w, (0, 2, 1, 3)).reshape(N_GROUPS, d, EXPERTS_PER_GROUP * f).astype(BF16)


def _token_tile(t, pref):
    return pref if t % pref == 0 else t


def _feature_major(cache):
    d, n, p, h, e = cache.shape
    return jnp.transpose(cache, (0, 1, 3, 4, 2)).reshape(d, n, h * e, p)


def _position_major(rows, heads):
    d, n, _, p = rows.shape
    return jnp.transpose(rows.reshape(d, n, heads, HEAD_DIM, p), (0, 1, 4, 2, 3))


def kernel(x_prompt, x_sample, p_prompt, p_sample, cache_a_k, cache_a_v, cache_a_logf, cache_b_k, cache_b_v, cache_c_k, cache_c_v, norm_mix, w_in, b_f, g_qa, g_ka, g_qb, g_kb, rel_bias, g_mix_out, w_out, norm_ffn, w_router_group, b_router_group, w_router_expert, b_router_expert, w_exp_gate, w_exp_up, w_exp_down, w_ple, norm_ple, w_ple_gate):
    nb, s, d = x_prompt.shape
    ns, t_new, _ = x_sample.shape
    depth = w_in.shape[0]
    past = cache_a_k.shape[2]
    rows_b = cache_b_k.shape[2]
    assert d == D_MODEL and s % PROJ_TILE == 0 and s >= BAND + ATT_TILE and past % ATT_TILE == 0
    tp, ts = nb * s, ns * t_new

    hp = x_prompt.reshape(tp, d)
    hs = x_sample.reshape(ts, d)
    pp = p_prompt.reshape(depth, tp, -1)
    ps = p_sample.reshape(depth, ts, -1)
    caches = tuple(_feature_major(c) for c in (cache_a_k, cache_a_v, cache_b_k, cache_b_v, cache_c_k, cache_c_v))
    lf_rows = jnp.transpose(cache_a_logf, (0, 3, 1, 2))
    rsum = _suffix_sum(lf_rows.reshape(depth * H_A * ns, past)).reshape(depth, H_A, ns, past)
    rsum = jnp.transpose(rsum, (0, 2, 1, 3))

    rows_p = None
    new_s = [[] for _ in range(7)]
    for i in range(depth):
        lw = _layer_weights(i, norm_mix, w_in, b_f, g_qa, g_ka, g_qb, g_kb, g_mix_out, w_out, norm_ffn,
                            w_router_group, b_router_group, w_router_expert, b_router_expert,
                            w_exp_gate, w_exp_up, w_exp_down, w_ple, norm_ple, w_ple_gate)
        tab_p, tab_c, tab_n = _rel_tables(rel_bias[i] * LOG2E, t_new, rows_b)

        pr = _inproj_prompt(hp, lw, i, depth, nb, s, rows_p)
        qat, ka, vat, qb, kb, vb, qc, kc, vc, logf = pr[:10]
        rows_p = pr[10:]
        kaug, qaug = _forget_aug(logf, nb, s)
        oa = _attn_a(qat, ka, vat, kaug, qaug, nb, s)
        ob = _prompt_attention(
            _attn_b_kernel, "attn_b", qb, kb, vb, (tab_p,),
            [pl.BlockSpec(tab_p.shape, lambda b, j: (0, 0, 0, 0))], [], nb, s, ATT_TILE)
        oc = _prompt_attention(
            _attn_c_kernel, "attn_c", qc, kc, vc, (), [],
            [pltpu.VMEM((H_C, C_QTILE, PAIR_W), BF16), pltpu.VMEM((H_C, C_QTILE, LANES), F32),
             pltpu.VMEM((H_C, C_QTILE, PAIR_W), F32)], nb, s, C_QTILE)
        h1, _, gates, *x_chunks = _post(oa, ob, oc, hp, lw, _token_tile(tp, 512), N_CHUNKS)
        dispatched = _moe_dispatch(x_chunks, gates)

        sr = _inproj_sample(hs, lw)
        oa, ob, oc = _sample_attention(sr, caches, i, rsum, tab_c, tab_n, ns, t_new)
        h1s, xn, gates = _post(oa, ob, oc, hs, lw, _token_tile(ts, 512))
        hs = _moe(xn, gates, h1s, ps, i, lw, _token_tile(ts, 1024))

        hp = _moe_combine(dispatched, h1, pp, i, lw)
        akf, avf, bkf, bvf, ckf, cvf, logf = sr[9:]
        rows = (akf.reshape(ns, t_new, H_A, HEAD_DIM), avf.reshape(ns, t_new, H_A, HEAD_DIM),
                logf[:, :H_A].reshape(ns, t_new, H_A),
                bkf.reshape(ns, t_new, H_B, HEAD_DIM), bvf.reshape(ns, t_new, H_B, HEAD_DIM),
                ckf.reshape(ns, t_new, H_C, HEAD_DIM), cvf.reshape(ns, t_new, H_C, HEAD_DIM))
        for j in range(7):
            new_s[j].append(rows[j])

    akt, avt, bkt, bvt, ckt, cvt, lft = rows_p
    outs_p = (_position_major(akt, H_A), _position_major(avt, H_A),
              jnp.transpose(lft.reshape(depth, H_A, nb, s), (0, 2, 3, 1)),
              _position_major(bkt, H_B), _position_major(bvt, H_B),
              _position_major(ckt, H_C), _position_major(cvt, H_C))
    outs_s = [jnp.stack(r, axis=0) for r in new_s]
    return (hp.reshape(nb, s, d), hs.reshape(ns, t_new, d), *outs_p, *outs_s)
```

```python
import functools
import math

import jax
import jax.numpy as jnp
from jax import lax
from jax.experimental import pallas as pl
from jax.experimental.pallas import tpu as pltpu
from jax.experimental.pallas import tpu_sc as plsc

F32 = jnp.float32
BF16 = jnp.bfloat16

D_MODEL = 1024
HEAD_DIM = 64
H_A, H_B, H_C = 6, 5, 5
W_A, W_B, W_C = H_A * HEAD_DIM, H_B * HEAD_DIM, H_C * HEAD_DIM
GROUP_W = 384
PAIR_W = 2 * HEAD_DIM
N_PAIRS = GROUP_W // PAIR_W
CHUNK = 64
BAND_CHUNKS = 8
BAND = BAND_CHUNKS * CHUNK
REL_CLIP = 128
N_GROUPS = 4
EXPERTS_PER_GROUP = 4
N_EXPERTS = N_GROUPS * EXPERTS_PER_GROUP
EPS = 1e-6
ATTN_SCALE = HEAD_DIM ** -0.5
LOG2E = math.log2(math.e)
Q_SCALE = ATTN_SCALE * LOG2E
NEG_INF = -1e30
LANES = 128
SUBLANES = 8
VMEM_LIMIT = 48 * 1024 * 1024
EXPERTS_PER_STEP = 4
MOE_VMEM_LIMIT = 58 * 1024 * 1024

ATT_TILE = 256
A_QTILE = 512
C_QTILE = 256
PROJ_TILE = 1024
PROJ_VMEM_LIMIT = 56 * 1024 * 1024
AUG_SLOTS = 8

_NT = (((1,), (1,)), ((), ()))


def _cparams(*sem):
    return pltpu.CompilerParams(dimension_semantics=sem, vmem_limit_bytes=VMEM_LIMIT)


def _dot(a, b):
    return jnp.dot(a, b, preferred_element_type=F32)


def _dot_nt(a, b):
    return lax.dot_general(a, b, _NT, preferred_element_type=F32)


def _split3(x):
    hi = x.astype(BF16)
    r1 = x - hi.astype(F32)
    mid = r1.astype(BF16)
    lo = (r1 - mid.astype(F32)).astype(BF16)
    return hi, mid, lo


def _split2(x):
    hi = x.astype(BF16)
    lo = (x - hi.astype(F32)).astype(BF16)
    return hi, lo


def _lane_iota(n=LANES):
    return lax.broadcasted_iota(jnp.int32, (1, n), 1)


def _half_mask(half):
    lane = _lane_iota()
    return (lane < HEAD_DIM) if half == 0 else (lane >= HEAD_DIM)


def _pair_slice(h):
    return slice((h // 2) * PAIR_W, (h // 2 + 1) * PAIR_W)


def _softplus2(z2):
    return jnp.maximum(z2, 0.0) + jnp.log2(1.0 + jnp.exp2(-jnp.abs(z2)))


def _sigmoid(z):
    return 1.0 / (1.0 + jnp.exp(-z))


def _rms(x, g):
    return x * lax.rsqrt(jnp.mean(x * x, axis=-1, keepdims=True) + EPS) * g


def _pair_rms(z, gain):
    first = _half_mask(0)
    outs = []
    for j in range(N_PAIRS):
        blk = z[:, j * PAIR_W:(j + 1) * PAIR_W]
        sq = blk * blk
        lo = jnp.sum(jnp.where(first, sq, 0.0), axis=-1, keepdims=True)
        hi = jnp.sum(jnp.where(first, 0.0, sq), axis=-1, keepdims=True)
        ms = jnp.where(first, lo, hi) * (1.0 / HEAD_DIM)
        outs.append(blk * lax.rsqrt(ms + EPS) * gain[:, j * PAIR_W:(j + 1) * PAIR_W])
    return jnp.concatenate(outs, axis=-1)


def _projector(x_ref, gn_ref, w_ref):
    xb = _rms(x_ref[...], gn_ref[...]).astype(BF16)
    return lambda g, width=GROUP_W: _dot_nt(xb, w_ref[g * GROUP_W:g * GROUP_W + width, :])


def _log_forget(zf, bf_ref):
    zf = zf + bf_ref[...]
    return jnp.minimum(zf, 0.0) - jnp.log(1.0 + jnp.exp(-jnp.abs(zf)))


def _inproj_sample_kernel(x_ref, gn_ref, w_ref, gq_ref, gk_ref, bf_ref,
                          qa_ref, qb_ref, qc_ref, ka_ref, kb_ref, kc_ref, va_ref, vb_ref, vc_ref,
                          akf_ref, avf_ref, bkf_ref, bvf_ref, ckf_ref, cvf_ref, logf_ref):
    proj = _projector(x_ref, gn_ref, w_ref)
    qa_ref[...] = (_pair_rms(proj(0), gq_ref[:, :GROUP_W]) * Q_SCALE).astype(BF16)
    qb_ref[...] = (_pair_rms(proj(1), gq_ref[:, GROUP_W:]) * Q_SCALE).astype(BF16)
    qc_ref[...] = (proj(2) * Q_SCALE).astype(BF16)
    for g, norm, bf_ref_, f_ref, width in (
            (3, gk_ref[:, :GROUP_W], ka_ref, akf_ref, W_A), (4, gk_ref[:, GROUP_W:], kb_ref, bkf_ref, W_B),
            (5, None, kc_ref, ckf_ref, W_C), (6, None, va_ref, avf_ref, W_A),
            (7, None, vb_ref, bvf_ref, W_B), (8, None, vc_ref, cvf_ref, W_C)):
        z = proj(g)
        if norm is not None:
            z = _pair_rms(z, norm)
        f_ref[...] = z[:, :width]
        bf_ref_[...] = z.astype(BF16)
    logf_ref[...] = _log_forget(proj(9, LANES), bf_ref)


def _inproj_sample(x, lw):
    t = x.shape[0]
    row = lambda w: pl.BlockSpec((t, w), lambda i: (0, 0))
    full = lambda a: pl.BlockSpec(a.shape, lambda i: (0,) * a.ndim)
    widths = [GROUP_W] * 9 + [W_A, W_A, W_B, W_B, W_C, W_C, LANES]
    out_shape = [jax.ShapeDtypeStruct((t, w), BF16 if j < 9 else F32) for j, w in enumerate(widths)]
    ins = (x, lw['norm_mix'], lw['w_in'], lw['gq'], lw['gk'], lw['b_f'])
    return pl.pallas_call(
        _inproj_sample_kernel, grid=(1,),
        in_specs=[row(D_MODEL)] + [full(a) for a in ins[1:]],
        out_specs=[row(w) for w in widths], out_shape=out_shape,
        compiler_params=_cparams("arbitrary"), name="inproj_sample")(*ins)


N_PROMPT_ROWS = 7


def _inproj_prompt_kernel(*refs, n_alias):
    x_ref, gn_ref, w_ref, gq_ref, gk_ref, bf_ref = refs[:6]
    (qat_ref, ka_ref, vat_ref, qb_ref, kb_ref, vb_ref, qc_ref, kc_ref, vc_ref, logf_ref,
     akt_ref, avt_ref, bkt_ref, bvt_ref, ckt_ref, cvt_ref, lft_ref) = refs[6 + n_alias:]
    proj = _projector(x_ref, gn_ref, w_ref)
    qat_ref[...] = jnp.transpose(_pair_rms(proj(0), gq_ref[:, :GROUP_W]) * Q_SCALE).astype(BF16)
    qb_ref[...] = (_pair_rms(proj(1), gq_ref[:, GROUP_W:]) * Q_SCALE).astype(BF16)
    qc_ref[...] = (proj(2) * Q_SCALE).astype(BF16)
    ka = _pair_rms(proj(3), gk_ref[:, :GROUP_W])
    ka_ref[...] = ka.astype(BF16)
    akt_ref[0, 0] = jnp.transpose(ka)
    kb = _pair_rms(proj(4), gk_ref[:, GROUP_W:])
    kb_ref[...] = kb.astype(BF16)
    kc = proj(5)
    kc_ref[...] = kc.astype(BF16)
    ckt_ref[0, 0] = jnp.transpose(kc)[:W_C]
    vat = jnp.transpose(proj(6))
    avt_ref[0, 0] = vat
    for j in range(vat_ref.shape[1]):
        vat_ref[0, j] = vat[:, j * ATT_TILE:(j + 1) * ATT_TILE].astype(BF16)
    vb = proj(7)
    vb_ref[...] = vb.astype(BF16)
    vc = proj(8)
    vc_ref[...] = vc.astype(BF16)
    cvt_ref[0, 0] = jnp.transpose(vc)[:W_C]

    @pl.when(pl.program_id(1) == pl.num_programs(1) - 1)
    def _():
        keep = kb.shape[0] - BAND
        bkt_ref[0, 0] = jnp.transpose(kb)[:W_B, keep:]
        bvt_ref[0, 0] = jnp.transpose(vb)[:W_B, keep:]

    lf = _log_forget(proj(9, LANES), bf_ref)
    logf_ref[...] = lf
    lft_ref[0] = jnp.transpose(lf)[:H_A]


def _inproj_prompt(x, lw, layer, depth, nb, s, prev_rows):
    tm = PROJ_TILE
    nt = s // tm
    t = nb * s
    per = tm // ATT_TILE
    row = lambda w: pl.BlockSpec((tm, w), lambda b, i: (b * nt + i, 0))
    full = lambda a: pl.BlockSpec(a.shape, lambda b, i: (0,) * a.ndim, pipeline_mode=pl.Buffered(1))
    feat = lambda w: pl.BlockSpec((1, 1, w, tm), lambda b, i: (layer, b, 0, i))
    band = lambda w: pl.BlockSpec((1, 1, w, BAND), lambda b, i: (layer, b, 0, 0))
    tok = jax.ShapeDtypeStruct((t, GROUP_W), BF16)
    out_shape = [
        jax.ShapeDtypeStruct((GROUP_W, t), BF16), tok,
        jax.ShapeDtypeStruct((nb, s // ATT_TILE, GROUP_W, ATT_TILE), BF16),
        tok, tok, tok, tok, tok, tok, jax.ShapeDtypeStruct((t, LANES), F32),
        jax.ShapeDtypeStruct((depth, nb, W_A, s), F32), jax.ShapeDtypeStruct((depth, nb, W_A, s), F32),
        jax.ShapeDtypeStruct((depth, nb, W_B, BAND), F32), jax.ShapeDtypeStruct((depth, nb, W_B, BAND), F32),
        jax.ShapeDtypeStruct((depth, nb, W_C, s), F32), jax.ShapeDtypeStruct((depth, nb, W_C, s), F32),
        jax.ShapeDtypeStruct((depth, H_A, t), F32)]
    out_specs = [
        pl.BlockSpec((GROUP_W, tm), lambda b, i: (0, b * nt + i)), row(GROUP_W),
        pl.BlockSpec((1, per, GROUP_W, ATT_TILE), lambda b, i: (b, i, 0, 0)),
        row(GROUP_W), row(GROUP_W), row(GROUP_W), row(GROUP_W), row(GROUP_W), row(GROUP_W), row(LANES),
        feat(W_A), feat(W_A), band(W_B), band(W_B), feat(W_C), feat(W_C),
        pl.BlockSpec((1, H_A, tm), lambda b, i: (layer, 0, b * nt + i))]
    ins = [x, lw['norm_mix'], lw['w_in'], lw['gq'], lw['gk'], lw['b_f']]
    in_specs = [row(D_MODEL)] + [full(a) for a in ins[1:]]
    aliases = {}
    n_alias = 0
    if prev_rows is not None:
        n_alias = N_PROMPT_ROWS
        first_row_out = len(out_shape) - N_PROMPT_ROWS
        for j, a in enumerate(prev_rows):
            aliases[len(ins)] = first_row_out + j
            ins.append(a)
            in_specs.append(pl.BlockSpec(memory_space=pl.ANY))
    return pl.pallas_call(
        functools.partial(_inproj_prompt_kernel, n_alias=n_alias), grid=(nb, nt),
        in_specs=in_specs, out_specs=out_specs, out_shape=out_shape,
        input_output_aliases=aliases,
        compiler_params=pltpu.CompilerParams(dimension_semantics=("parallel", "arbitrary"),
                                             vmem_limit_bytes=PROJ_VMEM_LIMIT),
        name="inproj_prompt")(*ins)


def _forget_aug_kernel(logf_ref, kaug_ref, qaug_ref, *, nblk):
    tb = ATT_TILE
    r = lax.broadcasted_iota(jnp.int32, (tb, tb), 0)
    c = lax.broadcasted_iota(jnp.int32, (tb, tb), 1)
    tri = jnp.where(c <= r, 1.0, 0.0).astype(BF16)
    src = lax.broadcasted_iota(jnp.int32, (LANES, LANES), 0)
    dst = lax.broadcasted_iota(jnp.int32, (LANES, LANES), 1)

    def place(slot):
        return jnp.where((dst == AUG_SLOTS * src + slot) & (src < H_A), 1.0, 0.0).astype(BF16)

    lane = _lane_iota()
    slot = jnp.bitwise_and(lane, AUG_SLOTS - 1)
    used = lane < AUG_SLOTS * H_A
    key_const = jnp.where(used & (slot >= 3) & (slot < 6), 1.0, 0.0)
    qry_const = jnp.where(used & (slot < 3), -1.0, 0.0)
    carry = jnp.zeros((1, LANES), F32)
    for blk in range(nblk):
        rows = slice(blk * tb, (blk + 1) * tb)
        hi, mid, lo = _split3(logf_ref[rows, :])
        cs = (_dot(tri, hi) + _dot(tri, mid) + _dot(tri, lo)) + carry
        carry = cs[tb - 1:tb, :]
        hi, mid, lo = _split3(cs * LOG2E)
        kaug = _dot(hi, place(0)) + _dot(mid, place(1)) + _dot(lo, place(2)) + key_const
        kaug_ref[rows, :] = kaug.astype(BF16)
        qaug = _dot(hi, place(3)) + _dot(mid, place(4)) + _dot(lo, place(5)) + qry_const
        qaug_ref[:, rows] = jnp.transpose(qaug).astype(BF16)


def _forget_aug(logf, nb, s):
    return pl.pallas_call(
        functools.partial(_forget_aug_kernel, nblk=s // ATT_TILE), grid=(nb,),
        in_specs=[pl.BlockSpec((s, LANES), lambda b: (b, 0))],
        out_specs=[pl.BlockSpec((s, LANES), lambda b: (b, 0)), pl.BlockSpec((LANES, s), lambda b: (0, b))],
        out_shape=[jax.ShapeDtypeStruct((nb * s, LANES), BF16), jax.ShapeDtypeStruct((LANES, nb * s), BF16)],
        compiler_params=_cparams("parallel"), name="forget_aug")(logf)


def _suffix_sum_kernel(x_ref, o_ref):
    p = x_ref.shape[1]
    j = lax.broadcasted_iota(jnp.int32, (p, p), 0)
    s = lax.broadcasted_iota(jnp.int32, (p, p), 1)
    tri = jnp.where(j > s, 1.0, 0.0).astype(BF16)
    hi, mid, lo = _split3(x_ref[...])
    o_ref[...] = _dot(hi, tri) + _dot(mid, tri) + _dot(lo, tri)


def _suffix_sum(x):
    return pl.pallas_call(
        _suffix_sum_kernel, out_shape=jax.ShapeDtypeStruct(x.shape, F32),
        compiler_params=pltpu.CompilerParams(vmem_limit_bytes=VMEM_LIMIT), name="suffix_sum")(x)


def _attn_a_kernel(qt_ref, k_ref, vt_ref, kaug_ref, qaug_ref, o_ref, qp_ref, m_ref, l_ref, acc_ref):
    tq = qt_ref.shape[1]
    tk = ATT_TILE
    qi = pl.program_id(1)
    row = lax.broadcasted_iota(jnp.int32, (PAIR_W, 1), 0)
    aug = qaug_ref[...]
    for h in range(H_A):
        q2 = qt_ref[_pair_slice(h), :]
        half = (row < HEAD_DIM) if h % 2 == 0 else (row >= HEAD_DIM)
        qp_ref[h, :PAIR_W, :] = jnp.where(half, q2, jnp.zeros_like(q2))
        mine = (row >= AUG_SLOTS * h) & (row < AUG_SLOTS * (h + 1))
        qp_ref[h, PAIR_W:, :] = jnp.where(mine, aug, jnp.zeros_like(aug))
    m_ref[...] = jnp.full(m_ref.shape, NEG_INF, F32)
    l_ref[...] = jnp.zeros(l_ref.shape, F32)
    acc_ref[...] = jnp.zeros(acc_ref.shape, F32)

    def step(kt, key_offset):
        ks = pl.multiple_of(kt * tk, tk)
        ka = kaug_ref[pl.ds(ks, tk), :]
        tile_scores = [_dot(jnp.concatenate([k_ref[pl.ds(ks, tk), _pair_slice(h)], ka], axis=1), qp_ref[h])
                       for h in range(H_A)]
        probs, alphas = [], []
        for h in range(H_A):
            s = tile_scores[h]
            if key_offset is not None:
                key = lax.broadcasted_iota(jnp.int32, (tk, tq), 0)
                qry = lax.broadcasted_iota(jnp.int32, (tk, tq), 1)
                s = jnp.where(key + key_offset <= qry, s, NEG_INF)
            m_prev = m_ref[h]
            m_new = jnp.maximum(m_prev, jnp.max(s, axis=0, keepdims=True))
            alpha = jnp.exp2(m_prev - m_new)
            p = jnp.exp2(s - m_new)
            l_ref[h] = alpha * l_ref[h] + jnp.sum(p, axis=0, keepdims=True)
            m_ref[h] = m_new
            alphas.append(alpha)
            probs.append(p.astype(BF16))
        for h in range(H_A):
            acc_ref[h] = alphas[h] * acc_ref[h] + _dot(vt_ref[0, kt, _pair_slice(h), :], probs[h])

    def body(kt, carry):
        step(kt, None)
        return carry

    per = tq // tk
    lax.fori_loop(0, qi * per, body, 0)
    for j in range(per):
        step(qi * per + j, j * tk)
    for pair in range(N_PAIRS):
        h = 2 * pair
        out_t = jnp.where(row < HEAD_DIM, acc_ref[h] * (1.0 / l_ref[h]), acc_ref[h + 1] * (1.0 / l_ref[h + 1]))
        o_ref[:, pair * PAIR_W:(pair + 1) * PAIR_W] = jnp.transpose(out_t)


def _attn_a(qat, ka, vat, kaug, qaug, nb, s):
    tq = A_QTILE
    nq = s // tq
    return pl.pallas_call(
        _attn_a_kernel, grid=(nb, nq),
        in_specs=[pl.BlockSpec((GROUP_W, tq), lambda b, i: (0, b * nq + i)),
                  pl.BlockSpec((s, GROUP_W), lambda b, i: (b, 0)),
                  pl.BlockSpec((1, s // ATT_TILE, GROUP_W, ATT_TILE), lambda b, i: (b, 0, 0, 0)),
                  pl.BlockSpec((s, LANES), lambda b, i: (b, 0)),
                  pl.BlockSpec((LANES, tq), lambda b, i: (0, b * nq + i))],
        out_specs=pl.BlockSpec((tq, GROUP_W), lambda b, i: (b * nq + i, 0)),
        out_shape=jax.ShapeDtypeStruct((nb * s, GROUP_W), F32),
        scratch_shapes=[pltpu.VMEM((H_A, 2 * PAIR_W, tq), BF16), pltpu.VMEM((H_A, 1, tq), F32),
                        pltpu.VMEM((H_A, 1, tq), F32), pltpu.VMEM((H_A, PAIR_W, tq), F32)],
        compiler_params=_cparams("parallel", "arbitrary"), name="attn_a")(qat, ka, vat, kaug, qaug)


def _masked_q(q_ref, qm_ref, n_heads):
    for h in range(n_heads):
        q2 = q_ref[:, _pair_slice(h)]
        qm_ref[h] = jnp.where(_half_mask(h % 2), q2, jnp.zeros_like(q2))


def _attn_b_kernel(q_ref, k_ref, v_ref, tab_ref, o_ref):
    tq = ATT_TILE
    win = tq + BAND
    i = pl.program_id(1)
    var = jnp.minimum(i, BAND // tq)
    ws = pl.multiple_of(jnp.maximum(i * tq - BAND, 0), tq)
    heads = range(H_B)
    scores = []
    for h in heads:
        q2 = q_ref[:, _pair_slice(h)]
        qm = jnp.where(_half_mask(h % 2), q2, jnp.zeros_like(q2))
        scores.append(_dot_nt(qm, k_ref[pl.ds(ws, win), _pair_slice(h)]) + tab_ref[var, h])
    probs, norms = [], []
    for h in heads:
        p = jnp.exp2(scores[h] - jnp.max(scores[h], axis=-1, keepdims=True))
        norms.append(1.0 / jnp.sum(p, axis=-1, keepdims=True))
        probs.append(p.astype(BF16))
    res = [_dot(probs[h], v_ref[pl.ds(ws, win), _pair_slice(h)]) * norms[h] for h in heads]
    res.append(jnp.zeros((tq, PAIR_W), F32))
    first = _half_mask(0)
    for pair in range(N_PAIRS):
        o_ref[:, pair * PAIR_W:(pair + 1) * PAIR_W] = jnp.where(first, res[2 * pair], res[2 * pair + 1])


def _suffix_tri(n):
    j = lax.broadcasted_iota(jnp.int32, (2 * n, n), 0)
    s = lax.broadcasted_iota(jnp.int32, (2 * n, n), 1)
    return jnp.where(jnp.where(j >= n, j - n, j) >= s, 1.0, 0.0).astype(BF16)


def _stick_scores(z, tri, seen):
    sp = _softplus2(z)
    if seen is not None:
        sp = jnp.where(seen, sp, 0.0)
    hi, lo = _split2(sp)
    n = sp.shape[1]
    if n % LANES == 0:
        return _dot(jnp.concatenate([hi, lo], axis=1), tri)
    return _dot(hi, tri[:n]) + _dot(lo, tri[:n])


def _stick_weights(z, s_in, carry, seen):
    a = jnp.exp2(z - (s_in + carry))
    if seen is not None:
        a = jnp.where(seen, a, 0.0)
    return a.astype(BF16)


def _attn_c_kernel(q_ref, k_ref, v_ref, o_ref, qm_ref, carry_ref, acc_ref):
    tq = q_ref.shape[0]
    tk = ATT_TILE
    qi = pl.program_id(1)
    _masked_q(q_ref, qm_ref, H_C)
    carry_ref[...] = jnp.zeros(carry_ref.shape, F32)
    acc_ref[...] = jnp.zeros(acc_ref.shape, F32)
    heads = range(H_C)

    def step(kt, key_offset):
        ks = pl.multiple_of(kt * tk, tk)
        zs = [_dot_nt(qm_ref[h], k_ref[pl.ds(ks, tk), _pair_slice(h)]) for h in heads]
        tri = _suffix_tri(tk)
        seen = None
        if key_offset is not None:
            r = lax.broadcasted_iota(jnp.int32, (tq, tk), 0)
            c = lax.broadcasted_iota(jnp.int32, (tq, tk), 1)
            seen = c + key_offset < r
        sums = [_stick_scores(zs[h], tri, seen) for h in heads]
        weights = []
        for h in heads:
            carry = carry_ref[h][:, 0:1]
            weights.append(_stick_weights(zs[h], sums[h], carry, seen))
            carry_ref[h] = jnp.broadcast_to(carry + sums[h][:, 0:1], (tq, LANES))
        for h in heads:
            acc_ref[h] += _dot(weights[h], v_ref[pl.ds(ks, tk), _pair_slice(h)])

    per = tq // tk

    def body(j, c):
        step(qi * per - 1 - j, None)
        return c

    for j in reversed(range(per)):
        step(qi * per + j, j * tk)
    lax.fori_loop(0, qi * per, body, 0)
    first = _half_mask(0)
    for pair in range(N_PAIRS):
        h = 2 * pair
        second = acc_ref[h + 1] if h + 1 < H_C else jnp.zeros((tq, PAIR_W), F32)
        o_ref[:, pair * PAIR_W:(pair + 1) * PAIR_W] = jnp.where(first, acc_ref[h], second)


def _prompt_attention(kernel, name, q, k, v, extra, extra_specs, scratch, nb, s, tq):
    nq = s // tq
    qspec = pl.BlockSpec((tq, GROUP_W), lambda b, i: (b * nq + i, 0))
    kvspec = pl.BlockSpec((s, GROUP_W), lambda b, i: (b, 0))
    return pl.pallas_call(
        kernel, grid=(nb, nq),
        in_specs=[qspec, kvspec, kvspec] + extra_specs,
        out_specs=qspec, out_shape=jax.ShapeDtypeStruct((nb * s, GROUP_W), F32),
        scratch_shapes=scratch,
        compiler_params=_cparams("parallel", "arbitrary"), name=name)(q, k, v, *extra)


def _sample_heads(n_heads, q_ref, width):
    out = []
    for h in range(n_heads):
        lo = (h // 2) * PAIR_W
        span = slice(lo, min(lo + PAIR_W, width))
        q2 = q_ref[:, span]
        if span.stop - span.start == PAIR_W:
            q2 = jnp.where(_half_mask(h % 2), q2, jnp.zeros_like(q2))
        out.append((h, span, q2))
    return out


def _store_heads(o_ref, res, n_heads):
    for pair in range(N_PAIRS):
        lo = pair * PAIR_W
        h = 2 * pair
        if h + 1 < n_heads:
            o_ref[:, lo:lo + PAIR_W] = jnp.where(_half_mask(0), res[h], res[h + 1])
        else:
            o_ref[:, lo:lo + HEAD_DIM] = res[h]
            o_ref[:, lo + HEAD_DIM:lo + PAIR_W] = jnp.zeros_like(res[h])


def _joint_softmax_pv(s_c, s_n, vt_c, v_n):
    m = jnp.maximum(jnp.max(s_c, axis=-1, keepdims=True), jnp.max(s_n, axis=-1, keepdims=True))
    p_c = jnp.exp2(s_c - m)
    p_n = jnp.exp2(s_n - m)
    l = jnp.sum(p_c, axis=-1, keepdims=True) + jnp.sum(p_n, axis=-1, keepdims=True)
    return (_dot_nt(p_c.astype(BF16), vt_c) + _dot(p_n.astype(BF16), v_n)) / l


def _sample_attn_kernel(qa_ref, qb_ref, qc_ref, kan_ref, kbn_ref, kcn_ref, van_ref, vbn_ref, vcn_ref,
                        cak_ref, cav_ref, cbk_ref, cbv_ref, cck_ref, ccv_ref,
                        rsum_ref, logf_ref, tabc_ref, tabn_ref,
                        oa_ref, ob_ref, oc_ref):
    t = qa_ref.shape[0]
    past = cak_ref.shape[3]
    r = lax.broadcasted_iota(jnp.int32, (t, t), 0)
    c = lax.broadcasted_iota(jnp.int32, (t, t), 1)

    ltri = jnp.where(c <= r, 1.0, 0.0).astype(BF16)
    hi, mid, lo = _split3(logf_ref[...])
    pcol = (_dot(ltri, hi) + _dot(ltri, mid) + _dot(ltri, lo)) * LOG2E
    prow = jnp.transpose(jnp.concatenate([pcol, jnp.zeros((LANES - t, LANES), F32)], axis=0))
    heads = _sample_heads(H_A, qa_ref, W_A)
    kts = {sp.start: cak_ref[0, 0, sp, :].astype(BF16) for _, sp, _ in heads}
    vts = {sp.start: cav_ref[0, 0, sp, :].astype(BF16) for _, sp, _ in heads}
    s_c = [_dot(qm, kts[sp.start]) + (pcol[:, h:h + 1] + rsum_ref[0, 0, h:h + 1, :] * LOG2E)
           for h, sp, qm in heads]
    s_n = [jnp.where(c <= r, _dot_nt(qm, kan_ref[:, sp]) + (pcol[:, h:h + 1] - prow[h:h + 1, 0:t]), NEG_INF)
           for h, sp, qm in heads]
    _store_heads(oa_ref, [_joint_softmax_pv(s_c[h], s_n[h], vts[sp.start], van_ref[:, sp])
                          for h, sp, _ in heads], H_A)

    heads = _sample_heads(H_B, qb_ref, W_B)
    kts = {sp.start: cbk_ref[0, 0, sp, :].astype(BF16) for _, sp, _ in heads}
    vts = {sp.start: cbv_ref[0, 0, sp, :].astype(BF16) for _, sp, _ in heads}
    s_c = [_dot(qm, kts[sp.start]) + tabc_ref[h] for h, sp, qm in heads]
    s_n = [_dot_nt(qm, kbn_ref[:, sp]) + tabn_ref[h] for h, sp, qm in heads]
    _store_heads(ob_ref, [_joint_softmax_pv(s_c[h], s_n[h], vts[sp.start], vbn_ref[:, sp])
                          for h, sp, _ in heads], H_B)

    heads = _sample_heads(H_C, qc_ref, W_C)
    nblk = past // ATT_TILE
    tri_n = _suffix_tri(t)
    tri_c = _suffix_tri(ATT_TILE)
    seen = c < r
    kts = {sp.start: cck_ref[0, 0, sp, :].astype(BF16) for _, sp, _ in heads}
    vts = {sp.start: ccv_ref[0, 0, sp, :].astype(BF16) for _, sp, _ in heads}
    z_n = [_dot_nt(qm, kcn_ref[:, sp]) for _, sp, qm in heads]
    z_c = [_dot(qm, kts[sp.start]) for _, sp, qm in heads]
    sum_n = [_stick_scores(z, tri_n, seen) for z in z_n]
    sum_c = [[_stick_scores(z[:, b * ATT_TILE:(b + 1) * ATT_TILE], tri_c, None) for b in range(nblk)]
             for z in z_c]
    res = []
    for h, sp, _ in heads:
        acc = _dot(_stick_weights(z_n[h], sum_n[h], 0.0, seen), vcn_ref[:, sp])
        carry = sum_n[h][:, 0:1]
        for b in reversed(range(nblk)):
            cols = slice(b * ATT_TILE, (b + 1) * ATT_TILE)
            a = _stick_weights(z_c[h][:, cols], sum_c[h][b], carry, None)
            acc = acc + _dot_nt(a, vts[sp.start][:, cols])
            carry = carry + sum_c[h][b][:, 0:1]
        res.append(acc)
    _store_heads(oc_ref, res, H_C)


def _sample_attention(proj, caches, layer, rsum, tabc, tabn, nb, t):
    qa, qb, qc, ka, kb, kc, va, vb, vc = proj[:9]
    logf = proj[15]
    new = pl.BlockSpec((t, GROUP_W), lambda b: (b, 0))
    cache = lambda a: pl.BlockSpec((1, 1) + a.shape[2:], lambda b: (layer, b, 0, 0))
    full = lambda a: pl.BlockSpec(a.shape, lambda b: (0,) * a.ndim)
    return pl.pallas_call(
        _sample_attn_kernel, grid=(nb,),
        in_specs=[new] * 9 + [cache(a) for a in caches] + [
            cache(rsum), pl.BlockSpec((t, LANES), lambda b: (b, 0)), full(tabc), full(tabn)],
        out_specs=[new] * 3,
        out_shape=[jax.ShapeDtypeStruct((nb * t, GROUP_W), F32)] * 3,
        compiler_params=_cparams("parallel"), name="sample_attn")(
            qa, qb, qc, ka, kb, kc, va, vb, vc, *caches, rsum, logf, tabc, tabn)


def _masked_max(x, mask):
    return jnp.max(jnp.where(mask, x, -jnp.inf), axis=-1, keepdims=True)


def _first_lane(mask, lane):
    return jnp.min(jnp.where(mask, lane, float(LANES)), axis=-1, keepdims=True)


def _route(logits):
    lane = _lane_iota().astype(F32)
    is_g = (lane >= N_EXPERTS) & (lane < N_EXPERTS + N_GROUPS)
    gmax = _masked_max(logits, is_g)
    p_g = 1.0 / jnp.sum(jnp.where(is_g, jnp.exp(logits - gmax), 0.0), axis=-1, keepdims=True)
    g_sel = _first_lane(is_g & (logits == gmax), lane) - N_EXPERTS
    lo = g_sel * EXPERTS_PER_GROUP
    in_g = (lane >= lo) & (lane < lo + EXPERTS_PER_GROUP)
    l1 = _masked_max(logits, in_g)
    i1 = _first_lane(in_g & (logits == l1), lane)
    rest = in_g & (lane != i1)
    l2 = _masked_max(logits, rest)
    i2 = _first_lane(rest & (logits == l2), lane)
    e2 = jnp.exp(l2 - l1)
    w1 = p_g / (1.0 + e2)
    w2 = p_g * e2 / (1.0 + e2)
    gates = jnp.where(lane == i1, w1, jnp.where(lane == i2, w2, 0.0))
    return jnp.where(lane == g_sel + N_EXPERTS, 1.0, gates)


def _post_kernel(oa_ref, ob_ref, oc_ref, h_ref, gmix_ref, wout_ref, nffn_ref, wrh_ref, wrl_ref, br_ref,
                 h1_ref, xn_ref, gates_ref, *xchunk_refs):
    def gnorm(o, width):
        ms = jnp.sum(o * o, axis=-1, keepdims=True) * (1.0 / width)
        return o * lax.rsqrt(ms + EPS)

    h1 = h_ref[...]
    for g, (o_ref, width) in enumerate(((oa_ref, W_A), (ob_ref, W_B), (oc_ref, W_C))):
        merged = (gnorm(o_ref[...], width) * gmix_ref[g]).astype(BF16)
        h1 = h1 + _dot(merged, wout_ref[g])
    h1_ref[...] = h1
    xf = _rms(h1, nffn_ref[...])
    hi, lo = _split2(xf)
    xn_ref[...] = hi
    bits = pltpu.bitcast(hi.astype(F32), jnp.uint32)
    for c, ref in enumerate(xchunk_refs):
        a = bits[:, (2 * c) * SC_ROW_WORDS:(2 * c + 1) * SC_ROW_WORDS]
        b = bits[:, (2 * c + 1) * SC_ROW_WORDS:(2 * c + 2) * SC_ROW_WORDS]
        ref[...] = jnp.bitwise_or(a, jnp.right_shift(b, jnp.uint32(16)))
    logits = _dot(hi, wrh_ref[...]) + _dot(lo, wrh_ref[...]) + _dot(hi, wrl_ref[...]) + br_ref[...]
    gates_ref[...] = _route(logits)


def _post(oa, ob, oc, h, lw, tm, n_chunks=0):
    t = h.shape[0]
    row = lambda w: pl.BlockSpec((tm, w), lambda i: (i, 0))
    full = lambda a: pl.BlockSpec(a.shape, lambda i: (0,) * a.ndim)
    ws = (lw['g_mix'], lw['w_out'], lw['norm_ffn'], lw['w_r_hi'], lw['w_r_lo'], lw['b_r'])
    return pl.pallas_call(
        _post_kernel, grid=(t // tm,),
        in_specs=[row(GROUP_W)] * 3 + [row(D_MODEL)] + [full(a) for a in ws],
        out_specs=[row(D_MODEL), row(D_MODEL), row(LANES)] + [row(SC_ROW_WORDS)] * n_chunks,
        out_shape=[jax.ShapeDtypeStruct((t, D_MODEL), F32), jax.ShapeDtypeStruct((t, D_MODEL), BF16),
                   jax.ShapeDtypeStruct((t, LANES), F32)]
        + [jax.ShapeDtypeStruct((t, SC_ROW_WORDS), jnp.uint32)] * n_chunks,
        compiler_params=_cparams("parallel"), name="post")(oa, ob, oc, h, *ws)


def _moe_kernel(xn_ref, gates_ref, wg_ref, wu_ref, wd_ref, h1_ref, p_ref, wple_ref, nple_ref, wpg_ref,
                out_ref, acc_ref):
    step = pl.program_id(1)

    @pl.when(step == 0)
    def _():
        acc_ref[...] = jnp.zeros_like(acc_ref)

    x = xn_ref[...]
    gates = gates_ref[...]
    acts = []
    for k in range(EXPERTS_PER_STEP):
        e = step * EXPERTS_PER_STEP + k
        g = _dot(x, wg_ref[k])
        u = _dot(x, wu_ref[k])
        gate = jnp.sum(jnp.where(_lane_iota() == e, gates, 0.0), axis=-1, keepdims=True)
        acts.append(((g * _sigmoid(g)) * u * gate).astype(BF16))
    acc_ref[...] += _dot(jnp.concatenate(acts, axis=1), wd_ref[...])

    @pl.when(step == pl.num_programs(1) - 1)
    def _():
        h2 = h1_ref[...] + acc_ref[...]
        gate_ple = _sigmoid(_dot(_rms(h2, nple_ref[...]).astype(BF16), wpg_ref[...]))
        out_ref[...] = h2 + _dot(p_ref[0].astype(BF16), wple_ref[...]) * gate_ple


def _moe(xn, gates, h1, p_all, layer, lw, tm):
    t = xn.shape[0]
    row = lambda w: pl.BlockSpec((tm, w), lambda i, e: (i, 0))
    once = pl.Buffered(1)
    full = lambda a: pl.BlockSpec(a.shape, lambda i, e: (0,) * a.ndim, pipeline_mode=once)
    exp = lambda a: pl.BlockSpec((EXPERTS_PER_STEP,) + a.shape[1:], lambda i, e: (e, 0, 0))
    d_hidden = lw['w_ed'].shape[0] // N_EXPERTS
    return pl.pallas_call(
        _moe_kernel, grid=(t // tm, N_EXPERTS // EXPERTS_PER_STEP),
        in_specs=[row(D_MODEL), row(LANES), exp(lw['w_eg']), exp(lw['w_eu']),
                  pl.BlockSpec((EXPERTS_PER_STEP * d_hidden, D_MODEL), lambda i, e: (e, 0)),
                  pl.BlockSpec((tm, D_MODEL), lambda i, e: (i, 0), pipeline_mode=once),
                  pl.BlockSpec((1, tm, p_all.shape[2]), lambda i, e: (layer, i, 0), pipeline_mode=once),
                  full(lw['w_ple']), full(lw['norm_ple']), full(lw['w_pg'])],
        out_specs=row(D_MODEL), out_shape=jax.ShapeDtypeStruct((t, D_MODEL), F32),
        scratch_shapes=[pltpu.VMEM((tm, D_MODEL), F32)],
        compiler_params=pltpu.CompilerParams(dimension_semantics=("parallel", "arbitrary"),
                                             vmem_limit_bytes=MOE_VMEM_LIMIT), name="moe")(
            xn, gates, lw['w_eg'], lw['w_eu'], lw['w_ed'], h1, p_all, lw['w_ple'], lw['norm_ple'], lw['w_pg'])


SORT_TILE = 512
SC_WINDOW = 128
SC_ROW_WORDS = 256
N_CHUNKS = D_MODEL // SC_ROW_WORDS
N_XCHUNKS = N_CHUNKS // 2


def _plan_kernel(gates_ref, dest_ref, total_ref, *, tb):
    lane = _lane_iota()
    is_g = (lane >= N_EXPERTS) & (lane < N_EXPERTS + N_GROUPS)
    total = jnp.sum(jnp.where(is_g, gates_ref[...], 0.0), axis=0, keepdims=True)
    padded = jnp.floor((total + (SORT_TILE - 1)) * (1.0 / SORT_TILE)) * SORT_TILE
    base = jnp.zeros((1, LANES), F32)
    for g in range(1, N_GROUPS):
        before = jnp.sum(jnp.where(lane < N_EXPERTS + g, padded, 0.0), axis=-1, keepdims=True)
        base = jnp.where(lane == N_EXPERTS + g, before, base)
    r = lax.broadcasted_iota(jnp.int32, (tb, tb), 0)
    c = lax.broadcasted_iota(jnp.int32, (tb, tb), 1)
    ltri = jnp.where(c < r, 1.0, 0.0).astype(BF16)
    carry = jnp.zeros((1, LANES), F32)
    for blk in range(gates_ref.shape[0] // tb):
        rows = slice(blk * tb, (blk + 1) * tb)
        member = jnp.where(is_g, gates_ref[rows, :], 0.0)
        rank = _dot(ltri, member.astype(BF16)) + carry
        carry = carry + jnp.sum(member, axis=0, keepdims=True)
        dest = jnp.sum(member * (base + rank), axis=-1, keepdims=True)
        dest_ref[rows, :] = jnp.broadcast_to(dest, (tb, LANES))
    total_ref[...] = jnp.broadcast_to(total, total_ref.shape)


def _plan(gates):
    t = gates.shape[0]
    return pl.pallas_call(
        functools.partial(_plan_kernel, tb=SORT_TILE),
        out_shape=[jax.ShapeDtypeStruct((t, LANES), F32), jax.ShapeDtypeStruct((SUBLANES, LANES), F32)],
        compiler_params=pltpu.CompilerParams(vmem_limit_bytes=VMEM_LIMIT), name="moe_plan")(gates)


def _sc_scatter_rows(x, idx, n_out):
    n, d = x.shape
    mesh = plsc.VectorSubcoreMesh(core_axis_name="core", subcore_axis_name="subcore")

    @functools.partial(pl.kernel, out_type=jax.ShapeDtypeStruct((n_out, d), x.dtype), mesh=mesh, scratch_types=[])
    def scatter(x_hbm, i_hbm, o_hbm):
        def body(x_vmem, i_vmem):
            pltpu.sync_copy(x_vmem, o_hbm.at[i_vmem.at[0]])

        pltpu.emit_pipeline(
            body, grid=(n // SC_WINDOW,),
            in_specs=[pl.BlockSpec((SC_WINDOW, d), index_map=lambda i: (i, 0)),
                      pl.BlockSpec((1, SC_WINDOW), index_map=lambda i: (0, i))],
            out_specs=[], core_axis_name=('core', 'subcore'), dimension_semantics=(pltpu.PARALLEL,))(x_hbm, i_hbm)

    return scatter(x, idx.reshape(1, n))


def _sc_gather_rows(x, idx):
    n = idx.shape[0]
    d = x.shape[1]
    mesh = plsc.VectorSubcoreMesh(core_axis_name="core", subcore_axis_name="subcore")

    @functools.partial(pl.kernel, out_type=jax.ShapeDtypeStruct((n, d), x.dtype), mesh=mesh)
    def gather(x_hbm, i_hbm, o_hbm):
        def body(i_vmem, o_vmem):
            pltpu.sync_copy(x_hbm.at[i_vmem.at[0]], o_vmem)

        pltpu.emit_pipeline(
            body, grid=(n // SC_WINDOW,),
            in_specs=[pl.BlockSpec((1, SC_WINDOW), index_map=lambda i: (0, i))],
            out_specs=[pl.BlockSpec((SC_WINDOW, d), index_map=lambda i: (i, 0))],
            core_axis_name=('core', 'subcore'), dimension_semantics=(pltpu.PARALLEL,))(i_hbm, o_hbm)

    return gather(x, idx.reshape(1, n))


def _experts_sorted_kernel(tg_ref, tv_ref, *refs):
    xs_refs, (gs_ref, wg_ref, wu_ref, wd_ref), ys_refs = (
        refs[:N_XCHUNKS], refs[N_XCHUNKS:N_XCHUNKS + 4], refs[-N_CHUNKS:])
    j = pl.program_id(0)

    @pl.when(tv_ref[j] > 0)
    def _():
        g = tg_ref[j]
        halves = []
        for r in xs_refs:
            w = r[...]
            halves.append(pltpu.bitcast(jnp.bitwise_and(w, jnp.uint32(0xFFFF0000)), F32))
            halves.append(pltpu.bitcast(jnp.left_shift(w, jnp.uint32(16)), F32))
        x = jnp.concatenate(halves, axis=1).astype(BF16)
        src = lax.broadcasted_iota(jnp.int32, (LANES, D_MODEL), 0)
        dst = lax.broadcasted_iota(jnp.int32, (LANES, D_MODEL), 1)
        d_shift = (D_MODEL // EXPERTS_PER_GROUP).bit_length() - 1
        expand = jnp.where(src == g * EXPERTS_PER_GROUP + jnp.right_shift(dst, d_shift), 1.0, 0.0).astype(BF16)
        hi, mid, lo = _split3(gs_ref[...])
        ge = _dot(hi, expand) + _dot(mid, expand) + _dot(lo, expand)
        hg = _dot(x, wg_ref[0])
        act = (hg * _sigmoid(hg)) * _dot(x, wu_ref[0]) * ge
        y = _dot(act.astype(BF16), wd_ref[0])
        for c, ref in enumerate(ys_refs):
            ref[...] = y[:, c * SC_ROW_WORDS:(c + 1) * SC_ROW_WORDS]


def _experts_sorted(xs_chunks, gs, tile_group, tile_valid, lw):
    n_rows = gs.shape[0]
    row = lambda w: pl.BlockSpec((SORT_TILE, w), lambda j, tg, tv: (j, 0))
    grp = lambda a: pl.BlockSpec((1,) + a.shape[1:], lambda j, tg, tv: (tg[j], 0, 0))
    return pl.pallas_call(
        _experts_sorted_kernel,
        grid_spec=pltpu.PrefetchScalarGridSpec(
            num_scalar_prefetch=2, grid=(n_rows // SORT_TILE,),
            in_specs=[row(SC_ROW_WORDS)] * N_XCHUNKS + [row(LANES), grp(lw['w_gg']), grp(lw['w_gu']), grp(lw['w_gd'])],
            out_specs=[row(SC_ROW_WORDS)] * N_CHUNKS),
        out_shape=[jax.ShapeDtypeStruct((n_rows, SC_ROW_WORDS), F32)] * N_CHUNKS,
        compiler_params=_cparams("arbitrary"), name="moe_sorted")(
            tile_group, tile_valid, *xs_chunks, gs, lw['w_gg'], lw['w_gu'], lw['w_gd'])


def _ple_kernel(h1_ref, *refs):
    y_refs, (p_ref, wple_ref, nple_ref, wpg_ref, out_ref) = refs[:N_CHUNKS], refs[N_CHUNKS:]
    h2 = h1_ref[...] + jnp.concatenate([r[...] for r in y_refs], axis=1)
    gate_ple = _sigmoid(_dot(_rms(h2, nple_ref[...]).astype(BF16), wpg_ref[...]))
    out_ref[...] = h2 + _dot(p_ref[0].astype(BF16), wple_ref[...]) * gate_ple


def _ple(h1, y_chunks, p_all, layer, lw, tm):
    t = h1.shape[0]
    row = lambda w: pl.BlockSpec((tm, w), lambda i: (i, 0))
    full = lambda a: pl.BlockSpec(a.shape, lambda i: (0,) * a.ndim)
    return pl.pallas_call(
        _ple_kernel, grid=(t // tm,),
        in_specs=[row(D_MODEL)] + [row(SC_ROW_WORDS)] * N_CHUNKS
        + [pl.BlockSpec((1, tm, p_all.shape[2]), lambda i: (layer, i, 0)),
           full(lw['w_ple']), full(lw['norm_ple']), full(lw['w_pg'])],
        out_specs=row(D_MODEL), out_shape=jax.ShapeDtypeStruct((t, D_MODEL), F32),
        compiler_params=_cparams("parallel"), name="ple")(
            h1, *y_chunks, p_all, lw['w_ple'], lw['norm_ple'], lw['w_pg'])


def _moe_dispatch(x_chunks, gates):
    t = gates.shape[0]
    n_rows = t + N_GROUPS * SORT_TILE
    dest_rep, total = _plan(gates)
    dest = dest_rep[:, 0].astype(jnp.int32)
    padded = (total[0, N_EXPERTS:N_EXPERTS + N_GROUPS].astype(jnp.int32) + (SORT_TILE - 1)) // SORT_TILE * SORT_TILE
    ends = jnp.cumsum(padded)
    tile_start = jnp.arange(n_rows // SORT_TILE, dtype=jnp.int32) * SORT_TILE
    tile_group = jnp.minimum(jnp.sum(tile_start[:, None] >= ends[None, :], axis=1), N_GROUPS - 1).astype(jnp.int32)
    tile_valid = (tile_start < ends[-1]).astype(jnp.int32)
    xs_chunks = [_sc_scatter_rows(xc, dest, n_rows) for xc in x_chunks]
    gs = _sc_scatter_rows(gates, dest, n_rows)
    return xs_chunks, gs, tile_group, tile_valid, dest


def _moe_combine(dispatched, h1, p_all, layer, lw):
    xs_chunks, gs, tile_group, tile_valid, dest = dispatched
    ys_chunks = _experts_sorted(xs_chunks, gs, tile_group, tile_valid, lw)
    y_chunks = [_sc_gather_rows(yc, dest) for yc in ys_chunks]
    return _ple(h1, y_chunks, p_all, layer, lw, _token_tile(h1.shape[0], 512))


def _regroup_out_rows(wt):
    z = jnp.zeros((HEAD_DIM, wt.shape[1]), wt.dtype)
    return jnp.concatenate([wt[:W_A], wt[W_A:W_A + W_B], z, wt[W_A + W_B:], z], axis=0)


def _regroup_rows(w):
    z = jnp.zeros((HEAD_DIM, w.shape[1]), w.dtype)
    return jnp.stack([w[:W_A], jnp.concatenate([w[W_A:W_A + W_B], z], axis=0),
                      jnp.concatenate([w[W_A + W_B:], z], axis=0)])


def _pad_lanes(a, n=LANES):
    return jnp.pad(a, [(0, 0)] * (a.ndim - 1) + [(0, n - a.shape[-1])])


def _toeplitz_bias(rel_bias, off, tq, win):
    length = tq + win
    pad = length + abs(off)
    ext = jnp.flip(jnp.pad(rel_bias, ((0, 0), (pad, pad)), mode='edge'), axis=1)
    s1 = ext.shape[1] - 1 - (off + REL_CLIP + pad)
    v = jnp.concatenate([ext[:, s1:s1 + win], ext[:, s1 - tq:s1]], axis=1)
    flat = jnp.tile(v, (1, tq))[:, :tq * (length - 1)]
    return flat.reshape(-1, tq, length - 1)[:, :, :win]


def _block_toeplitz_bias(rel_bias, off, tq, win):
    nq, nk = tq // LANES, win // LANES
    blocks = {d: _toeplitz_bias(rel_bias, off - d * LANES, LANES, LANES) for d in range(-(nq - 1), nk)}
    return jnp.concatenate(
        [jnp.concatenate([blocks[b - a] for b in range(nk)], axis=2) for a in range(nq)], axis=1)


def _rel_tables(rel_bias, t_new, rows_b):
    tq, win = ATT_TILE, ATT_TILE + BAND
    q = jnp.arange(tq)[:, None]
    tabs = []
    for var in range(BAND // tq + 1):
        rel_k = jnp.arange(win)[None, :] - var * tq
        kch = jnp.floor_divide(rel_k, CHUNK)
        qch = q // CHUNK
        valid = (kch <= qch) & (kch >= qch - BAND_CHUNKS)
        tabs.append(jnp.where(valid[None], _block_toeplitz_bias(rel_bias, var * tq, tq, win), NEG_INF))
    tab_prompt = jnp.stack(tabs).astype(F32)
    tab_c = _toeplitz_bias(rel_bias, rows_b, t_new, rows_b).astype(F32)
    tab_n = _toeplitz_bias(rel_bias, 0, t_new, t_new).astype(F32)
    return tab_prompt, tab_c, tab_n


def _layer_weights(i, norm_mix, w_in, b_f, g_qa, g_ka, g_qb, g_kb, g_mix_out, w_out, norm_ffn,
                   w_rg, b_rg, w_re, b_re, w_eg, w_eu, w_ed, w_ple, norm_ple, w_pg):
    d_mix = W_A + W_B + W_C
    wt = jnp.transpose(w_in, (2, 0, 1))[:, i, :]
    w_f = wt[3 * d_mix:]
    w_big = jnp.concatenate(
        [_regroup_out_rows(wt[j * d_mix:(j + 1) * d_mix]) for j in range(3)]
        + [jnp.pad(w_f, ((0, LANES - w_f.shape[0]), (0, 0)))], axis=0).astype(BF16)
    tile6 = lambda g: jnp.tile(g, GROUP_W // HEAD_DIM)
    w_r = _pad_lanes(jnp.concatenate([w_re[i], w_rg[i]], axis=1))
    w_r_hi = w_r.astype(BF16)
    gm = g_mix_out[i]
    zpad = jnp.zeros((HEAD_DIM,), F32)
    g_mix = jnp.stack([gm[:W_A], jnp.concatenate([gm[W_A:W_A + W_B], zpad]),
                       jnp.concatenate([gm[W_A + W_B:], zpad])])[:, None, :]
    return dict(
        norm_mix=norm_mix[i][None], w_in=w_big,
        gq=jnp.concatenate([tile6(g_qa[i]), tile6(g_qb[i])])[None],
        gk=jnp.concatenate([tile6(g_ka[i]), tile6(g_kb[i])])[None],
        b_f=_pad_lanes(b_f[i])[None],
        g_mix=g_mix, w_out=_regroup_rows(w_out[i]).astype(BF16), norm_ffn=norm_ffn[i][None],
        w_r_hi=w_r_hi, w_r_lo=(w_r - w_r_hi.astype(F32)).astype(BF16),
        b_r=_pad_lanes(jnp.concatenate([b_re[i], b_rg[i]]))[None],
        w_eg=w_eg[i].astype(BF16), w_eu=w_eu[i].astype(BF16),
        w_ed=w_ed[i].reshape(-1, w_ed.shape[-1]).astype(BF16),
        w_gg=_group_cols(w_eg[i]), w_gu=_group_cols(w_eu[i]),
        w_gd=w_ed[i].reshape(N_GROUPS, -1, w_ed.shape[-1]).astype(BF16),
        w_ple=w_ple[i].astype(BF16), norm_ple=norm_ple[i][None], w_pg=w_pg[i].astype(BF16))


def _group_cols(w):
    e, d, f = w.shape
    w = w.reshape(N_GROUPS, EXPERTS_PER_GROUP, d, f)
    return jnp.transpose(w, (0, 2, 1, 3)).reshape(N_GROUPS, d, EXPERTS_PER_GROUP * f).astype(BF16)


def _token_tile(t, pref):
    return pref if t % pref == 0 else t


def _feature_major(cache):
    d, n, p, h, e = cache.shape
    return jnp.transpose(cache, (0, 1, 3, 4, 2)).reshape(d, n, h * e, p)


def _position_major(rows, heads):
    d, n, _, p = rows.shape
    return jnp.transpose(rows.reshape(d, n, heads, HEAD_DIM, p), (0, 1, 4, 2, 3))


def kernel(x_prompt, x_sample, p_prompt, p_sample, cache_a_k, cache_a_v, cache_a_logf, cache_b_k, cache_b_v, cache_c_k, cache_c_v, norm_mix, w_in, b_f, g_qa, g_ka, g_qb, g_kb, rel_bias, g_mix_out, w_out, norm_ffn, w_router_group, b_router_group, w_router_expert, b_router_expert, w_exp_gate, w_exp_up, w_exp_down, w_ple, norm_ple, w_ple_gate):
    nb, s, d = x_prompt.shape
    ns, t_new, _ = x_sample.shape
    depth = w_in.shape[0]
    past = cache_a_k.shape[2]
    rows_b = cache_b_k.shape[2]
    assert d == D_MODEL and s % PROJ_TILE == 0 and s >= BAND + ATT_TILE and past % ATT_TILE == 0
    tp, ts = nb * s, ns * t_new

    hp = x_prompt.reshape(tp, d)
    hs = x_sample.reshape(ts, d)
    pp = p_prompt.reshape(depth, tp, -1)
    ps = p_sample.reshape(depth, ts, -1)
    caches = tuple(_feature_major(c) for c in (cache_a_k, cache_a_v, cache_b_k, cache_b_v, cache_c_k, cache_c_v))
    lf_rows = jnp.transpose(cache_a_logf, (0, 3, 1, 2))
    rsum = _suffix_sum(lf_rows.reshape(depth * H_A * ns, past)).reshape(depth, H_A, ns, past)
    rsum = jnp.transpose(rsum, (0, 2, 1, 3))

    rows_p = None
    new_s = [[] for _ in range(7)]
    for i in range(depth):
        lw = _layer_weights(i, norm_mix, w_in, b_f, g_qa, g_ka, g_qb, g_kb, g_mix_out, w_out, norm_ffn,
                            w_router_group, b_router_group, w_router_expert, b_router_expert,
                            w_exp_gate, w_exp_up, w_exp_down, w_ple, norm_ple, w_ple_gate)
        tab_p, tab_c, tab_n = _rel_tables(rel_bias[i] * LOG2E, t_new, rows_b)

        pr = _inproj_prompt(hp, lw, i, depth, nb, s, rows_p)
        qat, ka, vat, qb, kb, vb, qc, kc, vc, logf = pr[:10]
        rows_p = pr[10:]
        kaug, qaug = _forget_aug(logf, nb, s)
        oa = _attn_a(qat, ka, vat, kaug, qaug, nb, s)
        ob = _prompt_attention(
            _attn_b_kernel, "attn_b", qb, kb, vb, (tab_p,),
            [pl.BlockSpec(tab_p.shape, lambda b, j: (0, 0, 0, 0))], [], nb, s, ATT_TILE)
        oc = _prompt_attention(
            _attn_c_kernel, "attn_c", qc, kc, vc, (), [],
            [pltpu.VMEM((H_C, C_QTILE, PAIR_W), BF16), pltpu.VMEM((H_C, C_QTILE, LANES), F32),
             pltpu.VMEM((H_C, C_QTILE, PAIR_W), F32)], nb, s, C_QTILE)
        h1, _, gates, *x_chunks = _post(oa, ob, oc, hp, lw, _token_tile(tp, 512), N_XCHUNKS)
        dispatched = _moe_dispatch(x_chunks, gates)

        sr = _inproj_sample(hs, lw)
        oa, ob, oc = _sample_attention(sr, caches, i, rsum, tab_c, tab_n, ns, t_new)
        h1s, xn, gates = _post(oa, ob, oc, hs, lw, _token_tile(ts, 512))
        hs = _moe(xn, gates, h1s, ps, i, lw, _token_tile(ts, 1024))

        hp = _moe_combine(dispatched, h1, pp, i, lw)
        akf, avf, bkf, bvf, ckf, cvf, logf = sr[9:]
        rows = (akf.reshape(ns, t_new, H_A, HEAD_DIM), avf.reshape(ns, t_new, H_A, HEAD_DIM),
                logf[:, :H_A].reshape(ns, t_new, H_A),
                bkf.reshape(ns, t_new, H_B, HEAD_DIM), bvf.reshape(ns, t_new, H_B, HEAD_DIM),
                ckf.reshape(ns, t_new, H_C, HEAD_DIM), cvf.reshape(ns, t_new, H_C, HEAD_DIM))
        for j in range(7):
            new_s[j].append(rows[j])

    akt, avt, bkt, bvt, ckt, cvt, lft = rows_p
    outs_p = (_position_major(akt, H_A), _position_major(avt, H_A),
              jnp.transpose(lft.reshape(depth, H_A, nb, s), (0, 2, 3, 1)),
              _position_major(bkt, H_B), _position_major(bvt, H_B),
              _position_major(ckt, H_C), _position_major(cvt, H_C))
    outs_s = [jnp.stack(r, axis=0) for r in new_s]
    return (hp.reshape(nb, s, d), hs.reshape(ns, t_new, d), *outs_p, *outs_s)
```

```python
import functools
import math

import jax
import jax.numpy as jnp
from jax import lax
from jax.experimental import pallas as pl
from jax.experimental.pallas import tpu as pltpu
from jax.experimental.pallas import tpu_sc as plsc

F32 = jnp.float32
BF16 = jnp.bfloat16

D_MODEL = 1024
HEAD_DIM = 64
H_A, H_B, H_C = 6, 5, 5
W_A, W_B, W_C = H_A * HEAD_DIM, H_B * HEAD_DIM, H_C * HEAD_DIM
GROUP_W = 384
PAIR_W = 2 * HEAD_DIM
N_PAIRS = GROUP_W // PAIR_W
CHUNK = 64
BAND_CHUNKS = 8
BAND = BAND_CHUNKS * CHUNK
REL_CLIP = 128
N_GROUPS = 4
EXPERTS_PER_GROUP = 4
N_EXPERTS = N_GROUPS * EXPERTS_PER_GROUP
EPS = 1e-6
ATTN_SCALE = HEAD_DIM ** -0.5
LOG2E = math.log2(math.e)
Q_SCALE = ATTN_SCALE * LOG2E
NEG_INF = -1e30
LANES = 128
SUBLANES = 8
VMEM_LIMIT = 48 * 1024 * 1024
EXPERTS_PER_STEP = 4
MOE_VMEM_LIMIT = 58 * 1024 * 1024

ATT_TILE = 256
A_QTILE = 512
C_QTILE = 256
PROJ_TILE = 1024
PROJ_VMEM_LIMIT = 56 * 1024 * 1024
AUG_SLOTS = 8

_NT = (((1,), (1,)), ((), ()))


def _cparams(*sem):
    return pltpu.CompilerParams(dimension_semantics=sem, vmem_limit_bytes=VMEM_LIMIT)


def _dot(a, b):
    return jnp.dot(a, b, preferred_element_type=F32)


def _dot_nt(a, b):
    return lax.dot_general(a, b, _NT, preferred_element_type=F32)


def _split3(x):
    hi = x.astype(BF16)
    r1 = x - hi.astype(F32)
    mid = r1.astype(BF16)
    lo = (r1 - mid.astype(F32)).astype(BF16)
    return hi, mid, lo


def _split2(x):
    hi = x.astype(BF16)
    lo = (x - hi.astype(F32)).astype(BF16)
    return hi, lo


def _lane_iota(n=LANES):
    return lax.broadcasted_iota(jnp.int32, (1, n), 1)


def _half_mask(half):
    lane = _lane_iota()
    return (lane < HEAD_DIM) if half == 0 else (lane >= HEAD_DIM)


def _pair_slice(h):
    return slice((h // 2) * PAIR_W, (h // 2 + 1) * PAIR_W)


def _softplus2(z2):
    return jnp.maximum(z2, 0.0) + jnp.log2(1.0 + jnp.exp2(-jnp.abs(z2)))


def _sigmoid(z):
    return 1.0 / (1.0 + jnp.exp(-z))


def _rms(x, g):
    return x * lax.rsqrt(jnp.mean(x * x, axis=-1, keepdims=True) + EPS) * g


def _pair_rms(z, gain):
    first = _half_mask(0)
    outs = []
    for j in range(N_PAIRS):
        blk = z[:, j * PAIR_W:(j + 1) * PAIR_W]
        sq = blk * blk
        lo = jnp.sum(jnp.where(first, sq, 0.0), axis=-1, keepdims=True)
        hi = jnp.sum(jnp.where(first, 0.0, sq), axis=-1, keepdims=True)
        ms = jnp.where(first, lo, hi) * (1.0 / HEAD_DIM)
        outs.append(blk * lax.rsqrt(ms + EPS) * gain[:, j * PAIR_W:(j + 1) * PAIR_W])
    return jnp.concatenate(outs, axis=-1)


def _projector(x_ref, gn_ref, w_ref):
    xb = _rms(x_ref[...], gn_ref[...]).astype(BF16)
    return lambda g, width=GROUP_W: _dot_nt(xb, w_ref[g * GROUP_W:g * GROUP_W + width, :])


def _log_forget(zf, bf_ref):
    zf = zf + bf_ref[...]
    return jnp.minimum(zf, 0.0) - jnp.log(1.0 + jnp.exp(-jnp.abs(zf)))


def _inproj_sample_kernel(x_ref, gn_ref, w_ref, gq_ref, gk_ref, bf_ref,
                          qa_ref, qb_ref, qc_ref, ka_ref, kb_ref, kc_ref, va_ref, vb_ref, vc_ref,
                          akf_ref, avf_ref, bkf_ref, bvf_ref, ckf_ref, cvf_ref, logf_ref):
    proj = _projector(x_ref, gn_ref, w_ref)
    qa_ref[...] = (_pair_rms(proj(0), gq_ref[:, :GROUP_W]) * Q_SCALE).astype(BF16)
    qb_ref[...] = (_pair_rms(proj(1), gq_ref[:, GROUP_W:]) * Q_SCALE).astype(BF16)
    qc_ref[...] = (proj(2) * Q_SCALE).astype(BF16)
    for g, norm, bf_ref_, f_ref, width in (
            (3, gk_ref[:, :GROUP_W], ka_ref, akf_ref, W_A), (4, gk_ref[:, GROUP_W:], kb_ref, bkf_ref, W_B),
            (5, None, kc_ref, ckf_ref, W_C), (6, None, va_ref, avf_ref, W_A),
            (7, None, vb_ref, bvf_ref, W_B), (8, None, vc_ref, cvf_ref, W_C)):
        z = proj(g)
        if norm is not None:
            z = _pair_rms(z, norm)
        f_ref[...] = z[:, :width]
        bf_ref_[...] = z.astype(BF16)
    logf_ref[...] = _log_forget(proj(9, LANES), bf_ref)


def _inproj_sample(x, lw):
    t = x.shape[0]
    row = lambda w: pl.BlockSpec((t, w), lambda i: (0, 0))
    full = lambda a: pl.BlockSpec(a.shape, lambda i: (0,) * a.ndim)
    widths = [GROUP_W] * 9 + [W_A, W_A, W_B, W_B, W_C, W_C, LANES]
    out_shape = [jax.ShapeDtypeStruct((t, w), BF16 if j < 9 else F32) for j, w in enumerate(widths)]
    ins = (x, lw['norm_mix'], lw['w_in'], lw['gq'], lw['gk'], lw['b_f'])
    return pl.pallas_call(
        _inproj_sample_kernel, grid=(1,),
        in_specs=[row(D_MODEL)] + [full(a) for a in ins[1:]],
        out_specs=[row(w) for w in widths], out_shape=out_shape,
        compiler_params=_cparams("arbitrary"), name="inproj_sample")(*ins)


N_PROMPT_ROWS = 7


def _inproj_prompt_kernel(*refs, n_alias):
    x_ref, gn_ref, w_ref, gq_ref, gk_ref, bf_ref = refs[:6]
    (qat_ref, ka_ref, vat_ref, qb_ref, kb_ref, vb_ref, qc_ref, kc_ref, vc_ref, logf_ref,
     akt_ref, avt_ref, bkt_ref, bvt_ref, ckt_ref, cvt_ref, lft_ref) = refs[6 + n_alias:]
    proj = _projector(x_ref, gn_ref, w_ref)
    qat_ref[...] = jnp.transpose(_pair_rms(proj(0), gq_ref[:, :GROUP_W]) * Q_SCALE).astype(BF16)
    qb_ref[...] = (_pair_rms(proj(1), gq_ref[:, GROUP_W:]) * Q_SCALE).astype(BF16)
    qc_ref[...] = (proj(2) * Q_SCALE).astype(BF16)
    ka = _pair_rms(proj(3), gk_ref[:, :GROUP_W])
    ka_ref[...] = ka.astype(BF16)
    akt_ref[0, 0] = jnp.transpose(ka)
    kb = _pair_rms(proj(4), gk_ref[:, GROUP_W:])
    kb_ref[...] = kb.astype(BF16)
    kc = proj(5)
    kc_ref[...] = kc.astype(BF16)
    ckt_ref[0, 0] = jnp.transpose(kc)[:W_C]
    vat = jnp.transpose(proj(6))
    avt_ref[0, 0] = vat
    for j in range(vat_ref.shape[1]):
        vat_ref[0, j] = vat[:, j * ATT_TILE:(j + 1) * ATT_TILE].astype(BF16)
    vb = proj(7)
    vb_ref[...] = vb.astype(BF16)
    vc = proj(8)
    vc_ref[...] = vc.astype(BF16)
    cvt_ref[0, 0] = jnp.transpose(vc)[:W_C]

    @pl.when(pl.program_id(1) == pl.num_programs(1) - 1)
    def _():
        keep = kb.shape[0] - BAND
        bkt_ref[0, 0] = jnp.transpose(kb)[:W_B, keep:]
        bvt_ref[0, 0] = jnp.transpose(vb)[:W_B, keep:]

    lf = _log_forget(proj(9, LANES), bf_ref)
    logf_ref[...] = lf
    lft_ref[0] = jnp.transpose(lf)[:H_A]


def _inproj_prompt(x, lw, layer, depth, nb, s, prev_rows):
    tm = PROJ_TILE
    nt = s // tm
    t = nb * s
    per = tm // ATT_TILE
    row = lambda w: pl.BlockSpec((tm, w), lambda b, i: (b * nt + i, 0))
    full = lambda a: pl.BlockSpec(a.shape, lambda b, i: (0,) * a.ndim, pipeline_mode=pl.Buffered(1))
    feat = lambda w: pl.BlockSpec((1, 1, w, tm), lambda b, i: (layer, b, 0, i))
    band = lambda w: pl.BlockSpec((1, 1, w, BAND), lambda b, i: (layer, b, 0, 0))
    tok = jax.ShapeDtypeStruct((t, GROUP_W), BF16)
    out_shape = [
        jax.ShapeDtypeStruct((GROUP_W, t), BF16), tok,
        jax.ShapeDtypeStruct((nb, s // ATT_TILE, GROUP_W, ATT_TILE), BF16),
        tok, tok, tok, tok, tok, tok, jax.ShapeDtypeStruct((t, LANES), F32),
        jax.ShapeDtypeStruct((depth, nb, W_A, s), F32), jax.ShapeDtypeStruct((depth, nb, W_A, s), F32),
        jax.ShapeDtypeStruct((depth, nb, W_B, BAND), F32), jax.ShapeDtypeStruct((depth, nb, W_B, BAND), F32),
        jax.ShapeDtypeStruct((depth, nb, W_C, s), F32), jax.ShapeDtypeStruct((depth, nb, W_C, s), F32),
        jax.ShapeDtypeStruct((depth, H_A, t), F32)]
    out_specs = [
        pl.BlockSpec((GROUP_W, tm), lambda b, i: (0, b * nt + i)), row(GROUP_W),
        pl.BlockSpec((1, per, GROUP_W, ATT_TILE), lambda b, i: (b, i, 0, 0)),
        row(GROUP_W), row(GROUP_W), row(GROUP_W), row(GROUP_W), row(GROUP_W), row(GROUP_W), row(LANES),
        feat(W_A), feat(W_A), band(W_B), band(W_B), feat(W_C), feat(W_C),
        pl.BlockSpec((1, H_A, tm), lambda b, i: (layer, 0, b * nt + i))]
    ins = [x, lw['norm_mix'], lw['w_in'], lw['gq'], lw['gk'], lw['b_f']]
    in_specs = [row(D_MODEL)] + [full(a) for a in ins[1:]]
    aliases = {}
    n_alias = 0
    if prev_rows is not None:
        n_alias = N_PROMPT_ROWS
        first_row_out = len(out_shape) - N_PROMPT_ROWS
        for j, a in enumerate(prev_rows):
            aliases[len(ins)] = first_row_out + j
            ins.append(a)
            in_specs.append(pl.BlockSpec(memory_space=pl.ANY))
    return pl.pallas_call(
        functools.partial(_inproj_prompt_kernel, n_alias=n_alias), grid=(nb, nt),
        in_specs=in_specs, out_specs=out_specs, out_shape=out_shape,
        input_output_aliases=aliases,
        compiler_params=pltpu.CompilerParams(dimension_semantics=("parallel", "arbitrary"),
                                             vmem_limit_bytes=PROJ_VMEM_LIMIT),
        name="inproj_prompt")(*ins)


def _forget_aug_kernel(logf_ref, kaug_ref, qaug_ref, *, nblk):
    tb = ATT_TILE
    r = lax.broadcasted_iota(jnp.int32, (tb, tb), 0)
    c = lax.broadcasted_iota(jnp.int32, (tb, tb), 1)
    tri = jnp.where(c <= r, 1.0, 0.0).astype(BF16)
    src = lax.broadcasted_iota(jnp.int32, (LANES, LANES), 0)
    dst = lax.broadcasted_iota(jnp.int32, (LANES, LANES), 1)

    def place(slot):
        return jnp.where((dst == AUG_SLOTS * src + slot) & (src < H_A), 1.0, 0.0).astype(BF16)

    lane = _lane_iota()
    slot = jnp.bitwise_and(lane, AUG_SLOTS - 1)
    used = lane < AUG_SLOTS * H_A
    key_const = jnp.where(used & (slot >= 3) & (slot < 6), 1.0, 0.0)
    qry_const = jnp.where(used & (slot < 3), -1.0, 0.0)
    carry = jnp.zeros((1, LANES), F32)
    for blk in range(nblk):
        rows = slice(blk * tb, (blk + 1) * tb)
        hi, mid, lo = _split3(logf_ref[rows, :])
        cs = (_dot(tri, hi) + _dot(tri, mid) + _dot(tri, lo)) + carry
        carry = cs[tb - 1:tb, :]
        hi, mid, lo = _split3(cs * LOG2E)
        kaug = _dot(hi, place(0)) + _dot(mid, place(1)) + _dot(lo, place(2)) + key_const
        kaug_ref[rows, :] = kaug.astype(BF16)
        qaug = _dot(hi, place(3)) + _dot(mid, place(4)) + _dot(lo, place(5)) + qry_const
        qaug_ref[:, rows] = jnp.transpose(qaug).astype(BF16)


def _forget_aug(logf, nb, s):
    return pl.pallas_call(
        functools.partial(_forget_aug_kernel, nblk=s // ATT_TILE), grid=(nb,),
        in_specs=[pl.BlockSpec((s, LANES), lambda b: (b, 0))],
        out_specs=[pl.BlockSpec((s, LANES), lambda b: (b, 0)), pl.BlockSpec((LANES, s), lambda b: (0, b))],
        out_shape=[jax.ShapeDtypeStruct((nb * s, LANES), BF16), jax.ShapeDtypeStruct((LANES, nb * s), BF16)],
        compiler_params=_cparams("parallel"), name="forget_aug")(logf)


def _suffix_sum_kernel(x_ref, o_ref):
    p = x_ref.shape[1]
    j = lax.broadcasted_iota(jnp.int32, (p, p), 0)
    s = lax.broadcasted_iota(jnp.int32, (p, p), 1)
    tri = jnp.where(j > s, 1.0, 0.0).astype(BF16)
    hi, mid, lo = _split3(x_ref[...])
    o_ref[...] = _dot(hi, tri) + _dot(mid, tri) + _dot(lo, tri)


def _suffix_sum(x):
    return pl.pallas_call(
        _suffix_sum_kernel, out_shape=jax.ShapeDtypeStruct(x.shape, F32),
        compiler_params=pltpu.CompilerParams(vmem_limit_bytes=VMEM_LIMIT), name="suffix_sum")(x)


def _attn_a_kernel(qt_ref, k_ref, vt_ref, kaug_ref, qaug_ref, o_ref, qp_ref, m_ref, l_ref, acc_ref):
    tq = qt_ref.shape[1]
    tk = ATT_TILE
    qi = pl.program_id(1)
    row = lax.broadcasted_iota(jnp.int32, (PAIR_W, 1), 0)
    aug = qaug_ref[...]
    for h in range(H_A):
        q2 = qt_ref[_pair_slice(h), :]
        half = (row < HEAD_DIM) if h % 2 == 0 else (row >= HEAD_DIM)
        qp_ref[h, :PAIR_W, :] = jnp.where(half, q2, jnp.zeros_like(q2))
        mine = (row >= AUG_SLOTS * h) & (row < AUG_SLOTS * (h + 1))
        qp_ref[h, PAIR_W:, :] = jnp.where(mine, aug, jnp.zeros_like(aug))
    m_ref[...] = jnp.full(m_ref.shape, NEG_INF, F32)
    l_ref[...] = jnp.zeros(l_ref.shape, F32)
    acc_ref[...] = jnp.zeros(acc_ref.shape, F32)

    def step(kt, key_offset):
        ks = pl.multiple_of(kt * tk, tk)
        ka = kaug_ref[pl.ds(ks, tk), :]
        tile_scores = [_dot(jnp.concatenate([k_ref[pl.ds(ks, tk), _pair_slice(h)], ka], axis=1), qp_ref[h])
                       for h in range(H_A)]
        probs, alphas = [], []
        for h in range(H_A):
            s = tile_scores[h]
            if key_offset is not None:
                key = lax.broadcasted_iota(jnp.int32, (tk, tq), 0)
                qry = lax.broadcasted_iota(jnp.int32, (tk, tq), 1)
                s = jnp.where(key + key_offset <= qry, s, NEG_INF)
            m_prev = m_ref[h]
            m_new = jnp.maximum(m_prev, jnp.max(s, axis=0, keepdims=True))
            alpha = jnp.exp2(m_prev - m_new)
            p = jnp.exp2(s - m_new)
            l_ref[h] = alpha * l_ref[h] + jnp.sum(p, axis=0, keepdims=True)
            m_ref[h] = m_new
            alphas.append(alpha)
            probs.append(p.astype(BF16))
        for h in range(H_A):
            acc_ref[h] = alphas[h] * acc_ref[h] + _dot(vt_ref[0, kt, _pair_slice(h), :], probs[h])

    def body(kt, carry):
        step(kt, None)
        return carry

    per = tq // tk
    lax.fori_loop(0, qi * per, body, 0)
    for j in range(per):
        step(qi * per + j, j * tk)
    for pair in range(N_PAIRS):
        h = 2 * pair
        out_t = jnp.where(row < HEAD_DIM, acc_ref[h] * (1.0 / l_ref[h]), acc_ref[h + 1] * (1.0 / l_ref[h + 1]))
        o_ref[:, pair * PAIR_W:(pair + 1) * PAIR_W] = jnp.transpose(out_t)


def _attn_a(qat, ka, vat, kaug, qaug, nb, s):
    tq = A_QTILE
    nq = s // tq
    return pl.pallas_call(
        _attn_a_kernel, grid=(nb, nq),
        in_specs=[pl.BlockSpec((GROUP_W, tq), lambda b, i: (0, b * nq + i)),
                  pl.BlockSpec((s, GROUP_W), lambda b, i: (b, 0)),
                  pl.BlockSpec((1, s // ATT_TILE, GROUP_W, ATT_TILE), lambda b, i: (b, 0, 0, 0)),
                  pl.BlockSpec((s, LANES), lambda b, i: (b, 0)),
                  pl.BlockSpec((LANES, tq), lambda b, i: (0, b * nq + i))],
        out_specs=pl.BlockSpec((tq, GROUP_W), lambda b, i: (b * nq + i, 0)),
        out_shape=jax.ShapeDtypeStruct((nb * s, GROUP_W), F32),
        scratch_shapes=[pltpu.VMEM((H_A, 2 * PAIR_W, tq), BF16), pltpu.VMEM((H_A, 1, tq), F32),
                        pltpu.VMEM((H_A, 1, tq), F32), pltpu.VMEM((H_A, PAIR_W, tq), F32)],
        compiler_params=_cparams("parallel", "arbitrary"), name="attn_a")(qat, ka, vat, kaug, qaug)


def _masked_q(q_ref, qm_ref, n_heads):
    for h in range(n_heads):
        q2 = q_ref[:, _pair_slice(h)]
        qm_ref[h] = jnp.where(_half_mask(h % 2), q2, jnp.zeros_like(q2))


def _attn_b_kernel(q_ref, k_ref, v_ref, tab_ref, o_ref):
    tq = ATT_TILE
    win = tq + BAND
    i = pl.program_id(1)
    var = jnp.minimum(i, BAND // tq)
    ws = pl.multiple_of(jnp.maximum(i * tq - BAND, 0), tq)
    heads = range(H_B)
    scores = []
    for h in heads:
        q2 = q_ref[:, _pair_slice(h)]
        qm = jnp.where(_half_mask(h % 2), q2, jnp.zeros_like(q2))
        scores.append(_dot_nt(qm, k_ref[pl.ds(ws, win), _pair_slice(h)]) + tab_ref[var, h])
    probs, norms = [], []
    for h in heads:
        p = jnp.exp2(scores[h] - jnp.max(scores[h], axis=-1, keepdims=True))
        norms.append(1.0 / jnp.sum(p, axis=-1, keepdims=True))
        probs.append(p.astype(BF16))
    res = [_dot(probs[h], v_ref[pl.ds(ws, win), _pair_slice(h)]) * norms[h] for h in heads]
    res.append(jnp.zeros((tq, PAIR_W), F32))
    first = _half_mask(0)
    for pair in range(N_PAIRS):
        o_ref[:, pair * PAIR_W:(pair + 1) * PAIR_W] = jnp.where(first, res[2 * pair], res[2 * pair + 1])


def _suffix_tri(n):
    j = lax.broadcasted_iota(jnp.int32, (2 * n, n), 0)
    s = lax.broadcasted_iota(jnp.int32, (2 * n, n), 1)
    return jnp.where(jnp.where(j >= n, j - n, j) >= s, 1.0, 0.0).astype(BF16)


def _stick_scores(z, tri, seen):
    sp = _softplus2(z)
    if seen is not None:
        sp = jnp.where(seen, sp, 0.0)
    hi, lo = _split2(sp)
    n = sp.shape[1]
    if n % LANES == 0:
        return _dot(jnp.concatenate([hi, lo], axis=1), tri)
    return _dot(hi, tri[:n]) + _dot(lo, tri[:n])


def _stick_weights(z, s_in, carry, seen):
    a = jnp.exp2(z - (s_in + carry))
    if seen is not None:
        a = jnp.where(seen, a, 0.0)
    return a.astype(BF16)


def _attn_c_kernel(q_ref, k_ref, v_ref, o_ref, qm_ref, carry_ref, acc_ref):
    tq = q_ref.shape[0]
    tk = ATT_TILE
    qi = pl.program_id(1)
    _masked_q(q_ref, qm_ref, H_C)
    carry_ref[...] = jnp.zeros(carry_ref.shape, F32)
    acc_ref[...] = jnp.zeros(acc_ref.shape, F32)
    heads = range(H_C)

    def step(kt, key_offset):
        ks = pl.multiple_of(kt * tk, tk)
        zs = [_dot_nt(qm_ref[h], k_ref[pl.ds(ks, tk), _pair_slice(h)]) for h in heads]
        tri = _suffix_tri(tk)
        seen = None
        if key_offset is not None:
            r = lax.broadcasted_iota(jnp.int32, (tq, tk), 0)
            c = lax.broadcasted_iota(jnp.int32, (tq, tk), 1)
            seen = c + key_offset < r
        sums = [_stick_scores(zs[h], tri, seen) for h in heads]
        weights = []
        for h in heads:
            carry = carry_ref[h][:, 0:1]
            weights.append(_stick_weights(zs[h], sums[h], carry, seen))
            carry_ref[h] = jnp.broadcast_to(carry + sums[h][:, 0:1], (tq, LANES))
        for h in heads:
            acc_ref[h] += _dot(weights[h], v_ref[pl.ds(ks, tk), _pair_slice(h)])

    per = tq // tk

    def body(j, c):
        step(qi * per - 1 - j, None)
        return c

    for j in reversed(range(per)):
        step(qi * per + j, j * tk)
    lax.fori_loop(0, qi * per, body, 0)
    first = _half_mask(0)
    for pair in range(N_PAIRS):
        h = 2 * pair
        second = acc_ref[h + 1] if h + 1 < H_C else jnp.zeros((tq, PAIR_W), F32)
        o_ref[:, pair * PAIR_W:(pair + 1) * PAIR_W] = jnp.where(first, acc_ref[h], second)


def _prompt_attention(kernel, name, q, k, v, extra, extra_specs, scratch, nb, s, tq):
    nq = s // tq
    qspec = pl.BlockSpec((tq, GROUP_W), lambda b, i: (b * nq + i, 0))
    kvspec = pl.BlockSpec((s, GROUP_W), lambda b, i: (b, 0))
    return pl.pallas_call(
        kernel, grid=(nb, nq),
        in_specs=[qspec, kvspec, kvspec] + extra_specs,
        out_specs=qspec, out_shape=jax.ShapeDtypeStruct((nb * s, GROUP_W), F32),
        scratch_shapes=scratch,
        compiler_params=_cparams("parallel", "arbitrary"), name=name)(q, k, v, *extra)


def _sample_heads(n_heads, q_ref, width):
    out = []
    for h in range(n_heads):
        lo = (h // 2) * PAIR_W
        span = slice(lo, min(lo + PAIR_W, width))
        q2 = q_ref[:, span]
        if span.stop - span.start == PAIR_W:
            q2 = jnp.where(_half_mask(h % 2), q2, jnp.zeros_like(q2))
        out.append((h, span, q2))
    return out


def _store_heads(o_ref, res, n_heads):
    for pair in range(N_PAIRS):
        lo = pair * PAIR_W
        h = 2 * pair
        if h + 1 < n_heads:
            o_ref[:, lo:lo + PAIR_W] = jnp.where(_half_mask(0), res[h], res[h + 1])
        else:
            o_ref[:, lo:lo + HEAD_DIM] = res[h]
            o_ref[:, lo + HEAD_DIM:lo + PAIR_W] = jnp.zeros_like(res[h])


def _joint_softmax_pv(s_c, s_n, vt_c, v_n):
    m = jnp.maximum(jnp.max(s_c, axis=-1, keepdims=True), jnp.max(s_n, axis=-1, keepdims=True))
    p_c = jnp.exp2(s_c - m)
    p_n = jnp.exp2(s_n - m)
    l = jnp.sum(p_c, axis=-1, keepdims=True) + jnp.sum(p_n, axis=-1, keepdims=True)
    return (_dot_nt(p_c.astype(BF16), vt_c) + _dot(p_n.astype(BF16), v_n)) / l


def _sample_attn_kernel(qa_ref, qb_ref, qc_ref, kan_ref, kbn_ref, kcn_ref, van_ref, vbn_ref, vcn_ref,
                        cak_ref, cav_ref, cbk_ref, cbv_ref, cck_ref, ccv_ref,
                        rsum_ref, logf_ref, tabc_ref, tabn_ref,
                        oa_ref, ob_ref, oc_ref):
    t = qa_ref.shape[0]
    past = cak_ref.shape[3]
    r = lax.broadcasted_iota(jnp.int32, (t, t), 0)
    c = lax.broadcasted_iota(jnp.int32, (t, t), 1)

    ltri = jnp.where(c <= r, 1.0, 0.0).astype(BF16)
    hi, mid, lo = _split3(logf_ref[...])
    pcol = (_dot(ltri, hi) + _dot(ltri, mid) + _dot(ltri, lo)) * LOG2E
    prow = jnp.transpose(jnp.concatenate([pcol, jnp.zeros((LANES - t, LANES), F32)], axis=0))
    heads = _sample_heads(H_A, qa_ref, W_A)
    kts = {sp.start: cak_ref[0, 0, sp, :].astype(BF16) for _, sp, _ in heads}
    vts = {sp.start: cav_ref[0, 0, sp, :].astype(BF16) for _, sp, _ in heads}
    s_c = [_dot(qm, kts[sp.start]) + (pcol[:, h:h + 1] + rsum_ref[0, 0, h:h + 1, :] * LOG2E)
           for h, sp, qm in heads]
    s_n = [jnp.where(c <= r, _dot_nt(qm, kan_ref[:, sp]) + (pcol[:, h:h + 1] - prow[h:h + 1, 0:t]), NEG_INF)
           for h, sp, qm in heads]
    _store_heads(oa_ref, [_joint_softmax_pv(s_c[h], s_n[h], vts[sp.start], van_ref[:, sp])
                          for h, sp, _ in heads], H_A)

    heads = _sample_heads(H_B, qb_ref, W_B)
    kts = {sp.start: cbk_ref[0, 0, sp, :].astype(BF16) for _, sp, _ in heads}
    vts = {sp.start: cbv_ref[0, 0, sp, :].astype(BF16) for _, sp, _ in heads}
    s_c = [_dot(qm, kts[sp.start]) + tabc_ref[h] for h, sp, qm in heads]
    s_n = [_dot_nt(qm, kbn_ref[:, sp]) + tabn_ref[h] for h, sp, qm in heads]
    _store_heads(ob_ref, [_joint_softmax_pv(s_c[h], s_n[h], vts[sp.start], vbn_ref[:, sp])
                          for h, sp, _ in heads], H_B)

    heads = _sample_heads(H_C, qc_ref, W_C)
    nblk = past // ATT_TILE
    tri_n = _suffix_tri(t)
    tri_c = _suffix_tri(ATT_TILE)
    seen = c < r
    kts = {sp.start: cck_ref[0, 0, sp, :].astype(BF16) for _, sp, _ in heads}
    vts = {sp.start: ccv_ref[0, 0, sp, :].astype(BF16) for _, sp, _ in heads}
    z_n = [_dot_nt(qm, kcn_ref[:, sp]) for _, sp, qm in heads]
    z_c = [_dot(qm, kts[sp.start]) for _, sp, qm in heads]
    sum_n = [_stick_scores(z, tri_n, seen) for z in z_n]
    sum_c = [[_stick_scores(z[:, b * ATT_TILE:(b + 1) * ATT_TILE], tri_c, None) for b in range(nblk)]
             for z in z_c]
    res = []
    for h, sp, _ in heads:
        acc = _dot(_stick_weights(z_n[h], sum_n[h], 0.0, seen), vcn_ref[:, sp])
        carry = sum_n[h][:, 0:1]
        for b in reversed(range(nblk)):
            cols = slice(b * ATT_TILE, (b + 1) * ATT_TILE)
            a = _stick_weights(z_c[h][:, cols], sum_c[h][b], carry, None)
            acc = acc + _dot_nt(a, vts[sp.start][:, cols])
            carry = carry + sum_c[h][b][:, 0:1]
        res.append(acc)
    _store_heads(oc_ref, res, H_C)


def _sample_attention(proj, caches, layer, rsum, tabc, tabn, nb, t):
    qa, qb, qc, ka, kb, kc, va, vb, vc = proj[:9]
    logf = proj[15]
    new = pl.BlockSpec((t, GROUP_W), lambda b: (b, 0))
    cache = lambda a: pl.BlockSpec((1, 1) + a.shape[2:], lambda b: (layer, b, 0, 0))
    full = lambda a: pl.BlockSpec(a.shape, lambda b: (0,) * a.ndim)
    return pl.pallas_call(
        _sample_attn_kernel, grid=(nb,),
        in_specs=[new] * 9 + [cache(a) for a in caches] + [
            cache(rsum), pl.BlockSpec((t, LANES), lambda b: (b, 0)), full(tabc), full(tabn)],
        out_specs=[new] * 3,
        out_shape=[jax.ShapeDtypeStruct((nb * t, GROUP_W), F32)] * 3,
        compiler_params=_cparams("parallel"), name="sample_attn")(
            qa, qb, qc, ka, kb, kc, va, vb, vc, *caches, rsum, logf, tabc, tabn)


def _masked_max(x, mask):
    return jnp.max(jnp.where(mask, x, -jnp.inf), axis=-1, keepdims=True)


def _first_lane(mask, lane):
    return jnp.min(jnp.where(mask, lane, float(LANES)), axis=-1, keepdims=True)


def _route(logits):
    lane = _lane_iota().astype(F32)
    is_g = (lane >= N_EXPERTS) & (lane < N_EXPERTS + N_GROUPS)
    gmax = _masked_max(logits, is_g)
    p_g = 1.0 / jnp.sum(jnp.where(is_g, jnp.exp(logits - gmax), 0.0), axis=-1, keepdims=True)
    g_sel = _first_lane(is_g & (logits == gmax), lane) - N_EXPERTS
    lo = g_sel * EXPERTS_PER_GROUP
    in_g = (lane >= lo) & (lane < lo + EXPERTS_PER_GROUP)
    l1 = _masked_max(logits, in_g)
    i1 = _first_lane(in_g & (logits == l1), lane)
    rest = in_g & (lane != i1)
    l2 = _masked_max(logits, rest)
    i2 = _first_lane(rest & (logits == l2), lane)
    e2 = jnp.exp(l2 - l1)
    w1 = p_g / (1.0 + e2)
    w2 = p_g * e2 / (1.0 + e2)
    gates = jnp.where(lane == i1, w1, jnp.where(lane == i2, w2, 0.0))
    return jnp.where(lane == g_sel + N_EXPERTS, 1.0, gates)


def _post_kernel(oa_ref, ob_ref, oc_ref, h_ref, gmix_ref, wout_ref, nffn_ref, wrh_ref, wrl_ref, br_ref,
                 h1_ref, xn_ref, gates_ref, *xchunk_refs):
    def gnorm(o, width):
        ms = jnp.sum(o * o, axis=-1, keepdims=True) * (1.0 / width)
        return o * lax.rsqrt(ms + EPS)

    h1 = h_ref[...]
    for g, (o_ref, width) in enumerate(((oa_ref, W_A), (ob_ref, W_B), (oc_ref, W_C))):
        merged = (gnorm(o_ref[...], width) * gmix_ref[g]).astype(BF16)
        h1 = h1 + _dot(merged, wout_ref[g])
    h1_ref[...] = h1
    xf = _rms(h1, nffn_ref[...])
    hi, lo = _split2(xf)
    xn_ref[...] = hi
    bits = pltpu.bitcast(hi.astype(F32), jnp.uint32)
    for c, ref in enumerate(xchunk_refs):
        a = bits[:, (2 * c) * SC_ROW_WORDS:(2 * c + 1) * SC_ROW_WORDS]
        b = bits[:, (2 * c + 1) * SC_ROW_WORDS:(2 * c + 2) * SC_ROW_WORDS]
        ref[...] = jnp.bitwise_or(a, jnp.right_shift(b, jnp.uint32(16)))
    logits = _dot(hi, wrh_ref[...]) + _dot(lo, wrh_ref[...]) + _dot(hi, wrl_ref[...]) + br_ref[...]
    gates_ref[...] = _route(logits)


def _post(oa, ob, oc, h, lw, tm, n_chunks=0):
    t = h.shape[0]
    row = lambda w: pl.BlockSpec((tm, w), lambda i: (i, 0))
    full = lambda a: pl.BlockSpec(a.shape, lambda i: (0,) * a.ndim)
    ws = (lw['g_mix'], lw['w_out'], lw['norm_ffn'], lw['w_r_hi'], lw['w_r_lo'], lw['b_r'])
    return pl.pallas_call(
        _post_kernel, grid=(t // tm,),
        in_specs=[row(GROUP_W)] * 3 + [row(D_MODEL)] + [full(a) for a in ws],
        out_specs=[row(D_MODEL), row(D_MODEL), row(LANES)] + [row(SC_ROW_WORDS)] * n_chunks,
        out_shape=[jax.ShapeDtypeStruct((t, D_MODEL), F32), jax.ShapeDtypeStruct((t, D_MODEL), BF16),
                   jax.ShapeDtypeStruct((t, LANES), F32)]
        + [jax.ShapeDtypeStruct((t, SC_ROW_WORDS), jnp.uint32)] * n_chunks,
        compiler_params=_cparams("parallel"), name="post")(oa, ob, oc, h, *ws)


def _moe_kernel(xn_ref, gates_ref, wg_ref, wu_ref, wd_ref, h1_ref, p_ref, wple_ref, nple_ref, wpg_ref,
                out_ref, acc_ref):
    step = pl.program_id(1)

    @pl.when(step == 0)
    def _():
        acc_ref[...] = jnp.zeros_like(acc_ref)

    x = xn_ref[...]
    gates = gates_ref[...]
    acts = []
    for k in range(EXPERTS_PER_STEP):
        e = step * EXPERTS_PER_STEP + k
        g = _dot(x, wg_ref[k])
        u = _dot(x, wu_ref[k])
        gate = jnp.sum(jnp.where(_lane_iota() == e, gates, 0.0), axis=-1, keepdims=True)
        acts.append(((g * _sigmoid(g)) * u * gate).astype(BF16))
    acc_ref[...] += _dot(jnp.concatenate(acts, axis=1), wd_ref[...])

    @pl.when(step == pl.num_programs(1) - 1)
    def _():
        h2 = h1_ref[...] + acc_ref[...]
        gate_ple = _sigmoid(_dot(_rms(h2, nple_ref[...]).astype(BF16), wpg_ref[...]))
        out_ref[...] = h2 + _dot(p_ref[0].astype(BF16), wple_ref[...]) * gate_ple


def _moe(xn, gates, h1, p_all, layer, lw, tm):
    t = xn.shape[0]
    row = lambda w: pl.BlockSpec((tm, w), lambda i, e: (i, 0))
    once = pl.Buffered(1)
    full = lambda a: pl.BlockSpec(a.shape, lambda i, e: (0,) * a.ndim, pipeline_mode=once)
    exp = lambda a: pl.BlockSpec((EXPERTS_PER_STEP,) + a.shape[1:], lambda i, e: (e, 0, 0))
    d_hidden = lw['w_ed'].shape[0] // N_EXPERTS
    return pl.pallas_call(
        _moe_kernel, grid=(t // tm, N_EXPERTS // EXPERTS_PER_STEP),
        in_specs=[row(D_MODEL), row(LANES), exp(lw['w_eg']), exp(lw['w_eu']),
                  pl.BlockSpec((EXPERTS_PER_STEP * d_hidden, D_MODEL), lambda i, e: (e, 0)),
                  pl.BlockSpec((tm, D_MODEL), lambda i, e: (i, 0), pipeline_mode=once),
                  pl.BlockSpec((1, tm, p_all.shape[2]), lambda i, e: (layer, i, 0), pipeline_mode=once),
                  full(lw['w_ple']), full(lw['norm_ple']), full(lw['w_pg'])],
        out_specs=row(D_MODEL), out_shape=jax.ShapeDtypeStruct((t, D_MODEL), F32),
        scratch_shapes=[pltpu.VMEM((tm, D_MODEL), F32)],
        compiler_params=pltpu.CompilerParams(dimension_semantics=("parallel", "arbitrary"),
                                             vmem_limit_bytes=MOE_VMEM_LIMIT), name="moe")(
            xn, gates, lw['w_eg'], lw['w_eu'], lw['w_ed'], h1, p_all, lw['w_ple'], lw['norm_ple'], lw['w_pg'])


SORT_TILE = 512
SC_WINDOW = 128
SC_ROW_WORDS = 256
N_CHUNKS = D_MODEL // SC_ROW_WORDS
N_XCHUNKS = N_CHUNKS // 2


def _plan_kernel(gates_ref, dest_ref, total_ref, *, tb):
    lane = _lane_iota()
    is_g = (lane >= N_EXPERTS) & (lane < N_EXPERTS + N_GROUPS)
    total = jnp.sum(jnp.where(is_g, gates_ref[...], 0.0), axis=0, keepdims=True)
    padded = jnp.floor((total + (SORT_TILE - 1)) * (1.0 / SORT_TILE)) * SORT_TILE
    base = jnp.zeros((1, LANES), F32)
    for g in range(1, N_GROUPS):
        before = jnp.sum(jnp.where(lane < N_EXPERTS + g, padded, 0.0), axis=-1, keepdims=True)
        base = jnp.where(lane == N_EXPERTS + g, before, base)
    r = lax.broadcasted_iota(jnp.int32, (tb, tb), 0)
    c = lax.broadcasted_iota(jnp.int32, (tb, tb), 1)
    ltri = jnp.where(c < r, 1.0, 0.0).astype(BF16)
    carry = jnp.zeros((1, LANES), F32)
    for blk in range(gates_ref.shape[0] // tb):
        rows = slice(blk * tb, (blk + 1) * tb)
        member = jnp.where(is_g, gates_ref[rows, :], 0.0)
        rank = _dot(ltri, member.astype(BF16)) + carry
        carry = carry + jnp.sum(member, axis=0, keepdims=True)
        dest = jnp.sum(member * (base + rank), axis=-1, keepdims=True)
        dest_ref[rows, :] = jnp.broadcast_to(dest, (tb, LANES))
    total_ref[...] = jnp.broadcast_to(total, total_ref.shape)


def _plan(gates):
    t = gates.shape[0]
    return pl.pallas_call(
        functools.partial(_plan_kernel, tb=SORT_TILE),
        out_shape=[jax.ShapeDtypeStruct((t, LANES), F32), jax.ShapeDtypeStruct((SUBLANES, LANES), F32)],
        compiler_params=pltpu.CompilerParams(vmem_limit_bytes=VMEM_LIMIT), name="moe_plan")(gates)


def _sc_scatter_rows(x, idx, n_out):
    n, d = x.shape
    mesh = plsc.VectorSubcoreMesh(core_axis_name="core", subcore_axis_name="subcore")

    @functools.partial(pl.kernel, out_type=jax.ShapeDtypeStruct((n_out, d), x.dtype), mesh=mesh, scratch_types=[])
    def scatter(x_hbm, i_hbm, o_hbm):
        def body(x_vmem, i_vmem):
            pltpu.sync_copy(x_vmem, o_hbm.at[i_vmem.at[0]])

        pltpu.emit_pipeline(
            body, grid=(n // SC_WINDOW,),
            in_specs=[pl.BlockSpec((SC_WINDOW, d), index_map=lambda i: (i, 0)),
                      pl.BlockSpec((1, SC_WINDOW), index_map=lambda i: (0, i))],
            out_specs=[], core_axis_name=('core', 'subcore'), dimension_semantics=(pltpu.PARALLEL,))(x_hbm, i_hbm)

    return scatter(x, idx.reshape(1, n))


def _sc_gather_rows(x, idx):
    n = idx.shape[0]
    d = x.shape[1]
    mesh = plsc.VectorSubcoreMesh(core_axis_name="core", subcore_axis_name="subcore")

    @functools.partial(pl.kernel, out_type=jax.ShapeDtypeStruct((n, d), x.dtype), mesh=mesh)
    def gather(x_hbm, i_hbm, o_hbm):
        def body(i_vmem, o_vmem):
            pltpu.sync_copy(x_hbm.at[i_vmem.at[0]], o_vmem)

        pltpu.emit_pipeline(
            body, grid=(n // SC_WINDOW,),
            in_specs=[pl.BlockSpec((1, SC_WINDOW), index_map=lambda i: (0, i))],
            out_specs=[pl.BlockSpec((SC_WINDOW, d), index_map=lambda i: (i, 0))],
            core_axis_name=('core', 'subcore'), dimension_semantics=(pltpu.PARALLEL,))(i_hbm, o_hbm)

    return gather(x, idx.reshape(1, n))


def _experts_sorted_kernel(tg_ref, tv_ref, *refs):
    xs_refs, (gs_ref, wg_ref, wu_ref, wd_ref), ys_refs = (
        refs[:N_XCHUNKS], refs[N_XCHUNKS:N_XCHUNKS + 4], refs[-N_CHUNKS:])
    j = pl.program_id(0)

    @pl.when(tv_ref[j] > 0)
    def _():
        g = tg_ref[j]
        halves = []
        for r in xs_refs:
            w = r[...]
            halves.append(pltpu.bitcast(jnp.bitwise_and(w, jnp.uint32(0xFFFF0000)), F32))
            halves.append(pltpu.bitcast(jnp.left_shift(w, jnp.uint32(16)), F32))
        x = jnp.concatenate(halves, axis=1).astype(BF16)
        src = lax.broadcasted_iota(jnp.int32, (LANES, D_MODEL), 0)
        dst = lax.broadcasted_iota(jnp.int32, (LANES, D_MODEL), 1)
        d_shift = (D_MODEL // EXPERTS_PER_GROUP).bit_length() - 1
        expand = jnp.where(src == g * EXPERTS_PER_GROUP + jnp.right_shift(dst, d_shift), 1.0, 0.0).astype(BF16)
        hi, mid, lo = _split3(gs_ref[...])
        ge = _dot(hi, expand) + _dot(mid, expand) + _dot(lo, expand)
        hg = _dot(x, wg_ref[0])
        act = (hg * _sigmoid(hg)) * _dot(x, wu_ref[0]) * ge
        y = _dot(act.astype(BF16), wd_ref[0])
        for c, ref in enumerate(ys_refs):
            ref[...] = y[:, c * SC_ROW_WORDS:(c + 1) * SC_ROW_WORDS]


def _experts_sorted(xs_chunks, gs, tile_group, tile_valid, lw):
    n_rows = gs.shape[0]
    row = lambda w: pl.BlockSpec((SORT_TILE, w), lambda j, tg, tv: (j, 0))
    grp = lambda a: pl.BlockSpec((1,) + a.shape[1:], lambda j, tg, tv: (tg[j], 0, 0))
    return pl.pallas_call(
        _experts_sorted_kernel,
        grid_spec=pltpu.PrefetchScalarGridSpec(
            num_scalar_prefetch=2, grid=(n_rows // SORT_TILE,),
            in_specs=[row(SC_ROW_WORDS)] * N_XCHUNKS + [row(LANES), grp(lw['w_gg']), grp(lw['w_gu']), grp(lw['w_gd'])],
            out_specs=[row(SC_ROW_WORDS)] * N_CHUNKS),
        out_shape=[jax.ShapeDtypeStruct((n_rows, SC_ROW_WORDS), F32)] * N_CHUNKS,
        compiler_params=_cparams("arbitrary"), name="moe_sorted")(
            tile_group, tile_valid, *xs_chunks, gs, lw['w_gg'], lw['w_gu'], lw['w_gd'])


def _ple_kernel(h1_ref, *refs):
    y_refs, (p_ref, wple_ref, nple_ref, wpg_ref, out_ref) = refs[:N_CHUNKS], refs[N_CHUNKS:]
    h2 = h1_ref[...] + jnp.concatenate([r[...] for r in y_refs], axis=1)
    gate_ple = _sigmoid(_dot(_rms(h2, nple_ref[...]).astype(BF16), wpg_ref[...]))
    out_ref[...] = h2 + _dot(p_ref[0].astype(BF16), wple_ref[...]) * gate_ple


def _ple(h1, y_chunks, p_all, layer, lw, tm):
    t = h1.shape[0]
    row = lambda w: pl.BlockSpec((tm, w), lambda i: (i, 0))
    full = lambda a: pl.BlockSpec(a.shape, lambda i: (0,) * a.ndim)
    return pl.pallas_call(
        _ple_kernel, grid=(t // tm,),
        in_specs=[row(D_MODEL)] + [row(SC_ROW_WORDS)] * N_CHUNKS
        + [pl.BlockSpec((1, tm, p_all.shape[2]), lambda i: (layer, i, 0)),
           full(lw['w_ple']), full(lw['norm_ple']), full(lw['w_pg'])],
        out_specs=row(D_MODEL), out_shape=jax.ShapeDtypeStruct((t, D_MODEL), F32),
        compiler_params=_cparams("parallel"), name="ple")(
            h1, *y_chunks, p_all, lw['w_ple'], lw['norm_ple'], lw['w_pg'])


def _moe_dispatch(x_chunks, gates):
    t = gates.shape[0]
    n_rows = t + N_GROUPS * SORT_TILE
    dest_rep, total = _plan(gates)
    dest = dest_rep[:, 0].astype(jnp.int32)
    padded = (total[0, N_EXPERTS:N_EXPERTS + N_GROUPS].astype(jnp.int32) + (SORT_TILE - 1)) // SORT_TILE * SORT_TILE
    ends = jnp.cumsum(padded)
    tile_start = jnp.arange(n_rows // SORT_TILE, dtype=jnp.int32) * SORT_TILE
    tile_group = jnp.minimum(jnp.sum(tile_start[:, None] >= ends[None, :], axis=1), N_GROUPS - 1).astype(jnp.int32)
    tile_valid = (tile_start < ends[-1]).astype(jnp.int32)
    xs_chunks = [_sc_scatter_rows(xc, dest, n_rows) for xc in x_chunks]
    gs = _sc_scatter_rows(gates, dest, n_rows)
    return xs_chunks, gs, tile_group, tile_valid, dest


def _moe_combine(dispatched, h1, p_all, layer, lw):
    xs_chunks, gs, tile_group, tile_valid, dest = dispatched
    ys_chunks = _experts_sorted(xs_chunks, gs, tile_group, tile_valid, lw)
    y_chunks = [_sc_gather_rows(yc, dest) for yc in ys_chunks]
    return _ple(h1, y_chunks, p_all, layer, lw, _token_tile(h1.shape[0], 1024))


def _regroup_out_rows(wt):
    z = jnp.zeros((HEAD_DIM, wt.shape[1]), wt.dtype)
    return jnp.concatenate([wt[:W_A], wt[W_A:W_A + W_B], z, wt[W_A + W_B:], z], axis=0)


def _regroup_rows(w):
    z = jnp.zeros((HEAD_DIM, w.shape[1]), w.dtype)
    return jnp.stack([w[:W_A], jnp.concatenate([w[W_A:W_A + W_B], z], axis=0),
                      jnp.concatenate([w[W_A + W_B:], z], axis=0)])


def _pad_lanes(a, n=LANES):
    return jnp.pad(a, [(0, 0)] * (a.ndim - 1) + [(0, n - a.shape[-1])])


def _toeplitz_bias(rel_bias, off, tq, win):
    length = tq + win
    pad = length + abs(off)
    ext = jnp.flip(jnp.pad(rel_bias, ((0, 0), (pad, pad)), mode='edge'), axis=1)
    s1 = ext.shape[1] - 1 - (off + REL_CLIP + pad)
    v = jnp.concatenate([ext[:, s1:s1 + win], ext[:, s1 - tq:s1]], axis=1)
    flat = jnp.tile(v, (1, tq))[:, :tq * (length - 1)]
    return flat.reshape(-1, tq, length - 1)[:, :, :win]


def _block_toeplitz_bias(rel_bias, off, tq, win):
    nq, nk = tq // LANES, win // LANES
    blocks = {d: _toeplitz_bias(rel_bias, off - d * LANES, LANES, LANES) for d in range(-(nq - 1), nk)}
    return jnp.concatenate(
        [jnp.concatenate([blocks[b - a] for b in range(nk)], axis=2) for a in range(nq)], axis=1)


def _rel_tables(rel_bias, t_new, rows_b):
    tq, win = ATT_TILE, ATT_TILE + BAND
    q = jnp.arange(tq)[:, None]
    tabs = []
    for var in range(BAND // tq + 1):
        rel_k = jnp.arange(win)[None, :] - var * tq
        kch = jnp.floor_divide(rel_k, CHUNK)
        qch = q // CHUNK
        valid = (kch <= qch) & (kch >= qch - BAND_CHUNKS)
        tabs.append(jnp.where(valid[None], _block_toeplitz_bias(rel_bias, var * tq, tq, win), NEG_INF))
    tab_prompt = jnp.stack(tabs).astype(F32)
    tab_c = _toeplitz_bias(rel_bias, rows_b, t_new, rows_b).astype(F32)
    tab_n = _toeplitz_bias(rel_bias, 0, t_new, t_new).astype(F32)
    return tab_prompt, tab_c, tab_n


def _layer_weights(i, norm_mix, w_in, b_f, g_qa, g_ka, g_qb, g_kb, g_mix_out, w_out, norm_ffn,
                   w_rg, b_rg, w_re, b_re, w_eg, w_eu, w_ed, w_ple, norm_ple, w_pg):
    d_mix = W_A + W_B + W_C
    wt = jnp.transpose(w_in, (2, 0, 1))[:, i, :]
    w_f = wt[3 * d_mix:]
    w_big = jnp.concatenate(
        [_regroup_out_rows(wt[j * d_mix:(j + 1) * d_mix]) for j in range(3)]
        + [jnp.pad(w_f, ((0, LANES - w_f.shape[0]), (0, 0)))], axis=0).astype(BF16)
    tile6 = lambda g: jnp.tile(g, GROUP_W // HEAD_DIM)
    w_r = _pad_lanes(jnp.concatenate([w_re[i], w_rg[i]], axis=1))
    w_r_hi = w_r.astype(BF16)
    gm = g_mix_out[i]
    zpad = jnp.zeros((HEAD_DIM,), F32)
    g_mix = jnp.stack([gm[:W_A], jnp.concatenate([gm[W_A:W_A + W_B], zpad]),
                       jnp.concatenate([gm[W_A + W_B:], zpad])])[:, None, :]
    return dict(
        norm_mix=norm_mix[i][None], w_in=w_big,
        gq=jnp.concatenate([tile6(g_qa[i]), tile6(g_qb[i])])[None],
        gk=jnp.concatenate([tile6(g_ka[i]), tile6(g_kb[i])])[None],
        b_f=_pad_lanes(b_f[i])[None],
        g_mix=g_mix, w_out=_regroup_rows(w_out[i]).astype(BF16), norm_ffn=norm_ffn[i][None],
        w_r_hi=w_r_hi, w_r_lo=(w_r - w_r_hi.astype(F32)).astype(BF16),
        b_r=_pad_lanes(jnp.concatenate([b_re[i], b_rg[i]]))[None],
        w_eg=w_eg[i].astype(BF16), w_eu=w_eu[i].astype(BF16),
        w_ed=w_ed[i].reshape(-1, w_ed.shape[-1]).astype(BF16),
        w_gg=_group_cols(w_eg[i]), w_gu=_group_cols(w_eu[i]),
        w_gd=w_ed[i].reshape(N_GROUPS, -1, w_ed.shape[-1]).astype(BF16),
        w_ple=w_ple[i].astype(BF16), norm_ple=norm_ple[i][None], w_pg=w_pg[i].astype(BF16))


def _group_cols(w):
    e, d, f = w.shape
    w = w.reshape(N_GROUPS, EXPERTS_PER_GROUP, d, f)
    return jnp.transpose(w, (0, 2, 1, 3)).reshape(N_GROUPS, d, EXPERTS_PER_GROUP * f).astype(BF16)


def _token_tile(t, pref):
    return pref if t % pref == 0 else t


def _feature_major(cache):
    d, n, p, h, e = cache.shape
    return jnp.transpose(cache, (0, 1, 3, 4, 2)).reshape(d, n, h * e, p)


def _position_major(rows, heads):
    d, n, _, p = rows.shape
    return jnp.transpose(rows.reshape(d, n, heads, HEAD_DIM, p), (0, 1, 4, 2, 3))


def kernel(x_prompt, x_sample, p_prompt, p_sample, cache_a_k, cache_a_v, cache_a_logf, cache_b_k, cache_b_v, cache_c_k, cache_c_v, norm_mix, w_in, b_f, g_qa, g_ka, g_qb, g_kb, rel_bias, g_mix_out, w_out, norm_ffn, w_router_group, b_router_group, w_router_expert, b_router_expert, w_exp_gate, w_exp_up, w_exp_down, w_ple, norm_ple, w_ple_gate):
    nb, s, d = x_prompt.shape
    ns, t_new, _ = x_sample.shape
    depth = w_in.shape[0]
    past = cache_a_k.shape[2]
    rows_b = cache_b_k.shape[2]
    assert d == D_MODEL and s % PROJ_TILE == 0 and s >= BAND + ATT_TILE and past % ATT_TILE == 0
    tp, ts = nb * s, ns * t_new

    hp = x_prompt.reshape(tp, d)
    hs = x_sample.reshape(ts, d)
    pp = p_prompt.reshape(depth, tp, -1)
    ps = p_sample.reshape(depth, ts, -1)
    caches = tuple(_feature_major(c) for c in (cache_a_k, cache_a_v, cache_b_k, cache_b_v, cache_c_k, cache_c_v))
    lf_rows = jnp.transpose(cache_a_logf, (0, 3, 1, 2))
    rsum = _suffix_sum(lf_rows.reshape(depth * H_A * ns, past)).reshape(depth, H_A, ns, past)
    rsum = jnp.transpose(rsum, (0, 2, 1, 3))

    rows_p = None
    new_s = [[] for _ in range(7)]
    for i in range(depth):
        lw = _layer_weights(i, norm_mix, w_in, b_f, g_qa, g_ka, g_qb, g_kb, g_mix_out, w_out, norm_ffn,
                            w_router_group, b_router_group, w_router_expert, b_router_expert,
                            w_exp_gate, w_exp_up, w_exp_down, w_ple, norm_ple, w_ple_gate)
        tab_p, tab_c, tab_n = _rel_tables(rel_bias[i] * LOG2E, t_new, rows_b)

        pr = _inproj_prompt(hp, lw, i, depth, nb, s, rows_p)
        qat, ka, vat, qb, kb, vb, qc, kc, vc, logf = pr[:10]
        rows_p = pr[10:]
        kaug, qaug = _forget_aug(logf, nb, s)
        oa = _attn_a(qat, ka, vat, kaug, qaug, nb, s)
        ob = _prompt_attention(
            _attn_b_kernel, "attn_b", qb, kb, vb, (tab_p,),
            [pl.BlockSpec(tab_p.shape, lambda b, j: (0, 0, 0, 0))], [], nb, s, ATT_TILE)
        oc = _prompt_attention(
            _attn_c_kernel, "attn_c", qc, kc, vc, (), [],
            [pltpu.VMEM((H_C, C_QTILE, PAIR_W), BF16), pltpu.VMEM((H_C, C_QTILE, LANES), F32),
             pltpu.VMEM((H_C, C_QTILE, PAIR_W), F32)], nb, s, C_QTILE)
        h1, _, gates, *x_chunks = _post(oa, ob, oc, hp, lw, _token_tile(tp, 512), N_XCHUNKS)
        dispatched = _moe_dispatch(x_chunks, gates)

        sr = _inproj_sample(hs, lw)
        oa, ob, oc = _sample_attention(sr, caches, i, rsum, tab_c, tab_n, ns, t_new)
        h1s, xn, gates = _post(oa, ob, oc, hs, lw, _token_tile(ts, 512))
        hs = _moe(xn, gates, h1s, ps, i, lw, _token_tile(ts, 1024))

        hp = _moe_combine(dispatched, h1, pp, i, lw)
        akf, avf, bkf, bvf, ckf, cvf, logf = sr[9:]
        rows = (akf.reshape(ns, t_new, H_A, HEAD_DIM), avf.reshape(ns, t_new, H_A, HEAD_DIM),
                logf[:, :H_A].reshape(ns, t_new, H_A),
                bkf.reshape(ns, t_new, H_B, HEAD_DIM), bvf.reshape(ns, t_new, H_B, HEAD_DIM),
                ckf.reshape(ns, t_new, H_C, HEAD_DIM), cvf.reshape(ns, t_new, H_C, HEAD_DIM))
        for j in range(7):
            new_s[j].append(rows[j])

    akt, avt, bkt, bvt, ckt, cvt, lft = rows_p
    outs_p = (_position_major(akt, H_A), _position_major(avt, H_A),
              jnp.transpose(lft.reshape(depth, H_A, nb, s), (0, 2, 3, 1)),
              _position_major(bkt, H_B), _position_major(bvt, H_B),
              _position_major(ckt, H_C), _position_major(cvt, H_C))
    outs_s = [jnp.stack(r, axis=0) for r in new_s]
    return (hp.reshape(nb, s, d), hs.reshape(ns, t_new, d), *outs_p, *outs_s)
```

```python
import functools
import math

import jax
import jax.numpy as jnp
from jax import lax
from jax.experimental import pallas as pl
from jax.experimental.pallas import tpu as pltpu
from jax.experimental.pallas import tpu_sc as plsc

F32 = jnp.float32
BF16 = jnp.bfloat16

D_MODEL = 1024
HEAD_DIM = 64
H_A, H_B, H_C = 6, 5, 5
W_A, W_B, W_C = H_A * HEAD_DIM, H_B * HEAD_DIM, H_C * HEAD_DIM
GROUP_W = 384
PAIR_W = 2 * HEAD_DIM
N_PAIRS = GROUP_W // PAIR_W
CHUNK = 64
BAND_CHUNKS = 8
BAND = BAND_CHUNKS * CHUNK
REL_CLIP = 128
N_GROUPS = 4
EXPERTS_PER_GROUP = 4
N_EXPERTS = N_GROUPS * EXPERTS_PER_GROUP
EPS = 1e-6
ATTN_SCALE = HEAD_DIM ** -0.5
LOG2E = math.log2(math.e)
Q_SCALE = ATTN_SCALE * LOG2E
NEG_INF = -1e30
LANES = 128
SUBLANES = 8
VMEM_LIMIT = 48 * 1024 * 1024
EXPERTS_PER_STEP = 4
MOE_VMEM_LIMIT = 58 * 1024 * 1024

ATT_TILE = 256
A_QTILE = 512
C_QTILE = 256
PROJ_TILE = 1024
PROJ_VMEM_LIMIT = 56 * 1024 * 1024
AUG_SLOTS = 8

_NT = (((1,), (1,)), ((), ()))


def _cparams(*sem):
    return pltpu.CompilerParams(dimension_semantics=sem, vmem_limit_bytes=VMEM_LIMIT)


def _dot(a, b):
    return jnp.dot(a, b, preferred_element_type=F32)


def _dot_nt(a, b):
    return lax.dot_general(a, b, _NT, preferred_element_type=F32)


def _split3(x):
    hi = x.astype(BF16)
    r1 = x - hi.astype(F32)
    mid = r1.astype(BF16)
    lo = (r1 - mid.astype(F32)).astype(BF16)
    return hi, mid, lo


def _split2(x):
    hi = x.astype(BF16)
    lo = (x - hi.astype(F32)).astype(BF16)
    return hi, lo


def _lane_iota(n=LANES):
    return lax.broadcasted_iota(jnp.int32, (1, n), 1)


def _half_mask(half):
    lane = _lane_iota()
    return (lane < HEAD_DIM) if half == 0 else (lane >= HEAD_DIM)


def _pair_slice(h):
    return slice((h // 2) * PAIR_W, (h // 2 + 1) * PAIR_W)


def _softplus2(z2):
    return jnp.maximum(z2, 0.0) + jnp.log2(1.0 + jnp.exp2(-jnp.abs(z2)))


def _sigmoid(z):
    return 1.0 / (1.0 + jnp.exp(-z))


def _rms(x, g):
    return x * lax.rsqrt(jnp.mean(x * x, axis=-1, keepdims=True) + EPS) * g


def _pair_rms(z, gain):
    first = _half_mask(0)
    outs = []
    for j in range(N_PAIRS):
        blk = z[:, j * PAIR_W:(j + 1) * PAIR_W]
        sq = blk * blk
        lo = jnp.sum(jnp.where(first, sq, 0.0), axis=-1, keepdims=True)
        hi = jnp.sum(jnp.where(first, 0.0, sq), axis=-1, keepdims=True)
        ms = jnp.where(first, lo, hi) * (1.0 / HEAD_DIM)
        outs.append(blk * lax.rsqrt(ms + EPS) * gain[:, j * PAIR_W:(j + 1) * PAIR_W])
    return jnp.concatenate(outs, axis=-1)


def _projector(x_ref, gn_ref, w_ref):
    xb = _rms(x_ref[...], gn_ref[...]).astype(BF16)
    return lambda g, width=GROUP_W: _dot_nt(xb, w_ref[g * GROUP_W:g * GROUP_W + width, :])


def _log_forget(zf, bf_ref):
    zf = zf + bf_ref[...]
    return jnp.minimum(zf, 0.0) - jnp.log(1.0 + jnp.exp(-jnp.abs(zf)))


def _inproj_sample_kernel(x_ref, gn_ref, w_ref, gq_ref, gk_ref, bf_ref,
                          qa_ref, qb_ref, qc_ref, ka_ref, kb_ref, kc_ref, va_ref, vb_ref, vc_ref,
                          akf_ref, avf_ref, bkf_ref, bvf_ref, ckf_ref, cvf_ref, logf_ref):
    proj = _projector(x_ref, gn_ref, w_ref)
    qa_ref[...] = (_pair_rms(proj(0), gq_ref[:, :GROUP_W]) * Q_SCALE).astype(BF16)
    qb_ref[...] = (_pair_rms(proj(1), gq_ref[:, GROUP_W:]) * Q_SCALE).astype(BF16)
    qc_ref[...] = (proj(2) * Q_SCALE).astype(BF16)
    for g, norm, bf_ref_, f_ref, width in (
            (3, gk_ref[:, :GROUP_W], ka_ref, akf_ref, W_A), (4, gk_ref[:, GROUP_W:], kb_ref, bkf_ref, W_B),
            (5, None, kc_ref, ckf_ref, W_C), (6, None, va_ref, avf_ref, W_A),
            (7, None, vb_ref, bvf_ref, W_B), (8, None, vc_ref, cvf_ref, W_C)):
        z = proj(g)
        if norm is not None:
            z = _pair_rms(z, norm)
        f_ref[...] = z[:, :width]
        bf_ref_[...] = z.astype(BF16)
    logf_ref[...] = _log_forget(proj(9, LANES), bf_ref)


def _inproj_sample(x, lw):
    t = x.shape[0]
    row = lambda w: pl.BlockSpec((t, w), lambda i: (0, 0))
    full = lambda a: pl.BlockSpec(a.shape, lambda i: (0,) * a.ndim)
    widths = [GROUP_W] * 9 + [W_A, W_A, W_B, W_B, W_C, W_C, LANES]
    out_shape = [jax.ShapeDtypeStruct((t, w), BF16 if j < 9 else F32) for j, w in enumerate(widths)]
    ins = (x, lw['norm_mix'], lw['w_in'], lw['gq'], lw['gk'], lw['b_f'])
    return pl.pallas_call(
        _inproj_sample_kernel, grid=(1,),
        in_specs=[row(D_MODEL)] + [full(a) for a in ins[1:]],
        out_specs=[row(w) for w in widths], out_shape=out_shape,
        compiler_params=_cparams("arbitrary"), name="inproj_sample")(*ins)


N_PROMPT_ROWS = 7


def _inproj_prompt_kernel(*refs, n_alias):
    x_ref, gn_ref, w_ref, gq_ref, gk_ref, bf_ref = refs[:6]
    (qat_ref, ka_ref, vat_ref, qb_ref, kb_ref, vb_ref, qc_ref, kc_ref, vc_ref, logf_ref,
     akt_ref, avt_ref, bkt_ref, bvt_ref, ckt_ref, cvt_ref, lft_ref) = refs[6 + n_alias:]
    proj = _projector(x_ref, gn_ref, w_ref)
    qat_ref[...] = jnp.transpose(_pair_rms(proj(0), gq_ref[:, :GROUP_W]) * Q_SCALE).astype(BF16)
    qb_ref[...] = (_pair_rms(proj(1), gq_ref[:, GROUP_W:]) * Q_SCALE).astype(BF16)
    qc_ref[...] = (proj(2) * Q_SCALE).astype(BF16)
    ka = _pair_rms(proj(3), gk_ref[:, :GROUP_W])
    ka_ref[...] = ka.astype(BF16)
    akt_ref[0, 0] = jnp.transpose(ka)
    kb = _pair_rms(proj(4), gk_ref[:, GROUP_W:])
    kb_ref[...] = kb.astype(BF16)
    kc = proj(5)
    kc_ref[...] = kc.astype(BF16)
    ckt_ref[0, 0] = jnp.transpose(kc)[:W_C]
    vat = jnp.transpose(proj(6))
    avt_ref[0, 0] = vat
    for j in range(vat_ref.shape[1]):
        vat_ref[0, j] = vat[:, j * ATT_TILE:(j + 1) * ATT_TILE].astype(BF16)
    vb = proj(7)
    vb_ref[...] = vb.astype(BF16)
    vc = proj(8)
    vc_ref[...] = vc.astype(BF16)
    cvt_ref[0, 0] = jnp.transpose(vc)[:W_C]

    @pl.when(pl.program_id(1) == pl.num_programs(1) - 1)
    def _():
        keep = kb.shape[0] - BAND
        bkt_ref[0, 0] = jnp.transpose(kb)[:W_B, keep:]
        bvt_ref[0, 0] = jnp.transpose(vb)[:W_B, keep:]

    lf = _log_forget(proj(9, LANES), bf_ref)
    logf_ref[...] = lf
    lft_ref[0] = jnp.transpose(lf)[:H_A]


def _inproj_prompt(x, lw, layer, depth, nb, s, prev_rows):
    tm = PROJ_TILE
    nt = s // tm
    t = nb * s
    per = tm // ATT_TILE
    row = lambda w: pl.BlockSpec((tm, w), lambda b, i: (b * nt + i, 0))
    full = lambda a: pl.BlockSpec(a.shape, lambda b, i: (0,) * a.ndim, pipeline_mode=pl.Buffered(1))
    feat = lambda w: pl.BlockSpec((1, 1, w, tm), lambda b, i: (layer, b, 0, i))
    band = lambda w: pl.BlockSpec((1, 1, w, BAND), lambda b, i: (layer, b, 0, 0))
    tok = jax.ShapeDtypeStruct((t, GROUP_W), BF16)
    out_shape = [
        jax.ShapeDtypeStruct((GROUP_W, t), BF16), tok,
        jax.ShapeDtypeStruct((nb, s // ATT_TILE, GROUP_W, ATT_TILE), BF16),
        tok, tok, tok, tok, tok, tok, jax.ShapeDtypeStruct((t, LANES), F32),
        jax.ShapeDtypeStruct((depth, nb, W_A, s), F32), jax.ShapeDtypeStruct((depth, nb, W_A, s), F32),
        jax.ShapeDtypeStruct((depth, nb, W_B, BAND), F32), jax.ShapeDtypeStruct((depth, nb, W_B, BAND), F32),
        jax.ShapeDtypeStruct((depth, nb, W_C, s), F32), jax.ShapeDtypeStruct((depth, nb, W_C, s), F32),
        jax.ShapeDtypeStruct((depth, H_A, t), F32)]
    out_specs = [
        pl.BlockSpec((GROUP_W, tm), lambda b, i: (0, b * nt + i)), row(GROUP_W),
        pl.BlockSpec((1, per, GROUP_W, ATT_TILE), lambda b, i: (b, i, 0, 0)),
        row(GROUP_W), row(GROUP_W), row(GROUP_W), row(GROUP_W), row(GROUP_W), row(GROUP_W), row(LANES),
        feat(W_A), feat(W_A), band(W_B), band(W_B), feat(W_C), feat(W_C),
        pl.BlockSpec((1, H_A, tm), lambda b, i: (layer, 0, b * nt + i))]
    ins = [x, lw['norm_mix'], lw['w_in'], lw['gq'], lw['gk'], lw['b_f']]
    in_specs = [row(D_MODEL)] + [full(a) for a in ins[1:]]
    aliases = {}
    n_alias = 0
    if prev_rows is not None:
        n_alias = N_PROMPT_ROWS
        first_row_out = len(out_shape) - N_PROMPT_ROWS
        for j, a in enumerate(prev_rows):
            aliases[len(ins)] = first_row_out + j
            ins.append(a)
            in_specs.append(pl.BlockSpec(memory_space=pl.ANY))
    return pl.pallas_call(
        functools.partial(_inproj_prompt_kernel, n_alias=n_alias), grid=(nb, nt),
        in_specs=in_specs, out_specs=out_specs, out_shape=out_shape,
        input_output_aliases=aliases,
        compiler_params=pltpu.CompilerParams(dimension_semantics=("parallel", "arbitrary"),
                                             vmem_limit_bytes=PROJ_VMEM_LIMIT),
        name="inproj_prompt")(*ins)


def _forget_aug_kernel(logf_ref, kaug_ref, qaug_ref, *, nblk):
    tb = ATT_TILE
    r = lax.broadcasted_iota(jnp.int32, (tb, tb), 0)
    c = lax.broadcasted_iota(jnp.int32, (tb, tb), 1)
    tri = jnp.where(c <= r, 1.0, 0.0).astype(BF16)
    src = lax.broadcasted_iota(jnp.int32, (LANES, LANES), 0)
    dst = lax.broadcasted_iota(jnp.int32, (LANES, LANES), 1)

    def place(slot):
        return jnp.where((dst == AUG_SLOTS * src + slot) & (src < H_A), 1.0, 0.0).astype(BF16)

    lane = _lane_iota()
    slot = jnp.bitwise_and(lane, AUG_SLOTS - 1)
    used = lane < AUG_SLOTS * H_A
    key_const = jnp.where(used & (slot >= 3) & (slot < 6), 1.0, 0.0)
    qry_const = jnp.where(used & (slot < 3), -1.0, 0.0)
    carry = jnp.zeros((1, LANES), F32)
    for blk in range(nblk):
        rows = slice(blk * tb, (blk + 1) * tb)
        hi, mid, lo = _split3(logf_ref[rows, :])
        cs = (_dot(tri, hi) + _dot(tri, mid) + _dot(tri, lo)) + carry
        carry = cs[tb - 1:tb, :]
        hi, mid, lo = _split3(cs * LOG2E)
        kaug = _dot(hi, place(0)) + _dot(mid, place(1)) + _dot(lo, place(2)) + key_const
        kaug_ref[rows, :] = kaug.astype(BF16)
        qaug = _dot(hi, place(3)) + _dot(mid, place(4)) + _dot(lo, place(5)) + qry_const
        qaug_ref[:, rows] = jnp.transpose(qaug).astype(BF16)


def _forget_aug(logf, nb, s):
    return pl.pallas_call(
        functools.partial(_forget_aug_kernel, nblk=s // ATT_TILE), grid=(nb,),
        in_specs=[pl.BlockSpec((s, LANES), lambda b: (b, 0))],
        out_specs=[pl.BlockSpec((s, LANES), lambda b: (b, 0)), pl.BlockSpec((LANES, s), lambda b: (0, b))],
        out_shape=[jax.ShapeDtypeStruct((nb * s, LANES), BF16), jax.ShapeDtypeStruct((LANES, nb * s), BF16)],
        compiler_params=_cparams("parallel"), name="forget_aug")(logf)


def _suffix_sum_kernel(x_ref, o_ref):
    p = x_ref.shape[1]
    j = lax.broadcasted_iota(jnp.int32, (p, p), 0)
    s = lax.broadcasted_iota(jnp.int32, (p, p), 1)
    tri = jnp.where(j > s, 1.0, 0.0).astype(BF16)
    hi, mid, lo = _split3(x_ref[...])
    o_ref[...] = _dot(hi, tri) + _dot(mid, tri) + _dot(lo, tri)


def _suffix_sum(x):
    return pl.pallas_call(
        _suffix_sum_kernel, out_shape=jax.ShapeDtypeStruct(x.shape, F32),
        compiler_params=pltpu.CompilerParams(vmem_limit_bytes=VMEM_LIMIT), name="suffix_sum")(x)


def _attn_a_kernel(qt_ref, k_ref, vt_ref, kaug_ref, qaug_ref, o_ref, qp_ref, m_ref, l_ref, acc_ref):
    tq = qt_ref.shape[1]
    tk = ATT_TILE
    qi = pl.program_id(1)
    row = lax.broadcasted_iota(jnp.int32, (PAIR_W, 1), 0)
    aug = qaug_ref[...]
    for h in range(H_A):
        q2 = qt_ref[_pair_slice(h), :]
        half = (row < HEAD_DIM) if h % 2 == 0 else (row >= HEAD_DIM)
        qp_ref[h, :PAIR_W, :] = jnp.where(half, q2, jnp.zeros_like(q2))
        mine = (row >= AUG_SLOTS * h) & (row < AUG_SLOTS * (h + 1))
        qp_ref[h, PAIR_W:, :] = jnp.where(mine, aug, jnp.zeros_like(aug))
    m_ref[...] = jnp.full(m_ref.shape, NEG_INF, F32)
    l_ref[...] = jnp.zeros(l_ref.shape, F32)
    acc_ref[...] = jnp.zeros(acc_ref.shape, F32)

    def step(kt, key_offset):
        ks = pl.multiple_of(kt * tk, tk)
        ka = kaug_ref[pl.ds(ks, tk), :]
        tile_scores = [_dot(jnp.concatenate([k_ref[pl.ds(ks, tk), _pair_slice(h)], ka], axis=1), qp_ref[h])
                       for h in range(H_A)]
        probs, alphas = [], []
        for h in range(H_A):
            s = tile_scores[h]
            if key_offset is not None:
                key = lax.broadcasted_iota(jnp.int32, (tk, tq), 0)
                qry = lax.broadcasted_iota(jnp.int32, (tk, tq), 1)
                s = jnp.where(key + key_offset <= qry, s, NEG_INF)
            m_prev = m_ref[h]
            m_new = jnp.maximum(m_prev, jnp.max(s, axis=0, keepdims=True))
            alpha = jnp.exp2(m_prev - m_new)
            p = jnp.exp2(s - m_new)
            l_ref[h] = alpha * l_ref[h] + jnp.sum(p, axis=0, keepdims=True)
            m_ref[h] = m_new
            alphas.append(alpha)
            probs.append(p.astype(BF16))
        for h in range(H_A):
            acc_ref[h] = alphas[h] * acc_ref[h] + _dot(vt_ref[0, kt, _pair_slice(h), :], probs[h])

    def body(kt, carry):
        step(kt, None)
        return carry

    per = tq // tk
    lax.fori_loop(0, qi * per, body, 0)
    for j in range(per):
        step(qi * per + j, j * tk)
    for pair in range(N_PAIRS):
        h = 2 * pair
        out_t = jnp.where(row < HEAD_DIM, acc_ref[h] * (1.0 / l_ref[h]), acc_ref[h + 1] * (1.0 / l_ref[h + 1]))
        o_ref[:, pair * PAIR_W:(pair + 1) * PAIR_W] = jnp.transpose(out_t)


def _attn_a(qat, ka, vat, kaug, qaug, nb, s):
    tq = A_QTILE
    nq = s // tq
    return pl.pallas_call(
        _attn_a_kernel, grid=(nb, nq),
        in_specs=[pl.BlockSpec((GROUP_W, tq), lambda b, i: (0, b * nq + i)),
                  pl.BlockSpec((s, GROUP_W), lambda b, i: (b, 0)),
                  pl.BlockSpec((1, s // ATT_TILE, GROUP_W, ATT_TILE), lambda b, i: (b, 0, 0, 0)),
                  pl.BlockSpec((s, LANES), lambda b, i: (b, 0)),
                  pl.BlockSpec((LANES, tq), lambda b, i: (0, b * nq + i))],
        out_specs=pl.BlockSpec((tq, GROUP_W), lambda b, i: (b * nq + i, 0)),
        out_shape=jax.ShapeDtypeStruct((nb * s, GROUP_W), F32),
        scratch_shapes=[pltpu.VMEM((H_A, 2 * PAIR_W, tq), BF16), pltpu.VMEM((H_A, 1, tq), F32),
                        pltpu.VMEM((H_A, 1, tq), F32), pltpu.VMEM((H_A, PAIR_W, tq), F32)],
        compiler_params=_cparams("parallel", "arbitrary"), name="attn_a")(qat, ka, vat, kaug, qaug)


def _masked_q(q_ref, qm_ref, n_heads):
    for h in range(n_heads):
        q2 = q_ref[:, _pair_slice(h)]
        qm_ref[h] = jnp.where(_half_mask(h % 2), q2, jnp.zeros_like(q2))


def _attn_b_kernel(q_ref, k_ref, v_ref, tab_ref, o_ref):
    tq = ATT_TILE
    win = tq + BAND
    i = pl.program_id(1)
    var = jnp.minimum(i, BAND // tq)
    ws = pl.multiple_of(jnp.maximum(i * tq - BAND, 0), tq)
    heads = range(H_B)
    scores = []
    for h in heads:
        q2 = q_ref[:, _pair_slice(h)]
        qm = jnp.where(_half_mask(h % 2), q2, jnp.zeros_like(q2))
        scores.append(_dot_nt(qm, k_ref[pl.ds(ws, win), _pair_slice(h)]) + tab_ref[var, h])
    probs, norms = [], []
    for h in heads:
        p = jnp.exp2(scores[h] - jnp.max(scores[h], axis=-1, keepdims=True))
        norms.append(1.0 / jnp.sum(p, axis=-1, keepdims=True))
        probs.append(p.astype(BF16))
    res = [_dot(probs[h], v_ref[pl.ds(ws, win), _pair_slice(h)]) * norms[h] for h in heads]
    res.append(jnp.zeros((tq, PAIR_W), F32))
    first = _half_mask(0)
    for pair in range(N_PAIRS):
        o_ref[:, pair * PAIR_W:(pair + 1) * PAIR_W] = jnp.where(first, res[2 * pair], res[2 * pair + 1])


def _suffix_tri(n):
    j = lax.broadcasted_iota(jnp.int32, (2 * n, n), 0)
    s = lax.broadcasted_iota(jnp.int32, (2 * n, n), 1)
    return jnp.where(jnp.where(j >= n, j - n, j) >= s, 1.0, 0.0).astype(BF16)


def _stick_scores(z, tri, seen):
    sp = _softplus2(z)
    if seen is not None:
        sp = jnp.where(seen, sp, 0.0)
    hi, lo = _split2(sp)
    n = sp.shape[1]
    if n % LANES == 0:
        return _dot(jnp.concatenate([hi, lo], axis=1), tri)
    return _dot(hi, tri[:n]) + _dot(lo, tri[:n])


def _stick_weights(z, s_in, carry, seen):
    a = jnp.exp2(z - (s_in + carry))
    if seen is not None:
        a = jnp.where(seen, a, 0.0)
    return a.astype(BF16)


def _attn_c_kernel(q_ref, k_ref, v_ref, o_ref, qm_ref, carry_ref, acc_ref):
    tq = q_ref.shape[0]
    tk = ATT_TILE
    qi = pl.program_id(1)
    _masked_q(q_ref, qm_ref, H_C)
    carry_ref[...] = jnp.zeros(carry_ref.shape, F32)
    acc_ref[...] = jnp.zeros(acc_ref.shape, F32)
    heads = range(H_C)

    def step(kt, key_offset):
        ks = pl.multiple_of(kt * tk, tk)
        zs = [_dot_nt(qm_ref[h], k_ref[pl.ds(ks, tk), _pair_slice(h)]) for h in heads]
        tri = _suffix_tri(tk)
        seen = None
        if key_offset is not None:
            r = lax.broadcasted_iota(jnp.int32, (tq, tk), 0)
            c = lax.broadcasted_iota(jnp.int32, (tq, tk), 1)
            seen = c + key_offset < r
        sums = [_stick_scores(zs[h], tri, seen) for h in heads]
        weights = []
        for h in heads:
            carry = carry_ref[h][:, 0:1]
            weights.append(_stick_weights(zs[h], sums[h], carry, seen))
            carry_ref[h] = jnp.broadcast_to(carry + sums[h][:, 0:1], (tq, LANES))
        for h in heads:
            acc_ref[h] += _dot(weights[h], v_ref[pl.ds(ks, tk), _pair_slice(h)])

    per = tq // tk

    def body(j, c):
        step(qi * per - 1 - j, None)
        return c

    for j in reversed(range(per)):
        step(qi * per + j, j * tk)
    lax.fori_loop(0, qi * per, body, 0)
    first = _half_mask(0)
    for pair in range(N_PAIRS):
        h = 2 * pair
        second = acc_ref[h + 1] if h + 1 < H_C else jnp.zeros((tq, PAIR_W), F32)
        o_ref[:, pair * PAIR_W:(pair + 1) * PAIR_W] = jnp.where(first, acc_ref[h], second)


def _prompt_attention(kernel, name, q, k, v, extra, extra_specs, scratch, nb, s, tq):
    nq = s // tq
    qspec = pl.BlockSpec((tq, GROUP_W), lambda b, i: (b * nq + i, 0))
    kvspec = pl.BlockSpec((s, GROUP_W), lambda b, i: (b, 0))
    return pl.pallas_call(
        kernel, grid=(nb, nq),
        in_specs=[qspec, kvspec, kvspec] + extra_specs,
        out_specs=qspec, out_shape=jax.ShapeDtypeStruct((nb * s, GROUP_W), F32),
        scratch_shapes=scratch,
        compiler_params=_cparams("parallel", "arbitrary"), name=name)(q, k, v, *extra)


def _sample_heads(n_heads, q_ref, width):
    out = []
    for h in range(n_heads):
        lo = (h // 2) * PAIR_W
        span = slice(lo, min(lo + PAIR_W, width))
        q2 = q_ref[:, span]
        if span.stop - span.start == PAIR_W:
            q2 = jnp.where(_half_mask(h % 2), q2, jnp.zeros_like(q2))
        out.append((h, span, q2))
    return out


def _store_heads(o_ref, res, n_heads):
    for pair in range(N_PAIRS):
        lo = pair * PAIR_W
        h = 2 * pair
        if h + 1 < n_heads:
            o_ref[:, lo:lo + PAIR_W] = jnp.where(_half_mask(0), res[h], res[h + 1])
        else:
            o_ref[:, lo:lo + HEAD_DIM] = res[h]
            o_ref[:, lo + HEAD_DIM:lo + PAIR_W] = jnp.zeros_like(res[h])


def _joint_softmax_pv(s_c, s_n, vt_c, v_n):
    m = jnp.maximum(jnp.max(s_c, axis=-1, keepdims=True), jnp.max(s_n, axis=-1, keepdims=True))
    p_c = jnp.exp2(s_c - m)
    p_n = jnp.exp2(s_n - m)
    l = jnp.sum(p_c, axis=-1, keepdims=True) + jnp.sum(p_n, axis=-1, keepdims=True)
    return (_dot_nt(p_c.astype(BF16), vt_c) + _dot(p_n.astype(BF16), v_n)) / l


def _sample_attn_kernel(qa_ref, qb_ref, qc_ref, kan_ref, kbn_ref, kcn_ref, van_ref, vbn_ref, vcn_ref,
                        cak_ref, cav_ref, cbk_ref, cbv_ref, cck_ref, ccv_ref,
                        rsum_ref, logf_ref, tabc_ref, tabn_ref,
                        oa_ref, ob_ref, oc_ref):
    t = qa_ref.shape[0]
    past = cak_ref.shape[3]
    r = lax.broadcasted_iota(jnp.int32, (t, t), 0)
    c = lax.broadcasted_iota(jnp.int32, (t, t), 1)

    ltri = jnp.where(c <= r, 1.0, 0.0).astype(BF16)
    hi, mid, lo = _split3(logf_ref[...])
    pcol = (_dot(ltri, hi) + _dot(ltri, mid) + _dot(ltri, lo)) * LOG2E
    prow = jnp.transpose(jnp.concatenate([pcol, jnp.zeros((LANES - t, LANES), F32)], axis=0))
    heads = _sample_heads(H_A, qa_ref, W_A)
    kts = {sp.start: cak_ref[0, 0, sp, :].astype(BF16) for _, sp, _ in heads}
    vts = {sp.start: cav_ref[0, 0, sp, :].astype(BF16) for _, sp, _ in heads}
    s_c = [_dot(qm, kts[sp.start]) + (pcol[:, h:h + 1] + rsum_ref[0, 0, h:h + 1, :] * LOG2E)
           for h, sp, qm in heads]
    s_n = [jnp.where(c <= r, _dot_nt(qm, kan_ref[:, sp]) + (pcol[:, h:h + 1] - prow[h:h + 1, 0:t]), NEG_INF)
           for h, sp, qm in heads]
    _store_heads(oa_ref, [_joint_softmax_pv(s_c[h], s_n[h], vts[sp.start], van_ref[:, sp])
                          for h, sp, _ in heads], H_A)

    heads = _sample_heads(H_B, qb_ref, W_B)
    kts = {sp.start: cbk_ref[0, 0, sp, :].astype(BF16) for _, sp, _ in heads}
    vts = {sp.start: cbv_ref[0, 0, sp, :].astype(BF16) for _, sp, _ in heads}
    s_c = [_dot(qm, kts[sp.start]) + tabc_ref[h] for h, sp, qm in heads]
    s_n = [_dot_nt(qm, kbn_ref[:, sp]) + tabn_ref[h] for h, sp, qm in heads]
    _store_heads(ob_ref, [_joint_softmax_pv(s_c[h], s_n[h], vts[sp.start], vbn_ref[:, sp])
                          for h, sp, _ in heads], H_B)

    heads = _sample_heads(H_C, qc_ref, W_C)
    nblk = past // ATT_TILE
    tri_n = _suffix_tri(t)
    tri_c = _suffix_tri(ATT_TILE)
    seen = c < r
    kts = {sp.start: cck_ref[0, 0, sp, :].astype(BF16) for _, sp, _ in heads}
    vts = {sp.start: ccv_ref[0, 0, sp, :].astype(BF16) for _, sp, _ in heads}
    z_n = [_dot_nt(qm, kcn_ref[:, sp]) for _, sp, qm in heads]
    z_c = [_dot(qm, kts[sp.start]) for _, sp, qm in heads]
    sum_n = [_stick_scores(z, tri_n, seen) for z in z_n]
    sum_c = [[_stick_scores(z[:, b * ATT_TILE:(b + 1) * ATT_TILE], tri_c, None) for b in range(nblk)]
             for z in z_c]
    res = []
    for h, sp, _ in heads:
        acc = _dot(_stick_weights(z_n[h], sum_n[h], 0.0, seen), vcn_ref[:, sp])
        carry = sum_n[h][:, 0:1]
        for b in reversed(range(nblk)):
            cols = slice(b * ATT_TILE, (b + 1) * ATT_TILE)
            a = _stick_weights(z_c[h][:, cols], sum_c[h][b], carry, None)
            acc = acc + _dot_nt(a, vts[sp.start][:, cols])
            carry = carry + sum_c[h][b][:, 0:1]
        res.append(acc)
    _store_heads(oc_ref, res, H_C)


def _sample_attention(proj, caches, layer, rsum, tabc, tabn, nb, t):
    qa, qb, qc, ka, kb, kc, va, vb, vc = proj[:9]
    logf = proj[15]
    new = pl.BlockSpec((t, GROUP_W), lambda b: (b, 0))
    cache = lambda a: pl.BlockSpec((1, 1) + a.shape[2:], lambda b: (layer, b, 0, 0))
    full = lambda a: pl.BlockSpec(a.shape, lambda b: (0,) * a.ndim)
    return pl.pallas_call(
        _sample_attn_kernel, grid=(nb,),
        in_specs=[new] * 9 + [cache(a) for a in caches] + [
            cache(rsum), pl.BlockSpec((t, LANES), lambda b: (b, 0)), full(tabc), full(tabn)],
        out_specs=[new] * 3,
        out_shape=[jax.ShapeDtypeStruct((nb * t, GROUP_W), F32)] * 3,
        compiler_params=_cparams("parallel"), name="sample_attn")(
            qa, qb, qc, ka, kb, kc, va, vb, vc, *caches, rsum, logf, tabc, tabn)


def _masked_max(x, mask):
    return jnp.max(jnp.where(mask, x, -jnp.inf), axis=-1, keepdims=True)


def _first_lane(mask, lane):
    return jnp.min(jnp.where(mask, lane, float(LANES)), axis=-1, keepdims=True)


def _route(logits):
    lane = _lane_iota().astype(F32)
    is_g = (lane >= N_EXPERTS) & (lane < N_EXPERTS + N_GROUPS)
    gmax = _masked_max(logits, is_g)
    p_g = 1.0 / jnp.sum(jnp.where(is_g, jnp.exp(logits - gmax), 0.0), axis=-1, keepdims=True)
    g_sel = _first_lane(is_g & (logits == gmax), lane) - N_EXPERTS
    lo = g_sel * EXPERTS_PER_GROUP
    in_g = (lane >= lo) & (lane < lo + EXPERTS_PER_GROUP)
    l1 = _masked_max(logits, in_g)
    i1 = _first_lane(in_g & (logits == l1), lane)
    rest = in_g & (lane != i1)
    l2 = _masked_max(logits, rest)
    i2 = _first_lane(rest & (logits == l2), lane)
    e2 = jnp.exp(l2 - l1)
    w1 = p_g / (1.0 + e2)
    w2 = p_g * e2 / (1.0 + e2)
    gates = jnp.where(lane == i1, w1, jnp.where(lane == i2, w2, 0.0))
    return jnp.where(lane == g_sel + N_EXPERTS, 1.0, gates)


def _post_kernel(oa_ref, ob_ref, oc_ref, h_ref, gmix_ref, wout_ref, nffn_ref, wrh_ref, wrl_ref, br_ref,
                 h1_ref, xn_ref, gates_ref, *xchunk_refs):
    def gnorm(o, width):
        ms = jnp.sum(o * o, axis=-1, keepdims=True) * (1.0 / width)
        return o * lax.rsqrt(ms + EPS)

    h1 = h_ref[...]
    for g, (o_ref, width) in enumerate(((oa_ref, W_A), (ob_ref, W_B), (oc_ref, W_C))):
        merged = (gnorm(o_ref[...], width) * gmix_ref[g]).astype(BF16)
        h1 = h1 + _dot(merged, wout_ref[g])
    h1_ref[...] = h1
    xf = _rms(h1, nffn_ref[...])
    hi, lo = _split2(xf)
    xn_ref[...] = hi
    bits = pltpu.bitcast(hi.astype(F32), jnp.uint32)
    for c, ref in enumerate(xchunk_refs):
        a = bits[:, (2 * c) * SC_ROW_WORDS:(2 * c + 1) * SC_ROW_WORDS]
        b = bits[:, (2 * c + 1) * SC_ROW_WORDS:(2 * c + 2) * SC_ROW_WORDS]
        ref[...] = jnp.bitwise_or(a, jnp.right_shift(b, jnp.uint32(16)))
    logits = _dot(hi, wrh_ref[...]) + _dot(lo, wrh_ref[...]) + _dot(hi, wrl_ref[...]) + br_ref[...]
    gates_ref[...] = _route(logits)


def _post(oa, ob, oc, h, lw, tm, n_chunks=0):
    t = h.shape[0]
    row = lambda w: pl.BlockSpec((tm, w), lambda i: (i, 0))
    full = lambda a: pl.BlockSpec(a.shape, lambda i: (0,) * a.ndim)
    ws = (lw['g_mix'], lw['w_out'], lw['norm_ffn'], lw['w_r_hi'], lw['w_r_lo'], lw['b_r'])
    return pl.pallas_call(
        _post_kernel, grid=(t // tm,),
        in_specs=[row(GROUP_W)] * 3 + [row(D_MODEL)] + [full(a) for a in ws],
        out_specs=[row(D_MODEL), row(D_MODEL), row(LANES)] + [row(SC_ROW_WORDS)] * n_chunks,
        out_shape=[jax.ShapeDtypeStruct((t, D_MODEL), F32), jax.ShapeDtypeStruct((t, D_MODEL), BF16),
                   jax.ShapeDtypeStruct((t, LANES), F32)]
        + [jax.ShapeDtypeStruct((t, SC_ROW_WORDS), jnp.uint32)] * n_chunks,
        compiler_params=_cparams("parallel"), name="post")(oa, ob, oc, h, *ws)


def _moe_kernel(xn_ref, gates_ref, wg_ref, wu_ref, wd_ref, h1_ref, p_ref, wple_ref, nple_ref, wpg_ref,
                out_ref, acc_ref):
    step = pl.program_id(1)

    @pl.when(step == 0)
    def _():
        acc_ref[...] = jnp.zeros_like(acc_ref)

    x = xn_ref[...]
    gates = gates_ref[...]
    acts = []
    for k in range(EXPERTS_PER_STEP):
        e = step * EXPERTS_PER_STEP + k
        g = _dot(x, wg_ref[k])
        u = _dot(x, wu_ref[k])
        gate = jnp.sum(jnp.where(_lane_iota() == e, gates, 0.0), axis=-1, keepdims=True)
        acts.append(((g * _sigmoid(g)) * u * gate).astype(BF16))
    acc_ref[...] += _dot(jnp.concatenate(acts, axis=1), wd_ref[...])

    @pl.when(step == pl.num_programs(1) - 1)
    def _():
        h2 = h1_ref[...] + acc_ref[...]
        gate_ple = _sigmoid(_dot(_rms(h2, nple_ref[...]).astype(BF16), wpg_ref[...]))
        out_ref[...] = h2 + _dot(p_ref[0].astype(BF16), wple_ref[...]) * gate_ple


def _moe(xn, gates, h1, p_all, layer, lw, tm):
    t = xn.shape[0]
    row = lambda w: pl.BlockSpec((tm, w), lambda i, e: (i, 0))
    once = pl.Buffered(1)
    full = lambda a: pl.BlockSpec(a.shape, lambda i, e: (0,) * a.ndim, pipeline_mode=once)
    exp = lambda a: pl.BlockSpec((EXPERTS_PER_STEP,) + a.shape[1:], lambda i, e: (e, 0, 0))
    d_hidden = lw['w_ed'].shape[0] // N_EXPERTS
    return pl.pallas_call(
        _moe_kernel, grid=(t // tm, N_EXPERTS // EXPERTS_PER_STEP),
        in_specs=[row(D_MODEL), row(LANES), exp(lw['w_eg']), exp(lw['w_eu']),
                  pl.BlockSpec((EXPERTS_PER_STEP * d_hidden, D_MODEL), lambda i, e: (e, 0)),
                  pl.BlockSpec((tm, D_MODEL), lambda i, e: (i, 0), pipeline_mode=once),
                  pl.BlockSpec((1, tm, p_all.shape[2]), lambda i, e: (layer, i, 0), pipeline_mode=once),
                  full(lw['w_ple']), full(lw['norm_ple']), full(lw['w_pg'])],
        out_specs=row(D_MODEL), out_shape=jax.ShapeDtypeStruct((t, D_MODEL), F32),
        scratch_shapes=[pltpu.VMEM((tm, D_MODEL), F32)],
        compiler_params=pltpu.CompilerParams(dimension_semantics=("parallel", "arbitrary"),
                                             vmem_limit_bytes=MOE_VMEM_LIMIT), name="moe")(
            xn, gates, lw['w_eg'], lw['w_eu'], lw['w_ed'], h1, p_all, lw['w_ple'], lw['norm_ple'], lw['w_pg'])


SORT_TILE = 1024
SC_WINDOW = 128
SC_ROW_WORDS = 256
N_CHUNKS = D_MODEL // SC_ROW_WORDS
N_XCHUNKS = N_CHUNKS // 2


def _plan_kernel(gates_ref, dest_ref, total_ref, *, tb):
    lane = _lane_iota()
    is_g = (lane >= N_EXPERTS) & (lane < N_EXPERTS + N_GROUPS)
    total = jnp.sum(jnp.where(is_g, gates_ref[...], 0.0), axis=0, keepdims=True)
    padded = jnp.floor((total + (SORT_TILE - 1)) * (1.0 / SORT_TILE)) * SORT_TILE
    base = jnp.zeros((1, LANES), F32)
    for g in range(1, N_GROUPS):
        before = jnp.sum(jnp.where(lane < N_EXPERTS + g, padded, 0.0), axis=-1, keepdims=True)
        base = jnp.where(lane == N_EXPERTS + g, before, base)
    r = lax.broadcasted_iota(jnp.int32, (tb, tb), 0)
    c = lax.broadcasted_iota(jnp.int32, (tb, tb), 1)
    ltri = jnp.where(c < r, 1.0, 0.0).astype(BF16)
    carry = jnp.zeros((1, LANES), F32)
    for blk in range(gates_ref.shape[0] // tb):
        rows = slice(blk * tb, (blk + 1) * tb)
        member = jnp.where(is_g, gates_ref[rows, :], 0.0)
        rank = _dot(ltri, member.astype(BF16)) + carry
        carry = carry + jnp.sum(member, axis=0, keepdims=True)
        dest = jnp.sum(member * (base + rank), axis=-1, keepdims=True)
        dest_ref[rows, :] = jnp.broadcast_to(dest, (tb, LANES))
    total_ref[...] = jnp.broadcast_to(total, total_ref.shape)


def _plan(gates):
    t = gates.shape[0]
    return pl.pallas_call(
        functools.partial(_plan_kernel, tb=SORT_TILE),
        out_shape=[jax.ShapeDtypeStruct((t, LANES), F32), jax.ShapeDtypeStruct((SUBLANES, LANES), F32)],
        compiler_params=pltpu.CompilerParams(vmem_limit_bytes=VMEM_LIMIT), name="moe_plan")(gates)


def _sc_scatter_rows(x, idx, n_out):
    n, d = x.shape
    mesh = plsc.VectorSubcoreMesh(core_axis_name="core", subcore_axis_name="subcore")

    @functools.partial(pl.kernel, out_type=jax.ShapeDtypeStruct((n_out, d), x.dtype), mesh=mesh, scratch_types=[])
    def scatter(x_hbm, i_hbm, o_hbm):
        def body(x_vmem, i_vmem):
            pltpu.sync_copy(x_vmem, o_hbm.at[i_vmem.at[0]])

        pltpu.emit_pipeline(
            body, grid=(n // SC_WINDOW,),
            in_specs=[pl.BlockSpec((SC_WINDOW, d), index_map=lambda i: (i, 0)),
                      pl.BlockSpec((1, SC_WINDOW), index_map=lambda i: (0, i))],
            out_specs=[], core_axis_name=('core', 'subcore'), dimension_semantics=(pltpu.PARALLEL,))(x_hbm, i_hbm)

    return scatter(x, idx.reshape(1, n))


def _sc_gather_rows(x, idx):
    n = idx.shape[0]
    d = x.shape[1]
    mesh = plsc.VectorSubcoreMesh(core_axis_name="core", subcore_axis_name="subcore")

    @functools.partial(pl.kernel, out_type=jax.ShapeDtypeStruct((n, d), x.dtype), mesh=mesh)
    def gather(x_hbm, i_hbm, o_hbm):
        def body(i_vmem, o_vmem):
            pltpu.sync_copy(x_hbm.at[i_vmem.at[0]], o_vmem)

        pltpu.emit_pipeline(
            body, grid=(n // SC_WINDOW,),
            in_specs=[pl.BlockSpec((1, SC_WINDOW), index_map=lambda i: (0, i))],
            out_specs=[pl.BlockSpec((SC_WINDOW, d), index_map=lambda i: (i, 0))],
            core_axis_name=('core', 'subcore'), dimension_semantics=(pltpu.PARALLEL,))(i_hbm, o_hbm)

    return gather(x, idx.reshape(1, n))


def _experts_sorted_kernel(tg_ref, tv_ref, *refs):
    xs_refs, (gs_ref, wg_ref, wu_ref, wd_ref), ys_refs = (
        refs[:N_XCHUNKS], refs[N_XCHUNKS:N_XCHUNKS + 4], refs[-N_CHUNKS:])
    j = pl.program_id(0)

    @pl.when(tv_ref[j] > 0)
    def _():
        g = tg_ref[j]
        halves = []
        for r in xs_refs:
            w = r[...]
            halves.append(pltpu.bitcast(jnp.bitwise_and(w, jnp.uint32(0xFFFF0000)), F32))
            halves.append(pltpu.bitcast(jnp.left_shift(w, jnp.uint32(16)), F32))
        x = jnp.concatenate(halves, axis=1).astype(BF16)
        src = lax.broadcasted_iota(jnp.int32, (LANES, D_MODEL), 0)
        dst = lax.broadcasted_iota(jnp.int32, (LANES, D_MODEL), 1)
        d_shift = (D_MODEL // EXPERTS_PER_GROUP).bit_length() - 1
        expand = jnp.where(src == g * EXPERTS_PER_GROUP + jnp.right_shift(dst, d_shift), 1.0, 0.0).astype(BF16)
        hi, mid, lo = _split3(gs_ref[...])
        ge = _dot(hi, expand) + _dot(mid, expand) + _dot(lo, expand)
        hg = _dot(x, wg_ref[0])
        act = (hg * _sigmoid(hg)) * _dot(x, wu_ref[0]) * ge
        y = _dot(act.astype(BF16), wd_ref[0])
        for c, ref in enumerate(ys_refs):
            ref[...] = y[:, c * SC_ROW_WORDS:(c + 1) * SC_ROW_WORDS]


def _experts_sorted(xs_chunks, gs, tile_group, tile_valid, lw):
    n_rows = gs.shape[0]
    row = lambda w: pl.BlockSpec((SORT_TILE, w), lambda j, tg, tv: (j, 0))
    grp = lambda a: pl.BlockSpec((1,) + a.shape[1:], lambda j, tg, tv: (tg[j], 0, 0))
    return pl.pallas_call(
        _experts_sorted_kernel,
        grid_spec=pltpu.PrefetchScalarGridSpec(
            num_scalar_prefetch=2, grid=(n_rows // SORT_TILE,),
            in_specs=[row(SC_ROW_WORDS)] * N_XCHUNKS + [row(LANES), grp(lw['w_gg']), grp(lw['w_gu']), grp(lw['w_gd'])],
            out_specs=[row(SC_ROW_WORDS)] * N_CHUNKS),
        out_shape=[jax.ShapeDtypeStruct((n_rows, SC_ROW_WORDS), F32)] * N_CHUNKS,
        compiler_params=_cparams("arbitrary"), name="moe_sorted")(
            tile_group, tile_valid, *xs_chunks, gs, lw['w_gg'], lw['w_gu'], lw['w_gd'])


def _ple_kernel(h1_ref, *refs):
    y_refs, (p_ref, wple_ref, nple_ref, wpg_ref, out_ref) = refs[:N_CHUNKS], refs[N_CHUNKS:]
    h2 = h1_ref[...] + jnp.concatenate([r[...] for r in y_refs], axis=1)
    gate_ple = _sigmoid(_dot(_rms(h2, nple_ref[...]).astype(BF16), wpg_ref[...]))
    out_ref[...] = h2 + _dot(p_ref[0].astype(BF16), wple_ref[...]) * gate_ple


def _ple(h1, y_chunks, p_all, layer, lw, tm):
    t = h1.shape[0]
    row = lambda w: pl.BlockSpec((tm, w), lambda i: (i, 0))
    full = lambda a: pl.BlockSpec(a.shape, lambda i: (0,) * a.ndim)
    return pl.pallas_call(
        _ple_kernel, grid=(t // tm,),
        in_specs=[row(D_MODEL)] + [row(SC_ROW_WORDS)] * N_CHUNKS
        + [pl.BlockSpec((1, tm, p_all.shape[2]), lambda i: (layer, i, 0)),
           full(lw['w_ple']), full(lw['norm_ple']), full(lw['w_pg'])],
        out_specs=row(D_MODEL), out_shape=jax.ShapeDtypeStruct((t, D_MODEL), F32),
        compiler_params=_cparams("parallel"), name="ple")(
            h1, *y_chunks, p_all, lw['w_ple'], lw['norm_ple'], lw['w_pg'])


def _moe_dispatch(x_chunks, gates):
    t = gates.shape[0]
    n_rows = t + N_GROUPS * SORT_TILE
    dest_rep, total = _plan(gates)
    dest = dest_rep[:, 0].astype(jnp.int32)
    padded = (total[0, N_EXPERTS:N_EXPERTS + N_GROUPS].astype(jnp.int32) + (SORT_TILE - 1)) // SORT_TILE * SORT_TILE
    ends = jnp.cumsum(padded)
    tile_start = jnp.arange(n_rows // SORT_TILE, dtype=jnp.int32) * SORT_TILE
    tile_group = jnp.minimum(jnp.sum(tile_start[:, None] >= ends[None, :], axis=1), N_GROUPS - 1).astype(jnp.int32)
    tile_valid = (tile_start < ends[-1]).astype(jnp.int32)
    xs_chunks = [_sc_scatter_rows(xc, dest, n_rows) for xc in x_chunks]
    gs = _sc_scatter_rows(gates, dest, n_rows)
    return xs_chunks, gs, tile_group, tile_valid, dest


def _moe_combine(dispatched, h1, p_all, layer, lw):
    xs_chunks, gs, tile_group, tile_valid, dest = dispatched
    ys_chunks = _experts_sorted(xs_chunks, gs, tile_group, tile_valid, lw)
    y_chunks = [_sc_gather_rows(yc, dest) for yc in ys_chunks]
    return _ple(h1, y_chunks, p_all, layer, lw, _token_tile(h1.shape[0], 1024))


def _regroup_out_rows(wt):
    z = jnp.zeros((HEAD_DIM, wt.shape[1]), wt.dtype)
    return jnp.concatenate([wt[:W_A], wt[W_A:W_A + W_B], z, wt[W_A + W_B:], z], axis=0)


def _regroup_rows(w):
    z = jnp.zeros((HEAD_DIM, w.shape[1]), w.dtype)
    return jnp.stack([w[:W_A], jnp.concatenate([w[W_A:W_A + W_B], z], axis=0),
                      jnp.concatenate([w[W_A + W_B:], z], axis=0)])


def _pad_lanes(a, n=LANES):
    return jnp.pad(a, [(0, 0)] * (a.ndim - 1) + [(0, n - a.shape[-1])])


def _toeplitz_bias(rel_bias, off, tq, win):
    length = tq + win
    pad = length + abs(off)
    ext = jnp.flip(jnp.pad(rel_bias, ((0, 0), (pad, pad)), mode='edge'), axis=1)
    s1 = ext.shape[1] - 1 - (off + REL_CLIP + pad)
    v = jnp.concatenate([ext[:, s1:s1 + win], ext[:, s1 - tq:s1]], axis=1)
    flat = jnp.tile(v, (1, tq))[:, :tq * (length - 1)]
    return flat.reshape(-1, tq, length - 1)[:, :, :win]


def _block_toeplitz_bias(rel_bias, off, tq, win):
    nq, nk = tq // LANES, win // LANES
    blocks = {d: _toeplitz_bias(rel_bias, off - d * LANES, LANES, LANES) for d in range(-(nq - 1), nk)}
    return jnp.concatenate(
        [jnp.concatenate([blocks[b - a] for b in range(nk)], axis=2) for a in range(nq)], axis=1)


def _rel_tables(rel_bias, t_new, rows_b):
    tq, win = ATT_TILE, ATT_TILE + BAND
    q = jnp.arange(tq)[:, None]
    tabs = []
    for var in range(BAND // tq + 1):
        rel_k = jnp.arange(win)[None, :] - var * tq
        kch = jnp.floor_divide(rel_k, CHUNK)
        qch = q // CHUNK
        valid = (kch <= qch) & (kch >= qch - BAND_CHUNKS)
        tabs.append(jnp.where(valid[None], _block_toeplitz_bias(rel_bias, var * tq, tq, win), NEG_INF))
    tab_prompt = jnp.stack(tabs).astype(F32)
    tab_c = _toeplitz_bias(rel_bias, rows_b, t_new, rows_b).astype(F32)
    tab_n = _toeplitz_bias(rel_bias, 0, t_new, t_new).astype(F32)
    return tab_prompt, tab_c, tab_n


def _layer_weights(i, norm_mix, w_in, b_f, g_qa, g_ka, g_qb, g_kb, g_mix_out, w_out, norm_ffn,
                   w_rg, b_rg, w_re, b_re, w_eg, w_eu, w_ed, w_ple, norm_ple, w_pg):
    d_mix = W_A + W_B + W_C
    wt = jnp.transpose(w_in, (2, 0, 1))[:, i, :]
    w_f = wt[3 * d_mix:]
    w_big = jnp.concatenate(
        [_regroup_out_rows(wt[j * d_mix:(j + 1) * d_mix]) for j in range(3)]
        + [jnp.pad(w_f, ((0, LANES - w_f.shape[0]), (0, 0)))], axis=0).astype(BF16)
    tile6 = lambda g: jnp.tile(g, GROUP_W // HEAD_DIM)
    w_r = _pad_lanes(jnp.concatenate([w_re[i], w_rg[i]], axis=1))
    w_r_hi = w_r.astype(BF16)
    gm = g_mix_out[i]
    zpad = jnp.zeros((HEAD_DIM,), F32)
    g_mix = jnp.stack([gm[:W_A], jnp.concatenate([gm[W_A:W_A + W_B], zpad]),
                       jnp.concatenate([gm[W_A + W_B:], zpad])])[:, None, :]
    return dict(
        norm_mix=norm_mix[i][None], w_in=w_big,
        gq=jnp.concatenate([tile6(g_qa[i]), tile6(g_qb[i])])[None],
        gk=jnp.concatenate([tile6(g_ka[i]), tile6(g_kb[i])])[None],
        b_f=_pad_lanes(b_f[i])[None],
        g_mix=g_mix, w_out=_regroup_rows(w_out[i]).astype(BF16), norm_ffn=norm_ffn[i][None],
        w_r_hi=w_r_hi, w_r_lo=(w_r - w_r_hi.astype(F32)).astype(BF16),
        b_r=_pad_lanes(jnp.concatenate([b_re[i], b_rg[i]]))[None],
        w_eg=w_eg[i].astype(BF16), w_eu=w_eu[i].astype(BF16),
        w_ed=w_ed[i].reshape(-1, w_ed.shape[-1]).astype(BF16),
        w_gg=_group_cols(w_eg[i]), w_gu=_group_cols(w_eu[i]),
        w_gd=w_ed[i].reshape(N_GROUPS, -1, w_ed.shape[-1]).astype(BF16),
        w_ple=w_ple[i].astype(BF16), norm_ple=norm_ple[i][None], w_pg=w_pg[i].astype(BF16))


def _group_cols(w):
    e, d, f = w.shape
    w = w.reshape(N_GROUPS, EXPERTS_PER_GROUP, d, f)
    return jnp.transpose(w, (0, 2, 1, 3)).reshape(N_GROUPS, d, EXPERTS_PER_GROUP * f).astype(BF16)


def _token_tile(t, pref):
    return pref if t % pref == 0 else t


def _feature_major(cache):
    d, n, p, h, e = cache.shape
    return jnp.transpose(cache, (0, 1, 3, 4, 2)).reshape(d, n, h * e, p)


def _position_major(rows, heads):
    d, n, _, p = rows.shape
    return jnp.transpose(rows.reshape(d, n, heads, HEAD_DIM, p), (0, 1, 4, 2, 3))


def kernel(x_prompt, x_sample, p_prompt, p_sample, cache_a_k, cache_a_v, cache_a_logf, cache_b_k, cache_b_v, cache_c_k, cache_c_v, norm_mix, w_in, b_f, g_qa, g_ka, g_qb, g_kb, rel_bias, g_mix_out, w_out, norm_ffn, w_router_group, b_router_group, w_router_expert, b_router_expert, w_exp_gate, w_exp_up, w_exp_down, w_ple, norm_ple, w_ple_gate):
    nb, s, d = x_prompt.shape
    ns, t_new, _ = x_sample.shape
    depth = w_in.shape[0]
    past = cache_a_k.shape[2]
    rows_b = cache_b_k.shape[2]
    assert d == D_MODEL and s % PROJ_TILE == 0 and s >= BAND + ATT_TILE and past % ATT_TILE == 0
    tp, ts = nb * s, ns * t_new

    hp = x_prompt.reshape(tp, d)
    hs = x_sample.reshape(ts, d)
    pp = p_prompt.reshape(depth, tp, -1)
    ps = p_sample.reshape(depth, ts, -1)
    caches = tuple(_feature_major(c) for c in (cache_a_k, cache_a_v, cache_b_k, cache_b_v, cache_c_k, cache_c_v))
    lf_rows = jnp.transpose(cache_a_logf, (0, 3, 1, 2))
    rsum = _suffix_sum(lf_rows.reshape(depth * H_A * ns, past)).reshape(depth, H_A, ns, past)
    rsum = jnp.transpose(rsum, (0, 2, 1, 3))

    rows_p = None
    new_s = [[] for _ in range(7)]
    for i in range(depth):
        lw = _layer_weights(i, norm_mix, w_in, b_f, g_qa, g_ka, g_qb, g_kb, g_mix_out, w_out, norm_ffn,
                            w_router_group, b_router_group, w_router_expert, b_router_expert,
                            w_exp_gate, w_exp_up, w_exp_down, w_ple, norm_ple, w_ple_gate)
        tab_p, tab_c, tab_n = _rel_tables(rel_bias[i] * LOG2E, t_new, rows_b)

        pr = _inproj_prompt(hp, lw, i, depth, nb, s, rows_p)
        qat, ka, vat, qb, kb, vb, qc, kc, vc, logf = pr[:10]
        rows_p = pr[10:]
        kaug, qaug = _forget_aug(logf, nb, s)
        oa = _attn_a(qat, ka, vat, kaug, qaug, nb, s)
        ob = _prompt_attention(
            _attn_b_kernel, "attn_b", qb, kb, vb, (tab_p,),
            [pl.BlockSpec(tab_p.shape, lambda b, j: (0, 0, 0, 0))], [], nb, s, ATT_TILE)
        oc = _prompt_attention(
            _attn_c_kernel, "attn_c", qc, kc, vc, (), [],
            [pltpu.VMEM((H_C, C_QTILE, PAIR_W), BF16), pltpu.VMEM((H_C, C_QTILE, LANES), F32),
             pltpu.VMEM((H_C, C_QTILE, PAIR_W), F32)], nb, s, C_QTILE)
        h1, _, gates, *x_chunks = _post(oa, ob, oc, hp, lw, _token_tile(tp, 512), N_XCHUNKS)
        dispatched = _moe_dispatch(x_chunks, gates)

        sr = _inproj_sample(hs, lw)
        oa, ob, oc = _sample_attention(sr, caches, i, rsum, tab_c, tab_n, ns, t_new)
        h1s, xn, gates = _post(oa, ob, oc, hs, lw, _token_tile(ts, 512))
        hs = _moe(xn, gates, h1s, ps, i, lw, _token_tile(ts, 1024))

        hp = _moe_combine(dispatched, h1, pp, i, lw)
        akf, avf, bkf, bvf, ckf, cvf, logf = sr[9:]
        rows = (akf.reshape(ns, t_new, H_A, HEAD_DIM), avf.reshape(ns, t_new, H_A, HEAD_DIM),
                logf[:, :H_A].reshape(ns, t_new, H_A),
                bkf.reshape(ns, t_new, H_B, HEAD_DIM), bvf.reshape(ns, t_new, H_B, HEAD_DIM),
                ckf.reshape(ns, t_new, H_C, HEAD_DIM), cvf.reshape(ns, t_new, H_C, HEAD_DIM))
        for j in range(7):
            new_s[j].append(rows[j])

    akt, avt, bkt, bvt, ckt, cvt, lft = rows_p
    outs_p = (_position_major(akt, H_A), _position_major(avt, H_A),
              jnp.transpose(lft.reshape(depth, H_A, nb, s), (0, 2, 3, 1)),
              _position_major(bkt, H_B), _position_major(bvt, H_B),
              _position_major(ckt, H_C), _position_major(cvt, H_C))
    outs_s = [jnp.stack(r, axis=0) for r in new_s]
    return (hp.reshape(nb, s, d), hs.reshape(ns, t_new, d), *outs_p, *outs_s)
```

```python
import functools
import math

import jax
import jax.numpy as jnp
from jax import lax
from jax.experimental import pallas as pl
from jax.experimental.pallas import tpu as pltpu
from jax.experimental.pallas import tpu_sc as plsc

F32 = jnp.float32
BF16 = jnp.bfloat16

D_MODEL = 1024
HEAD_DIM = 64
H_A, H_B, H_C = 6, 5, 5
W_A, W_B, W_C = H_A * HEAD_DIM, H_B * HEAD_DIM, H_C * HEAD_DIM
GROUP_W = 384
PAIR_W = 2 * HEAD_DIM
N_PAIRS = GROUP_W // PAIR_W
CHUNK = 64
BAND_CHUNKS = 8
BAND = BAND_CHUNKS * CHUNK
REL_CLIP = 128
N_GROUPS = 4
EXPERTS_PER_GROUP = 4
N_EXPERTS = N_GROUPS * EXPERTS_PER_GROUP
EPS = 1e-6
ATTN_SCALE = HEAD_DIM ** -0.5
LOG2E = math.log2(math.e)
Q_SCALE = ATTN_SCALE * LOG2E
NEG_INF = -1e30
LANES = 128
SUBLANES = 8
VMEM_LIMIT = 48 * 1024 * 1024
EXPERTS_PER_STEP = 4
MOE_VMEM_LIMIT = 58 * 1024 * 1024

ATT_TILE = 256
A_QTILE = 512
C_QTILE = 256
PROJ_TILE = 1024
PROJ_VMEM_LIMIT = 56 * 1024 * 1024
AUG_SLOTS = 8

_NT = (((1,), (1,)), ((), ()))


def _cparams(*sem):
    return pltpu.CompilerParams(dimension_semantics=sem, vmem_limit_bytes=VMEM_LIMIT)


def _dot(a, b):
    return jnp.dot(a, b, preferred_element_type=F32)


def _dot_nt(a, b):
    return lax.dot_general(a, b, _NT, preferred_element_type=F32)


def _split3(x):
    hi = x.astype(BF16)
    r1 = x - hi.astype(F32)
    mid = r1.astype(BF16)
    lo = (r1 - mid.astype(F32)).astype(BF16)
    return hi, mid, lo


def _split2(x):
    hi = x.astype(BF16)
    lo = (x - hi.astype(F32)).astype(BF16)
    return hi, lo


def _lane_iota(n=LANES):
    return lax.broadcasted_iota(jnp.int32, (1, n), 1)


def _half_mask(half):
    lane = _lane_iota()
    return (lane < HEAD_DIM) if half == 0 else (lane >= HEAD_DIM)


def _pair_slice(h):
    return slice((h // 2) * PAIR_W, (h // 2 + 1) * PAIR_W)


def _softplus2(z2):
    return jnp.maximum(z2, 0.0) + jnp.log2(1.0 + jnp.exp2(-jnp.abs(z2)))


def _sigmoid(z):
    return 1.0 / (1.0 + jnp.exp(-z))


def _rms(x, g):
    return x * lax.rsqrt(jnp.mean(x * x, axis=-1, keepdims=True) + EPS) * g


def _pair_rms(z, gain):
    first = _half_mask(0)
    outs = []
    for j in range(N_PAIRS):
        blk = z[:, j * PAIR_W:(j + 1) * PAIR_W]
        sq = blk * blk
        lo = jnp.sum(jnp.where(first, sq, 0.0), axis=-1, keepdims=True)
        hi = jnp.sum(jnp.where(first, 0.0, sq), axis=-1, keepdims=True)
        ms = jnp.where(first, lo, hi) * (1.0 / HEAD_DIM)
        outs.append(blk * lax.rsqrt(ms + EPS) * gain[:, j * PAIR_W:(j + 1) * PAIR_W])
    return jnp.concatenate(outs, axis=-1)


def _projector(x_ref, gn_ref, w_ref):
    xb = _rms(x_ref[...], gn_ref[...]).astype(BF16)
    return lambda g, width=GROUP_W: _dot_nt(xb, w_ref[g * GROUP_W:g * GROUP_W + width, :])


def _log_forget(zf, bf_ref):
    zf = zf + bf_ref[...]
    return jnp.minimum(zf, 0.0) - jnp.log(1.0 + jnp.exp(-jnp.abs(zf)))


def _inproj_sample_kernel(x_ref, gn_ref, w_ref, gq_ref, gk_ref, bf_ref,
                          qa_ref, qb_ref, qc_ref, ka_ref, kb_ref, kc_ref, va_ref, vb_ref, vc_ref,
                          akf_ref, avf_ref, bkf_ref, bvf_ref, ckf_ref, cvf_ref, logf_ref):
    proj = _projector(x_ref, gn_ref, w_ref)
    qa_ref[...] = (_pair_rms(proj(0), gq_ref[:, :GROUP_W]) * Q_SCALE).astype(BF16)
    qb_ref[...] = (_pair_rms(proj(1), gq_ref[:, GROUP_W:]) * Q_SCALE).astype(BF16)
    qc_ref[...] = (proj(2) * Q_SCALE).astype(BF16)
    for g, norm, bf_ref_, f_ref, width in (
            (3, gk_ref[:, :GROUP_W], ka_ref, akf_ref, W_A), (4, gk_ref[:, GROUP_W:], kb_ref, bkf_ref, W_B),
            (5, None, kc_ref, ckf_ref, W_C), (6, None, va_ref, avf_ref, W_A),
            (7, None, vb_ref, bvf_ref, W_B), (8, None, vc_ref, cvf_ref, W_C)):
        z = proj(g)
        if norm is not None:
            z = _pair_rms(z, norm)
        f_ref[...] = z[:, :width]
        bf_ref_[...] = z.astype(BF16)
    logf_ref[...] = _log_forget(proj(9, LANES), bf_ref)


def _inproj_sample(x, lw):
    t = x.shape[0]
    row = lambda w: pl.BlockSpec((t, w), lambda i: (0, 0))
    full = lambda a: pl.BlockSpec(a.shape, lambda i: (0,) * a.ndim)
    widths = [GROUP_W] * 9 + [W_A, W_A, W_B, W_B, W_C, W_C, LANES]
    out_shape = [jax.ShapeDtypeStruct((t, w), BF16 if j < 9 else F32) for j, w in enumerate(widths)]
    ins = (x, lw['norm_mix'], lw['w_in'], lw['gq'], lw['gk'], lw['b_f'])
    return pl.pallas_call(
        _inproj_sample_kernel, grid=(1,),
        in_specs=[row(D_MODEL)] + [full(a) for a in ins[1:]],
        out_specs=[row(w) for w in widths], out_shape=out_shape,
        compiler_params=_cparams("arbitrary"), name="inproj_sample")(*ins)


N_PROMPT_ROWS = 7


def _inproj_prompt_kernel(*refs, n_alias):
    x_ref, gn_ref, w_ref, gq_ref, gk_ref, bf_ref = refs[:6]
    (qat_ref, ka_ref, vat_ref, qb_ref, kb_ref, vb_ref, qc_ref, kc_ref, vc_ref, logf_ref,
     akt_ref, avt_ref, bkt_ref, bvt_ref, ckt_ref, cvt_ref, lft_ref) = refs[6 + n_alias:]
    proj = _projector(x_ref, gn_ref, w_ref)
    qat_ref[...] = jnp.transpose(_pair_rms(proj(0), gq_ref[:, :GROUP_W]) * Q_SCALE).astype(BF16)
    qb_ref[...] = (_pair_rms(proj(1), gq_ref[:, GROUP_W:]) * Q_SCALE).astype(BF16)
    qc_ref[...] = (proj(2) * Q_SCALE).astype(BF16)
    ka = _pair_rms(proj(3), gk_ref[:, :GROUP_W])
    ka_ref[...] = ka.astype(BF16)
    akt_ref[0, 0] = jnp.transpose(ka)
    kb = _pair_rms(proj(4), gk_ref[:, GROUP_W:])
    kb_ref[...] = kb.astype(BF16)
    kc = proj(5)
    kc_ref[...] = kc.astype(BF16)
    ckt_ref[0, 0] = jnp.transpose(kc)[:W_C]
    vat = jnp.transpose(proj(6))
    avt_ref[0, 0] = vat
    for j in range(vat_ref.shape[1]):
        vat_ref[0, j] = vat[:, j * ATT_TILE:(j + 1) * ATT_TILE].astype(BF16)
    vb = proj(7)
    vb_ref[...] = vb.astype(BF16)
    vc = proj(8)
    vc_ref[...] = vc.astype(BF16)
    cvt_ref[0, 0] = jnp.transpose(vc)[:W_C]

    @pl.when(pl.program_id(1) == pl.num_programs(1) - 1)
    def _():
        keep = kb.shape[0] - BAND
        bkt_ref[0, 0] = jnp.transpose(kb)[:W_B, keep:]
        bvt_ref[0, 0] = jnp.transpose(vb)[:W_B, keep:]

    lf = _log_forget(proj(9, LANES), bf_ref)
    logf_ref[...] = lf
    lft_ref[0] = jnp.transpose(lf)[:H_A]


def _inproj_prompt(x, lw, layer, depth, nb, s, prev_rows):
    tm = PROJ_TILE
    nt = s // tm
    t = nb * s
    per = tm // ATT_TILE
    row = lambda w: pl.BlockSpec((tm, w), lambda b, i: (b * nt + i, 0))
    full = lambda a: pl.BlockSpec(a.shape, lambda b, i: (0,) * a.ndim, pipeline_mode=pl.Buffered(1))
    feat = lambda w: pl.BlockSpec((1, 1, w, tm), lambda b, i: (layer, b, 0, i))
    band = lambda w: pl.BlockSpec((1, 1, w, BAND), lambda b, i: (layer, b, 0, 0))
    tok = jax.ShapeDtypeStruct((t, GROUP_W), BF16)
    out_shape = [
        jax.ShapeDtypeStruct((GROUP_W, t), BF16), tok,
        jax.ShapeDtypeStruct((nb, s // ATT_TILE, GROUP_W, ATT_TILE), BF16),
        tok, tok, tok, tok, tok, tok, jax.ShapeDtypeStruct((t, LANES), F32),
        jax.ShapeDtypeStruct((depth, nb, W_A, s), F32), jax.ShapeDtypeStruct((depth, nb, W_A, s), F32),
        jax.ShapeDtypeStruct((depth, nb, W_B, BAND), F32), jax.ShapeDtypeStruct((depth, nb, W_B, BAND), F32),
        jax.ShapeDtypeStruct((depth, nb, W_C, s), F32), jax.ShapeDtypeStruct((depth, nb, W_C, s), F32),
        jax.ShapeDtypeStruct((depth, H_A, t), F32)]
    out_specs = [
        pl.BlockSpec((GROUP_W, tm), lambda b, i: (0, b * nt + i)), row(GROUP_W),
        pl.BlockSpec((1, per, GROUP_W, ATT_TILE), lambda b, i: (b, i, 0, 0)),
        row(GROUP_W), row(GROUP_W), row(GROUP_W), row(GROUP_W), row(GROUP_W), row(GROUP_W), row(LANES),
        feat(W_A), feat(W_A), band(W_B), band(W_B), feat(W_C), feat(W_C),
        pl.BlockSpec((1, H_A, tm), lambda b, i: (layer, 0, b * nt + i))]
    ins = [x, lw['norm_mix'], lw['w_in'], lw['gq'], lw['gk'], lw['b_f']]
    in_specs = [row(D_MODEL)] + [full(a) for a in ins[1:]]
    aliases = {}
    n_alias = 0
    if prev_rows is not None:
        n_alias = N_PROMPT_ROWS
        first_row_out = len(out_shape) - N_PROMPT_ROWS
        for j, a in enumerate(prev_rows):
            aliases[len(ins)] = first_row_out + j
            ins.append(a)
            in_specs.append(pl.BlockSpec(memory_space=pl.ANY))
    return pl.pallas_call(
        functools.partial(_inproj_prompt_kernel, n_alias=n_alias), grid=(nb, nt),
        in_specs=in_specs, out_specs=out_specs, out_shape=out_shape,
        input_output_aliases=aliases,
        compiler_params=pltpu.CompilerParams(dimension_semantics=("parallel", "arbitrary"),
                                             vmem_limit_bytes=PROJ_VMEM_LIMIT),
        name="inproj_prompt")(*ins)


def _forget_aug_kernel(logf_ref, kaug_ref, qaug_ref, *, nblk):
    tb = ATT_TILE
    r = lax.broadcasted_iota(jnp.int32, (tb, tb), 0)
    c = lax.broadcasted_iota(jnp.int32, (tb, tb), 1)
    tri = jnp.where(c <= r, 1.0, 0.0).astype(BF16)
    src = lax.broadcasted_iota(jnp.int32, (LANES, LANES), 0)
    dst = lax.broadcasted_iota(jnp.int32, (LANES, LANES), 1)

    def place(slot):
        return jnp.where((dst == AUG_SLOTS * src + slot) & (src < H_A), 1.0, 0.0).astype(BF16)

    lane = _lane_iota()
    slot = jnp.bitwise_and(lane, AUG_SLOTS - 1)
    used = lane < AUG_SLOTS * H_A
    key_const = jnp.where(used & (slot >= 3) & (slot < 6), 1.0, 0.0)
    qry_const = jnp.where(used & (slot < 3), -1.0, 0.0)
    carry = jnp.zeros((1, LANES), F32)
    for blk in range(nblk):
        rows = slice(blk * tb, (blk + 1) * tb)
        hi, mid, lo = _split3(logf_ref[rows, :])
        cs = (_dot(tri, hi) + _dot(tri, mid) + _dot(tri, lo)) + carry
        carry = cs[tb - 1:tb, :]
        hi, mid, lo = _split3(cs * LOG2E)
        kaug = _dot(hi, place(0)) + _dot(mid, place(1)) + _dot(lo, place(2)) + key_const
        kaug_ref[rows, :] = kaug.astype(BF16)
        qaug = _dot(hi, place(3)) + _dot(mid, place(4)) + _dot(lo, place(5)) + qry_const
        qaug_ref[:, rows] = jnp.transpose(qaug).astype(BF16)


def _forget_aug(logf, nb, s):
    return pl.pallas_call(
        functools.partial(_forget_aug_kernel, nblk=s // ATT_TILE), grid=(nb,),
        in_specs=[pl.BlockSpec((s, LANES), lambda b: (b, 0))],
        out_specs=[pl.BlockSpec((s, LANES), lambda b: (b, 0)), pl.BlockSpec((LANES, s), lambda b: (0, b))],
        out_shape=[jax.ShapeDtypeStruct((nb * s, LANES), BF16), jax.ShapeDtypeStruct((LANES, nb * s), BF16)],
        compiler_params=_cparams("parallel"), name="forget_aug")(logf)


def _suffix_sum_kernel(x_ref, o_ref):
    p = x_ref.shape[1]
    j = lax.broadcasted_iota(jnp.int32, (p, p), 0)
    s = lax.broadcasted_iota(jnp.int32, (p, p), 1)
    tri = jnp.where(j > s, 1.0, 0.0).astype(BF16)
    hi, mid, lo = _split3(x_ref[...])
    o_ref[...] = _dot(hi, tri) + _dot(mid, tri) + _dot(lo, tri)


def _suffix_sum(x):
    return pl.pallas_call(
        _suffix_sum_kernel, out_shape=jax.ShapeDtypeStruct(x.shape, F32),
        compiler_params=pltpu.CompilerParams(vmem_limit_bytes=VMEM_LIMIT), name="suffix_sum")(x)


def _attn_a_kernel(qt_ref, k_ref, vt_ref, kaug_ref, qaug_ref, o_ref, qp_ref, m_ref, l_ref, acc_ref):
    tq = qt_ref.shape[1]
    tk = ATT_TILE
    qi = pl.program_id(1)
    row = lax.broadcasted_iota(jnp.int32, (PAIR_W, 1), 0)
    aug = qaug_ref[...]
    for h in range(H_A):
        q2 = qt_ref[_pair_slice(h), :]
        half = (row < HEAD_DIM) if h % 2 == 0 else (row >= HEAD_DIM)
        qp_ref[h, :PAIR_W, :] = jnp.where(half, q2, jnp.zeros_like(q2))
        mine = (row >= AUG_SLOTS * h) & (row < AUG_SLOTS * (h + 1))
        qp_ref[h, PAIR_W:, :] = jnp.where(mine, aug, jnp.zeros_like(aug))
    m_ref[...] = jnp.full(m_ref.shape, NEG_INF, F32)
    l_ref[...] = jnp.zeros(l_ref.shape, F32)
    acc_ref[...] = jnp.zeros(acc_ref.shape, F32)

    def step(kt, key_offset):
        ks = pl.multiple_of(kt * tk, tk)
        ka = kaug_ref[pl.ds(ks, tk), :]
        tile_scores = [_dot(jnp.concatenate([k_ref[pl.ds(ks, tk), _pair_slice(h)], ka], axis=1), qp_ref[h])
                       for h in range(H_A)]
        probs, alphas = [], []
        for h in range(H_A):
            s = tile_scores[h]
            if key_offset is not None:
                key = lax.broadcasted_iota(jnp.int32, (tk, tq), 0)
                qry = lax.broadcasted_iota(jnp.int32, (tk, tq), 1)
                s = jnp.where(key + key_offset <= qry, s, NEG_INF)
            m_prev = m_ref[h]
            m_new = jnp.maximum(m_prev, jnp.max(s, axis=0, keepdims=True))
            alpha = jnp.exp2(m_prev - m_new)
            p = jnp.exp2(s - m_new)
            l_ref[h] = alpha * l_ref[h] + jnp.sum(p, axis=0, keepdims=True)
            m_ref[h] = m_new
            alphas.append(alpha)
            probs.append(p.astype(BF16))
        for h in range(H_A):
            acc_ref[h] = alphas[h] * acc_ref[h] + _dot(vt_ref[0, kt, _pair_slice(h), :], probs[h])

    def body(kt, carry):
        step(kt, None)
        return carry

    per = tq // tk
    lax.fori_loop(0, qi * per, body, 0)
    for j in range(per):
        step(qi * per + j, j * tk)
    for pair in range(N_PAIRS):
        h = 2 * pair
        out_t = jnp.where(row < HEAD_DIM, acc_ref[h] * (1.0 / l_ref[h]), acc_ref[h + 1] * (1.0 / l_ref[h + 1]))
        o_ref[:, pair * PAIR_W:(pair + 1) * PAIR_W] = jnp.transpose(out_t)


def _attn_a(qat, ka, vat, kaug, qaug, nb, s):
    tq = A_QTILE
    nq = s // tq
    return pl.pallas_call(
        _attn_a_kernel, grid=(nb, nq),
        in_specs=[pl.BlockSpec((GROUP_W, tq), lambda b, i: (0, b * nq + i)),
                  pl.BlockSpec((s, GROUP_W), lambda b, i: (b, 0)),
                  pl.BlockSpec((1, s // ATT_TILE, GROUP_W, ATT_TILE), lambda b, i: (b, 0, 0, 0)),
                  pl.BlockSpec((s, LANES), lambda b, i: (b, 0)),
                  pl.BlockSpec((LANES, tq), lambda b, i: (0, b * nq + i))],
        out_specs=pl.BlockSpec((tq, GROUP_W), lambda b, i: (b * nq + i, 0)),
        out_shape=jax.ShapeDtypeStruct((nb * s, GROUP_W), F32),
        scratch_shapes=[pltpu.VMEM((H_A, 2 * PAIR_W, tq), BF16), pltpu.VMEM((H_A, 1, tq), F32),
                        pltpu.VMEM((H_A, 1, tq), F32), pltpu.VMEM((H_A, PAIR_W, tq), F32)],
        compiler_params=_cparams("parallel", "arbitrary"), name="attn_a")(qat, ka, vat, kaug, qaug)


def _masked_q(q_ref, qm_ref, n_heads):
    for h in range(n_heads):
        q2 = q_ref[:, _pair_slice(h)]
        qm_ref[h] = jnp.where(_half_mask(h % 2), q2, jnp.zeros_like(q2))


def _attn_b_kernel(q_ref, k_ref, v_ref, tab_ref, o_ref):
    tq = ATT_TILE
    win = tq + BAND
    i = pl.program_id(1)
    var = jnp.minimum(i, BAND // tq)
    ws = pl.multiple_of(jnp.maximum(i * tq - BAND, 0), tq)
    heads = range(H_B)
    scores = []
    for h in heads:
        q2 = q_ref[:, _pair_slice(h)]
        qm = jnp.where(_half_mask(h % 2), q2, jnp.zeros_like(q2))
        scores.append(_dot_nt(qm, k_ref[pl.ds(ws, win), _pair_slice(h)]) + tab_ref[var, h])
    probs, norms = [], []
    for h in heads:
        p = jnp.exp2(scores[h] - jnp.max(scores[h], axis=-1, keepdims=True))
        norms.append(1.0 / jnp.sum(p, axis=-1, keepdims=True))
        probs.append(p.astype(BF16))
    res = [_dot(probs[h], v_ref[pl.ds(ws, win), _pair_slice(h)]) * norms[h] for h in heads]
    res.append(jnp.zeros((tq, PAIR_W), F32))
    first = _half_mask(0)
    for pair in range(N_PAIRS):
        o_ref[:, pair * PAIR_W:(pair + 1) * PAIR_W] = jnp.where(first, res[2 * pair], res[2 * pair + 1])


def _suffix_tri(n):
    j = lax.broadcasted_iota(jnp.int32, (2 * n, n), 0)
    s = lax.broadcasted_iota(jnp.int32, (2 * n, n), 1)
    return jnp.where(jnp.where(j >= n, j - n, j) >= s, 1.0, 0.0).astype(BF16)


def _stick_scores(z, tri, seen):
    sp = _softplus2(z)
    if seen is not None:
        sp = jnp.where(seen, sp, 0.0)
    hi, lo = _split2(sp)
    n = sp.shape[1]
    if n % LANES == 0:
        return _dot(jnp.concatenate([hi, lo], axis=1), tri)
    return _dot(hi, tri[:n]) + _dot(lo, tri[:n])


def _stick_weights(z, s_in, carry, seen):
    a = jnp.exp2(z - (s_in + carry))
    if seen is not None:
        a = jnp.where(seen, a, 0.0)
    return a.astype(BF16)


def _attn_c_kernel(q_ref, k_ref, v_ref, o_ref, qm_ref, carry_ref, acc_ref):
    tq = q_ref.shape[0]
    tk = ATT_TILE
    qi = pl.program_id(1)
    _masked_q(q_ref, qm_ref, H_C)
    carry_ref[...] = jnp.zeros(carry_ref.shape, F32)
    acc_ref[...] = jnp.zeros(acc_ref.shape, F32)
    heads = range(H_C)

    def step(kt, key_offset):
        ks = pl.multiple_of(kt * tk, tk)
        zs = [_dot_nt(qm_ref[h], k_ref[pl.ds(ks, tk), _pair_slice(h)]) for h in heads]
        tri = _suffix_tri(tk)
        seen = None
        if key_offset is not None:
            r = lax.broadcasted_iota(jnp.int32, (tq, tk), 0)
            c = lax.broadcasted_iota(jnp.int32, (tq, tk), 1)
            seen = c + key_offset < r
        sums = [_stick_scores(zs[h], tri, seen) for h in heads]
        weights = []
        for h in heads:
            carry = carry_ref[h][:, 0:1]
            weights.append(_stick_weights(zs[h], sums[h], carry, seen))
            carry_ref[h] = jnp.broadcast_to(carry + sums[h][:, 0:1], (tq, LANES))
        for h in heads:
            acc_ref[h] += _dot(weights[h], v_ref[pl.ds(ks, tk), _pair_slice(h)])

    per = tq // tk

    def body(j, c):
        step(qi * per - 1 - j, None)
        return c

    for j in reversed(range(per)):
        step(qi * per + j, j * tk)
    lax.fori_loop(0, qi * per, body, 0)
    first = _half_mask(0)
    for pair in range(N_PAIRS):
        h = 2 * pair
        second = acc_ref[h + 1] if h + 1 < H_C else jnp.zeros((tq, PAIR_W), F32)
        o_ref[:, pair * PAIR_W:(pair + 1) * PAIR_W] = jnp.where(first, acc_ref[h], second)


def _prompt_attention(kernel, name, q, k, v, extra, extra_specs, scratch, nb, s, tq):
    nq = s // tq
    qspec = pl.BlockSpec((tq, GROUP_W), lambda b, i: (b * nq + i, 0))
    kvspec = pl.BlockSpec((s, GROUP_W), lambda b, i: (b, 0))
    return pl.pallas_call(
        kernel, grid=(nb, nq),
        in_specs=[qspec, kvspec, kvspec] + extra_specs,
        out_specs=qspec, out_shape=jax.ShapeDtypeStruct((nb * s, GROUP_W), F32),
        scratch_shapes=scratch,
        compiler_params=_cparams("parallel", "arbitrary"), name=name)(q, k, v, *extra)


def _sample_heads(n_heads, q_ref, width):
    out = []
    for h in range(n_heads):
        lo = (h // 2) * PAIR_W
        span = slice(lo, min(lo + PAIR_W, width))
        q2 = q_ref[:, span]
        if span.stop - span.start == PAIR_W:
            q2 = jnp.where(_half_mask(h % 2), q2, jnp.zeros_like(q2))
        out.append((h, span, q2))
    return out


def _store_heads(o_ref, res, n_heads):
    for pair in range(N_PAIRS):
        lo = pair * PAIR_W
        h = 2 * pair
        if h + 1 < n_heads:
            o_ref[:, lo:lo + PAIR_W] = jnp.where(_half_mask(0), res[h], res[h + 1])
        else:
            o_ref[:, lo:lo + HEAD_DIM] = res[h]
            o_ref[:, lo + HEAD_DIM:lo + PAIR_W] = jnp.zeros_like(res[h])


def _joint_softmax_pv(s_c, s_n, vt_c, v_n):
    m = jnp.maximum(jnp.max(s_c, axis=-1, keepdims=True), jnp.max(s_n, axis=-1, keepdims=True))
    p_c = jnp.exp2(s_c - m)
    p_n = jnp.exp2(s_n - m)
    l = jnp.sum(p_c, axis=-1, keepdims=True) + jnp.sum(p_n, axis=-1, keepdims=True)
    return (_dot_nt(p_c.astype(BF16), vt_c) + _dot(p_n.astype(BF16), v_n)) / l


def _sample_attn_kernel(qa_ref, qb_ref, qc_ref, kan_ref, kbn_ref, kcn_ref, van_ref, vbn_ref, vcn_ref,
                        cak_ref, cav_ref, cbk_ref, cbv_ref, cck_ref, ccv_ref,
                        rsum_ref, logf_ref, tabc_ref, tabn_ref,
                        oa_ref, ob_ref, oc_ref):
    t = qa_ref.shape[0]
    past = cak_ref.shape[3]
    r = lax.broadcasted_iota(jnp.int32, (t, t), 0)
    c = lax.broadcasted_iota(jnp.int32, (t, t), 1)

    ltri = jnp.where(c <= r, 1.0, 0.0).astype(BF16)
    hi, mid, lo = _split3(logf_ref[...])
    pcol = (_dot(ltri, hi) + _dot(ltri, mid) + _dot(ltri, lo)) * LOG2E
    prow = jnp.transpose(jnp.concatenate([pcol, jnp.zeros((LANES - t, LANES), F32)], axis=0))
    heads = _sample_heads(H_A, qa_ref, W_A)
    kts = {sp.start: cak_ref[0, 0, sp, :].astype(BF16) for _, sp, _ in heads}
    vts = {sp.start: cav_ref[0, 0, sp, :].astype(BF16) for _, sp, _ in heads}
    s_c = [_dot(qm, kts[sp.start]) + (pcol[:, h:h + 1] + rsum_ref[0, 0, h:h + 1, :] * LOG2E)
           for h, sp, qm in heads]
    s_n = [jnp.where(c <= r, _dot_nt(qm, kan_ref[:, sp]) + (pcol[:, h:h + 1] - prow[h:h + 1, 0:t]), NEG_INF)
           for h, sp, qm in heads]
    _store_heads(oa_ref, [_joint_softmax_pv(s_c[h], s_n[h], vts[sp.start], van_ref[:, sp])
                          for h, sp, _ in heads], H_A)

    heads = _sample_heads(H_B, qb_ref, W_B)
    kts = {sp.start: cbk_ref[0, 0, sp, :].astype(BF16) for _, sp, _ in heads}
    vts = {sp.start: cbv_ref[0, 0, sp, :].astype(BF16) for _, sp, _ in heads}
    s_c = [_dot(qm, kts[sp.start]) + tabc_ref[h] for h, sp, qm in heads]
    s_n = [_dot_nt(qm, kbn_ref[:, sp]) + tabn_ref[h] for h, sp, qm in heads]
    _store_heads(ob_ref, [_joint_softmax_pv(s_c[h], s_n[h], vts[sp.start], vbn_ref[:, sp])
                          for h, sp, _ in heads], H_B)

    heads = _sample_heads(H_C, qc_ref, W_C)
    nblk = past // ATT_TILE
    tri_n = _suffix_tri(t)
    tri_c = _suffix_tri(ATT_TILE)
    seen = c < r
    kts = {sp.start: cck_ref[0, 0, sp, :].astype(BF16) for _, sp, _ in heads}
    vts = {sp.start: ccv_ref[0, 0, sp, :].astype(BF16) for _, sp, _ in heads}
    z_n = [_dot_nt(qm, kcn_ref[:, sp]) for _, sp, qm in heads]
    z_c = [_dot(qm, kts[sp.start]) for _, sp, qm in heads]
    sum_n = [_stick_scores(z, tri_n, seen) for z in z_n]
    sum_c = [[_stick_scores(z[:, b * ATT_TILE:(b + 1) * ATT_TILE], tri_c, None) for b in range(nblk)]
             for z in z_c]
    res = []
    for h, sp, _ in heads:
        acc = _dot(_stick_weights(z_n[h], sum_n[h], 0.0, seen), vcn_ref[:, sp])
        carry = sum_n[h][:, 0:1]
        for b in reversed(range(nblk)):
            cols = slice(b * ATT_TILE, (b + 1) * ATT_TILE)
            a = _stick_weights(z_c[h][:, cols], sum_c[h][b], carry, None)
            acc = acc + _dot_nt(a, vts[sp.start][:, cols])
            carry = carry + sum_c[h][b][:, 0:1]
        res.append(acc)
    _store_heads(oc_ref, res, H_C)


def _sample_attention(proj, caches, layer, rsum, tabc, tabn, nb, t):
    qa, qb, qc, ka, kb, kc, va, vb, vc = proj[:9]
    logf = proj[15]
    new = pl.BlockSpec((t, GROUP_W), lambda b: (b, 0))
    cache = lambda a: pl.BlockSpec((1, 1) + a.shape[2:], lambda b: (layer, b, 0, 0))
    full = lambda a: pl.BlockSpec(a.shape, lambda b: (0,) * a.ndim)
    return pl.pallas_call(
        _sample_attn_kernel, grid=(nb,),
        in_specs=[new] * 9 + [cache(a) for a in caches] + [
            cache(rsum), pl.BlockSpec((t, LANES), lambda b: (b, 0)), full(tabc), full(tabn)],
        out_specs=[new] * 3,
        out_shape=[jax.ShapeDtypeStruct((nb * t, GROUP_W), F32)] * 3,
        compiler_params=_cparams("parallel"), name="sample_attn")(
            qa, qb, qc, ka, kb, kc, va, vb, vc, *caches, rsum, logf, tabc, tabn)


def _masked_max(x, mask):
    return jnp.max(jnp.where(mask, x, -jnp.inf), axis=-1, keepdims=True)


def _first_lane(mask, lane):
    return jnp.min(jnp.where(mask, lane, float(LANES)), axis=-1, keepdims=True)


def _route(logits):
    lane = _lane_iota().astype(F32)
    is_g = (lane >= N_EXPERTS) & (lane < N_EXPERTS + N_GROUPS)
    gmax = _masked_max(logits, is_g)
    p_g = 1.0 / jnp.sum(jnp.where(is_g, jnp.exp(logits - gmax), 0.0), axis=-1, keepdims=True)
    g_sel = _first_lane(is_g & (logits == gmax), lane) - N_EXPERTS
    lo = g_sel * EXPERTS_PER_GROUP
    in_g = (lane >= lo) & (lane < lo + EXPERTS_PER_GROUP)
    l1 = _masked_max(logits, in_g)
    i1 = _first_lane(in_g & (logits == l1), lane)
    rest = in_g & (lane != i1)
    l2 = _masked_max(logits, rest)
    i2 = _first_lane(rest & (logits == l2), lane)
    e2 = jnp.exp(l2 - l1)
    w1 = p_g / (1.0 + e2)
    w2 = p_g * e2 / (1.0 + e2)
    gates = jnp.where(lane == i1, w1, jnp.where(lane == i2, w2, 0.0))
    return jnp.where(lane == g_sel + N_EXPERTS, 1.0, gates)


def _post_kernel(oa_ref, ob_ref, oc_ref, h_ref, gmix_ref, wout_ref, nffn_ref, wrh_ref, wrl_ref, br_ref,
                 h1_ref, gates_ref, *x_refs, packed):
    def gnorm(o, width):
        ms = jnp.sum(o * o, axis=-1, keepdims=True) * (1.0 / width)
        return o * lax.rsqrt(ms + EPS)

    h1 = h_ref[...]
    for g, (o_ref, width) in enumerate(((oa_ref, W_A), (ob_ref, W_B), (oc_ref, W_C))):
        merged = (gnorm(o_ref[...], width) * gmix_ref[g]).astype(BF16)
        h1 = h1 + _dot(merged, wout_ref[g])
    h1_ref[...] = h1
    xf = _rms(h1, nffn_ref[...])
    hi, lo = _split2(xf)
    if packed:
        bits = pltpu.bitcast(hi.astype(F32), jnp.uint32)
        for c, ref in enumerate(x_refs):
            a = bits[:, (2 * c) * SC_ROW_WORDS:(2 * c + 1) * SC_ROW_WORDS]
            b = bits[:, (2 * c + 1) * SC_ROW_WORDS:(2 * c + 2) * SC_ROW_WORDS]
            ref[...] = jnp.bitwise_or(a, jnp.right_shift(b, jnp.uint32(16)))
    else:
        x_refs[0][...] = hi
    logits = _dot(hi, wrh_ref[...]) + _dot(lo, wrh_ref[...]) + _dot(hi, wrl_ref[...]) + br_ref[...]
    gates_ref[...] = _route(logits)


def _post(oa, ob, oc, h, lw, tm, n_chunks=0):
    t = h.shape[0]
    row = lambda w: pl.BlockSpec((tm, w), lambda i: (i, 0))
    full = lambda a: pl.BlockSpec(a.shape, lambda i: (0,) * a.ndim)
    ws = (lw['g_mix'], lw['w_out'], lw['norm_ffn'], lw['w_r_hi'], lw['w_r_lo'], lw['b_r'])
    if n_chunks:
        x_specs = [row(SC_ROW_WORDS)] * n_chunks
        x_shapes = [jax.ShapeDtypeStruct((t, SC_ROW_WORDS), jnp.uint32)] * n_chunks
    else:
        x_specs, x_shapes = [row(D_MODEL)], [jax.ShapeDtypeStruct((t, D_MODEL), BF16)]
    return pl.pallas_call(
        functools.partial(_post_kernel, packed=bool(n_chunks)), grid=(t // tm,),
        in_specs=[row(GROUP_W)] * 3 + [row(D_MODEL)] + [full(a) for a in ws],
        out_specs=[row(D_MODEL), row(LANES)] + x_specs,
        out_shape=[jax.ShapeDtypeStruct((t, D_MODEL), F32), jax.ShapeDtypeStruct((t, LANES), F32)] + x_shapes,
        compiler_params=_cparams("parallel"), name="post")(oa, ob, oc, h, *ws)


def _moe_kernel(xn_ref, gates_ref, wg_ref, wu_ref, wd_ref, h1_ref, p_ref, wple_ref, nple_ref, wpg_ref,
                out_ref, acc_ref):
    step = pl.program_id(1)

    @pl.when(step == 0)
    def _():
        acc_ref[...] = jnp.zeros_like(acc_ref)

    x = xn_ref[...]
    gates = gates_ref[...]
    acts = []
    for k in range(EXPERTS_PER_STEP):
        e = step * EXPERTS_PER_STEP + k
        g = _dot(x, wg_ref[k])
        u = _dot(x, wu_ref[k])
        gate = jnp.sum(jnp.where(_lane_iota() == e, gates, 0.0), axis=-1, keepdims=True)
        acts.append(((g * _sigmoid(g)) * u * gate).astype(BF16))
    acc_ref[...] += _dot(jnp.concatenate(acts, axis=1), wd_ref[...])

    @pl.when(step == pl.num_programs(1) - 1)
    def _():
        h2 = h1_ref[...] + acc_ref[...]
        gate_ple = _sigmoid(_dot(_rms(h2, nple_ref[...]).astype(BF16), wpg_ref[...]))
        out_ref[...] = h2 + _dot(p_ref[0].astype(BF16), wple_ref[...]) * gate_ple


def _moe(xn, gates, h1, p_all, layer, lw, tm):
    t = xn.shape[0]
    row = lambda w: pl.BlockSpec((tm, w), lambda i, e: (i, 0))
    once = pl.Buffered(1)
    full = lambda a: pl.BlockSpec(a.shape, lambda i, e: (0,) * a.ndim, pipeline_mode=once)
    exp = lambda a: pl.BlockSpec((EXPERTS_PER_STEP,) + a.shape[1:], lambda i, e: (e, 0, 0))
    d_hidden = lw['w_ed'].shape[0] // N_EXPERTS
    return pl.pallas_call(
        _moe_kernel, grid=(t // tm, N_EXPERTS // EXPERTS_PER_STEP),
        in_specs=[row(D_MODEL), row(LANES), exp(lw['w_eg']), exp(lw['w_eu']),
                  pl.BlockSpec((EXPERTS_PER_STEP * d_hidden, D_MODEL), lambda i, e: (e, 0)),
                  pl.BlockSpec((tm, D_MODEL), lambda i, e: (i, 0), pipeline_mode=once),
                  pl.BlockSpec((1, tm, p_all.shape[2]), lambda i, e: (layer, i, 0), pipeline_mode=once),
                  full(lw['w_ple']), full(lw['norm_ple']), full(lw['w_pg'])],
        out_specs=row(D_MODEL), out_shape=jax.ShapeDtypeStruct((t, D_MODEL), F32),
        scratch_shapes=[pltpu.VMEM((tm, D_MODEL), F32)],
        compiler_params=pltpu.CompilerParams(dimension_semantics=("parallel", "arbitrary"),
                                             vmem_limit_bytes=MOE_VMEM_LIMIT), name="moe")(
            xn, gates, lw['w_eg'], lw['w_eu'], lw['w_ed'], h1, p_all, lw['w_ple'], lw['norm_ple'], lw['w_pg'])


SORT_TILE = 512
SC_WINDOW = 128
SC_ROW_WORDS = 256
N_CHUNKS = D_MODEL // SC_ROW_WORDS
N_XCHUNKS = N_CHUNKS // 2


def _plan_kernel(gates_ref, dest_ref, total_ref, *, tb):
    lane = _lane_iota()
    is_g = (lane >= N_EXPERTS) & (lane < N_EXPERTS + N_GROUPS)
    total = jnp.sum(jnp.where(is_g, gates_ref[...], 0.0), axis=0, keepdims=True)
    padded = jnp.floor((total + (SORT_TILE - 1)) * (1.0 / SORT_TILE)) * SORT_TILE
    base = jnp.zeros((1, LANES), F32)
    for g in range(1, N_GROUPS):
        before = jnp.sum(jnp.where(lane < N_EXPERTS + g, padded, 0.0), axis=-1, keepdims=True)
        base = jnp.where(lane == N_EXPERTS + g, before, base)
    r = lax.broadcasted_iota(jnp.int32, (tb, tb), 0)
    c = lax.broadcasted_iota(jnp.int32, (tb, tb), 1)
    ltri = jnp.where(c < r, 1.0, 0.0).astype(BF16)
    carry = jnp.zeros((1, LANES), F32)
    for blk in range(gates_ref.shape[0] // tb):
        rows = slice(blk * tb, (blk + 1) * tb)
        member = jnp.where(is_g, gates_ref[rows, :], 0.0)
        rank = _dot(ltri, member.astype(BF16)) + carry
        carry = carry + jnp.sum(member, axis=0, keepdims=True)
        dest = jnp.sum(member * (base + rank), axis=-1, keepdims=True)
        dest_ref[rows, :] = jnp.broadcast_to(dest, (tb, LANES))
    total_ref[...] = jnp.broadcast_to(total, total_ref.shape)


def _plan(gates):
    t = gates.shape[0]
    return pl.pallas_call(
        functools.partial(_plan_kernel, tb=SORT_TILE),
        out_shape=[jax.ShapeDtypeStruct((t, LANES), F32), jax.ShapeDtypeStruct((SUBLANES, LANES), F32)],
        compiler_params=pltpu.CompilerParams(vmem_limit_bytes=VMEM_LIMIT), name="moe_plan")(gates)


def _sc_scatter_rows(x, idx, n_out):
    n, d = x.shape
    mesh = plsc.VectorSubcoreMesh(core_axis_name="core", subcore_axis_name="subcore")

    @functools.partial(pl.kernel, out_type=jax.ShapeDtypeStruct((n_out, d), x.dtype), mesh=mesh, scratch_types=[])
    def scatter(x_hbm, i_hbm, o_hbm):
        def body(x_vmem, i_vmem):
            pltpu.sync_copy(x_vmem, o_hbm.at[i_vmem.at[0]])

        pltpu.emit_pipeline(
            body, grid=(n // SC_WINDOW,),
            in_specs=[pl.BlockSpec((SC_WINDOW, d), index_map=lambda i: (i, 0)),
                      pl.BlockSpec((1, SC_WINDOW), index_map=lambda i: (0, i))],
            out_specs=[], core_axis_name=('core', 'subcore'), dimension_semantics=(pltpu.PARALLEL,))(x_hbm, i_hbm)

    return scatter(x, idx.reshape(1, n))


def _sc_gather_rows(x, idx):
    n = idx.shape[0]
    d = x.shape[1]
    mesh = plsc.VectorSubcoreMesh(core_axis_name="core", subcore_axis_name="subcore")

    @functools.partial(pl.kernel, out_type=jax.ShapeDtypeStruct((n, d), x.dtype), mesh=mesh)
    def gather(x_hbm, i_hbm, o_hbm):
        def body(i_vmem, o_vmem):
            pltpu.sync_copy(x_hbm.at[i_vmem.at[0]], o_vmem)

        pltpu.emit_pipeline(
            body, grid=(n // SC_WINDOW,),
            in_specs=[pl.BlockSpec((1, SC_WINDOW), index_map=lambda i: (0, i))],
            out_specs=[pl.BlockSpec((SC_WINDOW, d), index_map=lambda i: (i, 0))],
            core_axis_name=('core', 'subcore'), dimension_semantics=(pltpu.PARALLEL,))(i_hbm, o_hbm)

    return gather(x, idx.reshape(1, n))


def _experts_sorted_kernel(tg_ref, tv_ref, *refs):
    xs_refs, (gs_ref, wg_ref, wu_ref, wd_ref), ys_refs = (
        refs[:N_XCHUNKS], refs[N_XCHUNKS:N_XCHUNKS + 4], refs[-N_CHUNKS:])
    j = pl.program_id(0)

    @pl.when(tv_ref[j] > 0)
    def _():
        g = tg_ref[j]
        halves = []
        for r in xs_refs:
            w = r[...]
            halves.append(pltpu.bitcast(jnp.bitwise_and(w, jnp.uint32(0xFFFF0000)), F32))
            halves.append(pltpu.bitcast(jnp.left_shift(w, jnp.uint32(16)), F32))
        x = jnp.concatenate(halves, axis=1).astype(BF16)
        src = lax.broadcasted_iota(jnp.int32, (LANES, D_MODEL), 0)
        dst = lax.broadcasted_iota(jnp.int32, (LANES, D_MODEL), 1)
        d_shift = (D_MODEL // EXPERTS_PER_GROUP).bit_length() - 1
        expand = jnp.where(src == g * EXPERTS_PER_GROUP + jnp.right_shift(dst, d_shift), 1.0, 0.0).astype(BF16)
        hi, mid, lo = _split3(gs_ref[...])
        ge = _dot(hi, expand) + _dot(mid, expand) + _dot(lo, expand)
        hg = _dot(x, wg_ref[0])
        act = (hg * _sigmoid(hg)) * _dot(x, wu_ref[0]) * ge
        y = _dot(act.astype(BF16), wd_ref[0])
        for c, ref in enumerate(ys_refs):
            ref[...] = y[:, c * SC_ROW_WORDS:(c + 1) * SC_ROW_WORDS]


def _experts_sorted(xs_chunks, gs, tile_group, tile_valid, lw):
    n_rows = gs.shape[0]
    row = lambda w: pl.BlockSpec((SORT_TILE, w), lambda j, tg, tv: (j, 0))
    grp = lambda a: pl.BlockSpec((1,) + a.shape[1:], lambda j, tg, tv: (tg[j], 0, 0))
    return pl.pallas_call(
        _experts_sorted_kernel,
        grid_spec=pltpu.PrefetchScalarGridSpec(
            num_scalar_prefetch=2, grid=(n_rows // SORT_TILE,),
            in_specs=[row(SC_ROW_WORDS)] * N_XCHUNKS + [row(LANES), grp(lw['w_gg']), grp(lw['w_gu']), grp(lw['w_gd'])],
            out_specs=[row(SC_ROW_WORDS)] * N_CHUNKS),
        out_shape=[jax.ShapeDtypeStruct((n_rows, SC_ROW_WORDS), F32)] * N_CHUNKS,
        compiler_params=_cparams("arbitrary"), name="moe_sorted")(
            tile_group, tile_valid, *xs_chunks, gs, lw['w_gg'], lw['w_gu'], lw['w_gd'])


def _ple_kernel(h1_ref, *refs):
    y_refs, (p_ref, wple_ref, nple_ref, wpg_ref, out_ref) = refs[:N_CHUNKS], refs[N_CHUNKS:]
    h2 = h1_ref[...] + jnp.concatenate([r[...] for r in y_refs], axis=1)
    gate_ple = _sigmoid(_dot(_rms(h2, nple_ref[...]).astype(BF16), wpg_ref[...]))
    out_ref[...] = h2 + _dot(p_ref[0].astype(BF16), wple_ref[...]) * gate_ple


def _ple(h1, y_chunks, p_all, layer, lw, tm):
    t = h1.shape[0]
    row = lambda w: pl.BlockSpec((tm, w), lambda i: (i, 0))
    full = lambda a: pl.BlockSpec(a.shape, lambda i: (0,) * a.ndim)
    return pl.pallas_call(
        _ple_kernel, grid=(t // tm,),
        in_specs=[row(D_MODEL)] + [row(SC_ROW_WORDS)] * N_CHUNKS
        + [pl.BlockSpec((1, tm, p_all.shape[2]), lambda i: (layer, i, 0)),
           full(lw['w_ple']), full(lw['norm_ple']), full(lw['w_pg'])],
        out_specs=row(D_MODEL), out_shape=jax.ShapeDtypeStruct((t, D_MODEL), F32),
        compiler_params=_cparams("parallel"), name="ple")(
            h1, *y_chunks, p_all, lw['w_ple'], lw['norm_ple'], lw['w_pg'])


def _moe_dispatch(x_chunks, gates):
    t = gates.shape[0]
    n_rows = t + N_GROUPS * SORT_TILE
    dest_rep, total = _plan(gates)
    dest = dest_rep[:, 0].astype(jnp.int32)
    padded = (total[0, N_EXPERTS:N_EXPERTS + N_GROUPS].astype(jnp.int32) + (SORT_TILE - 1)) // SORT_TILE * SORT_TILE
    ends = jnp.cumsum(padded)
    tile_start = jnp.arange(n_rows // SORT_TILE, dtype=jnp.int32) * SORT_TILE
    tile_group = jnp.minimum(jnp.sum(tile_start[:, None] >= ends[None, :], axis=1), N_GROUPS - 1).astype(jnp.int32)
    tile_valid = (tile_start < ends[-1]).astype(jnp.int32)
    xs_chunks = [_sc_scatter_rows(xc, dest, n_rows) for xc in x_chunks]
    gs = _sc_scatter_rows(gates, dest, n_rows)
    return xs_chunks, gs, tile_group, tile_valid, dest


def _moe_combine(dispatched, h1, p_all, layer, lw):
    xs_chunks, gs, tile_group, tile_valid, dest = dispatched
    ys_chunks = _experts_sorted(xs_chunks, gs, tile_group, tile_valid, lw)
    y_chunks = [_sc_gather_rows(yc, dest) for yc in ys_chunks]
    return _ple(h1, y_chunks, p_all, layer, lw, _token_tile(h1.shape[0], 1024))


def _regroup_out_rows(wt):
    z = jnp.zeros((HEAD_DIM, wt.shape[1]), wt.dtype)
    return jnp.concatenate([wt[:W_A], wt[W_A:W_A + W_B], z, wt[W_A + W_B:], z], axis=0)


def _regroup_rows(w):
    z = jnp.zeros((HEAD_DIM, w.shape[1]), w.dtype)
    return jnp.stack([w[:W_A], jnp.concatenate([w[W_A:W_A + W_B], z], axis=0),
                      jnp.concatenate([w[W_A + W_B:], z], axis=0)])


def _pad_lanes(a, n=LANES):
    return jnp.pad(a, [(0, 0)] * (a.ndim - 1) + [(0, n - a.shape[-1])])


def _toeplitz_bias(rel_bias, off, tq, win):
    length = tq + win
    pad = length + abs(off)
    ext = jnp.flip(jnp.pad(rel_bias, ((0, 0), (pad, pad)), mode='edge'), axis=1)
    s1 = ext.shape[1] - 1 - (off + REL_CLIP + pad)
    v = jnp.concatenate([ext[:, s1:s1 + win], ext[:, s1 - tq:s1]], axis=1)
    flat = jnp.tile(v, (1, tq))[:, :tq * (length - 1)]
    return flat.reshape(-1, tq, length - 1)[:, :, :win]


def _block_toeplitz_bias(rel_bias, off, tq, win):
    nq, nk = tq // LANES, win // LANES
    blocks = {d: _toeplitz_bias(rel_bias, off - d * LANES, LANES, LANES) for d in range(-(nq - 1), nk)}
    return jnp.concatenate(
        [jnp.concatenate([blocks[b - a] for b in range(nk)], axis=2) for a in range(nq)], axis=1)


def _rel_tables(rel_bias, t_new, rows_b):
    tq, win = ATT_TILE, ATT_TILE + BAND
    q = jnp.arange(tq)[:, None]
    tabs = []
    for var in range(BAND // tq + 1):
        rel_k = jnp.arange(win)[None, :] - var * tq
        kch = jnp.floor_divide(rel_k, CHUNK)
        qch = q // CHUNK
        valid = (kch <= qch) & (kch >= qch - BAND_CHUNKS)
        tabs.append(jnp.where(valid[None], _block_toeplitz_bias(rel_bias, var * tq, tq, win), NEG_INF))
    tab_prompt = jnp.stack(tabs).astype(F32)
    tab_c = _toeplitz_bias(rel_bias, rows_b, t_new, rows_b).astype(F32)
    tab_n = _toeplitz_bias(rel_bias, 0, t_new, t_new).astype(F32)
    return tab_prompt, tab_c, tab_n


def _layer_weights(i, norm_mix, w_in, b_f, g_qa, g_ka, g_qb, g_kb, g_mix_out, w_out, norm_ffn,
                   w_rg, b_rg, w_re, b_re, w_eg, w_eu, w_ed, w_ple, norm_ple, w_pg):
    d_mix = W_A + W_B + W_C
    wt = jnp.transpose(w_in, (2, 0, 1))[:, i, :]
    w_f = wt[3 * d_mix:]
    w_big = jnp.concatenate(
        [_regroup_out_rows(wt[j * d_mix:(j + 1) * d_mix]) for j in range(3)]
        + [jnp.pad(w_f, ((0, LANES - w_f.shape[0]), (0, 0)))], axis=0).astype(BF16)
    tile6 = lambda g: jnp.tile(g, GROUP_W // HEAD_DIM)
    w_r = _pad_lanes(jnp.concatenate([w_re[i], w_rg[i]], axis=1))
    w_r_hi = w_r.astype(BF16)
    gm = g_mix_out[i]
    zpad = jnp.zeros((HEAD_DIM,), F32)
    g_mix = jnp.stack([gm[:W_A], jnp.concatenate([gm[W_A:W_A + W_B], zpad]),
                       jnp.concatenate([gm[W_A + W_B:], zpad])])[:, None, :]
    return dict(
        norm_mix=norm_mix[i][None], w_in=w_big,
        gq=jnp.concatenate([tile6(g_qa[i]), tile6(g_qb[i])])[None],
        gk=jnp.concatenate([tile6(g_ka[i]), tile6(g_kb[i])])[None],
        b_f=_pad_lanes(b_f[i])[None],
        g_mix=g_mix, w_out=_regroup_rows(w_out[i]).astype(BF16), norm_ffn=norm_ffn[i][None],
        w_r_hi=w_r_hi, w_r_lo=(w_r - w_r_hi.astype(F32)).astype(BF16),
        b_r=_pad_lanes(jnp.concatenate([b_re[i], b_rg[i]]))[None],
        w_eg=w_eg[i].astype(BF16), w_eu=w_eu[i].astype(BF16),
        w_ed=w_ed[i].reshape(-1, w_ed.shape[-1]).astype(BF16),
        w_gg=_group_cols(w_eg[i]), w_gu=_group_cols(w_eu[i]),
        w_gd=w_ed[i].reshape(N_GROUPS, -1, w_ed.shape[-1]).astype(BF16),
        w_ple=w_ple[i].astype(BF16), norm_ple=norm_ple[i][None], w_pg=w_pg[i].astype(BF16))


def _group_cols(w):
    e, d, f = w.shape
    w = w.reshape(N_GROUPS, EXPERTS_PER_GROUP, d, f)
    return jnp.transpose(w, (0, 2, 1, 3)).reshape(N_GROUPS, d, EXPERTS_PER_GROUP * f).astype(BF16)


def _token_tile(t, pref):
    return pref if t % pref == 0 else t


def _feature_major(cache):
    d, n, p, h, e = cache.shape
    return jnp.transpose(cache, (0, 1, 3, 4, 2)).reshape(d, n, h * e, p)


def _position_major(rows, heads):
    d, n, _, p = rows.shape
    return jnp.transpose(rows.reshape(d, n, heads, HEAD_DIM, p), (0, 1, 4, 2, 3))


def kernel(x_prompt, x_sample, p_prompt, p_sample, cache_a_k, cache_a_v, cache_a_logf, cache_b_k, cache_b_v, cache_c_k, cache_c_v, norm_mix, w_in, b_f, g_qa, g_ka, g_qb, g_kb, rel_bias, g_mix_out, w_out, norm_ffn, w_router_group, b_router_group, w_router_expert, b_router_expert, w_exp_gate, w_exp_up, w_exp_down, w_ple, norm_ple, w_ple_gate):
    nb, s, d = x_prompt.shape
    ns, t_new, _ = x_sample.shape
    depth = w_in.shape[0]
    past = cache_a_k.shape[2]
    rows_b = cache_b_k.shape[2]
    assert d == D_MODEL and s % PROJ_TILE == 0 and s >= BAND + ATT_TILE and past % ATT_TILE == 0
    tp, ts = nb * s, ns * t_new

    hp = x_prompt.reshape(tp, d)
    hs = x_sample.reshape(ts, d)
    pp = p_prompt.reshape(depth, tp, -1)
    ps = p_sample.reshape(depth, ts, -1)
    caches = tuple(_feature_major(c) for c in (cache_a_k, cache_a_v, cache_b_k, cache_b_v, cache_c_k, cache_c_v))
    lf_rows = jnp.transpose(cache_a_logf, (0, 3, 1, 2))
    rsum = _suffix_sum(lf_rows.reshape(depth * H_A * ns, past)).reshape(depth, H_A, ns, past)
    rsum = jnp.transpose(rsum, (0, 2, 1, 3))

    rows_p = None
    new_s = [[] for _ in range(7)]
    for i in range(depth):
        lw = _layer_weights(i, norm_mix, w_in, b_f, g_qa, g_ka, g_qb, g_kb, g_mix_out, w_out, norm_ffn,
                            w_router_group, b_router_group, w_router_expert, b_router_expert,
                            w_exp_gate, w_exp_up, w_exp_down, w_ple, norm_ple, w_ple_gate)
        tab_p, tab_c, tab_n = _rel_tables(rel_bias[i] * LOG2E, t_new, rows_b)

        pr = _inproj_prompt(hp, lw, i, depth, nb, s, rows_p)
        qat, ka, vat, qb, kb, vb, qc, kc, vc, logf = pr[:10]
        rows_p = pr[10:]
        kaug, qaug = _forget_aug(logf, nb, s)
        oa = _attn_a(qat, ka, vat, kaug, qaug, nb, s)
        ob = _prompt_attention(
            _attn_b_kernel, "attn_b", qb, kb, vb, (tab_p,),
            [pl.BlockSpec(tab_p.shape, lambda b, j: (0, 0, 0, 0))], [], nb, s, ATT_TILE)
        oc = _prompt_attention(
            _attn_c_kernel, "attn_c", qc, kc, vc, (), [],
            [pltpu.VMEM((H_C, C_QTILE, PAIR_W), BF16), pltpu.VMEM((H_C, C_QTILE, LANES), F32),
             pltpu.VMEM((H_C, C_QTILE, PAIR_W), F32)], nb, s, C_QTILE)
        h1, gates, *x_chunks = _post(oa, ob, oc, hp, lw, _token_tile(tp, 512), N_XCHUNKS)
        dispatched = _moe_dispatch(x_chunks, gates)

        sr = _inproj_sample(hs, lw)
        oa, ob, oc = _sample_attention(sr, caches, i, rsum, tab_c, tab_n, ns, t_new)
        h1s, gates, xn = _post(oa, ob, oc, hs, lw, _token_tile(ts, 512))
        hs = _moe(xn, gates, h1s, ps, i, lw, _token_tile(ts, 1024))

        hp = _moe_combine(dispatched, h1, pp, i, lw)
        akf, avf, bkf, bvf, ckf, cvf, logf = sr[9:]
        rows = (akf.reshape(ns, t_new, H_A, HEAD_DIM), avf.reshape(ns, t_new, H_A, HEAD_DIM),
                logf[:, :H_A].reshape(ns, t_new, H_A),
                bkf.reshape(ns, t_new, H_B, HEAD_DIM), bvf.reshape(ns, t_new, H_B, HEAD_DIM),
                ckf.reshape(ns, t_new, H_C, HEAD_DIM), cvf.reshape(ns, t_new, H_C, HEAD_DIM))
        for j in range(7):
            new_s[j].append(rows[j])

    akt, avt, bkt, bvt, ckt, cvt, lft = rows_p
    outs_p = (_position_major(akt, H_A), _position_major(avt, H_A),
              jnp.transpose(lft.reshape(depth, H_A, nb, s), (0, 2, 3, 1)),
              _position_major(bkt, H_B), _position_major(bvt, H_B),
              _position_major(ckt, H_C), _position_major(cvt, H_C))
    outs_s = [jnp.stack(r, axis=0) for r in new_s]
    return (hp.reshape(nb, s, d), hs.reshape(ns, t_new, d), *outs_p, *outs_s)
```

```python
import functools
import math

import jax
import jax.numpy as jnp
from jax import lax
from jax.experimental import pallas as pl
from jax.experimental.pallas import tpu as pltpu
from jax.experimental.pallas import tpu_sc as plsc

F32 = jnp.float32
BF16 = jnp.bfloat16

D_MODEL = 1024
HEAD_DIM = 64
H_A, H_B, H_C = 6, 5, 5
W_A, W_B, W_C = H_A * HEAD_DIM, H_B * HEAD_DIM, H_C * HEAD_DIM
GROUP_W = 384
PAIR_W = 2 * HEAD_DIM
N_PAIRS = GROUP_W // PAIR_W
CHUNK = 64
BAND_CHUNKS = 8
BAND = BAND_CHUNKS * CHUNK
REL_CLIP = 128
N_GROUPS = 4
EXPERTS_PER_GROUP = 4
N_EXPERTS = N_GROUPS * EXPERTS_PER_GROUP
EPS = 1e-6
ATTN_SCALE = HEAD_DIM ** -0.5
LOG2E = math.log2(math.e)
Q_SCALE = ATTN_SCALE * LOG2E
NEG_INF = -1e30
LANES = 128
SUBLANES = 8
VMEM_LIMIT = 48 * 1024 * 1024
EXPERTS_PER_STEP = 4
MOE_VMEM_LIMIT = 58 * 1024 * 1024

ATT_TILE = 256
A_QTILE = 512
C_QTILE = 256
PROJ_TILE = 1024
PROJ_VMEM_LIMIT = 56 * 1024 * 1024
AUG_SLOTS = 8

_NT = (((1,), (1,)), ((), ()))


def _cparams(*sem):
    return pltpu.CompilerParams(dimension_semantics=sem, vmem_limit_bytes=VMEM_LIMIT)


def _dot(a, b):
    return jnp.dot(a, b, preferred_element_type=F32)


def _dot_nt(a, b):
    return lax.dot_general(a, b, _NT, preferred_element_type=F32)


def _split3(x):
    hi = x.astype(BF16)
    r1 = x - hi.astype(F32)
    mid = r1.astype(BF16)
    lo = (r1 - mid.astype(F32)).astype(BF16)
    return hi, mid, lo


def _split2(x):
    hi = x.astype(BF16)
    lo = (x - hi.astype(F32)).astype(BF16)
    return hi, lo


def _lane_iota(n=LANES):
    return lax.broadcasted_iota(jnp.int32, (1, n), 1)


def _half_mask(half):
    lane = _lane_iota()
    return (lane < HEAD_DIM) if half == 0 else (lane >= HEAD_DIM)


def _pair_slice(h):
    return slice((h // 2) * PAIR_W, (h // 2 + 1) * PAIR_W)


def _softplus2(z2):
    return jnp.maximum(z2, 0.0) + jnp.log2(1.0 + jnp.exp2(-jnp.abs(z2)))


def _sigmoid(z):
    return 1.0 / (1.0 + jnp.exp(-z))


def _rms(x, g):
    return x * lax.rsqrt(jnp.mean(x * x, axis=-1, keepdims=True) + EPS) * g


def _pair_rms(z, gain):
    first = _half_mask(0)
    outs = []
    for j in range(N_PAIRS):
        blk = z[:, j * PAIR_W:(j + 1) * PAIR_W]
        sq = blk * blk
        lo = jnp.sum(jnp.where(first, sq, 0.0), axis=-1, keepdims=True)
        hi = jnp.sum(jnp.where(first, 0.0, sq), axis=-1, keepdims=True)
        ms = jnp.where(first, lo, hi) * (1.0 / HEAD_DIM)
        outs.append(blk * lax.rsqrt(ms + EPS) * gain[:, j * PAIR_W:(j + 1) * PAIR_W])
    return jnp.concatenate(outs, axis=-1)


def _projector(x_ref, gn_ref, w_ref):
    xb = _rms(x_ref[...], gn_ref[...]).astype(BF16)
    return lambda g, width=GROUP_W: _dot_nt(xb, w_ref[g * GROUP_W:g * GROUP_W + width, :])


def _log_forget(zf, bf_ref):
    zf = zf + bf_ref[...]
    return jnp.minimum(zf, 0.0) - jnp.log(1.0 + jnp.exp(-jnp.abs(zf)))


def _inproj_sample_kernel(x_ref, gn_ref, w_ref, gq_ref, gk_ref, bf_ref,
                          qa_ref, qb_ref, qc_ref, ka_ref, kb_ref, kc_ref, va_ref, vb_ref, vc_ref,
                          akf_ref, avf_ref, bkf_ref, bvf_ref, ckf_ref, cvf_ref, logf_ref):
    proj = _projector(x_ref, gn_ref, w_ref)
    qa_ref[...] = (_pair_rms(proj(0), gq_ref[:, :GROUP_W]) * Q_SCALE).astype(BF16)
    qb_ref[...] = (_pair_rms(proj(1), gq_ref[:, GROUP_W:]) * Q_SCALE).astype(BF16)
    qc_ref[...] = (proj(2) * Q_SCALE).astype(BF16)
    for g, norm, bf_ref_, f_ref, width in (
            (3, gk_ref[:, :GROUP_W], ka_ref, akf_ref, W_A), (4, gk_ref[:, GROUP_W:], kb_ref, bkf_ref, W_B),
            (5, None, kc_ref, ckf_ref, W_C), (6, None, va_ref, avf_ref, W_A),
            (7, None, vb_ref, bvf_ref, W_B), (8, None, vc_ref, cvf_ref, W_C)):
        z = proj(g)
        if norm is not None:
            z = _pair_rms(z, norm)
        f_ref[...] = z[:, :width]
        bf_ref_[...] = z.astype(BF16)
    logf_ref[...] = _log_forget(proj(9, LANES), bf_ref)


def _inproj_sample(x, lw):
    t = x.shape[0]
    row = lambda w: pl.BlockSpec((t, w), lambda i: (0, 0))
    full = lambda a: pl.BlockSpec(a.shape, lambda i: (0,) * a.ndim)
    widths = [GROUP_W] * 9 + [W_A, W_A, W_B, W_B, W_C, W_C, LANES]
    out_shape = [jax.ShapeDtypeStruct((t, w), BF16 if j < 9 else F32) for j, w in enumerate(widths)]
    ins = (x, lw['norm_mix'], lw['w_in'], lw['gq'], lw['gk'], lw['b_f'])
    return pl.pallas_call(
        _inproj_sample_kernel, grid=(1,),
        in_specs=[row(D_MODEL)] + [full(a) for a in ins[1:]],
        out_specs=[row(w) for w in widths], out_shape=out_shape,
        compiler_params=_cparams("arbitrary"), name="inproj_sample")(*ins)


N_PROMPT_ROWS = 7


def _inproj_prompt_kernel(*refs, n_alias):
    x_ref, gn_ref, w_ref, gq_ref, gk_ref, bf_ref = refs[:6]
    (qat_ref, ka_ref, vat_ref, qb_ref, kb_ref, vb_ref, qc_ref, kc_ref, vc_ref, logf_ref,
     akt_ref, avt_ref, bkt_ref, bvt_ref, ckt_ref, cvt_ref, lft_ref) = refs[6 + n_alias:]
    proj = _projector(x_ref, gn_ref, w_ref)
    qat_ref[...] = jnp.transpose(_pair_rms(proj(0), gq_ref[:, :GROUP_W]) * Q_SCALE).astype(BF16)
    qb_ref[...] = (_pair_rms(proj(1), gq_ref[:, GROUP_W:]) * Q_SCALE).astype(BF16)
    qc_ref[...] = (proj(2) * Q_SCALE).astype(BF16)
    ka = _pair_rms(proj(3), gk_ref[:, :GROUP_W])
    ka_ref[...] = ka.astype(BF16)
    akt_ref[0, 0] = jnp.transpose(ka)
    kb = _pair_rms(proj(4), gk_ref[:, GROUP_W:])
    kb_ref[...] = kb.astype(BF16)
    kc = proj(5)
    kc_ref[...] = kc.astype(BF16)
    ckt_ref[0, 0] = jnp.transpose(kc)[:W_C]
    vat = jnp.transpose(proj(6))
    avt_ref[0, 0] = vat
    for j in range(vat_ref.shape[1]):
        vat_ref[0, j] = vat[:, j * ATT_TILE:(j + 1) * ATT_TILE].astype(BF16)
    vb = proj(7)
    vb_ref[...] = vb.astype(BF16)
    vc = proj(8)
    vc_ref[...] = vc.astype(BF16)
    cvt_ref[0, 0] = jnp.transpose(vc)[:W_C]

    @pl.when(pl.program_id(1) == pl.num_programs(1) - 1)
    def _():
        keep = kb.shape[0] - BAND
        bkt_ref[0, 0] = jnp.transpose(kb)[:W_B, keep:]
        bvt_ref[0, 0] = jnp.transpose(vb)[:W_B, keep:]

    lf = _log_forget(proj(9, LANES), bf_ref)
    logf_ref[...] = lf
    lft_ref[0] = jnp.transpose(lf)[:H_A]


def _inproj_prompt(x, lw, layer, depth, nb, s, prev_rows):
    tm = PROJ_TILE
    nt = s // tm
    t = nb * s
    per = tm // ATT_TILE
    row = lambda w: pl.BlockSpec((tm, w), lambda b, i: (b * nt + i, 0))
    full = lambda a: pl.BlockSpec(a.shape, lambda b, i: (0,) * a.ndim, pipeline_mode=pl.Buffered(1))
    feat = lambda w: pl.BlockSpec((1, 1, w, tm), lambda b, i: (layer, b, 0, i))
    band = lambda w: pl.BlockSpec((1, 1, w, BAND), lambda b, i: (layer, b, 0, 0))
    tok = jax.ShapeDtypeStruct((t, GROUP_W), BF16)
    out_shape = [
        jax.ShapeDtypeStruct((GROUP_W, t), BF16), tok,
        jax.ShapeDtypeStruct((nb, s // ATT_TILE, GROUP_W, ATT_TILE), BF16),
        tok, tok, tok, tok, tok, tok, jax.ShapeDtypeStruct((t, LANES), F32),
        jax.ShapeDtypeStruct((depth, nb, W_A, s), F32), jax.ShapeDtypeStruct((depth, nb, W_A, s), F32),
        jax.ShapeDtypeStruct((depth, nb, W_B, BAND), F32), jax.ShapeDtypeStruct((depth, nb, W_B, BAND), F32),
        jax.ShapeDtypeStruct((depth, nb, W_C, s), F32), jax.ShapeDtypeStruct((depth, nb, W_C, s), F32),
        jax.ShapeDtypeStruct((depth, H_A, t), F32)]
    out_specs = [
        pl.BlockSpec((GROUP_W, tm), lambda b, i: (0, b * nt + i)), row(GROUP_W),
        pl.BlockSpec((1, per, GROUP_W, ATT_TILE), lambda b, i: (b, i, 0, 0)),
        row(GROUP_W), row(GROUP_W), row(GROUP_W), row(GROUP_W), row(GROUP_W), row(GROUP_W), row(LANES),
        feat(W_A), feat(W_A), band(W_B), band(W_B), feat(W_C), feat(W_C),
        pl.BlockSpec((1, H_A, tm), lambda b, i: (layer, 0, b * nt + i))]
    ins = [x, lw['norm_mix'], lw['w_in'], lw['gq'], lw['gk'], lw['b_f']]
    in_specs = [row(D_MODEL)] + [full(a) for a in ins[1:]]
    aliases = {}
    n_alias = 0
    if prev_rows is not None:
        n_alias = N_PROMPT_ROWS
        first_row_out = len(out_shape) - N_PROMPT_ROWS
        for j, a in enumerate(prev_rows):
            aliases[len(ins)] = first_row_out + j
            ins.append(a)
            in_specs.append(pl.BlockSpec(memory_space=pl.ANY))
    return pl.pallas_call(
        functools.partial(_inproj_prompt_kernel, n_alias=n_alias), grid=(nb, nt),
        in_specs=in_specs, out_specs=out_specs, out_shape=out_shape,
        input_output_aliases=aliases,
        compiler_params=pltpu.CompilerParams(dimension_semantics=("parallel", "arbitrary"),
                                             vmem_limit_bytes=PROJ_VMEM_LIMIT),
        name="inproj_prompt")(*ins)


def _forget_aug_kernel(logf_ref, kaug_ref, qaug_ref, *, nblk):
    tb = ATT_TILE
    r = lax.broadcasted_iota(jnp.int32, (tb, tb), 0)
    c = lax.broadcasted_iota(jnp.int32, (tb, tb), 1)
    tri = jnp.where(c <= r, 1.0, 0.0).astype(BF16)
    src = lax.broadcasted_iota(jnp.int32, (LANES, LANES), 0)
    dst = lax.broadcasted_iota(jnp.int32, (LANES, LANES), 1)

    def place(slot):
        return jnp.where((dst == AUG_SLOTS * src + slot) & (src < H_A), 1.0, 0.0).astype(BF16)

    lane = _lane_iota()
    slot = jnp.bitwise_and(lane, AUG_SLOTS - 1)
    used = lane < AUG_SLOTS * H_A
    key_const = jnp.where(used & (slot >= 3) & (slot < 6), 1.0, 0.0)
    qry_const = jnp.where(used & (slot < 3), -1.0, 0.0)
    carry = jnp.zeros((1, LANES), F32)
    for blk in range(nblk):
        rows = slice(blk * tb, (blk + 1) * tb)
        hi, mid, lo = _split3(logf_ref[rows, :])
        cs = (_dot(tri, hi) + _dot(tri, mid) + _dot(tri, lo)) + carry
        carry = cs[tb - 1:tb, :]
        hi, mid, lo = _split3(cs * LOG2E)
        kaug = _dot(hi, place(0)) + _dot(mid, place(1)) + _dot(lo, place(2)) + key_const
        kaug_ref[rows, :] = kaug.astype(BF16)
        qaug = _dot(hi, place(3)) + _dot(mid, place(4)) + _dot(lo, place(5)) + qry_const
        qaug_ref[:, rows] = jnp.transpose(qaug).astype(BF16)


def _forget_aug(logf, nb, s):
    return pl.pallas_call(
        functools.partial(_forget_aug_kernel, nblk=s // ATT_TILE), grid=(nb,),
        in_specs=[pl.BlockSpec((s, LANES), lambda b: (b, 0))],
        out_specs=[pl.BlockSpec((s, LANES), lambda b: (b, 0)), pl.BlockSpec((LANES, s), lambda b: (0, b))],
        out_shape=[jax.ShapeDtypeStruct((nb * s, LANES), BF16), jax.ShapeDtypeStruct((LANES, nb * s), BF16)],
        compiler_params=_cparams("parallel"), name="forget_aug")(logf)


def _suffix_sum_kernel(x_ref, o_ref):
    p = x_ref.shape[1]
    j = lax.broadcasted_iota(jnp.int32, (p, p), 0)
    s = lax.broadcasted_iota(jnp.int32, (p, p), 1)
    tri = jnp.where(j > s, 1.0, 0.0).astype(BF16)
    hi, mid, lo = _split3(x_ref[...])
    o_ref[...] = _dot(hi, tri) + _dot(mid, tri) + _dot(lo, tri)


def _suffix_sum(x):
    return pl.pallas_call(
        _suffix_sum_kernel, out_shape=jax.ShapeDtypeStruct(x.shape, F32),
        compiler_params=pltpu.CompilerParams(vmem_limit_bytes=VMEM_LIMIT), name="suffix_sum")(x)


def _attn_a_kernel(qt_ref, k_ref, vt_ref, kaug_ref, qaug_ref, o_ref, qp_ref, m_ref, l_ref, acc_ref):
    tq = qt_ref.shape[1]
    tk = ATT_TILE
    qi = pl.program_id(1)
    row = lax.broadcasted_iota(jnp.int32, (PAIR_W, 1), 0)
    aug = qaug_ref[...]
    for h in range(H_A):
        q2 = qt_ref[_pair_slice(h), :]
        half = (row < HEAD_DIM) if h % 2 == 0 else (row >= HEAD_DIM)
        qp_ref[h, :PAIR_W, :] = jnp.where(half, q2, jnp.zeros_like(q2))
        mine = (row >= AUG_SLOTS * h) & (row < AUG_SLOTS * (h + 1))
        qp_ref[h, PAIR_W:, :] = jnp.where(mine, aug, jnp.zeros_like(aug))
    m_ref[...] = jnp.full(m_ref.shape, NEG_INF, F32)
    l_ref[...] = jnp.zeros(l_ref.shape, F32)
    acc_ref[...] = jnp.zeros(acc_ref.shape, F32)

    def step(kt, key_offset):
        ks = pl.multiple_of(kt * tk, tk)
        ka = kaug_ref[pl.ds(ks, tk), :]
        tile_scores = [_dot(jnp.concatenate([k_ref[pl.ds(ks, tk), _pair_slice(h)], ka], axis=1), qp_ref[h])
                       for h in range(H_A)]
        probs, alphas = [], []
        for h in range(H_A):
            s = tile_scores[h]
            if key_offset is not None:
                key = lax.broadcasted_iota(jnp.int32, (tk, tq), 0)
                qry = lax.broadcasted_iota(jnp.int32, (tk, tq), 1)
                s = jnp.where(key + key_offset <= qry, s, NEG_INF)
            m_prev = m_ref[h]
            m_new = jnp.maximum(m_prev, jnp.max(s, axis=0, keepdims=True))
            alpha = jnp.exp2(m_prev - m_new)
            p = jnp.exp2(s - m_new)
            l_ref[h] = alpha * l_ref[h] + jnp.sum(p, axis=0, keepdims=True)
            m_ref[h] = m_new
            alphas.append(alpha)
            probs.append(p.astype(BF16))
        for h in range(H_A):
            acc_ref[h] = alphas[h] * acc_ref[h] + _dot(vt_ref[0, kt, _pair_slice(h), :], probs[h])

    def body(kt, carry):
        step(kt, None)
        return carry

    per = tq // tk
    lax.fori_loop(0, qi * per, body, 0)
    for j in range(per):
        step(qi * per + j, j * tk)
    for pair in range(N_PAIRS):
        h = 2 * pair
        out_t = jnp.where(row < HEAD_DIM, acc_ref[h] * (1.0 / l_ref[h]), acc_ref[h + 1] * (1.0 / l_ref[h + 1]))
        o_ref[:, pair * PAIR_W:(pair + 1) * PAIR_W] = jnp.transpose(out_t)


def _attn_a(qat, ka, vat, kaug, qaug, nb, s):
    tq = A_QTILE
    nq = s // tq
    return pl.pallas_call(
        _attn_a_kernel, grid=(nb, nq),
        in_specs=[pl.BlockSpec((GROUP_W, tq), lambda b, i: (0, b * nq + i)),
                  pl.BlockSpec((s, GROUP_W), lambda b, i: (b, 0)),
                  pl.BlockSpec((1, s // ATT_TILE, GROUP_W, ATT_TILE), lambda b, i: (b, 0, 0, 0)),
                  pl.BlockSpec((s, LANES), lambda b, i: (b, 0)),
                  pl.BlockSpec((LANES, tq), lambda b, i: (0, b * nq + i))],
        out_specs=pl.BlockSpec((tq, GROUP_W), lambda b, i: (b * nq + i, 0)),
        out_shape=jax.ShapeDtypeStruct((nb * s, GROUP_W), F32),
        scratch_shapes=[pltpu.VMEM((H_A, 2 * PAIR_W, tq), BF16), pltpu.VMEM((H_A, 1, tq), F32),
                        pltpu.VMEM((H_A, 1, tq), F32), pltpu.VMEM((H_A, PAIR_W, tq), F32)],
        compiler_params=_cparams("parallel", "arbitrary"), name="attn_a")(qat, ka, vat, kaug, qaug)


def _masked_q(q_ref, qm_ref, n_heads):
    for h in range(n_heads):
        q2 = q_ref[:, _pair_slice(h)]
        qm_ref[h] = jnp.where(_half_mask(h % 2), q2, jnp.zeros_like(q2))


def _attn_b_kernel(q_ref, k_ref, v_ref, tab_ref, o_ref):
    tq = ATT_TILE
    win = tq + BAND
    i = pl.program_id(1)
    var = jnp.minimum(i, BAND // tq)
    ws = pl.multiple_of(jnp.maximum(i * tq - BAND, 0), tq)
    heads = range(H_B)
    scores = []
    for h in heads:
        q2 = q_ref[:, _pair_slice(h)]
        qm = jnp.where(_half_mask(h % 2), q2, jnp.zeros_like(q2))
        scores.append(_dot_nt(qm, k_ref[pl.ds(ws, win), _pair_slice(h)]) + tab_ref[var, h])
    probs, norms = [], []
    for h in heads:
        p = jnp.exp2(scores[h] - jnp.max(scores[h], axis=-1, keepdims=True))
        norms.append(1.0 / jnp.sum(p, axis=-1, keepdims=True))
        probs.append(p.astype(BF16))
    res = [_dot(probs[h], v_ref[pl.ds(ws, win), _pair_slice(h)]) * norms[h] for h in heads]
    res.append(jnp.zeros((tq, PAIR_W), F32))
    first = _half_mask(0)
    for pair in range(N_PAIRS):
        o_ref[:, pair * PAIR_W:(pair + 1) * PAIR_W] = jnp.where(first, res[2 * pair], res[2 * pair + 1])


def _suffix_tri(n):
    j = lax.broadcasted_iota(jnp.int32, (2 * n, n), 0)
    s = lax.broadcasted_iota(jnp.int32, (2 * n, n), 1)
    return jnp.where(jnp.where(j >= n, j - n, j) >= s, 1.0, 0.0).astype(BF16)


def _stick_scores(z, tri, seen):
    sp = _softplus2(z)
    if seen is not None:
        sp = jnp.where(seen, sp, 0.0)
    hi, lo = _split2(sp)
    n = sp.shape[1]
    if n % LANES == 0:
        return _dot(jnp.concatenate([hi, lo], axis=1), tri)
    return _dot(hi, tri[:n]) + _dot(lo, tri[:n])


def _stick_weights(z, s_in, carry, seen):
    a = jnp.exp2(z - (s_in + carry))
    if seen is not None:
        a = jnp.where(seen, a, 0.0)
    return a.astype(BF16)


def _attn_c_kernel(q_ref, k_ref, v_ref, o_ref, qm_ref, carry_ref, acc_ref):
    tq = q_ref.shape[0]
    tk = ATT_TILE
    qi = pl.program_id(1)
    _masked_q(q_ref, qm_ref, H_C)
    carry_ref[...] = jnp.zeros(carry_ref.shape, F32)
    acc_ref[...] = jnp.zeros(acc_ref.shape, F32)
    heads = range(H_C)

    def step(kt, key_offset):
        ks = pl.multiple_of(kt * tk, tk)
        zs = [_dot_nt(qm_ref[h], k_ref[pl.ds(ks, tk), _pair_slice(h)]) for h in heads]
        tri = _suffix_tri(tk)
        seen = None
        if key_offset is not None:
            r = lax.broadcasted_iota(jnp.int32, (tq, tk), 0)
            c = lax.broadcasted_iota(jnp.int32, (tq, tk), 1)
            seen = c + key_offset < r
        sums = [_stick_scores(zs[h], tri, seen) for h in heads]
        weights = []
        for h in heads:
            carry = carry_ref[h][:, 0:1]
            weights.append(_stick_weights(zs[h], sums[h], carry, seen))
            carry_ref[h] = jnp.broadcast_to(carry + sums[h][:, 0:1], (tq, LANES))
        for h in heads:
            acc_ref[h] += _dot(weights[h], v_ref[pl.ds(ks, tk), _pair_slice(h)])

    per = tq // tk

    def body(j, c):
        step(qi * per - 1 - j, None)
        return c

    for j in reversed(range(per)):
        step(qi * per + j, j * tk)
    lax.fori_loop(0, qi * per, body, 0)
    first = _half_mask(0)
    for pair in range(N_PAIRS):
        h = 2 * pair
        second = acc_ref[h + 1] if h + 1 < H_C else jnp.zeros((tq, PAIR_W), F32)
        o_ref[:, pair * PAIR_W:(pair + 1) * PAIR_W] = jnp.where(first, acc_ref[h], second)


def _prompt_attention(kernel, name, q, k, v, extra, extra_specs, scratch, nb, s, tq):
    nq = s // tq
    qspec = pl.BlockSpec((tq, GROUP_W), lambda b, i: (b * nq + i, 0))
    kvspec = pl.BlockSpec((s, GROUP_W), lambda b, i: (b, 0))
    return pl.pallas_call(
        kernel, grid=(nb, nq),
        in_specs=[qspec, kvspec, kvspec] + extra_specs,
        out_specs=qspec, out_shape=jax.ShapeDtypeStruct((nb * s, GROUP_W), F32),
        scratch_shapes=scratch,
        compiler_params=_cparams("parallel", "arbitrary"), name=name)(q, k, v, *extra)


def _sample_heads(n_heads, q_ref, width):
    out = []
    for h in range(n_heads):
        lo = (h // 2) * PAIR_W
        span = slice(lo, min(lo + PAIR_W, width))
        q2 = q_ref[:, span]
        if span.stop - span.start == PAIR_W:
            q2 = jnp.where(_half_mask(h % 2), q2, jnp.zeros_like(q2))
        out.append((h, span, q2))
    return out


def _store_heads(o_ref, res, n_heads):
    for pair in range(N_PAIRS):
        lo = pair * PAIR_W
        h = 2 * pair
        if h + 1 < n_heads:
            o_ref[:, lo:lo + PAIR_W] = jnp.where(_half_mask(0), res[h], res[h + 1])
        else:
            o_ref[:, lo:lo + HEAD_DIM] = res[h]
            o_ref[:, lo + HEAD_DIM:lo + PAIR_W] = jnp.zeros_like(res[h])


def _joint_softmax_pv(s_c, s_n, vt_c, v_n):
    m = jnp.maximum(jnp.max(s_c, axis=-1, keepdims=True), jnp.max(s_n, axis=-1, keepdims=True))
    p_c = jnp.exp2(s_c - m)
    p_n = jnp.exp2(s_n - m)
    l = jnp.sum(p_c, axis=-1, keepdims=True) + jnp.sum(p_n, axis=-1, keepdims=True)
    return (_dot_nt(p_c.astype(BF16), vt_c) + _dot(p_n.astype(BF16), v_n)) / l


def _sample_attn_kernel(qa_ref, qb_ref, qc_ref, kan_ref, kbn_ref, kcn_ref, van_ref, vbn_ref, vcn_ref,
                        cak_ref, cav_ref, cbk_ref, cbv_ref, cck_ref, ccv_ref,
                        rsum_ref, logf_ref, tabc_ref, tabn_ref,
                        oa_ref, ob_ref, oc_ref):
    t = qa_ref.shape[0]
    past = cak_ref.shape[3]
    r = lax.broadcasted_iota(jnp.int32, (t, t), 0)
    c = lax.broadcasted_iota(jnp.int32, (t, t), 1)

    ltri = jnp.where(c <= r, 1.0, 0.0).astype(BF16)
    hi, mid, lo = _split3(logf_ref[...])
    pcol = (_dot(ltri, hi) + _dot(ltri, mid) + _dot(ltri, lo)) * LOG2E
    prow = jnp.transpose(jnp.concatenate([pcol, jnp.zeros((LANES - t, LANES), F32)], axis=0))
    heads = _sample_heads(H_A, qa_ref, W_A)
    kts = {sp.start: cak_ref[0, 0, sp, :].astype(BF16) for _, sp, _ in heads}
    vts = {sp.start: cav_ref[0, 0, sp, :].astype(BF16) for _, sp, _ in heads}
    s_c = [_dot(qm, kts[sp.start]) + (pcol[:, h:h + 1] + rsum_ref[0, 0, h:h + 1, :] * LOG2E)
           for h, sp, qm in heads]
    s_n = [jnp.where(c <= r, _dot_nt(qm, kan_ref[:, sp]) + (pcol[:, h:h + 1] - prow[h:h + 1, 0:t]), NEG_INF)
           for h, sp, qm in heads]
    _store_heads(oa_ref, [_joint_softmax_pv(s_c[h], s_n[h], vts[sp.start], van_ref[:, sp])
                          for h, sp, _ in heads], H_A)

    heads = _sample_heads(H_B, qb_ref, W_B)
    kts = {sp.start: cbk_ref[0, 0, sp, :].astype(BF16) for _, sp, _ in heads}
    vts = {sp.start: cbv_ref[0, 0, sp, :].astype(BF16) for _, sp, _ in heads}
    s_c = [_dot(qm, kts[sp.start]) + tabc_ref[h] for h, sp, qm in heads]
    s_n = [_dot_nt(qm, kbn_ref[:, sp]) + tabn_ref[h] for h, sp, qm in heads]
    _store_heads(ob_ref, [_joint_softmax_pv(s_c[h], s_n[h], vts[sp.start], vbn_ref[:, sp])
                          for h, sp, _ in heads], H_B)

    heads = _sample_heads(H_C, qc_ref, W_C)
    nblk = past // ATT_TILE
    tri_n = _suffix_tri(t)
    tri_c = _suffix_tri(ATT_TILE)
    seen = c < r
    kts = {sp.start: cck_ref[0, 0, sp, :].astype(BF16) for _, sp, _ in heads}
    vts = {sp.start: ccv_ref[0, 0, sp, :].astype(BF16) for _, sp, _ in heads}
    z_n = [_dot_nt(qm, kcn_ref[:, sp]) for _, sp, qm in heads]
    z_c = [_dot(qm, kts[sp.start]) for _, sp, qm in heads]
    sum_n = [_stick_scores(z, tri_n, seen) for z in z_n]
    sum_c = [[_stick_scores(z[:, b * ATT_TILE:(b + 1) * ATT_TILE], tri_c, None) for b in range(nblk)]
             for z in z_c]
    res = []
    for h, sp, _ in heads:
        acc = _dot(_stick_weights(z_n[h], sum_n[h], 0.0, seen), vcn_ref[:, sp])
        carry = sum_n[h][:, 0:1]
        for b in reversed(range(nblk)):
            cols = slice(b * ATT_TILE, (b + 1) * ATT_TILE)
            a = _stick_weights(z_c[h][:, cols], sum_c[h][b], carry, None)
            acc = acc + _dot_nt(a, vts[sp.start][:, cols])
            carry = carry + sum_c[h][b][:, 0:1]
        res.append(acc)
    _store_heads(oc_ref, res, H_C)


def _sample_attention(proj, caches, layer, rsum, tabc, tabn, nb, t):
    qa, qb, qc, ka, kb, kc, va, vb, vc = proj[:9]
    logf = proj[15]
    new = pl.BlockSpec((t, GROUP_W), lambda b: (b, 0))
    cache = lambda a: pl.BlockSpec((1, 1) + a.shape[2:], lambda b: (layer, b, 0, 0))
    full = lambda a: pl.BlockSpec(a.shape, lambda b: (0,) * a.ndim)
    return pl.pallas_call(
        _sample_attn_kernel, grid=(nb,),
        in_specs=[new] * 9 + [cache(a) for a in caches] + [
            cache(rsum), pl.BlockSpec((t, LANES), lambda b: (b, 0)), full(tabc), full(tabn)],
        out_specs=[new] * 3,
        out_shape=[jax.ShapeDtypeStruct((nb * t, GROUP_W), F32)] * 3,
        compiler_params=_cparams("parallel"), name="sample_attn")(
            qa, qb, qc, ka, kb, kc, va, vb, vc, *caches, rsum, logf, tabc, tabn)


def _masked_max(x, mask):
    return jnp.max(jnp.where(mask, x, -jnp.inf), axis=-1, keepdims=True)


def _first_lane(mask, lane):
    return jnp.min(jnp.where(mask, lane, float(LANES)), axis=-1, keepdims=True)


def _route(logits):
    lane = _lane_iota().astype(F32)
    is_g = (lane >= N_EXPERTS) & (lane < N_EXPERTS + N_GROUPS)
    gmax = _masked_max(logits, is_g)
    p_g = 1.0 / jnp.sum(jnp.where(is_g, jnp.exp(logits - gmax), 0.0), axis=-1, keepdims=True)
    g_sel = _first_lane(is_g & (logits == gmax), lane) - N_EXPERTS
    lo = g_sel * EXPERTS_PER_GROUP
    in_g = (lane >= lo) & (lane < lo + EXPERTS_PER_GROUP)
    l1 = _masked_max(logits, in_g)
    i1 = _first_lane(in_g & (logits == l1), lane)
    rest = in_g & (lane != i1)
    l2 = _masked_max(logits, rest)
    i2 = _first_lane(rest & (logits == l2), lane)
    e2 = jnp.exp(l2 - l1)
    w1 = p_g / (1.0 + e2)
    w2 = p_g * e2 / (1.0 + e2)
    gates = jnp.where(lane == i1, w1, jnp.where(lane == i2, w2, 0.0))
    return jnp.where(lane == g_sel + N_EXPERTS, 1.0, gates)


def _post_kernel(oa_ref, ob_ref, oc_ref, h_ref, gmix_ref, wout_ref, nffn_ref, wrh_ref, wrl_ref, br_ref,
                 h1_ref, gates_ref, *x_refs, packed):
    def gnorm(o, width):
        ms = jnp.sum(o * o, axis=-1, keepdims=True) * (1.0 / width)
        return o * lax.rsqrt(ms + EPS)

    h1 = h_ref[...]
    for g, (o_ref, width) in enumerate(((oa_ref, W_A), (ob_ref, W_B), (oc_ref, W_C))):
        merged = (gnorm(o_ref[...], width) * gmix_ref[g]).astype(BF16)
        h1 = h1 + _dot(merged, wout_ref[g])
    h1_ref[...] = h1
    xf = _rms(h1, nffn_ref[...])
    hi, lo = _split2(xf)
    if packed:
        bits = pltpu.bitcast(hi.astype(F32), jnp.uint32)
        for c, ref in enumerate(x_refs):
            a = bits[:, (2 * c) * SC_ROW_WORDS:(2 * c + 1) * SC_ROW_WORDS]
            b = bits[:, (2 * c + 1) * SC_ROW_WORDS:(2 * c + 2) * SC_ROW_WORDS]
            ref[...] = jnp.bitwise_or(a, jnp.right_shift(b, jnp.uint32(16)))
    else:
        x_refs[0][...] = hi
    logits = _dot(hi, wrh_ref[...]) + _dot(lo, wrh_ref[...]) + _dot(hi, wrl_ref[...]) + br_ref[...]
    gates_ref[...] = _route(logits)


def _post(oa, ob, oc, h, lw, tm, n_chunks=0):
    t = h.shape[0]
    row = lambda w: pl.BlockSpec((tm, w), lambda i: (i, 0))
    full = lambda a: pl.BlockSpec(a.shape, lambda i: (0,) * a.ndim)
    ws = (lw['g_mix'], lw['w_out'], lw['norm_ffn'], lw['w_r_hi'], lw['w_r_lo'], lw['b_r'])
    if n_chunks:
        x_specs = [row(SC_ROW_WORDS)] * n_chunks
        x_shapes = [jax.ShapeDtypeStruct((t, SC_ROW_WORDS), jnp.uint32)] * n_chunks
    else:
        x_specs, x_shapes = [row(D_MODEL)], [jax.ShapeDtypeStruct((t, D_MODEL), BF16)]
    return pl.pallas_call(
        functools.partial(_post_kernel, packed=bool(n_chunks)), grid=(t // tm,),
        in_specs=[row(GROUP_W)] * 3 + [row(D_MODEL)] + [full(a) for a in ws],
        out_specs=[row(D_MODEL), row(LANES)] + x_specs,
        out_shape=[jax.ShapeDtypeStruct((t, D_MODEL), F32), jax.ShapeDtypeStruct((t, LANES), F32)] + x_shapes,
        compiler_params=_cparams("parallel"), name="post")(oa, ob, oc, h, *ws)


def _moe_kernel(xn_ref, gates_ref, wg_ref, wu_ref, wd_ref, h1_ref, p_ref, wple_ref, nple_ref, wpg_ref,
                out_ref, acc_ref):
    step = pl.program_id(1)

    @pl.when(step == 0)
    def _():
        acc_ref[...] = jnp.zeros_like(acc_ref)

    x = xn_ref[...]
    gates = gates_ref[...]
    acts = []
    for k in range(EXPERTS_PER_STEP):
        e = step * EXPERTS_PER_STEP + k
        g = _dot(x, wg_ref[k])
        u = _dot(x, wu_ref[k])
        gate = jnp.sum(jnp.where(_lane_iota() == e, gates, 0.0), axis=-1, keepdims=True)
        acts.append(((g * _sigmoid(g)) * u * gate).astype(BF16))
    acc_ref[...] += _dot(jnp.concatenate(acts, axis=1), wd_ref[...])

    @pl.when(step == pl.num_programs(1) - 1)
    def _():
        h2 = h1_ref[...] + acc_ref[...]
        gate_ple = _sigmoid(_dot(_rms(h2, nple_ref[...]).astype(BF16), wpg_ref[...]))
        out_ref[...] = h2 + _dot(p_ref[0].astype(BF16), wple_ref[...]) * gate_ple


def _moe(xn, gates, h1, p_all, layer, lw, tm):
    t = xn.shape[0]
    row = lambda w: pl.BlockSpec((tm, w), lambda i, e: (i, 0))
    once = pl.Buffered(1)
    full = lambda a: pl.BlockSpec(a.shape, lambda i, e: (0,) * a.ndim, pipeline_mode=once)
    exp = lambda a: pl.BlockSpec((EXPERTS_PER_STEP,) + a.shape[1:], lambda i, e: (e, 0, 0))
    d_hidden = lw['w_ed'].shape[0] // N_EXPERTS
    return pl.pallas_call(
        _moe_kernel, grid=(t // tm, N_EXPERTS // EXPERTS_PER_STEP),
        in_specs=[row(D_MODEL), row(LANES), exp(lw['w_eg']), exp(lw['w_eu']),
                  pl.BlockSpec((EXPERTS_PER_STEP * d_hidden, D_MODEL), lambda i, e: (e, 0)),
                  pl.BlockSpec((tm, D_MODEL), lambda i, e: (i, 0), pipeline_mode=once),
                  pl.BlockSpec((1, tm, p_all.shape[2]), lambda i, e: (layer, i, 0), pipeline_mode=once),
                  full(lw['w_ple']), full(lw['norm_ple']), full(lw['w_pg'])],
        out_specs=row(D_MODEL), out_shape=jax.ShapeDtypeStruct((t, D_MODEL), F32),
        scratch_shapes=[pltpu.VMEM((tm, D_MODEL), F32)],
        compiler_params=pltpu.CompilerParams(dimension_semantics=("parallel", "arbitrary"),
                                             vmem_limit_bytes=MOE_VMEM_LIMIT), name="moe")(
            xn, gates, lw['w_eg'], lw['w_eu'], lw['w_ed'], h1, p_all, lw['w_ple'], lw['norm_ple'], lw['w_pg'])


SORT_TILE = 512
SC_WINDOW = 128
SC_ROW_WORDS = 256
N_CHUNKS = D_MODEL // SC_ROW_WORDS
N_XCHUNKS = N_CHUNKS // 2


def _plan_kernel(gates_ref, dest_ref, total_ref, *, tb):
    lane = _lane_iota()
    is_g = (lane >= N_EXPERTS) & (lane < N_EXPERTS + N_GROUPS)
    total = jnp.sum(jnp.where(is_g, gates_ref[...], 0.0), axis=0, keepdims=True)
    padded = jnp.floor((total + (SORT_TILE - 1)) * (1.0 / SORT_TILE)) * SORT_TILE
    base = jnp.zeros((1, LANES), F32)
    for g in range(1, N_GROUPS):
        before = jnp.sum(jnp.where(lane < N_EXPERTS + g, padded, 0.0), axis=-1, keepdims=True)
        base = jnp.where(lane == N_EXPERTS + g, before, base)
    r = lax.broadcasted_iota(jnp.int32, (tb, tb), 0)
    c = lax.broadcasted_iota(jnp.int32, (tb, tb), 1)
    ltri = jnp.where(c < r, 1.0, 0.0).astype(BF16)
    carry = jnp.zeros((1, LANES), F32)
    for blk in range(gates_ref.shape[0] // tb):
        rows = slice(blk * tb, (blk + 1) * tb)
        member = jnp.where(is_g, gates_ref[rows, :], 0.0)
        rank = _dot(ltri, member.astype(BF16)) + carry
        carry = carry + jnp.sum(member, axis=0, keepdims=True)
        dest = jnp.sum(member * (base + rank), axis=-1, keepdims=True)
        dest_ref[rows, :] = jnp.broadcast_to(dest, (tb, LANES))
    total_ref[...] = jnp.broadcast_to(total, total_ref.shape)


def _plan(gates):
    t = gates.shape[0]
    return pl.pallas_call(
        functools.partial(_plan_kernel, tb=SORT_TILE),
        out_shape=[jax.ShapeDtypeStruct((t, LANES), F32), jax.ShapeDtypeStruct((SUBLANES, LANES), F32)],
        compiler_params=pltpu.CompilerParams(vmem_limit_bytes=VMEM_LIMIT), name="moe_plan")(gates)


def _sc_scatter_rows(x, idx, n_out):
    n, d = x.shape
    mesh = plsc.VectorSubcoreMesh(core_axis_name="core", subcore_axis_name="subcore")

    @functools.partial(pl.kernel, out_type=jax.ShapeDtypeStruct((n_out, d), x.dtype), mesh=mesh, scratch_types=[])
    def scatter(x_hbm, i_hbm, o_hbm):
        def body(x_vmem, i_vmem):
            pltpu.sync_copy(x_vmem, o_hbm.at[i_vmem.at[0]])

        pltpu.emit_pipeline(
            body, grid=(n // SC_WINDOW,),
            in_specs=[pl.BlockSpec((SC_WINDOW, d), index_map=lambda i: (i, 0)),
                      pl.BlockSpec((1, SC_WINDOW), index_map=lambda i: (0, i))],
            out_specs=[], core_axis_name=('core', 'subcore'), dimension_semantics=(pltpu.PARALLEL,))(x_hbm, i_hbm)

    return scatter(x, idx.reshape(1, n))


def _sc_gather_rows(x, idx):
    n = idx.shape[0]
    d = x.shape[1]
    mesh = plsc.VectorSubcoreMesh(core_axis_name="core", subcore_axis_name="subcore")

    @functools.partial(pl.kernel, out_type=jax.ShapeDtypeStruct((n, d), x.dtype), mesh=mesh)
    def gather(x_hbm, i_hbm, o_hbm):
        def body(i_vmem, o_vmem):
            pltpu.sync_copy(x_hbm.at[i_vmem.at[0]], o_vmem)

        pltpu.emit_pipeline(
            body, grid=(n // SC_WINDOW,),
            in_specs=[pl.BlockSpec((1, SC_WINDOW), index_map=lambda i: (0, i))],
            out_specs=[pl.BlockSpec((SC_WINDOW, d), index_map=lambda i: (i, 0))],
            core_axis_name=('core', 'subcore'), dimension_semantics=(pltpu.PARALLEL,))(i_hbm, o_hbm)

    return gather(x, idx.reshape(1, n))


def _experts_sorted_kernel(tg_ref, tv_ref, *refs):
    xs_refs, (gs_ref, wg_ref, wu_ref, wd_ref), ys_refs = (
        refs[:N_XCHUNKS], refs[N_XCHUNKS:N_XCHUNKS + 4], refs[-N_CHUNKS:])
    j = pl.program_id(0)

    @pl.when(tv_ref[j] > 0)
    def _():
        g = tg_ref[j]
        halves = []
        for r in xs_refs:
            w = r[...]
            halves.append(pltpu.bitcast(jnp.bitwise_and(w, jnp.uint32(0xFFFF0000)), F32))
            halves.append(pltpu.bitcast(jnp.left_shift(w, jnp.uint32(16)), F32))
        x = jnp.concatenate(halves, axis=1).astype(BF16)
        gates = gs_ref[...]
        acts = []
        for k in range(EXPERTS_PER_GROUP):
            hg = _dot(x, wg_ref[k])
            gate = jnp.sum(jnp.where(_lane_iota() == g * EXPERTS_PER_GROUP + k, gates, 0.0),
                           axis=-1, keepdims=True)
            acts.append(((hg * _sigmoid(hg)) * _dot(x, wu_ref[k]) * gate).astype(BF16))
        y = _dot(jnp.concatenate(acts, axis=1), wd_ref[...])
        for c, ref in enumerate(ys_refs):
            ref[...] = y[:, c * SC_ROW_WORDS:(c + 1) * SC_ROW_WORDS]


def _experts_sorted(xs_chunks, gs, tile_group, tile_valid, lw):
    n_rows = gs.shape[0]
    row = lambda w: pl.BlockSpec((SORT_TILE, w), lambda j, tg, tv: (j, 0))
    grp = lambda a: pl.BlockSpec((EXPERTS_PER_GROUP,) + a.shape[1:], lambda j, tg, tv: (tg[j], 0, 0))
    d_group = lw['w_ed'].shape[0] // N_GROUPS
    return pl.pallas_call(
        _experts_sorted_kernel,
        grid_spec=pltpu.PrefetchScalarGridSpec(
            num_scalar_prefetch=2, grid=(n_rows // SORT_TILE,),
            in_specs=[row(SC_ROW_WORDS)] * N_XCHUNKS + [
                row(LANES), grp(lw['w_eg']), grp(lw['w_eu']),
                pl.BlockSpec((d_group, D_MODEL), lambda j, tg, tv: (tg[j], 0))],
            out_specs=[row(SC_ROW_WORDS)] * N_CHUNKS),
        out_shape=[jax.ShapeDtypeStruct((n_rows, SC_ROW_WORDS), F32)] * N_CHUNKS,
        compiler_params=_cparams("arbitrary"), name="moe_sorted")(
            tile_group, tile_valid, *xs_chunks, gs, lw['w_eg'], lw['w_eu'], lw['w_ed'])


def _ple_kernel(h1_ref, *refs):
    y_refs, (p_ref, wple_ref, nple_ref, wpg_ref, out_ref) = refs[:N_CHUNKS], refs[N_CHUNKS:]
    h2 = h1_ref[...] + jnp.concatenate([r[...] for r in y_refs], axis=1)
    gate_ple = _sigmoid(_dot(_rms(h2, nple_ref[...]).astype(BF16), wpg_ref[...]))
    out_ref[...] = h2 + _dot(p_ref[0].astype(BF16), wple_ref[...]) * gate_ple


def _ple(h1, y_chunks, p_all, layer, lw, tm):
    t = h1.shape[0]
    row = lambda w: pl.BlockSpec((tm, w), lambda i: (i, 0))
    full = lambda a: pl.BlockSpec(a.shape, lambda i: (0,) * a.ndim)
    return pl.pallas_call(
        _ple_kernel, grid=(t // tm,),
        in_specs=[row(D_MODEL)] + [row(SC_ROW_WORDS)] * N_CHUNKS
        + [pl.BlockSpec((1, tm, p_all.shape[2]), lambda i: (layer, i, 0)),
           full(lw['w_ple']), full(lw['norm_ple']), full(lw['w_pg'])],
        out_specs=row(D_MODEL), out_shape=jax.ShapeDtypeStruct((t, D_MODEL), F32),
        compiler_params=_cparams("parallel"), name="ple")(
            h1, *y_chunks, p_all, lw['w_ple'], lw['norm_ple'], lw['w_pg'])


def _moe_dispatch(x_chunks, gates):
    t = gates.shape[0]
    n_rows = t + N_GROUPS * SORT_TILE
    dest_rep, total = _plan(gates)
    dest = dest_rep[:, 0].astype(jnp.int32)
    padded = (total[0, N_EXPERTS:N_EXPERTS + N_GROUPS].astype(jnp.int32) + (SORT_TILE - 1)) // SORT_TILE * SORT_TILE
    ends = jnp.cumsum(padded)
    tile_start = jnp.arange(n_rows // SORT_TILE, dtype=jnp.int32) * SORT_TILE
    tile_group = jnp.minimum(jnp.sum(tile_start[:, None] >= ends[None, :], axis=1), N_GROUPS - 1).astype(jnp.int32)
    tile_valid = (tile_start < ends[-1]).astype(jnp.int32)
    xs_chunks = [_sc_scatter_rows(xc, dest, n_rows) for xc in x_chunks]
    gs = _sc_scatter_rows(gates, dest, n_rows)
    return xs_chunks, gs, tile_group, tile_valid, dest


def _moe_combine(dispatched, h1, p_all, layer, lw):
    xs_chunks, gs, tile_group, tile_valid, dest = dispatched
    ys_chunks = _experts_sorted(xs_chunks, gs, tile_group, tile_valid, lw)
    y_chunks = [_sc_gather_rows(yc, dest) for yc in ys_chunks]
    return _ple(h1, y_chunks, p_all, layer, lw, _token_tile(h1.shape[0], 1024))


def _regroup_out_rows(wt):
    z = jnp.zeros((HEAD_DIM, wt.shape[1]), wt.dtype)
    return jnp.concatenate([wt[:W_A], wt[W_A:W_A + W_B], z, wt[W_A + W_B:], z], axis=0)


def _regroup_rows(w):
    z = jnp.zeros((HEAD_DIM, w.shape[1]), w.dtype)
    return jnp.stack([w[:W_A], jnp.concatenate([w[W_A:W_A + W_B], z], axis=0),
                      jnp.concatenate([w[W_A + W_B:], z], axis=0)])


def _pad_lanes(a, n=LANES):
    return jnp.pad(a, [(0, 0)] * (a.ndim - 1) + [(0, n - a.shape[-1])])


def _toeplitz_bias(rel_bias, off, tq, win):
    length = tq + win
    pad = length + abs(off)
    ext = jnp.flip(jnp.pad(rel_bias, ((0, 0), (pad, pad)), mode='edge'), axis=1)
    s1 = ext.shape[1] - 1 - (off + REL_CLIP + pad)
    v = jnp.concatenate([ext[:, s1:s1 + win], ext[:, s1 - tq:s1]], axis=1)
    flat = jnp.tile(v, (1, tq))[:, :tq * (length - 1)]
    return flat.reshape(-1, tq, length - 1)[:, :, :win]


def _block_toeplitz_bias(rel_bias, off, tq, win):
    nq, nk = tq // LANES, win // LANES
    blocks = {d: _toeplitz_bias(rel_bias, off - d * LANES, LANES, LANES) for d in range(-(nq - 1), nk)}
    return jnp.concatenate(
        [jnp.concatenate([blocks[b - a] for b in range(nk)], axis=2) for a in range(nq)], axis=1)


def _rel_tables(rel_bias, t_new, rows_b):
    tq, win = ATT_TILE, ATT_TILE + BAND
    q = jnp.arange(tq)[:, None]
    tabs = []
    for var in range(BAND // tq + 1):
        rel_k = jnp.arange(win)[None, :] - var * tq
        kch = jnp.floor_divide(rel_k, CHUNK)
        qch = q // CHUNK
        valid = (kch <= qch) & (kch >= qch - BAND_CHUNKS)
        tabs.append(jnp.where(valid[None], _block_toeplitz_bias(rel_bias, var * tq, tq, win), NEG_INF))
    tab_prompt = jnp.stack(tabs).astype(F32)
    tab_c = _toeplitz_bias(rel_bias, rows_b, t_new, rows_b).astype(F32)
    tab_n = _toeplitz_bias(rel_bias, 0, t_new, t_new).astype(F32)
    return tab_prompt, tab_c, tab_n


def _layer_weights(i, norm_mix, w_in, b_f, g_qa, g_ka, g_qb, g_kb, g_mix_out, w_out, norm_ffn,
                   w_rg, b_rg, w_re, b_re, w_eg, w_eu, w_ed, w_ple, norm_ple, w_pg):
    d_mix = W_A + W_B + W_C
    wt = jnp.transpose(w_in, (2, 0, 1))[:, i, :]
    w_f = wt[3 * d_mix:]
    w_big = jnp.concatenate(
        [_regroup_out_rows(wt[j * d_mix:(j + 1) * d_mix]) for j in range(3)]
        + [jnp.pad(w_f, ((0, LANES - w_f.shape[0]), (0, 0)))], axis=0).astype(BF16)
    tile6 = lambda g: jnp.tile(g, GROUP_W // HEAD_DIM)
    w_r = _pad_lanes(jnp.concatenate([w_re[i], w_rg[i]], axis=1))
    w_r_hi = w_r.astype(BF16)
    gm = g_mix_out[i]
    zpad = jnp.zeros((HEAD_DIM,), F32)
    g_mix = jnp.stack([gm[:W_A], jnp.concatenate([gm[W_A:W_A + W_B], zpad]),
                       jnp.concatenate([gm[W_A + W_B:], zpad])])[:, None, :]
    return dict(
        norm_mix=norm_mix[i][None], w_in=w_big,
        gq=jnp.concatenate([tile6(g_qa[i]), tile6(g_qb[i])])[None],
        gk=jnp.concatenate([tile6(g_ka[i]), tile6(g_kb[i])])[None],
        b_f=_pad_lanes(b_f[i])[None],
        g_mix=g_mix, w_out=_regroup_rows(w_out[i]).astype(BF16), norm_ffn=norm_ffn[i][None],
        w_r_hi=w_r_hi, w_r_lo=(w_r - w_r_hi.astype(F32)).astype(BF16),
        b_r=_pad_lanes(jnp.concatenate([b_re[i], b_rg[i]]))[None],
        w_eg=w_eg[i].astype(BF16), w_eu=w_eu[i].astype(BF16),
        w_ed=w_ed[i].reshape(-1, w_ed.shape[-1]).astype(BF16),
        w_ple=w_ple[i].astype(BF16), norm_ple=norm_ple[i][None], w_pg=w_pg[i].astype(BF16))


def _token_tile(t, pref):
    return pref if t % pref == 0 else t


def _feature_major(cache):
    d, n, p, h, e = cache.shape
    return jnp.transpose(cache, (0, 1, 3, 4, 2)).reshape(d, n, h * e, p)


def _position_major(rows, heads):
    d, n, _, p = rows.shape
    return jnp.transpose(rows.reshape(d, n, heads, HEAD_DIM, p), (0, 1, 4, 2, 3))


def kernel(x_prompt, x_sample, p_prompt, p_sample, cache_a_k, cache_a_v, cache_a_logf, cache_b_k, cache_b_v, cache_c_k, cache_c_v, norm_mix, w_in, b_f, g_qa, g_ka, g_qb, g_kb, rel_bias, g_mix_out, w_out, norm_ffn, w_router_group, b_router_group, w_router_expert, b_router_expert, w_exp_gate, w_exp_up, w_exp_down, w_ple, norm_ple, w_ple_gate):
    nb, s, d = x_prompt.shape
    ns, t_new, _ = x_sample.shape
    depth = w_in.shape[0]
    past = cache_a_k.shape[2]
    rows_b = cache_b_k.shape[2]
    assert d == D_MODEL and s % PROJ_TILE == 0 and s >= BAND + ATT_TILE and past % ATT_TILE == 0
    tp, ts = nb * s, ns * t_new

    hp = x_prompt.reshape(tp, d)
    hs = x_sample.reshape(ts, d)
    pp = p_prompt.reshape(depth, tp, -1)
    ps = p_sample.reshape(depth, ts, -1)
    caches = tuple(_feature_major(c) for c in (cache_a_k, cache_a_v, cache_b_k, cache_b_v, cache_c_k, cache_c_v))
    lf_rows = jnp.transpose(cache_a_logf, (0, 3, 1, 2))
    rsum = _suffix_sum(lf_rows.reshape(depth * H_A * ns, past)).reshape(depth, H_A, ns, past)
    rsum = jnp.transpose(rsum, (0, 2, 1, 3))

    rows_p = None
    new_s = [[] for _ in range(7)]
    for i in range(depth):
        lw = _layer_weights(i, norm_mix, w_in, b_f, g_qa, g_ka, g_qb, g_kb, g_mix_out, w_out, norm_ffn,
                            w_router_group, b_router_group, w_router_expert, b_router_expert,
                            w_exp_gate, w_exp_up, w_exp_down, w_ple, norm_ple, w_ple_gate)
        tab_p, tab_c, tab_n = _rel_tables(rel_bias[i] * LOG2E, t_new, rows_b)

        pr = _inproj_prompt(hp, lw, i, depth, nb, s, rows_p)
        qat, ka, vat, qb, kb, vb, qc, kc, vc, logf = pr[:10]
        rows_p = pr[10:]
        kaug, qaug = _forget_aug(logf, nb, s)
        oa = _attn_a(qat, ka, vat, kaug, qaug, nb, s)
        ob = _prompt_attention(
            _attn_b_kernel, "attn_b", qb, kb, vb, (tab_p,),
            [pl.BlockSpec(tab_p.shape, lambda b, j: (0, 0, 0, 0))], [], nb, s, ATT_TILE)
        oc = _prompt_attention(
            _attn_c_kernel, "attn_c", qc, kc, vc, (), [],
            [pltpu.VMEM((H_C, C_QTILE, PAIR_W), BF16), pltpu.VMEM((H_C, C_QTILE, LANES), F32),
             pltpu.VMEM((H_C, C_QTILE, PAIR_W), F32)], nb, s, C_QTILE)
        h1, gates, *x_chunks = _post(oa, ob, oc, hp, lw, _token_tile(tp, 512), N_XCHUNKS)
        dispatched = _moe_dispatch(x_chunks, gates)

        sr = _inproj_sample(hs, lw)
        oa, ob, oc = _sample_attention(sr, caches, i, rsum, tab_c, tab_n, ns, t_new)
        h1s, gates, xn = _post(oa, ob, oc, hs, lw, _token_tile(ts, 512))
        hs = _moe(xn, gates, h1s, ps, i, lw, _token_tile(ts, 1024))

        hp = _moe_combine(dispatched, h1, pp, i, lw)
        akf, avf, bkf, bvf, ckf, cvf, logf = sr[9:]
        rows = (akf.reshape(ns, t_new, H_A, HEAD_DIM), avf.reshape(ns, t_new, H_A, HEAD_DIM),
                logf[:, :H_A].reshape(ns, t_new, H_A),
                bkf.reshape(ns, t_new, H_B, HEAD_DIM), bvf.reshape(ns, t_new, H_B, HEAD_DIM),
                ckf.reshape(ns, t_new, H_C, HEAD_DIM), cvf.reshape(ns, t_new, H_C, HEAD_DIM))
        for j in range(7):
            new_s[j].append(rows[j])

    akt, avt, bkt, bvt, ckt, cvt, lft = rows_p
    outs_p = (_position_major(akt, H_A), _position_major(avt, H_A),
              jnp.transpose(lft.reshape(depth, H_A, nb, s), (0, 2, 3, 1)),
              _position_major(bkt, H_B), _position_major(bvt, H_B),
              _position_major(ckt, H_C), _position_major(cvt, H_C))
    outs_s = [jnp.stack(r, axis=0) for r in new_s]
    return (hp.reshape(nb, s, d), hs.reshape(ns, t_new, d), *outs_p, *outs_s)
```

```python
import functools
import math

import jax
import jax.numpy as jnp
from jax import lax
from jax.experimental import pallas as pl
from jax.experimental.pallas import tpu as pltpu
from jax.experimental.pallas import tpu_sc as plsc

F32 = jnp.float32
BF16 = jnp.bfloat16

D_MODEL = 1024
HEAD_DIM = 64
H_A, H_B, H_C = 6, 5, 5
W_A, W_B, W_C = H_A * HEAD_DIM, H_B * HEAD_DIM, H_C * HEAD_DIM
GROUP_W = 384
PAIR_W = 2 * HEAD_DIM
N_PAIRS = GROUP_W // PAIR_W
CHUNK = 64
BAND_CHUNKS = 8
BAND = BAND_CHUNKS * CHUNK
REL_CLIP = 128
N_GROUPS = 4
EXPERTS_PER_GROUP = 4
N_EXPERTS = N_GROUPS * EXPERTS_PER_GROUP
EPS = 1e-6
ATTN_SCALE = HEAD_DIM ** -0.5
LOG2E = math.log2(math.e)
Q_SCALE = ATTN_SCALE * LOG2E
NEG_INF = -1e30
LANES = 128
SUBLANES = 8
VMEM_LIMIT = 48 * 1024 * 1024
EXPERTS_PER_STEP = 4
MOE_VMEM_LIMIT = 58 * 1024 * 1024

ATT_TILE = 256
A_QTILE = 512
C_QTILE = 256
PROJ_TILE = 1024
PROJ_VMEM_LIMIT = 56 * 1024 * 1024
AUG_SLOTS = 8

_NT = (((1,), (1,)), ((), ()))


def _cparams(*sem):
    return pltpu.CompilerParams(dimension_semantics=sem, vmem_limit_bytes=VMEM_LIMIT)


def _dot(a, b):
    return jnp.dot(a, b, preferred_element_type=F32)


def _dot_nt(a, b):
    return lax.dot_general(a, b, _NT, preferred_element_type=F32)


def _split3(x):
    hi = x.astype(BF16)
    r1 = x - hi.astype(F32)
    mid = r1.astype(BF16)
    lo = (r1 - mid.astype(F32)).astype(BF16)
    return hi, mid, lo


def _split2(x):
    hi = x.astype(BF16)
    lo = (x - hi.astype(F32)).astype(BF16)
    return hi, lo


def _lane_iota(n=LANES):
    return lax.broadcasted_iota(jnp.int32, (1, n), 1)


def _half_mask(half):
    lane = _lane_iota()
    return (lane < HEAD_DIM) if half == 0 else (lane >= HEAD_DIM)


def _pair_slice(h):
    return slice((h // 2) * PAIR_W, (h // 2 + 1) * PAIR_W)


def _softplus2(z2):
    return jnp.maximum(z2, 0.0) + jnp.log2(1.0 + jnp.exp2(-jnp.abs(z2)))


def _sigmoid(z):
    return 1.0 / (1.0 + jnp.exp(-z))


def _rms(x, g):
    return x * lax.rsqrt(jnp.mean(x * x, axis=-1, keepdims=True) + EPS) * g


def _pair_rms(z, gain):
    first = _half_mask(0)
    outs = []
    for j in range(N_PAIRS):
        blk = z[:, j * PAIR_W:(j + 1) * PAIR_W]
        sq = blk * blk
        lo = jnp.sum(jnp.where(first, sq, 0.0), axis=-1, keepdims=True)
        hi = jnp.sum(jnp.where(first, 0.0, sq), axis=-1, keepdims=True)
        ms = jnp.where(first, lo, hi) * (1.0 / HEAD_DIM)
        outs.append(blk * lax.rsqrt(ms + EPS) * gain[:, j * PAIR_W:(j + 1) * PAIR_W])
    return jnp.concatenate(outs, axis=-1)


def _projector(x_ref, gn_ref, w_ref):
    xb = _rms(x_ref[...], gn_ref[...]).astype(BF16)
    return lambda g, width=GROUP_W: _dot_nt(xb, w_ref[g * GROUP_W:g * GROUP_W + width, :])


def _log_forget(zf, bf_ref):
    zf = zf + bf_ref[...]
    return jnp.minimum(zf, 0.0) - jnp.log(1.0 + jnp.exp(-jnp.abs(zf)))


def _inproj_sample_kernel(x_ref, gn_ref, w_ref, gq_ref, gk_ref, bf_ref,
                          qa_ref, qb_ref, qc_ref, ka_ref, kb_ref, kc_ref, va_ref, vb_ref, vc_ref,
                          akf_ref, avf_ref, bkf_ref, bvf_ref, ckf_ref, cvf_ref, logf_ref):
    proj = _projector(x_ref, gn_ref, w_ref)
    qa_ref[...] = (_pair_rms(proj(0), gq_ref[:, :GROUP_W]) * Q_SCALE).astype(BF16)
    qb_ref[...] = (_pair_rms(proj(1), gq_ref[:, GROUP_W:]) * Q_SCALE).astype(BF16)
    qc_ref[...] = (proj(2) * Q_SCALE).astype(BF16)
    for g, norm, bf_ref_, f_ref, width in (
            (3, gk_ref[:, :GROUP_W], ka_ref, akf_ref, W_A), (4, gk_ref[:, GROUP_W:], kb_ref, bkf_ref, W_B),
            (5, None, kc_ref, ckf_ref, W_C), (6, None, va_ref, avf_ref, W_A),
            (7, None, vb_ref, bvf_ref, W_B), (8, None, vc_ref, cvf_ref, W_C)):
        z = proj(g)
        if norm is not None:
            z = _pair_rms(z, norm)
        f_ref[...] = z[:, :width]
        bf_ref_[...] = z.astype(BF16)
    logf_ref[...] = _log_forget(proj(9, LANES), bf_ref)


def _inproj_sample(x, lw):
    t = x.shape[0]
    row = lambda w: pl.BlockSpec((t, w), lambda i: (0, 0))
    full = lambda a: pl.BlockSpec(a.shape, lambda i: (0,) * a.ndim)
    widths = [GROUP_W] * 9 + [W_A, W_A, W_B, W_B, W_C, W_C, LANES]
    out_shape = [jax.ShapeDtypeStruct((t, w), BF16 if j < 9 else F32) for j, w in enumerate(widths)]
    ins = (x, lw['norm_mix'], lw['w_in'], lw['gq'], lw['gk'], lw['b_f'])
    return pl.pallas_call(
        _inproj_sample_kernel, grid=(1,),
        in_specs=[row(D_MODEL)] + [full(a) for a in ins[1:]],
        out_specs=[row(w) for w in widths], out_shape=out_shape,
        compiler_params=_cparams("arbitrary"), name="inproj_sample")(*ins)


N_PROMPT_ROWS = 7


def _inproj_prompt_kernel(*refs, n_alias):
    x_ref, gn_ref, w_ref, gq_ref, gk_ref, bf_ref = refs[:6]
    (qat_ref, ka_ref, vat_ref, qb_ref, kb_ref, vb_ref, qc_ref, kc_ref, vc_ref, logf_ref,
     akt_ref, avt_ref, bkt_ref, bvt_ref, ckt_ref, cvt_ref, lft_ref) = refs[6 + n_alias:]
    proj = _projector(x_ref, gn_ref, w_ref)
    qat_ref[...] = jnp.transpose(_pair_rms(proj(0), gq_ref[:, :GROUP_W]) * Q_SCALE).astype(BF16)
    qb_ref[...] = (_pair_rms(proj(1), gq_ref[:, GROUP_W:]) * Q_SCALE).astype(BF16)
    qc_ref[...] = (proj(2) * Q_SCALE).astype(BF16)
    ka = _pair_rms(proj(3), gk_ref[:, :GROUP_W])
    ka_ref[...] = ka.astype(BF16)
    akt_ref[0, 0] = jnp.transpose(ka)
    kb = _pair_rms(proj(4), gk_ref[:, GROUP_W:])
    kb_ref[...] = kb.astype(BF16)
    kc = proj(5)
    kc_ref[...] = kc.astype(BF16)
    ckt_ref[0, 0] = jnp.transpose(kc)[:W_C]
    vat = jnp.transpose(proj(6))
    avt_ref[0, 0] = vat
    for j in range(vat_ref.shape[1]):
        vat_ref[0, j] = vat[:, j * ATT_TILE:(j + 1) * ATT_TILE].astype(BF16)
    vb = proj(7)
    vb_ref[...] = vb.astype(BF16)
    vc = proj(8)
    vc_ref[...] = vc.astype(BF16)
    cvt_ref[0, 0] = jnp.transpose(vc)[:W_C]

    @pl.when(pl.program_id(1) == pl.num_programs(1) - 1)
    def _():
        keep = kb.shape[0] - BAND
        bkt_ref[0, 0] = jnp.transpose(kb)[:W_B, keep:]
        bvt_ref[0, 0] = jnp.transpose(vb)[:W_B, keep:]

    lf = _log_forget(proj(9, LANES), bf_ref)
    logf_ref[...] = lf
    lft_ref[0] = jnp.transpose(lf)[:H_A]


def _inproj_prompt(x, lw, layer, depth, nb, s, prev_rows):
    tm = PROJ_TILE
    nt = s // tm
    t = nb * s
    per = tm // ATT_TILE
    row = lambda w: pl.BlockSpec((tm, w), lambda b, i: (b * nt + i, 0))
    full = lambda a: pl.BlockSpec(a.shape, lambda b, i: (0,) * a.ndim, pipeline_mode=pl.Buffered(1))
    feat = lambda w: pl.BlockSpec((1, 1, w, tm), lambda b, i: (layer, b, 0, i))
    band = lambda w: pl.BlockSpec((1, 1, w, BAND), lambda b, i: (layer, b, 0, 0))
    tok = jax.ShapeDtypeStruct((t, GROUP_W), BF16)
    out_shape = [
        jax.ShapeDtypeStruct((GROUP_W, t), BF16), tok,
        jax.ShapeDtypeStruct((nb, s // ATT_TILE, GROUP_W, ATT_TILE), BF16),
        tok, tok, tok, tok, tok, tok, jax.ShapeDtypeStruct((t, LANES), F32),
        jax.ShapeDtypeStruct((depth, nb, W_A, s), F32), jax.ShapeDtypeStruct((depth, nb, W_A, s), F32),
        jax.ShapeDtypeStruct((depth, nb, W_B, BAND), F32), jax.ShapeDtypeStruct((depth, nb, W_B, BAND), F32),
        jax.ShapeDtypeStruct((depth, nb, W_C, s), F32), jax.ShapeDtypeStruct((depth, nb, W_C, s), F32),
        jax.ShapeDtypeStruct((depth, H_A, t), F32)]
    out_specs = [
        pl.BlockSpec((GROUP_W, tm), lambda b, i: (0, b * nt + i)), row(GROUP_W),
        pl.BlockSpec((1, per, GROUP_W, ATT_TILE), lambda b, i: (b, i, 0, 0)),
        row(GROUP_W), row(GROUP_W), row(GROUP_W), row(GROUP_W), row(GROUP_W), row(GROUP_W), row(LANES),
        feat(W_A), feat(W_A), band(W_B), band(W_B), feat(W_C), feat(W_C),
        pl.BlockSpec((1, H_A, tm), lambda b, i: (layer, 0, b * nt + i))]
    ins = [x, lw['norm_mix'], lw['w_in'], lw['gq'], lw['gk'], lw['b_f']]
    in_specs = [row(D_MODEL)] + [full(a) for a in ins[1:]]
    aliases = {}
    n_alias = 0
    if prev_rows is not None:
        n_alias = N_PROMPT_ROWS
        first_row_out = len(out_shape) - N_PROMPT_ROWS
        for j, a in enumerate(prev_rows):
            aliases[len(ins)] = first_row_out + j
            ins.append(a)
            in_specs.append(pl.BlockSpec(memory_space=pl.ANY))
    return pl.pallas_call(
        functools.partial(_inproj_prompt_kernel, n_alias=n_alias), grid=(nb, nt),
        in_specs=in_specs, out_specs=out_specs, out_shape=out_shape,
        input_output_aliases=aliases,
        compiler_params=pltpu.CompilerParams(dimension_semantics=("parallel", "arbitrary"),
                                             vmem_limit_bytes=PROJ_VMEM_LIMIT),
        name="inproj_prompt")(*ins)


def _forget_aug_kernel(logf_ref, kaug_ref, qaug_ref, *, nblk):
    tb = ATT_TILE
    r = lax.broadcasted_iota(jnp.int32, (tb, tb), 0)
    c = lax.broadcasted_iota(jnp.int32, (tb, tb), 1)
    tri = jnp.where(c <= r, 1.0, 0.0).astype(BF16)
    src = lax.broadcasted_iota(jnp.int32, (LANES, LANES), 0)
    dst = lax.broadcasted_iota(jnp.int32, (LANES, LANES), 1)

    def place(slot):
        return jnp.where((dst == AUG_SLOTS * src + slot) & (src < H_A), 1.0, 0.0).astype(BF16)

    lane = _lane_iota()
    slot = jnp.bitwise_and(lane, AUG_SLOTS - 1)
    used = lane < AUG_SLOTS * H_A
    key_const = jnp.where(used & (slot >= 3) & (slot < 6), 1.0, 0.0)
    qry_const = jnp.where(used & (slot < 3), -1.0, 0.0)
    carry = jnp.zeros((1, LANES), F32)
    for blk in range(nblk):
        rows = slice(blk * tb, (blk + 1) * tb)
        hi, mid, lo = _split3(logf_ref[rows, :])
        cs = (_dot(tri, hi) + _dot(tri, mid) + _dot(tri, lo)) + carry
        carry = cs[tb - 1:tb, :]
        hi, mid, lo = _split3(cs * LOG2E)
        kaug = _dot(hi, place(0)) + _dot(mid, place(1)) + _dot(lo, place(2)) + key_const
        kaug_ref[rows, :] = kaug.astype(BF16)
        qaug = _dot(hi, place(3)) + _dot(mid, place(4)) + _dot(lo, place(5)) + qry_const
        qaug_ref[:, rows] = jnp.transpose(qaug).astype(BF16)


def _forget_aug(logf, nb, s):
    return pl.pallas_call(
        functools.partial(_forget_aug_kernel, nblk=s // ATT_TILE), grid=(nb,),
        in_specs=[pl.BlockSpec((s, LANES), lambda b: (b, 0))],
        out_specs=[pl.BlockSpec((s, LANES), lambda b: (b, 0)), pl.BlockSpec((LANES, s), lambda b: (0, b))],
        out_shape=[jax.ShapeDtypeStruct((nb * s, LANES), BF16), jax.ShapeDtypeStruct((LANES, nb * s), BF16)],
        compiler_params=_cparams("parallel"), name="forget_aug")(logf)


def _suffix_sum_kernel(x_ref, o_ref):
    p = x_ref.shape[1]
    j = lax.broadcasted_iota(jnp.int32, (p, p), 0)
    s = lax.broadcasted_iota(jnp.int32, (p, p), 1)
    tri = jnp.where(j > s, 1.0, 0.0).astype(BF16)
    hi, mid, lo = _split3(x_ref[...])
    o_ref[...] = _dot(hi, tri) + _dot(mid, tri) + _dot(lo, tri)


def _suffix_sum(x):
    return pl.pallas_call(
        _suffix_sum_kernel, out_shape=jax.ShapeDtypeStruct(x.shape, F32),
        compiler_params=pltpu.CompilerParams(vmem_limit_bytes=VMEM_LIMIT), name="suffix_sum")(x)


def _attn_a_kernel(qt_ref, k_ref, vt_ref, kaug_ref, qaug_ref, o_ref, qp_ref, m_ref, l_ref, acc_ref):
    tq = qt_ref.shape[1]
    tk = ATT_TILE
    qi = pl.program_id(1)
    row = lax.broadcasted_iota(jnp.int32, (PAIR_W, 1), 0)
    aug = qaug_ref[...]
    for h in range(H_A):
        q2 = qt_ref[_pair_slice(h), :]
        half = (row < HEAD_DIM) if h % 2 == 0 else (row >= HEAD_DIM)
        qp_ref[h, :PAIR_W, :] = jnp.where(half, q2, jnp.zeros_like(q2))
        mine = (row >= AUG_SLOTS * h) & (row < AUG_SLOTS * (h + 1))
        qp_ref[h, PAIR_W:, :] = jnp.where(mine, aug, jnp.zeros_like(aug))
    m_ref[...] = jnp.full(m_ref.shape, NEG_INF, F32)
    l_ref[...] = jnp.zeros(l_ref.shape, F32)
    acc_ref[...] = jnp.zeros(acc_ref.shape, F32)

    def step(kt, key_offset):
        ks = pl.multiple_of(kt * tk, tk)
        ka = kaug_ref[pl.ds(ks, tk), :]
        tile_scores = [_dot(jnp.concatenate([k_ref[pl.ds(ks, tk), _pair_slice(h)], ka], axis=1), qp_ref[h])
                       for h in range(H_A)]
        probs, alphas = [], []
        for h in range(H_A):
            s = tile_scores[h]
            if key_offset is not None:
                key = lax.broadcasted_iota(jnp.int32, (tk, tq), 0)
                qry = lax.broadcasted_iota(jnp.int32, (tk, tq), 1)
                s = jnp.where(key + key_offset <= qry, s, NEG_INF)
            m_prev = m_ref[h]
            m_new = jnp.maximum(m_prev, jnp.max(s, axis=0, keepdims=True))
            alpha = jnp.exp2(m_prev - m_new)
            p = jnp.exp2(s - m_new)
            l_ref[h] = alpha * l_ref[h] + jnp.sum(p, axis=0, keepdims=True)
            m_ref[h] = m_new
            alphas.append(alpha)
            probs.append(p.astype(BF16))
        for h in range(H_A):
            acc_ref[h] = alphas[h] * acc_ref[h] + _dot(vt_ref[0, kt, _pair_slice(h), :], probs[h])

    def body(kt, carry):
        step(kt, None)
        return carry

    per = tq // tk
    lax.fori_loop(0, qi * per, body, 0)
    for j in range(per):
        step(qi * per + j, j * tk)
    for pair in range(N_PAIRS):
        h = 2 * pair
        out_t = jnp.where(row < HEAD_DIM, acc_ref[h] * (1.0 / l_ref[h]), acc_ref[h + 1] * (1.0 / l_ref[h + 1]))
        o_ref[:, pair * PAIR_W:(pair + 1) * PAIR_W] = jnp.transpose(out_t)


def _attn_a(qat, ka, vat, kaug, qaug, nb, s):
    tq = A_QTILE
    nq = s // tq
    return pl.pallas_call(
        _attn_a_kernel, grid=(nb, nq),
        in_specs=[pl.BlockSpec((GROUP_W, tq), lambda b, i: (0, b * nq + i)),
                  pl.BlockSpec((s, GROUP_W), lambda b, i: (b, 0)),
                  pl.BlockSpec((1, s // ATT_TILE, GROUP_W, ATT_TILE), lambda b, i: (b, 0, 0, 0)),
                  pl.BlockSpec((s, LANES), lambda b, i: (b, 0)),
                  pl.BlockSpec((LANES, tq), lambda b, i: (0, b * nq + i))],
        out_specs=pl.BlockSpec((tq, GROUP_W), lambda b, i: (b * nq + i, 0)),
        out_shape=jax.ShapeDtypeStruct((nb * s, GROUP_W), F32),
        scratch_shapes=[pltpu.VMEM((H_A, 2 * PAIR_W, tq), BF16), pltpu.VMEM((H_A, 1, tq), F32),
                        pltpu.VMEM((H_A, 1, tq), F32), pltpu.VMEM((H_A, PAIR_W, tq), F32)],
        compiler_params=_cparams("parallel", "arbitrary"), name="attn_a")(qat, ka, vat, kaug, qaug)


def _masked_q(q_ref, qm_ref, n_heads):
    for h in range(n_heads):
        q2 = q_ref[:, _pair_slice(h)]
        qm_ref[h] = jnp.where(_half_mask(h % 2), q2, jnp.zeros_like(q2))


def _attn_b_kernel(q_ref, k_ref, v_ref, tab_ref, o_ref):
    tq = ATT_TILE
    win = tq + BAND
    i = pl.program_id(1)
    var = jnp.minimum(i, BAND // tq)
    ws = pl.multiple_of(jnp.maximum(i * tq - BAND, 0), tq)
    heads = range(H_B)
    scores = []
    for h in heads:
        q2 = q_ref[:, _pair_slice(h)]
        qm = jnp.where(_half_mask(h % 2), q2, jnp.zeros_like(q2))
        scores.append(_dot_nt(qm, k_ref[pl.ds(ws, win), _pair_slice(h)]) + tab_ref[var, h])
    probs, norms = [], []
    for h in heads:
        p = jnp.exp2(scores[h] - jnp.max(scores[h], axis=-1, keepdims=True))
        norms.append(1.0 / jnp.sum(p, axis=-1, keepdims=True))
        probs.append(p.astype(BF16))
    res = [_dot(probs[h], v_ref[pl.ds(ws, win), _pair_slice(h)]) * norms[h] for h in heads]
    res.append(jnp.zeros((tq, PAIR_W), F32))
    first = _half_mask(0)
    for pair in range(N_PAIRS):
        o_ref[:, pair * PAIR_W:(pair + 1) * PAIR_W] = jnp.where(first, res[2 * pair], res[2 * pair + 1])


def _suffix_tri(n):
    j = lax.broadcasted_iota(jnp.int32, (2 * n, n), 0)
    s = lax.broadcasted_iota(jnp.int32, (2 * n, n), 1)
    return jnp.where(jnp.where(j >= n, j - n, j) >= s, 1.0, 0.0).astype(BF16)


def _stick_scores(z, tri, seen):
    sp = _softplus2(z)
    if seen is not None:
        sp = jnp.where(seen, sp, 0.0)
    hi, lo = _split2(sp)
    n = sp.shape[1]
    if n % LANES == 0:
        return _dot(jnp.concatenate([hi, lo], axis=1), tri)
    return _dot(hi, tri[:n]) + _dot(lo, tri[:n])


def _stick_weights(z, s_in, carry, seen):
    a = jnp.exp2(z - (s_in + carry))
    if seen is not None:
        a = jnp.where(seen, a, 0.0)
    return a.astype(BF16)


def _attn_c_kernel(q_ref, k_ref, v_ref, o_ref, qm_ref, carry_ref, acc_ref):
    tq = q_ref.shape[0]
    tk = ATT_TILE
    qi = pl.program_id(1)
    _masked_q(q_ref, qm_ref, H_C)
    carry_ref[...] = jnp.zeros(carry_ref.shape, F32)
    acc_ref[...] = jnp.zeros(acc_ref.shape, F32)
    heads = range(H_C)

    def step(kt, key_offset):
        ks = pl.multiple_of(kt * tk, tk)
        zs = [_dot_nt(qm_ref[h], k_ref[pl.ds(ks, tk), _pair_slice(h)]) for h in heads]
        tri = _suffix_tri(tk)
        seen = None
        if key_offset is not None:
            r = lax.broadcasted_iota(jnp.int32, (tq, tk), 0)
            c = lax.broadcasted_iota(jnp.int32, (tq, tk), 1)
            seen = c + key_offset < r
        sums = [_stick_scores(zs[h], tri, seen) for h in heads]
        weights = []
        for h in heads:
            carry = carry_ref[h][:, 0:1]
            weights.append(_stick_weights(zs[h], sums[h], carry, seen))
            carry_ref[h] = jnp.broadcast_to(carry + sums[h][:, 0:1], (tq, LANES))
        for h in heads:
            acc_ref[h] += _dot(weights[h], v_ref[pl.ds(ks, tk), _pair_slice(h)])

    per = tq // tk

    def body(j, c):
        step(qi * per - 1 - j, None)
        return c

    for j in reversed(range(per)):
        step(qi * per + j, j * tk)
    lax.fori_loop(0, qi * per, body, 0)
    first = _half_mask(0)
    for pair in range(N_PAIRS):
        h = 2 * pair
        second = acc_ref[h + 1] if h + 1 < H_C else jnp.zeros((tq, PAIR_W), F32)
        o_ref[:, pair * PAIR_W:(pair + 1) * PAIR_W] = jnp.where(first, acc_ref[h], second)


def _prompt_attention(kernel, name, q, k, v, extra, extra_specs, scratch, nb, s, tq):
    nq = s // tq
    qspec = pl.BlockSpec((tq, GROUP_W), lambda b, i: (b * nq + i, 0))
    kvspec = pl.BlockSpec((s, GROUP_W), lambda b, i: (b, 0))
    return pl.pallas_call(
        kernel, grid=(nb, nq),
        in_specs=[qspec, kvspec, kvspec] + extra_specs,
        out_specs=qspec, out_shape=jax.ShapeDtypeStruct((nb * s, GROUP_W), F32),
        scratch_shapes=scratch,
        compiler_params=_cparams("parallel", "arbitrary"), name=name)(q, k, v, *extra)


def _sample_heads(n_heads, q_ref, width):
    out = []
    for h in range(n_heads):
        lo = (h // 2) * PAIR_W
        span = slice(lo, min(lo + PAIR_W, width))
        q2 = q_ref[:, span]
        if span.stop - span.start == PAIR_W:
            q2 = jnp.where(_half_mask(h % 2), q2, jnp.zeros_like(q2))
        out.append((h, span, q2))
    return out


def _store_heads(o_ref, res, n_heads):
    for pair in range(N_PAIRS):
        lo = pair * PAIR_W
        h = 2 * pair
        if h + 1 < n_heads:
            o_ref[:, lo:lo + PAIR_W] = jnp.where(_half_mask(0), res[h], res[h + 1])
        else:
            o_ref[:, lo:lo + HEAD_DIM] = res[h]
            o_ref[:, lo + HEAD_DIM:lo + PAIR_W] = jnp.zeros_like(res[h])


def _joint_softmax_pv(s_c, s_n, vt_c, v_n):
    m = jnp.maximum(jnp.max(s_c, axis=-1, keepdims=True), jnp.max(s_n, axis=-1, keepdims=True))
    p_c = jnp.exp2(s_c - m)
    p_n = jnp.exp2(s_n - m)
    l = jnp.sum(p_c, axis=-1, keepdims=True) + jnp.sum(p_n, axis=-1, keepdims=True)
    return (_dot_nt(p_c.astype(BF16), vt_c) + _dot(p_n.astype(BF16), v_n)) / l


def _sample_attn_kernel(qa_ref, qb_ref, qc_ref, kan_ref, kbn_ref, kcn_ref, van_ref, vbn_ref, vcn_ref,
                        cak_ref, cav_ref, cbk_ref, cbv_ref, cck_ref, ccv_ref,
                        rsum_ref, logf_ref, tabc_ref, tabn_ref,
                        oa_ref, ob_ref, oc_ref):
    t = qa_ref.shape[0]
    past = cak_ref.shape[3]
    r = lax.broadcasted_iota(jnp.int32, (t, t), 0)
    c = lax.broadcasted_iota(jnp.int32, (t, t), 1)

    ltri = jnp.where(c <= r, 1.0, 0.0).astype(BF16)
    hi, mid, lo = _split3(logf_ref[...])
    pcol = (_dot(ltri, hi) + _dot(ltri, mid) + _dot(ltri, lo)) * LOG2E
    prow = jnp.transpose(jnp.concatenate([pcol, jnp.zeros((LANES - t, LANES), F32)], axis=0))
    heads = _sample_heads(H_A, qa_ref, W_A)
    kts = {sp.start: cak_ref[0, 0, sp, :].astype(BF16) for _, sp, _ in heads}
    vts = {sp.start: cav_ref[0, 0, sp, :].astype(BF16) for _, sp, _ in heads}
    s_c = [_dot(qm, kts[sp.start]) + (pcol[:, h:h + 1] + rsum_ref[0, 0, h:h + 1, :] * LOG2E)
           for h, sp, qm in heads]
    s_n = [jnp.where(c <= r, _dot_nt(qm, kan_ref[:, sp]) + (pcol[:, h:h + 1] - prow[h:h + 1, 0:t]), NEG_INF)
           for h, sp, qm in heads]
    _store_heads(oa_ref, [_joint_softmax_pv(s_c[h], s_n[h], vts[sp.start], van_ref[:, sp])
                          for h, sp, _ in heads], H_A)

    heads = _sample_heads(H_B, qb_ref, W_B)
    kts = {sp.start: cbk_ref[0, 0, sp, :].astype(BF16) for _, sp, _ in heads}
    vts = {sp.start: cbv_ref[0, 0, sp, :].astype(BF16) for _, sp, _ in heads}
    s_c = [_dot(qm, kts[sp.start]) + tabc_ref[h] for h, sp, qm in heads]
    s_n = [_dot_nt(qm, kbn_ref[:, sp]) + tabn_ref[h] for h, sp, qm in heads]
    _store_heads(ob_ref, [_joint_softmax_pv(s_c[h], s_n[h], vts[sp.start], vbn_ref[:, sp])
                          for h, sp, _ in heads], H_B)

    heads = _sample_heads(H_C, qc_ref, W_C)
    nblk = past // ATT_TILE
    tri_n = _suffix_tri(t)
    tri_c = _suffix_tri(ATT_TILE)
    seen = c < r
    kts = {sp.start: cck_ref[0, 0, sp, :].astype(BF16) for _, sp, _ in heads}
    vts = {sp.start: ccv_ref[0, 0, sp, :].astype(BF16) for _, sp, _ in heads}
    z_n = [_dot_nt(qm, kcn_ref[:, sp]) for _, sp, qm in heads]
    z_c = [_dot(qm, kts[sp.start]) for _, sp, qm in heads]
    sum_n = [_stick_scores(z, tri_n, seen) for z in z_n]
    sum_c = [[_stick_scores(z[:, b * ATT_TILE:(b + 1) * ATT_TILE], tri_c, None) for b in range(nblk)]
             for z in z_c]
    res = []
    for h, sp, _ in heads:
        acc = _dot(_stick_weights(z_n[h], sum_n[h], 0.0, seen), vcn_ref[:, sp])
        carry = sum_n[h][:, 0:1]
        for b in reversed(range(nblk)):
            cols = slice(b * ATT_TILE, (b + 1) * ATT_TILE)
            a = _stick_weights(z_c[h][:, cols], sum_c[h][b], carry, None)
            acc = acc + _dot_nt(a, vts[sp.start][:, cols])
            carry = carry + sum_c[h][b][:, 0:1]
        res.append(acc)
    _store_heads(oc_ref, res, H_C)


def _sample_attention(proj, caches, layer, rsum, tabc, tabn, nb, t):
    qa, qb, qc, ka, kb, kc, va, vb, vc = proj[:9]
    logf = proj[15]
    new = pl.BlockSpec((t, GROUP_W), lambda b: (b, 0))
    cache = lambda a: pl.BlockSpec((1, 1) + a.shape[2:], lambda b: (layer, b, 0, 0))
    full = lambda a: pl.BlockSpec(a.shape, lambda b: (0,) * a.ndim)
    return pl.pallas_call(
        _sample_attn_kernel, grid=(nb,),
        in_specs=[new] * 9 + [cache(a) for a in caches] + [
            cache(rsum), pl.BlockSpec((t, LANES), lambda b: (b, 0)), full(tabc), full(tabn)],
        out_specs=[new] * 3,
        out_shape=[jax.ShapeDtypeStruct((nb * t, GROUP_W), F32)] * 3,
        compiler_params=_cparams("parallel"), name="sample_attn")(
            qa, qb, qc, ka, kb, kc, va, vb, vc, *caches, rsum, logf, tabc, tabn)


def _masked_max(x, mask):
    return jnp.max(jnp.where(mask, x, -jnp.inf), axis=-1, keepdims=True)


def _first_lane(mask, lane):
    return jnp.min(jnp.where(mask, lane, float(LANES)), axis=-1, keepdims=True)


def _route(logits):
    lane = _lane_iota().astype(F32)
    is_g = (lane >= N_EXPERTS) & (lane < N_EXPERTS + N_GROUPS)
    gmax = _masked_max(logits, is_g)
    p_g = 1.0 / jnp.sum(jnp.where(is_g, jnp.exp(logits - gmax), 0.0), axis=-1, keepdims=True)
    g_sel = _first_lane(is_g & (logits == gmax), lane) - N_EXPERTS
    lo = g_sel * EXPERTS_PER_GROUP
    in_g = (lane >= lo) & (lane < lo + EXPERTS_PER_GROUP)
    l1 = _masked_max(logits, in_g)
    i1 = _first_lane(in_g & (logits == l1), lane)
    rest = in_g & (lane != i1)
    l2 = _masked_max(logits, rest)
    i2 = _first_lane(rest & (logits == l2), lane)
    e2 = jnp.exp(l2 - l1)
    w1 = p_g / (1.0 + e2)
    w2 = p_g * e2 / (1.0 + e2)
    gates = jnp.where(lane == i1, w1, jnp.where(lane == i2, w2, 0.0))
    return jnp.where(lane == g_sel + N_EXPERTS, 1.0, gates)


def _post_kernel(oa_ref, ob_ref, oc_ref, h_ref, gmix_ref, wout_ref, nffn_ref, wrh_ref, wrl_ref, br_ref,
                 h1_ref, gates_ref, *x_refs, packed):
    def gnorm(o, width):
        ms = jnp.sum(o * o, axis=-1, keepdims=True) * (1.0 / width)
        return o * lax.rsqrt(ms + EPS)

    h1 = h_ref[...]
    for g, (o_ref, width) in enumerate(((oa_ref, W_A), (ob_ref, W_B), (oc_ref, W_C))):
        merged = (gnorm(o_ref[...], width) * gmix_ref[g]).astype(BF16)
        h1 = h1 + _dot(merged, wout_ref[g])
    h1_ref[...] = h1
    xf = _rms(h1, nffn_ref[...])
    hi, lo = _split2(xf)
    if packed:
        bits = pltpu.bitcast(hi.astype(F32), jnp.uint32)
        for c, ref in enumerate(x_refs):
            a = bits[:, (2 * c) * SC_ROW_WORDS:(2 * c + 1) * SC_ROW_WORDS]
            b = bits[:, (2 * c + 1) * SC_ROW_WORDS:(2 * c + 2) * SC_ROW_WORDS]
            ref[...] = jnp.bitwise_or(a, jnp.right_shift(b, jnp.uint32(16)))
    else:
        x_refs[0][...] = hi
    logits = _dot(hi, wrh_ref[...]) + _dot(lo, wrh_ref[...]) + _dot(hi, wrl_ref[...]) + br_ref[...]
    gates_ref[...] = _route(logits)


def _post(oa, ob, oc, h, lw, tm, n_chunks=0):
    t = h.shape[0]
    row = lambda w: pl.BlockSpec((tm, w), lambda i: (i, 0))
    full = lambda a: pl.BlockSpec(a.shape, lambda i: (0,) * a.ndim)
    ws = (lw['g_mix'], lw['w_out'], lw['norm_ffn'], lw['w_r_hi'], lw['w_r_lo'], lw['b_r'])
    if n_chunks:
        x_specs = [row(SC_ROW_WORDS)] * n_chunks
        x_shapes = [jax.ShapeDtypeStruct((t, SC_ROW_WORDS), jnp.uint32)] * n_chunks
    else:
        x_specs, x_shapes = [row(D_MODEL)], [jax.ShapeDtypeStruct((t, D_MODEL), BF16)]
    return pl.pallas_call(
        functools.partial(_post_kernel, packed=bool(n_chunks)), grid=(t // tm,),
        in_specs=[row(GROUP_W)] * 3 + [row(D_MODEL)] + [full(a) for a in ws],
        out_specs=[row(D_MODEL), row(LANES)] + x_specs,
        out_shape=[jax.ShapeDtypeStruct((t, D_MODEL), F32), jax.ShapeDtypeStruct((t, LANES), F32)] + x_shapes,
        compiler_params=_cparams("parallel"), name="post")(oa, ob, oc, h, *ws)


def _moe_kernel(xn_ref, gates_ref, wg_ref, wu_ref, wd_ref, h1_ref, p_ref, wple_ref, nple_ref, wpg_ref,
                out_ref, acc_ref):
    step = pl.program_id(1)

    @pl.when(step == 0)
    def _():
        acc_ref[...] = jnp.zeros_like(acc_ref)

    x = xn_ref[...]
    gates = gates_ref[...]
    acts = []
    for k in range(EXPERTS_PER_STEP):
        e = step * EXPERTS_PER_STEP + k
        g = _dot(x, wg_ref[0, k].astype(BF16))
        u = _dot(x, wu_ref[0, k].astype(BF16))
        gate = jnp.sum(jnp.where(_lane_iota() == e, gates, 0.0), axis=-1, keepdims=True)
        acts.append(((g * _sigmoid(g)) * u * gate).astype(BF16))
    wd = wd_ref[0]
    acc_ref[...] += _dot(jnp.concatenate(acts, axis=1),
                         wd.reshape(wd.shape[0] * wd.shape[1], wd.shape[2]).astype(BF16))

    @pl.when(step == pl.num_programs(1) - 1)
    def _():
        h2 = h1_ref[...] + acc_ref[...]
        gate_ple = _sigmoid(_dot(_rms(h2, nple_ref[...]).astype(BF16), wpg_ref[...]))
        out_ref[...] = h2 + _dot(p_ref[0].astype(BF16), wple_ref[...]) * gate_ple


def _moe(xn, gates, h1, p_all, layer, lw, tm):
    t = xn.shape[0]
    row = lambda w: pl.BlockSpec((tm, w), lambda i, e: (i, 0))
    once = pl.Buffered(1)
    full = lambda a: pl.BlockSpec(a.shape, lambda i, e: (0,) * a.ndim, pipeline_mode=once)
    exp = lambda a: pl.BlockSpec((1, EXPERTS_PER_STEP) + a.shape[2:], lambda i, e: (layer, e, 0, 0))
    return pl.pallas_call(
        _moe_kernel, grid=(t // tm, N_EXPERTS // EXPERTS_PER_STEP),
        in_specs=[row(D_MODEL), row(LANES), exp(lw['w_eg']), exp(lw['w_eu']), exp(lw['w_ed']),
                  pl.BlockSpec((tm, D_MODEL), lambda i, e: (i, 0), pipeline_mode=once),
                  pl.BlockSpec((1, tm, p_all.shape[2]), lambda i, e: (layer, i, 0), pipeline_mode=once),
                  full(lw['w_ple']), full(lw['norm_ple']), full(lw['w_pg'])],
        out_specs=row(D_MODEL), out_shape=jax.ShapeDtypeStruct((t, D_MODEL), F32),
        scratch_shapes=[pltpu.VMEM((tm, D_MODEL), F32)],
        compiler_params=pltpu.CompilerParams(dimension_semantics=("parallel", "arbitrary"),
                                             vmem_limit_bytes=MOE_VMEM_LIMIT), name="moe")(
            xn, gates, lw['w_eg'], lw['w_eu'], lw['w_ed'], h1, p_all, lw['w_ple'], lw['norm_ple'], lw['w_pg'])


SORT_TILE = 512
SC_WINDOW = 128
SC_ROW_WORDS = 256
N_CHUNKS = D_MODEL // SC_ROW_WORDS
N_XCHUNKS = N_CHUNKS // 2


def _plan_kernel(gates_ref, dest_ref, total_ref, *, tb):
    lane = _lane_iota()
    is_g = (lane >= N_EXPERTS) & (lane < N_EXPERTS + N_GROUPS)
    total = jnp.sum(jnp.where(is_g, gates_ref[...], 0.0), axis=0, keepdims=True)
    padded = jnp.floor((total + (SORT_TILE - 1)) * (1.0 / SORT_TILE)) * SORT_TILE
    base = jnp.zeros((1, LANES), F32)
    for g in range(1, N_GROUPS):
        before = jnp.sum(jnp.where(lane < N_EXPERTS + g, padded, 0.0), axis=-1, keepdims=True)
        base = jnp.where(lane == N_EXPERTS + g, before, base)
    r = lax.broadcasted_iota(jnp.int32, (tb, tb), 0)
    c = lax.broadcasted_iota(jnp.int32, (tb, tb), 1)
    ltri = jnp.where(c < r, 1.0, 0.0).astype(BF16)
    carry = jnp.zeros((1, LANES), F32)
    for blk in range(gates_ref.shape[0] // tb):
        rows = slice(blk * tb, (blk + 1) * tb)
        member = jnp.where(is_g, gates_ref[rows, :], 0.0)
        rank = _dot(ltri, member.astype(BF16)) + carry
        carry = carry + jnp.sum(member, axis=0, keepdims=True)
        dest = jnp.sum(member * (base + rank), axis=-1, keepdims=True)
        dest_ref[rows, :] = jnp.broadcast_to(dest, (tb, LANES))
    total_ref[...] = jnp.broadcast_to(total, total_ref.shape)


def _plan(gates):
    t = gates.shape[0]
    return pl.pallas_call(
        functools.partial(_plan_kernel, tb=SORT_TILE),
        out_shape=[jax.ShapeDtypeStruct((t, LANES), F32), jax.ShapeDtypeStruct((SUBLANES, LANES), F32)],
        compiler_params=pltpu.CompilerParams(vmem_limit_bytes=VMEM_LIMIT), name="moe_plan")(gates)


def _sc_scatter_rows(x, idx, n_out):
    n, d = x.shape
    mesh = plsc.VectorSubcoreMesh(core_axis_name="core", subcore_axis_name="subcore")

    @functools.partial(pl.kernel, out_type=jax.ShapeDtypeStruct((n_out, d), x.dtype), mesh=mesh, scratch_types=[])
    def scatter(x_hbm, i_hbm, o_hbm):
        def body(x_vmem, i_vmem):
            pltpu.sync_copy(x_vmem, o_hbm.at[i_vmem.at[0]])

        pltpu.emit_pipeline(
            body, grid=(n // SC_WINDOW,),
            in_specs=[pl.BlockSpec((SC_WINDOW, d), index_map=lambda i: (i, 0)),
                      pl.BlockSpec((1, SC_WINDOW), index_map=lambda i: (0, i))],
            out_specs=[], core_axis_name=('core', 'subcore'), dimension_semantics=(pltpu.PARALLEL,))(x_hbm, i_hbm)

    return scatter(x, idx.reshape(1, n))


def _sc_gather_rows(x, idx):
    n = idx.shape[0]
    d = x.shape[1]
    mesh = plsc.VectorSubcoreMesh(core_axis_name="core", subcore_axis_name="subcore")

    @functools.partial(pl.kernel, out_type=jax.ShapeDtypeStruct((n, d), x.dtype), mesh=mesh)
    def gather(x_hbm, i_hbm, o_hbm):
        def body(i_vmem, o_vmem):
            pltpu.sync_copy(x_hbm.at[i_vmem.at[0]], o_vmem)

        pltpu.emit_pipeline(
            body, grid=(n // SC_WINDOW,),
            in_specs=[pl.BlockSpec((1, SC_WINDOW), index_map=lambda i: (0, i))],
            out_specs=[pl.BlockSpec((SC_WINDOW, d), index_map=lambda i: (i, 0))],
            core_axis_name=('core', 'subcore'), dimension_semantics=(pltpu.PARALLEL,))(i_hbm, o_hbm)

    return gather(x, idx.reshape(1, n))


def _experts_sorted_kernel(tg_ref, tv_ref, *refs):
    xs_refs, (gs_ref, wgf_ref, wuf_ref, wdf_ref) = refs[:N_XCHUNKS], refs[N_XCHUNKS:N_XCHUNKS + 4]
    ys_refs = refs[N_XCHUNKS + 4:N_XCHUNKS + 4 + N_CHUNKS]
    wg_ref, wu_ref, wd_ref = refs[N_XCHUNKS + 4 + N_CHUNKS:]
    j = pl.program_id(0)
    g = tg_ref[j]
    valid = tv_ref[j] > 0

    @pl.when(valid & ((j == 0) | (g != tg_ref[jnp.maximum(j - 1, 0)])))
    def _():
        wg_ref[...] = wgf_ref[0].astype(BF16)
        wu_ref[...] = wuf_ref[0].astype(BF16)
        wd = wdf_ref[0]
        wd_ref[...] = wd.reshape(wd.shape[0] * wd.shape[1], wd.shape[2]).astype(BF16)

    @pl.when(valid)
    def _():
        halves = []
        for r in xs_refs:
            w = r[...]
            halves.append(pltpu.bitcast(jnp.bitwise_and(w, jnp.uint32(0xFFFF0000)), F32))
            halves.append(pltpu.bitcast(jnp.left_shift(w, jnp.uint32(16)), F32))
        x = jnp.concatenate(halves, axis=1).astype(BF16)
        gates = gs_ref[...]
        acts = []
        for k in range(EXPERTS_PER_GROUP):
            hg = _dot(x, wg_ref[k])
            gate = jnp.sum(jnp.where(_lane_iota() == g * EXPERTS_PER_GROUP + k, gates, 0.0),
                           axis=-1, keepdims=True)
            acts.append(((hg * _sigmoid(hg)) * _dot(x, wu_ref[k]) * gate).astype(BF16))
        y = _dot(jnp.concatenate(acts, axis=1), wd_ref[...])
        for c, ref in enumerate(ys_refs):
            ref[...] = y[:, c * SC_ROW_WORDS:(c + 1) * SC_ROW_WORDS]


def _experts_sorted(xs_chunks, gs, tile_group, tile_valid, layer, lw):
    n_rows = gs.shape[0]
    row = lambda w: pl.BlockSpec((SORT_TILE, w), lambda j, tg, tv: (j, 0))
    grp = lambda a: pl.BlockSpec((1, EXPERTS_PER_GROUP) + a.shape[2:], lambda j, tg, tv: (layer, tg[j], 0, 0))
    d_hidden = lw['w_eg'].shape[3]
    return pl.pallas_call(
        _experts_sorted_kernel,
        grid_spec=pltpu.PrefetchScalarGridSpec(
            num_scalar_prefetch=2, grid=(n_rows // SORT_TILE,),
            in_specs=[row(SC_ROW_WORDS)] * N_XCHUNKS + [row(LANES), grp(lw['w_eg']), grp(lw['w_eu']), grp(lw['w_ed'])],
            out_specs=[row(SC_ROW_WORDS)] * N_CHUNKS,
            scratch_shapes=[pltpu.VMEM((EXPERTS_PER_GROUP, D_MODEL, d_hidden), BF16),
                            pltpu.VMEM((EXPERTS_PER_GROUP, D_MODEL, d_hidden), BF16),
                            pltpu.VMEM((EXPERTS_PER_GROUP * d_hidden, D_MODEL), BF16)]),
        out_shape=[jax.ShapeDtypeStruct((n_rows, SC_ROW_WORDS), F32)] * N_CHUNKS,
        compiler_params=pltpu.CompilerParams(dimension_semantics=("arbitrary",),
                                             vmem_limit_bytes=MOE_VMEM_LIMIT), name="moe_sorted")(
            tile_group, tile_valid, *xs_chunks, gs, lw['w_eg'], lw['w_eu'], lw['w_ed'])


def _ple_kernel(h1_ref, *refs):
    y_refs, (p_ref, wple_ref, nple_ref, wpg_ref, out_ref) = refs[:N_CHUNKS], refs[N_CHUNKS:]
    h2 = h1_ref[...] + jnp.concatenate([r[...] for r in y_refs], axis=1)
    gate_ple = _sigmoid(_dot(_rms(h2, nple_ref[...]).astype(BF16), wpg_ref[...]))
    out_ref[...] = h2 + _dot(p_ref[0].astype(BF16), wple_ref[...]) * gate_ple


def _ple(h1, y_chunks, p_all, layer, lw, tm):
    t = h1.shape[0]
    row = lambda w: pl.BlockSpec((tm, w), lambda i: (i, 0))
    full = lambda a: pl.BlockSpec(a.shape, lambda i: (0,) * a.ndim)
    return pl.pallas_call(
        _ple_kernel, grid=(t // tm,),
        in_specs=[row(D_MODEL)] + [row(SC_ROW_WORDS)] * N_CHUNKS
        + [pl.BlockSpec((1, tm, p_all.shape[2]), lambda i: (layer, i, 0)),
           full(lw['w_ple']), full(lw['norm_ple']), full(lw['w_pg'])],
        out_specs=row(D_MODEL), out_shape=jax.ShapeDtypeStruct((t, D_MODEL), F32),
        compiler_params=_cparams("parallel"), name="ple")(
            h1, *y_chunks, p_all, lw['w_ple'], lw['norm_ple'], lw['w_pg'])


def _moe_dispatch(x_chunks, gates):
    t = gates.shape[0]
    n_rows = t + N_GROUPS * SORT_TILE
    dest_rep, total = _plan(gates)
    dest = dest_rep[:, 0].astype(jnp.int32)
    padded = (total[0, N_EXPERTS:N_EXPERTS + N_GROUPS].astype(jnp.int32) + (SORT_TILE - 1)) // SORT_TILE * SORT_TILE
    ends = jnp.cumsum(padded)
    tile_start = jnp.arange(n_rows // SORT_TILE, dtype=jnp.int32) * SORT_TILE
    tile_group = jnp.minimum(jnp.sum(tile_start[:, None] >= ends[None, :], axis=1), N_GROUPS - 1).astype(jnp.int32)
    tile_valid = (tile_start < ends[-1]).astype(jnp.int32)
    xs_chunks = [_sc_scatter_rows(xc, dest, n_rows) for xc in x_chunks]
    gs = _sc_scatter_rows(gates, dest, n_rows)
    return xs_chunks, gs, tile_group, tile_valid, dest


def _moe_combine(dispatched, h1, p_all, layer, lw):
    xs_chunks, gs, tile_group, tile_valid, dest = dispatched
    ys_chunks = _experts_sorted(xs_chunks, gs, tile_group, tile_valid, layer, lw)
    y_chunks = [_sc_gather_rows(yc, dest) for yc in ys_chunks]
    return _ple(h1, y_chunks, p_all, layer, lw, _token_tile(h1.shape[0], 1024))


def _regroup_out_rows(wt):
    z = jnp.zeros((HEAD_DIM, wt.shape[1]), wt.dtype)
    return jnp.concatenate([wt[:W_A], wt[W_A:W_A + W_B], z, wt[W_A + W_B:], z], axis=0)


def _regroup_rows(w):
    z = jnp.zeros((HEAD_DIM, w.shape[1]), w.dtype)
    return jnp.stack([w[:W_A], jnp.concatenate([w[W_A:W_A + W_B], z], axis=0),
                      jnp.concatenate([w[W_A + W_B:], z], axis=0)])


def _pad_lanes(a, n=LANES):
    return jnp.pad(a, [(0, 0)] * (a.ndim - 1) + [(0, n - a.shape[-1])])


def _toeplitz_bias(rel_bias, off, tq, win):
    length = tq + win
    pad = length + abs(off)
    ext = jnp.flip(jnp.pad(rel_bias, ((0, 0), (pad, pad)), mode='edge'), axis=1)
    s1 = ext.shape[1] - 1 - (off + REL_CLIP + pad)
    v = jnp.concatenate([ext[:, s1:s1 + win], ext[:, s1 - tq:s1]], axis=1)
    flat = jnp.tile(v, (1, tq))[:, :tq * (length - 1)]
    return flat.reshape(-1, tq, length - 1)[:, :, :win]


def _block_toeplitz_bias(rel_bias, off, tq, win):
    nq, nk = tq // LANES, win // LANES
    blocks = {d: _toeplitz_bias(rel_bias, off - d * LANES, LANES, LANES) for d in range(-(nq - 1), nk)}
    return jnp.concatenate(
        [jnp.concatenate([blocks[b - a] for b in range(nk)], axis=2) for a in range(nq)], axis=1)


def _rel_tables(rel_bias, t_new, rows_b):
    tq, win = ATT_TILE, ATT_TILE + BAND
    q = jnp.arange(tq)[:, None]
    tabs = []
    for var in range(BAND // tq + 1):
        rel_k = jnp.arange(win)[None, :] - var * tq
        kch = jnp.floor_divide(rel_k, CHUNK)
        qch = q // CHUNK
        valid = (kch <= qch) & (kch >= qch - BAND_CHUNKS)
        tabs.append(jnp.where(valid[None], _block_toeplitz_bias(rel_bias, var * tq, tq, win), NEG_INF))
    tab_prompt = jnp.stack(tabs).astype(F32)
    tab_c = _toeplitz_bias(rel_bias, rows_b, t_new, rows_b).astype(F32)
    tab_n = _toeplitz_bias(rel_bias, 0, t_new, t_new).astype(F32)
    return tab_prompt, tab_c, tab_n


def _layer_weights(i, norm_mix, w_in, b_f, g_qa, g_ka, g_qb, g_kb, g_mix_out, w_out, norm_ffn,
                   w_rg, b_rg, w_re, b_re, w_eg, w_eu, w_ed, w_ple, norm_ple, w_pg):
    d_mix = W_A + W_B + W_C
    wt = jnp.transpose(w_in, (2, 0, 1))[:, i, :]
    w_f = wt[3 * d_mix:]
    w_big = jnp.concatenate(
        [_regroup_out_rows(wt[j * d_mix:(j + 1) * d_mix]) for j in range(3)]
        + [jnp.pad(w_f, ((0, LANES - w_f.shape[0]), (0, 0)))], axis=0).astype(BF16)
    tile6 = lambda g: jnp.tile(g, GROUP_W // HEAD_DIM)
    w_r = _pad_lanes(jnp.concatenate([w_re[i], w_rg[i]], axis=1))
    w_r_hi = w_r.astype(BF16)
    gm = g_mix_out[i]
    zpad = jnp.zeros((HEAD_DIM,), F32)
    g_mix = jnp.stack([gm[:W_A], jnp.concatenate([gm[W_A:W_A + W_B], zpad]),
                       jnp.concatenate([gm[W_A + W_B:], zpad])])[:, None, :]
    return dict(
        norm_mix=norm_mix[i][None], w_in=w_big,
        gq=jnp.concatenate([tile6(g_qa[i]), tile6(g_qb[i])])[None],
        gk=jnp.concatenate([tile6(g_ka[i]), tile6(g_kb[i])])[None],
        b_f=_pad_lanes(b_f[i])[None],
        g_mix=g_mix, w_out=_regroup_rows(w_out[i]).astype(BF16), norm_ffn=norm_ffn[i][None],
        w_r_hi=w_r_hi, w_r_lo=(w_r - w_r_hi.astype(F32)).astype(BF16),
        b_r=_pad_lanes(jnp.concatenate([b_re[i], b_rg[i]]))[None],
        w_eg=w_eg, w_eu=w_eu, w_ed=w_ed,
        w_ple=w_ple[i].astype(BF16), norm_ple=norm_ple[i][None], w_pg=w_pg[i].astype(BF16))


def _token_tile(t, pref):
    return pref if t % pref == 0 else t


def _feature_major(cache):
    d, n, p, h, e = cache.shape
    return jnp.transpose(cache, (0, 1, 3, 4, 2)).reshape(d, n, h * e, p)


def _position_major(rows, heads):
    d, n, _, p = rows.shape
    return jnp.transpose(rows.reshape(d, n, heads, HEAD_DIM, p), (0, 1, 4, 2, 3))


def kernel(x_prompt, x_sample, p_prompt, p_sample, cache_a_k, cache_a_v, cache_a_logf, cache_b_k, cache_b_v, cache_c_k, cache_c_v, norm_mix, w_in, b_f, g_qa, g_ka, g_qb, g_kb, rel_bias, g_mix_out, w_out, norm_ffn, w_router_group, b_router_group, w_router_expert, b_router_expert, w_exp_gate, w_exp_up, w_exp_down, w_ple, norm_ple, w_ple_gate):
    nb, s, d = x_prompt.shape
    ns, t_new, _ = x_sample.shape
    depth = w_in.shape[0]
    past = cache_a_k.shape[2]
    rows_b = cache_b_k.shape[2]
    assert d == D_MODEL and s % PROJ_TILE == 0 and s >= BAND + ATT_TILE and past % ATT_TILE == 0
    tp, ts = nb * s, ns * t_new

    hp = x_prompt.reshape(tp, d)
    hs = x_sample.reshape(ts, d)
    pp = p_prompt.reshape(depth, tp, -1)
    ps = p_sample.reshape(depth, ts, -1)
    caches = tuple(_feature_major(c) for c in (cache_a_k, cache_a_v, cache_b_k, cache_b_v, cache_c_k, cache_c_v))
    lf_rows = jnp.transpose(cache_a_logf, (0, 3, 1, 2))
    rsum = _suffix_sum(lf_rows.reshape(depth * H_A * ns, past)).reshape(depth, H_A, ns, past)
    rsum = jnp.transpose(rsum, (0, 2, 1, 3))

    rows_p = None
    new_s = [[] for _ in range(7)]
    for i in range(depth):
        lw = _layer_weights(i, norm_mix, w_in, b_f, g_qa, g_ka, g_qb, g_kb, g_mix_out, w_out, norm_ffn,
                            w_router_group, b_router_group, w_router_expert, b_router_expert,
                            w_exp_gate, w_exp_up, w_exp_down, w_ple, norm_ple, w_ple_gate)
        tab_p, tab_c, tab_n = _rel_tables(rel_bias[i] * LOG2E, t_new, rows_b)

        pr = _inproj_prompt(hp, lw, i, depth, nb, s, rows_p)
        qat, ka, vat, qb, kb, vb, qc, kc, vc, logf = pr[:10]
        rows_p = pr[10:]
        kaug, qaug = _forget_aug(logf, nb, s)
        oa = _attn_a(qat, ka, vat, kaug, qaug, nb, s)
        ob = _prompt_attention(
            _attn_b_kernel, "attn_b", qb, kb, vb, (tab_p,),
            [pl.BlockSpec(tab_p.shape, lambda b, j: (0, 0, 0, 0))], [], nb, s, ATT_TILE)
        oc = _prompt_attention(
            _attn_c_kernel, "attn_c", qc, kc, vc, (), [],
            [pltpu.VMEM((H_C, C_QTILE, PAIR_W), BF16), pltpu.VMEM((H_C, C_QTILE, LANES), F32),
             pltpu.VMEM((H_C, C_QTILE, PAIR_W), F32)], nb, s, C_QTILE)
        h1, gates, *x_chunks = _post(oa, ob, oc, hp, lw, _token_tile(tp, 512), N_XCHUNKS)
        dispatched = _moe_dispatch(x_chunks, gates)

        sr = _inproj_sample(hs, lw)
        oa, ob, oc = _sample_attention(sr, caches, i, rsum, tab_c, tab_n, ns, t_new)
        h1s, gates, xn = _post(oa, ob, oc, hs, lw, _token_tile(ts, 512))
        hs = _moe(xn, gates, h1s, ps, i, lw, _token_tile(ts, 1024))

        hp = _moe_combine(dispatched, h1, pp, i, lw)
        akf, avf, bkf, bvf, ckf, cvf, logf = sr[9:]
        rows = (akf.reshape(ns, t_new, H_A, HEAD_DIM), avf.reshape(ns, t_new, H_A, HEAD_DIM),
                logf[:, :H_A].reshape(ns, t_new, H_A),
                bkf.reshape(ns, t_new, H_B, HEAD_DIM), bvf.reshape(ns, t_new, H_B, HEAD_DIM),
                ckf.reshape(ns, t_new, H_C, HEAD_DIM), cvf.reshape(ns, t_new, H_C, HEAD_DIM))
        for j in range(7):
            new_s[j].append(rows[j])

    akt, avt, bkt, bvt, ckt, cvt, lft = rows_p
    outs_p = (_position_major(akt, H_A), _position_major(avt, H_A),
              jnp.transpose(lft.reshape(depth, H_A, nb, s), (0, 2, 3, 1)),
              _position_major(bkt, H_B), _position_major(bvt, H_B),
              _position_major(ckt, H_C), _position_major(cvt, H_C))
    outs_s = [jnp.stack(r, axis=0) for r in new_s]
    return (hp.reshape(nb, s, d), hs.reshape(ns, t_new, d), *outs_p, *outs_s)
```

```python
import functools
import math

import jax
import jax.numpy as jnp
from jax import lax
from jax.experimental import pallas as pl
from jax.experimental.pallas import tpu as pltpu
from jax.experimental.pallas import tpu_sc as plsc

F32 = jnp.float32
BF16 = jnp.bfloat16

D_MODEL = 1024
HEAD_DIM = 64
H_A, H_B, H_C = 6, 5, 5
W_A, W_B, W_C = H_A * HEAD_DIM, H_B * HEAD_DIM, H_C * HEAD_DIM
GROUP_W = 384
PAIR_W = 2 * HEAD_DIM
N_PAIRS = GROUP_W // PAIR_W
CHUNK = 64
BAND_CHUNKS = 8
BAND = BAND_CHUNKS * CHUNK
REL_CLIP = 128
N_GROUPS = 4
EXPERTS_PER_GROUP = 4
N_EXPERTS = N_GROUPS * EXPERTS_PER_GROUP
EPS = 1e-6
ATTN_SCALE = HEAD_DIM ** -0.5
LOG2E = math.log2(math.e)
Q_SCALE = ATTN_SCALE * LOG2E
NEG_INF = -1e30
LANES = 128
SUBLANES = 8
VMEM_LIMIT = 48 * 1024 * 1024
EXPERTS_PER_STEP = 4
MOE_VMEM_LIMIT = 58 * 1024 * 1024

ATT_TILE = 256
A_QTILE = 512
C_QTILE = 256
PROJ_TILE = 1024
PROJ_VMEM_LIMIT = 56 * 1024 * 1024
AUG_SLOTS = 8

_NT = (((1,), (1,)), ((), ()))


def _cparams(*sem):
    return pltpu.CompilerParams(dimension_semantics=sem, vmem_limit_bytes=VMEM_LIMIT)


def _dot(a, b):
    return jnp.dot(a, b, preferred_element_type=F32)


def _dot_nt(a, b):
    return lax.dot_general(a, b, _NT, preferred_element_type=F32)


def _split3(x):
    hi = x.astype(BF16)
    r1 = x - hi.astype(F32)
    mid = r1.astype(BF16)
    lo = (r1 - mid.astype(F32)).astype(BF16)
    return hi, mid, lo


def _split2(x):
    hi = x.astype(BF16)
    lo = (x - hi.astype(F32)).astype(BF16)
    return hi, lo


def _lane_iota(n=LANES):
    return lax.broadcasted_iota(jnp.int32, (1, n), 1)


def _half_mask(half):
    lane = _lane_iota()
    return (lane < HEAD_DIM) if half == 0 else (lane >= HEAD_DIM)


def _pair_slice(h):
    return slice((h // 2) * PAIR_W, (h // 2 + 1) * PAIR_W)


def _softplus2(z2):
    return jnp.maximum(z2, 0.0) + jnp.log2(1.0 + jnp.exp2(-jnp.abs(z2)))


def _sigmoid(z):
    return 1.0 / (1.0 + jnp.exp(-z))


def _rms(x, g):
    return x * lax.rsqrt(jnp.mean(x * x, axis=-1, keepdims=True) + EPS) * g


def _pair_rms(z, gain):
    first = _half_mask(0)
    outs = []
    for j in range(N_PAIRS):
        blk = z[:, j * PAIR_W:(j + 1) * PAIR_W]
        sq = blk * blk
        lo = jnp.sum(jnp.where(first, sq, 0.0), axis=-1, keepdims=True)
        hi = jnp.sum(jnp.where(first, 0.0, sq), axis=-1, keepdims=True)
        ms = jnp.where(first, lo, hi) * (1.0 / HEAD_DIM)
        outs.append(blk * lax.rsqrt(ms + EPS) * gain[:, j * PAIR_W:(j + 1) * PAIR_W])
    return jnp.concatenate(outs, axis=-1)


def _projector(x_ref, gn_ref, w_ref):
    xb = _rms(x_ref[...], gn_ref[...]).astype(BF16)
    return lambda g, width=GROUP_W: _dot_nt(xb, w_ref[g * GROUP_W:g * GROUP_W + width, :])


def _log_forget(zf, bf_ref):
    zf = zf + bf_ref[...]
    return jnp.minimum(zf, 0.0) - jnp.log(1.0 + jnp.exp(-jnp.abs(zf)))


def _inproj_sample_kernel(x_ref, gn_ref, w_ref, gq_ref, gk_ref, bf_ref,
                          qa_ref, qb_ref, qc_ref, ka_ref, kb_ref, kc_ref, va_ref, vb_ref, vc_ref,
                          akf_ref, avf_ref, bkf_ref, bvf_ref, ckf_ref, cvf_ref, logf_ref):
    proj = _projector(x_ref, gn_ref, w_ref)
    qa_ref[...] = (_pair_rms(proj(0), gq_ref[:, :GROUP_W]) * Q_SCALE).astype(BF16)
    qb_ref[...] = (_pair_rms(proj(1), gq_ref[:, GROUP_W:]) * Q_SCALE).astype(BF16)
    qc_ref[...] = (proj(2) * Q_SCALE).astype(BF16)
    for g, norm, bf_ref_, f_ref, width in (
            (3, gk_ref[:, :GROUP_W], ka_ref, akf_ref, W_A), (4, gk_ref[:, GROUP_W:], kb_ref, bkf_ref, W_B),
            (5, None, kc_ref, ckf_ref, W_C), (6, None, va_ref, avf_ref, W_A),
            (7, None, vb_ref, bvf_ref, W_B), (8, None, vc_ref, cvf_ref, W_C)):
        z = proj(g)
        if norm is not None:
            z = _pair_rms(z, norm)
        f_ref[...] = z[:, :width]
        bf_ref_[...] = z.astype(BF16)
    logf_ref[...] = _log_forget(proj(9, LANES), bf_ref)


def _inproj_sample(x, lw):
    t = x.shape[0]
    row = lambda w: pl.BlockSpec((t, w), lambda i: (0, 0))
    full = lambda a: pl.BlockSpec(a.shape, lambda i: (0,) * a.ndim)
    widths = [GROUP_W] * 9 + [W_A, W_A, W_B, W_B, W_C, W_C, LANES]
    out_shape = [jax.ShapeDtypeStruct((t, w), BF16 if j < 9 else F32) for j, w in enumerate(widths)]
    ins = (x, lw['norm_mix'], lw['w_in'], lw['gq'], lw['gk'], lw['b_f'])
    return pl.pallas_call(
        _inproj_sample_kernel, grid=(1,),
        in_specs=[row(D_MODEL)] + [full(a) for a in ins[1:]],
        out_specs=[row(w) for w in widths], out_shape=out_shape,
        compiler_params=_cparams("arbitrary"), name="inproj_sample")(*ins)


N_PROMPT_ROWS = 7


def _inproj_prompt_kernel(*refs, n_alias):
    x_ref, gn_ref, w_ref, gq_ref, gk_ref, bf_ref = refs[:6]
    (qat_ref, ka_ref, vat_ref, qb_ref, kb_ref, vb_ref, qc_ref, kc_ref, vc_ref, logf_ref,
     akt_ref, avt_ref, bkt_ref, bvt_ref, ckt_ref, cvt_ref, lft_ref) = refs[6 + n_alias:]
    proj = _projector(x_ref, gn_ref, w_ref)
    qat_ref[...] = jnp.transpose(_pair_rms(proj(0), gq_ref[:, :GROUP_W]) * Q_SCALE).astype(BF16)
    qb_ref[...] = (_pair_rms(proj(1), gq_ref[:, GROUP_W:]) * Q_SCALE).astype(BF16)
    qc_ref[...] = (proj(2) * Q_SCALE).astype(BF16)
    ka = _pair_rms(proj(3), gk_ref[:, :GROUP_W])
    ka_ref[...] = ka.astype(BF16)
    akt_ref[0, 0] = jnp.transpose(ka)
    kb = _pair_rms(proj(4), gk_ref[:, GROUP_W:])
    kb_ref[...] = kb.astype(BF16)
    kc = proj(5)
    kc_ref[...] = kc.astype(BF16)
    ckt_ref[0, 0] = jnp.transpose(kc)[:W_C]
    vat = jnp.transpose(proj(6))
    avt_ref[0, 0] = vat
    for j in range(vat_ref.shape[1]):
        vat_ref[0, j] = vat[:, j * ATT_TILE:(j + 1) * ATT_TILE].astype(BF16)
    vb = proj(7)
    vb_ref[...] = vb.astype(BF16)
    vc = proj(8)
    vc_ref[...] = vc.astype(BF16)
    cvt_ref[0, 0] = jnp.transpose(vc)[:W_C]

    @pl.when(pl.program_id(1) == pl.num_programs(1) - 1)
    def _():
        keep = kb.shape[0] - BAND
        bkt_ref[0, 0] = jnp.transpose(kb)[:W_B, keep:]
        bvt_ref[0, 0] = jnp.transpose(vb)[:W_B, keep:]

    lf = _log_forget(proj(9, LANES), bf_ref)
    logf_ref[...] = lf
    lft_ref[0] = jnp.transpose(lf)[:H_A]


def _inproj_prompt(x, lw, layer, depth, nb, s, prev_rows):
    tm = PROJ_TILE
    nt = s // tm
    t = nb * s
    per = tm // ATT_TILE
    row = lambda w: pl.BlockSpec((tm, w), lambda b, i: (b * nt + i, 0))
    full = lambda a: pl.BlockSpec(a.shape, lambda b, i: (0,) * a.ndim, pipeline_mode=pl.Buffered(1))
    feat = lambda w: pl.BlockSpec((1, 1, w, tm), lambda b, i: (layer, b, 0, i))
    band = lambda w: pl.BlockSpec((1, 1, w, BAND), lambda b, i: (layer, b, 0, 0))
    tok = jax.ShapeDtypeStruct((t, GROUP_W), BF16)
    out_shape = [
        jax.ShapeDtypeStruct((GROUP_W, t), BF16), tok,
        jax.ShapeDtypeStruct((nb, s // ATT_TILE, GROUP_W, ATT_TILE), BF16),
        tok, tok, tok, tok, tok, tok, jax.ShapeDtypeStruct((t, LANES), F32),
        jax.ShapeDtypeStruct((depth, nb, W_A, s), F32), jax.ShapeDtypeStruct((depth, nb, W_A, s), F32),
        jax.ShapeDtypeStruct((depth, nb, W_B, BAND), F32), jax.ShapeDtypeStruct((depth, nb, W_B, BAND), F32),
        jax.ShapeDtypeStruct((depth, nb, W_C, s), F32), jax.ShapeDtypeStruct((depth, nb, W_C, s), F32),
        jax.ShapeDtypeStruct((depth, H_A, t), F32)]
    out_specs = [
        pl.BlockSpec((GROUP_W, tm), lambda b, i: (0, b * nt + i)), row(GROUP_W),
        pl.BlockSpec((1, per, GROUP_W, ATT_TILE), lambda b, i: (b, i, 0, 0)),
        row(GROUP_W), row(GROUP_W), row(GROUP_W), row(GROUP_W), row(GROUP_W), row(GROUP_W), row(LANES),
        feat(W_A), feat(W_A), band(W_B), band(W_B), feat(W_C), feat(W_C),
        pl.BlockSpec((1, H_A, tm), lambda b, i: (layer, 0, b * nt + i))]
    ins = [x, lw['norm_mix'], lw['w_in'], lw['gq'], lw['gk'], lw['b_f']]
    in_specs = [row(D_MODEL)] + [full(a) for a in ins[1:]]
    aliases = {}
    n_alias = 0
    if prev_rows is not None:
        n_alias = N_PROMPT_ROWS
        first_row_out = len(out_shape) - N_PROMPT_ROWS
        for j, a in enumerate(prev_rows):
            aliases[len(ins)] = first_row_out + j
            ins.append(a)
            in_specs.append(pl.BlockSpec(memory_space=pl.ANY))
    return pl.pallas_call(
        functools.partial(_inproj_prompt_kernel, n_alias=n_alias), grid=(nb, nt),
        in_specs=in_specs, out_specs=out_specs, out_shape=out_shape,
        input_output_aliases=aliases,
        compiler_params=pltpu.CompilerParams(dimension_semantics=("parallel", "arbitrary"),
                                             vmem_limit_bytes=PROJ_VMEM_LIMIT),
        name="inproj_prompt")(*ins)


def _forget_aug_kernel(logf_ref, kaug_ref, qaug_ref, *, nblk):
    tb = ATT_TILE
    r = lax.broadcasted_iota(jnp.int32, (tb, tb), 0)
    c = lax.broadcasted_iota(jnp.int32, (tb, tb), 1)
    tri = jnp.where(c <= r, 1.0, 0.0).astype(BF16)
    src = lax.broadcasted_iota(jnp.int32, (LANES, LANES), 0)
    dst = lax.broadcasted_iota(jnp.int32, (LANES, LANES), 1)

    def place(slot):
        return jnp.where((dst == AUG_SLOTS * src + slot) & (src < H_A), 1.0, 0.0).astype(BF16)

    lane = _lane_iota()
    slot = jnp.bitwise_and(lane, AUG_SLOTS - 1)
    used = lane < AUG_SLOTS * H_A
    key_const = jnp.where(used & (slot >= 3) & (slot < 6), 1.0, 0.0)
    qry_const = jnp.where(used & (slot < 3), -1.0, 0.0)
    carry = jnp.zeros((1, LANES), F32)
    for blk in range(nblk):
        rows = slice(blk * tb, (blk + 1) * tb)
        hi, mid, lo = _split3(logf_ref[rows, :])
        cs = (_dot(tri, hi) + _dot(tri, mid) + _dot(tri, lo)) + carry
        carry = cs[tb - 1:tb, :]
        hi, mid, lo = _split3(cs * LOG2E)
        kaug = _dot(hi, place(0)) + _dot(mid, place(1)) + _dot(lo, place(2)) + key_const
        kaug_ref[rows, :] = kaug.astype(BF16)
        qaug = _dot(hi, place(3)) + _dot(mid, place(4)) + _dot(lo, place(5)) + qry_const
        qaug_ref[:, rows] = jnp.transpose(qaug).astype(BF16)


def _forget_aug(logf, nb, s):
    return pl.pallas_call(
        functools.partial(_forget_aug_kernel, nblk=s // ATT_TILE), grid=(nb,),
        in_specs=[pl.BlockSpec((s, LANES), lambda b: (b, 0))],
        out_specs=[pl.BlockSpec((s, LANES), lambda b: (b, 0)), pl.BlockSpec((LANES, s), lambda b: (0, b))],
        out_shape=[jax.ShapeDtypeStruct((nb * s, LANES), BF16), jax.ShapeDtypeStruct((LANES, nb * s), BF16)],
        compiler_params=_cparams("parallel"), name="forget_aug")(logf)


def _suffix_sum_kernel(x_ref, o_ref):
    p = x_ref.shape[1]
    j = lax.broadcasted_iota(jnp.int32, (p, p), 0)
    s = lax.broadcasted_iota(jnp.int32, (p, p), 1)
    tri = jnp.where(j > s, 1.0, 0.0).astype(BF16)
    hi, mid, lo = _split3(x_ref[...])
    o_ref[...] = _dot(hi, tri) + _dot(mid, tri) + _dot(lo, tri)


def _suffix_sum(x):
    return pl.pallas_call(
        _suffix_sum_kernel, out_shape=jax.ShapeDtypeStruct(x.shape, F32),
        compiler_params=pltpu.CompilerParams(vmem_limit_bytes=VMEM_LIMIT), name="suffix_sum")(x)


def _attn_a_kernel(qt_ref, k_ref, vt_ref, kaug_ref, qaug_ref, o_ref, qp_ref, m_ref, l_ref, acc_ref):
    tq = qt_ref.shape[1]
    tk = ATT_TILE
    qi = pl.program_id(1)
    row = lax.broadcasted_iota(jnp.int32, (PAIR_W, 1), 0)
    aug = qaug_ref[...]
    for h in range(H_A):
        q2 = qt_ref[_pair_slice(h), :]
        half = (row < HEAD_DIM) if h % 2 == 0 else (row >= HEAD_DIM)
        qp_ref[h, :PAIR_W, :] = jnp.where(half, q2, jnp.zeros_like(q2))
        mine = (row >= AUG_SLOTS * h) & (row < AUG_SLOTS * (h + 1))
        qp_ref[h, PAIR_W:, :] = jnp.where(mine, aug, jnp.zeros_like(aug))
    m_ref[...] = jnp.full(m_ref.shape, NEG_INF, F32)
    l_ref[...] = jnp.zeros(l_ref.shape, F32)
    acc_ref[...] = jnp.zeros(acc_ref.shape, F32)

    def step(kt, key_offset):
        ks = pl.multiple_of(kt * tk, tk)
        ka = kaug_ref[pl.ds(ks, tk), :]
        tile_scores = [_dot(jnp.concatenate([k_ref[pl.ds(ks, tk), _pair_slice(h)], ka], axis=1), qp_ref[h])
                       for h in range(H_A)]
        probs, alphas = [], []
        for h in range(H_A):
            s = tile_scores[h]
            if key_offset is not None:
                key = lax.broadcasted_iota(jnp.int32, (tk, tq), 0)
                qry = lax.broadcasted_iota(jnp.int32, (tk, tq), 1)
                s = jnp.where(key + key_offset <= qry, s, NEG_INF)
            m_prev = m_ref[h]
            m_new = jnp.maximum(m_prev, jnp.max(s, axis=0, keepdims=True))
            alpha = jnp.exp2(m_prev - m_new)
            p = jnp.exp2(s - m_new)
            l_ref[h] = alpha * l_ref[h] + jnp.sum(p, axis=0, keepdims=True)
            m_ref[h] = m_new
            alphas.append(alpha)
            probs.append(p.astype(BF16))
        for h in range(H_A):
            acc_ref[h] = alphas[h] * acc_ref[h] + _dot(vt_ref[0, kt, _pair_slice(h), :], probs[h])

    def body(kt, carry):
        step(kt, None)
        return carry

    per = tq // tk
    lax.fori_loop(0, qi * per, body, 0)
    for j in range(per):
        step(qi * per + j, j * tk)
    for pair in range(N_PAIRS):
        h = 2 * pair
        out_t = jnp.where(row < HEAD_DIM, acc_ref[h] * (1.0 / l_ref[h]), acc_ref[h + 1] * (1.0 / l_ref[h + 1]))
        o_ref[:, pair * PAIR_W:(pair + 1) * PAIR_W] = jnp.transpose(out_t)


def _attn_a(qat, ka, vat, kaug, qaug, nb, s):
    tq = A_QTILE
    nq = s // tq
    return pl.pallas_call(
        _attn_a_kernel, grid=(nb, nq),
        in_specs=[pl.BlockSpec((GROUP_W, tq), lambda b, i: (0, b * nq + i)),
                  pl.BlockSpec((s, GROUP_W), lambda b, i: (b, 0)),
                  pl.BlockSpec((1, s // ATT_TILE, GROUP_W, ATT_TILE), lambda b, i: (b, 0, 0, 0)),
                  pl.BlockSpec((s, LANES), lambda b, i: (b, 0)),
                  pl.BlockSpec((LANES, tq), lambda b, i: (0, b * nq + i))],
        out_specs=pl.BlockSpec((tq, GROUP_W), lambda b, i: (b * nq + i, 0)),
        out_shape=jax.ShapeDtypeStruct((nb * s, GROUP_W), F32),
        scratch_shapes=[pltpu.VMEM((H_A, 2 * PAIR_W, tq), BF16), pltpu.VMEM((H_A, 1, tq), F32),
                        pltpu.VMEM((H_A, 1, tq), F32), pltpu.VMEM((H_A, PAIR_W, tq), F32)],
        compiler_params=_cparams("parallel", "arbitrary"), name="attn_a")(qat, ka, vat, kaug, qaug)


def _masked_q(q_ref, qm_ref, n_heads):
    for h in range(n_heads):
        q2 = q_ref[:, _pair_slice(h)]
        qm_ref[h] = jnp.where(_half_mask(h % 2), q2, jnp.zeros_like(q2))


def _attn_b_kernel(q_ref, k_ref, v_ref, tab_ref, o_ref):
    tq = ATT_TILE
    win = tq + BAND
    i = pl.program_id(1)
    var = jnp.minimum(i, BAND // tq)
    ws = pl.multiple_of(jnp.maximum(i * tq - BAND, 0), tq)
    heads = range(H_B)
    scores = []
    for h in heads:
        q2 = q_ref[:, _pair_slice(h)]
        qm = jnp.where(_half_mask(h % 2), q2, jnp.zeros_like(q2))
        scores.append(_dot_nt(qm, k_ref[pl.ds(ws, win), _pair_slice(h)]) + tab_ref[var, h])
    probs, norms = [], []
    for h in heads:
        p = jnp.exp2(scores[h] - jnp.max(scores[h], axis=-1, keepdims=True))
        norms.append(1.0 / jnp.sum(p, axis=-1, keepdims=True))
        probs.append(p.astype(BF16))
    res = [_dot(probs[h], v_ref[pl.ds(ws, win), _pair_slice(h)]) * norms[h] for h in heads]
    res.append(jnp.zeros((tq, PAIR_W), F32))
    first = _half_mask(0)
    for pair in range(N_PAIRS):
        o_ref[:, pair * PAIR_W:(pair + 1) * PAIR_W] = jnp.where(first, res[2 * pair], res[2 * pair + 1])


def _suffix_tri(n):
    j = lax.broadcasted_iota(jnp.int32, (2 * n, n), 0)
    s = lax.broadcasted_iota(jnp.int32, (2 * n, n), 1)
    return jnp.where(jnp.where(j >= n, j - n, j) >= s, 1.0, 0.0).astype(BF16)


def _stick_scores(z, tri, seen):
    sp = _softplus2(z)
    if seen is not None:
        sp = jnp.where(seen, sp, 0.0)
    hi, lo = _split2(sp)
    n = sp.shape[1]
    if n % LANES == 0:
        return _dot(jnp.concatenate([hi, lo], axis=1), tri)
    return _dot(hi, tri[:n]) + _dot(lo, tri[:n])


def _stick_weights(z, s_in, carry, seen):
    a = jnp.exp2(z - (s_in + carry))
    if seen is not None:
        a = jnp.where(seen, a, 0.0)
    return a.astype(BF16)


def _attn_c_kernel(q_ref, k_ref, v_ref, o_ref, qm_ref, carry_ref, acc_ref):
    tq = q_ref.shape[0]
    tk = ATT_TILE
    qi = pl.program_id(1)
    _masked_q(q_ref, qm_ref, H_C)
    carry_ref[...] = jnp.zeros(carry_ref.shape, F32)
    acc_ref[...] = jnp.zeros(acc_ref.shape, F32)
    heads = range(H_C)

    def step(kt, key_offset):
        ks = pl.multiple_of(kt * tk, tk)
        zs = [_dot_nt(qm_ref[h], k_ref[pl.ds(ks, tk), _pair_slice(h)]) for h in heads]
        tri = _suffix_tri(tk)
        seen = None
        if key_offset is not None:
            r = lax.broadcasted_iota(jnp.int32, (tq, tk), 0)
            c = lax.broadcasted_iota(jnp.int32, (tq, tk), 1)
            seen = c + key_offset < r
        sums = [_stick_scores(zs[h], tri, seen) for h in heads]
        weights = []
        for h in heads:
            carry = carry_ref[h][:, 0:1]
            weights.append(_stick_weights(zs[h], sums[h], carry, seen))
            carry_ref[h] = jnp.broadcast_to(carry + sums[h][:, 0:1], (tq, LANES))
        for h in heads:
            acc_ref[h] += _dot(weights[h], v_ref[pl.ds(ks, tk), _pair_slice(h)])

    per = tq // tk

    def body(j, c):
        step(qi * per - 1 - j, None)
        return c

    for j in reversed(range(per)):
        step(qi * per + j, j * tk)
    lax.fori_loop(0, qi * per, body, 0)
    first = _half_mask(0)
    for pair in range(N_PAIRS):
        h = 2 * pair
        second = acc_ref[h + 1] if h + 1 < H_C else jnp.zeros((tq, PAIR_W), F32)
        o_ref[:, pair * PAIR_W:(pair + 1) * PAIR_W] = jnp.where(first, acc_ref[h], second)


def _prompt_attention(kernel, name, q, k, v, extra, extra_specs, scratch, nb, s, tq):
    nq = s // tq
    qspec = pl.BlockSpec((tq, GROUP_W), lambda b, i: (b * nq + i, 0))
    kvspec = pl.BlockSpec((s, GROUP_W), lambda b, i: (b, 0))
    return pl.pallas_call(
        kernel, grid=(nb, nq),
        in_specs=[qspec, kvspec, kvspec] + extra_specs,
        out_specs=qspec, out_shape=jax.ShapeDtypeStruct((nb * s, GROUP_W), F32),
        scratch_shapes=scratch,
        compiler_params=_cparams("parallel", "arbitrary"), name=name)(q, k, v, *extra)


def _sample_heads(n_heads, q_ref, width):
    out = []
    for h in range(n_heads):
        lo = (h // 2) * PAIR_W
        span = slice(lo, min(lo + PAIR_W, width))
        q2 = q_ref[:, span]
        if span.stop - span.start == PAIR_W:
            q2 = jnp.where(_half_mask(h % 2), q2, jnp.zeros_like(q2))
        out.append((h, span, q2))
    return out


def _store_heads(o_ref, res, n_heads):
    for pair in range(N_PAIRS):
        lo = pair * PAIR_W
        h = 2 * pair
        if h + 1 < n_heads:
            o_ref[:, lo:lo + PAIR_W] = jnp.where(_half_mask(0), res[h], res[h + 1])
        else:
            o_ref[:, lo:lo + HEAD_DIM] = res[h]
            o_ref[:, lo + HEAD_DIM:lo + PAIR_W] = jnp.zeros_like(res[h])


def _joint_softmax_pv(s_c, s_n, vt_c, v_n):
    m = jnp.maximum(jnp.max(s_c, axis=-1, keepdims=True), jnp.max(s_n, axis=-1, keepdims=True))
    p_c = jnp.exp2(s_c - m)
    p_n = jnp.exp2(s_n - m)
    l = jnp.sum(p_c, axis=-1, keepdims=True) + jnp.sum(p_n, axis=-1, keepdims=True)
    return (_dot_nt(p_c.astype(BF16), vt_c) + _dot(p_n.astype(BF16), v_n)) / l


def _sample_attn_kernel(qa_ref, qb_ref, qc_ref, kan_ref, kbn_ref, kcn_ref, van_ref, vbn_ref, vcn_ref,
                        cak_ref, cav_ref, cbk_ref, cbv_ref, cck_ref, ccv_ref,
                        rsum_ref, logf_ref, tabc_ref, tabn_ref,
                        oa_ref, ob_ref, oc_ref):
    t = qa_ref.shape[0]
    past = cak_ref.shape[3]
    r = lax.broadcasted_iota(jnp.int32, (t, t), 0)
    c = lax.broadcasted_iota(jnp.int32, (t, t), 1)

    ltri = jnp.where(c <= r, 1.0, 0.0).astype(BF16)
    hi, mid, lo = _split3(logf_ref[...])
    pcol = (_dot(ltri, hi) + _dot(ltri, mid) + _dot(ltri, lo)) * LOG2E
    prow = jnp.transpose(jnp.concatenate([pcol, jnp.zeros((LANES - t, LANES), F32)], axis=0))
    heads = _sample_heads(H_A, qa_ref, W_A)
    kts = {sp.start: cak_ref[0, 0, sp, :].astype(BF16) for _, sp, _ in heads}
    vts = {sp.start: cav_ref[0, 0, sp, :].astype(BF16) for _, sp, _ in heads}
    s_c = [_dot(qm, kts[sp.start]) + (pcol[:, h:h + 1] + rsum_ref[0, 0, h:h + 1, :] * LOG2E)
           for h, sp, qm in heads]
    s_n = [jnp.where(c <= r, _dot_nt(qm, kan_ref[:, sp]) + (pcol[:, h:h + 1] - prow[h:h + 1, 0:t]), NEG_INF)
           for h, sp, qm in heads]
    _store_heads(oa_ref, [_joint_softmax_pv(s_c[h], s_n[h], vts[sp.start], van_ref[:, sp])
                          for h, sp, _ in heads], H_A)

    heads = _sample_heads(H_B, qb_ref, W_B)
    kts = {sp.start: cbk_ref[0, 0, sp, :].astype(BF16) for _, sp, _ in heads}
    vts = {sp.start: cbv_ref[0, 0, sp, :].astype(BF16) for _, sp, _ in heads}
    s_c = [_dot(qm, kts[sp.start]) + tabc_ref[h] for h, sp, qm in heads]
    s_n = [_dot_nt(qm, kbn_ref[:, sp]) + tabn_ref[h] for h, sp, qm in heads]
    _store_heads(ob_ref, [_joint_softmax_pv(s_c[h], s_n[h], vts[sp.start], vbn_ref[:, sp])
                          for h, sp, _ in heads], H_B)

    heads = _sample_heads(H_C, qc_ref, W_C)
    nblk = past // ATT_TILE
    tri_n = _suffix_tri(t)
    tri_c = _suffix_tri(ATT_TILE)
    seen = c < r
    kts = {sp.start: cck_ref[0, 0, sp, :].astype(BF16) for _, sp, _ in heads}
    vts = {sp.start: ccv_ref[0, 0, sp, :].astype(BF16) for _, sp, _ in heads}
    z_n = [_dot_nt(qm, kcn_ref[:, sp]) for _, sp, qm in heads]
    z_c = [_dot(qm, kts[sp.start]) for _, sp, qm in heads]
    sum_n = [_stick_scores(z, tri_n, seen) for z in z_n]
    sum_c = [[_stick_scores(z[:, b * ATT_TILE:(b + 1) * ATT_TILE], tri_c, None) for b in range(nblk)]
             for z in z_c]
    res = []
    for h, sp, _ in heads:
        acc = _dot(_stick_weights(z_n[h], sum_n[h], 0.0, seen), vcn_ref[:, sp])
        carry = sum_n[h][:, 0:1]
        for b in reversed(range(nblk)):
            cols = slice(b * ATT_TILE, (b + 1) * ATT_TILE)
            a = _stick_weights(z_c[h][:, cols], sum_c[h][b], carry, None)
            acc = acc + _dot_nt(a, vts[sp.start][:, cols])
            carry = carry + sum_c[h][b][:, 0:1]
        res.append(acc)
    _store_heads(oc_ref, res, H_C)


def _sample_attention(proj, caches, layer, rsum, tabc, tabn, nb, t):
    qa, qb, qc, ka, kb, kc, va, vb, vc = proj[:9]
    logf = proj[15]
    new = pl.BlockSpec((t, GROUP_W), lambda b: (b, 0))
    cache = lambda a: pl.BlockSpec((1, 1) + a.shape[2:], lambda b: (layer, b, 0, 0))
    full = lambda a: pl.BlockSpec(a.shape, lambda b: (0,) * a.ndim)
    return pl.pallas_call(
        _sample_attn_kernel, grid=(nb,),
        in_specs=[new] * 9 + [cache(a) for a in caches] + [
            cache(rsum), pl.BlockSpec((t, LANES), lambda b: (b, 0)), full(tabc), full(tabn)],
        out_specs=[new] * 3,
        out_shape=[jax.ShapeDtypeStruct((nb * t, GROUP_W), F32)] * 3,
        compiler_params=_cparams("parallel"), name="sample_attn")(
            qa, qb, qc, ka, kb, kc, va, vb, vc, *caches, rsum, logf, tabc, tabn)


def _masked_max(x, mask):
    return jnp.max(jnp.where(mask, x, -jnp.inf), axis=-1, keepdims=True)


def _first_lane(mask, lane):
    return jnp.min(jnp.where(mask, lane, float(LANES)), axis=-1, keepdims=True)


def _route(logits):
    lane = _lane_iota().astype(F32)
    is_g = (lane >= N_EXPERTS) & (lane < N_EXPERTS + N_GROUPS)
    gmax = _masked_max(logits, is_g)
    p_g = 1.0 / jnp.sum(jnp.where(is_g, jnp.exp(logits - gmax), 0.0), axis=-1, keepdims=True)
    g_sel = _first_lane(is_g & (logits == gmax), lane) - N_EXPERTS
    lo = g_sel * EXPERTS_PER_GROUP
    in_g = (lane >= lo) & (lane < lo + EXPERTS_PER_GROUP)
    l1 = _masked_max(logits, in_g)
    i1 = _first_lane(in_g & (logits == l1), lane)
    rest = in_g & (lane != i1)
    l2 = _masked_max(logits, rest)
    i2 = _first_lane(rest & (logits == l2), lane)
    e2 = jnp.exp(l2 - l1)
    w1 = p_g / (1.0 + e2)
    w2 = p_g * e2 / (1.0 + e2)
    gates = jnp.where(lane == i1, w1, jnp.where(lane == i2, w2, 0.0))
    return jnp.where(lane == g_sel + N_EXPERTS, 1.0, gates)


def _post_kernel(oa_ref, ob_ref, oc_ref, h_ref, gmix_ref, wout_ref, nffn_ref, wrh_ref, wrl_ref, br_ref,
                 h1_ref, gates_ref, *x_refs, packed):
    def gnorm(o, width):
        ms = jnp.sum(o * o, axis=-1, keepdims=True) * (1.0 / width)
        return o * lax.rsqrt(ms + EPS)

    h1 = h_ref[...]
    for g, (o_ref, width) in enumerate(((oa_ref, W_A), (ob_ref, W_B), (oc_ref, W_C))):
        merged = (gnorm(o_ref[...], width) * gmix_ref[g]).astype(BF16)
        h1 = h1 + _dot(merged, wout_ref[g])
    h1_ref[...] = h1
    xf = _rms(h1, nffn_ref[...])
    hi, lo = _split2(xf)
    if packed:
        bits = pltpu.bitcast(hi.astype(F32), jnp.uint32)
        for c, ref in enumerate(x_refs):
            a = bits[:, (2 * c) * SC_ROW_WORDS:(2 * c + 1) * SC_ROW_WORDS]
            b = bits[:, (2 * c + 1) * SC_ROW_WORDS:(2 * c + 2) * SC_ROW_WORDS]
            ref[...] = jnp.bitwise_or(a, jnp.right_shift(b, jnp.uint32(16)))
    else:
        x_refs[0][...] = hi
    logits = _dot(hi, wrh_ref[...]) + _dot(lo, wrh_ref[...]) + _dot(hi, wrl_ref[...]) + br_ref[...]
    gates_ref[...] = _route(logits)


def _post(oa, ob, oc, h, lw, tm, n_chunks=0):
    t = h.shape[0]
    row = lambda w: pl.BlockSpec((tm, w), lambda i: (i, 0))
    full = lambda a: pl.BlockSpec(a.shape, lambda i: (0,) * a.ndim)
    ws = (lw['g_mix'], lw['w_out'], lw['norm_ffn'], lw['w_r_hi'], lw['w_r_lo'], lw['b_r'])
    if n_chunks:
        x_specs = [row(SC_ROW_WORDS)] * n_chunks
        x_shapes = [jax.ShapeDtypeStruct((t, SC_ROW_WORDS), jnp.uint32)] * n_chunks
    else:
        x_specs, x_shapes = [row(D_MODEL)], [jax.ShapeDtypeStruct((t, D_MODEL), BF16)]
    return pl.pallas_call(
        functools.partial(_post_kernel, packed=bool(n_chunks)), grid=(t // tm,),
        in_specs=[row(GROUP_W)] * 3 + [row(D_MODEL)] + [full(a) for a in ws],
        out_specs=[row(D_MODEL), row(LANES)] + x_specs,
        out_shape=[jax.ShapeDtypeStruct((t, D_MODEL), F32), jax.ShapeDtypeStruct((t, LANES), F32)] + x_shapes,
        compiler_params=_cparams("parallel"), name="post")(oa, ob, oc, h, *ws)


def _moe_kernel(xn_ref, gates_ref, wg_ref, wu_ref, wd_ref, h1_ref, p_ref, wple_ref, nple_ref, wpg_ref,
                out_ref, acc_ref):
    step = pl.program_id(1)

    @pl.when(step == 0)
    def _():
        acc_ref[...] = jnp.zeros_like(acc_ref)

    x = xn_ref[...]
    gates = gates_ref[...]
    acts = []
    for k in range(EXPERTS_PER_STEP):
        e = step * EXPERTS_PER_STEP + k
        g = _dot(x, wg_ref[k])
        u = _dot(x, wu_ref[k])
        gate = jnp.sum(jnp.where(_lane_iota() == e, gates, 0.0), axis=-1, keepdims=True)
        acts.append(((g * _sigmoid(g)) * u * gate).astype(BF16))
    acc_ref[...] += _dot(jnp.concatenate(acts, axis=1), wd_ref[...])

    @pl.when(step == pl.num_programs(1) - 1)
    def _():
        h2 = h1_ref[...] + acc_ref[...]
        gate_ple = _sigmoid(_dot(_rms(h2, nple_ref[...]).astype(BF16), wpg_ref[...]))
        out_ref[...] = h2 + _dot(p_ref[0].astype(BF16), wple_ref[...]) * gate_ple


def _moe(xn, gates, h1, p_all, layer, lw, tm):
    t = xn.shape[0]
    row = lambda w: pl.BlockSpec((tm, w), lambda i, e: (i, 0))
    once = pl.Buffered(1)
    full = lambda a: pl.BlockSpec(a.shape, lambda i, e: (0,) * a.ndim, pipeline_mode=once)
    exp = lambda a: pl.BlockSpec((EXPERTS_PER_STEP,) + a.shape[1:], lambda i, e: (e, 0, 0))
    d_hidden = lw['w_ed'].shape[0] // N_EXPERTS
    return pl.pallas_call(
        _moe_kernel, grid=(t // tm, N_EXPERTS // EXPERTS_PER_STEP),
        in_specs=[row(D_MODEL), row(LANES), exp(lw['w_eg']), exp(lw['w_eu']),
                  pl.BlockSpec((EXPERTS_PER_STEP * d_hidden, D_MODEL), lambda i, e: (e, 0)),
                  pl.BlockSpec((tm, D_MODEL), lambda i, e: (i, 0), pipeline_mode=once),
                  pl.BlockSpec((1, tm, p_all.shape[2]), lambda i, e: (layer, i, 0), pipeline_mode=once),
                  full(lw['w_ple']), full(lw['norm_ple']), full(lw['w_pg'])],
        out_specs=row(D_MODEL), out_shape=jax.ShapeDtypeStruct((t, D_MODEL), F32),
        scratch_shapes=[pltpu.VMEM((tm, D_MODEL), F32)],
        compiler_params=pltpu.CompilerParams(dimension_semantics=("parallel", "arbitrary"),
                                             vmem_limit_bytes=MOE_VMEM_LIMIT), name="moe")(
            xn, gates, lw['w_eg'], lw['w_eu'], lw['w_ed'], h1, p_all, lw['w_ple'], lw['norm_ple'], lw['w_pg'])


SORT_TILE = 512
SC_WINDOW = 128
SC_ROW_WORDS = 256
N_CHUNKS = D_MODEL // SC_ROW_WORDS
N_XCHUNKS = N_CHUNKS // 2


def _plan_kernel(gates_ref, dest_ref, total_ref, *, tb):
    lane = _lane_iota()
    is_g = (lane >= N_EXPERTS) & (lane < N_EXPERTS + N_GROUPS)
    total = jnp.sum(jnp.where(is_g, gates_ref[...], 0.0), axis=0, keepdims=True)
    padded = jnp.floor((total + (SORT_TILE - 1)) * (1.0 / SORT_TILE)) * SORT_TILE
    base = jnp.zeros((1, LANES), F32)
    for g in range(1, N_GROUPS):
        before = jnp.sum(jnp.where(lane < N_EXPERTS + g, padded, 0.0), axis=-1, keepdims=True)
        base = jnp.where(lane == N_EXPERTS + g, before, base)
    r = lax.broadcasted_iota(jnp.int32, (tb, tb), 0)
    c = lax.broadcasted_iota(jnp.int32, (tb, tb), 1)
    ltri = jnp.where(c < r, 1.0, 0.0).astype(BF16)
    carry = jnp.zeros((1, LANES), F32)
    for blk in range(gates_ref.shape[0] // tb):
        rows = slice(blk * tb, (blk + 1) * tb)
        member = jnp.where(is_g, gates_ref[rows, :], 0.0)
        rank = _dot(ltri, member.astype(BF16)) + carry
        carry = carry + jnp.sum(member, axis=0, keepdims=True)
        dest = jnp.sum(member * (base + rank), axis=-1, keepdims=True)
        dest_ref[rows, :] = jnp.broadcast_to(dest, (tb, LANES))
    total_ref[...] = jnp.broadcast_to(total, total_ref.shape)


def _plan(gates):
    t = gates.shape[0]
    return pl.pallas_call(
        functools.partial(_plan_kernel, tb=SORT_TILE),
        out_shape=[jax.ShapeDtypeStruct((t, LANES), F32), jax.ShapeDtypeStruct((SUBLANES, LANES), F32)],
        compiler_params=pltpu.CompilerParams(vmem_limit_bytes=VMEM_LIMIT), name="moe_plan")(gates)


def _sc_scatter_rows(x, idx, n_out):
    n, d = x.shape
    mesh = plsc.VectorSubcoreMesh(core_axis_name="core", subcore_axis_name="subcore")
    assert n % (SC_WINDOW * mesh.num_cores * mesh.num_subcores) == 0

    @functools.partial(pl.kernel, out_type=jax.ShapeDtypeStruct((n_out, d), x.dtype), mesh=mesh, scratch_types=[])
    def scatter(x_hbm, i_hbm, o_hbm):
        def body(x_vmem, i_vmem):
            pltpu.sync_copy(x_vmem, o_hbm.at[i_vmem.at[0]])

        pltpu.emit_pipeline(
            body, grid=(n // SC_WINDOW,),
            in_specs=[pl.BlockSpec((SC_WINDOW, d), index_map=lambda i: (i, 0)),
                      pl.BlockSpec((1, SC_WINDOW), index_map=lambda i: (0, i))],
            out_specs=[], core_axis_name=('core', 'subcore'), dimension_semantics=(pltpu.PARALLEL,))(x_hbm, i_hbm)

    return scatter(x, idx.reshape(1, n))


def _sc_gather_rows(x, idx):
    n = idx.shape[0]
    d = x.shape[1]
    mesh = plsc.VectorSubcoreMesh(core_axis_name="core", subcore_axis_name="subcore")
    assert n % (SC_WINDOW * mesh.num_cores * mesh.num_subcores) == 0

    @functools.partial(pl.kernel, out_type=jax.ShapeDtypeStruct((n, d), x.dtype), mesh=mesh)
    def gather(x_hbm, i_hbm, o_hbm):
        def body(i_vmem, o_vmem):
            pltpu.sync_copy(x_hbm.at[i_vmem.at[0]], o_vmem)

        pltpu.emit_pipeline(
            body, grid=(n // SC_WINDOW,),
            in_specs=[pl.BlockSpec((1, SC_WINDOW), index_map=lambda i: (0, i))],
            out_specs=[pl.BlockSpec((SC_WINDOW, d), index_map=lambda i: (i, 0))],
            core_axis_name=('core', 'subcore'), dimension_semantics=(pltpu.PARALLEL,))(i_hbm, o_hbm)

    return gather(x, idx.reshape(1, n))


def _experts_sorted_kernel(tg_ref, tv_ref, *refs):
    xs_refs, (gs_ref, wg_ref, wu_ref, wd_ref), ys_refs = (
        refs[:N_XCHUNKS], refs[N_XCHUNKS:N_XCHUNKS + 4], refs[-N_CHUNKS:])
    j = pl.program_id(0)

    @pl.when(tv_ref[j] > 0)
    def _():
        g = tg_ref[j]
        halves = []
        for r in xs_refs:
            w = r[...]
            halves.append(pltpu.bitcast(jnp.bitwise_and(w, jnp.uint32(0xFFFF0000)), F32))
            halves.append(pltpu.bitcast(jnp.left_shift(w, jnp.uint32(16)), F32))
        x = jnp.concatenate(halves, axis=1).astype(BF16)
        gates = gs_ref[...]
        acts = []
        for k in range(EXPERTS_PER_GROUP):
            hg = _dot(x, wg_ref[k])
            gate = jnp.sum(jnp.where(_lane_iota() == g * EXPERTS_PER_GROUP + k, gates, 0.0),
                           axis=-1, keepdims=True)
            acts.append(((hg * _sigmoid(hg)) * _dot(x, wu_ref[k]) * gate).astype(BF16))
        y = _dot(jnp.concatenate(acts, axis=1), wd_ref[...])
        for c, ref in enumerate(ys_refs):
            ref[...] = y[:, c * SC_ROW_WORDS:(c + 1) * SC_ROW_WORDS]


def _experts_sorted(xs_chunks, gs, tile_group, tile_valid, lw):
    n_rows = gs.shape[0]
    row = lambda w: pl.BlockSpec((SORT_TILE, w), lambda j, tg, tv: (j, 0))
    grp = lambda a: pl.BlockSpec((EXPERTS_PER_GROUP,) + a.shape[1:], lambda j, tg, tv: (tg[j], 0, 0))
    d_group = lw['w_ed'].shape[0] // N_GROUPS
    return pl.pallas_call(
        _experts_sorted_kernel,
        grid_spec=pltpu.PrefetchScalarGridSpec(
            num_scalar_prefetch=2, grid=(n_rows // SORT_TILE,),
            in_specs=[row(SC_ROW_WORDS)] * N_XCHUNKS + [
                row(LANES), grp(lw['w_eg']), grp(lw['w_eu']),
                pl.BlockSpec((d_group, D_MODEL), lambda j, tg, tv: (tg[j], 0))],
            out_specs=[row(SC_ROW_WORDS)] * N_CHUNKS),
        out_shape=[jax.ShapeDtypeStruct((n_rows, SC_ROW_WORDS), F32)] * N_CHUNKS,
        compiler_params=_cparams("arbitrary"), name="moe_sorted")(
            tile_group, tile_valid, *xs_chunks, gs, lw['w_eg'], lw['w_eu'], lw['w_ed'])


def _ple_kernel(h1_ref, *refs):
    y_refs, (p_ref, wple_ref, nple_ref, wpg_ref, out_ref) = refs[:N_CHUNKS], refs[N_CHUNKS:]
    h2 = h1_ref[...] + jnp.concatenate([r[...] for r in y_refs], axis=1)
    gate_ple = _sigmoid(_dot(_rms(h2, nple_ref[...]).astype(BF16), wpg_ref[...]))
    out_ref[...] = h2 + _dot(p_ref[0].astype(BF16), wple_ref[...]) * gate_ple


def _ple(h1, y_chunks, p_all, layer, lw, tm):
    t = h1.shape[0]
    row = lambda w: pl.BlockSpec((tm, w), lambda i: (i, 0))
    full = lambda a: pl.BlockSpec(a.shape, lambda i: (0,) * a.ndim)
    return pl.pallas_call(
        _ple_kernel, grid=(t // tm,),
        in_specs=[row(D_MODEL)] + [row(SC_ROW_WORDS)] * N_CHUNKS
        + [pl.BlockSpec((1, tm, p_all.shape[2]), lambda i: (layer, i, 0)),
           full(lw['w_ple']), full(lw['norm_ple']), full(lw['w_pg'])],
        out_specs=row(D_MODEL), out_shape=jax.ShapeDtypeStruct((t, D_MODEL), F32),
        compiler_params=_cparams("parallel"), name="ple")(
            h1, *y_chunks, p_all, lw['w_ple'], lw['norm_ple'], lw['w_pg'])


def _moe_dispatch(x_chunks, gates):
    t = gates.shape[0]
    n_rows = t + N_GROUPS * SORT_TILE
    dest_rep, total = _plan(gates)
    dest = dest_rep[:, 0].astype(jnp.int32)
    padded = (total[0, N_EXPERTS:N_EXPERTS + N_GROUPS].astype(jnp.int32) + (SORT_TILE - 1)) // SORT_TILE * SORT_TILE
    ends = jnp.cumsum(padded)
    tile_start = jnp.arange(n_rows // SORT_TILE, dtype=jnp.int32) * SORT_TILE
    tile_group = jnp.minimum(jnp.sum(tile_start[:, None] >= ends[None, :], axis=1), N_GROUPS - 1).astype(jnp.int32)
    tile_valid = (tile_start < ends[-1]).astype(jnp.int32)
    xs_chunks = [_sc_scatter_rows(xc, dest, n_rows) for xc in x_chunks]
    gs = _sc_scatter_rows(gates, dest, n_rows)
    return xs_chunks, gs, tile_group, tile_valid, dest


def _moe_combine(dispatched, h1, p_all, layer, lw):
    xs_chunks, gs, tile_group, tile_valid, dest = dispatched
    ys_chunks = _experts_sorted(xs_chunks, gs, tile_group, tile_valid, lw)
    y_chunks = [_sc_gather_rows(yc, dest) for yc in ys_chunks]
    return _ple(h1, y_chunks, p_all, layer, lw, _token_tile(h1.shape[0], 1024))


def _regroup_out_rows(wt):
    z = jnp.zeros((HEAD_DIM, wt.shape[1]), wt.dtype)
    return jnp.concatenate([wt[:W_A], wt[W_A:W_A + W_B], z, wt[W_A + W_B:], z], axis=0)


def _regroup_rows(w):
    z = jnp.zeros((HEAD_DIM, w.shape[1]), w.dtype)
    return jnp.stack([w[:W_A], jnp.concatenate([w[W_A:W_A + W_B], z], axis=0),
                      jnp.concatenate([w[W_A + W_B:], z], axis=0)])


def _pad_lanes(a, n=LANES):
    return jnp.pad(a, [(0, 0)] * (a.ndim - 1) + [(0, n - a.shape[-1])])


def _toeplitz_bias(rel_bias, off, tq, win):
    length = tq + win
    pad = length + abs(off)
    ext = jnp.flip(jnp.pad(rel_bias, ((0, 0), (pad, pad)), mode='edge'), axis=1)
    s1 = ext.shape[1] - 1 - (off + REL_CLIP + pad)
    v = jnp.concatenate([ext[:, s1:s1 + win], ext[:, s1 - tq:s1]], axis=1)
    flat = jnp.tile(v, (1, tq))[:, :tq * (length - 1)]
    return flat.reshape(-1, tq, length - 1)[:, :, :win]


def _block_toeplitz_bias(rel_bias, off, tq, win):
    nq, nk = tq // LANES, win // LANES
    blocks = {d: _toeplitz_bias(rel_bias, off - d * LANES, LANES, LANES) for d in range(-(nq - 1), nk)}
    return jnp.concatenate(
        [jnp.concatenate([blocks[b - a] for b in range(nk)], axis=2) for a in range(nq)], axis=1)


def _rel_tables(rel_bias, t_new, rows_b):
    tq, win = ATT_TILE, ATT_TILE + BAND
    q = jnp.arange(tq)[:, None]
    tabs = []
    for var in range(BAND // tq + 1):
        rel_k = jnp.arange(win)[None, :] - var * tq
        kch = jnp.floor_divide(rel_k, CHUNK)
        qch = q // CHUNK
        valid = (kch <= qch) & (kch >= qch - BAND_CHUNKS)
        tabs.append(jnp.where(valid[None], _block_toeplitz_bias(rel_bias, var * tq, tq, win), NEG_INF))
    tab_prompt = jnp.stack(tabs).astype(F32)
    tab_c = _toeplitz_bias(rel_bias, rows_b, t_new, rows_b).astype(F32)
    tab_n = _toeplitz_bias(rel_bias, 0, t_new, t_new).astype(F32)
    return tab_prompt, tab_c, tab_n


def _layer_weights(i, norm_mix, w_in, b_f, g_qa, g_ka, g_qb, g_kb, g_mix_out, w_out, norm_ffn,
                   w_rg, b_rg, w_re, b_re, w_eg, w_eu, w_ed, w_ple, norm_ple, w_pg):
    d_mix = W_A + W_B + W_C
    wt = jnp.transpose(w_in, (2, 0, 1))[:, i, :]
    w_f = wt[3 * d_mix:]
    w_big = jnp.concatenate(
        [_regroup_out_rows(wt[j * d_mix:(j + 1) * d_mix]) for j in range(3)]
        + [jnp.pad(w_f, ((0, LANES - w_f.shape[0]), (0, 0)))], axis=0).astype(BF16)
    tile6 = lambda g: jnp.tile(g, GROUP_W // HEAD_DIM)
    w_r = _pad_lanes(jnp.concatenate([w_re[i], w_rg[i]], axis=1))
    w_r_hi = w_r.astype(BF16)
    gm = g_mix_out[i]
    zpad = jnp.zeros((HEAD_DIM,), F32)
    g_mix = jnp.stack([gm[:W_A], jnp.concatenate([gm[W_A:W_A + W_B], zpad]),
                       jnp.concatenate([gm[W_A + W_B:], zpad])])[:, None, :]
    return dict(
        norm_mix=norm_mix[i][None], w_in=w_big,
        gq=jnp.concatenate([tile6(g_qa[i]), tile6(g_qb[i])])[None],
        gk=jnp.concatenate([tile6(g_ka[i]), tile6(g_kb[i])])[None],
        b_f=_pad_lanes(b_f[i])[None],
        g_mix=g_mix, w_out=_regroup_rows(w_out[i]).astype(BF16), norm_ffn=norm_ffn[i][None],
        w_r_hi=w_r_hi, w_r_lo=(w_r - w_r_hi.astype(F32)).astype(BF16),
        b_r=_pad_lanes(jnp.concatenate([b_re[i], b_rg[i]]))[None],
        w_eg=w_eg[i].astype(BF16), w_eu=w_eu[i].astype(BF16),
        w_ed=w_ed[i].reshape(-1, w_ed.shape[-1]).astype(BF16),
        w_ple=w_ple[i].astype(BF16), norm_ple=norm_ple[i][None], w_pg=w_pg[i].astype(BF16))


def _token_tile(t, pref):
    return pref if t % pref == 0 else t


def _feature_major(cache):
    d, n, p, h, e = cache.shape
    return jnp.transpose(cache, (0, 1, 3, 4, 2)).reshape(d, n, h * e, p)


def _position_major(rows, heads):
    d, n, _, p = rows.shape
    return jnp.transpose(rows.reshape(d, n, heads, HEAD_DIM, p), (0, 1, 4, 2, 3))


def kernel(x_prompt, x_sample, p_prompt, p_sample, cache_a_k, cache_a_v, cache_a_logf, cache_b_k, cache_b_v, cache_c_k, cache_c_v, norm_mix, w_in, b_f, g_qa, g_ka, g_qb, g_kb, rel_bias, g_mix_out, w_out, norm_ffn, w_router_group, b_router_group, w_router_expert, b_router_expert, w_exp_gate, w_exp_up, w_exp_down, w_ple, norm_ple, w_ple_gate):
    nb, s, d = x_prompt.shape
    ns, t_new, _ = x_sample.shape
    depth = w_in.shape[0]
    past = cache_a_k.shape[2]
    rows_b = cache_b_k.shape[2]
    assert d == D_MODEL and s % PROJ_TILE == 0 and s >= BAND + ATT_TILE and past % ATT_TILE == 0
    tp, ts = nb * s, ns * t_new

    hp = x_prompt.reshape(tp, d)
    hs = x_sample.reshape(ts, d)
    pp = p_prompt.reshape(depth, tp, -1)
    ps = p_sample.reshape(depth, ts, -1)
    caches = tuple(_feature_major(c) for c in (cache_a_k, cache_a_v, cache_b_k, cache_b_v, cache_c_k, cache_c_v))
    lf_rows = jnp.transpose(cache_a_logf, (0, 3, 1, 2))
    rsum = _suffix_sum(lf_rows.reshape(depth * H_A * ns, past)).reshape(depth, H_A, ns, past)
    rsum = jnp.transpose(rsum, (0, 2, 1, 3))

    rows_p = None
    new_s = [[] for _ in range(7)]
    for i in range(depth):
        lw = _layer_weights(i, norm_mix, w_in, b_f, g_qa, g_ka, g_qb, g_kb, g_mix_out, w_out, norm_ffn,
                            w_router_group, b_router_group, w_router_expert, b_router_expert,
                            w_exp_gate, w_exp_up, w_exp_down, w_ple, norm_ple, w_ple_gate)
        tab_p, tab_c, tab_n = _rel_tables(rel_bias[i] * LOG2E, t_new, rows_b)

        pr = _inproj_prompt(hp, lw, i, depth, nb, s, rows_p)
        qat, ka, vat, qb, kb, vb, qc, kc, vc, logf = pr[:10]
        rows_p = pr[10:]
        kaug, qaug = _forget_aug(logf, nb, s)
        oa = _attn_a(qat, ka, vat, kaug, qaug, nb, s)
        ob = _prompt_attention(
            _attn_b_kernel, "attn_b", qb, kb, vb, (tab_p,),
            [pl.BlockSpec(tab_p.shape, lambda b, j: (0, 0, 0, 0))], [], nb, s, ATT_TILE)
        oc = _prompt_attention(
            _attn_c_kernel, "attn_c", qc, kc, vc, (), [],
            [pltpu.VMEM((H_C, C_QTILE, PAIR_W), BF16), pltpu.VMEM((H_C, C_QTILE, LANES), F32),
             pltpu.VMEM((H_C, C_QTILE, PAIR_W), F32)], nb, s, C_QTILE)
        h1, gates, *x_chunks = _post(oa, ob, oc, hp, lw, _token_tile(tp, 512), N_XCHUNKS)
        dispatched = _moe_dispatch(x_chunks, gates)

        sr = _inproj_sample(hs, lw)
        oa, ob, oc = _sample_attention(sr, caches, i, rsum, tab_c, tab_n, ns, t_new)
        h1s, gates, xn = _post(oa, ob, oc, hs, lw, _token_tile(ts, 512))
        hs = _moe(xn, gates, h1s, ps, i, lw, _token_tile(ts, 1024))

        hp = _moe_combine(dispatched, h1, pp, i, lw)
        akf, avf, bkf, bvf, ckf, cvf, logf = sr[9:]
        rows = (akf.reshape(ns, t_new, H_A, HEAD_DIM), avf.reshape(ns, t_new, H_A, HEAD_DIM),
                logf[:, :H_A].reshape(ns, t_new, H_A),
                bkf.reshape(ns, t_new, H_B, HEAD_DIM), bvf.reshape(ns, t_new, H_B, HEAD_DIM),
                ckf.reshape(ns, t_new, H_C, HEAD_DIM), cvf.reshape(ns, t_new, H_C, HEAD_DIM))
        for j in range(7):
            new_s[j].append(rows[j])

    akt, avt, bkt, bvt, ckt, cvt, lft = rows_p
    outs_p = (_position_major(akt, H_A), _position_major(avt, H_A),
              jnp.transpose(lft.reshape(depth, H_A, nb, s), (0, 2, 3, 1)),
              _position_major(bkt, H_B), _position_major(bvt, H_B),
              _position_major(ckt, H_C), _position_major(cvt, H_C))
    outs_s = [jnp.stack(r, axis=0) for r in new_s]
    return (hp.reshape(nb, s, d), hs.reshape(ns, t_new, d), *outs_p, *outs_s)
```

```python
import functools
import math

import jax
import jax.numpy as jnp
from jax import lax
from jax.experimental import pallas as pl
from jax.experimental.pallas import tpu as pltpu
from jax.experimental.pallas import tpu_sc as plsc

F32 = jnp.float32
BF16 = jnp.bfloat16

D_MODEL = 1024
HEAD_DIM = 64
H_A, H_B, H_C = 6, 5, 5
W_A, W_B, W_C = H_A * HEAD_DIM, H_B * HEAD_DIM, H_C * HEAD_DIM
GROUP_W = 384
PAIR_W = 2 * HEAD_DIM
N_PAIRS = GROUP_W // PAIR_W
CHUNK = 64
BAND_CHUNKS = 8
BAND = BAND_CHUNKS * CHUNK
REL_CLIP = 128
N_GROUPS = 4
EXPERTS_PER_GROUP = 4
N_EXPERTS = N_GROUPS * EXPERTS_PER_GROUP
EPS = 1e-6
ATTN_SCALE = HEAD_DIM ** -0.5
LOG2E = math.log2(math.e)
Q_SCALE = ATTN_SCALE * LOG2E
NEG_INF = -1e30
LANES = 128
SUBLANES = 8
VMEM_LIMIT = 48 * 1024 * 1024
EXPERTS_PER_STEP = 4
MOE_VMEM_LIMIT = 58 * 1024 * 1024

ATT_TILE = 256
A_QTILE = 512
C_QTILE = 256
PROJ_TILE = 1024
PROJ_VMEM_LIMIT = 56 * 1024 * 1024
AUG_SLOTS = 8

_NT = (((1,), (1,)), ((), ()))


def _cparams(*sem):
    return pltpu.CompilerParams(dimension_semantics=sem, vmem_limit_bytes=VMEM_LIMIT)


def _dot(a, b):
    return jnp.dot(a, b, preferred_element_type=F32)


def _dot_nt(a, b):
    return lax.dot_general(a, b, _NT, preferred_element_type=F32)


def _split3(x):
    hi = x.astype(BF16)
    r1 = x - hi.astype(F32)
    mid = r1.astype(BF16)
    lo = (r1 - mid.astype(F32)).astype(BF16)
    return hi, mid, lo


def _split2(x):
    hi = x.astype(BF16)
    lo = (x - hi.astype(F32)).astype(BF16)
    return hi, lo


def _lane_iota(n=LANES):
    return lax.broadcasted_iota(jnp.int32, (1, n), 1)


def _half_mask(half):
    lane = _lane_iota()
    return (lane < HEAD_DIM) if half == 0 else (lane >= HEAD_DIM)


def _pair_slice(h):
    return slice((h // 2) * PAIR_W, (h // 2 + 1) * PAIR_W)


def _softplus2(z2):
    return jnp.maximum(z2, 0.0) + jnp.log2(1.0 + jnp.exp2(-jnp.abs(z2)))


def _sigmoid(z):
    return 1.0 / (1.0 + jnp.exp(-z))


def _rms(x, g):
    return x * lax.rsqrt(jnp.mean(x * x, axis=-1, keepdims=True) + EPS) * g


def _pair_rms(z, gain):
    first = _half_mask(0)
    outs = []
    for j in range(N_PAIRS):
        blk = z[:, j * PAIR_W:(j + 1) * PAIR_W]
        sq = blk * blk
        lo = jnp.sum(jnp.where(first, sq, 0.0), axis=-1, keepdims=True)
        hi = jnp.sum(jnp.where(first, 0.0, sq), axis=-1, keepdims=True)
        ms = jnp.where(first, lo, hi) * (1.0 / HEAD_DIM)
        outs.append(blk * lax.rsqrt(ms + EPS) * gain[:, j * PAIR_W:(j + 1) * PAIR_W])
    return jnp.concatenate(outs, axis=-1)


def _projector(x_ref, gn_ref, w_ref):
    xb = _rms(x_ref[...], gn_ref[...]).astype(BF16)
    return lambda g, width=GROUP_W: _dot_nt(xb, w_ref[g * GROUP_W:g * GROUP_W + width, :])


def _log_forget(zf, bf_ref):
    zf = zf + bf_ref[...]
    return jnp.minimum(zf, 0.0) - jnp.log(1.0 + jnp.exp(-jnp.abs(zf)))


def _inproj_sample_kernel(x_ref, gn_ref, w_ref, gq_ref, gk_ref, bf_ref,
                          qa_ref, qb_ref, qc_ref, ka_ref, kb_ref, kc_ref, va_ref, vb_ref, vc_ref,
                          akf_ref, avf_ref, bkf_ref, bvf_ref, ckf_ref, cvf_ref, logf_ref):
    proj = _projector(x_ref, gn_ref, w_ref)
    qa_ref[...] = (_pair_rms(proj(0), gq_ref[:, :GROUP_W]) * Q_SCALE).astype(BF16)
    qb_ref[...] = (_pair_rms(proj(1), gq_ref[:, GROUP_W:]) * Q_SCALE).astype(BF16)
    qc_ref[...] = (proj(2) * Q_SCALE).astype(BF16)
    for g, norm, bf_ref_, f_ref, width in (
            (3, gk_ref[:, :GROUP_W], ka_ref, akf_ref, W_A), (4, gk_ref[:, GROUP_W:], kb_ref, bkf_ref, W_B),
            (5, None, kc_ref, ckf_ref, W_C), (6, None, va_ref, avf_ref, W_A),
            (7, None, vb_ref, bvf_ref, W_B), (8, None, vc_ref, cvf_ref, W_C)):
        z = proj(g)
        if norm is not None:
            z = _pair_rms(z, norm)
        f_ref[...] = z[:, :width]
        bf_ref_[...] = z.astype(BF16)
    logf_ref[...] = _log_forget(proj(9, LANES), bf_ref)


def _inproj_sample(x, lw):
    t = x.shape[0]
    row = lambda w: pl.BlockSpec((t, w), lambda i: (0, 0))
    full = lambda a: pl.BlockSpec(a.shape, lambda i: (0,) * a.ndim)
    widths = [GROUP_W] * 9 + [W_A, W_A, W_B, W_B, W_C, W_C, LANES]
    out_shape = [jax.ShapeDtypeStruct((t, w), BF16 if j < 9 else F32) for j, w in enumerate(widths)]
    ins = (x, lw['norm_mix'], lw['w_in'], lw['gq'], lw['gk'], lw['b_f'])
    return pl.pallas_call(
        _inproj_sample_kernel, grid=(1,),
        in_specs=[row(D_MODEL)] + [full(a) for a in ins[1:]],
        out_specs=[row(w) for w in widths], out_shape=out_shape,
        compiler_params=_cparams("arbitrary"), name="inproj_sample")(*ins)


N_PROMPT_ROWS = 7


def _inproj_prompt_kernel(*refs, n_alias):
    x_ref, gn_ref, w_ref, gq_ref, gk_ref, bf_ref = refs[:6]
    (qat_ref, ka_ref, vat_ref, qb_ref, kb_ref, vb_ref, qc_ref, kc_ref, vc_ref, logf_ref,
     akt_ref, avt_ref, bkt_ref, bvt_ref, ckt_ref, cvt_ref, lft_ref) = refs[6 + n_alias:]
    proj = _projector(x_ref, gn_ref, w_ref)
    qat_ref[...] = jnp.transpose(_pair_rms(proj(0), gq_ref[:, :GROUP_W]) * Q_SCALE).astype(BF16)
    qb_ref[...] = (_pair_rms(proj(1), gq_ref[:, GROUP_W:]) * Q_SCALE).astype(BF16)
    qc_ref[...] = (proj(2) * Q_SCALE).astype(BF16)
    ka = _pair_rms(proj(3), gk_ref[:, :GROUP_W])
    ka_ref[...] = ka.astype(BF16)
    akt_ref[0, 0] = jnp.transpose(ka)
    kb = _pair_rms(proj(4), gk_ref[:, GROUP_W:])
    kb_ref[...] = kb.astype(BF16)
    kc = proj(5)
    kc_ref[...] = kc.astype(BF16)
    ckt_ref[0, 0] = jnp.transpose(kc)[:W_C]
    vat = jnp.transpose(proj(6))
    avt_ref[0, 0] = vat
    for j in range(vat_ref.shape[1]):
        vat_ref[0, j] = vat[:, j * ATT_TILE:(j + 1) * ATT_TILE].astype(BF16)
    vb = proj(7)
    vb_ref[...] = vb.astype(BF16)
    vc = proj(8)
    vc_ref[...] = vc.astype(BF16)
    cvt_ref[0, 0] = jnp.transpose(vc)[:W_C]

    @pl.when(pl.program_id(1) == pl.num_programs(1) - 1)
    def _():
        keep = kb.shape[0] - BAND
        bkt_ref[0, 0] = jnp.transpose(kb)[:W_B, keep:]
        bvt_ref[0, 0] = jnp.transpose(vb)[:W_B, keep:]

    lf = _log_forget(proj(9, LANES), bf_ref)
    logf_ref[...] = lf
    lft_ref[0] = jnp.transpose(lf)[:H_A]


def _inproj_prompt(x, lw, layer, depth, nb, s, prev_rows):
    tm = PROJ_TILE
    nt = s // tm
    t = nb * s
    per = tm // ATT_TILE
    row = lambda w: pl.BlockSpec((tm, w), lambda b, i: (b * nt + i, 0))
    full = lambda a: pl.BlockSpec(a.shape, lambda b, i: (0,) * a.ndim, pipeline_mode=pl.Buffered(1))
    feat = lambda w: pl.BlockSpec((1, 1, w, tm), lambda b, i: (layer, b, 0, i))
    band = lambda w: pl.BlockSpec((1, 1, w, BAND), lambda b, i: (layer, b, 0, 0))
    tok = jax.ShapeDtypeStruct((t, GROUP_W), BF16)
    out_shape = [
        jax.ShapeDtypeStruct((GROUP_W, t), BF16), tok,
        jax.ShapeDtypeStruct((nb, s // ATT_TILE, GROUP_W, ATT_TILE), BF16),
        tok, tok, tok, tok, tok, tok, jax.ShapeDtypeStruct((t, LANES), F32),
        jax.ShapeDtypeStruct((depth, nb, W_A, s), F32), jax.ShapeDtypeStruct((depth, nb, W_A, s), F32),
        jax.ShapeDtypeStruct((depth, nb, W_B, BAND), F32), jax.ShapeDtypeStruct((depth, nb, W_B, BAND), F32),
        jax.ShapeDtypeStruct((depth, nb, W_C, s), F32), jax.ShapeDtypeStruct((depth, nb, W_C, s), F32),
        jax.ShapeDtypeStruct((depth, H_A, t), F32)]
    out_specs = [
        pl.BlockSpec((GROUP_W, tm), lambda b, i: (0, b * nt + i)), row(GROUP_W),
        pl.BlockSpec((1, per, GROUP_W, ATT_TILE), lambda b, i: (b, i, 0, 0)),
        row(GROUP_W), row(GROUP_W), row(GROUP_W), row(GROUP_W), row(GROUP_W), row(GROUP_W), row(LANES),
        feat(W_A), feat(W_A), band(W_B), band(W_B), feat(W_C), feat(W_C),
        pl.BlockSpec((1, H_A, tm), lambda b, i: (layer, 0, b * nt + i))]
    ins = [x, lw['norm_mix'], lw['w_in'], lw['gq'], lw['gk'], lw['b_f']]
    in_specs = [row(D_MODEL)] + [full(a) for a in ins[1:]]
    aliases = {}
    n_alias = N_PROMPT_ROWS
    first_row_out = len(out_shape) - N_PROMPT_ROWS
    if prev_rows is None:
        prev_rows = [jnp.zeros(o.shape, o.dtype) for o in out_shape[first_row_out:]]
    for j, a in enumerate(prev_rows):
        aliases[len(ins)] = first_row_out + j
        ins.append(a)
        in_specs.append(pl.BlockSpec(memory_space=pl.ANY))
    return pl.pallas_call(
        functools.partial(_inproj_prompt_kernel, n_alias=n_alias), grid=(nb, nt),
        in_specs=in_specs, out_specs=out_specs, out_shape=out_shape,
        input_output_aliases=aliases,
        compiler_params=pltpu.CompilerParams(dimension_semantics=("parallel", "arbitrary"),
                                             vmem_limit_bytes=PROJ_VMEM_LIMIT),
        name="inproj_prompt")(*ins)


def _forget_aug_kernel(logf_ref, kaug_ref, qaug_ref, *, nblk):
    tb = ATT_TILE
    r = lax.broadcasted_iota(jnp.int32, (tb, tb), 0)
    c = lax.broadcasted_iota(jnp.int32, (tb, tb), 1)
    tri = jnp.where(c <= r, 1.0, 0.0).astype(BF16)
    src = lax.broadcasted_iota(jnp.int32, (LANES, LANES), 0)
    dst = lax.broadcasted_iota(jnp.int32, (LANES, LANES), 1)

    def place(slot):
        return jnp.where((dst == AUG_SLOTS * src + slot) & (src < H_A), 1.0, 0.0).astype(BF16)

    lane = _lane_iota()
    slot = jnp.bitwise_and(lane, AUG_SLOTS - 1)
    used = lane < AUG_SLOTS * H_A
    key_const = jnp.where(used & (slot >= 3) & (slot < 6), 1.0, 0.0)
    qry_const = jnp.where(used & (slot < 3), -1.0, 0.0)
    carry = jnp.zeros((1, LANES), F32)
    for blk in range(nblk):
        rows = slice(blk * tb, (blk + 1) * tb)
        hi, mid, lo = _split3(logf_ref[rows, :])
        cs = (_dot(tri, hi) + _dot(tri, mid) + _dot(tri, lo)) + carry
        carry = cs[tb - 1:tb, :]
        hi, mid, lo = _split3(cs * LOG2E)
        kaug = _dot(hi, place(0)) + _dot(mid, place(1)) + _dot(lo, place(2)) + key_const
        kaug_ref[rows, :] = kaug.astype(BF16)
        qaug = _dot(hi, place(3)) + _dot(mid, place(4)) + _dot(lo, place(5)) + qry_const
        qaug_ref[:, rows] = jnp.transpose(qaug).astype(BF16)


def _forget_aug(logf, nb, s):
    return pl.pallas_call(
        functools.partial(_forget_aug_kernel, nblk=s // ATT_TILE), grid=(nb,),
        in_specs=[pl.BlockSpec((s, LANES), lambda b: (b, 0))],
        out_specs=[pl.BlockSpec((s, LANES), lambda b: (b, 0)), pl.BlockSpec((LANES, s), lambda b: (0, b))],
        out_shape=[jax.ShapeDtypeStruct((nb * s, LANES), BF16), jax.ShapeDtypeStruct((LANES, nb * s), BF16)],
        compiler_params=_cparams("parallel"), name="forget_aug")(logf)


def _suffix_sum_kernel(x_ref, o_ref):
    p = x_ref.shape[1]
    j = lax.broadcasted_iota(jnp.int32, (p, p), 0)
    s = lax.broadcasted_iota(jnp.int32, (p, p), 1)
    tri = jnp.where(j > s, 1.0, 0.0).astype(BF16)
    hi, mid, lo = _split3(x_ref[...])
    o_ref[...] = _dot(hi, tri) + _dot(mid, tri) + _dot(lo, tri)


def _suffix_sum(x):
    return pl.pallas_call(
        _suffix_sum_kernel, out_shape=jax.ShapeDtypeStruct(x.shape, F32),
        compiler_params=pltpu.CompilerParams(vmem_limit_bytes=VMEM_LIMIT), name="suffix_sum")(x)


def _attn_a_kernel(qt_ref, k_ref, vt_ref, kaug_ref, qaug_ref, o_ref, qp_ref, m_ref, l_ref, acc_ref):
    tq = qt_ref.shape[1]
    tk = ATT_TILE
    qi = pl.program_id(1)
    row = lax.broadcasted_iota(jnp.int32, (PAIR_W, 1), 0)
    aug = qaug_ref[...]
    for h in range(H_A):
        q2 = qt_ref[_pair_slice(h), :]
        half = (row < HEAD_DIM) if h % 2 == 0 else (row >= HEAD_DIM)
        qp_ref[h, :PAIR_W, :] = jnp.where(half, q2, jnp.zeros_like(q2))
        mine = (row >= AUG_SLOTS * h) & (row < AUG_SLOTS * (h + 1))
        qp_ref[h, PAIR_W:, :] = jnp.where(mine, aug, jnp.zeros_like(aug))
    m_ref[...] = jnp.full(m_ref.shape, NEG_INF, F32)
    l_ref[...] = jnp.zeros(l_ref.shape, F32)
    acc_ref[...] = jnp.zeros(acc_ref.shape, F32)

    def step(kt, key_offset):
        ks = pl.multiple_of(kt * tk, tk)
        ka = kaug_ref[pl.ds(ks, tk), :]
        tile_scores = [_dot(jnp.concatenate([k_ref[pl.ds(ks, tk), _pair_slice(h)], ka], axis=1), qp_ref[h])
                       for h in range(H_A)]
        probs, alphas = [], []
        for h in range(H_A):
            s = tile_scores[h]
            if key_offset is not None:
                key = lax.broadcasted_iota(jnp.int32, (tk, tq), 0)
                qry = lax.broadcasted_iota(jnp.int32, (tk, tq), 1)
                s = jnp.where(key + key_offset <= qry, s, NEG_INF)
            m_prev = m_ref[h]
            m_new = jnp.maximum(m_prev, jnp.max(s, axis=0, keepdims=True))
            alpha = jnp.exp2(m_prev - m_new)
            p = jnp.exp2(s - m_new)
            l_ref[h] = alpha * l_ref[h] + jnp.sum(p, axis=0, keepdims=True)
            m_ref[h] = m_new
            alphas.append(alpha)
            probs.append(p.astype(BF16))
        for h in range(H_A):
            acc_ref[h] = alphas[h] * acc_ref[h] + _dot(vt_ref[0, kt, _pair_slice(h), :], probs[h])

    def body(kt, carry):
        step(kt, None)
        return carry

    per = tq // tk
    lax.fori_loop(0, qi * per, body, 0)
    for j in range(per):
        step(qi * per + j, j * tk)
    for pair in range(N_PAIRS):
        h = 2 * pair
        out_t = jnp.where(row < HEAD_DIM, acc_ref[h] * (1.0 / l_ref[h]), acc_ref[h + 1] * (1.0 / l_ref[h + 1]))
        o_ref[:, pair * PAIR_W:(pair + 1) * PAIR_W] = jnp.transpose(out_t)


def _attn_a(qat, ka, vat, kaug, qaug, nb, s):
    tq = A_QTILE
    nq = s // tq
    return pl.pallas_call(
        _attn_a_kernel, grid=(nb, nq),
        in_specs=[pl.BlockSpec((GROUP_W, tq), lambda b, i: (0, b * nq + i)),
                  pl.BlockSpec((s, GROUP_W), lambda b, i: (b, 0)),
                  pl.BlockSpec((1, s // ATT_TILE, GROUP_W, ATT_TILE), lambda b, i: (b, 0, 0, 0)),
                  pl.BlockSpec((s, LANES), lambda b, i: (b, 0)),
                  pl.BlockSpec((LANES, tq), lambda b, i: (0, b * nq + i))],
        out_specs=pl.BlockSpec((tq, GROUP_W), lambda b, i: (b * nq + i, 0)),
        out_shape=jax.ShapeDtypeStruct((nb * s, GROUP_W), F32),
        scratch_shapes=[pltpu.VMEM((H_A, 2 * PAIR_W, tq), BF16), pltpu.VMEM((H_A, 1, tq), F32),
                        pltpu.VMEM((H_A, 1, tq), F32), pltpu.VMEM((H_A, PAIR_W, tq), F32)],
        compiler_params=_cparams("parallel", "arbitrary"), name="attn_a")(qat, ka, vat, kaug, qaug)


def _masked_q(q_ref, qm_ref, n_heads):
    for h in range(n_heads):
        q2 = q_ref[:, _pair_slice(h)]
        qm_ref[h] = jnp.where(_half_mask(h % 2), q2, jnp.zeros_like(q2))


def _attn_b_kernel(q_ref, k_ref, v_ref, tab_ref, o_ref):
    tq = ATT_TILE
    win = tq + BAND
    i = pl.program_id(1)
    var = jnp.minimum(i, BAND // tq)
    ws = pl.multiple_of(jnp.maximum(i * tq - BAND, 0), tq)
    heads = range(H_B)
    scores = []
    for h in heads:
        q2 = q_ref[:, _pair_slice(h)]
        qm = jnp.where(_half_mask(h % 2), q2, jnp.zeros_like(q2))
        scores.append(_dot_nt(qm, k_ref[pl.ds(ws, win), _pair_slice(h)]) + tab_ref[var, h])
    probs, norms = [], []
    for h in heads:
        p = jnp.exp2(scores[h] - jnp.max(scores[h], axis=-1, keepdims=True))
        norms.append(1.0 / jnp.sum(p, axis=-1, keepdims=True))
        probs.append(p.astype(BF16))
    res = [_dot(probs[h], v_ref[pl.ds(ws, win), _pair_slice(h)]) * norms[h] for h in heads]
    res.append(jnp.zeros((tq, PAIR_W), F32))
    first = _half_mask(0)
    for pair in range(N_PAIRS):
        o_ref[:, pair * PAIR_W:(pair + 1) * PAIR_W] = jnp.where(first, res[2 * pair], res[2 * pair + 1])


def _suffix_tri(n):
    j = lax.broadcasted_iota(jnp.int32, (2 * n, n), 0)
    s = lax.broadcasted_iota(jnp.int32, (2 * n, n), 1)
    return jnp.where(jnp.where(j >= n, j - n, j) >= s, 1.0, 0.0).astype(BF16)


def _stick_scores(z, tri, seen):
    sp = _softplus2(z)
    if seen is not None:
        sp = jnp.where(seen, sp, 0.0)
    hi, lo = _split2(sp)
    n = sp.shape[1]
    if n % LANES == 0:
        return _dot(jnp.concatenate([hi, lo], axis=1), tri)
    return _dot(hi, tri[:n]) + _dot(lo, tri[:n])


def _stick_weights(z, s_in, carry, seen):
    a = jnp.exp2(z - (s_in + carry))
    if seen is not None:
        a = jnp.where(seen, a, 0.0)
    return a.astype(BF16)


def _attn_c_kernel(q_ref, k_ref, v_ref, o_ref, qm_ref, carry_ref, acc_ref):
    tq = q_ref.shape[0]
    tk = ATT_TILE
    qi = pl.program_id(1)
    _masked_q(q_ref, qm_ref, H_C)
    carry_ref[...] = jnp.zeros(carry_ref.shape, F32)
    acc_ref[...] = jnp.zeros(acc_ref.shape, F32)
    heads = range(H_C)

    def step(kt, key_offset):
        ks = pl.multiple_of(kt * tk, tk)
        zs = [_dot_nt(qm_ref[h], k_ref[pl.ds(ks, tk), _pair_slice(h)]) for h in heads]
        tri = _suffix_tri(tk)
        seen = None
        if key_offset is not None:
            r = lax.broadcasted_iota(jnp.int32, (tq, tk), 0)
            c = lax.broadcasted_iota(jnp.int32, (tq, tk), 1)
            seen = c + key_offset < r
        sums = [_stick_scores(zs[h], tri, seen) for h in heads]
        weights = []
        for h in heads:
            carry = carry_ref[h][:, 0:1]
            weights.append(_stick_weights(zs[h], sums[h], carry, seen))
            carry_ref[h] = jnp.broadcast_to(carry + sums[h][:, 0:1], (tq, LANES))
        for h in heads:
            acc_ref[h] += _dot(weights[h], v_ref[pl.ds(ks, tk), _pair_slice(h)])

    per = tq // tk

    def body(j, c):
        step(qi * per - 1 - j, None)
        return c

    for j in reversed(range(per)):
        step(qi * per + j, j * tk)
    lax.fori_loop(0, qi * per, body, 0)
    first = _half_mask(0)
    for pair in range(N_PAIRS):
        h = 2 * pair
        second = acc_ref[h + 1] if h + 1 < H_C else jnp.zeros((tq, PAIR_W), F32)
        o_ref[:, pair * PAIR_W:(pair + 1) * PAIR_W] = jnp.where(first, acc_ref[h], second)


def _prompt_attention(kernel, name, q, k, v, extra, extra_specs, scratch, nb, s, tq):
    nq = s // tq
    qspec = pl.BlockSpec((tq, GROUP_W), lambda b, i: (b * nq + i, 0))
    kvspec = pl.BlockSpec((s, GROUP_W), lambda b, i: (b, 0))
    return pl.pallas_call(
        kernel, grid=(nb, nq),
        in_specs=[qspec, kvspec, kvspec] + extra_specs,
        out_specs=qspec, out_shape=jax.ShapeDtypeStruct((nb * s, GROUP_W), F32),
        scratch_shapes=scratch,
        compiler_params=_cparams("parallel", "arbitrary"), name=name)(q, k, v, *extra)


def _sample_heads(n_heads, q_ref, width):
    out = []
    for h in range(n_heads):
        lo = (h // 2) * PAIR_W
        span = slice(lo, min(lo + PAIR_W, width))
        q2 = q_ref[:, span]
        if span.stop - span.start == PAIR_W:
            q2 = jnp.where(_half_mask(h % 2), q2, jnp.zeros_like(q2))
        out.append((h, span, q2))
    return out


def _store_heads(o_ref, res, n_heads):
    for pair in range(N_PAIRS):
        lo = pair * PAIR_W
        h = 2 * pair
        if h + 1 < n_heads:
            o_ref[:, lo:lo + PAIR_W] = jnp.where(_half_mask(0), res[h], res[h + 1])
        else:
            o_ref[:, lo:lo + HEAD_DIM] = res[h]
            o_ref[:, lo + HEAD_DIM:lo + PAIR_W] = jnp.zeros_like(res[h])


def _joint_softmax_pv(s_c, s_n, vt_c, v_n):
    m = jnp.maximum(jnp.max(s_c, axis=-1, keepdims=True), jnp.max(s_n, axis=-1, keepdims=True))
    p_c = jnp.exp2(s_c - m)
    p_n = jnp.exp2(s_n - m)
    l = jnp.sum(p_c, axis=-1, keepdims=True) + jnp.sum(p_n, axis=-1, keepdims=True)
    return (_dot_nt(p_c.astype(BF16), vt_c) + _dot(p_n.astype(BF16), v_n)) / l


def _sample_attn_kernel(qa_ref, qb_ref, qc_ref, kan_ref, kbn_ref, kcn_ref, van_ref, vbn_ref, vcn_ref,
                        cak_ref, cav_ref, cbk_ref, cbv_ref, cck_ref, ccv_ref,
                        rsum_ref, logf_ref, tabc_ref, tabn_ref,
                        oa_ref, ob_ref, oc_ref):
    t = qa_ref.shape[0]
    past = cak_ref.shape[3]
    r = lax.broadcasted_iota(jnp.int32, (t, t), 0)
    c = lax.broadcasted_iota(jnp.int32, (t, t), 1)

    ltri = jnp.where(c <= r, 1.0, 0.0).astype(BF16)
    hi, mid, lo = _split3(logf_ref[...])
    pcol = (_dot(ltri, hi) + _dot(ltri, mid) + _dot(ltri, lo)) * LOG2E
    prow = jnp.transpose(jnp.concatenate([pcol, jnp.zeros((LANES - t, LANES), F32)], axis=0))
    heads = _sample_heads(H_A, qa_ref, W_A)
    kts = {sp.start: cak_ref[0, 0, sp, :].astype(BF16) for _, sp, _ in heads}
    vts = {sp.start: cav_ref[0, 0, sp, :].astype(BF16) for _, sp, _ in heads}
    s_c = [_dot(qm, kts[sp.start]) + (pcol[:, h:h + 1] + rsum_ref[0, 0, h:h + 1, :] * LOG2E)
           for h, sp, qm in heads]
    s_n = [jnp.where(c <= r, _dot_nt(qm, kan_ref[:, sp]) + (pcol[:, h:h + 1] - prow[h:h + 1, 0:t]), NEG_INF)
           for h, sp, qm in heads]
    _store_heads(oa_ref, [_joint_softmax_pv(s_c[h], s_n[h], vts[sp.start], van_ref[:, sp])
                          for h, sp, _ in heads], H_A)

    heads = _sample_heads(H_B, qb_ref, W_B)
    kts = {sp.start: cbk_ref[0, 0, sp, :].astype(BF16) for _, sp, _ in heads}
    vts = {sp.start: cbv_ref[0, 0, sp, :].astype(BF16) for _, sp, _ in heads}
    s_c = [_dot(qm, kts[sp.start]) + tabc_ref[h] for h, sp, qm in heads]
    s_n = [_dot_nt(qm, kbn_ref[:, sp]) + tabn_ref[h] for h, sp, qm in heads]
    _store_heads(ob_ref, [_joint_softmax_pv(s_c[h], s_n[h], vts[sp.start], vbn_ref[:, sp])
                          for h, sp, _ in heads], H_B)

    heads = _sample_heads(H_C, qc_ref, W_C)
    nblk = past // ATT_TILE
    tri_n = _suffix_tri(t)
    tri_c = _suffix_tri(ATT_TILE)
    seen = c < r
    kts = {sp.start: cck_ref[0, 0, sp, :].astype(BF16) for _, sp, _ in heads}
    vts = {sp.start: ccv_ref[0, 0, sp, :].astype(BF16) for _, sp, _ in heads}
    z_n = [_dot_nt(qm, kcn_ref[:, sp]) for _, sp, qm in heads]
    z_c = [_dot(qm, kts[sp.start]) for _, sp, qm in heads]
    sum_n = [_stick_scores(z, tri_n, seen) for z in z_n]
    sum_c = [[_stick_scores(z[:, b * ATT_TILE:(b + 1) * ATT_TILE], tri_c, None) for b in range(nblk)]
             for z in z_c]
    res = []
    for h, sp, _ in heads:
        acc = _dot(_stick_weights(z_n[h], sum_n[h], 0.0, seen), vcn_ref[:, sp])
        carry = sum_n[h][:, 0:1]
        for b in reversed(range(nblk)):
            cols = slice(b * ATT_TILE, (b + 1) * ATT_TILE)
            a = _stick_weights(z_c[h][:, cols], sum_c[h][b], carry, None)
            acc = acc + _dot_nt(a, vts[sp.start][:, cols])
            carry = carry + sum_c[h][b][:, 0:1]
        res.append(acc)
    _store_heads(oc_ref, res, H_C)


def _sample_attention(proj, caches, layer, rsum, tabc, tabn, nb, t):
    qa, qb, qc, ka, kb, kc, va, vb, vc = proj[:9]
    logf = proj[15]
    new = pl.BlockSpec((t, GROUP_W), lambda b: (b, 0))
    cache = lambda a: pl.BlockSpec((1, 1) + a.shape[2:], lambda b: (layer, b, 0, 0))
    full = lambda a: pl.BlockSpec(a.shape, lambda b: (0,) * a.ndim)
    return pl.pallas_call(
        _sample_attn_kernel, grid=(nb,),
        in_specs=[new] * 9 + [cache(a) for a in caches] + [
            cache(rsum), pl.BlockSpec((t, LANES), lambda b: (b, 0)), full(tabc), full(tabn)],
        out_specs=[new] * 3,
        out_shape=[jax.ShapeDtypeStruct((nb * t, GROUP_W), F32)] * 3,
        compiler_params=_cparams("parallel"), name="sample_attn")(
            qa, qb, qc, ka, kb, kc, va, vb, vc, *caches, rsum, logf, tabc, tabn)


def _masked_max(x, mask):
    return jnp.max(jnp.where(mask, x, -jnp.inf), axis=-1, keepdims=True)


def _first_lane(mask, lane):
    return jnp.min(jnp.where(mask, lane, float(LANES)), axis=-1, keepdims=True)


def _route(logits):
    lane = _lane_iota().astype(F32)
    is_g = (lane >= N_EXPERTS) & (lane < N_EXPERTS + N_GROUPS)
    gmax = _masked_max(logits, is_g)
    p_g = 1.0 / jnp.sum(jnp.where(is_g, jnp.exp(logits - gmax), 0.0), axis=-1, keepdims=True)
    g_sel = _first_lane(is_g & (logits == gmax), lane) - N_EXPERTS
    lo = g_sel * EXPERTS_PER_GROUP
    in_g = (lane >= lo) & (lane < lo + EXPERTS_PER_GROUP)
    l1 = _masked_max(logits, in_g)
    i1 = _first_lane(in_g & (logits == l1), lane)
    rest = in_g & (lane != i1)
    l2 = _masked_max(logits, rest)
    i2 = _first_lane(rest & (logits == l2), lane)
    e2 = jnp.exp(l2 - l1)
    w1 = p_g / (1.0 + e2)
    w2 = p_g * e2 / (1.0 + e2)
    gates = jnp.where(lane == i1, w1, jnp.where(lane == i2, w2, 0.0))
    return jnp.where(lane == g_sel + N_EXPERTS, 1.0, gates)


def _post_kernel(oa_ref, ob_ref, oc_ref, h_ref, gmix_ref, wout_ref, nffn_ref, wrh_ref, wrl_ref, br_ref,
                 h1_ref, gates_ref, *x_refs, packed):
    def gnorm(o, width):
        ms = jnp.sum(o * o, axis=-1, keepdims=True) * (1.0 / width)
        return o * lax.rsqrt(ms + EPS)

    h1 = h_ref[...]
    for g, (o_ref, width) in enumerate(((oa_ref, W_A), (ob_ref, W_B), (oc_ref, W_C))):
        merged = (gnorm(o_ref[...], width) * gmix_ref[g]).astype(BF16)
        h1 = h1 + _dot(merged, wout_ref[g])
    h1_ref[...] = h1
    xf = _rms(h1, nffn_ref[...])
    hi, lo = _split2(xf)
    if packed:
        bits = pltpu.bitcast(hi.astype(F32), jnp.uint32)
        for c, ref in enumerate(x_refs):
            a = bits[:, (2 * c) * SC_ROW_WORDS:(2 * c + 1) * SC_ROW_WORDS]
            b = bits[:, (2 * c + 1) * SC_ROW_WORDS:(2 * c + 2) * SC_ROW_WORDS]
            ref[...] = jnp.bitwise_or(a, jnp.right_shift(b, jnp.uint32(16)))
    else:
        x_refs[0][...] = hi
    logits = _dot(hi, wrh_ref[...]) + _dot(lo, wrh_ref[...]) + _dot(hi, wrl_ref[...]) + br_ref[...]
    gates_ref[...] = _route(logits)


def _post(oa, ob, oc, h, lw, tm, n_chunks=0):
    t = h.shape[0]
    row = lambda w: pl.BlockSpec((tm, w), lambda i: (i, 0))
    full = lambda a: pl.BlockSpec(a.shape, lambda i: (0,) * a.ndim)
    ws = (lw['g_mix'], lw['w_out'], lw['norm_ffn'], lw['w_r_hi'], lw['w_r_lo'], lw['b_r'])
    if n_chunks:
        x_specs = [row(SC_ROW_WORDS)] * n_chunks
        x_shapes = [jax.ShapeDtypeStruct((t, SC_ROW_WORDS), jnp.uint32)] * n_chunks
    else:
        x_specs, x_shapes = [row(D_MODEL)], [jax.ShapeDtypeStruct((t, D_MODEL), BF16)]
    return pl.pallas_call(
        functools.partial(_post_kernel, packed=bool(n_chunks)), grid=(t // tm,),
        in_specs=[row(GROUP_W)] * 3 + [row(D_MODEL)] + [full(a) for a in ws],
        out_specs=[row(D_MODEL), row(LANES)] + x_specs,
        out_shape=[jax.ShapeDtypeStruct((t, D_MODEL), F32), jax.ShapeDtypeStruct((t, LANES), F32)] + x_shapes,
        compiler_params=_cparams("parallel"), name="post")(oa, ob, oc, h, *ws)


def _moe_kernel(xn_ref, gates_ref, wg_ref, wu_ref, wd_ref, h1_ref, p_ref, wple_ref, nple_ref, wpg_ref,
                out_ref, acc_ref):
    step = pl.program_id(1)

    @pl.when(step == 0)
    def _():
        acc_ref[...] = jnp.zeros_like(acc_ref)

    x = xn_ref[...]
    gates = gates_ref[...]
    acts = []
    for k in range(EXPERTS_PER_STEP):
        e = step * EXPERTS_PER_STEP + k
        g = _dot(x, wg_ref[k])
        u = _dot(x, wu_ref[k])
        gate = jnp.sum(jnp.where(_lane_iota() == e, gates, 0.0), axis=-1, keepdims=True)
        acts.append(((g * _sigmoid(g)) * u * gate).astype(BF16))
    acc_ref[...] += _dot(jnp.concatenate(acts, axis=1), wd_ref[...])

    @pl.when(step == pl.num_programs(1) - 1)
    def _():
        h2 = h1_ref[...] + acc_ref[...]
        gate_ple = _sigmoid(_dot(_rms(h2, nple_ref[...]).astype(BF16), wpg_ref[...]))
        out_ref[...] = h2 + _dot(p_ref[0].astype(BF16), wple_ref[...]) * gate_ple


def _moe(xn, gates, h1, p_all, layer, lw, tm):
    t = xn.shape[0]
    row = lambda w: pl.BlockSpec((tm, w), lambda i, e: (i, 0))
    once = pl.Buffered(1)
    full = lambda a: pl.BlockSpec(a.shape, lambda i, e: (0,) * a.ndim, pipeline_mode=once)
    exp = lambda a: pl.BlockSpec((EXPERTS_PER_STEP,) + a.shape[1:], lambda i, e: (e, 0, 0))
    d_hidden = lw['w_ed'].shape[0] // N_EXPERTS
    return pl.pallas_call(
        _moe_kernel, grid=(t // tm, N_EXPERTS // EXPERTS_PER_STEP),
        in_specs=[row(D_MODEL), row(LANES), exp(lw['w_eg']), exp(lw['w_eu']),
                  pl.BlockSpec((EXPERTS_PER_STEP * d_hidden, D_MODEL), lambda i, e: (e, 0)),
                  pl.BlockSpec((tm, D_MODEL), lambda i, e: (i, 0), pipeline_mode=once),
                  pl.BlockSpec((1, tm, p_all.shape[2]), lambda i, e: (layer, i, 0), pipeline_mode=once),
                  full(lw['w_ple']), full(lw['norm_ple']), full(lw['w_pg'])],
        out_specs=row(D_MODEL), out_shape=jax.ShapeDtypeStruct((t, D_MODEL), F32),
        scratch_shapes=[pltpu.VMEM((tm, D_MODEL), F32)],
        compiler_params=pltpu.CompilerParams(dimension_semantics=("parallel", "arbitrary"),
                                             vmem_limit_bytes=MOE_VMEM_LIMIT), name="moe")(
            xn, gates, lw['w_eg'], lw['w_eu'], lw['w_ed'], h1, p_all, lw['w_ple'], lw['norm_ple'], lw['w_pg'])


SORT_TILE = 512
SC_WINDOW = 128
SC_ROW_WORDS = 256
N_CHUNKS = D_MODEL // SC_ROW_WORDS
N_XCHUNKS = N_CHUNKS // 2


def _plan_kernel(gates_ref, dest_ref, total_ref, *, tb):
    lane = _lane_iota()
    is_g = (lane >= N_EXPERTS) & (lane < N_EXPERTS + N_GROUPS)
    total = jnp.sum(jnp.where(is_g, gates_ref[...], 0.0), axis=0, keepdims=True)
    padded = jnp.floor((total + (SORT_TILE - 1)) * (1.0 / SORT_TILE)) * SORT_TILE
    base = jnp.zeros((1, LANES), F32)
    for g in range(1, N_GROUPS):
        before = jnp.sum(jnp.where(lane < N_EXPERTS + g, padded, 0.0), axis=-1, keepdims=True)
        base = jnp.where(lane == N_EXPERTS + g, before, base)
    r = lax.broadcasted_iota(jnp.int32, (tb, tb), 0)
    c = lax.broadcasted_iota(jnp.int32, (tb, tb), 1)
    ltri = jnp.where(c < r, 1.0, 0.0).astype(BF16)
    carry = jnp.zeros((1, LANES), F32)
    for blk in range(gates_ref.shape[0] // tb):
        rows = slice(blk * tb, (blk + 1) * tb)
        member = jnp.where(is_g, gates_ref[rows, :], 0.0)
        rank = _dot(ltri, member.astype(BF16)) + carry
        carry = carry + jnp.sum(member, axis=0, keepdims=True)
        dest = jnp.sum(member * (base + rank), axis=-1, keepdims=True)
        dest_ref[rows, :] = jnp.broadcast_to(dest, (tb, LANES))
    total_ref[...] = jnp.broadcast_to(total, total_ref.shape)


def _plan(gates):
    t = gates.shape[0]
    return pl.pallas_call(
        functools.partial(_plan_kernel, tb=SORT_TILE),
        out_shape=[jax.ShapeDtypeStruct((t, LANES), F32), jax.ShapeDtypeStruct((SUBLANES, LANES), F32)],
        compiler_params=pltpu.CompilerParams(vmem_limit_bytes=VMEM_LIMIT), name="moe_plan")(gates)


def _sc_scatter_rows(x, idx, n_out):
    n, d = x.shape
    mesh = plsc.VectorSubcoreMesh(core_axis_name="core", subcore_axis_name="subcore")
    assert n % (SC_WINDOW * mesh.num_cores * mesh.num_subcores) == 0

    @functools.partial(pl.kernel, out_type=jax.ShapeDtypeStruct((n_out, d), x.dtype), mesh=mesh, scratch_types=[])
    def scatter(x_hbm, i_hbm, o_hbm):
        def body(x_vmem, i_vmem):
            pltpu.sync_copy(x_vmem, o_hbm.at[i_vmem.at[0]])

        pltpu.emit_pipeline(
            body, grid=(n // SC_WINDOW,),
            in_specs=[pl.BlockSpec((SC_WINDOW, d), index_map=lambda i: (i, 0)),
                      pl.BlockSpec((1, SC_WINDOW), index_map=lambda i: (0, i))],
            out_specs=[], core_axis_name=('core', 'subcore'), dimension_semantics=(pltpu.PARALLEL,))(x_hbm, i_hbm)

    return scatter(x, idx.reshape(1, n))


def _sc_gather_rows(x, idx):
    n = idx.shape[0]
    d = x.shape[1]
    mesh = plsc.VectorSubcoreMesh(core_axis_name="core", subcore_axis_name="subcore")
    assert n % (SC_WINDOW * mesh.num_cores * mesh.num_subcores) == 0

    @functools.partial(pl.kernel, out_type=jax.ShapeDtypeStruct((n, d), x.dtype), mesh=mesh)
    def gather(x_hbm, i_hbm, o_hbm):
        def body(i_vmem, o_vmem):
            pltpu.sync_copy(x_hbm.at[i_vmem.at[0]], o_vmem)

        pltpu.emit_pipeline(
            body, grid=(n // SC_WINDOW,),
            in_specs=[pl.BlockSpec((1, SC_WINDOW), index_map=lambda i: (0, i))],
            out_specs=[pl.BlockSpec((SC_WINDOW, d), index_map=lambda i: (i, 0))],
            core_axis_name=('core', 'subcore'), dimension_semantics=(pltpu.PARALLEL,))(i_hbm, o_hbm)

    return gather(x, idx.reshape(1, n))


def _experts_sorted_kernel(tg_ref, tv_ref, *refs):
    xs_refs, (gs_ref, wg_ref, wu_ref, wd_ref), ys_refs = (
        refs[:N_XCHUNKS], refs[N_XCHUNKS:N_XCHUNKS + 4], refs[-N_CHUNKS:])
    j = pl.program_id(0)

    @pl.when(tv_ref[j] > 0)
    def _():
        g = tg_ref[j]
        halves = []
        for r in xs_refs:
            w = r[...]
            halves.append(pltpu.bitcast(jnp.bitwise_and(w, jnp.uint32(0xFFFF0000)), F32))
            halves.append(pltpu.bitcast(jnp.left_shift(w, jnp.uint32(16)), F32))
        x = jnp.concatenate(halves, axis=1).astype(BF16)
        gates = gs_ref[...]
        acts = []
        for k in range(EXPERTS_PER_GROUP):
            hg = _dot(x, wg_ref[k])
            gate = jnp.sum(jnp.where(_lane_iota() == g * EXPERTS_PER_GROUP + k, gates, 0.0),
                           axis=-1, keepdims=True)
            acts.append(((hg * _sigmoid(hg)) * _dot(x, wu_ref[k]) * gate).astype(BF16))
        y = _dot(jnp.concatenate(acts, axis=1), wd_ref[...])
        for c, ref in enumerate(ys_refs):
            ref[...] = y[:, c * SC_ROW_WORDS:(c + 1) * SC_ROW_WORDS]


def _experts_sorted(xs_chunks, gs, tile_group, tile_valid, lw):
    n_rows = gs.shape[0]
    row = lambda w: pl.BlockSpec((SORT_TILE, w), lambda j, tg, tv: (j, 0))
    grp = lambda a: pl.BlockSpec((EXPERTS_PER_GROUP,) + a.shape[1:], lambda j, tg, tv: (tg[j], 0, 0))
    d_group = lw['w_ed'].shape[0] // N_GROUPS
    return pl.pallas_call(
        _experts_sorted_kernel,
        grid_spec=pltpu.PrefetchScalarGridSpec(
            num_scalar_prefetch=2, grid=(n_rows // SORT_TILE,),
            in_specs=[row(SC_ROW_WORDS)] * N_XCHUNKS + [
                row(LANES), grp(lw['w_eg']), grp(lw['w_eu']),
                pl.BlockSpec((d_group, D_MODEL), lambda j, tg, tv: (tg[j], 0))],
            out_specs=[row(SC_ROW_WORDS)] * N_CHUNKS),
        out_shape=[jax.ShapeDtypeStruct((n_rows, SC_ROW_WORDS), F32)] * N_CHUNKS,
        compiler_params=_cparams("arbitrary"), name="moe_sorted")(
            tile_group, tile_valid, *xs_chunks, gs, lw['w_eg'], lw['w_eu'], lw['w_ed'])


def _ple_kernel(h1_ref, *refs):
    y_refs, (p_ref, wple_ref, nple_ref, wpg_ref, out_ref) = refs[:N_CHUNKS], refs[N_CHUNKS:]
    h2 = h1_ref[...] + jnp.concatenate([r[...] for r in y_refs], axis=1)
    gate_ple = _sigmoid(_dot(_rms(h2, nple_ref[...]).astype(BF16), wpg_ref[...]))
    out_ref[...] = h2 + _dot(p_ref[0].astype(BF16), wple_ref[...]) * gate_ple


def _ple(h1, y_chunks, p_all, layer, lw, tm):
    t = h1.shape[0]
    row = lambda w: pl.BlockSpec((tm, w), lambda i: (i, 0))
    full = lambda a: pl.BlockSpec(a.shape, lambda i: (0,) * a.ndim)
    return pl.pallas_call(
        _ple_kernel, grid=(t // tm,),
        in_specs=[row(D_MODEL)] + [row(SC_ROW_WORDS)] * N_CHUNKS
        + [pl.BlockSpec((1, tm, p_all.shape[2]), lambda i: (layer, i, 0)),
           full(lw['w_ple']), full(lw['norm_ple']), full(lw['w_pg'])],
        out_specs=row(D_MODEL), out_shape=jax.ShapeDtypeStruct((t, D_MODEL), F32),
        compiler_params=_cparams("parallel"), name="ple")(
            h1, *y_chunks, p_all, lw['w_ple'], lw['norm_ple'], lw['w_pg'])


def _moe_dispatch(x_chunks, gates):
    t = gates.shape[0]
    n_rows = t + N_GROUPS * SORT_TILE
    dest_rep, total = _plan(gates)
    dest = dest_rep[:, 0].astype(jnp.int32)
    padded = (total[0, N_EXPERTS:N_EXPERTS + N_GROUPS].astype(jnp.int32) + (SORT_TILE - 1)) // SORT_TILE * SORT_TILE
    ends = jnp.cumsum(padded)
    tile_start = jnp.arange(n_rows // SORT_TILE, dtype=jnp.int32) * SORT_TILE
    tile_group = jnp.minimum(jnp.sum(tile_start[:, None] >= ends[None, :], axis=1), N_GROUPS - 1).astype(jnp.int32)
    tile_valid = (tile_start < ends[-1]).astype(jnp.int32)
    xs_chunks = [_sc_scatter_rows(xc, dest, n_rows) for xc in x_chunks]
    gs = _sc_scatter_rows(gates, dest, n_rows)
    return xs_chunks, gs, tile_group, tile_valid, dest


def _moe_combine(dispatched, h1, p_all, layer, lw):
    xs_chunks, gs, tile_group, tile_valid, dest = dispatched
    ys_chunks = _experts_sorted(xs_chunks, gs, tile_group, tile_valid, lw)
    y_chunks = [_sc_gather_rows(yc, dest) for yc in ys_chunks]
    return _ple(h1, y_chunks, p_all, layer, lw, _token_tile(h1.shape[0], 1024))


def _regroup_out_rows(wt):
    z = jnp.zeros((HEAD_DIM, wt.shape[1]), wt.dtype)
    return jnp.concatenate([wt[:W_A], wt[W_A:W_A + W_B], z, wt[W_A + W_B:], z], axis=0)


def _regroup_rows(w):
    z = jnp.zeros((HEAD_DIM, w.shape[1]), w.dtype)
    return jnp.stack([w[:W_A], jnp.concatenate([w[W_A:W_A + W_B], z], axis=0),
                      jnp.concatenate([w[W_A + W_B:], z], axis=0)])


def _pad_lanes(a, n=LANES):
    return jnp.pad(a, [(0, 0)] * (a.ndim - 1) + [(0, n - a.shape[-1])])


def _toeplitz_bias(rel_bias, off, tq, win):
    length = tq + win
    pad = length + abs(off)
    ext = jnp.flip(jnp.pad(rel_bias, ((0, 0), (pad, pad)), mode='edge'), axis=1)
    s1 = ext.shape[1] - 1 - (off + REL_CLIP + pad)
    v = jnp.concatenate([ext[:, s1:s1 + win], ext[:, s1 - tq:s1]], axis=1)
    flat = jnp.tile(v, (1, tq))[:, :tq * (length - 1)]
    return flat.reshape(-1, tq, length - 1)[:, :, :win]


def _block_toeplitz_bias(rel_bias, off, tq, win):
    nq, nk = tq // LANES, win // LANES
    blocks = {d: _toeplitz_bias(rel_bias, off - d * LANES, LANES, LANES) for d in range(-(nq - 1), nk)}
    return jnp.concatenate(
        [jnp.concatenate([blocks[b - a] for b in range(nk)], axis=2) for a in range(nq)], axis=1)


def _rel_tables(rel_bias, t_new, rows_b):
    tq, win = ATT_TILE, ATT_TILE + BAND
    q = jnp.arange(tq)[:, None]
    tabs = []
    for var in range(BAND // tq + 1):
        rel_k = jnp.arange(win)[None, :] - var * tq
        kch = jnp.floor_divide(rel_k, CHUNK)
        qch = q // CHUNK
        valid = (kch <= qch) & (kch >= qch - BAND_CHUNKS)
        tabs.append(jnp.where(valid[None], _block_toeplitz_bias(rel_bias, var * tq, tq, win), NEG_INF))
    tab_prompt = jnp.stack(tabs).astype(F32)
    tab_c = _toeplitz_bias(rel_bias, rows_b, t_new, rows_b).astype(F32)
    tab_n = _toeplitz_bias(rel_bias, 0, t_new, t_new).astype(F32)
    return tab_prompt, tab_c, tab_n


def _layer_weights(i, norm_mix, w_in, b_f, g_qa, g_ka, g_qb, g_kb, g_mix_out, w_out, norm_ffn,
                   w_rg, b_rg, w_re, b_re, w_eg, w_eu, w_ed, w_ple, norm_ple, w_pg):
    d_mix = W_A + W_B + W_C
    wt = jnp.transpose(w_in, (2, 0, 1))[:, i, :]
    w_f = wt[3 * d_mix:]
    w_big = jnp.concatenate(
        [_regroup_out_rows(wt[j * d_mix:(j + 1) * d_mix]) for j in range(3)]
        + [jnp.pad(w_f, ((0, LANES - w_f.shape[0]), (0, 0)))], axis=0).astype(BF16)
    tile6 = lambda g: jnp.tile(g, GROUP_W // HEAD_DIM)
    w_r = _pad_lanes(jnp.concatenate([w_re[i], w_rg[i]], axis=1))
    w_r_hi = w_r.astype(BF16)
    gm = g_mix_out[i]
    zpad = jnp.zeros((HEAD_DIM,), F32)
    g_mix = jnp.stack([gm[:W_A], jnp.concatenate([gm[W_A:W_A + W_B], zpad]),
                       jnp.concatenate([gm[W_A + W_B:], zpad])])[:, None, :]
    return dict(
        norm_mix=norm_mix[i][None], w_in=w_big,
        gq=jnp.concatenate([tile6(g_qa[i]), tile6(g_qb[i])])[None],
        gk=jnp.concatenate([tile6(g_ka[i]), tile6(g_kb[i])])[None],
        b_f=_pad_lanes(b_f[i])[None],
        g_mix=g_mix, w_out=_regroup_rows(w_out[i]).astype(BF16), norm_ffn=norm_ffn[i][None],
        w_r_hi=w_r_hi, w_r_lo=(w_r - w_r_hi.astype(F32)).astype(BF16),
        b_r=_pad_lanes(jnp.concatenate([b_re[i], b_rg[i]]))[None],
        w_eg=w_eg[i].astype(BF16), w_eu=w_eu[i].astype(BF16),
        w_ed=w_ed[i].reshape(-1, w_ed.shape[-1]).astype(BF16),
        w_ple=w_ple[i].astype(BF16), norm_ple=norm_ple[i][None], w_pg=w_pg[i].astype(BF16))


def _token_tile(t, pref):
    return pref if t % pref == 0 else t


def _feature_major(cache):
    d, n, p, h, e = cache.shape
    return jnp.transpose(cache, (0, 1, 3, 4, 2)).reshape(d, n, h * e, p)


def _position_major(rows, heads):
    d, n, _, p = rows.shape
    return jnp.transpose(rows.reshape(d, n, heads, HEAD_DIM, p), (0, 1, 4, 2, 3))


def kernel(x_prompt, x_sample, p_prompt, p_sample, cache_a_k, cache_a_v, cache_a_logf, cache_b_k, cache_b_v, cache_c_k, cache_c_v, norm_mix, w_in, b_f, g_qa, g_ka, g_qb, g_kb, rel_bias, g_mix_out, w_out, norm_ffn, w_router_group, b_router_group, w_router_expert, b_router_expert, w_exp_gate, w_exp_up, w_exp_down, w_ple, norm_ple, w_ple_gate):
    nb, s, d = x_prompt.shape
    ns, t_new, _ = x_sample.shape
    depth = w_in.shape[0]
    past = cache_a_k.shape[2]
    rows_b = cache_b_k.shape[2]
    assert d == D_MODEL and s % PROJ_TILE == 0 and s >= BAND + ATT_TILE and past % ATT_TILE == 0
    tp, ts = nb * s, ns * t_new

    hp = x_prompt.reshape(tp, d)
    hs = x_sample.reshape(ts, d)
    pp = p_prompt.reshape(depth, tp, -1)
    ps = p_sample.reshape(depth, ts, -1)
    caches = tuple(_feature_major(c) for c in (cache_a_k, cache_a_v, cache_b_k, cache_b_v, cache_c_k, cache_c_v))
    lf_rows = jnp.transpose(cache_a_logf, (0, 3, 1, 2))
    rsum = _suffix_sum(lf_rows.reshape(depth * H_A * ns, past)).reshape(depth, H_A, ns, past)
    rsum = jnp.transpose(rsum, (0, 2, 1, 3))

    rows_p = None
    new_s = [[] for _ in range(7)]
    for i in range(depth):
        lw = _layer_weights(i, norm_mix, w_in, b_f, g_qa, g_ka, g_qb, g_kb, g_mix_out, w_out, norm_ffn,
                            w_router_group, b_router_group, w_router_expert, b_router_expert,
                            w_exp_gate, w_exp_up, w_exp_down, w_ple, norm_ple, w_ple_gate)
        tab_p, tab_c, tab_n = _rel_tables(rel_bias[i] * LOG2E, t_new, rows_b)

        pr = _inproj_prompt(hp, lw, i, depth, nb, s, rows_p)
        qat, ka, vat, qb, kb, vb, qc, kc, vc, logf = pr[:10]
        rows_p = pr[10:]
        kaug, qaug = _forget_aug(logf, nb, s)
        oa = _attn_a(qat, ka, vat, kaug, qaug, nb, s)
        ob = _prompt_attention(
            _attn_b_kernel, "attn_b", qb, kb, vb, (tab_p,),
            [pl.BlockSpec(tab_p.shape, lambda b, j: (0, 0, 0, 0))], [], nb, s, ATT_TILE)
        oc = _prompt_attention(
            _attn_c_kernel, "attn_c", qc, kc, vc, (), [],
            [pltpu.VMEM((H_C, C_QTILE, PAIR_W), BF16), pltpu.VMEM((H_C, C_QTILE, LANES), F32),
             pltpu.VMEM((H_C, C_QTILE, PAIR_W), F32)], nb, s, C_QTILE)
        h1, gates, *x_chunks = _post(oa, ob, oc, hp, lw, _token_tile(tp, 512), N_XCHUNKS)
        dispatched = _moe_dispatch(x_chunks, gates)

        sr = _inproj_sample(hs, lw)
        oa, ob, oc = _sample_attention(sr, caches, i, rsum, tab_c, tab_n, ns, t_new)
        h1s, gates, xn = _post(oa, ob, oc, hs, lw, _token_tile(ts, 512))
        hs = _moe(xn, gates, h1s, ps, i, lw, _token_tile(ts, 1024))

        hp = _moe_combine(dispatched, h1, pp, i, lw)
        akf, avf, bkf, bvf, ckf, cvf, logf = sr[9:]
        rows = (akf.reshape(ns, t_new, H_A, HEAD_DIM), avf.reshape(ns, t_new, H_A, HEAD_DIM),
                logf[:, :H_A].reshape(ns, t_new, H_A),
                bkf.reshape(ns, t_new, H_B, HEAD_DIM), bvf.reshape(ns, t_new, H_B, HEAD_DIM),
                ckf.reshape(ns, t_new, H_C, HEAD_DIM), cvf.reshape(ns, t_new, H_C, HEAD_DIM))
        for j in range(7):
            new_s[j].append(rows[j])

    akt, avt, bkt, bvt, ckt, cvt, lft = rows_p
    outs_p = (_position_major(akt, H_A), _position_major(avt, H_A),
              jnp.transpose(lft.reshape(depth, H_A, nb, s), (0, 2, 3, 1)),
              _position_major(bkt, H_B), _position_major(bvt, H_B),
              _position_major(ckt, H_C), _position_major(cvt, H_C))
    outs_s = [jnp.stack(r, axis=0) for r in new_s]
    return (hp.reshape(nb, s, d), hs.reshape(ns, t_new, d), *outs_p, *outs_s)
```

```python
import functools
import math

import jax
import jax.numpy as jnp
from jax import lax
from jax.experimental import pallas as pl
from jax.experimental.pallas import tpu as pltpu
from jax.experimental.pallas import tpu_sc as plsc

F32 = jnp.float32
BF16 = jnp.bfloat16

D_MODEL = 1024
HEAD_DIM = 64
H_A, H_B, H_C = 6, 5, 5
W_A, W_B, W_C = H_A * HEAD_DIM, H_B * HEAD_DIM, H_C * HEAD_DIM
GROUP_W = 384
PAIR_W = 2 * HEAD_DIM
N_PAIRS = GROUP_W // PAIR_W
CHUNK = 64
BAND_CHUNKS = 8
BAND = BAND_CHUNKS * CHUNK
REL_CLIP = 128
N_GROUPS = 4
EXPERTS_PER_GROUP = 4
N_EXPERTS = N_GROUPS * EXPERTS_PER_GROUP
EPS = 1e-6
ATTN_SCALE = HEAD_DIM ** -0.5
LOG2E = math.log2(math.e)
Q_SCALE = ATTN_SCALE * LOG2E
NEG_INF = -1e30
LANES = 128
SUBLANES = 8
VMEM_LIMIT = 48 * 1024 * 1024
EXPERTS_PER_STEP = 4
MOE_VMEM_LIMIT = 58 * 1024 * 1024

ATT_TILE = 256
A_QTILE = 512
C_QTILE = 256
PROJ_TILE = 512
PROJ_VMEM_LIMIT = 56 * 1024 * 1024
AUG_SLOTS = 8

_NT = (((1,), (1,)), ((), ()))


def _cparams(*sem):
    return pltpu.CompilerParams(dimension_semantics=sem, vmem_limit_bytes=VMEM_LIMIT)


def _dot(a, b):
    return jnp.dot(a, b, preferred_element_type=F32)


def _dot_nt(a, b):
    return lax.dot_general(a, b, _NT, preferred_element_type=F32)


def _split3(x):
    hi = x.astype(BF16)
    r1 = x - hi.astype(F32)
    mid = r1.astype(BF16)
    lo = (r1 - mid.astype(F32)).astype(BF16)
    return hi, mid, lo


def _split2(x):
    hi = x.astype(BF16)
    lo = (x - hi.astype(F32)).astype(BF16)
    return hi, lo


def _lane_iota(n=LANES):
    return lax.broadcasted_iota(jnp.int32, (1, n), 1)


def _half_mask(half):
    lane = _lane_iota()
    return (lane < HEAD_DIM) if half == 0 else (lane >= HEAD_DIM)


def _pair_slice(h):
    return slice((h // 2) * PAIR_W, (h // 2 + 1) * PAIR_W)


def _softplus2(z2):
    return jnp.maximum(z2, 0.0) + jnp.log2(1.0 + jnp.exp2(-jnp.abs(z2)))


def _sigmoid(z):
    return 1.0 / (1.0 + jnp.exp(-z))


def _rms(x, g):
    return x * lax.rsqrt(jnp.mean(x * x, axis=-1, keepdims=True) + EPS) * g


def _pair_rms(z, gain):
    first = _half_mask(0)
    outs = []
    for j in range(N_PAIRS):
        blk = z[:, j * PAIR_W:(j + 1) * PAIR_W]
        sq = blk * blk
        lo = jnp.sum(jnp.where(first, sq, 0.0), axis=-1, keepdims=True)
        hi = jnp.sum(jnp.where(first, 0.0, sq), axis=-1, keepdims=True)
        ms = jnp.where(first, lo, hi) * (1.0 / HEAD_DIM)
        outs.append(blk * lax.rsqrt(ms + EPS) * gain[:, j * PAIR_W:(j + 1) * PAIR_W])
    return jnp.concatenate(outs, axis=-1)


def _projector(x_ref, gn_ref, w_ref):
    xb = _rms(x_ref[...], gn_ref[...]).astype(BF16)
    return lambda g, width=GROUP_W: _dot_nt(xb, w_ref[g * GROUP_W:g * GROUP_W + width, :])


def _log_forget(zf, bf_ref):
    zf = zf + bf_ref[...]
    return jnp.minimum(zf, 0.0) - jnp.log(1.0 + jnp.exp(-jnp.abs(zf)))


def _inproj_sample_kernel(x_ref, gn_ref, w_ref, gq_ref, gk_ref, bf_ref,
                          qa_ref, qb_ref, qc_ref, ka_ref, kb_ref, kc_ref, va_ref, vb_ref, vc_ref,
                          akf_ref, avf_ref, bkf_ref, bvf_ref, ckf_ref, cvf_ref, logf_ref):
    proj = _projector(x_ref, gn_ref, w_ref)
    qa_ref[...] = (_pair_rms(proj(0), gq_ref[:, :GROUP_W]) * Q_SCALE).astype(BF16)
    qb_ref[...] = (_pair_rms(proj(1), gq_ref[:, GROUP_W:]) * Q_SCALE).astype(BF16)
    qc_ref[...] = (proj(2) * Q_SCALE).astype(BF16)
    for g, norm, bf_ref_, f_ref, width in (
            (3, gk_ref[:, :GROUP_W], ka_ref, akf_ref, W_A), (4, gk_ref[:, GROUP_W:], kb_ref, bkf_ref, W_B),
            (5, None, kc_ref, ckf_ref, W_C), (6, None, va_ref, avf_ref, W_A),
            (7, None, vb_ref, bvf_ref, W_B), (8, None, vc_ref, cvf_ref, W_C)):
        z = proj(g)
        if norm is not None:
            z = _pair_rms(z, norm)
        f_ref[...] = z[:, :width]
        bf_ref_[...] = z.astype(BF16)
    logf_ref[...] = _log_forget(proj(9, LANES), bf_ref)


def _inproj_sample(x, lw):
    t = x.shape[0]
    row = lambda w: pl.BlockSpec((t, w), lambda i: (0, 0))
    full = lambda a: pl.BlockSpec(a.shape, lambda i: (0,) * a.ndim)
    widths = [GROUP_W] * 9 + [W_A, W_A, W_B, W_B, W_C, W_C, LANES]
    out_shape = [jax.ShapeDtypeStruct((t, w), BF16 if j < 9 else F32) for j, w in enumerate(widths)]
    ins = (x, lw['norm_mix'], lw['w_in'], lw['gq'], lw['gk'], lw['b_f'])
    return pl.pallas_call(
        _inproj_sample_kernel, grid=(1,),
        in_specs=[row(D_MODEL)] + [full(a) for a in ins[1:]],
        out_specs=[row(w) for w in widths], out_shape=out_shape,
        compiler_params=_cparams("arbitrary"), name="inproj_sample")(*ins)


N_PROMPT_ROWS = 7


def _inproj_prompt_kernel(*refs, n_alias):
    x_ref, gn_ref, w_ref, gq_ref, gk_ref, bf_ref = refs[:6]
    (qat_ref, ka_ref, vat_ref, qb_ref, kb_ref, vb_ref, qc_ref, kc_ref, vc_ref, logf_ref,
     akt_ref, avt_ref, bkt_ref, bvt_ref, ckt_ref, cvt_ref, lft_ref) = refs[6 + n_alias:]
    proj = _projector(x_ref, gn_ref, w_ref)

    def put(ref, val):
        for l in range(ref.shape[0]):
            ref[l, 0] = val

    qat_ref[...] = jnp.transpose(_pair_rms(proj(0), gq_ref[:, :GROUP_W]) * Q_SCALE).astype(BF16)
    qb_ref[...] = (_pair_rms(proj(1), gq_ref[:, GROUP_W:]) * Q_SCALE).astype(BF16)
    qc_ref[...] = (proj(2) * Q_SCALE).astype(BF16)
    ka = _pair_rms(proj(3), gk_ref[:, :GROUP_W])
    ka_ref[...] = ka.astype(BF16)
    put(akt_ref, jnp.transpose(ka))
    kb = _pair_rms(proj(4), gk_ref[:, GROUP_W:])
    kb_ref[...] = kb.astype(BF16)
    kc = proj(5)
    kc_ref[...] = kc.astype(BF16)
    put(ckt_ref, jnp.transpose(kc)[:W_C])
    vat = jnp.transpose(proj(6))
    put(avt_ref, vat)
    for j in range(vat_ref.shape[1]):
        vat_ref[0, j] = vat[:, j * ATT_TILE:(j + 1) * ATT_TILE].astype(BF16)
    vb = proj(7)
    vb_ref[...] = vb.astype(BF16)
    vc = proj(8)
    vc_ref[...] = vc.astype(BF16)
    put(cvt_ref, jnp.transpose(vc)[:W_C])

    @pl.when(pl.program_id(1) == pl.num_programs(1) - 1)
    def _():
        keep = kb.shape[0] - BAND
        put(bkt_ref, jnp.transpose(kb)[:W_B, keep:])
        put(bvt_ref, jnp.transpose(vb)[:W_B, keep:])

    lf = _log_forget(proj(9, LANES), bf_ref)
    logf_ref[...] = lf
    for l in range(lft_ref.shape[0]):
        lft_ref[l] = jnp.transpose(lf)[:H_A]


def _inproj_prompt(x, lw, layer, depth, nb, s, prev_rows):
    tm = PROJ_TILE
    nt = s // tm
    t = nb * s
    per = tm // ATT_TILE
    span, slot = (depth, 0) if prev_rows is None else (1, layer)
    row = lambda w: pl.BlockSpec((tm, w), lambda b, i: (b * nt + i, 0))
    full = lambda a: pl.BlockSpec(a.shape, lambda b, i: (0,) * a.ndim, pipeline_mode=pl.Buffered(1))
    feat = lambda w: pl.BlockSpec((span, 1, w, tm), lambda b, i: (slot, b, 0, i))
    band = lambda w: pl.BlockSpec((span, 1, w, BAND), lambda b, i: (slot, b, 0, 0))
    tok = jax.ShapeDtypeStruct((t, GROUP_W), BF16)
    out_shape = [
        jax.ShapeDtypeStruct((GROUP_W, t), BF16), tok,
        jax.ShapeDtypeStruct((nb, s // ATT_TILE, GROUP_W, ATT_TILE), BF16),
        tok, tok, tok, tok, tok, tok, jax.ShapeDtypeStruct((t, LANES), F32),
        jax.ShapeDtypeStruct((depth, nb, W_A, s), F32), jax.ShapeDtypeStruct((depth, nb, W_A, s), F32),
        jax.ShapeDtypeStruct((depth, nb, W_B, BAND), F32), jax.ShapeDtypeStruct((depth, nb, W_B, BAND), F32),
        jax.ShapeDtypeStruct((depth, nb, W_C, s), F32), jax.ShapeDtypeStruct((depth, nb, W_C, s), F32),
        jax.ShapeDtypeStruct((depth, H_A, t), F32)]
    out_specs = [
        pl.BlockSpec((GROUP_W, tm), lambda b, i: (0, b * nt + i)), row(GROUP_W),
        pl.BlockSpec((1, per, GROUP_W, ATT_TILE), lambda b, i: (b, i, 0, 0)),
        row(GROUP_W), row(GROUP_W), row(GROUP_W), row(GROUP_W), row(GROUP_W), row(GROUP_W), row(LANES),
        feat(W_A), feat(W_A), band(W_B), band(W_B), feat(W_C), feat(W_C),
        pl.BlockSpec((span, H_A, tm), lambda b, i: (slot, 0, b * nt + i))]
    ins = [x, lw['norm_mix'], lw['w_in'], lw['gq'], lw['gk'], lw['b_f']]
    in_specs = [row(D_MODEL)] + [full(a) for a in ins[1:]]
    aliases = {}
    n_alias = 0
    if prev_rows is not None:
        n_alias = N_PROMPT_ROWS
        first_row_out = len(out_shape) - N_PROMPT_ROWS
        for j, a in enumerate(prev_rows):
            aliases[len(ins)] = first_row_out + j
            ins.append(a)
            in_specs.append(pl.BlockSpec(memory_space=pl.ANY))
    return pl.pallas_call(
        functools.partial(_inproj_prompt_kernel, n_alias=n_alias), grid=(nb, nt),
        in_specs=in_specs, out_specs=out_specs, out_shape=out_shape,
        input_output_aliases=aliases,
        compiler_params=pltpu.CompilerParams(dimension_semantics=("parallel", "arbitrary"),
                                             vmem_limit_bytes=PROJ_VMEM_LIMIT),
        name="inproj_prompt")(*ins)


def _forget_aug_kernel(logf_ref, kaug_ref, qaug_ref, *, nblk):
    tb = ATT_TILE
    r = lax.broadcasted_iota(jnp.int32, (tb, tb), 0)
    c = lax.broadcasted_iota(jnp.int32, (tb, tb), 1)
    tri = jnp.where(c <= r, 1.0, 0.0).astype(BF16)
    src = lax.broadcasted_iota(jnp.int32, (LANES, LANES), 0)
    dst = lax.broadcasted_iota(jnp.int32, (LANES, LANES), 1)

    def place(slot):
        return jnp.where((dst == AUG_SLOTS * src + slot) & (src < H_A), 1.0, 0.0).astype(BF16)

    lane = _lane_iota()
    slot = jnp.bitwise_and(lane, AUG_SLOTS - 1)
    used = lane < AUG_SLOTS * H_A
    key_const = jnp.where(used & (slot >= 3) & (slot < 6), 1.0, 0.0)
    qry_const = jnp.where(used & (slot < 3), -1.0, 0.0)
    carry = jnp.zeros((1, LANES), F32)
    for blk in range(nblk):
        rows = slice(blk * tb, (blk + 1) * tb)
        hi, mid, lo = _split3(logf_ref[rows, :])
        cs = (_dot(tri, hi) + _dot(tri, mid) + _dot(tri, lo)) + carry
        carry = cs[tb - 1:tb, :]
        hi, mid, lo = _split3(cs * LOG2E)
        kaug = _dot(hi, place(0)) + _dot(mid, place(1)) + _dot(lo, place(2)) + key_const
        kaug_ref[rows, :] = kaug.astype(BF16)
        qaug = _dot(hi, place(3)) + _dot(mid, place(4)) + _dot(lo, place(5)) + qry_const
        qaug_ref[:, rows] = jnp.transpose(qaug).astype(BF16)


def _forget_aug(logf, nb, s):
    return pl.pallas_call(
        functools.partial(_forget_aug_kernel, nblk=s // ATT_TILE), grid=(nb,),
        in_specs=[pl.BlockSpec((s, LANES), lambda b: (b, 0))],
        out_specs=[pl.BlockSpec((s, LANES), lambda b: (b, 0)), pl.BlockSpec((LANES, s), lambda b: (0, b))],
        out_shape=[jax.ShapeDtypeStruct((nb * s, LANES), BF16), jax.ShapeDtypeStruct((LANES, nb * s), BF16)],
        compiler_params=_cparams("parallel"), name="forget_aug")(logf)


def _suffix_sum_kernel(x_ref, o_ref):
    p = x_ref.shape[1]
    j = lax.broadcasted_iota(jnp.int32, (p, p), 0)
    s = lax.broadcasted_iota(jnp.int32, (p, p), 1)
    tri = jnp.where(j > s, 1.0, 0.0).astype(BF16)
    hi, mid, lo = _split3(x_ref[...])
    o_ref[...] = _dot(hi, tri) + _dot(mid, tri) + _dot(lo, tri)


def _suffix_sum(x):
    return pl.pallas_call(
        _suffix_sum_kernel, out_shape=jax.ShapeDtypeStruct(x.shape, F32),
        compiler_params=pltpu.CompilerParams(vmem_limit_bytes=VMEM_LIMIT), name="suffix_sum")(x)


def _attn_a_kernel(qt_ref, k_ref, vt_ref, kaug_ref, qaug_ref, o_ref, qp_ref, m_ref, l_ref, acc_ref):
    tq = qt_ref.shape[1]
    tk = ATT_TILE
    qi = pl.program_id(1)
    row = lax.broadcasted_iota(jnp.int32, (PAIR_W, 1), 0)
    aug = qaug_ref[...]
    for h in range(H_A):
        q2 = qt_ref[_pair_slice(h), :]
        half = (row < HEAD_DIM) if h % 2 == 0 else (row >= HEAD_DIM)
        qp_ref[h, :PAIR_W, :] = jnp.where(half, q2, jnp.zeros_like(q2))
        mine = (row >= AUG_SLOTS * h) & (row < AUG_SLOTS * (h + 1))
        qp_ref[h, PAIR_W:, :] = jnp.where(mine, aug, jnp.zeros_like(aug))
    m_ref[...] = jnp.full(m_ref.shape, NEG_INF, F32)
    l_ref[...] = jnp.zeros(l_ref.shape, F32)
    acc_ref[...] = jnp.zeros(acc_ref.shape, F32)

    def step(kt, key_offset):
        ks = pl.multiple_of(kt * tk, tk)
        ka = kaug_ref[pl.ds(ks, tk), :]
        tile_scores = [_dot(jnp.concatenate([k_ref[pl.ds(ks, tk), _pair_slice(h)], ka], axis=1), qp_ref[h])
                       for h in range(H_A)]
        probs, alphas = [], []
        for h in range(H_A):
            s = tile_scores[h]
            if key_offset is not None:
                key = lax.broadcasted_iota(jnp.int32, (tk, tq), 0)
                qry = lax.broadcasted_iota(jnp.int32, (tk, tq), 1)
                s = jnp.where(key + key_offset <= qry, s, NEG_INF)
            m_prev = m_ref[h]
            m_new = jnp.maximum(m_prev, jnp.max(s, axis=0, keepdims=True))
            alpha = jnp.exp2(m_prev - m_new)
            p = jnp.exp2(s - m_new)
            l_ref[h] = alpha * l_ref[h] + jnp.sum(p, axis=0, keepdims=True)
            m_ref[h] = m_new
            alphas.append(alpha)
            probs.append(p.astype(BF16))
        for h in range(H_A):
            acc_ref[h] = alphas[h] * acc_ref[h] + _dot(vt_ref[0, kt, _pair_slice(h), :], probs[h])

    def body(kt, carry):
        step(kt, None)
        return carry

    per = tq // tk
    lax.fori_loop(0, qi * per, body, 0)
    for j in range(per):
        step(qi * per + j, j * tk)
    for pair in range(N_PAIRS):
        h = 2 * pair
        out_t = jnp.where(row < HEAD_DIM, acc_ref[h] * (1.0 / l_ref[h]), acc_ref[h + 1] * (1.0 / l_ref[h + 1]))
        o_ref[:, pair * PAIR_W:(pair + 1) * PAIR_W] = jnp.transpose(out_t)


def _attn_a(qat, ka, vat, kaug, qaug, nb, s):
    tq = A_QTILE
    nq = s // tq
    return pl.pallas_call(
        _attn_a_kernel, grid=(nb, nq),
        in_specs=[pl.BlockSpec((GROUP_W, tq), lambda b, i: (0, b * nq + i)),
                  pl.BlockSpec((s, GROUP_W), lambda b, i: (b, 0)),
                  pl.BlockSpec((1, s // ATT_TILE, GROUP_W, ATT_TILE), lambda b, i: (b, 0, 0, 0)),
                  pl.BlockSpec((s, LANES), lambda b, i: (b, 0)),
                  pl.BlockSpec((LANES, tq), lambda b, i: (0, b * nq + i))],
        out_specs=pl.BlockSpec((tq, GROUP_W), lambda b, i: (b * nq + i, 0)),
        out_shape=jax.ShapeDtypeStruct((nb * s, GROUP_W), F32),
        scratch_shapes=[pltpu.VMEM((H_A, 2 * PAIR_W, tq), BF16), pltpu.VMEM((H_A, 1, tq), F32),
                        pltpu.VMEM((H_A, 1, tq), F32), pltpu.VMEM((H_A, PAIR_W, tq), F32)],
        compiler_params=_cparams("parallel", "arbitrary"), name="attn_a")(qat, ka, vat, kaug, qaug)


def _masked_q(q_ref, qm_ref, n_heads):
    for h in range(n_heads):
        q2 = q_ref[:, _pair_slice(h)]
        qm_ref[h] = jnp.where(_half_mask(h % 2), q2, jnp.zeros_like(q2))


def _attn_b_kernel(q_ref, k_ref, v_ref, tab_ref, o_ref):
    tq = ATT_TILE
    win = tq + BAND
    i = pl.program_id(1)
    var = jnp.minimum(i, BAND // tq)
    ws = pl.multiple_of(jnp.maximum(i * tq - BAND, 0), tq)
    heads = range(H_B)
    scores = []
    for h in heads:
        q2 = q_ref[:, _pair_slice(h)]
        qm = jnp.where(_half_mask(h % 2), q2, jnp.zeros_like(q2))
        scores.append(_dot_nt(qm, k_ref[pl.ds(ws, win), _pair_slice(h)]) + tab_ref[var, h])
    probs, norms = [], []
    for h in heads:
        p = jnp.exp2(scores[h] - jnp.max(scores[h], axis=-1, keepdims=True))
        norms.append(1.0 / jnp.sum(p, axis=-1, keepdims=True))
        probs.append(p.astype(BF16))
    res = [_dot(probs[h], v_ref[pl.ds(ws, win), _pair_slice(h)]) * norms[h] for h in heads]
    res.append(jnp.zeros((tq, PAIR_W), F32))
    first = _half_mask(0)
    for pair in range(N_PAIRS):
        o_ref[:, pair * PAIR_W:(pair + 1) * PAIR_W] = jnp.where(first, res[2 * pair], res[2 * pair + 1])


def _suffix_tri(n):
    j = lax.broadcasted_iota(jnp.int32, (2 * n, n), 0)
    s = lax.broadcasted_iota(jnp.int32, (2 * n, n), 1)
    return jnp.where(jnp.where(j >= n, j - n, j) >= s, 1.0, 0.0).astype(BF16)


def _stick_scores(z, tri, seen):
    sp = _softplus2(z)
    if seen is not None:
        sp = jnp.where(seen, sp, 0.0)
    hi, lo = _split2(sp)
    n = sp.shape[1]
    if n % LANES == 0:
        return _dot(jnp.concatenate([hi, lo], axis=1), tri)
    return _dot(hi, tri[:n]) + _dot(lo, tri[:n])


def _stick_weights(z, s_in, carry, seen):
    a = jnp.exp2(z - (s_in + carry))
    if seen is not None:
        a = jnp.where(seen, a, 0.0)
    return a.astype(BF16)


def _attn_c_kernel(q_ref, k_ref, v_ref, o_ref, qm_ref, carry_ref, acc_ref):
    tq = q_ref.shape[0]
    tk = ATT_TILE
    qi = pl.program_id(1)
    _masked_q(q_ref, qm_ref, H_C)
    carry_ref[...] = jnp.zeros(carry_ref.shape, F32)
    acc_ref[...] = jnp.zeros(acc_ref.shape, F32)
    heads = range(H_C)

    def step(kt, key_offset):
        ks = pl.multiple_of(kt * tk, tk)
        zs = [_dot_nt(qm_ref[h], k_ref[pl.ds(ks, tk), _pair_slice(h)]) for h in heads]
        tri = _suffix_tri(tk)
        seen = None
        if key_offset is not None:
            r = lax.broadcasted_iota(jnp.int32, (tq, tk), 0)
            c = lax.broadcasted_iota(jnp.int32, (tq, tk), 1)
            seen = c + key_offset < r
        sums = [_stick_scores(zs[h], tri, seen) for h in heads]
        weights = []
        for h in heads:
            carry = carry_ref[h][:, 0:1]
            weights.append(_stick_weights(zs[h], sums[h], carry, seen))
            carry_ref[h] = jnp.broadcast_to(carry + sums[h][:, 0:1], (tq, LANES))
        for h in heads:
            acc_ref[h] += _dot(weights[h], v_ref[pl.ds(ks, tk), _pair_slice(h)])

    per = tq // tk

    def body(j, c):
        step(qi * per - 1 - j, None)
        return c

    for j in reversed(range(per)):
        step(qi * per + j, j * tk)
    lax.fori_loop(0, qi * per, body, 0)
    first = _half_mask(0)
    for pair in range(N_PAIRS):
        h = 2 * pair
        second = acc_ref[h + 1] if h + 1 < H_C else jnp.zeros((tq, PAIR_W), F32)
        o_ref[:, pair * PAIR_W:(pair + 1) * PAIR_W] = jnp.where(first, acc_ref[h], second)


def _prompt_attention(kernel, name, q, k, v, extra, extra_specs, scratch, nb, s, tq):
    nq = s // tq
    qspec = pl.BlockSpec((tq, GROUP_W), lambda b, i: (b * nq + i, 0))
    kvspec = pl.BlockSpec((s, GROUP_W), lambda b, i: (b, 0))
    return pl.pallas_call(
        kernel, grid=(nb, nq),
        in_specs=[qspec, kvspec, kvspec] + extra_specs,
        out_specs=qspec, out_shape=jax.ShapeDtypeStruct((nb * s, GROUP_W), F32),
        scratch_shapes=scratch,
        compiler_params=_cparams("parallel", "arbitrary"), name=name)(q, k, v, *extra)


def _sample_heads(n_heads, q_ref, width):
    out = []
    for h in range(n_heads):
        lo = (h // 2) * PAIR_W
        span = slice(lo, min(lo + PAIR_W, width))
        q2 = q_ref[:, span]
        if span.stop - span.start == PAIR_W:
            q2 = jnp.where(_half_mask(h % 2), q2, jnp.zeros_like(q2))
        out.append((h, span, q2))
    return out


def _store_heads(o_ref, res, n_heads):
    for pair in range(N_PAIRS):
        lo = pair * PAIR_W
        h = 2 * pair
        if h + 1 < n_heads:
            o_ref[:, lo:lo + PAIR_W] = jnp.where(_half_mask(0), res[h], res[h + 1])
        else:
            o_ref[:, lo:lo + HEAD_DIM] = res[h]
            o_ref[:, lo + HEAD_DIM:lo + PAIR_W] = jnp.zeros_like(res[h])


def _joint_softmax_pv(s_c, s_n, vt_c, v_n):
    m = jnp.maximum(jnp.max(s_c, axis=-1, keepdims=True), jnp.max(s_n, axis=-1, keepdims=True))
    p_c = jnp.exp2(s_c - m)
    p_n = jnp.exp2(s_n - m)
    l = jnp.sum(p_c, axis=-1, keepdims=True) + jnp.sum(p_n, axis=-1, keepdims=True)
    return (_dot_nt(p_c.astype(BF16), vt_c) + _dot(p_n.astype(BF16), v_n)) / l


def _sample_attn_kernel(qa_ref, qb_ref, qc_ref, kan_ref, kbn_ref, kcn_ref, van_ref, vbn_ref, vcn_ref,
                        cak_ref, cav_ref, cbk_ref, cbv_ref, cck_ref, ccv_ref,
                        rsum_ref, logf_ref, tabc_ref, tabn_ref,
                        oa_ref, ob_ref, oc_ref):
    t = qa_ref.shape[0]
    past = cak_ref.shape[3]
    r = lax.broadcasted_iota(jnp.int32, (t, t), 0)
    c = lax.broadcasted_iota(jnp.int32, (t, t), 1)

    ltri = jnp.where(c <= r, 1.0, 0.0).astype(BF16)
    hi, mid, lo = _split3(logf_ref[...])
    pcol = (_dot(ltri, hi) + _dot(ltri, mid) + _dot(ltri, lo)) * LOG2E
    prow = jnp.transpose(jnp.concatenate([pcol, jnp.zeros((LANES - t, LANES), F32)], axis=0))
    heads = _sample_heads(H_A, qa_ref, W_A)
    kts = {sp.start: cak_ref[0, 0, sp, :].astype(BF16) for _, sp, _ in heads}
    vts = {sp.start: cav_ref[0, 0, sp, :].astype(BF16) for _, sp, _ in heads}
    s_c = [_dot(qm, kts[sp.start]) + (pcol[:, h:h + 1] + rsum_ref[0, 0, h:h + 1, :] * LOG2E)
           for h, sp, qm in heads]
    s_n = [jnp.where(c <= r, _dot_nt(qm, kan_ref[:, sp]) + (pcol[:, h:h + 1] - prow[h:h + 1, 0:t]), NEG_INF)
           for h, sp, qm in heads]
    _store_heads(oa_ref, [_joint_softmax_pv(s_c[h], s_n[h], vts[sp.start], van_ref[:, sp])
                          for h, sp, _ in heads], H_A)

    heads = _sample_heads(H_B, qb_ref, W_B)
    kts = {sp.start: cbk_ref[0, 0, sp, :].astype(BF16) for _, sp, _ in heads}
    vts = {sp.start: cbv_ref[0, 0, sp, :].astype(BF16) for _, sp, _ in heads}
    s_c = [_dot(qm, kts[sp.start]) + tabc_ref[h] for h, sp, qm in heads]
    s_n = [_dot_nt(qm, kbn_ref[:, sp]) + tabn_ref[h] for h, sp, qm in heads]
    _store_heads(ob_ref, [_joint_softmax_pv(s_c[h], s_n[h], vts[sp.start], vbn_ref[:, sp])
                          for h, sp, _ in heads], H_B)

    heads = _sample_heads(H_C, qc_ref, W_C)
    nblk = past // ATT_TILE
    tri_n = _suffix_tri(t)
    tri_c = _suffix_tri(ATT_TILE)
    seen = c < r
    kts = {sp.start: cck_ref[0, 0, sp, :].astype(BF16) for _, sp, _ in heads}
    vts = {sp.start: ccv_ref[0, 0, sp, :].astype(BF16) for _, sp, _ in heads}
    z_n = [_dot_nt(qm, kcn_ref[:, sp]) for _, sp, qm in heads]
    z_c = [_dot(qm, kts[sp.start]) for _, sp, qm in heads]
    sum_n = [_stick_scores(z, tri_n, seen) for z in z_n]
    sum_c = [[_stick_scores(z[:, b * ATT_TILE:(b + 1) * ATT_TILE], tri_c, None) for b in range(nblk)]
             for z in z_c]
    res = []
    for h, sp, _ in heads:
        acc = _dot(_stick_weights(z_n[h], sum_n[h], 0.0, seen), vcn_ref[:, sp])
        carry = sum_n[h][:, 0:1]
        for b in reversed(range(nblk)):
            cols = slice(b * ATT_TILE, (b + 1) * ATT_TILE)
            a = _stick_weights(z_c[h][:, cols], sum_c[h][b], carry, None)
            acc = acc + _dot_nt(a, vts[sp.start][:, cols])
            carry = carry + sum_c[h][b][:, 0:1]
        res.append(acc)
    _store_heads(oc_ref, res, H_C)


def _sample_attention(proj, caches, layer, rsum, tabc, tabn, nb, t):
    qa, qb, qc, ka, kb, kc, va, vb, vc = proj[:9]
    logf = proj[15]
    new = pl.BlockSpec((t, GROUP_W), lambda b: (b, 0))
    cache = lambda a: pl.BlockSpec((1, 1) + a.shape[2:], lambda b: (layer, b, 0, 0))
    full = lambda a: pl.BlockSpec(a.shape, lambda b: (0,) * a.ndim)
    return pl.pallas_call(
        _sample_attn_kernel, grid=(nb,),
        in_specs=[new] * 9 + [cache(a) for a in caches] + [
            cache(rsum), pl.BlockSpec((t, LANES), lambda b: (b, 0)), full(tabc), full(tabn)],
        out_specs=[new] * 3,
        out_shape=[jax.ShapeDtypeStruct((nb * t, GROUP_W), F32)] * 3,
        compiler_params=_cparams("parallel"), name="sample_attn")(
            qa, qb, qc, ka, kb, kc, va, vb, vc, *caches, rsum, logf, tabc, tabn)


def _masked_max(x, mask):
    return jnp.max(jnp.where(mask, x, -jnp.inf), axis=-1, keepdims=True)


def _first_lane(mask, lane):
    return jnp.min(jnp.where(mask, lane, float(LANES)), axis=-1, keepdims=True)


def _route(logits):
    lane = _lane_iota().astype(F32)
    is_g = (lane >= N_EXPERTS) & (lane < N_EXPERTS + N_GROUPS)
    gmax = _masked_max(logits, is_g)
    p_g = 1.0 / jnp.sum(jnp.where(is_g, jnp.exp(logits - gmax), 0.0), axis=-1, keepdims=True)
    g_sel = _first_lane(is_g & (logits == gmax), lane) - N_EXPERTS
    lo = g_sel * EXPERTS_PER_GROUP
    in_g = (lane >= lo) & (lane < lo + EXPERTS_PER_GROUP)
    l1 = _masked_max(logits, in_g)
    i1 = _first_lane(in_g & (logits == l1), lane)
    rest = in_g & (lane != i1)
    l2 = _masked_max(logits, rest)
    i2 = _first_lane(rest & (logits == l2), lane)
    e2 = jnp.exp(l2 - l1)
    w1 = p_g / (1.0 + e2)
    w2 = p_g * e2 / (1.0 + e2)
    gates = jnp.where(lane == i1, w1, jnp.where(lane == i2, w2, 0.0))
    return jnp.where(lane == g_sel + N_EXPERTS, 1.0, gates)


def _post_kernel(oa_ref, ob_ref, oc_ref, h_ref, gmix_ref, wout_ref, nffn_ref, wrh_ref, wrl_ref, br_ref,
                 h1_ref, gates_ref, *x_refs, packed):
    def gnorm(o, width):
        ms = jnp.sum(o * o, axis=-1, keepdims=True) * (1.0 / width)
        return o * lax.rsqrt(ms + EPS)

    h1 = h_ref[...]
    for g, (o_ref, width) in enumerate(((oa_ref, W_A), (ob_ref, W_B), (oc_ref, W_C))):
        merged = (gnorm(o_ref[...], width) * gmix_ref[g]).astype(BF16)
        h1 = h1 + _dot(merged, wout_ref[g])
    h1_ref[...] = h1
    xf = _rms(h1, nffn_ref[...])
    hi, lo = _split2(xf)
    if packed:
        bits = pltpu.bitcast(hi.astype(F32), jnp.uint32)
        for c, ref in enumerate(x_refs):
            a = bits[:, (2 * c) * SC_ROW_WORDS:(2 * c + 1) * SC_ROW_WORDS]
            b = bits[:, (2 * c + 1) * SC_ROW_WORDS:(2 * c + 2) * SC_ROW_WORDS]
            ref[...] = jnp.bitwise_or(a, jnp.right_shift(b, jnp.uint32(16)))
    else:
        x_refs[0][...] = hi
    logits = _dot(hi, wrh_ref[...]) + _dot(lo, wrh_ref[...]) + _dot(hi, wrl_ref[...]) + br_ref[...]
    gates_ref[...] = _route(logits)


def _post(oa, ob, oc, h, lw, tm, n_chunks=0):
    t = h.shape[0]
    row = lambda w: pl.BlockSpec((tm, w), lambda i: (i, 0))
    full = lambda a: pl.BlockSpec(a.shape, lambda i: (0,) * a.ndim)
    ws = (lw['g_mix'], lw['w_out'], lw['norm_ffn'], lw['w_r_hi'], lw['w_r_lo'], lw['b_r'])
    if n_chunks:
        x_specs = [row(SC_ROW_WORDS)] * n_chunks
        x_shapes = [jax.ShapeDtypeStruct((t, SC_ROW_WORDS), jnp.uint32)] * n_chunks
    else:
        x_specs, x_shapes = [row(D_MODEL)], [jax.ShapeDtypeStruct((t, D_MODEL), BF16)]
    return pl.pallas_call(
        functools.partial(_post_kernel, packed=bool(n_chunks)), grid=(t // tm,),
        in_specs=[row(GROUP_W)] * 3 + [row(D_MODEL)] + [full(a) for a in ws],
        out_specs=[row(D_MODEL), row(LANES)] + x_specs,
        out_shape=[jax.ShapeDtypeStruct((t, D_MODEL), F32), jax.ShapeDtypeStruct((t, LANES), F32)] + x_shapes,
        compiler_params=_cparams("parallel"), name="post")(oa, ob, oc, h, *ws)


def _moe_kernel(xn_ref, gates_ref, wg_ref, wu_ref, wd_ref, h1_ref, p_ref, wple_ref, nple_ref, wpg_ref,
                out_ref, acc_ref):
    step = pl.program_id(1)

    @pl.when(step == 0)
    def _():
        acc_ref[...] = jnp.zeros_like(acc_ref)

    x = xn_ref[...]
    gates = gates_ref[...]
    acts = []
    for k in range(EXPERTS_PER_STEP):
        e = step * EXPERTS_PER_STEP + k
        g = _dot(x, wg_ref[k])
        u = _dot(x, wu_ref[k])
        gate = jnp.sum(jnp.where(_lane_iota() == e, gates, 0.0), axis=-1, keepdims=True)
        acts.append(((g * _sigmoid(g)) * u * gate).astype(BF16))
    acc_ref[...] += _dot(jnp.concatenate(acts, axis=1), wd_ref[...])

    @pl.when(step == pl.num_programs(1) - 1)
    def _():
        h2 = h1_ref[...] + acc_ref[...]
        gate_ple = _sigmoid(_dot(_rms(h2, nple_ref[...]).astype(BF16), wpg_ref[...]))
        out_ref[...] = h2 + _dot(p_ref[0].astype(BF16), wple_ref[...]) * gate_ple


def _moe(xn, gates, h1, p_all, layer, lw, tm):
    t = xn.shape[0]
    row = lambda w: pl.BlockSpec((tm, w), lambda i, e: (i, 0))
    once = pl.Buffered(1)
    full = lambda a: pl.BlockSpec(a.shape, lambda i, e: (0,) * a.ndim, pipeline_mode=once)
    exp = lambda a: pl.BlockSpec((EXPERTS_PER_STEP,) + a.shape[1:], lambda i, e: (e, 0, 0))
    d_hidden = lw['w_ed'].shape[0] // N_EXPERTS
    return pl.pallas_call(
        _moe_kernel, grid=(t // tm, N_EXPERTS // EXPERTS_PER_STEP),
        in_specs=[row(D_MODEL), row(LANES), exp(lw['w_eg']), exp(lw['w_eu']),
                  pl.BlockSpec((EXPERTS_PER_STEP * d_hidden, D_MODEL), lambda i, e: (e, 0)),
                  pl.BlockSpec((tm, D_MODEL), lambda i, e: (i, 0), pipeline_mode=once),
                  pl.BlockSpec((1, tm, p_all.shape[2]), lambda i, e: (layer, i, 0), pipeline_mode=once),
                  full(lw['w_ple']), full(lw['norm_ple']), full(lw['w_pg'])],
        out_specs=row(D_MODEL), out_shape=jax.ShapeDtypeStruct((t, D_MODEL), F32),
        scratch_shapes=[pltpu.VMEM((tm, D_MODEL), F32)],
        compiler_params=pltpu.CompilerParams(dimension_semantics=("parallel", "arbitrary"),
                                             vmem_limit_bytes=MOE_VMEM_LIMIT), name="moe")(
            xn, gates, lw['w_eg'], lw['w_eu'], lw['w_ed'], h1, p_all, lw['w_ple'], lw['norm_ple'], lw['w_pg'])


SORT_TILE = 512
SC_WINDOW = 128
SC_ROW_WORDS = 256
N_CHUNKS = D_MODEL // SC_ROW_WORDS
N_XCHUNKS = N_CHUNKS // 2


def _plan_kernel(gates_ref, dest_ref, total_ref, *, tb):
    lane = _lane_iota()
    is_g = (lane >= N_EXPERTS) & (lane < N_EXPERTS + N_GROUPS)
    total = jnp.sum(jnp.where(is_g, gates_ref[...], 0.0), axis=0, keepdims=True)
    padded = jnp.floor((total + (SORT_TILE - 1)) * (1.0 / SORT_TILE)) * SORT_TILE
    base = jnp.zeros((1, LANES), F32)
    for g in range(1, N_GROUPS):
        before = jnp.sum(jnp.where(lane < N_EXPERTS + g, padded, 0.0), axis=-1, keepdims=True)
        base = jnp.where(lane == N_EXPERTS + g, before, base)
    r = lax.broadcasted_iota(jnp.int32, (tb, tb), 0)
    c = lax.broadcasted_iota(jnp.int32, (tb, tb), 1)
    ltri = jnp.where(c < r, 1.0, 0.0).astype(BF16)
    carry = jnp.zeros((1, LANES), F32)
    for blk in range(gates_ref.shape[0] // tb):
        rows = slice(blk * tb, (blk + 1) * tb)
        member = jnp.where(is_g, gates_ref[rows, :], 0.0)
        rank = _dot(ltri, member.astype(BF16)) + carry
        carry = carry + jnp.sum(member, axis=0, keepdims=True)
        dest = jnp.sum(member * (base + rank), axis=-1, keepdims=True)
        dest_ref[rows, :] = jnp.broadcast_to(dest, (tb, LANES))
    total_ref[...] = jnp.broadcast_to(total, total_ref.shape)


def _plan(gates):
    t = gates.shape[0]
    return pl.pallas_call(
        functools.partial(_plan_kernel, tb=SORT_TILE),
        out_shape=[jax.ShapeDtypeStruct((t, LANES), F32), jax.ShapeDtypeStruct((SUBLANES, LANES), F32)],
        compiler_params=pltpu.CompilerParams(vmem_limit_bytes=VMEM_LIMIT), name="moe_plan")(gates)


def _sc_scatter_rows(x, idx, n_out):
    n, d = x.shape
    mesh = plsc.VectorSubcoreMesh(core_axis_name="core", subcore_axis_name="subcore")
    assert n % (SC_WINDOW * mesh.num_cores * mesh.num_subcores) == 0

    @functools.partial(pl.kernel, out_type=jax.ShapeDtypeStruct((n_out, d), x.dtype), mesh=mesh, scratch_types=[])
    def scatter(x_hbm, i_hbm, o_hbm):
        def body(x_vmem, i_vmem):
            pltpu.sync_copy(x_vmem, o_hbm.at[i_vmem.at[0]])

        pltpu.emit_pipeline(
            body, grid=(n // SC_WINDOW,),
            in_specs=[pl.BlockSpec((SC_WINDOW, d), index_map=lambda i: (i, 0)),
                      pl.BlockSpec((1, SC_WINDOW), index_map=lambda i: (0, i))],
            out_specs=[], core_axis_name=('core', 'subcore'), dimension_semantics=(pltpu.PARALLEL,))(x_hbm, i_hbm)

    return scatter(x, idx.reshape(1, n))


def _sc_gather_rows(x, idx):
    n = idx.shape[0]
    d = x.shape[1]
    mesh = plsc.VectorSubcoreMesh(core_axis_name="core", subcore_axis_name="subcore")
    assert n % (SC_WINDOW * mesh.num_cores * mesh.num_subcores) == 0

    @functools.partial(pl.kernel, out_type=jax.ShapeDtypeStruct((n, d), x.dtype), mesh=mesh)
    def gather(x_hbm, i_hbm, o_hbm):
        def body(i_vmem, o_vmem):
            pltpu.sync_copy(x_hbm.at[i_vmem.at[0]], o_vmem)

        pltpu.emit_pipeline(
            body, grid=(n // SC_WINDOW,),
            in_specs=[pl.BlockSpec((1, SC_WINDOW), index_map=lambda i: (0, i))],
            out_specs=[pl.BlockSpec((SC_WINDOW, d), index_map=lambda i: (i, 0))],
            core_axis_name=('core', 'subcore'), dimension_semantics=(pltpu.PARALLEL,))(i_hbm, o_hbm)

    return gather(x, idx.reshape(1, n))


def _experts_sorted_kernel(tg_ref, tv_ref, *refs):
    xs_refs, (gs_ref, wg_ref, wu_ref, wd_ref), ys_refs = (
        refs[:N_XCHUNKS], refs[N_XCHUNKS:N_XCHUNKS + 4], refs[-N_CHUNKS:])
    j = pl.program_id(0)

    @pl.when(tv_ref[j] <= 0)
    def _():
        for ref in ys_refs:
            ref[...] = jnp.zeros(ref.shape, F32)

    @pl.when(tv_ref[j] > 0)
    def _():
        g = tg_ref[j]
        halves = []
        for r in xs_refs:
            w = r[...]
            halves.append(pltpu.bitcast(jnp.bitwise_and(w, jnp.uint32(0xFFFF0000)), F32))
            halves.append(pltpu.bitcast(jnp.left_shift(w, jnp.uint32(16)), F32))
        x = jnp.concatenate(halves, axis=1).astype(BF16)
        gates = gs_ref[...]
        acts = []
        for k in range(EXPERTS_PER_GROUP):
            hg = _dot(x, wg_ref[k])
            gate = jnp.sum(jnp.where(_lane_iota() == g * EXPERTS_PER_GROUP + k, gates, 0.0),
                           axis=-1, keepdims=True)
            acts.append(((hg * _sigmoid(hg)) * _dot(x, wu_ref[k]) * gate).astype(BF16))
        y = _dot(jnp.concatenate(acts, axis=1), wd_ref[...])
        for c, ref in enumerate(ys_refs):
            ref[...] = y[:, c * SC_ROW_WORDS:(c + 1) * SC_ROW_WORDS]


def _experts_sorted(xs_chunks, gs, tile_group, tile_valid, lw):
    n_rows = gs.shape[0]
    row = lambda w: pl.BlockSpec((SORT_TILE, w), lambda j, tg, tv: (j, 0))
    grp = lambda a: pl.BlockSpec((EXPERTS_PER_GROUP,) + a.shape[1:], lambda j, tg, tv: (tg[j], 0, 0))
    d_group = lw['w_ed'].shape[0] // N_GROUPS
    return pl.pallas_call(
        _experts_sorted_kernel,
        grid_spec=pltpu.PrefetchScalarGridSpec(
            num_scalar_prefetch=2, grid=(n_rows // SORT_TILE,),
            in_specs=[row(SC_ROW_WORDS)] * N_XCHUNKS + [
                row(LANES), grp(lw['w_eg']), grp(lw['w_eu']),
                pl.BlockSpec((d_group, D_MODEL), lambda j, tg, tv: (tg[j], 0))],
            out_specs=[row(SC_ROW_WORDS)] * N_CHUNKS),
        out_shape=[jax.ShapeDtypeStruct((n_rows, SC_ROW_WORDS), F32)] * N_CHUNKS,
        compiler_params=_cparams("arbitrary"), name="moe_sorted")(
            tile_group, tile_valid, *xs_chunks, gs, lw['w_eg'], lw['w_eu'], lw['w_ed'])


def _ple_kernel(h1_ref, *refs):
    y_refs, (p_ref, wple_ref, nple_ref, wpg_ref, out_ref) = refs[:N_CHUNKS], refs[N_CHUNKS:]
    h2 = h1_ref[...] + jnp.concatenate([r[...] for r in y_refs], axis=1)
    gate_ple = _sigmoid(_dot(_rms(h2, nple_ref[...]).astype(BF16), wpg_ref[...]))
    out_ref[...] = h2 + _dot(p_ref[0].astype(BF16), wple_ref[...]) * gate_ple


def _ple(h1, y_chunks, p_all, layer, lw, tm):
    t = h1.shape[0]
    row = lambda w: pl.BlockSpec((tm, w), lambda i: (i, 0))
    full = lambda a: pl.BlockSpec(a.shape, lambda i: (0,) * a.ndim)
    return pl.pallas_call(
        _ple_kernel, grid=(t // tm,),
        in_specs=[row(D_MODEL)] + [row(SC_ROW_WORDS)] * N_CHUNKS
        + [pl.BlockSpec((1, tm, p_all.shape[2]), lambda i: (layer, i, 0)),
           full(lw['w_ple']), full(lw['norm_ple']), full(lw['w_pg'])],
        out_specs=row(D_MODEL), out_shape=jax.ShapeDtypeStruct((t, D_MODEL), F32),
        compiler_params=_cparams("parallel"), name="ple")(
            h1, *y_chunks, p_all, lw['w_ple'], lw['norm_ple'], lw['w_pg'])


def _moe_dispatch(x_chunks, gates):
    t = gates.shape[0]
    n_rows = t + N_GROUPS * SORT_TILE
    dest_rep, total = _plan(gates)
    dest = dest_rep[:, 0].astype(jnp.int32)
    padded = (total[0, N_EXPERTS:N_EXPERTS + N_GROUPS].astype(jnp.int32) + (SORT_TILE - 1)) // SORT_TILE * SORT_TILE
    ends = jnp.cumsum(padded)
    tile_start = jnp.arange(n_rows // SORT_TILE, dtype=jnp.int32) * SORT_TILE
    tile_group = jnp.minimum(jnp.sum(tile_start[:, None] >= ends[None, :], axis=1), N_GROUPS - 1).astype(jnp.int32)
    tile_valid = (tile_start < ends[-1]).astype(jnp.int32)
    xs_chunks = [_sc_scatter_rows(xc, dest, n_rows) for xc in x_chunks]
    gs = _sc_scatter_rows(gates, dest, n_rows)
    return xs_chunks, gs, tile_group, tile_valid, dest


def _moe_combine(dispatched, h1, p_all, layer, lw):
    xs_chunks, gs, tile_group, tile_valid, dest = dispatched
    ys_chunks = _experts_sorted(xs_chunks, gs, tile_group, tile_valid, lw)
    y_chunks = [_sc_gather_rows(yc, dest) for yc in ys_chunks]
    return _ple(h1, y_chunks, p_all, layer, lw, _token_tile(h1.shape[0], 1024))


def _regroup_out_rows(wt):
    z = jnp.zeros((HEAD_DIM, wt.shape[1]), wt.dtype)
    return jnp.concatenate([wt[:W_A], wt[W_A:W_A + W_B], z, wt[W_A + W_B:], z], axis=0)


def _regroup_rows(w):
    z = jnp.zeros((HEAD_DIM, w.shape[1]), w.dtype)
    return jnp.stack([w[:W_A], jnp.concatenate([w[W_A:W_A + W_B], z], axis=0),
                      jnp.concatenate([w[W_A + W_B:], z], axis=0)])


def _pad_lanes(a, n=LANES):
    return jnp.pad(a, [(0, 0)] * (a.ndim - 1) + [(0, n - a.shape[-1])])


def _toeplitz_bias(rel_bias, off, tq, win):
    length = tq + win
    pad = length + abs(off)
    ext = jnp.flip(jnp.pad(rel_bias, ((0, 0), (pad, pad)), mode='edge'), axis=1)
    s1 = ext.shape[1] - 1 - (off + REL_CLIP + pad)
    v = jnp.concatenate([ext[:, s1:s1 + win], ext[:, s1 - tq:s1]], axis=1)
    flat = jnp.tile(v, (1, tq))[:, :tq * (length - 1)]
    return flat.reshape(-1, tq, length - 1)[:, :, :win]


def _block_toeplitz_bias(rel_bias, off, tq, win):
    nq, nk = tq // LANES, win // LANES
    blocks = {d: _toeplitz_bias(rel_bias, off - d * LANES, LANES, LANES) for d in range(-(nq - 1), nk)}
    return jnp.concatenate(
        [jnp.concatenate([blocks[b - a] for b in range(nk)], axis=2) for a in range(nq)], axis=1)


def _rel_tables(rel_bias, t_new, rows_b):
    tq, win = ATT_TILE, ATT_TILE + BAND
    q = jnp.arange(tq)[:, None]
    tabs = []
    for var in range(BAND // tq + 1):
        rel_k = jnp.arange(win)[None, :] - var * tq
        kch = jnp.floor_divide(rel_k, CHUNK)
        qch = q // CHUNK
        valid = (kch <= qch) & (kch >= qch - BAND_CHUNKS)
        tabs.append(jnp.where(valid[None], _block_toeplitz_bias(rel_bias, var * tq, tq, win), NEG_INF))
    tab_prompt = jnp.stack(tabs).astype(F32)
    tab_c = _toeplitz_bias(rel_bias, rows_b, t_new, rows_b).astype(F32)
    tab_n = _toeplitz_bias(rel_bias, 0, t_new, t_new).astype(F32)
    return tab_prompt, tab_c, tab_n


def _layer_weights(i, norm_mix, w_in, b_f, g_qa, g_ka, g_qb, g_kb, g_mix_out, w_out, norm_ffn,
                   w_rg, b_rg, w_re, b_re, w_eg, w_eu, w_ed, w_ple, norm_ple, w_pg):
    d_mix = W_A + W_B + W_C
    wt = jnp.transpose(w_in, (2, 0, 1))[:, i, :]
    w_f = wt[3 * d_mix:]
    w_big = jnp.concatenate(
        [_regroup_out_rows(wt[j * d_mix:(j + 1) * d_mix]) for j in range(3)]
        + [jnp.pad(w_f, ((0, LANES - w_f.shape[0]), (0, 0)))], axis=0).astype(BF16)
    tile6 = lambda g: jnp.tile(g, GROUP_W // HEAD_DIM)
    w_r = _pad_lanes(jnp.concatenate([w_re[i], w_rg[i]], axis=1))
    w_r_hi = w_r.astype(BF16)
    gm = g_mix_out[i]
    zpad = jnp.zeros((HEAD_DIM,), F32)
    g_mix = jnp.stack([gm[:W_A], jnp.concatenate([gm[W_A:W_A + W_B], zpad]),
                       jnp.concatenate([gm[W_A + W_B:], zpad])])[:, None, :]
    return dict(
        norm_mix=norm_mix[i][None], w_in=w_big,
        gq=jnp.concatenate([tile6(g_qa[i]), tile6(g_qb[i])])[None],
        gk=jnp.concatenate([tile6(g_ka[i]), tile6(g_kb[i])])[None],
        b_f=_pad_lanes(b_f[i])[None],
        g_mix=g_mix, w_out=_regroup_rows(w_out[i]).astype(BF16), norm_ffn=norm_ffn[i][None],
        w_r_hi=w_r_hi, w_r_lo=(w_r - w_r_hi.astype(F32)).astype(BF16),
        b_r=_pad_lanes(jnp.concatenate([b_re[i], b_rg[i]]))[None],
        w_eg=w_eg[i].astype(BF16), w_eu=w_eu[i].astype(BF16),
        w_ed=w_ed[i].reshape(-1, w_ed.shape[-1]).astype(BF16),
        w_ple=w_ple[i].astype(BF16), norm_ple=norm_ple[i][None], w_pg=w_pg[i].astype(BF16))


def _token_tile(t, pref):
    return pref if t % pref == 0 else t


def _feature_major(cache):
    d, n, p, h, e = cache.shape
    return jnp.transpose(cache, (0, 1, 3, 4, 2)).reshape(d, n, h * e, p)


def _position_major(rows, heads):
    d, n, _, p = rows.shape
    return jnp.transpose(rows.reshape(d, n, heads, HEAD_DIM, p), (0, 1, 4, 2, 3))


def kernel(x_prompt, x_sample, p_prompt, p_sample, cache_a_k, cache_a_v, cache_a_logf, cache_b_k, cache_b_v, cache_c_k, cache_c_v, norm_mix, w_in, b_f, g_qa, g_ka, g_qb, g_kb, rel_bias, g_mix_out, w_out, norm_ffn, w_router_group, b_router_group, w_router_expert, b_router_expert, w_exp_gate, w_exp_up, w_exp_down, w_ple, norm_ple, w_ple_gate):
    nb, s, d = x_prompt.shape
    ns, t_new, _ = x_sample.shape
    depth = w_in.shape[0]
    past = cache_a_k.shape[2]
    rows_b = cache_b_k.shape[2]
    assert d == D_MODEL and s % PROJ_TILE == 0 and s >= BAND + ATT_TILE and past % ATT_TILE == 0
    tp, ts = nb * s, ns * t_new

    hp = x_prompt.reshape(tp, d)
    hs = x_sample.reshape(ts, d)
    pp = p_prompt.reshape(depth, tp, -1)
    ps = p_sample.reshape(depth, ts, -1)
    caches = tuple(_feature_major(c) for c in (cache_a_k, cache_a_v, cache_b_k, cache_b_v, cache_c_k, cache_c_v))
    lf_rows = jnp.transpose(cache_a_logf, (0, 3, 1, 2))
    rsum = _suffix_sum(lf_rows.reshape(depth * H_A * ns, past)).reshape(depth, H_A, ns, past)
    rsum = jnp.transpose(rsum, (0, 2, 1, 3))

    rows_p = None
    new_s = [[] for _ in range(7)]
    for i in range(depth):
        lw = _layer_weights(i, norm_mix, w_in, b_f, g_qa, g_ka, g_qb, g_kb, g_mix_out, w_out, norm_ffn,
                            w_router_group, b_router_group, w_router_expert, b_router_expert,
                            w_exp_gate, w_exp_up, w_exp_down, w_ple, norm_ple, w_ple_gate)
        tab_p, tab_c, tab_n = _rel_tables(rel_bias[i] * LOG2E, t_new, rows_b)

        pr = _inproj_prompt(hp, lw, i, depth, nb, s, rows_p)
        qat, ka, vat, qb, kb, vb, qc, kc, vc, logf = pr[:10]
        rows_p = pr[10:]
        kaug, qaug = _forget_aug(logf, nb, s)
        oa = _attn_a(qat, ka, vat, kaug, qaug, nb, s)
        ob = _prompt_attention(
            _attn_b_kernel, "attn_b", qb, kb, vb, (tab_p,),
            [pl.BlockSpec(tab_p.shape, lambda b, j: (0, 0, 0, 0))], [], nb, s, ATT_TILE)
        oc = _prompt_attention(
            _attn_c_kernel, "attn_c", qc, kc, vc, (), [],
            [pltpu.VMEM((H_C, C_QTILE, PAIR_W), BF16), pltpu.VMEM((H_C, C_QTILE, LANES), F32),
             pltpu.VMEM((H_C, C_QTILE, PAIR_W), F32)], nb, s, C_QTILE)
        h1, gates, *x_chunks = _post(oa, ob, oc, hp, lw, _token_tile(tp, 512), N_XCHUNKS)
        dispatched = _moe_dispatch(x_chunks, gates)

        sr = _inproj_sample(hs, lw)
        oa, ob, oc = _sample_attention(sr, caches, i, rsum, tab_c, tab_n, ns, t_new)
        h1s, gates, xn = _post(oa, ob, oc, hs, lw, _token_tile(ts, 512))
        hs = _moe(xn, gates, h1s, ps, i, lw, _token_tile(ts, 1024))

        hp = _moe_combine(dispatched, h1, pp, i, lw)
        akf, avf, bkf, bvf, ckf, cvf, logf = sr[9:]
        rows = (akf.reshape(ns, t_new, H_A, HEAD_DIM), avf.reshape(ns, t_new, H_A, HEAD_DIM),
                logf[:, :H_A].reshape(ns, t_new, H_A),
                bkf.reshape(ns, t_new, H_B, HEAD_DIM), bvf.reshape(ns, t_new, H_B, HEAD_DIM),
                ckf.reshape(ns, t_new, H_C, HEAD_DIM), cvf.reshape(ns, t_new, H_C, HEAD_DIM))
        for j in range(7):
            new_s[j].append(rows[j])

    akt, avt, bkt, bvt, ckt, cvt, lft = rows_p
    outs_p = (_position_major(akt, H_A), _position_major(avt, H_A),
              jnp.transpose(lft.reshape(depth, H_A, nb, s), (0, 2, 3, 1)),
              _position_major(bkt, H_B), _position_major(bvt, H_B),
              _position_major(ckt, H_C), _position_major(cvt, H_C))
    outs_s = [jnp.stack(r, axis=0) for r in new_s]
    return (hp.reshape(nb, s, d), hs.reshape(ns, t_new, d), *outs_p, *outs_s)
```

```python
import functools
import math

import jax
import jax.numpy as jnp
from jax import lax
from jax.experimental import pallas as pl
from jax.experimental.pallas import tpu as pltpu
from jax.experimental.pallas import tpu_sc as plsc

F32 = jnp.float32
BF16 = jnp.bfloat16

D_MODEL = 1024
HEAD_DIM = 64
H_A, H_B, H_C = 6, 5, 5
W_A, W_B, W_C = H_A * HEAD_DIM, H_B * HEAD_DIM, H_C * HEAD_DIM
GROUP_W = 384
PAIR_W = 2 * HEAD_DIM
N_PAIRS = GROUP_W // PAIR_W
CHUNK = 64
BAND_CHUNKS = 8
BAND = BAND_CHUNKS * CHUNK
REL_CLIP = 128
N_GROUPS = 4
EXPERTS_PER_GROUP = 4
N_EXPERTS = N_GROUPS * EXPERTS_PER_GROUP
EPS = 1e-6
ATTN_SCALE = HEAD_DIM ** -0.5
LOG2E = math.log2(math.e)
Q_SCALE = ATTN_SCALE * LOG2E
NEG_INF = -1e30
LANES = 128
SUBLANES = 8
VMEM_LIMIT = 48 * 1024 * 1024
EXPERTS_PER_STEP = 4
MOE_VMEM_LIMIT = 58 * 1024 * 1024

ATT_TILE = 256
A_QTILE = 512
C_QTILE = 256
PROJ_TILE = 512
PROJ_VMEM_LIMIT = 56 * 1024 * 1024
AUG_SLOTS = 8

_NT = (((1,), (1,)), ((), ()))


def _cparams(*sem):
    return pltpu.CompilerParams(dimension_semantics=sem, vmem_limit_bytes=VMEM_LIMIT)


def _dot(a, b):
    return jnp.dot(a, b, preferred_element_type=F32)


def _dot_nt(a, b):
    return lax.dot_general(a, b, _NT, preferred_element_type=F32)


def _split3(x):
    hi = x.astype(BF16)
    r1 = x - hi.astype(F32)
    mid = r1.astype(BF16)
    lo = (r1 - mid.astype(F32)).astype(BF16)
    return hi, mid, lo


def _split2(x):
    hi = x.astype(BF16)
    lo = (x - hi.astype(F32)).astype(BF16)
    return hi, lo


def _lane_iota(n=LANES):
    return lax.broadcasted_iota(jnp.int32, (1, n), 1)


def _half_mask(half):
    lane = _lane_iota()
    return (lane < HEAD_DIM) if half == 0 else (lane >= HEAD_DIM)


def _pair_slice(h):
    return slice((h // 2) * PAIR_W, (h // 2 + 1) * PAIR_W)


def _softplus2(z2):
    return jnp.maximum(z2, 0.0) + jnp.log2(1.0 + jnp.exp2(-jnp.abs(z2)))


def _sigmoid(z):
    return 1.0 / (1.0 + jnp.exp(-z))


def _rms(x, g):
    return x * lax.rsqrt(jnp.mean(x * x, axis=-1, keepdims=True) + EPS) * g


def _pair_rms(z, gain):
    first = _half_mask(0)
    outs = []
    for j in range(N_PAIRS):
        blk = z[:, j * PAIR_W:(j + 1) * PAIR_W]
        sq = blk * blk
        lo = jnp.sum(jnp.where(first, sq, 0.0), axis=-1, keepdims=True)
        hi = jnp.sum(jnp.where(first, 0.0, sq), axis=-1, keepdims=True)
        ms = jnp.where(first, lo, hi) * (1.0 / HEAD_DIM)
        outs.append(blk * lax.rsqrt(ms + EPS) * gain[:, j * PAIR_W:(j + 1) * PAIR_W])
    return jnp.concatenate(outs, axis=-1)


def _projector(x_ref, gn_ref, w_ref):
    xb = _rms(x_ref[...], gn_ref[...]).astype(BF16)
    return lambda g, width=GROUP_W: _dot_nt(xb, w_ref[g * GROUP_W:g * GROUP_W + width, :])


def _log_forget(zf, bf_ref):
    zf = zf + bf_ref[...]
    return jnp.minimum(zf, 0.0) - jnp.log(1.0 + jnp.exp(-jnp.abs(zf)))


def _inproj_sample_kernel(x_ref, gn_ref, w_ref, gq_ref, gk_ref, bf_ref,
                          qa_ref, qb_ref, qc_ref, ka_ref, kb_ref, kc_ref, va_ref, vb_ref, vc_ref,
                          akf_ref, avf_ref, bkf_ref, bvf_ref, ckf_ref, cvf_ref, logf_ref):
    proj = _projector(x_ref, gn_ref, w_ref)
    qa_ref[...] = (_pair_rms(proj(0), gq_ref[:, :GROUP_W]) * Q_SCALE).astype(BF16)
    qb_ref[...] = (_pair_rms(proj(1), gq_ref[:, GROUP_W:]) * Q_SCALE).astype(BF16)
    qc_ref[...] = (proj(2) * Q_SCALE).astype(BF16)
    for g, norm, bf_ref_, f_ref, width in (
            (3, gk_ref[:, :GROUP_W], ka_ref, akf_ref, W_A), (4, gk_ref[:, GROUP_W:], kb_ref, bkf_ref, W_B),
            (5, None, kc_ref, ckf_ref, W_C), (6, None, va_ref, avf_ref, W_A),
            (7, None, vb_ref, bvf_ref, W_B), (8, None, vc_ref, cvf_ref, W_C)):
        z = proj(g)
        if norm is not None:
            z = _pair_rms(z, norm)
        f_ref[...] = z[:, :width]
        bf_ref_[...] = z.astype(BF16)
    logf_ref[...] = _log_forget(proj(9, LANES), bf_ref)


def _inproj_sample(x, lw):
    t = x.shape[0]
    row = lambda w: pl.BlockSpec((t, w), lambda i: (0, 0))
    full = lambda a: pl.BlockSpec(a.shape, lambda i: (0,) * a.ndim)
    widths = [GROUP_W] * 9 + [W_A, W_A, W_B, W_B, W_C, W_C, LANES]
    out_shape = [jax.ShapeDtypeStruct((t, w), BF16 if j < 9 else F32) for j, w in enumerate(widths)]
    ins = (x, lw['norm_mix'], lw['w_in'], lw['gq'], lw['gk'], lw['b_f'])
    return pl.pallas_call(
        _inproj_sample_kernel, grid=(1,),
        in_specs=[row(D_MODEL)] + [full(a) for a in ins[1:]],
        out_specs=[row(w) for w in widths], out_shape=out_shape,
        compiler_params=_cparams("arbitrary"), name="inproj_sample")(*ins)


N_PROMPT_ROWS = 7


def _inproj_prompt_kernel(*refs, n_alias):
    x_ref, gn_ref, w_ref, gq_ref, gk_ref, bf_ref = refs[:6]
    (qat_ref, ka_ref, vat_ref, qb_ref, kb_ref, vb_ref, qc_ref, kc_ref, vc_ref, logf_ref,
     akt_ref, avt_ref, bkt_ref, bvt_ref, ckt_ref, cvt_ref, lft_ref) = refs[6 + n_alias:]
    proj = _projector(x_ref, gn_ref, w_ref)

    def put(ref, val):
        for l in range(ref.shape[0]):
            ref[l, 0] = val

    qat_ref[...] = jnp.transpose(_pair_rms(proj(0), gq_ref[:, :GROUP_W]) * Q_SCALE).astype(BF16)
    qb_ref[...] = (_pair_rms(proj(1), gq_ref[:, GROUP_W:]) * Q_SCALE).astype(BF16)
    qc_ref[...] = (proj(2) * Q_SCALE).astype(BF16)
    ka = _pair_rms(proj(3), gk_ref[:, :GROUP_W])
    ka_ref[...] = ka.astype(BF16)
    put(akt_ref, jnp.transpose(ka))
    kb = _pair_rms(proj(4), gk_ref[:, GROUP_W:])
    kb_ref[...] = kb.astype(BF16)
    kc = proj(5)
    kc_ref[...] = kc.astype(BF16)
    put(ckt_ref, jnp.transpose(kc)[:W_C])
    vat = jnp.transpose(proj(6))
    put(avt_ref, vat)
    for j in range(vat_ref.shape[1]):
        vat_ref[0, j] = vat[:, j * ATT_TILE:(j + 1) * ATT_TILE].astype(BF16)
    vb = proj(7)
    vb_ref[...] = vb.astype(BF16)
    vc = proj(8)
    vc_ref[...] = vc.astype(BF16)
    put(cvt_ref, jnp.transpose(vc)[:W_C])

    @pl.when(pl.program_id(1) == pl.num_programs(1) - 1)
    def _():
        keep = kb.shape[0] - BAND
        put(bkt_ref, jnp.transpose(kb)[:W_B, keep:])
        put(bvt_ref, jnp.transpose(vb)[:W_B, keep:])

    lf = _log_forget(proj(9, LANES), bf_ref)
    logf_ref[...] = lf
    for l in range(lft_ref.shape[0]):
        lft_ref[l] = jnp.transpose(lf)[:H_A]


def _inproj_prompt(x, lw, layer, depth, nb, s, prev_rows):
    tm = PROJ_TILE
    nt = s // tm
    t = nb * s
    per = tm // ATT_TILE
    span, slot = (depth, 0) if prev_rows is None else (1, layer)
    row = lambda w: pl.BlockSpec((tm, w), lambda b, i: (b * nt + i, 0))
    full = lambda a: pl.BlockSpec(a.shape, lambda b, i: (0,) * a.ndim, pipeline_mode=pl.Buffered(1))
    feat = lambda w: pl.BlockSpec((span, 1, w, tm), lambda b, i: (slot, b, 0, i))
    band = lambda w: pl.BlockSpec((span, 1, w, BAND), lambda b, i: (slot, b, 0, 0))
    tok = jax.ShapeDtypeStruct((t, GROUP_W), BF16)
    out_shape = [
        jax.ShapeDtypeStruct((GROUP_W, t), BF16), tok,
        jax.ShapeDtypeStruct((nb, s // ATT_TILE, GROUP_W, ATT_TILE), BF16),
        tok, tok, tok, tok, tok, tok, jax.ShapeDtypeStruct((t, LANES), F32),
        jax.ShapeDtypeStruct((depth, nb, W_A, s), F32), jax.ShapeDtypeStruct((depth, nb, W_A, s), F32),
        jax.ShapeDtypeStruct((depth, nb, W_B, BAND), F32), jax.ShapeDtypeStruct((depth, nb, W_B, BAND), F32),
        jax.ShapeDtypeStruct((depth, nb, W_C, s), F32), jax.ShapeDtypeStruct((depth, nb, W_C, s), F32),
        jax.ShapeDtypeStruct((depth, H_A, t), F32)]
    out_specs = [
        pl.BlockSpec((GROUP_W, tm), lambda b, i: (0, b * nt + i)), row(GROUP_W),
        pl.BlockSpec((1, per, GROUP_W, ATT_TILE), lambda b, i: (b, i, 0, 0)),
        row(GROUP_W), row(GROUP_W), row(GROUP_W), row(GROUP_W), row(GROUP_W), row(GROUP_W), row(LANES),
        feat(W_A), feat(W_A), band(W_B), band(W_B), feat(W_C), feat(W_C),
        pl.BlockSpec((span, H_A, tm), lambda b, i: (slot, 0, b * nt + i))]
    ins = [x, lw['norm_mix'], lw['w_in'], lw['gq'], lw['gk'], lw['b_f']]
    in_specs = [row(D_MODEL)] + [full(a) for a in ins[1:]]
    aliases = {}
    n_alias = 0
    if prev_rows is not None:
        n_alias = N_PROMPT_ROWS
        first_row_out = len(out_shape) - N_PROMPT_ROWS
        for j, a in enumerate(prev_rows):
            aliases[len(ins)] = first_row_out + j
            ins.append(a)
            in_specs.append(pl.BlockSpec(memory_space=pl.ANY))
    return pl.pallas_call(
        functools.partial(_inproj_prompt_kernel, n_alias=n_alias), grid=(nb, nt),
        in_specs=in_specs, out_specs=out_specs, out_shape=out_shape,
        input_output_aliases=aliases,
        compiler_params=pltpu.CompilerParams(dimension_semantics=("parallel", "arbitrary"),
                                             vmem_limit_bytes=PROJ_VMEM_LIMIT),
        name="inproj_prompt")(*ins)


def _forget_aug_kernel(logf_ref, kaug_ref, qaug_ref, *, nblk):
    tb = ATT_TILE
    r = lax.broadcasted_iota(jnp.int32, (tb, tb), 0)
    c = lax.broadcasted_iota(jnp.int32, (tb, tb), 1)
    tri = jnp.where(c <= r, 1.0, 0.0).astype(BF16)
    src = lax.broadcasted_iota(jnp.int32, (LANES, LANES), 0)
    dst = lax.broadcasted_iota(jnp.int32, (LANES, LANES), 1)

    def place(slot):
        return jnp.where((dst == AUG_SLOTS * src + slot) & (src < H_A), 1.0, 0.0).astype(BF16)

    lane = _lane_iota()
    slot = jnp.bitwise_and(lane, AUG_SLOTS - 1)
    used = lane < AUG_SLOTS * H_A
    key_const = jnp.where(used & (slot >= 3) & (slot < 6), 1.0, 0.0)
    qry_const = jnp.where(used & (slot < 3), -1.0, 0.0)
    carry = jnp.zeros((1, LANES), F32)
    for blk in range(nblk):
        rows = slice(blk * tb, (blk + 1) * tb)
        hi, mid, lo = _split3(logf_ref[rows, :])
        cs = (_dot(tri, hi) + _dot(tri, mid) + _dot(tri, lo)) + carry
        carry = cs[tb - 1:tb, :]
        hi, mid, lo = _split3(cs * LOG2E)
        kaug = _dot(hi, place(0)) + _dot(mid, place(1)) + _dot(lo, place(2)) + key_const
        kaug_ref[rows, :] = kaug.astype(BF16)
        qaug = _dot(hi, place(3)) + _dot(mid, place(4)) + _dot(lo, place(5)) + qry_const
        qaug_ref[:, rows] = jnp.transpose(qaug).astype(BF16)


def _forget_aug(logf, nb, s):
    return pl.pallas_call(
        functools.partial(_forget_aug_kernel, nblk=s // ATT_TILE), grid=(nb,),
        in_specs=[pl.BlockSpec((s, LANES), lambda b: (b, 0))],
        out_specs=[pl.BlockSpec((s, LANES), lambda b: (b, 0)), pl.BlockSpec((LANES, s), lambda b: (0, b))],
        out_shape=[jax.ShapeDtypeStruct((nb * s, LANES), BF16), jax.ShapeDtypeStruct((LANES, nb * s), BF16)],
        compiler_params=_cparams("parallel"), name="forget_aug")(logf)


def _suffix_sum_kernel(x_ref, o_ref):
    p = x_ref.shape[1]
    j = lax.broadcasted_iota(jnp.int32, (p, p), 0)
    s = lax.broadcasted_iota(jnp.int32, (p, p), 1)
    tri = jnp.where(j > s, 1.0, 0.0).astype(BF16)
    hi, mid, lo = _split3(x_ref[...])
    o_ref[...] = _dot(hi, tri) + _dot(mid, tri) + _dot(lo, tri)


def _suffix_sum(x):
    return pl.pallas_call(
        _suffix_sum_kernel, out_shape=jax.ShapeDtypeStruct(x.shape, F32),
        compiler_params=pltpu.CompilerParams(vmem_limit_bytes=VMEM_LIMIT), name="suffix_sum")(x)


def _attn_a_kernel(qt_ref, k_ref, vt_ref, kaug_ref, qaug_ref, o_ref, qp_ref, m_ref, l_ref, acc_ref):
    tq = qt_ref.shape[1]
    tk = ATT_TILE
    qi = pl.program_id(1)
    row = lax.broadcasted_iota(jnp.int32, (PAIR_W, 1), 0)
    aug = qaug_ref[...]
    for h in range(H_A):
        q2 = qt_ref[_pair_slice(h), :]
        half = (row < HEAD_DIM) if h % 2 == 0 else (row >= HEAD_DIM)
        qp_ref[h, :PAIR_W, :] = jnp.where(half, q2, jnp.zeros_like(q2))
        mine = (row >= AUG_SLOTS * h) & (row < AUG_SLOTS * (h + 1))
        qp_ref[h, PAIR_W:, :] = jnp.where(mine, aug, jnp.zeros_like(aug))
    m_ref[...] = jnp.full(m_ref.shape, NEG_INF, F32)
    l_ref[...] = jnp.zeros(l_ref.shape, F32)
    acc_ref[...] = jnp.zeros(acc_ref.shape, F32)

    def step(kt, key_offset):
        ks = pl.multiple_of(kt * tk, tk)
        ka = kaug_ref[pl.ds(ks, tk), :]
        tile_scores = [_dot(jnp.concatenate([k_ref[pl.ds(ks, tk), _pair_slice(h)], ka], axis=1), qp_ref[h])
                       for h in range(H_A)]
        probs, alphas = [], []
        for h in range(H_A):
            s = tile_scores[h]
            if key_offset is not None:
                key = lax.broadcasted_iota(jnp.int32, (tk, tq), 0)
                qry = lax.broadcasted_iota(jnp.int32, (tk, tq), 1)
                s = jnp.where(key + key_offset <= qry, s, NEG_INF)
            m_prev = m_ref[h]
            m_new = jnp.maximum(m_prev, jnp.max(s, axis=0, keepdims=True))
            alpha = jnp.exp2(m_prev - m_new)
            p = jnp.exp2(s - m_new)
            l_ref[h] = alpha * l_ref[h] + jnp.sum(p, axis=0, keepdims=True)
            m_ref[h] = m_new
            alphas.append(alpha)
            probs.append(p.astype(BF16))
        for h in range(H_A):
            acc_ref[h] = alphas[h] * acc_ref[h] + _dot(vt_ref[0, kt, _pair_slice(h), :], probs[h])

    def body(kt, carry):
        step(kt, None)
        return carry

    per = tq // tk
    lax.fori_loop(0, qi * per, body, 0)
    for j in range(per):
        step(qi * per + j, j * tk)
    for pair in range(N_PAIRS):
        h = 2 * pair
        out_t = jnp.where(row < HEAD_DIM, acc_ref[h] * (1.0 / l_ref[h]), acc_ref[h + 1] * (1.0 / l_ref[h + 1]))
        o_ref[:, pair * PAIR_W:(pair + 1) * PAIR_W] = jnp.transpose(out_t)


def _attn_a(qat, ka, vat, kaug, qaug, nb, s):
    tq = A_QTILE
    nq = s // tq
    return pl.pallas_call(
        _attn_a_kernel, grid=(nb, nq),
        in_specs=[pl.BlockSpec((GROUP_W, tq), lambda b, i: (0, b * nq + i)),
                  pl.BlockSpec((s, GROUP_W), lambda b, i: (b, 0)),
                  pl.BlockSpec((1, s // ATT_TILE, GROUP_W, ATT_TILE), lambda b, i: (b, 0, 0, 0)),
                  pl.BlockSpec((s, LANES), lambda b, i: (b, 0)),
                  pl.BlockSpec((LANES, tq), lambda b, i: (0, b * nq + i))],
        out_specs=pl.BlockSpec((tq, GROUP_W), lambda b, i: (b * nq + i, 0)),
        out_shape=jax.ShapeDtypeStruct((nb * s, GROUP_W), F32),
        scratch_shapes=[pltpu.VMEM((H_A, 2 * PAIR_W, tq), BF16), pltpu.VMEM((H_A, 1, tq), F32),
                        pltpu.VMEM((H_A, 1, tq), F32), pltpu.VMEM((H_A, PAIR_W, tq), F32)],
        compiler_params=_cparams("parallel", "arbitrary"), name="attn_a")(qat, ka, vat, kaug, qaug)


def _masked_q(q_ref, qm_ref, n_heads):
    for h in range(n_heads):
        q2 = q_ref[:, _pair_slice(h)]
        qm_ref[h] = jnp.where(_half_mask(h % 2), q2, jnp.zeros_like(q2))


def _attn_b_kernel(q_ref, k_ref, v_ref, tab_ref, o_ref):
    tq = ATT_TILE
    win = tq + BAND
    i = pl.program_id(1)
    var = jnp.minimum(i, BAND // tq)
    ws = pl.multiple_of(jnp.maximum(i * tq - BAND, 0), tq)
    heads = range(H_B)
    scores = []
    for h in heads:
        q2 = q_ref[:, _pair_slice(h)]
        qm = jnp.where(_half_mask(h % 2), q2, jnp.zeros_like(q2))
        scores.append(_dot_nt(qm, k_ref[pl.ds(ws, win), _pair_slice(h)]) + tab_ref[var, h])
    probs, norms = [], []
    for h in heads:
        p = jnp.exp2(scores[h] - jnp.max(scores[h], axis=-1, keepdims=True))
        norms.append(1.0 / jnp.sum(p, axis=-1, keepdims=True))
        probs.append(p.astype(BF16))
    res = [_dot(probs[h], v_ref[pl.ds(ws, win), _pair_slice(h)]) * norms[h] for h in heads]
    res.append(jnp.zeros((tq, PAIR_W), F32))
    first = _half_mask(0)
    for pair in range(N_PAIRS):
        o_ref[:, pair * PAIR_W:(pair + 1) * PAIR_W] = jnp.where(first, res[2 * pair], res[2 * pair + 1])


def _suffix_tri(n):
    j = lax.broadcasted_iota(jnp.int32, (2 * n, n), 0)
    s = lax.broadcasted_iota(jnp.int32, (2 * n, n), 1)
    return jnp.where(jnp.where(j >= n, j - n, j) >= s, 1.0, 0.0).astype(BF16)


def _stick_scores(z, tri, seen):
    sp = _softplus2(z)
    if seen is not None:
        sp = jnp.where(seen, sp, 0.0)
    hi, lo = _split2(sp)
    n = sp.shape[1]
    if n % LANES == 0:
        return _dot(jnp.concatenate([hi, lo], axis=1), tri)
    return _dot(hi, tri[:n]) + _dot(lo, tri[:n])


def _stick_weights(z, s_in, carry, seen):
    a = jnp.exp2(z - (s_in + carry))
    if seen is not None:
        a = jnp.where(seen, a, 0.0)
    return a.astype(BF16)


def _attn_c_kernel(q_ref, k_ref, v_ref, o_ref, qm_ref, carry_ref, acc_ref):
    tq = q_ref.shape[0]
    tk = ATT_TILE
    qi = pl.program_id(1)
    _masked_q(q_ref, qm_ref, H_C)
    carry_ref[...] = jnp.zeros(carry_ref.shape, F32)
    acc_ref[...] = jnp.zeros(acc_ref.shape, F32)
    heads = range(H_C)

    def step(kt, key_offset):
        ks = pl.multiple_of(kt * tk, tk)
        zs = [_dot_nt(qm_ref[h], k_ref[pl.ds(ks, tk), _pair_slice(h)]) for h in heads]
        tri = _suffix_tri(tk)
        seen = None
        if key_offset is not None:
            r = lax.broadcasted_iota(jnp.int32, (tq, tk), 0)
            c = lax.broadcasted_iota(jnp.int32, (tq, tk), 1)
            seen = c + key_offset < r
        sums = [_stick_scores(zs[h], tri, seen) for h in heads]
        weights = []
        for h in heads:
            carry = carry_ref[h][:, 0:1]
            weights.append(_stick_weights(zs[h], sums[h], carry, seen))
            carry_ref[h] = jnp.broadcast_to(carry + sums[h][:, 0:1], (tq, LANES))
        for h in heads:
            acc_ref[h] += _dot(weights[h], v_ref[pl.ds(ks, tk), _pair_slice(h)])

    per = tq // tk

    def body(j, c):
        step(qi * per - 1 - j, None)
        return c

    for j in reversed(range(per)):
        step(qi * per + j, j * tk)
    lax.fori_loop(0, qi * per, body, 0)
    first = _half_mask(0)
    for pair in range(N_PAIRS):
        h = 2 * pair
        second = acc_ref[h + 1] if h + 1 < H_C else jnp.zeros((tq, PAIR_W), F32)
        o_ref[:, pair * PAIR_W:(pair + 1) * PAIR_W] = jnp.where(first, acc_ref[h], second)


def _prompt_attention(kernel, name, q, k, v, extra, extra_specs, scratch, nb, s, tq):
    nq = s // tq
    qspec = pl.BlockSpec((tq, GROUP_W), lambda b, i: (b * nq + i, 0))
    kvspec = pl.BlockSpec((s, GROUP_W), lambda b, i: (b, 0))
    return pl.pallas_call(
        kernel, grid=(nb, nq),
        in_specs=[qspec, kvspec, kvspec] + extra_specs,
        out_specs=qspec, out_shape=jax.ShapeDtypeStruct((nb * s, GROUP_W), F32),
        scratch_shapes=scratch,
        compiler_params=_cparams("parallel", "arbitrary"), name=name)(q, k, v, *extra)


def _sample_heads(n_heads, q_ref, width):
    out = []
    for h in range(n_heads):
        lo = (h // 2) * PAIR_W
        span = slice(lo, min(lo + PAIR_W, width))
        q2 = q_ref[:, span]
        if span.stop - span.start == PAIR_W:
            q2 = jnp.where(_half_mask(h % 2), q2, jnp.zeros_like(q2))
        out.append((h, span, q2))
    return out


def _store_heads(o_ref, res, n_heads):
    for pair in range(N_PAIRS):
        lo = pair * PAIR_W
        h = 2 * pair
        if h + 1 < n_heads:
            o_ref[:, lo:lo + PAIR_W] = jnp.where(_half_mask(0), res[h], res[h + 1])
        else:
            o_ref[:, lo:lo + HEAD_DIM] = res[h]
            o_ref[:, lo + HEAD_DIM:lo + PAIR_W] = jnp.zeros_like(res[h])


def _joint_softmax_pv(s_c, s_n, vt_c, v_n):
    m = jnp.maximum(jnp.max(s_c, axis=-1, keepdims=True), jnp.max(s_n, axis=-1, keepdims=True))
    p_c = jnp.exp2(s_c - m)
    p_n = jnp.exp2(s_n - m)
    l = jnp.sum(p_c, axis=-1, keepdims=True) + jnp.sum(p_n, axis=-1, keepdims=True)
    return (_dot_nt(p_c.astype(BF16), vt_c) + _dot(p_n.astype(BF16), v_n)) / l


def _sample_attn_kernel(qa_ref, qb_ref, qc_ref, kan_ref, kbn_ref, kcn_ref, van_ref, vbn_ref, vcn_ref,
                        cak_ref, cav_ref, cbk_ref, cbv_ref, cck_ref, ccv_ref,
                        rsum_ref, logf_ref, tabc_ref, tabn_ref,
                        oa_ref, ob_ref, oc_ref):
    t = qa_ref.shape[0]
    past = cak_ref.shape[3]
    r = lax.broadcasted_iota(jnp.int32, (t, t), 0)
    c = lax.broadcasted_iota(jnp.int32, (t, t), 1)

    ltri = jnp.where(c <= r, 1.0, 0.0).astype(BF16)
    hi, mid, lo = _split3(logf_ref[...])
    pcol = (_dot(ltri, hi) + _dot(ltri, mid) + _dot(ltri, lo)) * LOG2E
    prow = jnp.transpose(jnp.concatenate([pcol, jnp.zeros((LANES - t, LANES), F32)], axis=0))
    heads = _sample_heads(H_A, qa_ref, W_A)
    kts = {sp.start: cak_ref[0, 0, sp, :].astype(BF16) for _, sp, _ in heads}
    vts = {sp.start: cav_ref[0, 0, sp, :].astype(BF16) for _, sp, _ in heads}
    s_c = [_dot(qm, kts[sp.start]) + (pcol[:, h:h + 1] + rsum_ref[0, 0, h:h + 1, :] * LOG2E)
           for h, sp, qm in heads]
    s_n = [jnp.where(c <= r, _dot_nt(qm, kan_ref[:, sp]) + (pcol[:, h:h + 1] - prow[h:h + 1, 0:t]), NEG_INF)
           for h, sp, qm in heads]
    _store_heads(oa_ref, [_joint_softmax_pv(s_c[h], s_n[h], vts[sp.start], van_ref[:, sp])
                          for h, sp, _ in heads], H_A)

    heads = _sample_heads(H_B, qb_ref, W_B)
    kts = {sp.start: cbk_ref[0, 0, sp, :].astype(BF16) for _, sp, _ in heads}
    vts = {sp.start: cbv_ref[0, 0, sp, :].astype(BF16) for _, sp, _ in heads}
    s_c = [_dot(qm, kts[sp.start]) + tabc_ref[h] for h, sp, qm in heads]
    s_n = [_dot_nt(qm, kbn_ref[:, sp]) + tabn_ref[h] for h, sp, qm in heads]
    _store_heads(ob_ref, [_joint_softmax_pv(s_c[h], s_n[h], vts[sp.start], vbn_ref[:, sp])
                          for h, sp, _ in heads], H_B)

    heads = _sample_heads(H_C, qc_ref, W_C)
    nblk = past // ATT_TILE
    tri_n = _suffix_tri(t)
    tri_c = _suffix_tri(ATT_TILE)
    seen = c < r
    kts = {sp.start: cck_ref[0, 0, sp, :].astype(BF16) for _, sp, _ in heads}
    vts = {sp.start: ccv_ref[0, 0, sp, :].astype(BF16) for _, sp, _ in heads}
    z_n = [_dot_nt(qm, kcn_ref[:, sp]) for _, sp, qm in heads]
    z_c = [_dot(qm, kts[sp.start]) for _, sp, qm in heads]
    sum_n = [_stick_scores(z, tri_n, seen) for z in z_n]
    sum_c = [[_stick_scores(z[:, b * ATT_TILE:(b + 1) * ATT_TILE], tri_c, None) for b in range(nblk)]
             for z in z_c]
    res = []
    for h, sp, _ in heads:
        acc = _dot(_stick_weights(z_n[h], sum_n[h], 0.0, seen), vcn_ref[:, sp])
        carry = sum_n[h][:, 0:1]
        for b in reversed(range(nblk)):
            cols = slice(b * ATT_TILE, (b + 1) * ATT_TILE)
            a = _stick_weights(z_c[h][:, cols], sum_c[h][b], carry, None)
            acc = acc + _dot_nt(a, vts[sp.start][:, cols])
            carry = carry + sum_c[h][b][:, 0:1]
        res.append(acc)
    _store_heads(oc_ref, res, H_C)


def _sample_attention(proj, caches, layer, rsum, tabc, tabn, nb, t):
    qa, qb, qc, ka, kb, kc, va, vb, vc = proj[:9]
    logf = proj[15]
    new = pl.BlockSpec((t, GROUP_W), lambda b: (b, 0))
    cache = lambda a: pl.BlockSpec((1, 1) + a.shape[2:], lambda b: (layer, b, 0, 0))
    full = lambda a: pl.BlockSpec(a.shape, lambda b: (0,) * a.ndim)
    return pl.pallas_call(
        _sample_attn_kernel, grid=(nb,),
        in_specs=[new] * 9 + [cache(a) for a in caches] + [
            cache(rsum), pl.BlockSpec((t, LANES), lambda b: (b, 0)), full(tabc), full(tabn)],
        out_specs=[new] * 3,
        out_shape=[jax.ShapeDtypeStruct((nb * t, GROUP_W), F32)] * 3,
        compiler_params=_cparams("parallel"), name="sample_attn")(
            qa, qb, qc, ka, kb, kc, va, vb, vc, *caches, rsum, logf, tabc, tabn)


def _masked_max(x, mask):
    return jnp.max(jnp.where(mask, x, -jnp.inf), axis=-1, keepdims=True)


def _first_lane(mask, lane):
    return jnp.min(jnp.where(mask, lane, float(LANES)), axis=-1, keepdims=True)


def _route(logits):
    lane = _lane_iota().astype(F32)
    is_g = (lane >= N_EXPERTS) & (lane < N_EXPERTS + N_GROUPS)
    gmax = _masked_max(logits, is_g)
    p_g = 1.0 / jnp.sum(jnp.where(is_g, jnp.exp(logits - gmax), 0.0), axis=-1, keepdims=True)
    g_sel = _first_lane(is_g & (logits == gmax), lane) - N_EXPERTS
    lo = g_sel * EXPERTS_PER_GROUP
    in_g = (lane >= lo) & (lane < lo + EXPERTS_PER_GROUP)
    l1 = _masked_max(logits, in_g)
    i1 = _first_lane(in_g & (logits == l1), lane)
    rest = in_g & (lane != i1)
    l2 = _masked_max(logits, rest)
    i2 = _first_lane(rest & (logits == l2), lane)
    e2 = jnp.exp(l2 - l1)
    w1 = p_g / (1.0 + e2)
    w2 = p_g * e2 / (1.0 + e2)
    gates = jnp.where(lane == i1, w1, jnp.where(lane == i2, w2, 0.0))
    return jnp.where(lane == g_sel + N_EXPERTS, 1.0, gates)


def _post_kernel(oa_ref, ob_ref, oc_ref, h_ref, gmix_ref, wout_ref, nffn_ref, wrh_ref, wrl_ref, br_ref,
                 h1_ref, gates_ref, *x_refs, packed):
    def gnorm(o, width):
        ms = jnp.sum(o * o, axis=-1, keepdims=True) * (1.0 / width)
        return o * lax.rsqrt(ms + EPS)

    h1 = h_ref[...]
    for g, (o_ref, width) in enumerate(((oa_ref, W_A), (ob_ref, W_B), (oc_ref, W_C))):
        merged = (gnorm(o_ref[...], width) * gmix_ref[g]).astype(BF16)
        h1 = h1 + _dot(merged, wout_ref[g])
    h1_ref[...] = h1
    xf = _rms(h1, nffn_ref[...])
    hi, lo = _split2(xf)
    if packed:
        bits = pltpu.bitcast(hi.astype(F32), jnp.uint32)
        for c, ref in enumerate(x_refs):
            a = bits[:, (2 * c) * SC_ROW_WORDS:(2 * c + 1) * SC_ROW_WORDS]
            b = bits[:, (2 * c + 1) * SC_ROW_WORDS:(2 * c + 2) * SC_ROW_WORDS]
            ref[...] = jnp.bitwise_or(a, jnp.right_shift(b, jnp.uint32(16)))
    else:
        x_refs[0][...] = hi
    logits = _dot(hi, wrh_ref[...]) + _dot(lo, wrh_ref[...]) + _dot(hi, wrl_ref[...]) + br_ref[...]
    gates_ref[...] = _route(logits)


def _post(oa, ob, oc, h, lw, tm, n_chunks=0):
    t = h.shape[0]
    row = lambda w: pl.BlockSpec((tm, w), lambda i: (i, 0))
    full = lambda a: pl.BlockSpec(a.shape, lambda i: (0,) * a.ndim)
    ws = (lw['g_mix'], lw['w_out'], lw['norm_ffn'], lw['w_r_hi'], lw['w_r_lo'], lw['b_r'])
    if n_chunks:
        x_specs = [row(SC_ROW_WORDS)] * n_chunks
        x_shapes = [jax.ShapeDtypeStruct((t, SC_ROW_WORDS), jnp.uint32)] * n_chunks
    else:
        x_specs, x_shapes = [row(D_MODEL)], [jax.ShapeDtypeStruct((t, D_MODEL), BF16)]
    return pl.pallas_call(
        functools.partial(_post_kernel, packed=bool(n_chunks)), grid=(t // tm,),
        in_specs=[row(GROUP_W)] * 3 + [row(D_MODEL)] + [full(a) for a in ws],
        out_specs=[row(D_MODEL), row(LANES)] + x_specs,
        out_shape=[jax.ShapeDtypeStruct((t, D_MODEL), F32), jax.ShapeDtypeStruct((t, LANES), F32)] + x_shapes,
        compiler_params=_cparams("parallel"), name="post")(oa, ob, oc, h, *ws)


def _moe_kernel(xn_ref, gates_ref, wg_ref, wu_ref, wd_ref, h1_ref, p_ref, wple_ref, nple_ref, wpg_ref,
                out_ref, acc_ref):
    step = pl.program_id(1)

    @pl.when(step == 0)
    def _():
        acc_ref[...] = jnp.zeros_like(acc_ref)

    x = xn_ref[...]
    gates = gates_ref[...]
    acts = []
    for k in range(EXPERTS_PER_STEP):
        e = step * EXPERTS_PER_STEP + k
        g = _dot(x, wg_ref[k])
        u = _dot(x, wu_ref[k])
        gate = jnp.sum(jnp.where(_lane_iota() == e, gates, 0.0), axis=-1, keepdims=True)
        acts.append(((g * _sigmoid(g)) * u * gate).astype(BF16))
    acc_ref[...] += _dot(jnp.concatenate(acts, axis=1), wd_ref[...])

    @pl.when(step == pl.num_programs(1) - 1)
    def _():
        h2 = h1_ref[...] + acc_ref[...]
        gate_ple = _sigmoid(_dot(_rms(h2, nple_ref[...]).astype(BF16), wpg_ref[...]))
        out_ref[...] = h2 + _dot(p_ref[0].astype(BF16), wple_ref[...]) * gate_ple


def _moe(xn, gates, h1, p_all, layer, lw, tm):
    t = xn.shape[0]
    row = lambda w: pl.BlockSpec((tm, w), lambda i, e: (i, 0))
    once = pl.Buffered(1)
    full = lambda a: pl.BlockSpec(a.shape, lambda i, e: (0,) * a.ndim, pipeline_mode=once)
    exp = lambda a: pl.BlockSpec((EXPERTS_PER_STEP,) + a.shape[1:], lambda i, e: (e, 0, 0))
    d_hidden = lw['w_ed'].shape[0] // N_EXPERTS
    return pl.pallas_call(
        _moe_kernel, grid=(t // tm, N_EXPERTS // EXPERTS_PER_STEP),
        in_specs=[row(D_MODEL), row(LANES), exp(lw['w_eg']), exp(lw['w_eu']),
                  pl.BlockSpec((EXPERTS_PER_STEP * d_hidden, D_MODEL), lambda i, e: (e, 0)),
                  pl.BlockSpec((tm, D_MODEL), lambda i, e: (i, 0), pipeline_mode=once),
                  pl.BlockSpec((1, tm, p_all.shape[2]), lambda i, e: (layer, i, 0), pipeline_mode=once),
                  full(lw['w_ple']), full(lw['norm_ple']), full(lw['w_pg'])],
        out_specs=row(D_MODEL), out_shape=jax.ShapeDtypeStruct((t, D_MODEL), F32),
        scratch_shapes=[pltpu.VMEM((tm, D_MODEL), F32)],
        compiler_params=pltpu.CompilerParams(dimension_semantics=("parallel", "arbitrary"),
                                             vmem_limit_bytes=MOE_VMEM_LIMIT), name="moe")(
            xn, gates, lw['w_eg'], lw['w_eu'], lw['w_ed'], h1, p_all, lw['w_ple'], lw['norm_ple'], lw['w_pg'])


SORT_TILE = 512
SC_WINDOW = 128
SC_ROW_WORDS = 256
N_CHUNKS = D_MODEL // SC_ROW_WORDS
N_XCHUNKS = N_CHUNKS // 2


def _plan_kernel(gates_ref, dest_ref, total_ref, *, tb):
    lane = _lane_iota()
    is_g = (lane >= N_EXPERTS) & (lane < N_EXPERTS + N_GROUPS)
    total = jnp.sum(jnp.where(is_g, gates_ref[...], 0.0), axis=0, keepdims=True)
    padded = jnp.floor((total + (SORT_TILE - 1)) * (1.0 / SORT_TILE)) * SORT_TILE
    base = jnp.zeros((1, LANES), F32)
    for g in range(1, N_GROUPS):
        before = jnp.sum(jnp.where(lane < N_EXPERTS + g, padded, 0.0), axis=-1, keepdims=True)
        base = jnp.where(lane == N_EXPERTS + g, before, base)
    r = lax.broadcasted_iota(jnp.int32, (tb, tb), 0)
    c = lax.broadcasted_iota(jnp.int32, (tb, tb), 1)
    ltri = jnp.where(c < r, 1.0, 0.0).astype(BF16)
    carry = jnp.zeros((1, LANES), F32)
    for blk in range(gates_ref.shape[0] // tb):
        rows = slice(blk * tb, (blk + 1) * tb)
        member = jnp.where(is_g, gates_ref[rows, :], 0.0)
        rank = _dot(ltri, member.astype(BF16)) + carry
        carry = carry + jnp.sum(member, axis=0, keepdims=True)
        dest = jnp.sum(member * (base + rank), axis=-1, keepdims=True)
        dest_ref[rows, :] = jnp.broadcast_to(dest, (tb, LANES))
    total_ref[...] = jnp.broadcast_to(total, total_ref.shape)


def _plan(gates):
    t = gates.shape[0]
    return pl.pallas_call(
        functools.partial(_plan_kernel, tb=SORT_TILE),
        out_shape=[jax.ShapeDtypeStruct((t, LANES), F32), jax.ShapeDtypeStruct((SUBLANES, LANES), F32)],
        compiler_params=pltpu.CompilerParams(vmem_limit_bytes=VMEM_LIMIT), name="moe_plan")(gates)


def _sc_scatter_rows(xs, idx, n_out):
    xs = list(xs)
    n = idx.shape[0]
    mesh = plsc.VectorSubcoreMesh(core_axis_name="core", subcore_axis_name="subcore")
    assert n % (SC_WINDOW * mesh.num_cores * mesh.num_subcores) == 0

    @functools.partial(pl.kernel, out_type=[jax.ShapeDtypeStruct((n_out, x.shape[1]), x.dtype) for x in xs],
                       mesh=mesh, scratch_types=[])
    def scatter(*refs):
        x_hbms, i_hbm, o_hbms = refs[:len(xs)], refs[len(xs)], refs[len(xs) + 1:]
        for x_hbm, o_hbm in zip(x_hbms, o_hbms):
            def body(x_vmem, i_vmem, o_hbm=o_hbm):
                pltpu.sync_copy(x_vmem, o_hbm.at[i_vmem.at[0]])

            pltpu.emit_pipeline(
                body, grid=(n // SC_WINDOW,),
                in_specs=[pl.BlockSpec((SC_WINDOW, x_hbm.shape[1]), index_map=lambda i: (i, 0)),
                          pl.BlockSpec((1, SC_WINDOW), index_map=lambda i: (0, i))],
                out_specs=[], core_axis_name=('core', 'subcore'),
                dimension_semantics=(pltpu.PARALLEL,))(x_hbm, i_hbm)

    return scatter(*xs, idx.reshape(1, n))


def _sc_gather_rows(xs, idx):
    xs = list(xs)
    n = idx.shape[0]
    mesh = plsc.VectorSubcoreMesh(core_axis_name="core", subcore_axis_name="subcore")
    assert n % (SC_WINDOW * mesh.num_cores * mesh.num_subcores) == 0

    @functools.partial(pl.kernel, out_type=[jax.ShapeDtypeStruct((n, x.shape[1]), x.dtype) for x in xs], mesh=mesh)
    def gather(*refs):
        x_hbms, i_hbm, o_hbms = refs[:len(xs)], refs[len(xs)], refs[len(xs) + 1:]
        for x_hbm, o_hbm in zip(x_hbms, o_hbms):
            def body(i_vmem, o_vmem, x_hbm=x_hbm):
                pltpu.sync_copy(x_hbm.at[i_vmem.at[0]], o_vmem)

            pltpu.emit_pipeline(
                body, grid=(n // SC_WINDOW,),
                in_specs=[pl.BlockSpec((1, SC_WINDOW), index_map=lambda i: (0, i))],
                out_specs=[pl.BlockSpec((SC_WINDOW, x_hbm.shape[1]), index_map=lambda i: (i, 0))],
                core_axis_name=('core', 'subcore'), dimension_semantics=(pltpu.PARALLEL,))(i_hbm, o_hbm)

    return gather(*xs, idx.reshape(1, n))


def _experts_sorted_kernel(tg_ref, tv_ref, *refs):
    xs_refs, (gs_ref, wg_ref, wu_ref, wd_ref), ys_refs = (
        refs[:N_XCHUNKS], refs[N_XCHUNKS:N_XCHUNKS + 4], refs[-N_CHUNKS:])
    j = pl.program_id(0)

    @pl.when(tv_ref[j] <= 0)
    def _():
        for ref in ys_refs:
            ref[...] = jnp.zeros(ref.shape, F32)

    @pl.when(tv_ref[j] > 0)
    def _():
        g = tg_ref[j]
        halves = []
        for r in xs_refs:
            w = r[...]
            halves.append(pltpu.bitcast(jnp.bitwise_and(w, jnp.uint32(0xFFFF0000)), F32))
            halves.append(pltpu.bitcast(jnp.left_shift(w, jnp.uint32(16)), F32))
        x = jnp.concatenate(halves, axis=1).astype(BF16)
        gates = gs_ref[...]
        acts = []
        for k in range(EXPERTS_PER_GROUP):
            hg = _dot(x, wg_ref[k])
            gate = jnp.sum(jnp.where(_lane_iota() == g * EXPERTS_PER_GROUP + k, gates, 0.0),
                           axis=-1, keepdims=True)
            acts.append(((hg * _sigmoid(hg)) * _dot(x, wu_ref[k]) * gate).astype(BF16))
        y = _dot(jnp.concatenate(acts, axis=1), wd_ref[...])
        for c, ref in enumerate(ys_refs):
            ref[...] = y[:, c * SC_ROW_WORDS:(c + 1) * SC_ROW_WORDS]


def _experts_sorted(xs_chunks, gs, tile_group, tile_valid, lw):
    n_rows = gs.shape[0]
    row = lambda w: pl.BlockSpec((SORT_TILE, w), lambda j, tg, tv: (j, 0))
    grp = lambda a: pl.BlockSpec((EXPERTS_PER_GROUP,) + a.shape[1:], lambda j, tg, tv: (tg[j], 0, 0))
    d_group = lw['w_ed'].shape[0] // N_GROUPS
    return pl.pallas_call(
        _experts_sorted_kernel,
        grid_spec=pltpu.PrefetchScalarGridSpec(
            num_scalar_prefetch=2, grid=(n_rows // SORT_TILE,),
            in_specs=[row(SC_ROW_WORDS)] * N_XCHUNKS + [
                row(LANES), grp(lw['w_eg']), grp(lw['w_eu']),
                pl.BlockSpec((d_group, D_MODEL), lambda j, tg, tv: (tg[j], 0))],
            out_specs=[row(SC_ROW_WORDS)] * N_CHUNKS),
        out_shape=[jax.ShapeDtypeStruct((n_rows, SC_ROW_WORDS), F32)] * N_CHUNKS,
        compiler_params=_cparams("arbitrary"), name="moe_sorted")(
            tile_group, tile_valid, *xs_chunks, gs, lw['w_eg'], lw['w_eu'], lw['w_ed'])


def _ple_kernel(h1_ref, *refs):
    y_refs, (p_ref, wple_ref, nple_ref, wpg_ref, out_ref) = refs[:N_CHUNKS], refs[N_CHUNKS:]
    h2 = h1_ref[...] + jnp.concatenate([r[...] for r in y_refs], axis=1)
    gate_ple = _sigmoid(_dot(_rms(h2, nple_ref[...]).astype(BF16), wpg_ref[...]))
    out_ref[...] = h2 + _dot(p_ref[0].astype(BF16), wple_ref[...]) * gate_ple


def _ple(h1, y_chunks, p_all, layer, lw, tm):
    t = h1.shape[0]
    row = lambda w: pl.BlockSpec((tm, w), lambda i: (i, 0))
    full = lambda a: pl.BlockSpec(a.shape, lambda i: (0,) * a.ndim)
    return pl.pallas_call(
        _ple_kernel, grid=(t // tm,),
        in_specs=[row(D_MODEL)] + [row(SC_ROW_WORDS)] * N_CHUNKS
        + [pl.BlockSpec((1, tm, p_all.shape[2]), lambda i: (layer, i, 0)),
           full(lw['w_ple']), full(lw['norm_ple']), full(lw['w_pg'])],
        out_specs=row(D_MODEL), out_shape=jax.ShapeDtypeStruct((t, D_MODEL), F32),
        compiler_params=_cparams("parallel"), name="ple")(
            h1, *y_chunks, p_all, lw['w_ple'], lw['norm_ple'], lw['w_pg'])


def _moe_dispatch(x_chunks, gates):
    t = gates.shape[0]
    n_rows = t + N_GROUPS * SORT_TILE
    dest_rep, total = _plan(gates)
    dest = dest_rep[:, 0].astype(jnp.int32)
    padded = (total[0, N_EXPERTS:N_EXPERTS + N_GROUPS].astype(jnp.int32) + (SORT_TILE - 1)) // SORT_TILE * SORT_TILE
    ends = jnp.cumsum(padded)
    tile_start = jnp.arange(n_rows // SORT_TILE, dtype=jnp.int32) * SORT_TILE
    tile_group = jnp.minimum(jnp.sum(tile_start[:, None] >= ends[None, :], axis=1), N_GROUPS - 1).astype(jnp.int32)
    tile_valid = (tile_start < ends[-1]).astype(jnp.int32)
    *xs_chunks, gs = _sc_scatter_rows([*x_chunks, gates], dest, n_rows)
    return xs_chunks, gs, tile_group, tile_valid, dest


def _moe_combine(dispatched, h1, p_all, layer, lw):
    xs_chunks, gs, tile_group, tile_valid, dest = dispatched
    ys_chunks = _experts_sorted(xs_chunks, gs, tile_group, tile_valid, lw)
    y_chunks = _sc_gather_rows(ys_chunks, dest)
    return _ple(h1, y_chunks, p_all, layer, lw, _token_tile(h1.shape[0], 1024))


def _regroup_out_rows(wt):
    z = jnp.zeros((HEAD_DIM, wt.shape[1]), wt.dtype)
    return jnp.concatenate([wt[:W_A], wt[W_A:W_A + W_B], z, wt[W_A + W_B:], z], axis=0)


def _regroup_rows(w):
    z = jnp.zeros((HEAD_DIM, w.shape[1]), w.dtype)
    return jnp.stack([w[:W_A], jnp.concatenate([w[W_A:W_A + W_B], z], axis=0),
                      jnp.concatenate([w[W_A + W_B:], z], axis=0)])


def _pad_lanes(a, n=LANES):
    return jnp.pad(a, [(0, 0)] * (a.ndim - 1) + [(0, n - a.shape[-1])])


def _toeplitz_bias(rel_bias, off, tq, win):
    length = tq + win
    pad = length + abs(off)
    ext = jnp.flip(jnp.pad(rel_bias, ((0, 0), (pad, pad)), mode='edge'), axis=1)
    s1 = ext.shape[1] - 1 - (off + REL_CLIP + pad)
    v = jnp.concatenate([ext[:, s1:s1 + win], ext[:, s1 - tq:s1]], axis=1)
    flat = jnp.tile(v, (1, tq))[:, :tq * (length - 1)]
    return flat.reshape(-1, tq, length - 1)[:, :, :win]


def _block_toeplitz_bias(rel_bias, off, tq, win):
    nq, nk = tq // LANES, win // LANES
    blocks = {d: _toeplitz_bias(rel_bias, off - d * LANES, LANES, LANES) for d in range(-(nq - 1), nk)}
    return jnp.concatenate(
        [jnp.concatenate([blocks[b - a] for b in range(nk)], axis=2) for a in range(nq)], axis=1)


def _rel_tables(rel_bias, t_new, rows_b):
    tq, win = ATT_TILE, ATT_TILE + BAND
    q = jnp.arange(tq)[:, None]
    tabs = []
    for var in range(BAND // tq + 1):
        rel_k = jnp.arange(win)[None, :] - var * tq
        kch = jnp.floor_divide(rel_k, CHUNK)
        qch = q // CHUNK
        valid = (kch <= qch) & (kch >= qch - BAND_CHUNKS)
        tabs.append(jnp.where(valid[None], _block_toeplitz_bias(rel_bias, var * tq, tq, win), NEG_INF))
    tab_prompt = jnp.stack(tabs).astype(F32)
    tab_c = _toeplitz_bias(rel_bias, rows_b, t_new, rows_b).astype(F32)
    tab_n = _toeplitz_bias(rel_bias, 0, t_new, t_new).astype(F32)
    return tab_prompt, tab_c, tab_n


def _layer_weights(i, norm_mix, w_in, b_f, g_qa, g_ka, g_qb, g_kb, g_mix_out, w_out, norm_ffn,
                   w_rg, b_rg, w_re, b_re, w_eg, w_eu, w_ed, w_ple, norm_ple, w_pg):
    d_mix = W_A + W_B + W_C
    wt = jnp.transpose(w_in, (2, 0, 1))[:, i, :]
    w_f = wt[3 * d_mix:]
    w_big = jnp.concatenate(
        [_regroup_out_rows(wt[j * d_mix:(j + 1) * d_mix]) for j in range(3)]
        + [jnp.pad(w_f, ((0, LANES - w_f.shape[0]), (0, 0)))], axis=0).astype(BF16)
    tile6 = lambda g: jnp.tile(g, GROUP_W // HEAD_DIM)
    w_r = _pad_lanes(jnp.concatenate([w_re[i], w_rg[i]], axis=1))
    w_r_hi = w_r.astype(BF16)
    gm = g_mix_out[i]
    zpad = jnp.zeros((HEAD_DIM,), F32)
    g_mix = jnp.stack([gm[:W_A], jnp.concatenate([gm[W_A:W_A + W_B], zpad]),
                       jnp.concatenate([gm[W_A + W_B:], zpad])])[:, None, :]
    return dict(
        norm_mix=norm_mix[i][None], w_in=w_big,
        gq=jnp.concatenate([tile6(g_qa[i]), tile6(g_qb[i])])[None],
        gk=jnp.concatenate([tile6(g_ka[i]), tile6(g_kb[i])])[None],
        b_f=_pad_lanes(b_f[i])[None],
        g_mix=g_mix, w_out=_regroup_rows(w_out[i]).astype(BF16), norm_ffn=norm_ffn[i][None],
        w_r_hi=w_r_hi, w_r_lo=(w_r - w_r_hi.astype(F32)).astype(BF16),
        b_r=_pad_lanes(jnp.concatenate([b_re[i], b_rg[i]]))[None],
        w_eg=w_eg[i].astype(BF16), w_eu=w_eu[i].astype(BF16),
        w_ed=w_ed[i].reshape(-1, w_ed.shape[-1]).astype(BF16),
        w_ple=w_ple[i].astype(BF16), norm_ple=norm_ple[i][None], w_pg=w_pg[i].astype(BF16))


def _token_tile(t, pref):
    return pref if t % pref == 0 else t


def _feature_major(cache):
    d, n, p, h, e = cache.shape
    return jnp.transpose(cache, (0, 1, 3, 4, 2)).reshape(d, n, h * e, p)


def _position_major(rows, heads):
    d, n, _, p = rows.shape
    return jnp.transpose(rows.reshape(d, n, heads, HEAD_DIM, p), (0, 1, 4, 2, 3))


def kernel(x_prompt, x_sample, p_prompt, p_sample, cache_a_k, cache_a_v, cache_a_logf, cache_b_k, cache_b_v, cache_c_k, cache_c_v, norm_mix, w_in, b_f, g_qa, g_ka, g_qb, g_kb, rel_bias, g_mix_out, w_out, norm_ffn, w_router_group, b_router_group, w_router_expert, b_router_expert, w_exp_gate, w_exp_up, w_exp_down, w_ple, norm_ple, w_ple_gate):
    nb, s, d = x_prompt.shape
    ns, t_new, _ = x_sample.shape
    depth = w_in.shape[0]
    past = cache_a_k.shape[2]
    rows_b = cache_b_k.shape[2]
    assert d == D_MODEL and s % PROJ_TILE == 0 and s >= BAND + ATT_TILE and past % ATT_TILE == 0
    tp, ts = nb * s, ns * t_new

    hp = x_prompt.reshape(tp, d)
    hs = x_sample.reshape(ts, d)
    pp = p_prompt.reshape(depth, tp, -1)
    ps = p_sample.reshape(depth, ts, -1)
    caches = tuple(_feature_major(c) for c in (cache_a_k, cache_a_v, cache_b_k, cache_b_v, cache_c_k, cache_c_v))
    lf_rows = jnp.transpose(cache_a_logf, (0, 3, 1, 2))
    rsum = _suffix_sum(lf_rows.reshape(depth * H_A * ns, past)).reshape(depth, H_A, ns, past)
    rsum = jnp.transpose(rsum, (0, 2, 1, 3))

    rows_p = None
    new_s = [[] for _ in range(7)]
    for i in range(depth):
        lw = _layer_weights(i, norm_mix, w_in, b_f, g_qa, g_ka, g_qb, g_kb, g_mix_out, w_out, norm_ffn,
                            w_router_group, b_router_group, w_router_expert, b_router_expert,
                            w_exp_gate, w_exp_up, w_exp_down, w_ple, norm_ple, w_ple_gate)
        tab_p, tab_c, tab_n = _rel_tables(rel_bias[i] * LOG2E, t_new, rows_b)

        pr = _inproj_prompt(hp, lw, i, depth, nb, s, rows_p)
        qat, ka, vat, qb, kb, vb, qc, kc, vc, logf = pr[:10]
        rows_p = pr[10:]
        kaug, qaug = _forget_aug(logf, nb, s)
        oa = _attn_a(qat, ka, vat, kaug, qaug, nb, s)
        ob = _prompt_attention(
            _attn_b_kernel, "attn_b", qb, kb, vb, (tab_p,),
            [pl.BlockSpec(tab_p.shape, lambda b, j: (0, 0, 0, 0))], [], nb, s, ATT_TILE)
        oc = _prompt_attention(
            _attn_c_kernel, "attn_c", qc, kc, vc, (), [],
            [pltpu.VMEM((H_C, C_QTILE, PAIR_W), BF16), pltpu.VMEM((H_C, C_QTILE, LANES), F32),
             pltpu.VMEM((H_C, C_QTILE, PAIR_W), F32)], nb, s, C_QTILE)
        h1, gates, *x_chunks = _post(oa, ob, oc, hp, lw, _token_tile(tp, 512), N_XCHUNKS)
        dispatched = _moe_dispatch(x_chunks, gates)

        sr = _inproj_sample(hs, lw)
        oa, ob, oc = _sample_attention(sr, caches, i, rsum, tab_c, tab_n, ns, t_new)
        h1s, gates, xn = _post(oa, ob, oc, hs, lw, _token_tile(ts, 512))
        hs = _moe(xn, gates, h1s, ps, i, lw, _token_tile(ts, 1024))

        hp = _moe_combine(dispatched, h1, pp, i, lw)
        akf, avf, bkf, bvf, ckf, cvf, logf = sr[9:]
        rows = (akf.reshape(ns, t_new, H_A, HEAD_DIM), avf.reshape(ns, t_new, H_A, HEAD_DIM),
                logf[:, :H_A].reshape(ns, t_new, H_A),
                bkf.reshape(ns, t_new, H_B, HEAD_DIM), bvf.reshape(ns, t_new, H_B, HEAD_DIM),
                ckf.reshape(ns, t_new, H_C, HEAD_DIM), cvf.reshape(ns, t_new, H_C, HEAD_DIM))
        for j in range(7):
            new_s[j].append(rows[j])

    akt, avt, bkt, bvt, ckt, cvt, lft = rows_p
    outs_p = (_position_major(akt, H_A), _position_major(avt, H_A),
              jnp.transpose(lft.reshape(depth, H_A, nb, s), (0, 2, 3, 1)),
              _position_major(bkt, H_B), _position_major(bvt, H_B),
              _position_major(ckt, H_C), _position_major(cvt, H_C))
    outs_s = [jnp.stack(r, axis=0) for r in new_s]
    return (hp.reshape(nb, s, d), hs.reshape(ns, t_new, d), *outs_p, *outs_s)
```
